```python
import math
import jax, jax.numpy as jnp
from jax import lax
import numpy as np

D_MODEL = 1024
BATCH = 4
SEQ = 8192
DEPTH = 2

GRID_W = 64
CTX_LEN = 256
N_EVEN = (DEPTH + 1) // 2
N_ODD = DEPTH // 2
ALPHA = (2.0 * DEPTH) ** 0.25
BETA = (8.0 * DEPTH) ** -0.25
LN_EPS = 1e-5
GN_EPS = 1e-6
ROPE_BASE = 10000.0
RET_DK = 64
RET_DV = 128
S5_CH = D_MODEL // 4
RET_HEADS = (D_MODEL - S5_CH) // RET_DV
RET_QK_W = RET_HEADS * RET_DK
RET_V_W = RET_HEADS * RET_DV
RET_CHUNK = 128
S5_GROUP_SIZE = 16
S5_GROUPS = S5_CH // S5_GROUP_SIZE
S5_STATE = 64
EVEN_SPLITS = (RET_QK_W, 2 * RET_QK_W, 2 * RET_QK_W + RET_V_W, 2 * RET_QK_W + 2 * RET_V_W)
EVEN_IN_W = 2 * RET_QK_W + 2 * RET_V_W + S5_CH
EVEN_OUT_W = RET_V_W + S5_CH
DIFF_HEADS = 8
DIFF_DH = D_MODEL // (2 * DIFF_HEADS)
Q_BLOCK = 128
N_EXPERTS = 16
N_EXPERT_GROUPS = 4
EXPERTS_PER_GROUP = N_EXPERTS // N_EXPERT_GROUPS
TOP_K = 2
D_FF = D_MODEL
MOE_BLOCK = 256

kernel_name = "hybrid_retention_s5_diffattn_moe_dit"


def _layer_norm(x, g):
    xf = x.astype(jnp.float32)
    mu = xf.mean(-1, keepdims=True)
    var = jnp.mean(jnp.square(xf - mu), -1, keepdims=True)
    return ((xf - mu) * lax.rsqrt(var + LN_EPS) * g.astype(jnp.float32)).astype(x.dtype)


def _modulate(x, shift, scale):
    return x * (1 + scale) + shift


def _rope_cos_sin(pos, dim):
    inv = ROPE_BASE ** (-jnp.arange(0, dim, 2, dtype=jnp.float32) / dim)
    ang = pos.astype(jnp.float32)[:, None] * inv[None, :]
    return jnp.cos(ang), jnp.sin(ang)


def _apply_rope(x, cos, sin):
    half = x.shape[-1] // 2
    x1, x2 = x[..., :half], x[..., half:]
    cos = cos.astype(x.dtype)
    sin = sin.astype(x.dtype)
    return jnp.concatenate([x1 * cos - x2 * sin, x1 * sin + x2 * cos], -1)


def _axial_rope(x, rows, cols):
    half = x.shape[-1] // 2
    cr, sr = _rope_cos_sin(rows, half)
    cc, sc = _rope_cos_sin(cols, half)
    return jnp.concatenate([_apply_rope(x[..., :half], cr, sr),
                            _apply_rope(x[..., half:], cc, sc)], -1)


def _retention_chunkwise(q, k, v, log_gamma, s0, strict):
    bsz, nh, L, dk = q.shape
    dv = v.shape[-1]
    nc = L // RET_CHUNK
    qc = q.reshape(bsz, nh, nc, RET_CHUNK, dk)
    kc = k.reshape(bsz, nh, nc, RET_CHUNK, dk)
    vc = v.reshape(bsz, nh, nc, RET_CHUNK, dv)
    idx = jnp.arange(RET_CHUNK, dtype=jnp.float32)
    diff = idx[:, None] - idx[None, :]
    mask = diff > 0 if strict else diff >= 0
    intra_decay = jnp.where(mask, jnp.exp(log_gamma[:, None, None] * jnp.maximum(diff, 0.0)), 0.0)
    scores = jnp.einsum('bhcid,bhcjd->bhcij', qc, kc) * intra_decay[None, :, None]
    o_intra = jnp.einsum('bhcij,bhcje->bhcie', scores, vc)
    k_decay = jnp.exp(log_gamma[:, None] * (RET_CHUNK - 1.0 - idx)[None])
    chunk_states = jnp.einsum('bhcjd,bhcje->cbhde', kc * k_decay[None, :, None, :, None], vc)
    gamma_chunk = jnp.exp(log_gamma * RET_CHUNK)[None, :, None, None]

    def step(s, r):
        return gamma_chunk * s + r, s

    s_final, s_prev = lax.scan(step, s0, chunk_states)
    q_decay = jnp.exp(log_gamma[:, None] * (idx + 1.0)[None])
    o_inter = jnp.einsum('bhcid,cbhde->bhcie', qc * q_decay[None, :, None, :, None], s_prev)
    return (o_intra + o_inter).reshape(bsz, nh, L, dv), s_final


def _bidir_retention(q_lat, k_lat, v_lat, q_ctx, k_ctx, v_ctx, log_gammas):
    bsz, nh, _, dk = q_lat.shape
    zeros = jnp.zeros((bsz, nh, dk, v_lat.shape[-1]), jnp.float32)
    flip = lambda t: jnp.flip(t, axis=2)
    oc_f, sc_f = _retention_chunkwise(q_ctx, k_ctx, v_ctx, log_gammas[0], zeros, False)
    ol_f, _ = _retention_chunkwise(q_lat, k_lat, v_lat, log_gammas[0], sc_f, False)
    oc_b, sc_b = _retention_chunkwise(flip(q_ctx), flip(k_ctx), flip(v_ctx), log_gammas[1], zeros, True)
    ol_b, _ = _retention_chunkwise(flip(q_lat), flip(k_lat), flip(v_lat), log_gammas[1], sc_b, True)
    return ol_f + flip(ol_b), oc_f + flip(oc_b)


def _head_group_norm(o):
    mu = o.mean(-1, keepdims=True)
    var = jnp.mean(jnp.square(o - mu), -1, keepdims=True)
    return (o - mu) * lax.rsqrt(var + GN_EPS)


def _ssm_combine(e_i, e_j):
    a_i, b_i = e_i
    a_j, b_j = e_j
    return a_j * a_i, a_j * b_i + b_j


def _s5_scan(u, lam_bar, b_bar, c_mat, h0):
    bu = jnp.einsum('blgp,gnp->blgn', u.astype(jnp.complex64), b_bar)
    if h0 is not None:
        bu = bu.at[:, 0].add(lam_bar[None] * h0)
    a = jnp.broadcast_to(lam_bar[None, None], (1,) + bu.shape[1:])
    _, h = lax.associative_scan(_ssm_combine, (a, bu), axis=1)
    y = jnp.einsum('blgn,gpn->blgp', h, c_mat).real
    return y, h[:, -1]


def _bidir_s5(u_lat, u_ctx, lam_re, lam_im, log_dt, b_re, b_im, c_re, c_im, d_skip):
    f32 = jnp.float32
    flip = lambda t: jnp.flip(t, axis=1)
    d = d_skip.astype(f32)
    y_lat = d * u_lat
    y_ctx = d * u_ctx
    for direction in range(2):
        lam = lax.complex(lam_re[direction].astype(f32), lam_im[direction].astype(f32))
        dt = jnp.exp(log_dt[direction].astype(f32))[:, None]
        lam_bar = jnp.exp(lam * dt)
        b_bar = ((lam_bar - 1.0) / lam)[..., None] * lax.complex(b_re[direction].astype(f32), b_im[direction].astype(f32))
        c_mat = lax.complex(c_re[direction].astype(f32), c_im[direction].astype(f32))
        uc, ul = (u_ctx, u_lat) if direction == 0 else (flip(u_ctx), flip(u_lat))
        yc, hc = _s5_scan(uc, lam_bar, b_bar, c_mat, None)
        yl, _ = _s5_scan(ul, lam_bar, b_bar, c_mat, hc)
        if direction == 1:
            yc, yl = flip(yc), flip(yl)
        y_lat = y_lat + yl
        y_ctx = y_ctx + yc
    return y_lat, y_ctx


def _even_mixer(h_lat, h_ctx, w_in, decay_logit, lam_re, lam_im, log_dt, b_re, b_im,
                c_re, c_im, d_skip, w_glu, w_out):
    dtype = h_lat.dtype
    L = h_lat.shape[1]
    cos, sin = _rope_cos_sin(jnp.arange(L), RET_DK)

    def heads(t, dh):
        return t.reshape(t.shape[0], t.shape[1], RET_HEADS, dh).transpose(0, 2, 1, 3).astype(jnp.float32)

    def project(h):
        q, k, v, g, u = jnp.split(h @ w_in, EVEN_SPLITS, axis=-1)
        u = u.reshape(u.shape[0], u.shape[1], S5_GROUPS, S5_GROUP_SIZE).astype(jnp.float32)
        return heads(q, RET_DK), heads(k, RET_DK) * RET_DK ** -0.5, heads(v, RET_DV), g, u

    q_l, k_l, v_l, g_l, u_l = project(h_lat)
    q_l = _apply_rope(q_l, cos, sin)
    k_l = _apply_rope(k_l, cos, sin)
    q_c, k_c, v_c, g_c, u_c = project(h_ctx)
    log_gammas = jax.nn.log_sigmoid(decay_logit.astype(jnp.float32))
    r_lat, r_ctx = _bidir_retention(q_l, k_l, v_l, q_c, k_c, v_c, log_gammas)
    s_lat, s_ctx = _bidir_s5(u_l, u_c, lam_re, lam_im, log_dt, b_re, b_im, c_re, c_im, d_skip)

    def merge(r, g, s):
        bsz, n = g.shape[:2]
        r = _head_group_norm(r).transpose(0, 2, 1, 3).reshape(bsz, n, RET_V_W).astype(dtype) * jax.nn.silu(g)
        z = jax.nn.gelu(s.reshape(bsz, n, S5_CH)).astype(dtype) @ w_glu
        za, zb = jnp.split(z, 2, axis=-1)
        return jnp.concatenate([r, za * jax.nn.sigmoid(zb)], -1) @ w_out

    return merge(r_lat, g_l, s_lat), merge(r_ctx, g_c, s_ctx)


def _diff_attend(q1, q2, k1, k2, v, lam):
    s1 = jnp.einsum('bhqd,bhkd->bhqk', q1, k1).astype(jnp.float32)
    s2 = jnp.einsum('bhqd,bhkd->bhqk', q2, k2).astype(jnp.float32)
    a = jax.nn.softmax(s1, axis=-1) - lam * jax.nn.softmax(s2, axis=-1)
    return jnp.einsum('bhqk,bhkd->bhqd', a.astype(v.dtype), v)


def _odd_mixer(h_lat, h_ctx, w_in, diff_lambda, subln_g, w_out, lambda_init, need_ctx):
    dtype = h_lat.dtype
    bsz, L, _ = h_lat.shape
    n_rows = L // GRID_W
    rows = jnp.repeat(jnp.arange(n_rows), GRID_W)
    cols = jnp.arange(n_rows * GRID_W) % GRID_W
    scale = DIFF_DH ** -0.5

    def qk_heads(t):
        return t.reshape(t.shape[0], t.shape[1], DIFF_HEADS, 2, DIFF_DH).transpose(3, 0, 2, 1, 4)

    def v_heads(t):
        return t.reshape(t.shape[0], t.shape[1], DIFF_HEADS, 2 * DIFF_DH).transpose(0, 2, 1, 3)

    def head_out(o):
        of = o.astype(jnp.float32)
        of = of * lax.rsqrt(jnp.mean(jnp.square(of), -1, keepdims=True) + GN_EPS)
        of = of * subln_g.astype(jnp.float32) * (1.0 - lambda_init)
        n = o.shape[2]
        return of.transpose(0, 2, 1, 3).reshape(o.shape[0], n, D_MODEL).astype(dtype) @ w_out

    q, k, v = jnp.split(h_lat @ w_in, 3, axis=-1)
    q_l = _axial_rope(qk_heads(q), rows, cols) * scale
    k_l = _axial_rope(qk_heads(k), rows, cols)
    k_cx, v_cx = jnp.split(h_ctx @ w_in[:, D_MODEL:], 2, axis=-1)
    k_c = qk_heads(k_cx)
    v_c = v_heads(v_cx)
    k_all = jnp.concatenate([k_l, k_c], axis=3)
    v_all = jnp.concatenate([v_heads(v), v_c], axis=2)
    lf = diff_lambda.astype(jnp.float32)
    lam = jnp.exp(jnp.sum(lf[0] * lf[1])) - jnp.exp(jnp.sum(lf[2] * lf[3])) + lambda_init
    nb = L // Q_BLOCK
    q_blocks = q_l.reshape(2, bsz, DIFF_HEADS, nb, Q_BLOCK, DIFF_DH).transpose(3, 0, 1, 2, 4, 5)
    o = lax.map(lambda qb: _diff_attend(qb[0], qb[1], k_all[0], k_all[1], v_all, lam), q_blocks)
    o = o.transpose(1, 2, 0, 3, 4).reshape(bsz, DIFF_HEADS, L, 2 * DIFF_DH)
    o_lat = head_out(o)
    o_ctx = None
    if need_ctx:
        q_c = qk_heads(h_ctx @ w_in[:, :D_MODEL]) * scale
        o_ctx = head_out(_diff_attend(q_c[0], q_c[1], k_c[0], k_c[1], v_c, lam))
    return o_lat, o_ctx


def _moe(h, router_w, router_bias, w_gate, w_up, w_down):
    n_tok, d = h.shape
    scores = jax.nn.sigmoid((h @ router_w).astype(jnp.float32))
    biased = scores + router_bias.astype(jnp.float32)
    group_score = lax.top_k(biased.reshape(n_tok, N_EXPERT_GROUPS, EXPERTS_PER_GROUP), 2)[0].sum(-1)
    group_sel = jnp.argmax(group_score, axis=-1)
    in_group = (jnp.arange(N_EXPERTS) // EXPERTS_PER_GROUP)[None, :] == group_sel[:, None]
    _, expert_idx = lax.top_k(jnp.where(in_group, biased, -jnp.inf), TOP_K)
    gate = jnp.take_along_axis(scores, expert_idx, axis=-1)
    gate = gate / gate.sum(-1, keepdims=True)

    n_assign = n_tok * TOP_K
    e_flat = expert_idx.reshape(-1)
    order = jnp.argsort(e_flat)
    e_sorted = e_flat[order]
    counts = jnp.bincount(e_flat, length=N_EXPERTS)
    starts = jnp.cumsum(counts) - counts
    padded = (counts + MOE_BLOCK - 1) // MOE_BLOCK * MOE_BLOCK
    pad_end = jnp.cumsum(padded)
    pad_start = pad_end - padded
    dest_sorted = (pad_start[e_sorted] + jnp.arange(n_assign) - starts[e_sorted]).astype(jnp.int32)
    dest = jnp.zeros((n_assign,), jnp.int32).at[order].set(dest_sorted)
    n_blocks = -(-(n_assign + N_EXPERTS * (MOE_BLOCK - 1)) // MOE_BLOCK)
    buf = jnp.zeros((n_blocks * MOE_BLOCK, d), h.dtype).at[dest].set(jnp.repeat(h, TOP_K, axis=0))
    block_expert = jnp.minimum(
        jnp.searchsorted(pad_end, jnp.arange(n_blocks) * MOE_BLOCK, side='right'), N_EXPERTS - 1)

    def expert_block(args):
        xb, e = args
        return (jax.nn.silu(xb @ w_gate[e]) * (xb @ w_up[e])) @ w_down[e]

    out = lax.map(expert_block, (buf.reshape(n_blocks, MOE_BLOCK, d), block_expert)).reshape(-1, d)
    y = out[dest].reshape(n_tok, TOP_K, d)
    return jnp.einsum('nkd,nk->nd', y, gate.astype(h.dtype))


def setup_inputs(seed: int = 0) -> dict:
    key = jax.random.key(seed)
    ks = jax.random.split(key, 32)
    f32 = jnp.float32

    def nrm(i, shape, scale):
        return jax.random.normal(ks[i], shape, f32) * scale

    g0 = 1.0 - 2.0 ** (-5.0 - np.arange(RET_HEADS))
    logit0 = jnp.asarray(np.log(g0 / (1.0 - g0)), f32)
    lam_im = jnp.broadcast_to(jnp.pi * jnp.arange(S5_STATE, dtype=f32), (N_EVEN, 2, S5_GROUPS, S5_STATE))
    b_scale = (2.0 * S5_GROUP_SIZE) ** -0.5
    c_scale = (2.0 * S5_STATE) ** -0.5
    return {
        "x": nrm(0, (BATCH, SEQ, D_MODEL), 1.0),
        "c": nrm(1, (BATCH, D_MODEL), 1.0),
        "ctx": nrm(2, (BATCH, CTX_LEN, D_MODEL), 1.0),
        "c_ctx": nrm(3, (D_MODEL,), 1.0),
        "ada_w": nrm(4, (DEPTH, D_MODEL, 6 * D_MODEL), 0.5 * D_MODEL ** -0.5),
        "ada_b": nrm(5, (DEPTH, 6 * D_MODEL), 0.02),
        "ln_g": 1.0 + nrm(6, (DEPTH, 2, D_MODEL), 0.02),
        "w_in_ab": nrm(7, (N_EVEN, D_MODEL, EVEN_IN_W), D_MODEL ** -0.5),
        "ret_decay_logit": logit0 + nrm(8, (N_EVEN, 2, RET_HEADS), 0.05),
        "s5_lam_re": -0.5 + nrm(9, (N_EVEN, 2, S5_GROUPS, S5_STATE), 0.01),
        "s5_lam_im": lam_im,
        "s5_log_dt": jax.random.uniform(ks[10], (N_EVEN, 2, S5_GROUPS), f32,
                                        math.log(1e-3), math.log(1e-1)),
        "s5_b_re": nrm(11, (N_EVEN, 2, S5_GROUPS, S5_STATE, S5_GROUP_SIZE), b_scale),
        "s5_b_im": nrm(12, (N_EVEN, 2, S5_GROUPS, S5_STATE, S5_GROUP_SIZE), b_scale),
        "s5_c_re": nrm(13, (N_EVEN, 2, S5_GROUPS, S5_GROUP_SIZE, S5_STATE), c_scale),
        "s5_c_im": nrm(14, (N_EVEN, 2, S5_GROUPS, S5_GROUP_SIZE, S5_STATE), c_scale),
        "s5_d": nrm(15, (N_EVEN, S5_GROUPS, S5_GROUP_SIZE), 1.0),
        "s5_w_glu": nrm(16, (N_EVEN, S5_CH, 2 * S5_CH), S5_CH ** -0.5),
        "w_out_ab": nrm(17, (N_EVEN, EVEN_OUT_W, D_MODEL), BETA * EVEN_OUT_W ** -0.5),
        "w_in_c": nrm(18, (N_ODD, D_MODEL, 3 * D_MODEL), D_MODEL ** -0.5),
        "diff_lambda": nrm(19, (N_ODD, 4, DIFF_DH), 0.1),
        "diff_subln_g": 1.0 + nrm(20, (N_ODD, 2 * DIFF_DH), 0.02),
        "w_out_c": nrm(21, (N_ODD, D_MODEL, D_MODEL), BETA * D_MODEL ** -0.5),
        "router_w": nrm(22, (D_MODEL, N_EXPERTS), D_MODEL ** -0.5),
        "router_bias": nrm(23, (N_EXPERTS,), 0.01),
        "exp_w_gate": nrm(24, (DEPTH, N_EXPERTS, D_MODEL, D_FF), D_MODEL ** -0.5),
        "exp_w_up": nrm(25, (DEPTH, N_EXPERTS, D_MODEL, D_FF), D_MODEL ** -0.5),
        "exp_w_down": nrm(26, (DEPTH, N_EXPERTS, D_FF, D_MODEL), BETA * D_FF ** -0.5),
    }


def reference(x, c, ctx, c_ctx, ada_w, ada_b, ln_g, w_in_ab, ret_decay_logit, s5_lam_re,
              s5_lam_im, s5_log_dt, s5_b_re, s5_b_im, s5_c_re, s5_c_im, s5_d, s5_w_glu,
              w_out_ab, w_in_c, diff_lambda, diff_subln_g, w_out_c, router_w, router_bias,
              exp_w_gate, exp_w_up, exp_w_down):
    bsz, L, d = x.shape
    x_lat, x_ctx = x, ctx
    c_all = jnp.concatenate([c, c_ctx[None].astype(c.dtype)], axis=0)
    for i in range(DEPTH):
        last = i == DEPTH - 1
        mod = jax.nn.silu(c_all) @ ada_w[i] + ada_b[i]
        sh1, sc1, g1, sh2, sc2, g2 = jnp.split(mod[:bsz, None, :], 6, axis=-1)
        csh1, csc1, cg1, csh2, csc2, cg2 = jnp.split(mod[bsz:, None, :], 6, axis=-1)
        h_lat = _modulate(x_lat, sh1, sc1)
        h_ctx = _modulate(x_ctx, csh1, csc1)
        j = i // 2
        if i % 2 == 0:
            o_lat, o_ctx = _even_mixer(h_lat, h_ctx, w_in_ab[j], ret_decay_logit[j], s5_lam_re[j],
                                       s5_lam_im[j], s5_log_dt[j], s5_b_re[j], s5_b_im[j],
                                       s5_c_re[j], s5_c_im[j], s5_d[j], s5_w_glu[j], w_out_ab[j])
        else:
            lambda_init = 0.8 - 0.6 * math.exp(-0.3 * i)
            o_lat, o_ctx = _odd_mixer(h_lat, h_ctx, w_in_c[j], diff_lambda[j], diff_subln_g[j],
                                      w_out_c[j], lambda_init, not last)
        x_lat = _layer_norm(ALPHA * x_lat + g1 * o_lat, ln_g[i, 0])
        h_lat = _modulate(x_lat, sh2, sc2)
        if last:
            y_lat = _moe(h_lat.reshape(-1, d), router_w, router_bias, exp_w_gate[i],
                         exp_w_up[i], exp_w_down[i]).reshape(h_lat.shape)
        else:
            x_ctx = _layer_norm(ALPHA * x_ctx + cg1 * o_ctx, ln_g[i, 0])
            h_ctx = _modulate(x_ctx, csh2, csc2)
            tokens = jnp.concatenate([h_lat.reshape(-1, d), h_ctx.reshape(-1, d)], axis=0)
            y_all = _moe(tokens, router_w, router_bias, exp_w_gate[i], exp_w_up[i], exp_w_down[i])
            y_lat = y_all[:bsz * L].reshape(h_lat.shape)
            y_ctx = y_all[bsz * L:].reshape(h_ctx.shape)
            x_ctx = _layer_norm(ALPHA * x_ctx + cg2 * y_ctx, ln_g[i, 1])
        x_lat = _layer_norm(ALPHA * x_lat + g2 * y_lat, ln_g[i, 1])
    return x_lat
```

```python
import functools
import math

import jax
import jax.numpy as jnp
import numpy as np
from jax import lax
from jax.experimental import pallas as pl
from jax.experimental.pallas import tpu as pltpu

F32 = jnp.float32
BF16 = jnp.bfloat16

D_MODEL = 1024
DEPTH = 2
GRID_W = 64
ALPHA = (2.0 * DEPTH) ** 0.25
LN_EPS = 1e-5
GN_EPS = 1e-6
ROPE_BASE = 10000.0
RET_DK = 64
RET_DV = 128
RET_HEADS = 6
RET_QK_W = RET_HEADS * RET_DK
RET_V_W = RET_HEADS * RET_DV
S5_CH = 256
S5_P = 16
S5_G = 16
S5_N = 64
DIFF_HEADS = 8
DIFF_DH = 64
N_EXPERTS = 16
EXPERTS_PER_GROUP = 4
TOP_K = 2

LANES = 128
SUBLANES = 8
MXU_DIM = 256
ROW_TILE = 256
RET_CHUNK = 256
S5_CHUNK = 32
MOE_ROWS = 256
ATT_TQ = 256
ATT_TK = 768
VMEM_LIMIT = 48 * 1024 * 1024


def _cparams(sem):
    return pltpu.CompilerParams(dimension_semantics=sem, vmem_limit_bytes=VMEM_LIMIT)


def _dot(a, b):
    return jnp.dot(a, b, preferred_element_type=F32)


def _dot_nt(a, b):
    return lax.dot_general(a, b, (((1,), (1,)), ((), ())), preferred_element_type=F32)


def _dot_tn(a, b):
    return lax.dot_general(a, b, (((0,), (0,)), ((), ())), preferred_element_type=F32)


def _split_bf16(x):
    hi = x.astype(BF16)
    lo = (x - hi.astype(F32)).astype(BF16)
    return hi, lo


def _dot3(a, b):
    ah, al = _split_bf16(a)
    bh, bl = _split_bf16(b)
    return _dot(ah, bh) + _dot(ah, bl) + _dot(al, bh)


def _dot3_nt(a, b):
    ah, al = _split_bf16(a)
    bh, bl = _split_bf16(b)
    return _dot_nt(ah, bh) + _dot_nt(ah, bl) + _dot_nt(al, bh)


def _sigmoid(x):
    return 1.0 / (1.0 + jnp.exp(-x))


def _silu(x):
    return x * _sigmoid(x)


def _gelu_tanh(x):
    c = math.sqrt(2.0 / math.pi)
    return 0.5 * x * (1.0 + jnp.tanh(c * (x + 0.044715 * (x * x * x))))


def _adaln_kernel(c_ref, w_ref, b_ref, o_ref):
    c = c_ref[...]
    o_ref[0] = _dot3(_silu(c), w_ref[0]) + b_ref[0]


def _adaln(c_pad, ada_w, ada_b):
    depth, d, n = ada_w.shape
    tn = 1536
    return pl.pallas_call(
        _adaln_kernel,
        grid=(depth, n // tn),
        in_specs=[
            pl.BlockSpec((SUBLANES, d), lambda i, j: (0, 0)),
            pl.BlockSpec((1, d, tn), lambda i, j: (i, 0, j)),
            pl.BlockSpec((1, 1, tn), lambda i, j: (i, 0, j)),
        ],
        out_specs=pl.BlockSpec((1, SUBLANES, tn), lambda i, j: (i, 0, j)),
        out_shape=jax.ShapeDtypeStruct((depth, SUBLANES, n), F32),
        compiler_params=_cparams(("parallel", "parallel")),
        name="adaln",
    )(c_pad, ada_w, ada_b.reshape(depth, 1, n))


def _mod_row(t, tiles_per_batch, ctx_tiles, nb):
    b = lax.div(t, tiles_per_batch)
    w = lax.rem(t, tiles_per_batch)
    return jnp.where(w < ctx_tiles, nb, b)


def _rope_block(a, cos, sin, half):
    lane = lax.broadcasted_iota(jnp.int32, a.shape, 1)
    first = lax.rem(lane, 2 * half) < half
    rot = jnp.where(first, pltpu.roll(a, LANES - half, 1), pltpu.roll(a, half, 1))
    return a * cos + rot * sin


def _inproj_kernel(x_ref, mod_ref, w_ref, cos_ref, sin_ref, o_ref, *, tiles_per_batch,
                   ctx_tiles, nb, rope_tab, rope_half):
    d = x_ref.shape[1]
    n = w_ref.shape[1]
    r = _mod_row(pl.program_id(0), tiles_per_batch, ctx_tiles, nb)
    sh = mod_ref[0, pl.ds(r, 1), 0:d]
    sc = mod_ref[0, pl.ds(r, 1), d:2 * d]
    xm = (x_ref[...] * (1.0 + sc) + sh).astype(BF16)
    for j in range(n // MXU_DIM):
        acc = _dot(xm, w_ref[:, j * MXU_DIM:(j + 1) * MXU_DIM])
        parts = []
        for s in range(MXU_DIM // LANES):
            blk = acc[:, s * LANES:(s + 1) * LANES]
            tab = rope_tab[j * (MXU_DIM // LANES) + s]
            if tab is not None:
                blk = _rope_block(blk, cos_ref[tab], sin_ref[tab], rope_half)
            parts.append(blk)
        o_ref[:, j * MXU_DIM:(j + 1) * MXU_DIM] = jnp.concatenate(parts, axis=1).astype(BF16)


def _inproj(x, mod, layer, w, cos, sin, rope_tab, rope_half, *, nb, t_len, ctx_len):
    nt, d = x.shape
    n = w.shape[1]
    tpb = t_len // ROW_TILE
    kern = functools.partial(_inproj_kernel, tiles_per_batch=tpb, ctx_tiles=ctx_len // ROW_TILE,
                             nb=nb, rope_tab=tuple(rope_tab), rope_half=rope_half)
    ntab = cos.shape[0]
    return pl.pallas_call(
        kern,
        grid=(nt // ROW_TILE,),
        in_specs=[
            pl.BlockSpec((ROW_TILE, d), lambda t: (t, 0)),
            pl.BlockSpec((1, SUBLANES, mod.shape[2]), lambda t: (layer, 0, 0)),
            pl.BlockSpec((d, n), lambda t: (0, 0)),
            pl.BlockSpec((ntab, ROW_TILE, LANES), lambda t: (0, lax.rem(t, tpb), 0)),
            pl.BlockSpec((ntab, ROW_TILE, LANES), lambda t: (0, lax.rem(t, tpb), 0)),
        ],
        out_specs=pl.BlockSpec((ROW_TILE, n), lambda t: (t, 0)),
        out_shape=jax.ShapeDtypeStruct((nt, n), BF16),
        compiler_params=_cparams(("parallel",)),
        name="inproj%d" % layer,
    )(x, mod, w, cos, sin)


def _retention_kernel(lg_ref, qk_ref, v_ref, o_ref, sf_ref, sb_ref, *, nctx):
    c_len = RET_CHUNK
    t_len = qk_ref.shape[0]
    nc = t_len // c_len
    h = pl.program_id(1)
    lgf = lg_ref[0, h]
    lgb = lg_ref[1, h]
    ii = lax.broadcasted_iota(jnp.int32, (c_len, 1), 0).astype(F32)
    jj = lax.broadcasted_iota(jnp.int32, (1, c_len), 1).astype(F32)
    diff = ii - jj
    decay = jnp.where(diff >= 0.0, jnp.exp(lgf * jnp.maximum(diff, 0.0)),
                      jnp.exp(lgb * jnp.maximum(-diff, 0.0)))
    kdf = jnp.exp(lgf * (c_len - 1.0 - ii))
    kdb = jnp.exp(lgb * ii)
    qdf = jnp.exp(lgf * (ii + 1.0))
    qdb = jnp.exp(lgb * (c_len - ii))
    zrow = jnp.zeros((1, RET_DV), F32)
    gf_chunk = jnp.exp(zrow + lgf * c_len)
    gb_chunk = jnp.exp(zrow + lgb * c_len)

    def load(c):
        rows = pl.ds(pl.multiple_of(c * c_len, c_len), c_len)
        qk = qk_ref[rows, :].astype(F32)
        return qk[:, :RET_DK], qk[:, RET_DK:], v_ref[rows, :]

    def fwd_state(c, s):
        sf_ref[c] = s
        _, k, v = load(c)
        return gf_chunk * s + _dot_tn((k * kdf).astype(BF16), v)

    lax.fori_loop(0, nc, fwd_state, jnp.zeros((RET_DK, RET_DV), F32))

    def bwd_state(j, s):
        c = jnp.where(j < nctx, nctx - 1 - j, nc - 1 - (j - nctx))
        sb_ref[c] = s
        _, k, v = load(c)
        return gb_chunk * s + _dot_tn((k * kdb).astype(BF16), v)

    lax.fori_loop(0, nc, bwd_state, jnp.zeros((RET_DK, RET_DV), F32))

    def out_chunk(c, carry):
        q, k, v = load(c)
        scores = _dot_nt(q.astype(BF16), k.astype(BF16)) * decay
        o = _dot(scores.astype(BF16), v)
        o = o + _dot((q * qdf).astype(BF16), sf_ref[c].astype(BF16))
        o = o + _dot((q * qdb).astype(BF16), sb_ref[c].astype(BF16))
        mu = jnp.mean(o, axis=-1, keepdims=True)
        oc = o - mu
        var = jnp.mean(oc * oc, axis=-1, keepdims=True)
        rows = pl.ds(pl.multiple_of(c * c_len, c_len), c_len)
        o_ref[rows, :] = (oc * lax.rsqrt(var + GN_EPS)).astype(BF16)
        return carry

    lax.fori_loop(0, nc, out_chunk, 0)


def _retention(proj, log_gammas, *, nb, t_len, ctx_len):
    nt = proj.shape[0]
    nc = t_len // RET_CHUNK
    kern = functools.partial(_retention_kernel, nctx=ctx_len // RET_CHUNK)
    vcol0 = RET_HEADS
    return pl.pallas_call(
        kern,
        grid_spec=pltpu.PrefetchScalarGridSpec(
            num_scalar_prefetch=1,
            grid=(nb, RET_HEADS),
            in_specs=[
                pl.BlockSpec((t_len, LANES), lambda b, h, lg: (b, h)),
                pl.BlockSpec((t_len, LANES), lambda b, h, lg: (b, vcol0 + h)),
            ],
            out_specs=pl.BlockSpec((t_len, LANES), lambda b, h, lg: (b, h)),
            scratch_shapes=[pltpu.VMEM((nc, RET_DK, RET_DV), F32),
                            pltpu.VMEM((nc, RET_DK, RET_DV), F32)],
        ),
        out_shape=jax.ShapeDtypeStruct((nt, RET_V_W), BF16),
        compiler_params=_cparams(("parallel", "parallel")),
        name="retention",
    )(log_gammas, proj, proj)


def _s5_operators(lam_re, lam_im, log_dt, b_re, b_im, c_re, c_im, d_skip):
    tc = S5_CHUNK
    hp = lax.Precision.HIGHEST
    ks = jnp.arange(tc + 1, dtype=F32)
    pw, bbar, cm = [], [], []
    for direction in range(2):
        dt = jnp.exp(log_dt[direction].astype(F32))[:, None]
        lam = lax.complex(lam_re[direction].astype(F32), lam_im[direction].astype(F32))
        z = lam * dt
        p = jnp.exp(z[None] * ks[:, None, None])
        lam_bar = p[1]
        bb = ((lam_bar - 1.0) / lam)[..., None] * lax.complex(
            b_re[direction].astype(F32), b_im[direction].astype(F32))
        pw.append(p)
        bbar.append(bb)
        cm.append(lax.complex(c_re[direction].astype(F32), c_im[direction].astype(F32)))

    def lag_kernel(p, bb, c):
        return jnp.einsum('gpn,kgn,gnq->kgpq', c, p[:tc], bb, precision=hp).real

    kf = lag_kernel(pw[0], bbar[0], cm[0])
    kb = lag_kernel(pw[1], bbar[1], cm[1])
    k0 = kf[0] + kb[0] + jnp.eye(S5_P, dtype=F32)[None] * d_skip.astype(F32)[:, :, None]
    kcat = jnp.concatenate([kb[1:][::-1], k0[None], kf[1:]], axis=0)
    s_idx = jnp.arange(tc)[:, None]
    t_idx = jnp.arange(tc)[None, :]
    m5 = kcat[t_idx - s_idx + tc - 1]
    intra = m5.transpose(2, 0, 4, 1, 3).reshape(S5_G, tc * S5_P, tc * S5_P)

    ef = pw[0][:tc][::-1][:, :, :, None] * bbar[0][None]
    eb = pw[1][:tc][:, :, :, None] * bbar[1][None]
    to_in = lambda e: e.transpose(1, 0, 3, 2).reshape(S5_G, tc * S5_P, S5_N)
    w1 = jnp.concatenate([intra, to_in(ef.real), to_in(ef.imag), to_in(eb.real), to_in(eb.imag)],
                         axis=2)
    of = cm[0][None] * pw[0][1:][:, :, None, :]
    ob = cm[1][None] * pw[1][1:][::-1][:, :, None, :]
    to_out = lambda o: o.transpose(1, 3, 0, 2).reshape(S5_G, S5_N, tc * S5_P)
    w2 = jnp.concatenate([to_out(of.real), -to_out(of.imag), to_out(ob.real), -to_out(ob.imag)],
                         axis=1)
    a = jnp.stack([pw[0][tc].real, pw[0][tc].imag, pw[1][tc].real, pw[1][tc].imag], axis=1)
    return w1.astype(BF16), w2.astype(BF16), a


def _s5_kernel(x_ref, w1_ref, w2_ref, a_ref, y_ref, e_ref, st_ref, *, nb, nctx):
    n = S5_N
    w = x_ref.shape[2]
    rows = x_ref.shape[1]
    nc = rows // nb
    x = x_ref[0]
    e_ref[...] = _dot(x, w1_ref[0, :, w:])
    afr = a_ref[0, 0:1, :]
    afi = a_ref[0, 1:2, :]
    abr = a_ref[0, 2:3, :]
    abi = a_ref[0, 3:4, :]

    def step(j, carry):
        fr, fi, br, bi = carry
        rf = pl.ds(pl.multiple_of(j * nb, nb), nb)
        cb = jnp.where(j < nctx, nctx - 1 - j, nc - 1 - (j - nctx))
        rb = pl.ds(pl.multiple_of(cb * nb, nb), nb)
        st_ref[rf, 0:n] = fr
        st_ref[rf, n:2 * n] = fi
        st_ref[rb, 2 * n:3 * n] = br
        st_ref[rb, 3 * n:4 * n] = bi
        nfr = afr * fr - afi * fi + e_ref[rf, 0:n]
        nfi = afr * fi + afi * fr + e_ref[rf, n:2 * n]
        nbr = abr * br - abi * bi + e_ref[rb, 2 * n:3 * n]
        nbi = abr * bi + abi * br + e_ref[rb, 3 * n:4 * n]
        return nfr, nfi, nbr, nbi

    z = jnp.zeros((nb, n), F32)
    lax.fori_loop(0, nc, step, (z, z, z, z))
    y_ref[0] = _dot(x, w1_ref[0, :, :w]) + _dot(st_ref[...].astype(BF16), w2_ref[0])


def _s5(xg, w1, w2, a, *, nb, nctx):
    g, rows, w = xg.shape
    kern = functools.partial(_s5_kernel, nb=nb, nctx=nctx)
    return pl.pallas_call(
        kern,
        grid=(g,),
        in_specs=[
            pl.BlockSpec((1, rows, w), lambda i: (i, 0, 0)),
            pl.BlockSpec((1, w, w + 4 * S5_N), lambda i: (i, 0, 0)),
            pl.BlockSpec((1, 4 * S5_N, w), lambda i: (i, 0, 0)),
            pl.BlockSpec((1, 4, S5_N), lambda i: (i, 0, 0)),
        ],
        out_specs=pl.BlockSpec((1, rows, w), lambda i: (i, 0, 0)),
        out_shape=jax.ShapeDtypeStruct((g, rows, w), F32),
        scratch_shapes=[pltpu.VMEM((rows, 4 * S5_N), F32), pltpu.VMEM((rows, 4 * S5_N), F32)],
        compiler_params=_cparams(("parallel",)),
        name="s5",
    )(xg, w1, w2, a)


def _route(logits_t, bias):
    scores = _sigmoid(logits_t)
    biased = scores + bias
    s_rows = [scores[e:e + 1, :] for e in range(N_EXPERTS)]
    b_rows = [biased[e:e + 1, :] for e in range(N_EXPERTS)]
    n_groups = N_EXPERTS // EXPERTS_PER_GROUP
    best = None
    sel = None
    for g in range(n_groups):
        a, b, c, d = b_rows[4 * g:4 * g + 4]
        hi1, lo1 = jnp.maximum(a, b), jnp.minimum(a, b)
        hi2, lo2 = jnp.maximum(c, d), jnp.minimum(c, d)
        top1 = jnp.maximum(hi1, hi2)
        top2 = jnp.maximum(jnp.minimum(hi1, hi2), jnp.maximum(lo1, lo2))
        gs = top1 + top2
        if g == 0:
            best, sel = gs, jnp.zeros(gs.shape, jnp.int32)
        else:
            better = gs > best
            sel = jnp.where(better, g, sel)
            best = jnp.where(better, gs, best)
    neg = jnp.full(best.shape, -jnp.inf, F32)
    masked = [jnp.where(sel == (e // EXPERTS_PER_GROUP), b_rows[e], neg) for e in range(N_EXPERTS)]
    v1, i1, g1 = masked[0], jnp.zeros(best.shape, jnp.int32), s_rows[0]
    for e in range(1, N_EXPERTS):
        better = masked[e] > v1
        v1 = jnp.where(better, masked[e], v1)
        i1 = jnp.where(better, e, i1)
        g1 = jnp.where(better, s_rows[e], g1)
    v2, i2, g2 = neg, jnp.zeros(best.shape, jnp.int32), jnp.zeros(best.shape, F32)
    for e in range(N_EXPERTS):
        cand = jnp.where(i1 == e, neg, masked[e])
        better = cand > v2
        v2 = jnp.where(better, cand, v2)
        i2 = jnp.where(better, e, i2)
        g2 = jnp.where(better, s_rows[e], g2)
    tot = g1 + g2
    return jnp.concatenate([i1, i2], axis=0), jnp.concatenate([g1 / tot, g2 / tot], axis=0)


def _tail(x, o, mod_ref, r, lng, wr, rb, x1_ref, h2_ref, ei_ref, gt_ref):
    d = x.shape[1]
    g1 = mod_ref[0, pl.ds(r, 1), 2 * d:3 * d]
    sh2 = mod_ref[0, pl.ds(r, 1), 3 * d:4 * d]
    sc2 = mod_ref[0, pl.ds(r, 1), 4 * d:5 * d]
    y = ALPHA * x + g1 * o
    mu = jnp.mean(y, axis=-1, keepdims=True)
    yc = y - mu
    var = jnp.mean(yc * yc, axis=-1, keepdims=True)
    x1 = yc * lax.rsqrt(var + LN_EPS) * lng
    h2 = x1 * (1.0 + sc2) + sh2
    x1_ref[...] = x1
    h2_ref[...] = h2
    ei, gt = _route(_dot3_nt(wr, h2), rb)
    ei_ref[...] = ei
    gt_ref[...] = gt


def _merge0_kernel(r_ref, g_ref, s_ref, x_ref, mod_ref, wglu_ref, wout_ref, lng_ref, wr_ref,
                   rb_ref, x1_ref, h2_ref, ei_ref, gt_ref, *, tiles_per_batch, ctx_tiles, nb):
    r = _mod_row(pl.program_id(0), tiles_per_batch, ctx_tiles, nb)
    ret = r_ref[...].astype(F32) * _silu(g_ref[...].astype(F32))
    z = _dot(_gelu_tanh(s_ref[...]).astype(BF16), wglu_ref[...])
    zz = z[:, :S5_CH] * _sigmoid(z[:, S5_CH:])
    o = _dot(ret.astype(BF16), wout_ref[0:RET_V_W, :]) + _dot(zz.astype(BF16), wout_ref[RET_V_W:, :])
    _tail(x_ref[...], o, mod_ref, r, lng_ref[...], wr_ref[...], rb_ref[...],
          x1_ref, h2_ref, ei_ref, gt_ref)


def _merge1_kernel(a_ref, x_ref, mod_ref, wout_ref, lng_ref, wr_ref, rb_ref,
                   x1_ref, h2_ref, ei_ref, gt_ref, *, tiles_per_batch):
    r = lax.div(pl.program_id(0), tiles_per_batch)
    o = _dot(a_ref[...], wout_ref[...])
    _tail(x_ref[...], o, mod_ref, r, lng_ref[...], wr_ref[...], rb_ref[...],
          x1_ref, h2_ref, ei_ref, gt_ref)


def _tail_outs(n_rows, d):
    shapes = (jax.ShapeDtypeStruct((n_rows, d), F32), jax.ShapeDtypeStruct((n_rows, d), F32),
              jax.ShapeDtypeStruct((TOP_K, n_rows), jnp.int32),
              jax.ShapeDtypeStruct((TOP_K, n_rows), F32))
    specs = (pl.BlockSpec((ROW_TILE, d), lambda t: (t, 0)), pl.BlockSpec((ROW_TILE, d), lambda t: (t, 0)),
             pl.BlockSpec((TOP_K, ROW_TILE), lambda t: (0, t)),
             pl.BlockSpec((TOP_K, ROW_TILE), lambda t: (0, t)))
    return shapes, specs


def _merge0(ret, proj, s5y, x, mod, w_glu, w_out, lng, wr_t, rbias, *, nb, t_len, ctx_len):
    nt, d = x.shape
    tpb = t_len // ROW_TILE
    kern = functools.partial(_merge0_kernel, tiles_per_batch=tpb, ctx_tiles=ctx_len // ROW_TILE, nb=nb)
    shapes, specs = _tail_outs(nt, d)
    gcol = (2 * RET_QK_W + RET_V_W) // RET_V_W
    full = lambda a: pl.BlockSpec(a.shape, lambda t: (0,) * a.ndim)
    return pl.pallas_call(
        kern,
        grid=(nt // ROW_TILE,),
        in_specs=[
            pl.BlockSpec((ROW_TILE, RET_V_W), lambda t: (t, 0)),
            pl.BlockSpec((ROW_TILE, RET_V_W), lambda t: (t, gcol)),
            pl.BlockSpec((ROW_TILE, S5_CH), lambda t: (t, 0)),
            pl.BlockSpec((ROW_TILE, d), lambda t: (t, 0)),
            pl.BlockSpec((1, SUBLANES, mod.shape[2]), lambda t: (0, 0, 0)),
            full(w_glu), full(w_out), full(lng), full(wr_t), full(rbias),
        ],
        out_specs=specs,
        out_shape=shapes,
        compiler_params=_cparams(("parallel",)),
        name="merge0",
    )(ret, proj, s5y, x, mod, w_glu, w_out, lng, wr_t, rbias)


def _merge1(att, x, mod, layer, w_out, lng, wr_t, rbias, *, nb, l_len, t_len, ctx_len):
    n_lat, d = att.shape
    tpb = l_len // ROW_TILE
    tpb_t = t_len // ROW_TILE
    ctx_tiles = ctx_len // ROW_TILE
    kern = functools.partial(_merge1_kernel, tiles_per_batch=tpb)
    shapes, specs = _tail_outs(n_lat, d)
    full = lambda a: pl.BlockSpec(a.shape, lambda t: (0,) * a.ndim)
    xrow = lambda t: (lax.div(t, tpb) * tpb_t + ctx_tiles + lax.rem(t, tpb), 0)
    return pl.pallas_call(
        kern,
        grid=(n_lat // ROW_TILE,),
        in_specs=[
            pl.BlockSpec((ROW_TILE, d), lambda t: (t, 0)),
            pl.BlockSpec((ROW_TILE, d), xrow),
            pl.BlockSpec((1, SUBLANES, mod.shape[2]), lambda t: (layer, 0, 0)),
            full(w_out), full(lng), full(wr_t), full(rbias),
        ],
        out_specs=specs,
        out_shape=shapes,
        compiler_params=_cparams(("parallel",)),
        name="merge1",
    )(att, x, mod, w_out, lng, wr_t, rbias)


def _moe_plan(eidx):
    k, n = eidx.shape
    a = k * n
    e_flat = eidx.reshape(a)
    onehot = (e_flat[:, None] == jnp.arange(N_EXPERTS, dtype=jnp.int32)[None, :]).astype(jnp.int32)
    csum = jnp.cumsum(onehot, axis=0)
    counts = csum[-1]
    rank = jnp.take_along_axis(csum, e_flat[:, None], axis=1)[:, 0] - 1
    padded = (counts + MOE_ROWS - 1) // MOE_ROWS * MOE_ROWS
    pad_end = jnp.cumsum(padded)
    pad_start = pad_end - padded
    dest = (pad_start[e_flat] + rank).astype(jnp.int32)
    n_blocks = -(-(a + N_EXPERTS * (MOE_ROWS - 1)) // MOE_ROWS)
    tok = jnp.tile(jnp.arange(n, dtype=jnp.int32), k)
    src = jnp.zeros((n_blocks * MOE_ROWS,), jnp.int32).at[dest].set(tok)
    block_expert = jnp.minimum(
        jnp.searchsorted(pad_end, jnp.arange(n_blocks, dtype=jnp.int32) * MOE_ROWS, side='right'),
        N_EXPERTS - 1).astype(jnp.int32)
    n_used = (pad_end[-1] // MOE_ROWS).astype(jnp.int32).reshape(1)
    return src.reshape(n_blocks, 1, MOE_ROWS), dest.reshape(k, n), block_expert, n_used


def _row_copy(src_hbm, row, dst_vmem, slot, sem):
    return pltpu.make_async_copy(src_hbm.at[pl.ds(row, 1)], dst_vmem.at[pl.ds(slot, 1)], sem)


def _experts_kernel(be_ref, nu_ref, src_ref, h_hbm, wg_ref, wu_ref, wd_ref, o_ref, xbuf, sem):
    i = pl.program_id(0)
    rows = xbuf.shape[0]

    @pl.when(i < nu_ref[0])
    def _():
        def issue(r, c):
            _row_copy(h_hbm, src_ref[0, 0, r], xbuf, r, sem).start()
            return c

        lax.fori_loop(0, rows, issue, 0)

        def wait(r, c):
            _row_copy(h_hbm, 0, xbuf, r, sem).wait()
            return c

        lax.fori_loop(0, rows, wait, 0)
        x = xbuf[...].astype(BF16)
        hg = _dot(x, wg_ref[0])
        hu = _dot(x, wu_ref[0])
        o_ref[...] = _dot((_silu(hg) * hu).astype(BF16), wd_ref[0])

    @pl.when(i >= nu_ref[0])
    def _():
        o_ref[...] = jnp.zeros(o_ref.shape, o_ref.dtype)


def _experts(h, src, block_expert, n_used, wg, wu, wd):
    n_blocks = src.shape[0]
    d = h.shape[1]
    dff = wg.shape[2]
    return pl.pallas_call(
        _experts_kernel,
        grid_spec=pltpu.PrefetchScalarGridSpec(
            num_scalar_prefetch=2,
            grid=(n_blocks,),
            in_specs=[
                pl.BlockSpec((1, 1, MOE_ROWS), lambda i, be, nu: (i, 0, 0), memory_space=pltpu.SMEM),
                pl.BlockSpec(memory_space=pl.ANY),
                pl.BlockSpec((1, d, dff), lambda i, be, nu: (be[i], 0, 0)),
                pl.BlockSpec((1, d, dff), lambda i, be, nu: (be[i], 0, 0)),
                pl.BlockSpec((1, dff, d), lambda i, be, nu: (be[i], 0, 0)),
            ],
            out_specs=pl.BlockSpec((MOE_ROWS, d), lambda i, be, nu: (i, 0)),
            scratch_shapes=[pltpu.VMEM((MOE_ROWS, d), F32), pltpu.SemaphoreType.DMA],
        ),
        out_shape=jax.ShapeDtypeStruct((n_blocks * MOE_ROWS, d), F32),
        compiler_params=_cparams(("arbitrary",)),
        name="moe_experts",
    )(block_expert, n_used, src, h, wg, wu, wd)


def _combine_kernel(dest_ref, x_ref, gt_ref, mod_ref, lng_ref, y_hbm, o_ref, ybuf, sem, *,
                    tiles_per_batch, ctx_tiles, nb):
    rows = x_ref.shape[0]
    d = x_ref.shape[1]
    r = _mod_row(pl.program_id(0), tiles_per_batch, ctx_tiles, nb)

    def issue(j, c):
        _row_copy(y_hbm, dest_ref[0, 0, j], ybuf, j, sem).start()
        return c

    lax.fori_loop(0, TOP_K * rows, issue, 0)

    def wait(j, c):
        _row_copy(y_hbm, 0, ybuf, j, sem).wait()
        return c

    lax.fori_loop(0, TOP_K * rows, wait, 0)
    gt = gt_ref[...]
    y = ybuf[0:rows, :] * gt[:, 0:1] + ybuf[rows:2 * rows, :] * gt[:, 1:2]
    g2 = mod_ref[0, pl.ds(r, 1), 5 * d:6 * d]
    z = ALPHA * x_ref[...] + g2 * y
    mu = jnp.mean(z, axis=-1, keepdims=True)
    zc = z - mu
    var = jnp.mean(zc * zc, axis=-1, keepdims=True)
    o_ref[...] = zc * lax.rsqrt(var + LN_EPS) * lng_ref[...]


def _combine(x1, dest, gates, mod, layer, lng, yexp, *, tiles_per_batch, ctx_tiles, nb):
    n, d = x1.shape
    nt = n // ROW_TILE
    dest_t = dest.reshape(TOP_K, nt, ROW_TILE).transpose(1, 0, 2).reshape(nt, 1, TOP_K * ROW_TILE)
    kern = functools.partial(_combine_kernel, tiles_per_batch=tiles_per_batch, ctx_tiles=ctx_tiles, nb=nb)
    return pl.pallas_call(
        kern,
        grid=(nt,),
        in_specs=[
            pl.BlockSpec((1, 1, TOP_K * ROW_TILE), lambda t: (t, 0, 0), memory_space=pltpu.SMEM),
            pl.BlockSpec((ROW_TILE, d), lambda t: (t, 0)),
            pl.BlockSpec((ROW_TILE, TOP_K), lambda t: (t, 0)),
            pl.BlockSpec((1, SUBLANES, mod.shape[2]), lambda t: (layer, 0, 0)),
            pl.BlockSpec((1, d), lambda t: (0, 0)),
            pl.BlockSpec(memory_space=pl.ANY),
        ],
        out_specs=pl.BlockSpec((ROW_TILE, d), lambda t: (t, 0)),
        out_shape=jax.ShapeDtypeStruct((n, d), F32),
        scratch_shapes=[pltpu.VMEM((TOP_K * ROW_TILE, d), F32), pltpu.SemaphoreType.DMA],
        compiler_params=_cparams(("arbitrary",)),
        name="moe_combine%d" % layer,
    )(dest_t, x1, gates.T, mod, lng, yexp)


def _moe_layer(x1, h2, eidx, gates, mod, layer, lng, wg, wu, wd, *, tiles_per_batch, ctx_tiles, nb):
    src, dest, block_expert, n_used = _moe_plan(eidx)
    yexp = _experts(h2, src, block_expert, n_used, wg, wu, wd)
    return _combine(x1, dest, gates, mod, layer, lng, yexp,
                    tiles_per_batch=tiles_per_batch, ctx_tiles=ctx_tiles, nb=nb)


def _attn_kernel(lam_ref, q_ref, k_ref, v_ref, g_ref, o_ref, acc1, acc2, m1, m2, l1, l2, *,
                 out_scale):
    t_len = k_ref.shape[0]
    nk = t_len // ATT_TK
    q = q_ref[...]
    lane = lax.broadcasted_iota(jnp.int32, q.shape, 1)
    zero = jnp.zeros(q.shape, q.dtype)
    qa = jnp.where(lane < DIFF_DH, q, zero)
    qb = jnp.where(lane >= DIFF_DH, q, zero)
    acc1[...] = jnp.zeros(acc1.shape, F32)
    acc2[...] = jnp.zeros(acc2.shape, F32)
    m1[...] = jnp.full(m1.shape, -jnp.inf, F32)
    m2[...] = jnp.full(m2.shape, -jnp.inf, F32)
    l1[...] = jnp.zeros(l1.shape, F32)
    l2[...] = jnp.zeros(l2.shape, F32)

    def kv_step(j, carry):
        rows = pl.ds(pl.multiple_of(j * ATT_TK, ATT_TK), ATT_TK)
        k = k_ref[rows, :]
        v = v_ref[rows, :]
        for qz, acc, m, l in ((qa, acc1, m1, l1), (qb, acc2, m2, l2)):
            s = _dot_nt(qz, k)
            m_old = m[...]
            m_new = jnp.maximum(m_old, jnp.max(s, axis=-1, keepdims=True))
            p = jnp.exp(s - m_new)
            corr = jnp.exp(m_old - m_new)
            l[...] = corr * l[...] + jnp.sum(p, axis=-1, keepdims=True)
            acc[...] = corr * acc[...] + _dot(p.astype(BF16), v)
            m[...] = m_new
        return carry

    lax.fori_loop(0, nk, kv_step, 0)
    o = acc1[...] / l1[...] - lam_ref[0] * (acc2[...] / l2[...])
    o = o * lax.rsqrt(jnp.mean(o * o, axis=-1, keepdims=True) + GN_EPS)
    o_ref[...] = (o * g_ref[...] * out_scale).astype(BF16)


def _diff_attention(qkv, lam, subln_g, lambda_init, *, nb, l_len, t_len, ctx_len):
    d = D_MODEL
    nq = l_len // ATT_TQ
    q_off = ctx_len // ATT_TQ
    tpb_t = t_len // ATT_TQ
    kern = functools.partial(_attn_kernel, out_scale=1.0 - lambda_init)
    stat = pltpu.VMEM((ATT_TQ, 1), F32)
    acc = pltpu.VMEM((ATT_TQ, 2 * DIFF_DH), F32)
    return pl.pallas_call(
        kern,
        grid_spec=pltpu.PrefetchScalarGridSpec(
            num_scalar_prefetch=1,
            grid=(nb, DIFF_HEADS, nq),
            in_specs=[
                pl.BlockSpec((ATT_TQ, LANES), lambda b, h, i, lam: (b * tpb_t + q_off + i, h)),
                pl.BlockSpec((t_len, LANES), lambda b, h, i, lam: (b, DIFF_HEADS + h)),
                pl.BlockSpec((t_len, LANES), lambda b, h, i, lam: (b, 2 * DIFF_HEADS + h)),
                pl.BlockSpec((1, LANES), lambda b, h, i, lam: (0, 0)),
            ],
            out_specs=pl.BlockSpec((ATT_TQ, LANES), lambda b, h, i, lam: (b * nq + i, h)),
            scratch_shapes=[acc, acc, stat, stat, stat, stat],
        ),
        out_shape=jax.ShapeDtypeStruct((nb * l_len, d), BF16),
        compiler_params=_cparams(("parallel", "parallel", "arbitrary")),
        name="diff_attention",
    )(lam, qkv, qkv, qkv, subln_g.reshape(1, LANES).astype(F32))


def _ret_rope_tables(l_len, ctx_len):
    half = RET_DK // 2
    inv = ROPE_BASE ** (-jnp.arange(0, RET_DK, 2, dtype=F32) / RET_DK)
    ang = jnp.arange(l_len, dtype=F32)[:, None] * inv[None, :]
    ang = jnp.concatenate([jnp.zeros((ctx_len, half), F32), ang], axis=0)
    cos64 = jnp.concatenate([jnp.cos(ang), jnp.cos(ang)], axis=1)
    sin64 = jnp.concatenate([-jnp.sin(ang), jnp.sin(ang)], axis=1)
    kscale = RET_DK ** -0.5
    cos = jnp.concatenate([cos64, cos64 * kscale], axis=1)
    sin = jnp.concatenate([sin64, sin64 * kscale], axis=1)
    return cos[None], sin[None]


def _attn_rope_tables(l_len, ctx_len):
    quarter = DIFF_DH // 4
    inv = ROPE_BASE ** (-jnp.arange(0, DIFF_DH // 2, 2, dtype=F32) / (DIFF_DH // 2))
    pos = jnp.arange(l_len)
    ang_r = (pos // GRID_W).astype(F32)[:, None] * inv[None, :]
    ang_c = (pos % GRID_W).astype(F32)[:, None] * inv[None, :]
    pad = lambda a: jnp.concatenate([jnp.zeros((ctx_len, quarter), F32), a], axis=0)
    ang_r, ang_c = pad(ang_r), pad(ang_c)
    cos64 = jnp.concatenate([jnp.cos(ang_r)] * 2 + [jnp.cos(ang_c)] * 2, axis=1)
    sin64 = jnp.concatenate([-jnp.sin(ang_r), jnp.sin(ang_r), -jnp.sin(ang_c), jnp.sin(ang_c)], axis=1)
    cos = jnp.concatenate([cos64, cos64], axis=1)
    sin = jnp.concatenate([sin64, sin64], axis=1)
    qscale = DIFF_DH ** -0.5
    return jnp.stack([cos * qscale, cos]), jnp.stack([sin * qscale, sin])


def kernel(x, c, ctx, c_ctx, ada_w, ada_b, ln_g, w_in_ab, ret_decay_logit, s5_lam_re, s5_lam_im,
           s5_log_dt, s5_b_re, s5_b_im, s5_c_re, s5_c_im, s5_d, s5_w_glu, w_out_ab, w_in_c,
           diff_lambda, diff_subln_g, w_out_c, router_w, router_bias, exp_w_gate, exp_w_up,
           exp_w_down):
    nb, l_len, d = x.shape
    ctx_len = ctx.shape[1]
    t_len = ctx_len + l_len
    nt = nb * t_len
    tpb = t_len // ROW_TILE
    ctx_tiles = ctx_len // ROW_TILE
    assert d == D_MODEL and nb < SUBLANES
    assert l_len % ROW_TILE == 0 and ctx_len % ROW_TILE == 0 and t_len % ATT_TK == 0

    xt = jnp.concatenate([ctx, x], axis=1).reshape(nt, d)
    c_all = jnp.concatenate([c, c_ctx[None].astype(c.dtype)], axis=0)
    c_pad = jnp.zeros((SUBLANES, d), F32).at[:nb + 1].set(c_all)
    mod = _adaln(c_pad, ada_w, ada_b)

    wr_t = router_w.T
    rbias = router_bias.reshape(N_EXPERTS, 1).astype(F32)

    w0 = w_in_ab[0]
    q_w, k_w, v_w, g_w, u_w = jnp.split(w0, (RET_QK_W, 2 * RET_QK_W, 2 * RET_QK_W + RET_V_W,
                                             2 * RET_QK_W + 2 * RET_V_W), axis=1)
    qk_w = jnp.concatenate([q_w.reshape(d, RET_HEADS, RET_DK), k_w.reshape(d, RET_HEADS, RET_DK)],
                           axis=2).reshape(d, 2 * RET_QK_W)
    w0p = jnp.concatenate([qk_w, v_w, g_w, u_w], axis=1).astype(BF16)
    cos0, sin0 = _ret_rope_tables(l_len, ctx_len)
    rope_tab0 = [0] * RET_HEADS + [None] * ((w0p.shape[1] - 2 * RET_QK_W) // LANES)
    proj0 = _inproj(xt, mod, 0, w0p, cos0, sin0, rope_tab0, RET_DK // 2,
                    nb=nb, t_len=t_len, ctx_len=ctx_len)

    log_gammas = jax.nn.log_sigmoid(ret_decay_logit[0].astype(F32))
    ret = _retention(proj0, log_gammas, nb=nb, t_len=t_len, ctx_len=ctx_len)

    tc = S5_CHUNK
    nc5 = t_len // tc
    u = proj0[:, 2 * RET_QK_W + 2 * RET_V_W:]
    xg = u.reshape(nb, nc5, tc, S5_G, S5_P).transpose(3, 1, 0, 2, 4)
    xg = jnp.pad(xg, ((0, 0), (0, 0), (0, SUBLANES - nb), (0, 0), (0, 0)))
    xg = xg.reshape(S5_G, nc5 * SUBLANES, tc * S5_P)
    w1, w2, a5 = _s5_operators(s5_lam_re[0], s5_lam_im[0], s5_log_dt[0], s5_b_re[0], s5_b_im[0],
                               s5_c_re[0], s5_c_im[0], s5_d[0])
    yg = _s5(xg, w1, w2, a5, nb=SUBLANES, nctx=ctx_len // tc)
    s5y = yg.reshape(S5_G, nc5, SUBLANES, tc, S5_P)[:, :, :nb].transpose(2, 1, 3, 0, 4).reshape(nt, S5_CH)

    x1, h2, eidx, gates = _merge0(ret, proj0, s5y, xt, mod, s5_w_glu[0].astype(BF16),
                                  w_out_ab[0].astype(BF16), ln_g[0, 0].reshape(1, d), wr_t, rbias,
                                  nb=nb, t_len=t_len, ctx_len=ctx_len)
    x2 = _moe_layer(x1, h2, eidx, gates, mod, 0, ln_g[0, 1].reshape(1, d),
                    exp_w_gate[0].astype(BF16), exp_w_up[0].astype(BF16), exp_w_down[0].astype(BF16),
                    tiles_per_batch=tpb, ctx_tiles=ctx_tiles, nb=nb)

    cos1, sin1 = _attn_rope_tables(l_len, ctx_len)
    n_heads_cols = D_MODEL // LANES
    rope_tab1 = [0] * n_heads_cols + [1] * n_heads_cols + [None] * n_heads_cols
    qkv = _inproj(x2, mod, 1, w_in_c[0].astype(BF16), cos1, sin1, rope_tab1, DIFF_DH // 4,
                  nb=nb, t_len=t_len, ctx_len=ctx_len)
    lf = diff_lambda[0].astype(F32)
    lambda_init = 0.8 - 0.6 * math.exp(-0.3 * 1)
    lam = (jnp.exp(jnp.sum(lf[0] * lf[1])) - jnp.exp(jnp.sum(lf[2] * lf[3])) + lambda_init).reshape(1)
    att = _diff_attention(qkv, lam, diff_subln_g[0], lambda_init,
                          nb=nb, l_len=l_len, t_len=t_len, ctx_len=ctx_len)
    x3, h3, eidx1, gates1 = _merge1(att, x2, mod, 1, w_out_c[0].astype(BF16), ln_g[1, 0].reshape(1, d),
                                    wr_t, rbias, nb=nb, l_len=l_len, t_len=t_len, ctx_len=ctx_len)
    out = _moe_layer(x3, h3, eidx1, gates1, mod, 1, ln_g[1, 1].reshape(1, d),
                     exp_w_gate[1].astype(BF16), exp_w_up[1].astype(BF16), exp_w_down[1].astype(BF16),
                     tiles_per_batch=l_len // ROW_TILE, ctx_tiles=0, nb=nb)
    return out.reshape(nb, l_len, d)
```

```python
import functools
import math

import jax
import jax.numpy as jnp
import numpy as np
from jax import lax
from jax.experimental import pallas as pl
from jax.experimental.pallas import tpu as pltpu

F32 = jnp.float32
BF16 = jnp.bfloat16

D_MODEL = 1024
DEPTH = 2
GRID_W = 64
ALPHA = (2.0 * DEPTH) ** 0.25
LN_EPS = 1e-5
GN_EPS = 1e-6
ROPE_BASE = 10000.0
RET_DK = 64
RET_DV = 128
RET_HEADS = 6
RET_QK_W = RET_HEADS * RET_DK
RET_V_W = RET_HEADS * RET_DV
S5_CH = 256
S5_P = 16
S5_G = 16
S5_N = 64
DIFF_HEADS = 8
DIFF_DH = 64
N_EXPERTS = 16
EXPERTS_PER_GROUP = 4
TOP_K = 2

LANES = 128
SUBLANES = 8
MXU_DIM = 256
ROW_TILE = 256
RET_CHUNK = 256
S5_CHUNK = 32
MOE_ROWS = 256
ATT_TK = 768
VMEM_LIMIT = 48 * 1024 * 1024


def _cparams(sem):
    return pltpu.CompilerParams(dimension_semantics=sem, vmem_limit_bytes=VMEM_LIMIT)


def _dot(a, b):
    return jnp.dot(a, b, preferred_element_type=F32)


def _dot_nt(a, b):
    return lax.dot_general(a, b, (((1,), (1,)), ((), ())), preferred_element_type=F32)


def _dot_tn(a, b):
    return lax.dot_general(a, b, (((0,), (0,)), ((), ())), preferred_element_type=F32)


def _split_bf16(x):
    hi = x.astype(BF16)
    lo = (x - hi.astype(F32)).astype(BF16)
    return hi, lo


def _dot3(a, b):
    ah, al = _split_bf16(a)
    bh, bl = _split_bf16(b)
    return _dot(ah, bh) + _dot(ah, bl) + _dot(al, bh)


def _dot3_nt(a, b):
    ah, al = _split_bf16(a)
    bh, bl = _split_bf16(b)
    return _dot_nt(ah, bh) + _dot_nt(ah, bl) + _dot_nt(al, bh)


def _sigmoid(x):
    return 1.0 / (1.0 + jnp.exp(-x))


def _silu(x):
    return x * _sigmoid(x)


def _gelu_tanh(x):
    c = math.sqrt(2.0 / math.pi)
    return 0.5 * x * (1.0 + jnp.tanh(c * (x + 0.044715 * (x * x * x))))


def _adaln_kernel(c_ref, w_ref, b_ref, o_ref):
    c = c_ref[...]
    o_ref[0] = _dot3(_silu(c), w_ref[0]) + b_ref[0]


def _adaln(c_pad, ada_w, ada_b):
    depth, d, n = ada_w.shape
    tn = 1536
    return pl.pallas_call(
        _adaln_kernel,
        grid=(depth, n // tn),
        in_specs=[
            pl.BlockSpec((SUBLANES, d), lambda i, j: (0, 0)),
            pl.BlockSpec((1, d, tn), lambda i, j: (i, 0, j)),
            pl.BlockSpec((1, 1, tn), lambda i, j: (i, 0, j)),
        ],
        out_specs=pl.BlockSpec((1, SUBLANES, tn), lambda i, j: (i, 0, j)),
        out_shape=jax.ShapeDtypeStruct((depth, SUBLANES, n), F32),
        compiler_params=_cparams(("parallel", "parallel")),
        name="adaln",
    )(c_pad, ada_w, ada_b.reshape(depth, 1, n))


def _mod_row(t, tiles_per_batch, ctx_tiles, nb):
    b = lax.div(t, tiles_per_batch)
    w = lax.rem(t, tiles_per_batch)
    return jnp.where(w < ctx_tiles, nb, b)


def _rope_block(a, cos, sin, half):
    lane = lax.broadcasted_iota(jnp.int32, a.shape, 1)
    first = lax.rem(lane, 2 * half) < half
    rot = jnp.where(first, pltpu.roll(a, LANES - half, 1), pltpu.roll(a, half, 1))
    return a * cos + rot * sin


def _inproj_kernel(x_ref, mod_ref, w_ref, cos_ref, sin_ref, o_ref, *, tiles_per_batch,
                   ctx_tiles, nb, rope_tab, rope_half):
    d = x_ref.shape[1]
    n = w_ref.shape[1]
    r = _mod_row(pl.program_id(0), tiles_per_batch, ctx_tiles, nb)
    sh = mod_ref[0, pl.ds(r, 1), 0:d]
    sc = mod_ref[0, pl.ds(r, 1), d:2 * d]
    xm = (x_ref[...] * (1.0 + sc) + sh).astype(BF16)
    for j in range(n // MXU_DIM):
        acc = _dot(xm, w_ref[:, j * MXU_DIM:(j + 1) * MXU_DIM])
        parts = []
        for s in range(MXU_DIM // LANES):
            blk = acc[:, s * LANES:(s + 1) * LANES]
            tab = rope_tab[j * (MXU_DIM // LANES) + s]
            if tab is not None:
                blk = _rope_block(blk, cos_ref[tab], sin_ref[tab], rope_half)
            parts.append(blk)
        o_ref[:, j * MXU_DIM:(j + 1) * MXU_DIM] = jnp.concatenate(parts, axis=1).astype(BF16)


def _inproj(x, mod, layer, w, cos, sin, rope_tab, rope_half, *, nb, t_len, ctx_len):
    nt, d = x.shape
    n = w.shape[1]
    tpb = t_len // ROW_TILE
    kern = functools.partial(_inproj_kernel, tiles_per_batch=tpb, ctx_tiles=ctx_len // ROW_TILE,
                             nb=nb, rope_tab=tuple(rope_tab), rope_half=rope_half)
    ntab = cos.shape[0]
    return pl.pallas_call(
        kern,
        grid=(nt // ROW_TILE,),
        in_specs=[
            pl.BlockSpec((ROW_TILE, d), lambda t: (t, 0)),
            pl.BlockSpec((1, SUBLANES, mod.shape[2]), lambda t: (layer, 0, 0)),
            pl.BlockSpec((d, n), lambda t: (0, 0)),
            pl.BlockSpec((ntab, ROW_TILE, LANES), lambda t: (0, lax.rem(t, tpb), 0)),
            pl.BlockSpec((ntab, ROW_TILE, LANES), lambda t: (0, lax.rem(t, tpb), 0)),
        ],
        out_specs=pl.BlockSpec((ROW_TILE, n), lambda t: (t, 0)),
        out_shape=jax.ShapeDtypeStruct((nt, n), BF16),
        compiler_params=_cparams(("parallel",)),
        name="inproj%d" % layer,
    )(x, mod, w, cos, sin)


def _retention_kernel(lg_ref, qk_ref, v_ref, o_ref, sf_ref, sb_ref, *, nctx):
    c_len = RET_CHUNK
    t_len = qk_ref.shape[0]
    nc = t_len // c_len
    h = pl.program_id(1)
    lgf = lg_ref[0, h]
    lgb = lg_ref[1, h]
    ii = lax.broadcasted_iota(jnp.int32, (c_len, 1), 0).astype(F32)
    jj = lax.broadcasted_iota(jnp.int32, (1, c_len), 1).astype(F32)
    diff = ii - jj
    decay = jnp.where(diff >= 0.0, jnp.exp(lgf * jnp.maximum(diff, 0.0)),
                      jnp.exp(lgb * jnp.maximum(-diff, 0.0)))
    kdf = jnp.exp(lgf * (c_len - 1.0 - ii))
    kdb = jnp.exp(lgb * ii)
    qdf = jnp.exp(lgf * (ii + 1.0))
    qdb = jnp.exp(lgb * (c_len - ii))
    zrow = jnp.zeros((1, RET_DV), F32)
    gf_chunk = jnp.exp(zrow + lgf * c_len)
    gb_chunk = jnp.exp(zrow + lgb * c_len)

    def load(c):
        rows = pl.ds(pl.multiple_of(c * c_len, c_len), c_len)
        qk = qk_ref[rows, :].astype(F32)
        return qk[:, :RET_DK], qk[:, RET_DK:], v_ref[rows, :]

    def fwd_state(c, s):
        sf_ref[c] = s
        _, k, v = load(c)
        return gf_chunk * s + _dot_tn((k * kdf).astype(BF16), v)

    lax.fori_loop(0, nc, fwd_state, jnp.zeros((RET_DK, RET_DV), F32))

    def bwd_state(j, s):
        c = jnp.where(j < nctx, nctx - 1 - j, nc - 1 - (j - nctx))
        sb_ref[c] = s
        _, k, v = load(c)
        return gb_chunk * s + _dot_tn((k * kdb).astype(BF16), v)

    lax.fori_loop(0, nc, bwd_state, jnp.zeros((RET_DK, RET_DV), F32))

    def out_chunk(c, carry):
        q, k, v = load(c)
        scores = _dot_nt(q.astype(BF16), k.astype(BF16)) * decay
        o = _dot(scores.astype(BF16), v)
        o = o + _dot((q * qdf).astype(BF16), sf_ref[c].astype(BF16))
        o = o + _dot((q * qdb).astype(BF16), sb_ref[c].astype(BF16))
        mu = jnp.mean(o, axis=-1, keepdims=True)
        oc = o - mu
        var = jnp.mean(oc * oc, axis=-1, keepdims=True)
        rows = pl.ds(pl.multiple_of(c * c_len, c_len), c_len)
        o_ref[rows, :] = (oc * lax.rsqrt(var + GN_EPS)).astype(BF16)
        return carry

    lax.fori_loop(0, nc, out_chunk, 0)


def _retention(proj, log_gammas, *, nb, t_len, ctx_len):
    nt = proj.shape[0]
    nc = t_len // RET_CHUNK
    kern = functools.partial(_retention_kernel, nctx=ctx_len // RET_CHUNK)
    vcol0 = RET_HEADS
    return pl.pallas_call(
        kern,
        grid_spec=pltpu.PrefetchScalarGridSpec(
            num_scalar_prefetch=1,
            grid=(nb, RET_HEADS),
            in_specs=[
                pl.BlockSpec((t_len, LANES), lambda b, h, lg: (b, h)),
                pl.BlockSpec((t_len, LANES), lambda b, h, lg: (b, vcol0 + h)),
            ],
            out_specs=pl.BlockSpec((t_len, LANES), lambda b, h, lg: (b, h)),
            scratch_shapes=[pltpu.VMEM((nc, RET_DK, RET_DV), F32),
                            pltpu.VMEM((nc, RET_DK, RET_DV), F32)],
        ),
        out_shape=jax.ShapeDtypeStruct((nt, RET_V_W), BF16),
        compiler_params=_cparams(("parallel", "parallel")),
        name="retention",
    )(log_gammas, proj, proj)


def _s5_operators(lam_re, lam_im, log_dt, b_re, b_im, c_re, c_im, d_skip):
    tc = S5_CHUNK
    hp = lax.Precision.HIGHEST
    ks = jnp.arange(tc + 1, dtype=F32)
    pw, bbar, cm = [], [], []
    for direction in range(2):
        dt = jnp.exp(log_dt[direction].astype(F32))[:, None]
        lam = lax.complex(lam_re[direction].astype(F32), lam_im[direction].astype(F32))
        z = lam * dt
        p = jnp.exp(z[None] * ks[:, None, None])
        lam_bar = p[1]
        bb = ((lam_bar - 1.0) / lam)[..., None] * lax.complex(
            b_re[direction].astype(F32), b_im[direction].astype(F32))
        pw.append(p)
        bbar.append(bb)
        cm.append(lax.complex(c_re[direction].astype(F32), c_im[direction].astype(F32)))

    def lag_kernel(p, bb, c):
        return jnp.einsum('gpn,kgn,gnq->kgpq', c, p[:tc], bb, precision=hp).real

    kf = lag_kernel(pw[0], bbar[0], cm[0])
    kb = lag_kernel(pw[1], bbar[1], cm[1])
    k0 = kf[0] + kb[0] + jnp.eye(S5_P, dtype=F32)[None] * d_skip.astype(F32)[:, :, None]
    kcat = jnp.concatenate([kb[1:][::-1], k0[None], kf[1:]], axis=0)
    s_idx = jnp.arange(tc)[:, None]
    t_idx = jnp.arange(tc)[None, :]
    m5 = kcat[t_idx - s_idx + tc - 1]
    intra = m5.transpose(2, 0, 4, 1, 3).reshape(S5_G, tc * S5_P, tc * S5_P)

    ef = pw[0][:tc][::-1][:, :, :, None] * bbar[0][None]
    eb = pw[1][:tc][:, :, :, None] * bbar[1][None]
    to_in = lambda e: e.transpose(1, 0, 3, 2).reshape(S5_G, tc * S5_P, S5_N)
    w1 = jnp.concatenate([intra, to_in(ef.real), to_in(ef.imag), to_in(eb.real), to_in(eb.imag)],
                         axis=2)
    of = cm[0][None] * pw[0][1:][:, :, None, :]
    ob = cm[1][None] * pw[1][1:][::-1][:, :, None, :]
    to_out = lambda o: o.transpose(1, 3, 0, 2).reshape(S5_G, S5_N, tc * S5_P)
    w2 = jnp.concatenate([to_out(of.real), -to_out(of.imag), to_out(ob.real), -to_out(ob.imag)],
                         axis=1)
    a = jnp.stack([pw[0][tc].real, pw[0][tc].imag, pw[1][tc].real, pw[1][tc].imag], axis=1)
    return w1.astype(BF16), w2.astype(BF16), a


def _s5_kernel(x_ref, w1_ref, w2_ref, a_ref, y_ref, e_ref, st_ref, *, nb, nctx):
    n = S5_N
    w = x_ref.shape[2]
    rows = x_ref.shape[1]
    nc = rows // nb
    x = x_ref[0]
    e_ref[...] = _dot(x, w1_ref[0, :, w:])
    afr = a_ref[0, 0:1, :]
    afi = a_ref[0, 1:2, :]
    abr = a_ref[0, 2:3, :]
    abi = a_ref[0, 3:4, :]

    def step(j, carry):
        fr, fi, br, bi = carry
        rf = pl.ds(pl.multiple_of(j * nb, nb), nb)
        cb = jnp.where(j < nctx, nctx - 1 - j, nc - 1 - (j - nctx))
        rb = pl.ds(pl.multiple_of(cb * nb, nb), nb)
        st_ref[rf, 0:n] = fr
        st_ref[rf, n:2 * n] = fi
        st_ref[rb, 2 * n:3 * n] = br
        st_ref[rb, 3 * n:4 * n] = bi
        nfr = afr * fr - afi * fi + e_ref[rf, 0:n]
        nfi = afr * fi + afi * fr + e_ref[rf, n:2 * n]
        nbr = abr * br - abi * bi + e_ref[rb, 2 * n:3 * n]
        nbi = abr * bi + abi * br + e_ref[rb, 3 * n:4 * n]
        return nfr, nfi, nbr, nbi

    z = jnp.zeros((nb, n), F32)
    lax.fori_loop(0, nc, step, (z, z, z, z))
    y_ref[0] = _dot(x, w1_ref[0, :, :w]) + _dot(st_ref[...].astype(BF16), w2_ref[0])


def _s5(xg, w1, w2, a, *, nb, nctx):
    g, rows, w = xg.shape
    kern = functools.partial(_s5_kernel, nb=nb, nctx=nctx)
    return pl.pallas_call(
        kern,
        grid=(g,),
        in_specs=[
            pl.BlockSpec((1, rows, w), lambda i: (i, 0, 0)),
            pl.BlockSpec((1, w, w + 4 * S5_N), lambda i: (i, 0, 0)),
            pl.BlockSpec((1, 4 * S5_N, w), lambda i: (i, 0, 0)),
            pl.BlockSpec((1, 4, S5_N), lambda i: (i, 0, 0)),
        ],
        out_specs=pl.BlockSpec((1, rows, w), lambda i: (i, 0, 0)),
        out_shape=jax.ShapeDtypeStruct((g, rows, w), F32),
        scratch_shapes=[pltpu.VMEM((rows, 4 * S5_N), F32), pltpu.VMEM((rows, 4 * S5_N), F32)],
        compiler_params=_cparams(("parallel",)),
        name="s5",
    )(xg, w1, w2, a)


def _route(logits_t, bias):
    scores = _sigmoid(logits_t)
    biased = scores + bias
    s_rows = [scores[e:e + 1, :] for e in range(N_EXPERTS)]
    b_rows = [biased[e:e + 1, :] for e in range(N_EXPERTS)]
    n_groups = N_EXPERTS // EXPERTS_PER_GROUP
    best = None
    sel = None
    for g in range(n_groups):
        a, b, c, d = b_rows[4 * g:4 * g + 4]
        hi1, lo1 = jnp.maximum(a, b), jnp.minimum(a, b)
        hi2, lo2 = jnp.maximum(c, d), jnp.minimum(c, d)
        top1 = jnp.maximum(hi1, hi2)
        top2 = jnp.maximum(jnp.minimum(hi1, hi2), jnp.maximum(lo1, lo2))
        gs = top1 + top2
        if g == 0:
            best, sel = gs, jnp.zeros(gs.shape, jnp.int32)
        else:
            better = gs > best
            sel = jnp.where(better, g, sel)
            best = jnp.where(better, gs, best)
    neg = jnp.full(best.shape, -jnp.inf, F32)
    masked = [jnp.where(sel == (e // EXPERTS_PER_GROUP), b_rows[e], neg) for e in range(N_EXPERTS)]
    v1, i1, g1 = masked[0], jnp.zeros(best.shape, jnp.int32), s_rows[0]
    for e in range(1, N_EXPERTS):
        better = masked[e] > v1
        v1 = jnp.where(better, masked[e], v1)
        i1 = jnp.where(better, e, i1)
        g1 = jnp.where(better, s_rows[e], g1)
    v2, i2, g2 = neg, jnp.zeros(best.shape, jnp.int32), jnp.zeros(best.shape, F32)
    for e in range(N_EXPERTS):
        cand = jnp.where(i1 == e, neg, masked[e])
        better = cand > v2
        v2 = jnp.where(better, cand, v2)
        i2 = jnp.where(better, e, i2)
        g2 = jnp.where(better, s_rows[e], g2)
    tot = g1 + g2
    return jnp.concatenate([i1, i2], axis=0), jnp.concatenate([g1 / tot, g2 / tot], axis=0)


def _tail(x, o, mod_ref, r, lng, wr, rb, x1_ref, h2_ref, ei_ref, gt_ref):
    d = x.shape[1]
    g1 = mod_ref[0, pl.ds(r, 1), 2 * d:3 * d]
    sh2 = mod_ref[0, pl.ds(r, 1), 3 * d:4 * d]
    sc2 = mod_ref[0, pl.ds(r, 1), 4 * d:5 * d]
    y = ALPHA * x + g1 * o
    mu = jnp.mean(y, axis=-1, keepdims=True)
    yc = y - mu
    var = jnp.mean(yc * yc, axis=-1, keepdims=True)
    x1 = yc * lax.rsqrt(var + LN_EPS) * lng
    h2 = x1 * (1.0 + sc2) + sh2
    x1_ref[...] = x1
    h2_ref[...] = h2
    ei, gt = _route(_dot3_nt(wr, h2), rb)
    ei_ref[...] = ei
    gt_ref[...] = gt


def _merge0_kernel(r_ref, g_ref, s_ref, x_ref, mod_ref, wglu_ref, wout_ref, lng_ref, wr_ref,
                   rb_ref, x1_ref, h2_ref, ei_ref, gt_ref, *, tiles_per_batch, ctx_tiles, nb):
    r = _mod_row(pl.program_id(0), tiles_per_batch, ctx_tiles, nb)
    ret = r_ref[...].astype(F32) * _silu(g_ref[...].astype(F32))
    z = _dot(_gelu_tanh(s_ref[...]).astype(BF16), wglu_ref[...])
    zz = z[:, :S5_CH] * _sigmoid(z[:, S5_CH:])
    o = _dot(ret.astype(BF16), wout_ref[0:RET_V_W, :]) + _dot(zz.astype(BF16), wout_ref[RET_V_W:, :])
    _tail(x_ref[...], o, mod_ref, r, lng_ref[...], wr_ref[...], rb_ref[...],
          x1_ref, h2_ref, ei_ref, gt_ref)


def _merge1_kernel(a_ref, x_ref, mod_ref, wout_ref, lng_ref, wr_ref, rb_ref,
                   x1_ref, h2_ref, ei_ref, gt_ref, *, tiles_per_batch):
    r = lax.div(pl.program_id(0), tiles_per_batch)
    o = _dot(a_ref[...], wout_ref[...])
    _tail(x_ref[...], o, mod_ref, r, lng_ref[...], wr_ref[...], rb_ref[...],
          x1_ref, h2_ref, ei_ref, gt_ref)


def _tail_outs(n_rows, d):
    shapes = (jax.ShapeDtypeStruct((n_rows, d), F32), jax.ShapeDtypeStruct((n_rows, d), F32),
              jax.ShapeDtypeStruct((TOP_K, n_rows), jnp.int32),
              jax.ShapeDtypeStruct((TOP_K, n_rows), F32))
    specs = (pl.BlockSpec((ROW_TILE, d), lambda t: (t, 0)), pl.BlockSpec((ROW_TILE, d), lambda t: (t, 0)),
             pl.BlockSpec((TOP_K, ROW_TILE), lambda t: (0, t)),
             pl.BlockSpec((TOP_K, ROW_TILE), lambda t: (0, t)))
    return shapes, specs


def _merge0(ret, proj, s5y, x, mod, w_glu, w_out, lng, wr_t, rbias, *, nb, t_len, ctx_len):
    nt, d = x.shape
    tpb = t_len // ROW_TILE
    kern = functools.partial(_merge0_kernel, tiles_per_batch=tpb, ctx_tiles=ctx_len // ROW_TILE, nb=nb)
    shapes, specs = _tail_outs(nt, d)
    gcol = (2 * RET_QK_W + RET_V_W) // RET_V_W
    full = lambda a: pl.BlockSpec(a.shape, lambda t: (0,) * a.ndim)
    return pl.pallas_call(
        kern,
        grid=(nt // ROW_TILE,),
        in_specs=[
            pl.BlockSpec((ROW_TILE, RET_V_W), lambda t: (t, 0)),
            pl.BlockSpec((ROW_TILE, RET_V_W), lambda t: (t, gcol)),
            pl.BlockSpec((ROW_TILE, S5_CH), lambda t: (t, 0)),
            pl.BlockSpec((ROW_TILE, d), lambda t: (t, 0)),
            pl.BlockSpec((1, SUBLANES, mod.shape[2]), lambda t: (0, 0, 0)),
            full(w_glu), full(w_out), full(lng), full(wr_t), full(rbias),
        ],
        out_specs=specs,
        out_shape=shapes,
        compiler_params=_cparams(("parallel",)),
        name="merge0",
    )(ret, proj, s5y, x, mod, w_glu, w_out, lng, wr_t, rbias)


def _merge1(att, x, mod, layer, w_out, lng, wr_t, rbias, *, nb, l_len, t_len, ctx_len):
    n_lat, d = att.shape
    tpb = l_len // ROW_TILE
    tpb_t = t_len // ROW_TILE
    ctx_tiles = ctx_len // ROW_TILE
    kern = functools.partial(_merge1_kernel, tiles_per_batch=tpb)
    shapes, specs = _tail_outs(n_lat, d)
    full = lambda a: pl.BlockSpec(a.shape, lambda t: (0,) * a.ndim)
    xrow = lambda t: (lax.div(t, tpb) * tpb_t + ctx_tiles + lax.rem(t, tpb), 0)
    return pl.pallas_call(
        kern,
        grid=(n_lat // ROW_TILE,),
        in_specs=[
            pl.BlockSpec((ROW_TILE, d), lambda t: (t, 0)),
            pl.BlockSpec((ROW_TILE, d), xrow),
            pl.BlockSpec((1, SUBLANES, mod.shape[2]), lambda t: (layer, 0, 0)),
            full(w_out), full(lng), full(wr_t), full(rbias),
        ],
        out_specs=specs,
        out_shape=shapes,
        compiler_params=_cparams(("parallel",)),
        name="merge1",
    )(att, x, mod, w_out, lng, wr_t, rbias)


def _moe_plan(eidx):
    k, n = eidx.shape
    a = k * n
    e_flat = eidx.reshape(a)
    onehot = (e_flat[:, None] == jnp.arange(N_EXPERTS, dtype=jnp.int32)[None, :]).astype(jnp.int32)
    csum = jnp.cumsum(onehot, axis=0)
    counts = csum[-1]
    rank = jnp.take_along_axis(csum, e_flat[:, None], axis=1)[:, 0] - 1
    padded = (counts + MOE_ROWS - 1) // MOE_ROWS * MOE_ROWS
    pad_end = jnp.cumsum(padded)
    pad_start = pad_end - padded
    dest = (pad_start[e_flat] + rank).astype(jnp.int32)
    n_blocks = -(-(a + N_EXPERTS * (MOE_ROWS - 1)) // MOE_ROWS)
    tok = jnp.tile(jnp.arange(n, dtype=jnp.int32), k)
    src = jnp.zeros((n_blocks * MOE_ROWS,), jnp.int32).at[dest].set(tok)
    block_expert = jnp.minimum(
        jnp.searchsorted(pad_end, jnp.arange(n_blocks, dtype=jnp.int32) * MOE_ROWS, side='right'),
        N_EXPERTS - 1).astype(jnp.int32)
    n_used = (pad_end[-1] // MOE_ROWS).astype(jnp.int32).reshape(1)
    return src.reshape(n_blocks, 1, MOE_ROWS), dest.reshape(k, n), block_expert, n_used


def _row_copy(src_hbm, row, dst_vmem, slot, sem):
    return pltpu.make_async_copy(src_hbm.at[pl.ds(row, 1)], dst_vmem.at[pl.ds(slot, 1)], sem)


def _experts_kernel(be_ref, nu_ref, src_ref, h_hbm, wg_ref, wu_ref, wd_ref, o_ref, xbuf, sem):
    i = pl.program_id(0)
    rows = xbuf.shape[0]

    @pl.when(i < nu_ref[0])
    def _():
        def issue(r, c):
            _row_copy(h_hbm, src_ref[0, 0, r], xbuf, r, sem).start()
            return c

        lax.fori_loop(0, rows, issue, 0)

        def wait(r, c):
            _row_copy(h_hbm, 0, xbuf, r, sem).wait()
            return c

        lax.fori_loop(0, rows, wait, 0)
        x = xbuf[...].astype(BF16)
        hg = _dot(x, wg_ref[0])
        hu = _dot(x, wu_ref[0])
        o_ref[...] = _dot((_silu(hg) * hu).astype(BF16), wd_ref[0])

    @pl.when(i >= nu_ref[0])
    def _():
        o_ref[...] = jnp.zeros(o_ref.shape, o_ref.dtype)


def _experts(h, src, block_expert, n_used, wg, wu, wd):
    n_blocks = src.shape[0]
    d = h.shape[1]
    dff = wg.shape[2]
    return pl.pallas_call(
        _experts_kernel,
        grid_spec=pltpu.PrefetchScalarGridSpec(
            num_scalar_prefetch=2,
            grid=(n_blocks,),
            in_specs=[
                pl.BlockSpec((1, 1, MOE_ROWS), lambda i, be, nu: (i, 0, 0), memory_space=pltpu.SMEM),
                pl.BlockSpec(memory_space=pl.ANY),
                pl.BlockSpec((1, d, dff), lambda i, be, nu: (be[i], 0, 0)),
                pl.BlockSpec((1, d, dff), lambda i, be, nu: (be[i], 0, 0)),
                pl.BlockSpec((1, dff, d), lambda i, be, nu: (be[i], 0, 0)),
            ],
            out_specs=pl.BlockSpec((MOE_ROWS, d), lambda i, be, nu: (i, 0)),
            scratch_shapes=[pltpu.VMEM((MOE_ROWS, d), F32), pltpu.SemaphoreType.DMA],
        ),
        out_shape=jax.ShapeDtypeStruct((n_blocks * MOE_ROWS, d), F32),
        compiler_params=_cparams(("arbitrary",)),
        name="moe_experts",
    )(block_expert, n_used, src, h, wg, wu, wd)


def _combine_kernel(dest_ref, x_ref, gt_ref, mod_ref, lng_ref, y_hbm, o_ref, ybuf, sem, *,
                    tiles_per_batch, ctx_tiles, nb):
    rows = x_ref.shape[0]
    d = x_ref.shape[1]
    r = _mod_row(pl.program_id(0), tiles_per_batch, ctx_tiles, nb)

    def issue(j, c):
        _row_copy(y_hbm, dest_ref[0, 0, j], ybuf, j, sem).start()
        return c

    lax.fori_loop(0, TOP_K * rows, issue, 0)

    def wait(j, c):
        _row_copy(y_hbm, 0, ybuf, j, sem).wait()
        return c

    lax.fori_loop(0, TOP_K * rows, wait, 0)
    gt = gt_ref[...]
    y = ybuf[0:rows, :] * gt[:, 0:1] + ybuf[rows:2 * rows, :] * gt[:, 1:2]
    g2 = mod_ref[0, pl.ds(r, 1), 5 * d:6 * d]
    z = ALPHA * x_ref[...] + g2 * y
    mu = jnp.mean(z, axis=-1, keepdims=True)
    zc = z - mu
    var = jnp.mean(zc * zc, axis=-1, keepdims=True)
    o_ref[...] = zc * lax.rsqrt(var + LN_EPS) * lng_ref[...]


def _combine(x1, dest, gates, mod, layer, lng, yexp, *, tiles_per_batch, ctx_tiles, nb):
    n, d = x1.shape
    nt = n // ROW_TILE
    dest_t = dest.reshape(TOP_K, nt, ROW_TILE).transpose(1, 0, 2).reshape(nt, 1, TOP_K * ROW_TILE)
    kern = functools.partial(_combine_kernel, tiles_per_batch=tiles_per_batch, ctx_tiles=ctx_tiles, nb=nb)
    return pl.pallas_call(
        kern,
        grid=(nt,),
        in_specs=[
            pl.BlockSpec((1, 1, TOP_K * ROW_TILE), lambda t: (t, 0, 0), memory_space=pltpu.SMEM),
            pl.BlockSpec((ROW_TILE, d), lambda t: (t, 0)),
            pl.BlockSpec((ROW_TILE, TOP_K), lambda t: (t, 0)),
            pl.BlockSpec((1, SUBLANES, mod.shape[2]), lambda t: (layer, 0, 0)),
            pl.BlockSpec((1, d), lambda t: (0, 0)),
            pl.BlockSpec(memory_space=pl.ANY),
        ],
        out_specs=pl.BlockSpec((ROW_TILE, d), lambda t: (t, 0)),
        out_shape=jax.ShapeDtypeStruct((n, d), F32),
        scratch_shapes=[pltpu.VMEM((TOP_K * ROW_TILE, d), F32), pltpu.SemaphoreType.DMA],
        compiler_params=_cparams(("arbitrary",)),
        name="moe_combine%d" % layer,
    )(dest_t, x1, gates.T, mod, lng, yexp)


def _moe_layer(x1, h2, eidx, gates, mod, layer, lng, wg, wu, wd, *, tiles_per_batch, ctx_tiles, nb):
    src, dest, block_expert, n_used = _moe_plan(eidx)
    yexp = _experts(h2, src, block_expert, n_used, wg, wu, wd)
    return _combine(x1, dest, gates, mod, layer, lng, yexp,
                    tiles_per_batch=tiles_per_batch, ctx_tiles=ctx_tiles, nb=nb)


def _attn_kernel(lam_ref, q_ref, k_ref, v_ref, g_ref, o_ref, vext, s_buf0, s_buf1,
                 p_buf0, p_buf1, corr_buf0, corr_buf1, m_buf, acc, *, out_scale, ctx_len):
    t_len = k_ref.shape[0]
    nk = t_len // ATT_TK
    dv = v_ref.shape[1]
    tq = acc.shape[1]
    n_tiles = (o_ref.shape[0] // tq) * nk

    vext[:, 0:dv] = v_ref[...]
    vext[:, dv:2 * dv] = jnp.ones((t_len, dv), BF16)
    lam = lam_ref[0]
    gain = g_ref[...] * out_scale

    s_bufs, p_bufs, corr_bufs = (s_buf0, s_buf1), (p_buf0, p_buf1), (corr_buf0, corr_buf1)

    def key_rows(kj):
        return pl.ds(pl.multiple_of(kj * ATT_TK, ATT_TK), ATT_TK)

    def advance(tile):
        qi, kj = tile
        wrap = kj + 1 == nk
        return jnp.where(wrap, qi + 1, qi), jnp.where(wrap, 0, kj + 1)

    def scores(tile, slot):
        qi, kj = tile
        q = q_ref[pl.ds(pl.multiple_of(ctx_len + qi * tq, ROW_TILE), tq), :]
        lane = lax.broadcasted_iota(jnp.int32, q.shape, 1)
        zero = jnp.zeros(q.shape, q.dtype)
        k = k_ref[key_rows(kj), :]
        s_bufs[slot][0] = _dot_nt(jnp.where(lane < DIFF_DH, q, zero), k)
        s_bufs[slot][1] = _dot_nt(jnp.where(lane >= DIFF_DH, q, zero), k)

    def numerators(tile, slot):
        _, kj = tile
        for w in range(2):
            s = s_bufs[slot][w]
            m_old = jnp.where(kj == 0, -jnp.inf, m_buf[w])
            m_new = jnp.maximum(m_old, jnp.max(s, axis=-1, keepdims=True))
            p_bufs[slot][w] = jnp.exp2(s - m_new).astype(BF16)
            corr_bufs[slot][w] = jnp.exp2(m_old - m_new)
            m_buf[w] = m_new

    def values(tile, slot):
        qi, kj = tile
        ve = vext[key_rows(kj), :]
        a = []
        for w in range(2):
            a.append(corr_bufs[slot][w] * acc[w] + _dot(p_bufs[slot][w], ve))
            acc[w] = a[w]
        o = a[0][:, 0:dv] / a[0][:, dv:2 * dv] - lam * (a[1][:, 0:dv] / a[1][:, dv:2 * dv])
        o = o * lax.rsqrt(jnp.mean(o * o, axis=-1, keepdims=True) + GN_EPS)
        o_ref[pl.ds(pl.multiple_of(qi * tq, tq), tq), :] = (o * gain).astype(BF16)

    def step(tiles, slot):
        a, b, c = tiles
        values(c, slot)
        scores(a, slot)
        numerators(b, 1 - slot)
        return advance(a), a, b

    acc[...] = jnp.zeros(acc.shape, F32)
    t0 = (jnp.int32(0), jnp.int32(0))
    t1 = advance(t0)
    scores(t0, 0)
    scores(t1, 1)
    numerators(t0, 0)

    def pair(_, tiles):
        return step(step(tiles, 0), 1)

    _, last, prev = lax.fori_loop(0, (n_tiles - 2) // 2, pair, (advance(t1), t1, t0))
    numerators(last, (n_tiles - 1) % 2)
    values(prev, n_tiles % 2)
    values(last, (n_tiles - 1) % 2)


def _diff_attention(qkv, lam, subln_g, lambda_init, *, nb, l_len, t_len, ctx_len):
    d = D_MODEL
    tq = 2 * ROW_TILE
    dv = 2 * DIFF_DH
    assert ((l_len // tq) * (t_len // ATT_TK)) % 2 == 0
    kern = functools.partial(_attn_kernel, out_scale=1.0 - lambda_init, ctx_len=ctx_len)
    return pl.pallas_call(
        kern,
        grid_spec=pltpu.PrefetchScalarGridSpec(
            num_scalar_prefetch=1,
            grid=(nb, DIFF_HEADS),
            in_specs=[
                pl.BlockSpec((t_len, LANES), lambda b, h, lam: (b, h)),
                pl.BlockSpec((t_len, LANES), lambda b, h, lam: (b, DIFF_HEADS + h)),
                pl.BlockSpec((t_len, LANES), lambda b, h, lam: (b, 2 * DIFF_HEADS + h)),
                pl.BlockSpec((1, LANES), lambda b, h, lam: (0, 0)),
            ],
            out_specs=pl.BlockSpec((l_len, LANES), lambda b, h, lam: (b, h)),
            scratch_shapes=[
                pltpu.VMEM((t_len, 2 * dv), BF16),
                pltpu.VMEM((2, tq, ATT_TK), F32), pltpu.VMEM((2, tq, ATT_TK), F32),
                pltpu.VMEM((2, tq, ATT_TK), BF16), pltpu.VMEM((2, tq, ATT_TK), BF16),
                pltpu.VMEM((2, tq, 1), F32), pltpu.VMEM((2, tq, 1), F32),
                pltpu.VMEM((2, tq, 1), F32),
                pltpu.VMEM((2, tq, 2 * dv), F32),
            ],
        ),
        out_shape=jax.ShapeDtypeStruct((nb * l_len, d), BF16),
        compiler_params=_cparams(("parallel", "parallel")),
        name="diff_attention",
    )(lam, qkv, qkv, qkv, subln_g.reshape(1, LANES).astype(F32))


def _ret_rope_tables(l_len, ctx_len):
    half = RET_DK // 2
    inv = ROPE_BASE ** (-jnp.arange(0, RET_DK, 2, dtype=F32) / RET_DK)
    ang = jnp.arange(l_len, dtype=F32)[:, None] * inv[None, :]
    ang = jnp.concatenate([jnp.zeros((ctx_len, half), F32), ang], axis=0)
    cos64 = jnp.concatenate([jnp.cos(ang), jnp.cos(ang)], axis=1)
    sin64 = jnp.concatenate([-jnp.sin(ang), jnp.sin(ang)], axis=1)
    kscale = RET_DK ** -0.5
    cos = jnp.concatenate([cos64, cos64 * kscale], axis=1)
    sin = jnp.concatenate([sin64, sin64 * kscale], axis=1)
    return cos[None], sin[None]


def _attn_rope_tables(l_len, ctx_len):
    quarter = DIFF_DH // 4
    inv = ROPE_BASE ** (-jnp.arange(0, DIFF_DH // 2, 2, dtype=F32) / (DIFF_DH // 2))
    pos = jnp.arange(l_len)
    ang_r = (pos // GRID_W).astype(F32)[:, None] * inv[None, :]
    ang_c = (pos % GRID_W).astype(F32)[:, None] * inv[None, :]
    pad = lambda a: jnp.concatenate([jnp.zeros((ctx_len, quarter), F32), a], axis=0)
    ang_r, ang_c = pad(ang_r), pad(ang_c)
    cos64 = jnp.concatenate([jnp.cos(ang_r)] * 2 + [jnp.cos(ang_c)] * 2, axis=1)
    sin64 = jnp.concatenate([-jnp.sin(ang_r), jnp.sin(ang_r), -jnp.sin(ang_c), jnp.sin(ang_c)], axis=1)
    cos = jnp.concatenate([cos64, cos64], axis=1)
    sin = jnp.concatenate([sin64, sin64], axis=1)
    qscale = DIFF_DH ** -0.5 * math.log2(math.e)
    return jnp.stack([cos * qscale, cos]), jnp.stack([sin * qscale, sin])


def kernel(x, c, ctx, c_ctx, ada_w, ada_b, ln_g, w_in_ab, ret_decay_logit, s5_lam_re, s5_lam_im,
           s5_log_dt, s5_b_re, s5_b_im, s5_c_re, s5_c_im, s5_d, s5_w_glu, w_out_ab, w_in_c,
           diff_lambda, diff_subln_g, w_out_c, router_w, router_bias, exp_w_gate, exp_w_up,
           exp_w_down):
    nb, l_len, d = x.shape
    ctx_len = ctx.shape[1]
    t_len = ctx_len + l_len
    nt = nb * t_len
    tpb = t_len // ROW_TILE
    ctx_tiles = ctx_len // ROW_TILE
    assert d == D_MODEL and nb < SUBLANES
    assert l_len % (2 * ROW_TILE) == 0 and ctx_len % ROW_TILE == 0 and t_len % ATT_TK == 0

    xt = jnp.concatenate([ctx, x], axis=1).reshape(nt, d)
    c_all = jnp.concatenate([c, c_ctx[None].astype(c.dtype)], axis=0)
    c_pad = jnp.zeros((SUBLANES, d), F32).at[:nb + 1].set(c_all)
    mod = _adaln(c_pad, ada_w, ada_b)

    wr_t = router_w.T
    rbias = router_bias.reshape(N_EXPERTS, 1).astype(F32)

    w0 = w_in_ab[0]
    q_w, k_w, v_w, g_w, u_w = jnp.split(w0, (RET_QK_W, 2 * RET_QK_W, 2 * RET_QK_W + RET_V_W,
                                             2 * RET_QK_W + 2 * RET_V_W), axis=1)
    qk_w = jnp.concatenate([q_w.reshape(d, RET_HEADS, RET_DK), k_w.reshape(d, RET_HEADS, RET_DK)],
                           axis=2).reshape(d, 2 * RET_QK_W)
    w0p = jnp.concatenate([qk_w, v_w, g_w, u_w], axis=1).astype(BF16)
    cos0, sin0 = _ret_rope_tables(l_len, ctx_len)
    rope_tab0 = [0] * RET_HEADS + [None] * ((w0p.shape[1] - 2 * RET_QK_W) // LANES)
    proj0 = _inproj(xt, mod, 0, w0p, cos0, sin0, rope_tab0, RET_DK // 2,
                    nb=nb, t_len=t_len, ctx_len=ctx_len)

    log_gammas = jax.nn.log_sigmoid(ret_decay_logit[0].astype(F32))
    ret = _retention(proj0, log_gammas, nb=nb, t_len=t_len, ctx_len=ctx_len)

    tc = S5_CHUNK
    nc5 = t_len // tc
    u = proj0[:, 2 * RET_QK_W + 2 * RET_V_W:]
    xg = u.reshape(nb, nc5, tc, S5_G, S5_P).transpose(3, 1, 0, 2, 4)
    xg = jnp.pad(xg, ((0, 0), (0, 0), (0, SUBLANES - nb), (0, 0), (0, 0)))
    xg = xg.reshape(S5_G, nc5 * SUBLANES, tc * S5_P)
    w1, w2, a5 = _s5_operators(s5_lam_re[0], s5_lam_im[0], s5_log_dt[0], s5_b_re[0], s5_b_im[0],
                               s5_c_re[0], s5_c_im[0], s5_d[0])
    yg = _s5(xg, w1, w2, a5, nb=SUBLANES, nctx=ctx_len // tc)
    s5y = yg.reshape(S5_G, nc5, SUBLANES, tc, S5_P)[:, :, :nb].transpose(2, 1, 3, 0, 4).reshape(nt, S5_CH)

    x1, h2, eidx, gates = _merge0(ret, proj0, s5y, xt, mod, s5_w_glu[0].astype(BF16),
                                  w_out_ab[0].astype(BF16), ln_g[0, 0].reshape(1, d), wr_t, rbias,
                                  nb=nb, t_len=t_len, ctx_len=ctx_len)
    x2 = _moe_layer(x1, h2, eidx, gates, mod, 0, ln_g[0, 1].reshape(1, d),
                    exp_w_gate[0].astype(BF16), exp_w_up[0].astype(BF16), exp_w_down[0].astype(BF16),
                    tiles_per_batch=tpb, ctx_tiles=ctx_tiles, nb=nb)

    cos1, sin1 = _attn_rope_tables(l_len, ctx_len)
    n_heads_cols = D_MODEL // LANES
    rope_tab1 = [0] * n_heads_cols + [1] * n_heads_cols + [None] * n_heads_cols
    qkv = _inproj(x2, mod, 1, w_in_c[0].astype(BF16), cos1, sin1, rope_tab1, DIFF_DH // 4,
                  nb=nb, t_len=t_len, ctx_len=ctx_len)
    lf = diff_lambda[0].astype(F32)
    lambda_init = 0.8 - 0.6 * math.exp(-0.3 * 1)
    lam = (jnp.exp(jnp.sum(lf[0] * lf[1])) - jnp.exp(jnp.sum(lf[2] * lf[3])) + lambda_init).reshape(1)
    att = _diff_attention(qkv, lam, diff_subln_g[0], lambda_init,
                          nb=nb, l_len=l_len, t_len=t_len, ctx_len=ctx_len)
    x3, h3, eidx1, gates1 = _merge1(att, x2, mod, 1, w_out_c[0].astype(BF16), ln_g[1, 0].reshape(1, d),
                                    wr_t, rbias, nb=nb, l_len=l_len, t_len=t_len, ctx_len=ctx_len)
    out = _moe_layer(x3, h3, eidx1, gates1, mod, 1, ln_g[1, 1].reshape(1, d),
                     exp_w_gate[1].astype(BF16), exp_w_up[1].astype(BF16), exp_w_down[1].astype(BF16),
                     tiles_per_batch=l_len // ROW_TILE, ctx_tiles=0, nb=nb)
    return out.reshape(nb, l_len, d)
```

```python
import functools
import math

import jax
import jax.numpy as jnp
import numpy as np
from jax import lax
from jax.experimental import pallas as pl
from jax.experimental.pallas import tpu as pltpu

F32 = jnp.float32
BF16 = jnp.bfloat16

D_MODEL = 1024
DEPTH = 2
GRID_W = 64
ALPHA = (2.0 * DEPTH) ** 0.25
LN_EPS = 1e-5
GN_EPS = 1e-6
ROPE_BASE = 10000.0
RET_DK = 64
RET_DV = 128
RET_HEADS = 6
RET_QK_W = RET_HEADS * RET_DK
RET_V_W = RET_HEADS * RET_DV
S5_CH = 256
S5_P = 16
S5_G = 16
S5_N = 64
DIFF_HEADS = 8
DIFF_DH = 64
N_EXPERTS = 16
EXPERTS_PER_GROUP = 4
TOP_K = 2

LANES = 128
SUBLANES = 8
MXU_DIM = 256
ROW_TILE = 256
RET_CHUNK = 256
S5_CHUNK = 32
MOE_ROWS = 256
ATT_TK = 768
VMEM_LIMIT = 48 * 1024 * 1024


def _cparams(sem):
    return pltpu.CompilerParams(dimension_semantics=sem, vmem_limit_bytes=VMEM_LIMIT)


def _dot(a, b):
    return jnp.dot(a, b, preferred_element_type=F32)


def _dot_nt(a, b):
    return lax.dot_general(a, b, (((1,), (1,)), ((), ())), preferred_element_type=F32)


def _dot_tn(a, b):
    return lax.dot_general(a, b, (((0,), (0,)), ((), ())), preferred_element_type=F32)


def _split_bf16(x):
    hi = x.astype(BF16)
    lo = (x - hi.astype(F32)).astype(BF16)
    return hi, lo


def _dot3(a, b):
    ah, al = _split_bf16(a)
    bh, bl = _split_bf16(b)
    return _dot(ah, bh) + _dot(ah, bl) + _dot(al, bh)


def _dot3_nt(a, b):
    ah, al = _split_bf16(a)
    bh, bl = _split_bf16(b)
    return _dot_nt(ah, bh) + _dot_nt(ah, bl) + _dot_nt(al, bh)


def _sigmoid(x):
    return 1.0 / (1.0 + jnp.exp(-x))


def _silu(x):
    return x * _sigmoid(x)


def _pack_bf16_pairs(v):
    half = v.shape[1] // 2
    bits = lax.bitcast_convert_type(v.astype(BF16).astype(F32), jnp.uint32)
    return (bits[:, :half] >> 16) | (bits[:, half:] & jnp.uint32(0xFFFF0000))


def _unpack_bf16_pairs(p):
    lo = lax.bitcast_convert_type(p << 16, F32)
    hi = lax.bitcast_convert_type(p & jnp.uint32(0xFFFF0000), F32)
    return jnp.concatenate([lo, hi], axis=1)


def _gelu_tanh(x):
    c = math.sqrt(2.0 / math.pi)
    return 0.5 * x * (1.0 + jnp.tanh(c * (x + 0.044715 * (x * x * x))))


def _adaln_kernel(c_ref, w_ref, b_ref, o_ref):
    c = c_ref[...]
    o_ref[0] = _dot3(_silu(c), w_ref[0]) + b_ref[0]


def _adaln(c_pad, ada_w, ada_b):
    depth, d, n = ada_w.shape
    tn = 1536
    return pl.pallas_call(
        _adaln_kernel,
        grid=(depth, n // tn),
        in_specs=[
            pl.BlockSpec((SUBLANES, d), lambda i, j: (0, 0)),
            pl.BlockSpec((1, d, tn), lambda i, j: (i, 0, j)),
            pl.BlockSpec((1, 1, tn), lambda i, j: (i, 0, j)),
        ],
        out_specs=pl.BlockSpec((1, SUBLANES, tn), lambda i, j: (i, 0, j)),
        out_shape=jax.ShapeDtypeStruct((depth, SUBLANES, n), F32),
        compiler_params=_cparams(("parallel", "parallel")),
        name="adaln",
    )(c_pad, ada_w, ada_b.reshape(depth, 1, n))


def _mod_row(t, tiles_per_batch, ctx_tiles, nb):
    b = lax.div(t, tiles_per_batch)
    w = lax.rem(t, tiles_per_batch)
    return jnp.where(w < ctx_tiles, nb, b)


def _rope_block(a, cos, sin, half):
    lane = lax.broadcasted_iota(jnp.int32, a.shape, 1)
    first = lax.rem(lane, 2 * half) < half
    rot = jnp.where(first, pltpu.roll(a, LANES - half, 1), pltpu.roll(a, half, 1))
    return a * cos + rot * sin


def _inproj_kernel(x_ref, mod_ref, w_ref, cos_ref, sin_ref, o_ref, *, tiles_per_batch,
                   ctx_tiles, nb, rope_tab, rope_half):
    d = x_ref.shape[1]
    n = w_ref.shape[1]
    r = _mod_row(pl.program_id(0), tiles_per_batch, ctx_tiles, nb)
    sh = mod_ref[0, pl.ds(r, 1), 0:d]
    sc = mod_ref[0, pl.ds(r, 1), d:2 * d]
    xm = (x_ref[...] * (1.0 + sc) + sh).astype(BF16)
    for j in range(n // MXU_DIM):
        acc = _dot(xm, w_ref[:, j * MXU_DIM:(j + 1) * MXU_DIM])
        parts = []
        for s in range(MXU_DIM // LANES):
            blk = acc[:, s * LANES:(s + 1) * LANES]
            tab = rope_tab[j * (MXU_DIM // LANES) + s]
            if tab is not None:
                blk = _rope_block(blk, cos_ref[tab], sin_ref[tab], rope_half)
            parts.append(blk)
        o_ref[:, j * MXU_DIM:(j + 1) * MXU_DIM] = jnp.concatenate(parts, axis=1).astype(BF16)


def _inproj(x, mod, layer, w, cos, sin, rope_tab, rope_half, *, nb, t_len, ctx_len):
    nt, d = x.shape
    n = w.shape[1]
    tpb = t_len // ROW_TILE
    kern = functools.partial(_inproj_kernel, tiles_per_batch=tpb, ctx_tiles=ctx_len // ROW_TILE,
                             nb=nb, rope_tab=tuple(rope_tab), rope_half=rope_half)
    ntab = cos.shape[0]
    return pl.pallas_call(
        kern,
        grid=(nt // ROW_TILE,),
        in_specs=[
            pl.BlockSpec((ROW_TILE, d), lambda t: (t, 0)),
            pl.BlockSpec((1, SUBLANES, mod.shape[2]), lambda t: (layer, 0, 0)),
            pl.BlockSpec((d, n), lambda t: (0, 0)),
            pl.BlockSpec((ntab, ROW_TILE, LANES), lambda t: (0, lax.rem(t, tpb), 0)),
            pl.BlockSpec((ntab, ROW_TILE, LANES), lambda t: (0, lax.rem(t, tpb), 0)),
        ],
        out_specs=pl.BlockSpec((ROW_TILE, n), lambda t: (t, 0)),
        out_shape=jax.ShapeDtypeStruct((nt, n), BF16),
        compiler_params=_cparams(("parallel",)),
        name="inproj%d" % layer,
    )(x, mod, w, cos, sin)


def _retention_kernel(lg_ref, qk_ref, v_ref, o_ref, sf_ref, sb_ref, *, nctx):
    c_len = RET_CHUNK
    t_len = qk_ref.shape[0]
    nc = t_len // c_len
    h = pl.program_id(1)
    lgf = lg_ref[0, h]
    lgb = lg_ref[1, h]
    ii = lax.broadcasted_iota(jnp.int32, (c_len, 1), 0).astype(F32)
    jj = lax.broadcasted_iota(jnp.int32, (1, c_len), 1).astype(F32)
    diff = ii - jj
    decay = jnp.where(diff >= 0.0, jnp.exp(lgf * jnp.maximum(diff, 0.0)),
                      jnp.exp(lgb * jnp.maximum(-diff, 0.0)))
    kdf = jnp.exp(lgf * (c_len - 1.0 - ii))
    kdb = jnp.exp(lgb * ii)
    qdf = jnp.exp(lgf * (ii + 1.0))
    qdb = jnp.exp(lgb * (c_len - ii))
    zrow = jnp.zeros((1, RET_DV), F32)
    gf_chunk = jnp.exp(zrow + lgf * c_len)
    gb_chunk = jnp.exp(zrow + lgb * c_len)

    def load(c):
        rows = pl.ds(pl.multiple_of(c * c_len, c_len), c_len)
        qk = qk_ref[rows, :].astype(F32)
        return qk[:, :RET_DK], qk[:, RET_DK:], v_ref[rows, :]

    def fwd_state(c, s):
        sf_ref[c] = s
        _, k, v = load(c)
        return gf_chunk * s + _dot_tn((k * kdf).astype(BF16), v)

    lax.fori_loop(0, nc, fwd_state, jnp.zeros((RET_DK, RET_DV), F32))

    def bwd_state(j, s):
        c = jnp.where(j < nctx, nctx - 1 - j, nc - 1 - (j - nctx))
        sb_ref[c] = s
        _, k, v = load(c)
        return gb_chunk * s + _dot_tn((k * kdb).astype(BF16), v)

    lax.fori_loop(0, nc, bwd_state, jnp.zeros((RET_DK, RET_DV), F32))

    def out_chunk(c, carry):
        q, k, v = load(c)
        scores = _dot_nt(q.astype(BF16), k.astype(BF16)) * decay
        o = _dot(scores.astype(BF16), v)
        o = o + _dot((q * qdf).astype(BF16), sf_ref[c].astype(BF16))
        o = o + _dot((q * qdb).astype(BF16), sb_ref[c].astype(BF16))
        mu = jnp.mean(o, axis=-1, keepdims=True)
        oc = o - mu
        var = jnp.mean(oc * oc, axis=-1, keepdims=True)
        rows = pl.ds(pl.multiple_of(c * c_len, c_len), c_len)
        o_ref[rows, :] = (oc * lax.rsqrt(var + GN_EPS)).astype(BF16)
        return carry

    lax.fori_loop(0, nc, out_chunk, 0)


def _retention(proj, log_gammas, *, nb, t_len, ctx_len):
    nt = proj.shape[0]
    nc = t_len // RET_CHUNK
    kern = functools.partial(_retention_kernel, nctx=ctx_len // RET_CHUNK)
    vcol0 = RET_HEADS
    return pl.pallas_call(
        kern,
        grid_spec=pltpu.PrefetchScalarGridSpec(
            num_scalar_prefetch=1,
            grid=(nb, RET_HEADS),
            in_specs=[
                pl.BlockSpec((t_len, LANES), lambda b, h, lg: (b, h)),
                pl.BlockSpec((t_len, LANES), lambda b, h, lg: (b, vcol0 + h)),
            ],
            out_specs=pl.BlockSpec((t_len, LANES), lambda b, h, lg: (b, h)),
            scratch_shapes=[pltpu.VMEM((nc, RET_DK, RET_DV), F32),
                            pltpu.VMEM((nc, RET_DK, RET_DV), F32)],
        ),
        out_shape=jax.ShapeDtypeStruct((nt, RET_V_W), BF16),
        compiler_params=_cparams(("parallel", "parallel")),
        name="retention",
    )(log_gammas, proj, proj)


def _s5_operators(lam_re, lam_im, log_dt, b_re, b_im, c_re, c_im, d_skip):
    tc = S5_CHUNK
    hp = lax.Precision.HIGHEST
    ks = jnp.arange(tc + 1, dtype=F32)
    pw, bbar, cm = [], [], []
    for direction in range(2):
        dt = jnp.exp(log_dt[direction].astype(F32))[:, None]
        lam = lax.complex(lam_re[direction].astype(F32), lam_im[direction].astype(F32))
        z = lam * dt
        p = jnp.exp(z[None] * ks[:, None, None])
        lam_bar = p[1]
        bb = ((lam_bar - 1.0) / lam)[..., None] * lax.complex(
            b_re[direction].astype(F32), b_im[direction].astype(F32))
        pw.append(p)
        bbar.append(bb)
        cm.append(lax.complex(c_re[direction].astype(F32), c_im[direction].astype(F32)))

    def lag_kernel(p, bb, c):
        return jnp.einsum('gpn,kgn,gnq->kgpq', c, p[:tc], bb, precision=hp).real

    kf = lag_kernel(pw[0], bbar[0], cm[0])
    kb = lag_kernel(pw[1], bbar[1], cm[1])
    k0 = kf[0] + kb[0] + jnp.eye(S5_P, dtype=F32)[None] * d_skip.astype(F32)[:, :, None]
    kcat = jnp.concatenate([kb[1:][::-1], k0[None], kf[1:]], axis=0)
    s_idx = jnp.arange(tc)[:, None]
    t_idx = jnp.arange(tc)[None, :]
    m5 = kcat[t_idx - s_idx + tc - 1]
    intra = m5.transpose(2, 0, 4, 1, 3).reshape(S5_G, tc * S5_P, tc * S5_P)

    ef = pw[0][:tc][::-1][:, :, :, None] * bbar[0][None]
    eb = pw[1][:tc][:, :, :, None] * bbar[1][None]
    to_in = lambda e: e.transpose(1, 0, 3, 2).reshape(S5_G, tc * S5_P, S5_N)
    w1 = jnp.concatenate([intra, to_in(ef.real), to_in(ef.imag), to_in(eb.real), to_in(eb.imag)],
                         axis=2)
    of = cm[0][None] * pw[0][1:][:, :, None, :]
    ob = cm[1][None] * pw[1][1:][::-1][:, :, None, :]
    to_out = lambda o: o.transpose(1, 3, 0, 2).reshape(S5_G, S5_N, tc * S5_P)
    w2 = jnp.concatenate([to_out(of.real), -to_out(of.imag), to_out(ob.real), -to_out(ob.imag)],
                         axis=1)
    a = jnp.stack([pw[0][tc].real, pw[0][tc].imag, pw[1][tc].real, pw[1][tc].imag], axis=1)
    return w1.astype(BF16), w2.astype(BF16), a


def _s5_kernel(x_ref, w1_ref, w2_ref, a_ref, y_ref, e_ref, st_ref, *, nb, nctx):
    n = S5_N
    w = x_ref.shape[2]
    rows = x_ref.shape[1]
    nc = rows // nb
    x = x_ref[0]
    e_ref[...] = _dot(x, w1_ref[0, :, w:])
    afr = a_ref[0, 0:1, :]
    afi = a_ref[0, 1:2, :]
    abr = a_ref[0, 2:3, :]
    abi = a_ref[0, 3:4, :]

    def step(j, carry):
        fr, fi, br, bi = carry
        rf = pl.ds(pl.multiple_of(j * nb, nb), nb)
        cb = jnp.where(j < nctx, nctx - 1 - j, nc - 1 - (j - nctx))
        rb = pl.ds(pl.multiple_of(cb * nb, nb), nb)
        st_ref[rf, 0:n] = fr
        st_ref[rf, n:2 * n] = fi
        st_ref[rb, 2 * n:3 * n] = br
        st_ref[rb, 3 * n:4 * n] = bi
        nfr = afr * fr - afi * fi + e_ref[rf, 0:n]
        nfi = afr * fi + afi * fr + e_ref[rf, n:2 * n]
        nbr = abr * br - abi * bi + e_ref[rb, 2 * n:3 * n]
        nbi = abr * bi + abi * br + e_ref[rb, 3 * n:4 * n]
        return nfr, nfi, nbr, nbi

    z = jnp.zeros((nb, n), F32)
    lax.fori_loop(0, nc, step, (z, z, z, z))
    y_ref[0] = _dot(x, w1_ref[0, :, :w]) + _dot(st_ref[...].astype(BF16), w2_ref[0])


def _s5(xg, w1, w2, a, *, nb, nctx):
    g, rows, w = xg.shape
    kern = functools.partial(_s5_kernel, nb=nb, nctx=nctx)
    return pl.pallas_call(
        kern,
        grid=(g,),
        in_specs=[
            pl.BlockSpec((1, rows, w), lambda i: (i, 0, 0)),
            pl.BlockSpec((1, w, w + 4 * S5_N), lambda i: (i, 0, 0)),
            pl.BlockSpec((1, 4 * S5_N, w), lambda i: (i, 0, 0)),
            pl.BlockSpec((1, 4, S5_N), lambda i: (i, 0, 0)),
        ],
        out_specs=pl.BlockSpec((1, rows, w), lambda i: (i, 0, 0)),
        out_shape=jax.ShapeDtypeStruct((g, rows, w), F32),
        scratch_shapes=[pltpu.VMEM((rows, 4 * S5_N), F32), pltpu.VMEM((rows, 4 * S5_N), F32)],
        compiler_params=_cparams(("parallel",)),
        name="s5",
    )(xg, w1, w2, a)


def _route(logits_t, bias):
    scores = _sigmoid(logits_t)
    biased = scores + bias
    s_rows = [scores[e:e + 1, :] for e in range(N_EXPERTS)]
    b_rows = [biased[e:e + 1, :] for e in range(N_EXPERTS)]
    n_groups = N_EXPERTS // EXPERTS_PER_GROUP
    best = None
    sel = None
    for g in range(n_groups):
        a, b, c, d = b_rows[4 * g:4 * g + 4]
        hi1, lo1 = jnp.maximum(a, b), jnp.minimum(a, b)
        hi2, lo2 = jnp.maximum(c, d), jnp.minimum(c, d)
        top1 = jnp.maximum(hi1, hi2)
        top2 = jnp.maximum(jnp.minimum(hi1, hi2), jnp.maximum(lo1, lo2))
        gs = top1 + top2
        if g == 0:
            best, sel = gs, jnp.zeros(gs.shape, jnp.int32)
        else:
            better = gs > best
            sel = jnp.where(better, g, sel)
            best = jnp.where(better, gs, best)
    neg = jnp.full(best.shape, -jnp.inf, F32)
    masked = [jnp.where(sel == (e // EXPERTS_PER_GROUP), b_rows[e], neg) for e in range(N_EXPERTS)]
    v1, i1, g1 = masked[0], jnp.zeros(best.shape, jnp.int32), s_rows[0]
    for e in range(1, N_EXPERTS):
        better = masked[e] > v1
        v1 = jnp.where(better, masked[e], v1)
        i1 = jnp.where(better, e, i1)
        g1 = jnp.where(better, s_rows[e], g1)
    v2, i2, g2 = neg, jnp.zeros(best.shape, jnp.int32), jnp.zeros(best.shape, F32)
    for e in range(N_EXPERTS):
        cand = jnp.where(i1 == e, neg, masked[e])
        better = cand > v2
        v2 = jnp.where(better, cand, v2)
        i2 = jnp.where(better, e, i2)
        g2 = jnp.where(better, s_rows[e], g2)
    tot = g1 + g2
    return jnp.concatenate([i1, i2], axis=0), jnp.concatenate([g1 / tot, g2 / tot], axis=0)


def _tail(x, o, mod_ref, r, lng, wr, rb, x1_ref, h2_ref, ei_ref, gt_ref):
    d = x.shape[1]
    g1 = mod_ref[0, pl.ds(r, 1), 2 * d:3 * d]
    sh2 = mod_ref[0, pl.ds(r, 1), 3 * d:4 * d]
    sc2 = mod_ref[0, pl.ds(r, 1), 4 * d:5 * d]
    y = ALPHA * x + g1 * o
    mu = jnp.mean(y, axis=-1, keepdims=True)
    yc = y - mu
    var = jnp.mean(yc * yc, axis=-1, keepdims=True)
    x1 = yc * lax.rsqrt(var + LN_EPS) * lng
    h2 = x1 * (1.0 + sc2) + sh2
    x1_ref[...] = x1
    h2_ref[...] = _pack_bf16_pairs(h2)
    ei, gt = _route(_dot3_nt(wr, h2), rb)
    ei_ref[...] = ei
    gt_ref[...] = gt


def _merge0_kernel(r_ref, g_ref, s_ref, x_ref, mod_ref, wglu_ref, wout_ref, lng_ref, wr_ref,
                   rb_ref, x1_ref, h2_ref, ei_ref, gt_ref, *, tiles_per_batch, ctx_tiles, nb):
    r = _mod_row(pl.program_id(0), tiles_per_batch, ctx_tiles, nb)
    ret = r_ref[...].astype(F32) * _silu(g_ref[...].astype(F32))
    z = _dot(_gelu_tanh(s_ref[...]).astype(BF16), wglu_ref[...])
    zz = z[:, :S5_CH] * _sigmoid(z[:, S5_CH:])
    o = _dot(ret.astype(BF16), wout_ref[0:RET_V_W, :]) + _dot(zz.astype(BF16), wout_ref[RET_V_W:, :])
    _tail(x_ref[...], o, mod_ref, r, lng_ref[...], wr_ref[...], rb_ref[...],
          x1_ref, h2_ref, ei_ref, gt_ref)


def _merge1_kernel(a_ref, x_ref, mod_ref, wout_ref, lng_ref, wr_ref, rb_ref,
                   x1_ref, h2_ref, ei_ref, gt_ref, *, tiles_per_batch):
    r = lax.div(pl.program_id(0), tiles_per_batch)
    o = _dot(a_ref[...], wout_ref[...])
    _tail(x_ref[...], o, mod_ref, r, lng_ref[...], wr_ref[...], rb_ref[...],
          x1_ref, h2_ref, ei_ref, gt_ref)


def _tail_outs(n_rows, d):
    shapes = (jax.ShapeDtypeStruct((n_rows, d), F32), jax.ShapeDtypeStruct((n_rows, d // 2), jnp.uint32),
              jax.ShapeDtypeStruct((TOP_K, n_rows), jnp.int32),
              jax.ShapeDtypeStruct((TOP_K, n_rows), F32))
    specs = (pl.BlockSpec((ROW_TILE, d), lambda t: (t, 0)), pl.BlockSpec((ROW_TILE, d // 2), lambda t: (t, 0)),
             pl.BlockSpec((TOP_K, ROW_TILE), lambda t: (0, t)),
             pl.BlockSpec((TOP_K, ROW_TILE), lambda t: (0, t)))
    return shapes, specs


def _merge0(ret, proj, s5y, x, mod, w_glu, w_out, lng, wr_t, rbias, *, nb, t_len, ctx_len):
    nt, d = x.shape
    tpb = t_len // ROW_TILE
    kern = functools.partial(_merge0_kernel, tiles_per_batch=tpb, ctx_tiles=ctx_len // ROW_TILE, nb=nb)
    shapes, specs = _tail_outs(nt, d)
    gcol = (2 * RET_QK_W + RET_V_W) // RET_V_W
    full = lambda a: pl.BlockSpec(a.shape, lambda t: (0,) * a.ndim)
    return pl.pallas_call(
        kern,
        grid=(nt // ROW_TILE,),
        in_specs=[
            pl.BlockSpec((ROW_TILE, RET_V_W), lambda t: (t, 0)),
            pl.BlockSpec((ROW_TILE, RET_V_W), lambda t: (t, gcol)),
            pl.BlockSpec((ROW_TILE, S5_CH), lambda t: (t, 0)),
            pl.BlockSpec((ROW_TILE, d), lambda t: (t, 0)),
            pl.BlockSpec((1, SUBLANES, mod.shape[2]), lambda t: (0, 0, 0)),
            full(w_glu), full(w_out), full(lng), full(wr_t), full(rbias),
        ],
        out_specs=specs,
        out_shape=shapes,
        compiler_params=_cparams(("parallel",)),
        name="merge0",
    )(ret, proj, s5y, x, mod, w_glu, w_out, lng, wr_t, rbias)


def _merge1(att, x, mod, layer, w_out, lng, wr_t, rbias, *, nb, l_len, t_len, ctx_len):
    n_lat, d = att.shape
    tpb = l_len // ROW_TILE
    tpb_t = t_len // ROW_TILE
    ctx_tiles = ctx_len // ROW_TILE
    kern = functools.partial(_merge1_kernel, tiles_per_batch=tpb)
    shapes, specs = _tail_outs(n_lat, d)
    full = lambda a: pl.BlockSpec(a.shape, lambda t: (0,) * a.ndim)
    xrow = lambda t: (lax.div(t, tpb) * tpb_t + ctx_tiles + lax.rem(t, tpb), 0)
    return pl.pallas_call(
        kern,
        grid=(n_lat // ROW_TILE,),
        in_specs=[
            pl.BlockSpec((ROW_TILE, d), lambda t: (t, 0)),
            pl.BlockSpec((ROW_TILE, d), xrow),
            pl.BlockSpec((1, SUBLANES, mod.shape[2]), lambda t: (layer, 0, 0)),
            full(w_out), full(lng), full(wr_t), full(rbias),
        ],
        out_specs=specs,
        out_shape=shapes,
        compiler_params=_cparams(("parallel",)),
        name="merge1",
    )(att, x, mod, w_out, lng, wr_t, rbias)


def _moe_plan(eidx):
    k, n = eidx.shape
    a = k * n
    e_flat = eidx.reshape(a)
    onehot = (e_flat[:, None] == jnp.arange(N_EXPERTS, dtype=jnp.int32)[None, :]).astype(jnp.int32)
    csum = jnp.cumsum(onehot, axis=0)
    counts = csum[-1]
    rank = jnp.take_along_axis(csum, e_flat[:, None], axis=1)[:, 0] - 1
    padded = (counts + MOE_ROWS - 1) // MOE_ROWS * MOE_ROWS
    pad_end = jnp.cumsum(padded)
    pad_start = pad_end - padded
    dest = (pad_start[e_flat] + rank).astype(jnp.int32)
    n_blocks = -(-(a + N_EXPERTS * (MOE_ROWS - 1)) // MOE_ROWS)
    n_rows = n_blocks * MOE_ROWS
    row_assign = jnp.full((n_rows,), -1, jnp.int32).at[dest].set(jnp.arange(a, dtype=jnp.int32))
    valid = row_assign >= 0
    token = jnp.where(row_assign >= n, row_assign - n, row_assign)
    src = jnp.where(valid, token, 0)
    spill = a + jnp.cumsum(jnp.logical_not(valid).astype(jnp.int32)) - 1
    dst = jnp.where(valid, row_assign, spill)
    dst = jnp.concatenate([n_rows + jnp.arange(MOE_ROWS, dtype=jnp.int32), dst])
    block_expert = jnp.minimum(
        jnp.searchsorted(pad_end, jnp.arange(n_blocks, dtype=jnp.int32) * MOE_ROWS, side='right'),
        N_EXPERTS - 1).astype(jnp.int32)
    return (src.reshape(n_blocks, 1, MOE_ROWS), dst.reshape(n_blocks + 1, 1, MOE_ROWS), block_expert)


def _experts_kernel(be_ref, src0_ref, srcn_ref, dstp_ref, dstc_ref, h_hbm, wg_ref, wu_ref, wd_ref,
                    y_hbm, xbuf, obuf, gsem, ssem):
    i = pl.program_id(0)
    last = pl.num_programs(0) - 1
    rows = xbuf.shape[1]
    slot = lax.rem(i, 2)
    other = 1 - slot

    def gather(idx_ref, r, s):
        return pltpu.make_async_copy(h_hbm.at[pl.ds(idx_ref[0, 0, r], 1)],
                                     xbuf.at[s, pl.ds(r, 1)], gsem.at[s])

    def scatter(idx_ref, r, s):
        return pltpu.make_async_copy(obuf.at[s, pl.ds(r, 1)],
                                     y_hbm.at[pl.ds(idx_ref[0, 0, r], 1)], ssem.at[s])

    def wait_gathers(s):
        pltpu.make_async_copy(h_hbm.at[pl.ds(0, rows)], xbuf.at[s], gsem.at[s]).wait()

    def wait_scatters(s):
        pltpu.make_async_copy(obuf.at[s], y_hbm.at[pl.ds(0, rows)], ssem.at[s]).wait()

    @pl.when(i == 0)
    def _():
        obuf[1] = jnp.zeros(obuf.shape[1:], obuf.dtype)

        def start(r, c):
            gather(src0_ref, r, 0).start()
            return c

        lax.fori_loop(0, rows, start, 0)
        wait_gathers(0)

    x = _unpack_bf16_pairs(xbuf[slot]).astype(BF16)
    for r in range(rows):
        gather(srcn_ref, r, other).start()
    for r in range(rows):
        scatter(dstp_ref, r, other).start()
    hg = _dot(x, wg_ref[0])
    hu = _dot(x, wu_ref[0])
    obuf[slot] = _pack_bf16_pairs(_dot((_silu(hg) * hu).astype(BF16), wd_ref[0]))
    wait_scatters(other)
    wait_gathers(other)

    @pl.when(i == last)
    def _():
        def start(r, c):
            scatter(dstc_ref, r, slot).start()
            return c

        lax.fori_loop(0, rows, start, 0)
        wait_scatters(slot)


def _experts(h, src, dst, block_expert, wg, wu, wd):
    n_blocks = src.shape[0]
    w = h.shape[1]
    d, dff = wg.shape[1], wg.shape[2]
    idx_spec = lambda f: pl.BlockSpec((1, 1, MOE_ROWS), f, memory_space=pltpu.SMEM)
    return pl.pallas_call(
        _experts_kernel,
        grid_spec=pltpu.PrefetchScalarGridSpec(
            num_scalar_prefetch=1,
            grid=(n_blocks,),
            in_specs=[
                idx_spec(lambda i, be: (0, 0, 0)),
                idx_spec(lambda i, be: (jnp.minimum(i + 1, n_blocks - 1), 0, 0)),
                idx_spec(lambda i, be: (i, 0, 0)),
                idx_spec(lambda i, be: (i + 1, 0, 0)),
                pl.BlockSpec(memory_space=pl.ANY),
                pl.BlockSpec((1, d, dff), lambda i, be: (be[i], 0, 0)),
                pl.BlockSpec((1, d, dff), lambda i, be: (be[i], 0, 0)),
                pl.BlockSpec((1, dff, d), lambda i, be: (be[i], 0, 0)),
            ],
            out_specs=pl.BlockSpec(memory_space=pl.ANY),
            scratch_shapes=[pltpu.VMEM((2, MOE_ROWS, w), jnp.uint32),
                            pltpu.VMEM((2, MOE_ROWS, w), jnp.uint32),
                            pltpu.SemaphoreType.DMA((2,)), pltpu.SemaphoreType.DMA((2,))],
        ),
        out_shape=jax.ShapeDtypeStruct(((n_blocks + 1) * MOE_ROWS, w), jnp.uint32),
        compiler_params=_cparams(("arbitrary",)),
        name="moe_experts",
    )(block_expert, src, src, dst, dst, h, wg, wu, wd)


def _combine_kernel(x_ref, y0_ref, y1_ref, gt_ref, mod_ref, lng_ref, o_ref, *,
                    tiles_per_batch, ctx_tiles, nb):
    d = x_ref.shape[1]
    r = _mod_row(pl.program_id(0), tiles_per_batch, ctx_tiles, nb)
    gt = gt_ref[...]
    y = _unpack_bf16_pairs(y0_ref[...]) * gt[:, 0:1] + _unpack_bf16_pairs(y1_ref[...]) * gt[:, 1:2]
    g2 = mod_ref[0, pl.ds(r, 1), 5 * d:6 * d]
    z = ALPHA * x_ref[...] + g2 * y
    mu = jnp.mean(z, axis=-1, keepdims=True)
    zc = z - mu
    var = jnp.mean(zc * zc, axis=-1, keepdims=True)
    o_ref[...] = zc * lax.rsqrt(var + LN_EPS) * lng_ref[...]


def _combine(x1, gates, mod, layer, lng, y2, *, tiles_per_batch, ctx_tiles, nb):
    n, d = x1.shape
    nt = n // ROW_TILE
    kern = functools.partial(_combine_kernel, tiles_per_batch=tiles_per_batch, ctx_tiles=ctx_tiles, nb=nb)
    return pl.pallas_call(
        kern,
        grid=(nt,),
        in_specs=[
            pl.BlockSpec((ROW_TILE, d), lambda t: (t, 0)),
            pl.BlockSpec((ROW_TILE, d // 2), lambda t: (t, 0)),
            pl.BlockSpec((ROW_TILE, d // 2), lambda t: (nt + t, 0)),
            pl.BlockSpec((ROW_TILE, TOP_K), lambda t: (t, 0)),
            pl.BlockSpec((1, SUBLANES, mod.shape[2]), lambda t: (layer, 0, 0)),
            pl.BlockSpec((1, d), lambda t: (0, 0)),
        ],
        out_specs=pl.BlockSpec((ROW_TILE, d), lambda t: (t, 0)),
        out_shape=jax.ShapeDtypeStruct((n, d), F32),
        compiler_params=_cparams(("parallel",)),
        name="moe_combine%d" % layer,
    )(x1, y2, y2, gates.T, mod, lng)


def _moe_layer(x1, h2, eidx, gates, mod, layer, lng, wg, wu, wd, *, tiles_per_batch, ctx_tiles, nb):
    src, dst, block_expert = _moe_plan(eidx)
    y2 = _experts(h2, src, dst, block_expert, wg, wu, wd)
    return _combine(x1, gates, mod, layer, lng, y2,
                    tiles_per_batch=tiles_per_batch, ctx_tiles=ctx_tiles, nb=nb)


def _attn_kernel(lam_ref, q_ref, k_ref, v_ref, g_ref, o_ref, vext, s_buf0, s_buf1,
                 p_buf0, p_buf1, corr_buf0, corr_buf1, m_buf, acc, *, out_scale, ctx_len):
    t_len = k_ref.shape[0]
    nk = t_len // ATT_TK
    dv = v_ref.shape[1]
    tq = acc.shape[1]
    n_tiles = (o_ref.shape[0] // tq) * nk

    vext[:, 0:dv] = v_ref[...]
    vext[:, dv:2 * dv] = jnp.ones((t_len, dv), BF16)
    lam = lam_ref[0]
    gain = g_ref[...] * out_scale

    s_bufs, p_bufs, corr_bufs = (s_buf0, s_buf1), (p_buf0, p_buf1), (corr_buf0, corr_buf1)

    def key_rows(kj):
        return pl.ds(pl.multiple_of(kj * ATT_TK, ATT_TK), ATT_TK)

    def advance(tile):
        qi, kj = tile
        wrap = kj + 1 == nk
        return jnp.where(wrap, qi + 1, qi), jnp.where(wrap, 0, kj + 1)

    def scores(tile, slot):
        qi, kj = tile
        q = q_ref[pl.ds(pl.multiple_of(ctx_len + qi * tq, ROW_TILE), tq), :]
        lane = lax.broadcasted_iota(jnp.int32, q.shape, 1)
        zero = jnp.zeros(q.shape, q.dtype)
        k = k_ref[key_rows(kj), :]
        s_bufs[slot][0] = _dot_nt(jnp.where(lane < DIFF_DH, q, zero), k)
        s_bufs[slot][1] = _dot_nt(jnp.where(lane >= DIFF_DH, q, zero), k)

    def numerators(tile, slot):
        _, kj = tile
        for w in range(2):
            s = s_bufs[slot][w]
            m_old = jnp.where(kj == 0, -jnp.inf, m_buf[w])
            m_new = jnp.maximum(m_old, jnp.max(s, axis=-1, keepdims=True))
            p_bufs[slot][w] = jnp.exp2(s - m_new).astype(BF16)
            corr_bufs[slot][w] = jnp.exp2(m_old - m_new)
            m_buf[w] = m_new

    def values(tile, slot):
        qi, kj = tile
        ve = vext[key_rows(kj), :]
        a = []
        for w in range(2):
            a.append(corr_bufs[slot][w] * acc[w] + _dot(p_bufs[slot][w], ve))
            acc[w] = a[w]
        o = a[0][:, 0:dv] / a[0][:, dv:2 * dv] - lam * (a[1][:, 0:dv] / a[1][:, dv:2 * dv])
        o = o * lax.rsqrt(jnp.mean(o * o, axis=-1, keepdims=True) + GN_EPS)
        o_ref[pl.ds(pl.multiple_of(qi * tq, tq), tq), :] = (o * gain).astype(BF16)

    def step(tiles, slot):
        a, b, c = tiles
        values(c, slot)
        scores(a, slot)
        numerators(b, 1 - slot)
        return advance(a), a, b

    acc[...] = jnp.zeros(acc.shape, F32)
    t0 = (jnp.int32(0), jnp.int32(0))
    t1 = advance(t0)
    scores(t0, 0)
    scores(t1, 1)
    numerators(t0, 0)

    def pair(_, tiles):
        return step(step(tiles, 0), 1)

    _, last, prev = lax.fori_loop(0, (n_tiles - 2) // 2, pair, (advance(t1), t1, t0))
    numerators(last, (n_tiles - 1) % 2)
    values(prev, n_tiles % 2)
    values(last, (n_tiles - 1) % 2)


def _diff_attention(qkv, lam, subln_g, lambda_init, *, nb, l_len, t_len, ctx_len):
    d = D_MODEL
    tq = 2 * ROW_TILE
    dv = 2 * DIFF_DH
    assert ((l_len // tq) * (t_len // ATT_TK)) % 2 == 0
    kern = functools.partial(_attn_kernel, out_scale=1.0 - lambda_init, ctx_len=ctx_len)
    return pl.pallas_call(
        kern,
        grid_spec=pltpu.PrefetchScalarGridSpec(
            num_scalar_prefetch=1,
            grid=(nb, DIFF_HEADS),
            in_specs=[
                pl.BlockSpec((t_len, LANES), lambda b, h, lam: (b, h)),
                pl.BlockSpec((t_len, LANES), lambda b, h, lam: (b, DIFF_HEADS + h)),
                pl.BlockSpec((t_len, LANES), lambda b, h, lam: (b, 2 * DIFF_HEADS + h)),
                pl.BlockSpec((1, LANES), lambda b, h, lam: (0, 0)),
            ],
            out_specs=pl.BlockSpec((l_len, LANES), lambda b, h, lam: (b, h)),
            scratch_shapes=[
                pltpu.VMEM((t_len, 2 * dv), BF16),
                pltpu.VMEM((2, tq, ATT_TK), F32), pltpu.VMEM((2, tq, ATT_TK), F32),
                pltpu.VMEM((2, tq, ATT_TK), BF16), pltpu.VMEM((2, tq, ATT_TK), BF16),
                pltpu.VMEM((2, tq, 1), F32), pltpu.VMEM((2, tq, 1), F32),
                pltpu.VMEM((2, tq, 1), F32),
                pltpu.VMEM((2, tq, 2 * dv), F32),
            ],
        ),
        out_shape=jax.ShapeDtypeStruct((nb * l_len, d), BF16),
        compiler_params=_cparams(("parallel", "parallel")),
        name="diff_attention",
    )(lam, qkv, qkv, qkv, subln_g.reshape(1, LANES).astype(F32))


def _ret_rope_tables(l_len, ctx_len):
    half = RET_DK // 2
    inv = ROPE_BASE ** (-jnp.arange(0, RET_DK, 2, dtype=F32) / RET_DK)
    ang = jnp.arange(l_len, dtype=F32)[:, None] * inv[None, :]
    ang = jnp.concatenate([jnp.zeros((ctx_len, half), F32), ang], axis=0)
    cos64 = jnp.concatenate([jnp.cos(ang), jnp.cos(ang)], axis=1)
    sin64 = jnp.concatenate([-jnp.sin(ang), jnp.sin(ang)], axis=1)
    kscale = RET_DK ** -0.5
    cos = jnp.concatenate([cos64, cos64 * kscale], axis=1)
    sin = jnp.concatenate([sin64, sin64 * kscale], axis=1)
    return cos[None], sin[None]


def _attn_rope_tables(l_len, ctx_len):
    quarter = DIFF_DH // 4
    inv = ROPE_BASE ** (-jnp.arange(0, DIFF_DH // 2, 2, dtype=F32) / (DIFF_DH // 2))
    pos = jnp.arange(l_len)
    ang_r = (pos // GRID_W).astype(F32)[:, None] * inv[None, :]
    ang_c = (pos % GRID_W).astype(F32)[:, None] * inv[None, :]
    pad = lambda a: jnp.concatenate([jnp.zeros((ctx_len, quarter), F32), a], axis=0)
    ang_r, ang_c = pad(ang_r), pad(ang_c)
    cos64 = jnp.concatenate([jnp.cos(ang_r)] * 2 + [jnp.cos(ang_c)] * 2, axis=1)
    sin64 = jnp.concatenate([-jnp.sin(ang_r), jnp.sin(ang_r), -jnp.sin(ang_c), jnp.sin(ang_c)], axis=1)
    cos = jnp.concatenate([cos64, cos64], axis=1)
    sin = jnp.concatenate([sin64, sin64], axis=1)
    qscale = DIFF_DH ** -0.5 * math.log2(math.e)
    return jnp.stack([cos * qscale, cos]), jnp.stack([sin * qscale, sin])


def kernel(x, c, ctx, c_ctx, ada_w, ada_b, ln_g, w_in_ab, ret_decay_logit, s5_lam_re, s5_lam_im,
           s5_log_dt, s5_b_re, s5_b_im, s5_c_re, s5_c_im, s5_d, s5_w_glu, w_out_ab, w_in_c,
           diff_lambda, diff_subln_g, w_out_c, router_w, router_bias, exp_w_gate, exp_w_up,
           exp_w_down):
    nb, l_len, d = x.shape
    ctx_len = ctx.shape[1]
    t_len = ctx_len + l_len
    nt = nb * t_len
    tpb = t_len // ROW_TILE
    ctx_tiles = ctx_len // ROW_TILE
    assert d == D_MODEL and nb < SUBLANES
    assert l_len % (2 * ROW_TILE) == 0 and ctx_len % ROW_TILE == 0 and t_len % ATT_TK == 0

    xt = jnp.concatenate([ctx, x], axis=1).reshape(nt, d)
    c_all = jnp.concatenate([c, c_ctx[None].astype(c.dtype)], axis=0)
    c_pad = jnp.zeros((SUBLANES, d), F32).at[:nb + 1].set(c_all)
    mod = _adaln(c_pad, ada_w, ada_b)

    wr_t = router_w.T
    rbias = router_bias.reshape(N_EXPERTS, 1).astype(F32)

    w0 = w_in_ab[0]
    q_w, k_w, v_w, g_w, u_w = jnp.split(w0, (RET_QK_W, 2 * RET_QK_W, 2 * RET_QK_W + RET_V_W,
                                             2 * RET_QK_W + 2 * RET_V_W), axis=1)
    qk_w = jnp.concatenate([q_w.reshape(d, RET_HEADS, RET_DK), k_w.reshape(d, RET_HEADS, RET_DK)],
                           axis=2).reshape(d, 2 * RET_QK_W)
    w0p = jnp.concatenate([qk_w, v_w, g_w, u_w], axis=1).astype(BF16)
    cos0, sin0 = _ret_rope_tables(l_len, ctx_len)
    rope_tab0 = [0] * RET_HEADS + [None] * ((w0p.shape[1] - 2 * RET_QK_W) // LANES)
    proj0 = _inproj(xt, mod, 0, w0p, cos0, sin0, rope_tab0, RET_DK // 2,
                    nb=nb, t_len=t_len, ctx_len=ctx_len)

    log_gammas = jax.nn.log_sigmoid(ret_decay_logit[0].astype(F32))
    ret = _retention(proj0, log_gammas, nb=nb, t_len=t_len, ctx_len=ctx_len)

    tc = S5_CHUNK
    nc5 = t_len // tc
    u = proj0[:, 2 * RET_QK_W + 2 * RET_V_W:]
    xg = u.reshape(nb, nc5, tc, S5_G, S5_P).transpose(3, 1, 0, 2, 4)
    xg = jnp.pad(xg, ((0, 0), (0, 0), (0, SUBLANES - nb), (0, 0), (0, 0)))
    xg = xg.reshape(S5_G, nc5 * SUBLANES, tc * S5_P)
    w1, w2, a5 = _s5_operators(s5_lam_re[0], s5_lam_im[0], s5_log_dt[0], s5_b_re[0], s5_b_im[0],
                               s5_c_re[0], s5_c_im[0], s5_d[0])
    yg = _s5(xg, w1, w2, a5, nb=SUBLANES, nctx=ctx_len // tc)
    s5y = yg.reshape(S5_G, nc5, SUBLANES, tc, S5_P)[:, :, :nb].transpose(2, 1, 3, 0, 4).reshape(nt, S5_CH)

    x1, h2, eidx, gates = _merge0(ret, proj0, s5y, xt, mod, s5_w_glu[0].astype(BF16),
                                  w_out_ab[0].astype(BF16), ln_g[0, 0].reshape(1, d), wr_t, rbias,
                                  nb=nb, t_len=t_len, ctx_len=ctx_len)
    x2 = _moe_layer(x1, h2, eidx, gates, mod, 0, ln_g[0, 1].reshape(1, d),
                    exp_w_gate[0].astype(BF16), exp_w_up[0].astype(BF16), exp_w_down[0].astype(BF16),
                    tiles_per_batch=tpb, ctx_tiles=ctx_tiles, nb=nb)

    cos1, sin1 = _attn_rope_tables(l_len, ctx_len)
    n_heads_cols = D_MODEL // LANES
    rope_tab1 = [0] * n_heads_cols + [1] * n_heads_cols + [None] * n_heads_cols
    qkv = _inproj(x2, mod, 1, w_in_c[0].astype(BF16), cos1, sin1, rope_tab1, DIFF_DH // 4,
                  nb=nb, t_len=t_len, ctx_len=ctx_len)
    lf = diff_lambda[0].astype(F32)
    lambda_init = 0.8 - 0.6 * math.exp(-0.3 * 1)
    lam = (jnp.exp(jnp.sum(lf[0] * lf[1])) - jnp.exp(jnp.sum(lf[2] * lf[3])) + lambda_init).reshape(1)
    att = _diff_attention(qkv, lam, diff_subln_g[0], lambda_init,
                          nb=nb, l_len=l_len, t_len=t_len, ctx_len=ctx_len)
    x3, h3, eidx1, gates1 = _merge1(att, x2, mod, 1, w_out_c[0].astype(BF16), ln_g[1, 0].reshape(1, d),
                                    wr_t, rbias, nb=nb, l_len=l_len, t_len=t_len, ctx_len=ctx_len)
    out = _moe_layer(x3, h3, eidx1, gates1, mod, 1, ln_g[1, 1].reshape(1, d),
                     exp_w_gate[1].astype(BF16), exp_w_up[1].astype(BF16), exp_w_down[1].astype(BF16),
                     tiles_per_batch=l_len // ROW_TILE, ctx_tiles=0, nb=nb)
    return out.reshape(nb, l_len, d)
```

```python
import functools
import math

import jax
import jax.numpy as jnp
import numpy as np
from jax import lax
from jax.experimental import pallas as pl
from jax.experimental.pallas import tpu as pltpu

F32 = jnp.float32
BF16 = jnp.bfloat16

D_MODEL = 1024
DEPTH = 2
GRID_W = 64
ALPHA = (2.0 * DEPTH) ** 0.25
LN_EPS = 1e-5
GN_EPS = 1e-6
ROPE_BASE = 10000.0
RET_DK = 64
RET_DV = 128
RET_HEADS = 6
RET_QK_W = RET_HEADS * RET_DK
RET_V_W = RET_HEADS * RET_DV
S5_CH = 256
S5_P = 16
S5_G = 16
S5_N = 64
DIFF_HEADS = 8
DIFF_DH = 64
N_EXPERTS = 16
EXPERTS_PER_GROUP = 4
TOP_K = 2

LANES = 128
SUBLANES = 8
MXU_DIM = 256
ROW_TILE = 256
RET_CHUNK = 256
S5_CHUNK = 32
MOE_ROWS = 256
DISPATCH_TILE = 512
ROW_WORDS = D_MODEL // 2
ROW_SUB = ROW_WORDS // LANES
ATT_TK = 768
VMEM_LIMIT = 48 * 1024 * 1024


def _cparams(sem):
    return pltpu.CompilerParams(dimension_semantics=sem, vmem_limit_bytes=VMEM_LIMIT)


def _dot(a, b):
    return jnp.dot(a, b, preferred_element_type=F32)


def _dot_nt(a, b):
    return lax.dot_general(a, b, (((1,), (1,)), ((), ())), preferred_element_type=F32)


def _dot_tn(a, b):
    return lax.dot_general(a, b, (((0,), (0,)), ((), ())), preferred_element_type=F32)


def _split_bf16(x):
    hi = x.astype(BF16)
    lo = (x - hi.astype(F32)).astype(BF16)
    return hi, lo


def _dot3(a, b):
    ah, al = _split_bf16(a)
    bh, bl = _split_bf16(b)
    return _dot(ah, bh) + _dot(ah, bl) + _dot(al, bh)


def _dot3_nt(a, b):
    ah, al = _split_bf16(a)
    bh, bl = _split_bf16(b)
    return _dot_nt(ah, bh) + _dot_nt(ah, bl) + _dot_nt(al, bh)


def _sigmoid(x):
    return 1.0 / (1.0 + jnp.exp(-x))


def _silu(x):
    return x * _sigmoid(x)


def _pack_bf16_pairs(v):
    half = v.shape[1] // 2
    bits = lax.bitcast_convert_type(v.astype(BF16).astype(F32), jnp.uint32)
    return (bits[:, :half] >> 16) | (bits[:, half:] & jnp.uint32(0xFFFF0000))


def _unpack_bf16_pairs(p):
    lo = lax.bitcast_convert_type(p << 16, F32)
    hi = lax.bitcast_convert_type(p & jnp.uint32(0xFFFF0000), F32)
    return jnp.concatenate([lo, hi], axis=1)


def _store_rows(ref, packed):
    for j in range(ROW_SUB):
        ref[:, j, :] = packed[:, j * LANES:(j + 1) * LANES]


def _load_rows(ref):
    return jnp.concatenate([ref[:, j, :] for j in range(ROW_SUB)], axis=1)


def _gelu_tanh(x):
    c = math.sqrt(2.0 / math.pi)
    return 0.5 * x * (1.0 + jnp.tanh(c * (x + 0.044715 * (x * x * x))))


def _adaln_kernel(c_ref, w_ref, b_ref, o_ref):
    c = c_ref[...]
    o_ref[0] = _dot3(_silu(c), w_ref[0]) + b_ref[0]


def _adaln(c_pad, ada_w, ada_b):
    depth, d, n = ada_w.shape
    tn = 1536
    return pl.pallas_call(
        _adaln_kernel,
        grid=(depth, n // tn),
        in_specs=[
            pl.BlockSpec((SUBLANES, d), lambda i, j: (0, 0)),
            pl.BlockSpec((1, d, tn), lambda i, j: (i, 0, j)),
            pl.BlockSpec((1, 1, tn), lambda i, j: (i, 0, j)),
        ],
        out_specs=pl.BlockSpec((1, SUBLANES, tn), lambda i, j: (i, 0, j)),
        out_shape=jax.ShapeDtypeStruct((depth, SUBLANES, n), F32),
        compiler_params=_cparams(("parallel", "parallel")),
        name="adaln",
    )(c_pad, ada_w, ada_b.reshape(depth, 1, n))


def _mod_row(t, tiles_per_batch, ctx_tiles, nb):
    b = lax.div(t, tiles_per_batch)
    w = lax.rem(t, tiles_per_batch)
    return jnp.where(w < ctx_tiles, nb, b)


def _rope_block(a, cos, sin, half):
    lane = lax.broadcasted_iota(jnp.int32, a.shape, 1)
    first = lax.rem(lane, 2 * half) < half
    rot = jnp.where(first, pltpu.roll(a, LANES - half, 1), pltpu.roll(a, half, 1))
    return a * cos + rot * sin


def _inproj_kernel(x_ref, mod_ref, w_ref, cos_ref, sin_ref, o_ref, *, tiles_per_batch,
                   ctx_tiles, nb, rope_tab, rope_half):
    d = x_ref.shape[1]
    n = w_ref.shape[1]
    r = _mod_row(pl.program_id(0), tiles_per_batch, ctx_tiles, nb)
    sh = mod_ref[0, pl.ds(r, 1), 0:d]
    sc = mod_ref[0, pl.ds(r, 1), d:2 * d]
    xm = (x_ref[...] * (1.0 + sc) + sh).astype(BF16)
    for j in range(n // MXU_DIM):
        acc = _dot(xm, w_ref[:, j * MXU_DIM:(j + 1) * MXU_DIM])
        parts = []
        for s in range(MXU_DIM // LANES):
            blk = acc[:, s * LANES:(s + 1) * LANES]
            tab = rope_tab[j * (MXU_DIM // LANES) + s]
            if tab is not None:
                blk = _rope_block(blk, cos_ref[tab], sin_ref[tab], rope_half)
            parts.append(blk)
        o_ref[:, j * MXU_DIM:(j + 1) * MXU_DIM] = jnp.concatenate(parts, axis=1).astype(BF16)


def _inproj(x, mod, layer, w, cos, sin, rope_tab, rope_half, *, nb, t_len, ctx_len):
    nt, d = x.shape
    n = w.shape[1]
    tpb = t_len // ROW_TILE
    kern = functools.partial(_inproj_kernel, tiles_per_batch=tpb, ctx_tiles=ctx_len // ROW_TILE,
                             nb=nb, rope_tab=tuple(rope_tab), rope_half=rope_half)
    ntab = cos.shape[0]
    return pl.pallas_call(
        kern,
        grid=(nt // ROW_TILE,),
        in_specs=[
            pl.BlockSpec((ROW_TILE, d), lambda t: (t, 0)),
            pl.BlockSpec((1, SUBLANES, mod.shape[2]), lambda t: (layer, 0, 0)),
            pl.BlockSpec((d, n), lambda t: (0, 0)),
            pl.BlockSpec((ntab, ROW_TILE, LANES), lambda t: (0, lax.rem(t, tpb), 0)),
            pl.BlockSpec((ntab, ROW_TILE, LANES), lambda t: (0, lax.rem(t, tpb), 0)),
        ],
        out_specs=pl.BlockSpec((ROW_TILE, n), lambda t: (t, 0)),
        out_shape=jax.ShapeDtypeStruct((nt, n), BF16),
        compiler_params=_cparams(("parallel",)),
        name="inproj%d" % layer,
    )(x, mod, w, cos, sin)


def _retention_kernel(lg_ref, qk_ref, v_ref, o_ref, sf_ref, sb_ref, *, nctx):
    c_len = RET_CHUNK
    t_len = qk_ref.shape[0]
    nc = t_len // c_len
    h = pl.program_id(1)
    lgf = lg_ref[0, h]
    lgb = lg_ref[1, h]
    ii = lax.broadcasted_iota(jnp.int32, (c_len, 1), 0).astype(F32)
    jj = lax.broadcasted_iota(jnp.int32, (1, c_len), 1).astype(F32)
    diff = ii - jj
    decay = jnp.where(diff >= 0.0, jnp.exp(lgf * jnp.maximum(diff, 0.0)),
                      jnp.exp(lgb * jnp.maximum(-diff, 0.0)))
    kdf = jnp.exp(lgf * (c_len - 1.0 - ii))
    kdb = jnp.exp(lgb * ii)
    qdf = jnp.exp(lgf * (ii + 1.0))
    qdb = jnp.exp(lgb * (c_len - ii))
    zrow = jnp.zeros((1, RET_DV), F32)
    gf_chunk = jnp.exp(zrow + lgf * c_len)
    gb_chunk = jnp.exp(zrow + lgb * c_len)

    def load(c):
        rows = pl.ds(pl.multiple_of(c * c_len, c_len), c_len)
        qk = qk_ref[rows, :].astype(F32)
        return qk[:, :RET_DK], qk[:, RET_DK:], v_ref[rows, :]

    def fwd_state(c, s):
        sf_ref[c] = s
        _, k, v = load(c)
        return gf_chunk * s + _dot_tn((k * kdf).astype(BF16), v)

    lax.fori_loop(0, nc, fwd_state, jnp.zeros((RET_DK, RET_DV), F32))

    def bwd_state(j, s):
        c = jnp.where(j < nctx, nctx - 1 - j, nc - 1 - (j - nctx))
        sb_ref[c] = s
        _, k, v = load(c)
        return gb_chunk * s + _dot_tn((k * kdb).astype(BF16), v)

    lax.fori_loop(0, nc, bwd_state, jnp.zeros((RET_DK, RET_DV), F32))

    def out_chunk(c, carry):
        q, k, v = load(c)
        scores = _dot_nt(q.astype(BF16), k.astype(BF16)) * decay
        o = _dot(scores.astype(BF16), v)
        o = o + _dot((q * qdf).astype(BF16), sf_ref[c].astype(BF16))
        o = o + _dot((q * qdb).astype(BF16), sb_ref[c].astype(BF16))
        mu = jnp.mean(o, axis=-1, keepdims=True)
        oc = o - mu
        var = jnp.mean(oc * oc, axis=-1, keepdims=True)
        rows = pl.ds(pl.multiple_of(c * c_len, c_len), c_len)
        o_ref[rows, :] = (oc * lax.rsqrt(var + GN_EPS)).astype(BF16)
        return carry

    lax.fori_loop(0, nc, out_chunk, 0)


def _retention(proj, log_gammas, *, nb, t_len, ctx_len):
    nt = proj.shape[0]
    nc = t_len // RET_CHUNK
    kern = functools.partial(_retention_kernel, nctx=ctx_len // RET_CHUNK)
    vcol0 = RET_HEADS
    return pl.pallas_call(
        kern,
        grid_spec=pltpu.PrefetchScalarGridSpec(
            num_scalar_prefetch=1,
            grid=(nb, RET_HEADS),
            in_specs=[
                pl.BlockSpec((t_len, LANES), lambda b, h, lg: (b, h)),
                pl.BlockSpec((t_len, LANES), lambda b, h, lg: (b, vcol0 + h)),
            ],
            out_specs=pl.BlockSpec((t_len, LANES), lambda b, h, lg: (b, h)),
            scratch_shapes=[pltpu.VMEM((nc, RET_DK, RET_DV), F32),
                            pltpu.VMEM((nc, RET_DK, RET_DV), F32)],
        ),
        out_shape=jax.ShapeDtypeStruct((nt, RET_V_W), BF16),
        compiler_params=_cparams(("parallel", "parallel")),
        name="retention",
    )(log_gammas, proj, proj)


def _s5_operators(lam_re, lam_im, log_dt, b_re, b_im, c_re, c_im, d_skip):
    tc = S5_CHUNK
    hp = lax.Precision.HIGHEST
    ks = jnp.arange(tc + 1, dtype=F32)
    pw, bbar, cm = [], [], []
    for direction in range(2):
        dt = jnp.exp(log_dt[direction].astype(F32))[:, None]
        lam = lax.complex(lam_re[direction].astype(F32), lam_im[direction].astype(F32))
        z = lam * dt
        p = jnp.exp(z[None] * ks[:, None, None])
        lam_bar = p[1]
        bb = ((lam_bar - 1.0) / lam)[..., None] * lax.complex(
            b_re[direction].astype(F32), b_im[direction].astype(F32))
        pw.append(p)
        bbar.append(bb)
        cm.append(lax.complex(c_re[direction].astype(F32), c_im[direction].astype(F32)))

    def lag_kernel(p, bb, c):
        return jnp.einsum('gpn,kgn,gnq->kgpq', c, p[:tc], bb, precision=hp).real

    kf = lag_kernel(pw[0], bbar[0], cm[0])
    kb = lag_kernel(pw[1], bbar[1], cm[1])
    k0 = kf[0] + kb[0] + jnp.eye(S5_P, dtype=F32)[None] * d_skip.astype(F32)[:, :, None]
    kcat = jnp.concatenate([kb[1:][::-1], k0[None], kf[1:]], axis=0)
    s_idx = jnp.arange(tc)[:, None]
    t_idx = jnp.arange(tc)[None, :]
    m5 = kcat[t_idx - s_idx + tc - 1]
    intra = m5.transpose(2, 0, 4, 1, 3).reshape(S5_G, tc * S5_P, tc * S5_P)

    ef = pw[0][:tc][::-1][:, :, :, None] * bbar[0][None]
    eb = pw[1][:tc][:, :, :, None] * bbar[1][None]
    to_in = lambda e: e.transpose(1, 0, 3, 2).reshape(S5_G, tc * S5_P, S5_N)
    w1 = jnp.concatenate([intra, to_in(ef.real), to_in(ef.imag), to_in(eb.real), to_in(eb.imag)],
                         axis=2)
    of = cm[0][None] * pw[0][1:][:, :, None, :]
    ob = cm[1][None] * pw[1][1:][::-1][:, :, None, :]
    to_out = lambda o: o.transpose(1, 3, 0, 2).reshape(S5_G, S5_N, tc * S5_P)
    w2 = jnp.concatenate([to_out(of.real), -to_out(of.imag), to_out(ob.real), -to_out(ob.imag)],
                         axis=1)
    a = jnp.stack([pw[0][tc].real, pw[0][tc].imag, pw[1][tc].real, pw[1][tc].imag], axis=1)
    return w1.astype(BF16), w2.astype(BF16), a


def _s5_kernel(x_ref, w1_ref, w2_ref, a_ref, y_ref, e_ref, st_ref, *, nb, nctx):
    n = S5_N
    w = x_ref.shape[2]
    rows = x_ref.shape[1]
    nc = rows // nb
    x = x_ref[0]
    e_ref[...] = _dot(x, w1_ref[0, :, w:])
    afr = a_ref[0, 0:1, :]
    afi = a_ref[0, 1:2, :]
    abr = a_ref[0, 2:3, :]
    abi = a_ref[0, 3:4, :]

    def step(j, carry):
        fr, fi, br, bi = carry
        rf = pl.ds(pl.multiple_of(j * nb, nb), nb)
        cb = jnp.where(j < nctx, nctx - 1 - j, nc - 1 - (j - nctx))
        rb = pl.ds(pl.multiple_of(cb * nb, nb), nb)
        st_ref[rf, 0:n] = fr
        st_ref[rf, n:2 * n] = fi
        st_ref[rb, 2 * n:3 * n] = br
        st_ref[rb, 3 * n:4 * n] = bi
        nfr = afr * fr - afi * fi + e_ref[rf, 0:n]
        nfi = afr * fi + afi * fr + e_ref[rf, n:2 * n]
        nbr = abr * br - abi * bi + e_ref[rb, 2 * n:3 * n]
        nbi = abr * bi + abi * br + e_ref[rb, 3 * n:4 * n]
        return nfr, nfi, nbr, nbi

    z = jnp.zeros((nb, n), F32)
    lax.fori_loop(0, nc, step, (z, z, z, z))
    y_ref[0] = _dot(x, w1_ref[0, :, :w]) + _dot(st_ref[...].astype(BF16), w2_ref[0])


def _s5(xg, w1, w2, a, *, nb, nctx):
    g, rows, w = xg.shape
    kern = functools.partial(_s5_kernel, nb=nb, nctx=nctx)
    return pl.pallas_call(
        kern,
        grid=(g,),
        in_specs=[
            pl.BlockSpec((1, rows, w), lambda i: (i, 0, 0)),
            pl.BlockSpec((1, w, w + 4 * S5_N), lambda i: (i, 0, 0)),
            pl.BlockSpec((1, 4 * S5_N, w), lambda i: (i, 0, 0)),
            pl.BlockSpec((1, 4, S5_N), lambda i: (i, 0, 0)),
        ],
        out_specs=pl.BlockSpec((1, rows, w), lambda i: (i, 0, 0)),
        out_shape=jax.ShapeDtypeStruct((g, rows, w), F32),
        scratch_shapes=[pltpu.VMEM((rows, 4 * S5_N), F32), pltpu.VMEM((rows, 4 * S5_N), F32)],
        compiler_params=_cparams(("parallel",)),
        name="s5",
    )(xg, w1, w2, a)


def _route(logits_t, bias):
    scores = _sigmoid(logits_t)
    biased = scores + bias
    s_rows = [scores[e:e + 1, :] for e in range(N_EXPERTS)]
    b_rows = [biased[e:e + 1, :] for e in range(N_EXPERTS)]
    n_groups = N_EXPERTS // EXPERTS_PER_GROUP
    best = None
    sel = None
    for g in range(n_groups):
        a, b, c, d = b_rows[4 * g:4 * g + 4]
        hi1, lo1 = jnp.maximum(a, b), jnp.minimum(a, b)
        hi2, lo2 = jnp.maximum(c, d), jnp.minimum(c, d)
        top1 = jnp.maximum(hi1, hi2)
        top2 = jnp.maximum(jnp.minimum(hi1, hi2), jnp.maximum(lo1, lo2))
        gs = top1 + top2
        if g == 0:
            best, sel = gs, jnp.zeros(gs.shape, jnp.int32)
        else:
            better = gs > best
            sel = jnp.where(better, g, sel)
            best = jnp.where(better, gs, best)
    neg = jnp.full(best.shape, -jnp.inf, F32)
    masked = [jnp.where(sel == (e // EXPERTS_PER_GROUP), b_rows[e], neg) for e in range(N_EXPERTS)]
    v1, i1, g1 = masked[0], jnp.zeros(best.shape, jnp.int32), s_rows[0]
    for e in range(1, N_EXPERTS):
        better = masked[e] > v1
        v1 = jnp.where(better, masked[e], v1)
        i1 = jnp.where(better, e, i1)
        g1 = jnp.where(better, s_rows[e], g1)
    v2, i2, g2 = neg, jnp.zeros(best.shape, jnp.int32), jnp.zeros(best.shape, F32)
    for e in range(N_EXPERTS):
        cand = jnp.where(i1 == e, neg, masked[e])
        better = cand > v2
        v2 = jnp.where(better, cand, v2)
        i2 = jnp.where(better, e, i2)
        g2 = jnp.where(better, s_rows[e], g2)
    tot = g1 + g2
    return jnp.concatenate([i1, i2], axis=0), jnp.concatenate([g1 / tot, g2 / tot], axis=0)


def _tail(x, o, mod_ref, r, lng, wr, rb, x1_ref, h2_ref, ei_ref, gt_ref):
    d = x.shape[1]
    g1 = mod_ref[0, pl.ds(r, 1), 2 * d:3 * d]
    sh2 = mod_ref[0, pl.ds(r, 1), 3 * d:4 * d]
    sc2 = mod_ref[0, pl.ds(r, 1), 4 * d:5 * d]
    y = ALPHA * x + g1 * o
    mu = jnp.mean(y, axis=-1, keepdims=True)
    yc = y - mu
    var = jnp.mean(yc * yc, axis=-1, keepdims=True)
    x1 = yc * lax.rsqrt(var + LN_EPS) * lng
    h2 = x1 * (1.0 + sc2) + sh2
    x1_ref[...] = x1
    _store_rows(h2_ref, _pack_bf16_pairs(h2))
    ei, gt = _route(_dot3_nt(wr, h2), rb)
    ei_ref[...] = ei
    gt_ref[...] = gt


def _merge0_kernel(r_ref, g_ref, s_ref, x_ref, mod_ref, wglu_ref, wout_ref, lng_ref, wr_ref,
                   rb_ref, x1_ref, h2_ref, ei_ref, gt_ref, *, tiles_per_batch, ctx_tiles, nb):
    r = _mod_row(pl.program_id(0), tiles_per_batch, ctx_tiles, nb)
    ret = r_ref[...].astype(F32) * _silu(g_ref[...].astype(F32))
    z = _dot(_gelu_tanh(s_ref[...]).astype(BF16), wglu_ref[...])
    zz = z[:, :S5_CH] * _sigmoid(z[:, S5_CH:])
    o = _dot(ret.astype(BF16), wout_ref[0:RET_V_W, :]) + _dot(zz.astype(BF16), wout_ref[RET_V_W:, :])
    _tail(x_ref[...], o, mod_ref, r, lng_ref[...], wr_ref[...], rb_ref[...],
          x1_ref, h2_ref, ei_ref, gt_ref)


def _merge1_kernel(a_ref, x_ref, mod_ref, wout_ref, lng_ref, wr_ref, rb_ref,
                   x1_ref, h2_ref, ei_ref, gt_ref, *, tiles_per_batch):
    r = lax.div(pl.program_id(0), tiles_per_batch)
    o = _dot(a_ref[...], wout_ref[...])
    _tail(x_ref[...], o, mod_ref, r, lng_ref[...], wr_ref[...], rb_ref[...],
          x1_ref, h2_ref, ei_ref, gt_ref)


def _tail_outs(n_rows, d):
    shapes = (jax.ShapeDtypeStruct((n_rows, d), F32),
              jax.ShapeDtypeStruct((n_rows, ROW_SUB, LANES), jnp.uint32),
              jax.ShapeDtypeStruct((TOP_K, n_rows), jnp.int32),
              jax.ShapeDtypeStruct((TOP_K, n_rows), F32))
    specs = (pl.BlockSpec((ROW_TILE, d), lambda t: (t, 0)),
             pl.BlockSpec((ROW_TILE, ROW_SUB, LANES), lambda t: (t, 0, 0)),
             pl.BlockSpec((TOP_K, ROW_TILE), lambda t: (0, t)),
             pl.BlockSpec((TOP_K, ROW_TILE), lambda t: (0, t)))
    return shapes, specs


def _merge0(ret, proj, s5y, x, mod, w_glu, w_out, lng, wr_t, rbias, *, nb, t_len, ctx_len):
    nt, d = x.shape
    tpb = t_len // ROW_TILE
    kern = functools.partial(_merge0_kernel, tiles_per_batch=tpb, ctx_tiles=ctx_len // ROW_TILE, nb=nb)
    shapes, specs = _tail_outs(nt, d)
    gcol = (2 * RET_QK_W + RET_V_W) // RET_V_W
    full = lambda a: pl.BlockSpec(a.shape, lambda t: (0,) * a.ndim)
    return pl.pallas_call(
        kern,
        grid=(nt // ROW_TILE,),
        in_specs=[
            pl.BlockSpec((ROW_TILE, RET_V_W), lambda t: (t, 0)),
            pl.BlockSpec((ROW_TILE, RET_V_W), lambda t: (t, gcol)),
            pl.BlockSpec((ROW_TILE, S5_CH), lambda t: (t, 0)),
            pl.BlockSpec((ROW_TILE, d), lambda t: (t, 0)),
            pl.BlockSpec((1, SUBLANES, mod.shape[2]), lambda t: (0, 0, 0)),
            full(w_glu), full(w_out), full(lng), full(wr_t), full(rbias),
        ],
        out_specs=specs,
        out_shape=shapes,
        compiler_params=_cparams(("parallel",)),
        name="merge0",
    )(ret, proj, s5y, x, mod, w_glu, w_out, lng, wr_t, rbias)


def _merge1(att, x, mod, layer, w_out, lng, wr_t, rbias, *, nb, l_len, t_len, ctx_len):
    n_lat, d = att.shape
    tpb = l_len // ROW_TILE
    tpb_t = t_len // ROW_TILE
    ctx_tiles = ctx_len // ROW_TILE
    kern = functools.partial(_merge1_kernel, tiles_per_batch=tpb)
    shapes, specs = _tail_outs(n_lat, d)
    full = lambda a: pl.BlockSpec(a.shape, lambda t: (0,) * a.ndim)
    xrow = lambda t: (lax.div(t, tpb) * tpb_t + ctx_tiles + lax.rem(t, tpb), 0)
    return pl.pallas_call(
        kern,
        grid=(n_lat // ROW_TILE,),
        in_specs=[
            pl.BlockSpec((ROW_TILE, d), lambda t: (t, 0)),
            pl.BlockSpec((ROW_TILE, d), xrow),
            pl.BlockSpec((1, SUBLANES, mod.shape[2]), lambda t: (layer, 0, 0)),
            full(w_out), full(lng), full(wr_t), full(rbias),
        ],
        out_specs=specs,
        out_shape=shapes,
        compiler_params=_cparams(("parallel",)),
        name="merge1",
    )(att, x, mod, w_out, lng, wr_t, rbias)


def _moe_plan(eidx):
    k, n = eidx.shape
    a = k * n
    e_flat = eidx.reshape(a)
    onehot = (e_flat[:, None] == jnp.arange(N_EXPERTS, dtype=jnp.int32)[None, :]).astype(jnp.int32)
    csum = jnp.cumsum(onehot, axis=0)
    counts = csum[-1]
    rank = jnp.take_along_axis(csum, e_flat[:, None], axis=1)[:, 0] - 1
    padded = (counts + MOE_ROWS - 1) // MOE_ROWS * MOE_ROWS
    pad_end = jnp.cumsum(padded)
    pad_start = pad_end - padded
    dest = (pad_start[e_flat] + rank).astype(jnp.int32)
    n_blocks = -(-(a + N_EXPERTS * (MOE_ROWS - 1)) // MOE_ROWS)
    block_expert = jnp.minimum(
        jnp.searchsorted(pad_end, jnp.arange(n_blocks, dtype=jnp.int32) * MOE_ROWS, side='right'),
        N_EXPERTS - 1).astype(jnp.int32)
    return dest.reshape(k, n), block_expert, n_blocks


def _tile_rows_of(dest, tile):
    k, n = dest.shape
    return dest.reshape(k, n // tile, tile).transpose(1, 0, 2).reshape(n // tile, 1, k * tile)


def _dispatch_kernel(dest_ref, h_ref, xs_in_hbm, xs_hbm, sem):
    del xs_in_hbm
    rows = h_ref.shape[0]

    def start(r, c):
        for choice in range(TOP_K):
            pltpu.make_async_copy(h_ref.at[r], xs_hbm.at[dest_ref[0, 0, choice * rows + r]], sem).start()
        return c

    lax.fori_loop(0, rows, start, 0, unroll=8)
    for _ in range(TOP_K):
        pltpu.make_async_copy(h_ref, xs_hbm.at[pl.ds(0, rows)], sem).wait()


def _dispatch(h, dest, n_rows):
    n = h.shape[0]
    tile = DISPATCH_TILE if n % DISPATCH_TILE == 0 else ROW_TILE
    return pl.pallas_call(
        _dispatch_kernel,
        grid=(n // tile,),
        in_specs=[
            pl.BlockSpec((1, 1, TOP_K * tile), lambda t: (t, 0, 0), memory_space=pltpu.SMEM),
            pl.BlockSpec((tile, ROW_SUB, LANES), lambda t: (t, 0, 0)),
            pl.BlockSpec(memory_space=pl.ANY),
        ],
        out_specs=pl.BlockSpec(memory_space=pl.ANY),
        out_shape=jax.ShapeDtypeStruct((n_rows, ROW_SUB, LANES), jnp.uint32),
        scratch_shapes=[pltpu.SemaphoreType.DMA],
        input_output_aliases={2: 0},
        compiler_params=_cparams(("arbitrary",)),
        name="moe_dispatch",
    )(_tile_rows_of(dest, tile), h, jnp.zeros((n_rows, ROW_SUB, LANES), jnp.uint32))


def _experts_kernel(be_ref, x_ref, wg_ref, wu_ref, wd_ref, o_ref):
    x = _unpack_bf16_pairs(_load_rows(x_ref)).astype(BF16)
    hg = _dot(x, wg_ref[0])
    hu = _dot(x, wu_ref[0])
    _store_rows(o_ref, _pack_bf16_pairs(_dot((_silu(hg) * hu).astype(BF16), wd_ref[0])))


def _experts(xs, block_expert, wg, wu, wd):
    n_blocks = block_expert.shape[0]
    d, dff = wg.shape[1], wg.shape[2]
    rows_spec = pl.BlockSpec((MOE_ROWS, ROW_SUB, LANES), lambda i, be: (i, 0, 0))
    return pl.pallas_call(
        _experts_kernel,
        grid_spec=pltpu.PrefetchScalarGridSpec(
            num_scalar_prefetch=1,
            grid=(n_blocks,),
            in_specs=[
                rows_spec,
                pl.BlockSpec((1, d, dff), lambda i, be: (be[i], 0, 0)),
                pl.BlockSpec((1, d, dff), lambda i, be: (be[i], 0, 0)),
                pl.BlockSpec((1, dff, d), lambda i, be: (be[i], 0, 0)),
            ],
            out_specs=rows_spec,
        ),
        out_shape=jax.ShapeDtypeStruct(xs.shape, jnp.uint32),
        compiler_params=_cparams(("parallel",)),
        name="moe_experts",
    )(block_expert, xs, wg, wu, wd)


def _combine_kernel(dcur_ref, dnxt_ref, x_ref, gt_ref, mod_ref, lng_ref, y_hbm, o_ref, ybuf, sem, *,
                    tiles_per_batch, ctx_tiles, nb):
    t = pl.program_id(0)
    last = pl.num_programs(0) - 1
    rows = x_ref.shape[0]
    n = TOP_K * rows
    d = x_ref.shape[1]
    slot = lax.rem(t, 2)

    def start_all(idx_ref, s):
        def start(j, c):
            pltpu.make_async_copy(y_hbm.at[idx_ref[0, 0, j]], ybuf.at[s, j], sem.at[s]).start()
            return c

        lax.fori_loop(0, n, start, 0, unroll=8)

    @pl.when(t == 0)
    def _():
        start_all(dcur_ref, 0)

    @pl.when(t < last)
    def _():
        start_all(dnxt_ref, 1 - slot)

    pltpu.make_async_copy(y_hbm.at[pl.ds(0, n)], ybuf.at[slot], sem.at[slot]).wait()
    r = _mod_row(t, tiles_per_batch, ctx_tiles, nb)
    gt = gt_ref[...]
    yb = ybuf.at[slot]
    y = (_unpack_bf16_pairs(_load_rows(yb.at[pl.ds(0, rows)])) * gt[:, 0:1]
         + _unpack_bf16_pairs(_load_rows(yb.at[pl.ds(rows, rows)])) * gt[:, 1:2])
    g2 = mod_ref[0, pl.ds(r, 1), 5 * d:6 * d]
    z = ALPHA * x_ref[...] + g2 * y
    mu = jnp.mean(z, axis=-1, keepdims=True)
    zc = z - mu
    var = jnp.mean(zc * zc, axis=-1, keepdims=True)
    o_ref[...] = zc * lax.rsqrt(var + LN_EPS) * lng_ref[...]


def _combine(x1, dest, gates, mod, layer, lng, ys, *, tiles_per_batch, ctx_tiles, nb):
    n, d = x1.shape
    nt = n // ROW_TILE
    dest_t = _tile_rows_of(dest, ROW_TILE)
    kern = functools.partial(_combine_kernel, tiles_per_batch=tiles_per_batch, ctx_tiles=ctx_tiles, nb=nb)
    idx_spec = lambda f: pl.BlockSpec((1, 1, TOP_K * ROW_TILE), f, memory_space=pltpu.SMEM)
    return pl.pallas_call(
        kern,
        grid=(nt,),
        in_specs=[
            idx_spec(lambda t: (t, 0, 0)),
            idx_spec(lambda t: (jnp.minimum(t + 1, nt - 1), 0, 0)),
            pl.BlockSpec((ROW_TILE, d), lambda t: (t, 0)),
            pl.BlockSpec((ROW_TILE, TOP_K), lambda t: (t, 0)),
            pl.BlockSpec((1, SUBLANES, mod.shape[2]), lambda t: (layer, 0, 0)),
            pl.BlockSpec((1, d), lambda t: (0, 0)),
            pl.BlockSpec(memory_space=pl.ANY),
        ],
        out_specs=pl.BlockSpec((ROW_TILE, d), lambda t: (t, 0)),
        out_shape=jax.ShapeDtypeStruct((n, d), F32),
        scratch_shapes=[pltpu.VMEM((2, TOP_K * ROW_TILE, ROW_SUB, LANES), jnp.uint32),
                        pltpu.SemaphoreType.DMA((2,))],
        compiler_params=_cparams(("arbitrary",)),
        name="moe_combine%d" % layer,
    )(dest_t, dest_t, x1, gates.T, mod, lng, ys)


def _moe_layer(x1, h2, eidx, gates, mod, layer, lng, wg, wu, wd, *, tiles_per_batch, ctx_tiles, nb):
    dest, block_expert, n_blocks = _moe_plan(eidx)
    xs = _dispatch(h2, dest, n_blocks * MOE_ROWS)
    ys = _experts(xs, block_expert, wg, wu, wd)
    return _combine(x1, dest, gates, mod, layer, lng, ys,
                    tiles_per_batch=tiles_per_batch, ctx_tiles=ctx_tiles, nb=nb)


def _attn_kernel(lam_ref, q_ref, k_ref, v_ref, g_ref, o_ref, vext, s_buf0, s_buf1,
                 p_buf0, p_buf1, corr_buf0, corr_buf1, m_buf, acc, *, out_scale, ctx_len):
    t_len = k_ref.shape[0]
    nk = t_len // ATT_TK
    dv = v_ref.shape[1]
    tq = acc.shape[1]
    n_tiles = (o_ref.shape[0] // tq) * nk

    vext[:, 0:dv] = v_ref[...]
    vext[:, dv:2 * dv] = jnp.ones((t_len, dv), BF16)
    lam = lam_ref[0]
    gain = g_ref[...] * out_scale

    s_bufs, p_bufs, corr_bufs = (s_buf0, s_buf1), (p_buf0, p_buf1), (corr_buf0, corr_buf1)

    def key_rows(kj):
        return pl.ds(pl.multiple_of(kj * ATT_TK, ATT_TK), ATT_TK)

    def advance(tile):
        qi, kj = tile
        wrap = kj + 1 == nk
        return jnp.where(wrap, qi + 1, qi), jnp.where(wrap, 0, kj + 1)

    def scores(tile, slot):
        qi, kj = tile
        q = q_ref[pl.ds(pl.multiple_of(ctx_len + qi * tq, ROW_TILE), tq), :]
        lane = lax.broadcasted_iota(jnp.int32, q.shape, 1)
        zero = jnp.zeros(q.shape, q.dtype)
        k = k_ref[key_rows(kj), :]
        s_bufs[slot][0] = _dot_nt(jnp.where(lane < DIFF_DH, q, zero), k)
        s_bufs[slot][1] = _dot_nt(jnp.where(lane >= DIFF_DH, q, zero), k)

    def numerators(tile, slot):
        _, kj = tile
        for w in range(2):
            s = s_bufs[slot][w]
            m_old = jnp.where(kj == 0, -jnp.inf, m_buf[w])
            m_new = jnp.maximum(m_old, jnp.max(s, axis=-1, keepdims=True))
            p_bufs[slot][w] = jnp.exp2(s - m_new).astype(BF16)
            corr_bufs[slot][w] = jnp.exp2(m_old - m_new)
            m_buf[w] = m_new

    def values(tile, slot):
        qi, kj = tile
        ve = vext[key_rows(kj), :]
        a = []
        for w in range(2):
            a.append(corr_bufs[slot][w] * acc[w] + _dot(p_bufs[slot][w], ve))
            acc[w] = a[w]
        o = a[0][:, 0:dv] / a[0][:, dv:2 * dv] - lam * (a[1][:, 0:dv] / a[1][:, dv:2 * dv])
        o = o * lax.rsqrt(jnp.mean(o * o, axis=-1, keepdims=True) + GN_EPS)
        o_ref[pl.ds(pl.multiple_of(qi * tq, tq), tq), :] = (o * gain).astype(BF16)

    def step(tiles, slot):
        a, b, c = tiles
        values(c, slot)
        scores(a, slot)
        numerators(b, 1 - slot)
        return advance(a), a, b

    acc[...] = jnp.zeros(acc.shape, F32)
    t0 = (jnp.int32(0), jnp.int32(0))
    t1 = advance(t0)
    scores(t0, 0)
    scores(t1, 1)
    numerators(t0, 0)

    def pair(_, tiles):
        return step(step(tiles, 0), 1)

    _, last, prev = lax.fori_loop(0, (n_tiles - 2) // 2, pair, (advance(t1), t1, t0))
    numerators(last, (n_tiles - 1) % 2)
    values(prev, n_tiles % 2)
    values(last, (n_tiles - 1) % 2)


def _diff_attention(qkv, lam, subln_g, lambda_init, *, nb, l_len, t_len, ctx_len):
    d = D_MODEL
    tq = 2 * ROW_TILE
    dv = 2 * DIFF_DH
    assert ((l_len // tq) * (t_len // ATT_TK)) % 2 == 0
    kern = functools.partial(_attn_kernel, out_scale=1.0 - lambda_init, ctx_len=ctx_len)
    return pl.pallas_call(
        kern,
        grid_spec=pltpu.PrefetchScalarGridSpec(
            num_scalar_prefetch=1,
            grid=(nb, DIFF_HEADS),
            in_specs=[
                pl.BlockSpec((t_len, LANES), lambda b, h, lam: (b, h)),
                pl.BlockSpec((t_len, LANES), lambda b, h, lam: (b, DIFF_HEADS + h)),
                pl.BlockSpec((t_len, LANES), lambda b, h, lam: (b, 2 * DIFF_HEADS + h)),
                pl.BlockSpec((1, LANES), lambda b, h, lam: (0, 0)),
            ],
            out_specs=pl.BlockSpec((l_len, LANES), lambda b, h, lam: (b, h)),
            scratch_shapes=[
                pltpu.VMEM((t_len, 2 * dv), BF16),
                pltpu.VMEM((2, tq, ATT_TK), F32), pltpu.VMEM((2, tq, ATT_TK), F32),
                pltpu.VMEM((2, tq, ATT_TK), BF16), pltpu.VMEM((2, tq, ATT_TK), BF16),
                pltpu.VMEM((2, tq, 1), F32), pltpu.VMEM((2, tq, 1), F32),
                pltpu.VMEM((2, tq, 1), F32),
                pltpu.VMEM((2, tq, 2 * dv), F32),
            ],
        ),
        out_shape=jax.ShapeDtypeStruct((nb * l_len, d), BF16),
        compiler_params=_cparams(("parallel", "parallel")),
        name="diff_attention",
    )(lam, qkv, qkv, qkv, subln_g.reshape(1, LANES).astype(F32))


def _ret_rope_tables(l_len, ctx_len):
    half = RET_DK // 2
    inv = ROPE_BASE ** (-jnp.arange(0, RET_DK, 2, dtype=F32) / RET_DK)
    ang = jnp.arange(l_len, dtype=F32)[:, None] * inv[None, :]
    ang = jnp.concatenate([jnp.zeros((ctx_len, half), F32), ang], axis=0)
    cos64 = jnp.concatenate([jnp.cos(ang), jnp.cos(ang)], axis=1)
    sin64 = jnp.concatenate([-jnp.sin(ang), jnp.sin(ang)], axis=1)
    kscale = RET_DK ** -0.5
    cos = jnp.concatenate([cos64, cos64 * kscale], axis=1)
    sin = jnp.concatenate([sin64, sin64 * kscale], axis=1)
    return cos[None], sin[None]


def _attn_rope_tables(l_len, ctx_len):
    quarter = DIFF_DH // 4
    inv = ROPE_BASE ** (-jnp.arange(0, DIFF_DH // 2, 2, dtype=F32) / (DIFF_DH // 2))
    pos = jnp.arange(l_len)
    ang_r = (pos // GRID_W).astype(F32)[:, None] * inv[None, :]
    ang_c = (pos % GRID_W).astype(F32)[:, None] * inv[None, :]
    pad = lambda a: jnp.concatenate([jnp.zeros((ctx_len, quarter), F32), a], axis=0)
    ang_r, ang_c = pad(ang_r), pad(ang_c)
    cos64 = jnp.concatenate([jnp.cos(ang_r)] * 2 + [jnp.cos(ang_c)] * 2, axis=1)
    sin64 = jnp.concatenate([-jnp.sin(ang_r), jnp.sin(ang_r), -jnp.sin(ang_c), jnp.sin(ang_c)], axis=1)
    cos = jnp.concatenate([cos64, cos64], axis=1)
    sin = jnp.concatenate([sin64, sin64], axis=1)
    qscale = DIFF_DH ** -0.5 * math.log2(math.e)
    return jnp.stack([cos * qscale, cos]), jnp.stack([sin * qscale, sin])


def kernel(x, c, ctx, c_ctx, ada_w, ada_b, ln_g, w_in_ab, ret_decay_logit, s5_lam_re, s5_lam_im,
           s5_log_dt, s5_b_re, s5_b_im, s5_c_re, s5_c_im, s5_d, s5_w_glu, w_out_ab, w_in_c,
           diff_lambda, diff_subln_g, w_out_c, router_w, router_bias, exp_w_gate, exp_w_up,
           exp_w_down):
    nb, l_len, d = x.shape
    ctx_len = ctx.shape[1]
    t_len = ctx_len + l_len
    nt = nb * t_len
    tpb = t_len // ROW_TILE
    ctx_tiles = ctx_len // ROW_TILE
    assert d == D_MODEL and nb < SUBLANES
    assert l_len % (2 * ROW_TILE) == 0 and ctx_len % ROW_TILE == 0 and t_len % ATT_TK == 0

    xt = jnp.concatenate([ctx, x], axis=1).reshape(nt, d)
    c_all = jnp.concatenate([c, c_ctx[None].astype(c.dtype)], axis=0)
    c_pad = jnp.zeros((SUBLANES, d), F32).at[:nb + 1].set(c_all)
    mod = _adaln(c_pad, ada_w, ada_b)

    wr_t = router_w.T
    rbias = router_bias.reshape(N_EXPERTS, 1).astype(F32)

    w0 = w_in_ab[0]
    q_w, k_w, v_w, g_w, u_w = jnp.split(w0, (RET_QK_W, 2 * RET_QK_W, 2 * RET_QK_W + RET_V_W,
                                             2 * RET_QK_W + 2 * RET_V_W), axis=1)
    qk_w = jnp.concatenate([q_w.reshape(d, RET_HEADS, RET_DK), k_w.reshape(d, RET_HEADS, RET_DK)],
                           axis=2).reshape(d, 2 * RET_QK_W)
    w0p = jnp.concatenate([qk_w, v_w, g_w, u_w], axis=1).astype(BF16)
    cos0, sin0 = _ret_rope_tables(l_len, ctx_len)
    rope_tab0 = [0] * RET_HEADS + [None] * ((w0p.shape[1] - 2 * RET_QK_W) // LANES)
    proj0 = _inproj(xt, mod, 0, w0p, cos0, sin0, rope_tab0, RET_DK // 2,
                    nb=nb, t_len=t_len, ctx_len=ctx_len)

    log_gammas = jax.nn.log_sigmoid(ret_decay_logit[0].astype(F32))
    ret = _retention(proj0, log_gammas, nb=nb, t_len=t_len, ctx_len=ctx_len)

    tc = S5_CHUNK
    nc5 = t_len // tc
    u = proj0[:, 2 * RET_QK_W + 2 * RET_V_W:]
    xg = u.reshape(nb, nc5, tc, S5_G, S5_P).transpose(3, 1, 0, 2, 4)
    xg = jnp.pad(xg, ((0, 0), (0, 0), (0, SUBLANES - nb), (0, 0), (0, 0)))
    xg = xg.reshape(S5_G, nc5 * SUBLANES, tc * S5_P)
    w1, w2, a5 = _s5_operators(s5_lam_re[0], s5_lam_im[0], s5_log_dt[0], s5_b_re[0], s5_b_im[0],
                               s5_c_re[0], s5_c_im[0], s5_d[0])
    yg = _s5(xg, w1, w2, a5, nb=SUBLANES, nctx=ctx_len // tc)
    s5y = yg.reshape(S5_G, nc5, SUBLANES, tc, S5_P)[:, :, :nb].transpose(2, 1, 3, 0, 4).reshape(nt, S5_CH)

    x1, h2, eidx, gates = _merge0(ret, proj0, s5y, xt, mod, s5_w_glu[0].astype(BF16),
                                  w_out_ab[0].astype(BF16), ln_g[0, 0].reshape(1, d), wr_t, rbias,
                                  nb=nb, t_len=t_len, ctx_len=ctx_len)
    x2 = _moe_layer(x1, h2, eidx, gates, mod, 0, ln_g[0, 1].reshape(1, d),
                    exp_w_gate[0].astype(BF16), exp_w_up[0].astype(BF16), exp_w_down[0].astype(BF16),
                    tiles_per_batch=tpb, ctx_tiles=ctx_tiles, nb=nb)

    cos1, sin1 = _attn_rope_tables(l_len, ctx_len)
    n_heads_cols = D_MODEL // LANES
    rope_tab1 = [0] * n_heads_cols + [1] * n_heads_cols + [None] * n_heads_cols
    qkv = _inproj(x2, mod, 1, w_in_c[0].astype(BF16), cos1, sin1, rope_tab1, DIFF_DH // 4,
                  nb=nb, t_len=t_len, ctx_len=ctx_len)
    lf = diff_lambda[0].astype(F32)
    lambda_init = 0.8 - 0.6 * math.exp(-0.3 * 1)
    lam = (jnp.exp(jnp.sum(lf[0] * lf[1])) - jnp.exp(jnp.sum(lf[2] * lf[3])) + lambda_init).reshape(1)
    att = _diff_attention(qkv, lam, diff_subln_g[0], lambda_init,
                          nb=nb, l_len=l_len, t_len=t_len, ctx_len=ctx_len)
    x3, h3, eidx1, gates1 = _merge1(att, x2, mod, 1, w_out_c[0].astype(BF16), ln_g[1, 0].reshape(1, d),
                                    wr_t, rbias, nb=nb, l_len=l_len, t_len=t_len, ctx_len=ctx_len)
    out = _moe_layer(x3, h3, eidx1, gates1, mod, 1, ln_g[1, 1].reshape(1, d),
                     exp_w_gate[1].astype(BF16), exp_w_up[1].astype(BF16), exp_w_down[1].astype(BF16),
                     tiles_per_batch=l_len // ROW_TILE, ctx_tiles=0, nb=nb)
    return out.reshape(nb, l_len, d)
```

```python
import functools
import math

import jax
import jax.numpy as jnp
import numpy as np
from jax import lax
from jax.experimental import pallas as pl
from jax.experimental.pallas import tpu as pltpu

F32 = jnp.float32
BF16 = jnp.bfloat16

D_MODEL = 1024
DEPTH = 2
GRID_W = 64
ALPHA = (2.0 * DEPTH) ** 0.25
LN_EPS = 1e-5
GN_EPS = 1e-6
ROPE_BASE = 10000.0
RET_DK = 64
RET_DV = 128
RET_HEADS = 6
RET_QK_W = RET_HEADS * RET_DK
RET_V_W = RET_HEADS * RET_DV
S5_CH = 256
S5_P = 16
S5_G = 16
S5_N = 64
DIFF_HEADS = 8
DIFF_DH = 64
N_EXPERTS = 16
EXPERTS_PER_GROUP = 4
TOP_K = 2

LANES = 128
SUBLANES = 8
MXU_DIM = 256
ROW_TILE = 256
RET_CHUNK = 256
RET_STATE_UNROLL = 3
RET_OUT_UNROLL = 11
S5_CHUNK = 8
S5_HALVES = 2
MOE_ROWS = 256
DISPATCH_TILE = 512
ROW_WORDS = D_MODEL // 2
ROW_SUB = ROW_WORDS // LANES
ATT_TK = 768
VMEM_LIMIT = 48 * 1024 * 1024


def _cparams(sem, flags=None):
    return pltpu.CompilerParams(dimension_semantics=sem, vmem_limit_bytes=VMEM_LIMIT, flags=flags)


def _dot(a, b):
    return jnp.dot(a, b, preferred_element_type=F32)


def _dot_nt(a, b):
    return lax.dot_general(a, b, (((1,), (1,)), ((), ())), preferred_element_type=F32)


def _dot_tn(a, b):
    return lax.dot_general(a, b, (((0,), (0,)), ((), ())), preferred_element_type=F32)


def _split_bf16(x):
    hi = x.astype(BF16)
    lo = (x - hi.astype(F32)).astype(BF16)
    return hi, lo


def _dot3(a, b):
    ah, al = _split_bf16(a)
    bh, bl = _split_bf16(b)
    return _dot(ah, bh) + _dot(ah, bl) + _dot(al, bh)


def _dot3_nt(a, b):
    ah, al = _split_bf16(a)
    bh, bl = _split_bf16(b)
    return _dot_nt(ah, bh) + _dot_nt(ah, bl) + _dot_nt(al, bh)


def _sigmoid(x):
    return 1.0 / (1.0 + jnp.exp(-x))


def _silu(x):
    return x * _sigmoid(x)


def _pack_bf16_pairs(v):
    half = v.shape[1] // 2
    bits = lax.bitcast_convert_type(v.astype(BF16).astype(F32), jnp.uint32)
    return (bits[:, :half] >> 16) | (bits[:, half:] & jnp.uint32(0xFFFF0000))


def _unpack_bf16_pairs(p):
    lo = lax.bitcast_convert_type(p << 16, F32)
    hi = lax.bitcast_convert_type(p & jnp.uint32(0xFFFF0000), F32)
    return jnp.concatenate([lo, hi], axis=1)


def _store_rows(ref, packed):
    for j in range(ROW_SUB):
        ref[:, j, :] = packed[:, j * LANES:(j + 1) * LANES]


def _load_rows(ref):
    return jnp.concatenate([ref[:, j, :] for j in range(ROW_SUB)], axis=1)


def _gelu_tanh(x):
    c = math.sqrt(2.0 / math.pi)
    return 0.5 * x * (1.0 + jnp.tanh(c * (x + 0.044715 * (x * x * x))))


def _adaln_kernel(c_ref, w_ref, b_ref, o_ref):
    c = c_ref[...]
    o_ref[0] = _dot3(_silu(c), w_ref[0]) + b_ref[0]


def _adaln(c_pad, ada_w, ada_b):
    depth, d, n = ada_w.shape
    tn = 1536
    return pl.pallas_call(
        _adaln_kernel,
        grid=(depth, n // tn),
        in_specs=[
            pl.BlockSpec((SUBLANES, d), lambda i, j: (0, 0)),
            pl.BlockSpec((1, d, tn), lambda i, j: (i, 0, j)),
            pl.BlockSpec((1, 1, tn), lambda i, j: (i, 0, j)),
        ],
        out_specs=pl.BlockSpec((1, SUBLANES, tn), lambda i, j: (i, 0, j)),
        out_shape=jax.ShapeDtypeStruct((depth, SUBLANES, n), F32),
        compiler_params=_cparams(("parallel", "parallel")),
        name="adaln",
    )(c_pad, ada_w, ada_b.reshape(depth, 1, n))


def _mod_row(t, tiles_per_batch, ctx_tiles, nb):
    b = lax.div(t, tiles_per_batch)
    w = lax.rem(t, tiles_per_batch)
    return jnp.where(w < ctx_tiles, nb, b)


def _rope_block(a, cos, sin, half):
    lane = lax.broadcasted_iota(jnp.int32, a.shape, 1)
    first = lax.rem(lane, 2 * half) < half
    rot = jnp.where(first, pltpu.roll(a, LANES - half, 1), pltpu.roll(a, half, 1))
    return a * cos + rot * sin


def _chunk_perm(rows, transpose):
    per = rows // S5_CHUNK
    r = lax.broadcasted_iota(jnp.int32, (rows, rows), 0)
    c = lax.broadcasted_iota(jnp.int32, (rows, rows), 1)
    if transpose:
        r, c = c, r
    return (c == S5_CHUNK * lax.rem(r, per) + lax.div(r, per)).astype(BF16)


def _inproj_kernel(x_ref, mod_ref, w_ref, cos_ref, sin_ref, o_ref, *rest, tiles_per_batch,
                   ctx_tiles, nb, rope_tab, rope_half):
    d = x_ref.shape[1]
    n = o_ref.shape[1]
    r = _mod_row(pl.program_id(0), tiles_per_batch, ctx_tiles, nb)
    sh = mod_ref[0, pl.ds(r, 1), 0:d]
    sc = mod_ref[0, pl.ds(r, 1), d:2 * d]
    xm = (x_ref[...] * (1.0 + sc) + sh).astype(BF16)
    for j in range(n // MXU_DIM):
        acc = _dot(xm, w_ref[:, j * MXU_DIM:(j + 1) * MXU_DIM])
        parts = []
        for s in range(MXU_DIM // LANES):
            blk = acc[:, s * LANES:(s + 1) * LANES]
            tab = rope_tab[j * (MXU_DIM // LANES) + s]
            if tab is not None:
                blk = _rope_block(blk, cos_ref[tab], sin_ref[tab], rope_half)
            parts.append(blk)
        o_ref[:, j * MXU_DIM:(j + 1) * MXU_DIM] = jnp.concatenate(parts, axis=1).astype(BF16)
    if rest:
        u_ref, = rest
        rows = x_ref.shape[0]
        per = rows // S5_CHUNK
        u = _dot(xm, w_ref[:, n:n + S5_CH]).astype(BF16)
        up = _dot(_chunk_perm(rows, False), u).astype(BF16)
        for s in range(S5_CHUNK):
            for hf in range(S5_CH // LANES):
                u_ref[hf, :, s * LANES:(s + 1) * LANES] = up[s * per:(s + 1) * per,
                                                             hf * LANES:(hf + 1) * LANES]


def _inproj(x, mod, layer, w, cos, sin, rope_tab, rope_half, *, nb, t_len, ctx_len, s5_cols=0):
    nt, d = x.shape
    n = w.shape[1] - s5_cols
    tpb = t_len // ROW_TILE
    kern = functools.partial(_inproj_kernel, tiles_per_batch=tpb, ctx_tiles=ctx_len // ROW_TILE,
                             nb=nb, rope_tab=tuple(rope_tab), rope_half=rope_half)
    ntab = cos.shape[0]
    out_specs = [pl.BlockSpec((ROW_TILE, n), lambda t: (t, 0))]
    out_shape = [jax.ShapeDtypeStruct((nt, n), BF16)]
    if s5_cols:
        halves, per = s5_cols // LANES, ROW_TILE // S5_CHUNK
        out_specs.append(pl.BlockSpec((halves, per, S5_CHUNK * LANES), lambda t: (0, t, 0)))
        out_shape.append(jax.ShapeDtypeStruct((halves, nt // S5_CHUNK, S5_CHUNK * LANES), BF16))
    return pl.pallas_call(
        kern,
        grid=(nt // ROW_TILE,),
        in_specs=[
            pl.BlockSpec((ROW_TILE, d), lambda t: (t, 0)),
            pl.BlockSpec((1, SUBLANES, mod.shape[2]), lambda t: (layer, 0, 0)),
            pl.BlockSpec(w.shape, lambda t: (0, 0)),
            pl.BlockSpec((ntab, ROW_TILE, LANES), lambda t: (0, lax.rem(t, tpb), 0)),
            pl.BlockSpec((ntab, ROW_TILE, LANES), lambda t: (0, lax.rem(t, tpb), 0)),
        ],
        out_specs=out_specs,
        out_shape=out_shape,
        compiler_params=_cparams(("parallel",)),
        name="inproj%d" % layer,
    )(x, mod, w, cos, sin)


def _retention_kernel(lg_ref, qk_ref, v_ref, o_ref, sf_ref, sb_ref, *, nctx):
    c_len = RET_CHUNK
    t_len = qk_ref.shape[0]
    nc = t_len // c_len
    h = pl.program_id(1)
    lgf = lg_ref[0, h]
    lgb = lg_ref[1, h]
    ii = lax.broadcasted_iota(jnp.int32, (c_len, 1), 0).astype(F32)
    jj = lax.broadcasted_iota(jnp.int32, (1, c_len), 1).astype(F32)
    diff = ii - jj
    decay = jnp.where(diff >= 0.0, jnp.exp(lgf * jnp.maximum(diff, 0.0)),
                      jnp.exp(lgb * jnp.maximum(-diff, 0.0)))
    kdf = jnp.exp(lgf * (c_len - 1.0 - ii))
    kdb = jnp.exp(lgb * ii)
    qdf = jnp.exp(lgf * (ii + 1.0))
    qdb = jnp.exp(lgb * (c_len - ii))
    zrow = jnp.zeros((1, RET_DV), F32)
    gf_chunk = jnp.exp(zrow + lgf * c_len)
    gb_chunk = jnp.exp(zrow + lgb * c_len)

    def load(c):
        rows = pl.ds(pl.multiple_of(c * c_len, c_len), c_len)
        qk = qk_ref[rows, :].astype(F32)
        return qk[:, :RET_DK], qk[:, RET_DK:], v_ref[rows, :]

    def states(j, carry):
        sf, sb = carry
        cb = jnp.where(j < nctx, nctx - 1 - j, nc - 1 - (j - nctx))
        sf_ref[j] = sf
        sb_ref[cb] = sb
        _, kf, vf = load(j)
        _, kb, vb = load(cb)
        return (gf_chunk * sf + _dot_tn((kf * kdf).astype(BF16), vf),
                gb_chunk * sb + _dot_tn((kb * kdb).astype(BF16), vb))

    zero_state = jnp.zeros((RET_DK, RET_DV), F32)
    lax.fori_loop(0, nc, states, (zero_state, zero_state), unroll=RET_STATE_UNROLL)

    def out_chunk(c, carry):
        q, k, v = load(c)
        scores = _dot_nt(q.astype(BF16), k.astype(BF16)) * decay
        o = _dot(scores.astype(BF16), v)
        o = o + _dot((q * qdf).astype(BF16), sf_ref[c].astype(BF16))
        o = o + _dot((q * qdb).astype(BF16), sb_ref[c].astype(BF16))
        mu = jnp.mean(o, axis=-1, keepdims=True)
        oc = o - mu
        var = jnp.mean(oc * oc, axis=-1, keepdims=True)
        rows = pl.ds(pl.multiple_of(c * c_len, c_len), c_len)
        o_ref[rows, :] = (oc * lax.rsqrt(var + GN_EPS)).astype(BF16)
        return carry

    lax.fori_loop(0, nc, out_chunk, 0, unroll=RET_OUT_UNROLL)


def _retention(proj, log_gammas, *, nb, t_len, ctx_len):
    nt = proj.shape[0]
    nc = t_len // RET_CHUNK
    kern = functools.partial(_retention_kernel, nctx=ctx_len // RET_CHUNK)
    vcol0 = RET_HEADS
    return pl.pallas_call(
        kern,
        grid_spec=pltpu.PrefetchScalarGridSpec(
            num_scalar_prefetch=1,
            grid=(nb, RET_HEADS),
            in_specs=[
                pl.BlockSpec((t_len, LANES), lambda b, h, lg: (b, h)),
                pl.BlockSpec((t_len, LANES), lambda b, h, lg: (b, vcol0 + h)),
            ],
            out_specs=pl.BlockSpec((t_len, LANES), lambda b, h, lg: (b, h)),
            scratch_shapes=[pltpu.VMEM((nc, RET_DK, RET_DV), F32),
                            pltpu.VMEM((nc, RET_DK, RET_DV), F32)],
        ),
        out_shape=jax.ShapeDtypeStruct((nt, RET_V_W), BF16),
        compiler_params=_cparams(("parallel", "parallel")),
        name="retention",
    )(log_gammas, proj, proj)


def _s5_operators(lam_re, lam_im, log_dt, b_re, b_im, c_re, c_im, d_skip):
    tc = S5_CHUNK
    hp = lax.Precision.HIGHEST
    ks = jnp.arange(tc + 1, dtype=F32)
    pw, bbar, cm = [], [], []
    for direction in range(2):
        dt = jnp.exp(log_dt[direction].astype(F32))[:, None]
        lam = lax.complex(lam_re[direction].astype(F32), lam_im[direction].astype(F32))
        z = lam * dt
        p = jnp.exp(z[None] * ks[:, None, None])
        lam_bar = p[1]
        bb = ((lam_bar - 1.0) / lam)[..., None] * lax.complex(
            b_re[direction].astype(F32), b_im[direction].astype(F32))
        pw.append(p)
        bbar.append(bb)
        cm.append(lax.complex(c_re[direction].astype(F32), c_im[direction].astype(F32)))

    def lag_kernel(p, bb, c):
        return jnp.einsum('gpn,kgn,gnq->kgpq', c, p[:tc], bb, precision=hp).real

    kf = lag_kernel(pw[0], bbar[0], cm[0])
    kb = lag_kernel(pw[1], bbar[1], cm[1])
    k0 = kf[0] + kb[0] + jnp.eye(S5_P, dtype=F32)[None] * d_skip.astype(F32)[:, :, None]
    kcat = jnp.concatenate([kb[1:][::-1], k0[None], kf[1:]], axis=0)
    s_idx = jnp.arange(tc)[:, None]
    t_idx = jnp.arange(tc)[None, :]
    m5 = kcat[t_idx - s_idx + tc - 1]
    hg = S5_G // S5_HALVES
    eye = jnp.eye(hg, dtype=F32)
    split = lambda z, axis: z.reshape(z.shape[:axis] + (S5_HALVES, hg) + z.shape[axis + 1:])
    wide = tc * hg * S5_P
    intra = jnp.einsum('sthgpq,gk->hsgqtkp', split(m5, 2), eye).reshape(S5_HALVES, wide, wide)

    ef = pw[0][:tc][::-1][:, :, :, None] * bbar[0][None]
    eb = pw[1][:tc][:, :, :, None] * bbar[1][None]
    to_in = lambda e: jnp.einsum('shgnq,gk->hsgqkn', split(e, 1), eye).reshape(
        S5_HALVES, wide, hg * S5_N)
    w_in = jnp.concatenate([to_in(ef.real), to_in(ef.imag), to_in(eb.real), to_in(eb.imag)], axis=2)
    of = cm[0][None] * pw[0][1:][:, :, None, :]
    ob = cm[1][None] * pw[1][1:][::-1][:, :, None, :]
    to_out = lambda o: jnp.einsum('thgpn,gk->hkntgp', split(o, 1), eye).reshape(
        S5_HALVES, hg * S5_N, wide)
    w_out = jnp.concatenate([to_out(of.real), -to_out(of.imag), to_out(ob.real), -to_out(ob.imag)],
                            axis=1)
    a = jnp.stack([pw[0][tc].real, pw[0][tc].imag, pw[1][tc].real, pw[1][tc].imag], axis=0)
    a = a.reshape(4, S5_HALVES, hg * S5_N).transpose(1, 0, 2)
    return intra.astype(BF16), w_in.astype(BF16), w_out.astype(BF16), a


def _s5_kernel(x_ref, wi_ref, win_ref, wout_ref, a_ref, y_ref, st_ref, *, nctx):
    x = x_ref[0]
    nc = x.shape[0]
    w = a_ref.shape[2]
    st_ref[...] = _dot(x, win_ref[0])
    afr, afi, abr, abi = (a_ref[0, i:i + 1, :] for i in range(4))

    def step(j, carry):
        fr, fi, br, bi = carry
        rf = pl.ds(j, 1)
        rb = pl.ds(jnp.where(j < nctx, nctx - 1 - j, nc - 1 - (j - nctx)), 1)
        efr, efi = st_ref[rf, 0:w], st_ref[rf, w:2 * w]
        ebr, ebi = st_ref[rb, 2 * w:3 * w], st_ref[rb, 3 * w:4 * w]
        st_ref[rf, 0:w] = fr
        st_ref[rf, w:2 * w] = fi
        st_ref[rb, 2 * w:3 * w] = br
        st_ref[rb, 3 * w:4 * w] = bi
        return (afr * fr - afi * fi + efr, afr * fi + afi * fr + efi,
                abr * br - abi * bi + ebr, abr * bi + abi * br + ebi)

    z = jnp.zeros((1, w), F32)
    lax.fori_loop(0, nc, step, (z, z, z, z), unroll=4)
    y_ref[0] = (_dot(x, wi_ref[0]) + _dot(st_ref[...].astype(BF16), wout_ref[0])).astype(BF16)


def _s5(xc, intra, w_in, w_out, a, *, nb, nctx):
    halves, rows, wide = xc.shape
    nc = rows // nb
    kern = functools.partial(_s5_kernel, nctx=nctx)
    per_half = lambda arr: pl.BlockSpec((1,) + arr.shape[1:], lambda hf, b: (hf, 0, 0))
    return pl.pallas_call(
        kern,
        grid=(halves, nb),
        in_specs=[
            pl.BlockSpec((1, nc, wide), lambda hf, b: (hf, b, 0)),
            per_half(intra), per_half(w_in), per_half(w_out), per_half(a),
        ],
        out_specs=pl.BlockSpec((1, nc, wide), lambda hf, b: (hf, b, 0)),
        out_shape=jax.ShapeDtypeStruct(xc.shape, BF16),
        scratch_shapes=[pltpu.VMEM((nc, w_in.shape[2]), F32)],
        compiler_params=_cparams(("parallel", "parallel")),
        name="s5",
    )(xc, intra, w_in, w_out, a)


def _route(logits_t, bias):
    scores = _sigmoid(logits_t)
    biased = scores + bias
    s_rows = [scores[e:e + 1, :] for e in range(N_EXPERTS)]
    b_rows = [biased[e:e + 1, :] for e in range(N_EXPERTS)]
    n_groups = N_EXPERTS // EXPERTS_PER_GROUP
    best = None
    sel = None
    for g in range(n_groups):
        a, b, c, d = b_rows[4 * g:4 * g + 4]
        hi1, lo1 = jnp.maximum(a, b), jnp.minimum(a, b)
        hi2, lo2 = jnp.maximum(c, d), jnp.minimum(c, d)
        top1 = jnp.maximum(hi1, hi2)
        top2 = jnp.maximum(jnp.minimum(hi1, hi2), jnp.maximum(lo1, lo2))
        gs = top1 + top2
        if g == 0:
            best, sel = gs, jnp.zeros(gs.shape, jnp.int32)
        else:
            better = gs > best
            sel = jnp.where(better, g, sel)
            best = jnp.where(better, gs, best)
    neg = jnp.full(best.shape, -jnp.inf, F32)
    masked = [jnp.where(sel == (e // EXPERTS_PER_GROUP), b_rows[e], neg) for e in range(N_EXPERTS)]
    v1, i1, g1 = masked[0], jnp.zeros(best.shape, jnp.int32), s_rows[0]
    for e in range(1, N_EXPERTS):
        better = masked[e] > v1
        v1 = jnp.where(better, masked[e], v1)
        i1 = jnp.where(better, e, i1)
        g1 = jnp.where(better, s_rows[e], g1)
    v2, i2, g2 = neg, jnp.zeros(best.shape, jnp.int32), jnp.zeros(best.shape, F32)
    for e in range(N_EXPERTS):
        cand = jnp.where(i1 == e, neg, masked[e])
        better = cand > v2
        v2 = jnp.where(better, cand, v2)
        i2 = jnp.where(better, e, i2)
        g2 = jnp.where(better, s_rows[e], g2)
    tot = g1 + g2
    return jnp.concatenate([i1, i2], axis=0), jnp.concatenate([g1 / tot, g2 / tot], axis=0)


def _tail(x, o, mod_ref, r, lng, wr, rb, x1_ref, h2_ref, ei_ref, gt_ref):
    d = x.shape[1]
    g1 = mod_ref[0, pl.ds(r, 1), 2 * d:3 * d]
    sh2 = mod_ref[0, pl.ds(r, 1), 3 * d:4 * d]
    sc2 = mod_ref[0, pl.ds(r, 1), 4 * d:5 * d]
    y = ALPHA * x + g1 * o
    mu = jnp.mean(y, axis=-1, keepdims=True)
    yc = y - mu
    var = jnp.mean(yc * yc, axis=-1, keepdims=True)
    x1 = yc * lax.rsqrt(var + LN_EPS) * lng
    h2 = x1 * (1.0 + sc2) + sh2
    x1_ref[...] = x1
    _store_rows(h2_ref, _pack_bf16_pairs(h2))
    ei, gt = _route(_dot3_nt(wr, h2), rb)
    ei_ref[...] = ei
    gt_ref[...] = gt


def _merge0_kernel(r_ref, g_ref, s_ref, x_ref, mod_ref, wglu_ref, wout_ref, lng_ref, wr_ref,
                   rb_ref, x1_ref, h2_ref, ei_ref, gt_ref, *, tiles_per_batch, ctx_tiles, nb):
    r = _mod_row(pl.program_id(0), tiles_per_batch, ctx_tiles, nb)
    ret = r_ref[...].astype(F32) * _silu(g_ref[...].astype(F32))
    rows = x_ref.shape[0]
    sp = jnp.concatenate(
        [jnp.concatenate([s_ref[hf, :, s * LANES:(s + 1) * LANES] for hf in range(S5_HALVES)], axis=1)
         for s in range(S5_CHUNK)], axis=0)
    s5 = _dot(_chunk_perm(rows, True), sp)
    z = _dot(_gelu_tanh(s5).astype(BF16), wglu_ref[...])
    zz = z[:, :S5_CH] * _sigmoid(z[:, S5_CH:])
    o = _dot(ret.astype(BF16), wout_ref[0:RET_V_W, :]) + _dot(zz.astype(BF16), wout_ref[RET_V_W:, :])
    _tail(x_ref[...], o, mod_ref, r, lng_ref[...], wr_ref[...], rb_ref[...],
          x1_ref, h2_ref, ei_ref, gt_ref)


def _merge1_kernel(a_ref, x_ref, mod_ref, wout_ref, lng_ref, wr_ref, rb_ref,
                   x1_ref, h2_ref, ei_ref, gt_ref, *, tiles_per_batch):
    r = lax.div(pl.program_id(0), tiles_per_batch)
    o = _dot(a_ref[...], wout_ref[...])
    _tail(x_ref[...], o, mod_ref, r, lng_ref[...], wr_ref[...], rb_ref[...],
          x1_ref, h2_ref, ei_ref, gt_ref)


def _tail_outs(n_rows, d):
    shapes = (jax.ShapeDtypeStruct((n_rows, d), F32),
              jax.ShapeDtypeStruct((n_rows, ROW_SUB, LANES), jnp.uint32),
              jax.ShapeDtypeStruct((TOP_K, n_rows), jnp.int32),
              jax.ShapeDtypeStruct((TOP_K, n_rows), F32))
    specs = (pl.BlockSpec((ROW_TILE, d), lambda t: (t, 0)),
             pl.BlockSpec((ROW_TILE, ROW_SUB, LANES), lambda t: (t, 0, 0)),
             pl.BlockSpec((TOP_K, ROW_TILE), lambda t: (0, t)),
             pl.BlockSpec((TOP_K, ROW_TILE), lambda t: (0, t)))
    return shapes, specs


def _merge0(ret, proj, s5y, x, mod, w_glu, w_out, lng, wr_t, rbias, *, nb, t_len, ctx_len):
    nt, d = x.shape
    tpb = t_len // ROW_TILE
    kern = functools.partial(_merge0_kernel, tiles_per_batch=tpb, ctx_tiles=ctx_len // ROW_TILE, nb=nb)
    shapes, specs = _tail_outs(nt, d)
    gcol = (2 * RET_QK_W + RET_V_W) // RET_V_W
    full = lambda a: pl.BlockSpec(a.shape, lambda t: (0,) * a.ndim)
    return pl.pallas_call(
        kern,
        grid=(nt // ROW_TILE,),
        in_specs=[
            pl.BlockSpec((ROW_TILE, RET_V_W), lambda t: (t, 0)),
            pl.BlockSpec((ROW_TILE, RET_V_W), lambda t: (t, gcol)),
            pl.BlockSpec((S5_HALVES, ROW_TILE // S5_CHUNK, S5_CHUNK * LANES), lambda t: (0, t, 0)),
            pl.BlockSpec((ROW_TILE, d), lambda t: (t, 0)),
            pl.BlockSpec((1, SUBLANES, mod.shape[2]), lambda t: (0, 0, 0)),
            full(w_glu), full(w_out), full(lng), full(wr_t), full(rbias),
        ],
        out_specs=specs,
        out_shape=shapes,
        compiler_params=_cparams(("parallel",)),
        name="merge0",
    )(ret, proj, s5y, x, mod, w_glu, w_out, lng, wr_t, rbias)


def _merge1(att, x, mod, layer, w_out, lng, wr_t, rbias, *, nb, l_len, t_len, ctx_len):
    n_lat, d = att.shape
    tpb = l_len // ROW_TILE
    tpb_t = t_len // ROW_TILE
    ctx_tiles = ctx_len // ROW_TILE
    kern = functools.partial(_merge1_kernel, tiles_per_batch=tpb)
    shapes, specs = _tail_outs(n_lat, d)
    full = lambda a: pl.BlockSpec(a.shape, lambda t: (0,) * a.ndim)
    xrow = lambda t: (lax.div(t, tpb) * tpb_t + ctx_tiles + lax.rem(t, tpb), 0)
    return pl.pallas_call(
        kern,
        grid=(n_lat // ROW_TILE,),
        in_specs=[
            pl.BlockSpec((ROW_TILE, d), lambda t: (t, 0)),
            pl.BlockSpec((ROW_TILE, d), xrow),
            pl.BlockSpec((1, SUBLANES, mod.shape[2]), lambda t: (layer, 0, 0)),
            full(w_out), full(lng), full(wr_t), full(rbias),
        ],
        out_specs=specs,
        out_shape=shapes,
        compiler_params=_cparams(("parallel",)),
        name="merge1",
    )(att, x, mod, w_out, lng, wr_t, rbias)


def _moe_plan(eidx):
    k, n = eidx.shape
    a = k * n
    e_flat = eidx.reshape(a)
    onehot = (e_flat[:, None] == jnp.arange(N_EXPERTS, dtype=jnp.int32)[None, :]).astype(jnp.int32)
    csum = jnp.cumsum(onehot, axis=0)
    counts = csum[-1]
    rank = jnp.take_along_axis(csum, e_flat[:, None], axis=1)[:, 0] - 1
    padded = (counts + MOE_ROWS - 1) // MOE_ROWS * MOE_ROWS
    pad_end = jnp.cumsum(padded)
    pad_start = pad_end - padded
    dest = (pad_start[e_flat] + rank).astype(jnp.int32)
    n_blocks = -(-(a + N_EXPERTS * (MOE_ROWS - 1)) // MOE_ROWS)
    block_expert = jnp.minimum(
        jnp.searchsorted(pad_end, jnp.arange(n_blocks, dtype=jnp.int32) * MOE_ROWS, side='right'),
        N_EXPERTS - 1).astype(jnp.int32)
    return dest.reshape(k, n), block_expert, n_blocks


def _tile_rows_of(dest, tile):
    k, n = dest.shape
    return dest.reshape(k, n // tile, tile).transpose(1, 0, 2).reshape(n // tile, 1, k * tile)


def _dispatch_kernel(dest_ref, h_ref, xs_in_hbm, xs_hbm, sem):
    del xs_in_hbm
    rows = h_ref.shape[0]

    def start(r, c):
        for choice in range(TOP_K):
            pltpu.make_async_copy(h_ref.at[r], xs_hbm.at[dest_ref[0, 0, choice * rows + r]], sem).start()
        return c

    lax.fori_loop(0, rows, start, 0, unroll=8)
    for _ in range(TOP_K):
        pltpu.make_async_copy(h_ref, xs_hbm.at[pl.ds(0, rows)], sem).wait()


def _dispatch(h, dest, n_rows):
    n = h.shape[0]
    tile = DISPATCH_TILE if n % DISPATCH_TILE == 0 else ROW_TILE
    return pl.pallas_call(
        _dispatch_kernel,
        grid=(n // tile,),
        in_specs=[
            pl.BlockSpec((1, 1, TOP_K * tile), lambda t: (t, 0, 0), memory_space=pltpu.SMEM),
            pl.BlockSpec((tile, ROW_SUB, LANES), lambda t: (t, 0, 0)),
            pl.BlockSpec(memory_space=pl.ANY),
        ],
        out_specs=pl.BlockSpec(memory_space=pl.ANY),
        out_shape=jax.ShapeDtypeStruct((n_rows, ROW_SUB, LANES), jnp.uint32),
        scratch_shapes=[pltpu.SemaphoreType.DMA],
        input_output_aliases={2: 0},
        compiler_params=_cparams(("arbitrary",)),
        name="moe_dispatch",
    )(_tile_rows_of(dest, tile), h, jnp.zeros((n_rows, ROW_SUB, LANES), jnp.uint32))


def _experts_kernel(be_ref, x_ref, wg_ref, wu_ref, wd_ref, o_ref):
    x = _unpack_bf16_pairs(_load_rows(x_ref)).astype(BF16)
    hg = _dot(x, wg_ref[0])
    hu = _dot(x, wu_ref[0])
    _store_rows(o_ref, _pack_bf16_pairs(_dot((_silu(hg) * hu).astype(BF16), wd_ref[0])))


def _experts(xs, block_expert, wg, wu, wd):
    n_blocks = block_expert.shape[0]
    d, dff = wg.shape[1], wg.shape[2]
    rows_spec = pl.BlockSpec((MOE_ROWS, ROW_SUB, LANES), lambda i, be: (i, 0, 0))
    return pl.pallas_call(
        _experts_kernel,
        grid_spec=pltpu.PrefetchScalarGridSpec(
            num_scalar_prefetch=1,
            grid=(n_blocks,),
            in_specs=[
                rows_spec,
                pl.BlockSpec((1, d, dff), lambda i, be: (be[i], 0, 0)),
                pl.BlockSpec((1, d, dff), lambda i, be: (be[i], 0, 0)),
                pl.BlockSpec((1, dff, d), lambda i, be: (be[i], 0, 0)),
            ],
            out_specs=rows_spec,
        ),
        out_shape=jax.ShapeDtypeStruct(xs.shape, jnp.uint32),
        compiler_params=_cparams(("parallel",)),
        name="moe_experts",
    )(block_expert, xs, wg, wu, wd)


def _combine_kernel(dcur_ref, dnxt_ref, x_ref, gt_ref, mod_ref, lng_ref, y_hbm, o_ref, ybuf, sem, *,
                    tiles_per_batch, ctx_tiles, nb):
    t = pl.program_id(0)
    last = pl.num_programs(0) - 1
    rows = x_ref.shape[0]
    n = TOP_K * rows
    d = x_ref.shape[1]
    slot = lax.rem(t, 2)

    def start_all(idx_ref, s):
        def start(j, c):
            pltpu.make_async_copy(y_hbm.at[idx_ref[0, 0, j]], ybuf.at[s, j], sem.at[s]).start()
            return c

        lax.fori_loop(0, n, start, 0, unroll=8)

    @pl.when(t == 0)
    def _():
        start_all(dcur_ref, 0)

    @pl.when(t < last)
    def _():
        start_all(dnxt_ref, 1 - slot)

    pltpu.make_async_copy(y_hbm.at[pl.ds(0, n)], ybuf.at[slot], sem.at[slot]).wait()
    r = _mod_row(t, tiles_per_batch, ctx_tiles, nb)
    gt = gt_ref[...]
    yb = ybuf.at[slot]
    y = (_unpack_bf16_pairs(_load_rows(yb.at[pl.ds(0, rows)])) * gt[:, 0:1]
         + _unpack_bf16_pairs(_load_rows(yb.at[pl.ds(rows, rows)])) * gt[:, 1:2])
    g2 = mod_ref[0, pl.ds(r, 1), 5 * d:6 * d]
    z = ALPHA * x_ref[...] + g2 * y
    mu = jnp.mean(z, axis=-1, keepdims=True)
    zc = z - mu
    var = jnp.mean(zc * zc, axis=-1, keepdims=True)
    o_ref[...] = zc * lax.rsqrt(var + LN_EPS) * lng_ref[...]


def _combine(x1, dest, gates, mod, layer, lng, ys, *, tiles_per_batch, ctx_tiles, nb):
    n, d = x1.shape
    nt = n // ROW_TILE
    dest_t = _tile_rows_of(dest, ROW_TILE)
    kern = functools.partial(_combine_kernel, tiles_per_batch=tiles_per_batch, ctx_tiles=ctx_tiles, nb=nb)
    idx_spec = lambda f: pl.BlockSpec((1, 1, TOP_K * ROW_TILE), f, memory_space=pltpu.SMEM)
    return pl.pallas_call(
        kern,
        grid=(nt,),
        in_specs=[
            idx_spec(lambda t: (t, 0, 0)),
            idx_spec(lambda t: (jnp.minimum(t + 1, nt - 1), 0, 0)),
            pl.BlockSpec((ROW_TILE, d), lambda t: (t, 0)),
            pl.BlockSpec((ROW_TILE, TOP_K), lambda t: (t, 0)),
            pl.BlockSpec((1, SUBLANES, mod.shape[2]), lambda t: (layer, 0, 0)),
            pl.BlockSpec((1, d), lambda t: (0, 0)),
            pl.BlockSpec(memory_space=pl.ANY),
        ],
        out_specs=pl.BlockSpec((ROW_TILE, d), lambda t: (t, 0)),
        out_shape=jax.ShapeDtypeStruct((n, d), F32),
        scratch_shapes=[pltpu.VMEM((2, TOP_K * ROW_TILE, ROW_SUB, LANES), jnp.uint32),
                        pltpu.SemaphoreType.DMA((2,))],
        compiler_params=_cparams(("arbitrary",)),
        name="moe_combine%d" % layer,
    )(dest_t, dest_t, x1, gates.T, mod, lng, ys)


def _moe_layer(x1, h2, eidx, gates, mod, layer, lng, wg, wu, wd, *, tiles_per_batch, ctx_tiles, nb):
    dest, block_expert, n_blocks = _moe_plan(eidx)
    xs = _dispatch(h2, dest, n_blocks * MOE_ROWS)
    ys = _experts(xs, block_expert, wg, wu, wd)
    return _combine(x1, dest, gates, mod, layer, lng, ys,
                    tiles_per_batch=tiles_per_batch, ctx_tiles=ctx_tiles, nb=nb)


def _attn_kernel(lam_ref, q_ref, k_ref, v_ref, g_ref, o_ref, vext, s_buf0, s_buf1,
                 p_buf0, p_buf1, corr_buf0, corr_buf1, m_buf, acc, *, out_scale, ctx_len):
    t_len = k_ref.shape[0]
    nk = t_len // ATT_TK
    dv = v_ref.shape[1]
    tq = acc.shape[1]
    n_tiles = (o_ref.shape[0] // tq) * nk

    vext[:, 0:dv] = v_ref[...]
    vext[:, dv:2 * dv] = jnp.ones((t_len, dv), BF16)
    lam = lam_ref[0]
    gain = g_ref[...] * out_scale

    s_bufs, p_bufs, corr_bufs = (s_buf0, s_buf1), (p_buf0, p_buf1), (corr_buf0, corr_buf1)

    def key_rows(kj):
        return pl.ds(pl.multiple_of(kj * ATT_TK, ATT_TK), ATT_TK)

    def advance(tile):
        qi, kj = tile
        wrap = kj + 1 == nk
        return jnp.where(wrap, qi + 1, qi), jnp.where(wrap, 0, kj + 1)

    def scores(tile, slot):
        qi, kj = tile
        q = q_ref[pl.ds(pl.multiple_of(ctx_len + qi * tq, ROW_TILE), tq), :]
        lane = lax.broadcasted_iota(jnp.int32, q.shape, 1)
        zero = jnp.zeros(q.shape, q.dtype)
        k = k_ref[key_rows(kj), :]
        s_bufs[slot][0] = _dot_nt(jnp.where(lane < DIFF_DH, q, zero), k)
        s_bufs[slot][1] = _dot_nt(jnp.where(lane >= DIFF_DH, q, zero), k)

    def numerators(tile, slot):
        _, kj = tile
        for w in range(2):
            s = s_bufs[slot][w]
            m_old = jnp.where(kj == 0, -jnp.inf, m_buf[w])
            m_new = jnp.maximum(m_old, jnp.max(s, axis=-1, keepdims=True))
            p_bufs[slot][w] = jnp.exp2(s - m_new).astype(BF16)
            corr_bufs[slot][w] = jnp.exp2(m_old - m_new)
            m_buf[w] = m_new

    def values(tile, slot):
        qi, kj = tile
        ve = vext[key_rows(kj), :]
        a = []
        for w in range(2):
            a.append(corr_bufs[slot][w] * acc[w] + _dot(p_bufs[slot][w], ve))
            acc[w] = a[w]
        o = a[0][:, 0:dv] / a[0][:, dv:2 * dv] - lam * (a[1][:, 0:dv] / a[1][:, dv:2 * dv])
        o = o * lax.rsqrt(jnp.mean(o * o, axis=-1, keepdims=True) + GN_EPS)
        o_ref[pl.ds(pl.multiple_of(qi * tq, tq), tq), :] = (o * gain).astype(BF16)

    def step(tiles, slot):
        a, b, c = tiles
        values(c, slot)
        scores(a, slot)
        numerators(b, 1 - slot)
        return advance(a), a, b

    acc[...] = jnp.zeros(acc.shape, F32)
    t0 = (jnp.int32(0), jnp.int32(0))
    t1 = advance(t0)
    scores(t0, 0)
    scores(t1, 1)
    numerators(t0, 0)

    def pair(_, tiles):
        return step(step(tiles, 0), 1)

    _, last, prev = lax.fori_loop(0, (n_tiles - 2) // 2, pair, (advance(t1), t1, t0))
    numerators(last, (n_tiles - 1) % 2)
    values(prev, n_tiles % 2)
    values(last, (n_tiles - 1) % 2)


def _diff_attention(qkv, lam, subln_g, lambda_init, *, nb, l_len, t_len, ctx_len):
    d = D_MODEL
    tq = 2 * ROW_TILE
    dv = 2 * DIFF_DH
    assert ((l_len // tq) * (t_len // ATT_TK)) % 2 == 0
    kern = functools.partial(_attn_kernel, out_scale=1.0 - lambda_init, ctx_len=ctx_len)
    return pl.pallas_call(
        kern,
        grid_spec=pltpu.PrefetchScalarGridSpec(
            num_scalar_prefetch=1,
            grid=(nb, DIFF_HEADS),
            in_specs=[
                pl.BlockSpec((t_len, LANES), lambda b, h, lam: (b, h)),
                pl.BlockSpec((t_len, LANES), lambda b, h, lam: (b, DIFF_HEADS + h)),
                pl.BlockSpec((t_len, LANES), lambda b, h, lam: (b, 2 * DIFF_HEADS + h)),
                pl.BlockSpec((1, LANES), lambda b, h, lam: (0, 0)),
            ],
            out_specs=pl.BlockSpec((l_len, LANES), lambda b, h, lam: (b, h)),
            scratch_shapes=[
                pltpu.VMEM((t_len, 2 * dv), BF16),
                pltpu.VMEM((2, tq, ATT_TK), F32), pltpu.VMEM((2, tq, ATT_TK), F32),
                pltpu.VMEM((2, tq, ATT_TK), BF16), pltpu.VMEM((2, tq, ATT_TK), BF16),
                pltpu.VMEM((2, tq, 1), F32), pltpu.VMEM((2, tq, 1), F32),
                pltpu.VMEM((2, tq, 1), F32),
                pltpu.VMEM((2, tq, 2 * dv), F32),
            ],
        ),
        out_shape=jax.ShapeDtypeStruct((nb * l_len, d), BF16),
        compiler_params=_cparams(("parallel", "parallel")),
        name="diff_attention",
    )(lam, qkv, qkv, qkv, subln_g.reshape(1, LANES).astype(F32))


def _ret_rope_tables(l_len, ctx_len):
    half = RET_DK // 2
    inv = ROPE_BASE ** (-jnp.arange(0, RET_DK, 2, dtype=F32) / RET_DK)
    ang = jnp.arange(l_len, dtype=F32)[:, None] * inv[None, :]
    ang = jnp.concatenate([jnp.zeros((ctx_len, half), F32), ang], axis=0)
    cos64 = jnp.concatenate([jnp.cos(ang), jnp.cos(ang)], axis=1)
    sin64 = jnp.concatenate([-jnp.sin(ang), jnp.sin(ang)], axis=1)
    kscale = RET_DK ** -0.5
    cos = jnp.concatenate([cos64, cos64 * kscale], axis=1)
    sin = jnp.concatenate([sin64, sin64 * kscale], axis=1)
    return cos[None], sin[None]


def _attn_rope_tables(l_len, ctx_len):
    quarter = DIFF_DH // 4
    inv = ROPE_BASE ** (-jnp.arange(0, DIFF_DH // 2, 2, dtype=F32) / (DIFF_DH // 2))
    pos = jnp.arange(l_len)
    ang_r = (pos // GRID_W).astype(F32)[:, None] * inv[None, :]
    ang_c = (pos % GRID_W).astype(F32)[:, None] * inv[None, :]
    pad = lambda a: jnp.concatenate([jnp.zeros((ctx_len, quarter), F32), a], axis=0)
    ang_r, ang_c = pad(ang_r), pad(ang_c)
    cos64 = jnp.concatenate([jnp.cos(ang_r)] * 2 + [jnp.cos(ang_c)] * 2, axis=1)
    sin64 = jnp.concatenate([-jnp.sin(ang_r), jnp.sin(ang_r), -jnp.sin(ang_c), jnp.sin(ang_c)], axis=1)
    cos = jnp.concatenate([cos64, cos64], axis=1)
    sin = jnp.concatenate([sin64, sin64], axis=1)
    qscale = DIFF_DH ** -0.5 * math.log2(math.e)
    return jnp.stack([cos * qscale, cos]), jnp.stack([sin * qscale, sin])


def kernel(x, c, ctx, c_ctx, ada_w, ada_b, ln_g, w_in_ab, ret_decay_logit, s5_lam_re, s5_lam_im,
           s5_log_dt, s5_b_re, s5_b_im, s5_c_re, s5_c_im, s5_d, s5_w_glu, w_out_ab, w_in_c,
           diff_lambda, diff_subln_g, w_out_c, router_w, router_bias, exp_w_gate, exp_w_up,
           exp_w_down):
    nb, l_len, d = x.shape
    ctx_len = ctx.shape[1]
    t_len = ctx_len + l_len
    nt = nb * t_len
    tpb = t_len // ROW_TILE
    ctx_tiles = ctx_len // ROW_TILE
    assert d == D_MODEL and nb < SUBLANES
    assert l_len % (2 * ROW_TILE) == 0 and ctx_len % ROW_TILE == 0 and t_len % ATT_TK == 0

    xt = jnp.concatenate([ctx, x], axis=1).reshape(nt, d)
    c_all = jnp.concatenate([c, c_ctx[None].astype(c.dtype)], axis=0)
    c_pad = jnp.zeros((SUBLANES, d), F32).at[:nb + 1].set(c_all)
    mod = _adaln(c_pad, ada_w, ada_b)

    wr_t = router_w.T
    rbias = router_bias.reshape(N_EXPERTS, 1).astype(F32)

    w0 = w_in_ab[0]
    q_w, k_w, v_w, g_w, u_w = jnp.split(w0, (RET_QK_W, 2 * RET_QK_W, 2 * RET_QK_W + RET_V_W,
                                             2 * RET_QK_W + 2 * RET_V_W), axis=1)
    qk_w = jnp.concatenate([q_w.reshape(d, RET_HEADS, RET_DK), k_w.reshape(d, RET_HEADS, RET_DK)],
                           axis=2).reshape(d, 2 * RET_QK_W)
    w0p = jnp.concatenate([qk_w, v_w, g_w, u_w], axis=1).astype(BF16)
    cos0, sin0 = _ret_rope_tables(l_len, ctx_len)
    rope_tab0 = [0] * RET_HEADS + [None] * ((w0p.shape[1] - S5_CH - 2 * RET_QK_W) // LANES)
    proj0, u5 = _inproj(xt, mod, 0, w0p, cos0, sin0, rope_tab0, RET_DK // 2,
                        nb=nb, t_len=t_len, ctx_len=ctx_len, s5_cols=S5_CH)

    log_gammas = jax.nn.log_sigmoid(ret_decay_logit[0].astype(F32))
    ret = _retention(proj0, log_gammas, nb=nb, t_len=t_len, ctx_len=ctx_len)

    s5_ops = _s5_operators(s5_lam_re[0], s5_lam_im[0], s5_log_dt[0], s5_b_re[0], s5_b_im[0],
                           s5_c_re[0], s5_c_im[0], s5_d[0])
    s5y = _s5(u5, *s5_ops, nb=nb, nctx=ctx_len // S5_CHUNK)

    x1, h2, eidx, gates = _merge0(ret, proj0, s5y, xt, mod, s5_w_glu[0].astype(BF16),
                                  w_out_ab[0].astype(BF16), ln_g[0, 0].reshape(1, d), wr_t, rbias,
                                  nb=nb, t_len=t_len, ctx_len=ctx_len)
    x2 = _moe_layer(x1, h2, eidx, gates, mod, 0, ln_g[0, 1].reshape(1, d),
                    exp_w_gate[0].astype(BF16), exp_w_up[0].astype(BF16), exp_w_down[0].astype(BF16),
                    tiles_per_batch=tpb, ctx_tiles=ctx_tiles, nb=nb)

    cos1, sin1 = _attn_rope_tables(l_len, ctx_len)
    n_heads_cols = D_MODEL // LANES
    rope_tab1 = [0] * n_heads_cols + [1] * n_heads_cols + [None] * n_heads_cols
    qkv, = _inproj(x2, mod, 1, w_in_c[0].astype(BF16), cos1, sin1, rope_tab1, DIFF_DH // 4,
                   nb=nb, t_len=t_len, ctx_len=ctx_len)
    lf = diff_lambda[0].astype(F32)
    lambda_init = 0.8 - 0.6 * math.exp(-0.3 * 1)
    lam = (jnp.exp(jnp.sum(lf[0] * lf[1])) - jnp.exp(jnp.sum(lf[2] * lf[3])) + lambda_init).reshape(1)
    att = _diff_attention(qkv, lam, diff_subln_g[0], lambda_init,
                          nb=nb, l_len=l_len, t_len=t_len, ctx_len=ctx_len)
    x3, h3, eidx1, gates1 = _merge1(att, x2, mod, 1, w_out_c[0].astype(BF16), ln_g[1, 0].reshape(1, d),
                                    wr_t, rbias, nb=nb, l_len=l_len, t_len=t_len, ctx_len=ctx_len)
    out = _moe_layer(x3, h3, eidx1, gates1, mod, 1, ln_g[1, 1].reshape(1, d),
                     exp_w_gate[1].astype(BF16), exp_w_up[1].astype(BF16), exp_w_down[1].astype(BF16),
                     tiles_per_batch=l_len // ROW_TILE, ctx_tiles=0, nb=nb)
    return out.reshape(nb, l_len, d)
```

```python
import functools
import math

import jax
import jax.numpy as jnp
import numpy as np
from jax import lax
from jax.experimental import pallas as pl
from jax.experimental.pallas import tpu as pltpu

F32 = jnp.float32
BF16 = jnp.bfloat16

D_MODEL = 1024
DEPTH = 2
GRID_W = 64
ALPHA = (2.0 * DEPTH) ** 0.25
LN_EPS = 1e-5
GN_EPS = 1e-6
ROPE_BASE = 10000.0
RET_DK = 64
RET_DV = 128
RET_HEADS = 6
RET_QK_W = RET_HEADS * RET_DK
RET_V_W = RET_HEADS * RET_DV
S5_CH = 256
S5_P = 16
S5_G = 16
S5_N = 64
DIFF_HEADS = 8
DIFF_DH = 64
N_EXPERTS = 16
EXPERTS_PER_GROUP = 4
TOP_K = 2

LANES = 128
SUBLANES = 8
MXU_DIM = 256
ROW_TILE = 256
RET_CHUNK = 256
RET_STATE_UNROLL = 3
RET_OUT_UNROLL = 11
S5_CHUNK = 8
S5_HALVES = 2
MOE_ROWS = 256
DISPATCH_TILE = 512
ROW_WORDS = D_MODEL // 2
ROW_SUB = ROW_WORDS // LANES
ATT_TK = 768
ATT_SAFE_SHIFT = 48.0
VMEM_LIMIT = 48 * 1024 * 1024


def _cparams(sem, flags=None):
    return pltpu.CompilerParams(dimension_semantics=sem, vmem_limit_bytes=VMEM_LIMIT, flags=flags)


def _dot(a, b):
    return jnp.dot(a, b, preferred_element_type=F32)


def _dot_nt(a, b):
    return lax.dot_general(a, b, (((1,), (1,)), ((), ())), preferred_element_type=F32)


def _dot_tn(a, b):
    return lax.dot_general(a, b, (((0,), (0,)), ((), ())), preferred_element_type=F32)


def _split_bf16(x):
    hi = x.astype(BF16)
    lo = (x - hi.astype(F32)).astype(BF16)
    return hi, lo


def _dot3(a, b):
    ah, al = _split_bf16(a)
    bh, bl = _split_bf16(b)
    return _dot(ah, bh) + _dot(ah, bl) + _dot(al, bh)


def _dot3_nt(a, b):
    ah, al = _split_bf16(a)
    bh, bl = _split_bf16(b)
    return _dot_nt(ah, bh) + _dot_nt(ah, bl) + _dot_nt(al, bh)


def _sigmoid(x):
    return 1.0 / (1.0 + jnp.exp(-x))


def _silu(x):
    return x * _sigmoid(x)


def _pack_bf16_pairs(v):
    half = v.shape[1] // 2
    bits = lax.bitcast_convert_type(v.astype(BF16).astype(F32), jnp.uint32)
    return (bits[:, :half] >> 16) | (bits[:, half:] & jnp.uint32(0xFFFF0000))


def _unpack_bf16_pairs(p):
    lo = lax.bitcast_convert_type(p << 16, F32)
    hi = lax.bitcast_convert_type(p & jnp.uint32(0xFFFF0000), F32)
    return jnp.concatenate([lo, hi], axis=1)


def _store_rows(ref, packed):
    for j in range(ROW_SUB):
        ref[:, j, :] = packed[:, j * LANES:(j + 1) * LANES]


def _load_rows(ref):
    return jnp.concatenate([ref[:, j, :] for j in range(ROW_SUB)], axis=1)


def _gelu_tanh(x):
    c = math.sqrt(2.0 / math.pi)
    return 0.5 * x * (1.0 + jnp.tanh(c * (x + 0.044715 * (x * x * x))))


def _adaln_kernel(c_ref, w_ref, b_ref, o_ref):
    c = c_ref[...]
    o_ref[0] = _dot3(_silu(c), w_ref[0]) + b_ref[0]


def _adaln(c_pad, ada_w, ada_b):
    depth, d, n = ada_w.shape
    tn = 1536
    return pl.pallas_call(
        _adaln_kernel,
        grid=(depth, n // tn),
        in_specs=[
            pl.BlockSpec((SUBLANES, d), lambda i, j: (0, 0)),
            pl.BlockSpec((1, d, tn), lambda i, j: (i, 0, j)),
            pl.BlockSpec((1, 1, tn), lambda i, j: (i, 0, j)),
        ],
        out_specs=pl.BlockSpec((1, SUBLANES, tn), lambda i, j: (i, 0, j)),
        out_shape=jax.ShapeDtypeStruct((depth, SUBLANES, n), F32),
        compiler_params=_cparams(("parallel", "parallel")),
        name="adaln",
    )(c_pad, ada_w, ada_b.reshape(depth, 1, n))


def _mod_row(t, tiles_per_batch, ctx_tiles, nb):
    b = lax.div(t, tiles_per_batch)
    w = lax.rem(t, tiles_per_batch)
    return jnp.where(w < ctx_tiles, nb, b)


def _rope_block(a, cos, sin, half):
    lane = lax.broadcasted_iota(jnp.int32, a.shape, 1)
    first = lax.rem(lane, 2 * half) < half
    rot = jnp.where(first, pltpu.roll(a, LANES - half, 1), pltpu.roll(a, half, 1))
    return a * cos + rot * sin


def _chunk_perm(rows, transpose):
    per = rows // S5_CHUNK
    r = lax.broadcasted_iota(jnp.int32, (rows, rows), 0)
    c = lax.broadcasted_iota(jnp.int32, (rows, rows), 1)
    if transpose:
        r, c = c, r
    return (c == S5_CHUNK * lax.rem(r, per) + lax.div(r, per)).astype(BF16)


def _inproj_kernel(x_ref, mod_ref, w_ref, cos_ref, sin_ref, o_ref, *rest, tiles_per_batch,
                   ctx_tiles, nb, rope_tab, rope_half):
    d = x_ref.shape[1]
    n = o_ref.shape[1]
    r = _mod_row(pl.program_id(0), tiles_per_batch, ctx_tiles, nb)
    sh = mod_ref[0, pl.ds(r, 1), 0:d]
    sc = mod_ref[0, pl.ds(r, 1), d:2 * d]
    xm = (x_ref[...] * (1.0 + sc) + sh).astype(BF16)
    for j in range(n // MXU_DIM):
        acc = _dot(xm, w_ref[:, j * MXU_DIM:(j + 1) * MXU_DIM])
        parts = []
        for s in range(MXU_DIM // LANES):
            blk = acc[:, s * LANES:(s + 1) * LANES]
            tab = rope_tab[j * (MXU_DIM // LANES) + s]
            if tab is not None:
                blk = _rope_block(blk, cos_ref[tab], sin_ref[tab], rope_half)
            parts.append(blk)
        o_ref[:, j * MXU_DIM:(j + 1) * MXU_DIM] = jnp.concatenate(parts, axis=1).astype(BF16)
    if rest:
        u_ref, = rest
        rows = x_ref.shape[0]
        per = rows // S5_CHUNK
        u = _dot(xm, w_ref[:, n:n + S5_CH]).astype(BF16)
        up = _dot(_chunk_perm(rows, False), u).astype(BF16)
        for s in range(S5_CHUNK):
            for hf in range(S5_CH // LANES):
                u_ref[hf, :, s * LANES:(s + 1) * LANES] = up[s * per:(s + 1) * per,
                                                             hf * LANES:(hf + 1) * LANES]


def _inproj(x, mod, layer, w, cos, sin, rope_tab, rope_half, *, nb, t_len, ctx_len, s5_cols=0):
    nt, d = x.shape
    n = w.shape[1] - s5_cols
    tpb = t_len // ROW_TILE
    kern = functools.partial(_inproj_kernel, tiles_per_batch=tpb, ctx_tiles=ctx_len // ROW_TILE,
                             nb=nb, rope_tab=tuple(rope_tab), rope_half=rope_half)
    ntab = cos.shape[0]
    out_specs = [pl.BlockSpec((ROW_TILE, n), lambda t: (t, 0))]
    out_shape = [jax.ShapeDtypeStruct((nt, n), BF16)]
    if s5_cols:
        halves, per = s5_cols // LANES, ROW_TILE // S5_CHUNK
        out_specs.append(pl.BlockSpec((halves, per, S5_CHUNK * LANES), lambda t: (0, t, 0)))
        out_shape.append(jax.ShapeDtypeStruct((halves, nt // S5_CHUNK, S5_CHUNK * LANES), BF16))
    return pl.pallas_call(
        kern,
        grid=(nt // ROW_TILE,),
        in_specs=[
            pl.BlockSpec((ROW_TILE, d), lambda t: (t, 0)),
            pl.BlockSpec((1, SUBLANES, mod.shape[2]), lambda t: (layer, 0, 0)),
            pl.BlockSpec(w.shape, lambda t: (0, 0)),
            pl.BlockSpec((ntab, ROW_TILE, LANES), lambda t: (0, lax.rem(t, tpb), 0)),
            pl.BlockSpec((ntab, ROW_TILE, LANES), lambda t: (0, lax.rem(t, tpb), 0)),
        ],
        out_specs=out_specs,
        out_shape=out_shape,
        compiler_params=_cparams(("parallel",)),
        name="inproj%d" % layer,
    )(x, mod, w, cos, sin)


def _retention_kernel(lg_ref, qk_ref, v_ref, o_ref, sf_ref, sb_ref, *, nctx):
    c_len = RET_CHUNK
    t_len = qk_ref.shape[0]
    nc = t_len // c_len
    h = pl.program_id(1)
    lgf = lg_ref[0, h]
    lgb = lg_ref[1, h]
    ii = lax.broadcasted_iota(jnp.int32, (c_len, 1), 0).astype(F32)
    jj = lax.broadcasted_iota(jnp.int32, (1, c_len), 1).astype(F32)
    diff = ii - jj
    decay = jnp.where(diff >= 0.0, jnp.exp(lgf * jnp.maximum(diff, 0.0)),
                      jnp.exp(lgb * jnp.maximum(-diff, 0.0)))
    kdf = jnp.exp(lgf * (c_len - 1.0 - ii))
    kdb = jnp.exp(lgb * ii)
    qdf = jnp.exp(lgf * (ii + 1.0))
    qdb = jnp.exp(lgb * (c_len - ii))
    zrow = jnp.zeros((1, RET_DV), F32)
    gf_chunk = jnp.exp(zrow + lgf * c_len)
    gb_chunk = jnp.exp(zrow + lgb * c_len)

    def load(c):
        rows = pl.ds(pl.multiple_of(c * c_len, c_len), c_len)
        qk = qk_ref[rows, :].astype(F32)
        return qk[:, :RET_DK], qk[:, RET_DK:], v_ref[rows, :]

    def states(j, carry):
        sf, sb = carry
        cb = jnp.where(j < nctx, nctx - 1 - j, nc - 1 - (j - nctx))
        sf_ref[j] = sf
        sb_ref[cb] = sb
        _, kf, vf = load(j)
        _, kb, vb = load(cb)
        return (gf_chunk * sf + _dot_tn((kf * kdf).astype(BF16), vf),
                gb_chunk * sb + _dot_tn((kb * kdb).astype(BF16), vb))

    zero_state = jnp.zeros((RET_DK, RET_DV), F32)
    lax.fori_loop(0, nc, states, (zero_state, zero_state), unroll=RET_STATE_UNROLL)

    def out_chunk(c, carry):
        q, k, v = load(c)
        scores = _dot_nt(q.astype(BF16), k.astype(BF16)) * decay
        o = _dot(scores.astype(BF16), v)
        o = o + _dot((q * qdf).astype(BF16), sf_ref[c].astype(BF16))
        o = o + _dot((q * qdb).astype(BF16), sb_ref[c].astype(BF16))
        mu = jnp.mean(o, axis=-1, keepdims=True)
        oc = o - mu
        var = jnp.mean(oc * oc, axis=-1, keepdims=True)
        rows = pl.ds(pl.multiple_of(c * c_len, c_len), c_len)
        o_ref[rows, :] = (oc * lax.rsqrt(var + GN_EPS)).astype(BF16)
        return carry

    lax.fori_loop(0, nc, out_chunk, 0, unroll=RET_OUT_UNROLL)


def _retention(proj, log_gammas, *, nb, t_len, ctx_len):
    nt = proj.shape[0]
    nc = t_len // RET_CHUNK
    kern = functools.partial(_retention_kernel, nctx=ctx_len // RET_CHUNK)
    vcol0 = RET_HEADS
    return pl.pallas_call(
        kern,
        grid_spec=pltpu.PrefetchScalarGridSpec(
            num_scalar_prefetch=1,
            grid=(nb, RET_HEADS),
            in_specs=[
                pl.BlockSpec((t_len, LANES), lambda b, h, lg: (b, h)),
                pl.BlockSpec((t_len, LANES), lambda b, h, lg: (b, vcol0 + h)),
            ],
            out_specs=pl.BlockSpec((t_len, LANES), lambda b, h, lg: (b, h)),
            scratch_shapes=[pltpu.VMEM((nc, RET_DK, RET_DV), F32),
                            pltpu.VMEM((nc, RET_DK, RET_DV), F32)],
        ),
        out_shape=jax.ShapeDtypeStruct((nt, RET_V_W), BF16),
        compiler_params=_cparams(("parallel", "parallel")),
        name="retention",
    )(log_gammas, proj, proj)


def _s5_operators(lam_re, lam_im, log_dt, b_re, b_im, c_re, c_im, d_skip):
    tc = S5_CHUNK
    hp = lax.Precision.HIGHEST
    ks = jnp.arange(tc + 1, dtype=F32)
    pw, bbar, cm = [], [], []
    for direction in range(2):
        dt = jnp.exp(log_dt[direction].astype(F32))[:, None]
        lam = lax.complex(lam_re[direction].astype(F32), lam_im[direction].astype(F32))
        z = lam * dt
        p = jnp.exp(z[None] * ks[:, None, None])
        lam_bar = p[1]
        bb = ((lam_bar - 1.0) / lam)[..., None] * lax.complex(
            b_re[direction].astype(F32), b_im[direction].astype(F32))
        pw.append(p)
        bbar.append(bb)
        cm.append(lax.complex(c_re[direction].astype(F32), c_im[direction].astype(F32)))

    def lag_kernel(p, bb, c):
        return jnp.einsum('gpn,kgn,gnq->kgpq', c, p[:tc], bb, precision=hp).real

    kf = lag_kernel(pw[0], bbar[0], cm[0])
    kb = lag_kernel(pw[1], bbar[1], cm[1])
    k0 = kf[0] + kb[0] + jnp.eye(S5_P, dtype=F32)[None] * d_skip.astype(F32)[:, :, None]
    kcat = jnp.concatenate([kb[1:][::-1], k0[None], kf[1:]], axis=0)
    s_idx = jnp.arange(tc)[:, None]
    t_idx = jnp.arange(tc)[None, :]
    m5 = kcat[t_idx - s_idx + tc - 1]
    hg = S5_G // S5_HALVES
    eye = jnp.eye(hg, dtype=F32)
    split = lambda z, axis: z.reshape(z.shape[:axis] + (S5_HALVES, hg) + z.shape[axis + 1:])
    wide = tc * hg * S5_P
    intra = jnp.einsum('sthgpq,gk->hsgqtkp', split(m5, 2), eye).reshape(S5_HALVES, wide, wide)

    ef = pw[0][:tc][::-1][:, :, :, None] * bbar[0][None]
    eb = pw[1][:tc][:, :, :, None] * bbar[1][None]
    to_in = lambda e: jnp.einsum('shgnq,gk->hsgqkn', split(e, 1), eye).reshape(
        S5_HALVES, wide, hg * S5_N)
    w_in = jnp.concatenate([to_in(ef.real), to_in(ef.imag), to_in(eb.real), to_in(eb.imag)], axis=2)
    of = cm[0][None] * pw[0][1:][:, :, None, :]
    ob = cm[1][None] * pw[1][1:][::-1][:, :, None, :]
    to_out = lambda o: jnp.einsum('thgpn,gk->hkntgp', split(o, 1), eye).reshape(
        S5_HALVES, hg * S5_N, wide)
    w_out = jnp.concatenate([to_out(of.real), -to_out(of.imag), to_out(ob.real), -to_out(ob.imag)],
                            axis=1)
    a = jnp.stack([pw[0][tc].real, pw[0][tc].imag, pw[1][tc].real, pw[1][tc].imag], axis=0)
    a = a.reshape(4, S5_HALVES, hg * S5_N).transpose(1, 0, 2)
    return intra.astype(BF16), w_in.astype(BF16), w_out.astype(BF16), a


def _s5_kernel(x_ref, wi_ref, win_ref, wout_ref, a_ref, y_ref, st_ref, *, nctx):
    x = x_ref[0]
    nc = x.shape[0]
    w = a_ref.shape[2]
    st_ref[...] = _dot(x, win_ref[0])
    afr, afi, abr, abi = (a_ref[0, i:i + 1, :] for i in range(4))

    def step(j, carry):
        fr, fi, br, bi = carry
        rf = pl.ds(j, 1)
        rb = pl.ds(jnp.where(j < nctx, nctx - 1 - j, nc - 1 - (j - nctx)), 1)
        efr, efi = st_ref[rf, 0:w], st_ref[rf, w:2 * w]
        ebr, ebi = st_ref[rb, 2 * w:3 * w], st_ref[rb, 3 * w:4 * w]
        st_ref[rf, 0:w] = fr
        st_ref[rf, w:2 * w] = fi
        st_ref[rb, 2 * w:3 * w] = br
        st_ref[rb, 3 * w:4 * w] = bi
        return (afr * fr - afi * fi + efr, afr * fi + afi * fr + efi,
                abr * br - abi * bi + ebr, abr * bi + abi * br + ebi)

    z = jnp.zeros((1, w), F32)
    lax.fori_loop(0, nc, step, (z, z, z, z), unroll=4)
    y_ref[0] = (_dot(x, wi_ref[0]) + _dot(st_ref[...].astype(BF16), wout_ref[0])).astype(BF16)


def _s5(xc, intra, w_in, w_out, a, *, nb, nctx):
    halves, rows, wide = xc.shape
    nc = rows // nb
    kern = functools.partial(_s5_kernel, nctx=nctx)
    per_half = lambda arr: pl.BlockSpec((1,) + arr.shape[1:], lambda hf, b: (hf, 0, 0))
    return pl.pallas_call(
        kern,
        grid=(halves, nb),
        in_specs=[
            pl.BlockSpec((1, nc, wide), lambda hf, b: (hf, b, 0)),
            per_half(intra), per_half(w_in), per_half(w_out), per_half(a),
        ],
        out_specs=pl.BlockSpec((1, nc, wide), lambda hf, b: (hf, b, 0)),
        out_shape=jax.ShapeDtypeStruct(xc.shape, BF16),
        scratch_shapes=[pltpu.VMEM((nc, w_in.shape[2]), F32)],
        compiler_params=_cparams(("parallel", "parallel")),
        name="s5",
    )(xc, intra, w_in, w_out, a)


def _route(logits_t, bias):
    scores = _sigmoid(logits_t)
    biased = scores + bias
    s_rows = [scores[e:e + 1, :] for e in range(N_EXPERTS)]
    b_rows = [biased[e:e + 1, :] for e in range(N_EXPERTS)]
    n_groups = N_EXPERTS // EXPERTS_PER_GROUP
    best = None
    sel = None
    for g in range(n_groups):
        a, b, c, d = b_rows[4 * g:4 * g + 4]
        hi1, lo1 = jnp.maximum(a, b), jnp.minimum(a, b)
        hi2, lo2 = jnp.maximum(c, d), jnp.minimum(c, d)
        top1 = jnp.maximum(hi1, hi2)
        top2 = jnp.maximum(jnp.minimum(hi1, hi2), jnp.maximum(lo1, lo2))
        gs = top1 + top2
        if g == 0:
            best, sel = gs, jnp.zeros(gs.shape, jnp.int32)
        else:
            better = gs > best
            sel = jnp.where(better, g, sel)
            best = jnp.where(better, gs, best)
    neg = jnp.full(best.shape, -jnp.inf, F32)
    masked = [jnp.where(sel == (e // EXPERTS_PER_GROUP), b_rows[e], neg) for e in range(N_EXPERTS)]
    v1, i1, g1 = masked[0], jnp.zeros(best.shape, jnp.int32), s_rows[0]
    for e in range(1, N_EXPERTS):
        better = masked[e] > v1
        v1 = jnp.where(better, masked[e], v1)
        i1 = jnp.where(better, e, i1)
        g1 = jnp.where(better, s_rows[e], g1)
    v2, i2, g2 = neg, jnp.zeros(best.shape, jnp.int32), jnp.zeros(best.shape, F32)
    for e in range(N_EXPERTS):
        cand = jnp.where(i1 == e, neg, masked[e])
        better = cand > v2
        v2 = jnp.where(better, cand, v2)
        i2 = jnp.where(better, e, i2)
        g2 = jnp.where(better, s_rows[e], g2)
    tot = g1 + g2
    return jnp.concatenate([i1, i2], axis=0), jnp.concatenate([g1 / tot, g2 / tot], axis=0)


def _tail(x, o, mod_ref, r, lng, wr, rb, x1_ref, h2_ref, ei_ref, gt_ref):
    d = x.shape[1]
    g1 = mod_ref[0, pl.ds(r, 1), 2 * d:3 * d]
    sh2 = mod_ref[0, pl.ds(r, 1), 3 * d:4 * d]
    sc2 = mod_ref[0, pl.ds(r, 1), 4 * d:5 * d]
    y = ALPHA * x + g1 * o
    mu = jnp.mean(y, axis=-1, keepdims=True)
    yc = y - mu
    var = jnp.mean(yc * yc, axis=-1, keepdims=True)
    x1 = yc * lax.rsqrt(var + LN_EPS) * lng
    h2 = x1 * (1.0 + sc2) + sh2
    x1_ref[...] = x1
    _store_rows(h2_ref, _pack_bf16_pairs(h2))
    ei, gt = _route(_dot3_nt(wr, h2), rb)
    ei_ref[...] = ei
    gt_ref[...] = gt


def _merge0_kernel(r_ref, g_ref, s_ref, x_ref, mod_ref, wglu_ref, wout_ref, lng_ref, wr_ref,
                   rb_ref, x1_ref, h2_ref, ei_ref, gt_ref, *, tiles_per_batch, ctx_tiles, nb):
    r = _mod_row(pl.program_id(0), tiles_per_batch, ctx_tiles, nb)
    ret = r_ref[...].astype(F32) * _silu(g_ref[...].astype(F32))
    rows = x_ref.shape[0]
    sp = jnp.concatenate(
        [jnp.concatenate([s_ref[hf, :, s * LANES:(s + 1) * LANES] for hf in range(S5_HALVES)], axis=1)
         for s in range(S5_CHUNK)], axis=0)
    s5 = _dot(_chunk_perm(rows, True), sp)
    z = _dot(_gelu_tanh(s5).astype(BF16), wglu_ref[...])
    zz = z[:, :S5_CH] * _sigmoid(z[:, S5_CH:])
    o = _dot(ret.astype(BF16), wout_ref[0:RET_V_W, :]) + _dot(zz.astype(BF16), wout_ref[RET_V_W:, :])
    _tail(x_ref[...], o, mod_ref, r, lng_ref[...], wr_ref[...], rb_ref[...],
          x1_ref, h2_ref, ei_ref, gt_ref)


def _merge1_kernel(a_ref, x_ref, mod_ref, wout_ref, lng_ref, wr_ref, rb_ref,
                   x1_ref, h2_ref, ei_ref, gt_ref, *, tiles_per_batch):
    r = lax.div(pl.program_id(0), tiles_per_batch)
    o = _dot(a_ref[...], wout_ref[...])
    _tail(x_ref[...], o, mod_ref, r, lng_ref[...], wr_ref[...], rb_ref[...],
          x1_ref, h2_ref, ei_ref, gt_ref)


def _tail_outs(n_rows, d):
    shapes = (jax.ShapeDtypeStruct((n_rows, d), F32),
              jax.ShapeDtypeStruct((n_rows, ROW_SUB, LANES), jnp.uint32),
              jax.ShapeDtypeStruct((TOP_K, n_rows), jnp.int32),
              jax.ShapeDtypeStruct((TOP_K, n_rows), F32))
    specs = (pl.BlockSpec((ROW_TILE, d), lambda t: (t, 0)),
             pl.BlockSpec((ROW_TILE, ROW_SUB, LANES), lambda t: (t, 0, 0)),
             pl.BlockSpec((TOP_K, ROW_TILE), lambda t: (0, t)),
             pl.BlockSpec((TOP_K, ROW_TILE), lambda t: (0, t)))
    return shapes, specs


def _merge0(ret, proj, s5y, x, mod, w_glu, w_out, lng, wr_t, rbias, *, nb, t_len, ctx_len):
    nt, d = x.shape
    tpb = t_len // ROW_TILE
    kern = functools.partial(_merge0_kernel, tiles_per_batch=tpb, ctx_tiles=ctx_len // ROW_TILE, nb=nb)
    shapes, specs = _tail_outs(nt, d)
    gcol = (2 * RET_QK_W + RET_V_W) // RET_V_W
    full = lambda a: pl.BlockSpec(a.shape, lambda t: (0,) * a.ndim)
    return pl.pallas_call(
        kern,
        grid=(nt // ROW_TILE,),
        in_specs=[
            pl.BlockSpec((ROW_TILE, RET_V_W), lambda t: (t, 0)),
            pl.BlockSpec((ROW_TILE, RET_V_W), lambda t: (t, gcol)),
            pl.BlockSpec((S5_HALVES, ROW_TILE // S5_CHUNK, S5_CHUNK * LANES), lambda t: (0, t, 0)),
            pl.BlockSpec((ROW_TILE, d), lambda t: (t, 0)),
            pl.BlockSpec((1, SUBLANES, mod.shape[2]), lambda t: (0, 0, 0)),
            full(w_glu), full(w_out), full(lng), full(wr_t), full(rbias),
        ],
        out_specs=specs,
        out_shape=shapes,
        compiler_params=_cparams(("parallel",)),
        name="merge0",
    )(ret, proj, s5y, x, mod, w_glu, w_out, lng, wr_t, rbias)


def _merge1(att, x, mod, layer, w_out, lng, wr_t, rbias, *, nb, l_len, t_len, ctx_len):
    n_lat, d = att.shape
    tpb = l_len // ROW_TILE
    tpb_t = t_len // ROW_TILE
    ctx_tiles = ctx_len // ROW_TILE
    kern = functools.partial(_merge1_kernel, tiles_per_batch=tpb)
    shapes, specs = _tail_outs(n_lat, d)
    full = lambda a: pl.BlockSpec(a.shape, lambda t: (0,) * a.ndim)
    xrow = lambda t: (lax.div(t, tpb) * tpb_t + ctx_tiles + lax.rem(t, tpb), 0)
    return pl.pallas_call(
        kern,
        grid=(n_lat // ROW_TILE,),
        in_specs=[
            pl.BlockSpec((ROW_TILE, d), lambda t: (t, 0)),
            pl.BlockSpec((ROW_TILE, d), xrow),
            pl.BlockSpec((1, SUBLANES, mod.shape[2]), lambda t: (layer, 0, 0)),
            full(w_out), full(lng), full(wr_t), full(rbias),
        ],
        out_specs=specs,
        out_shape=shapes,
        compiler_params=_cparams(("parallel",)),
        name="merge1",
    )(att, x, mod, w_out, lng, wr_t, rbias)


def _moe_plan(eidx):
    k, n = eidx.shape
    a = k * n
    e_flat = eidx.reshape(a)
    onehot = (e_flat[:, None] == jnp.arange(N_EXPERTS, dtype=jnp.int32)[None, :]).astype(jnp.int32)
    csum = jnp.cumsum(onehot, axis=0)
    counts = csum[-1]
    rank = jnp.take_along_axis(csum, e_flat[:, None], axis=1)[:, 0] - 1
    padded = (counts + MOE_ROWS - 1) // MOE_ROWS * MOE_ROWS
    pad_end = jnp.cumsum(padded)
    pad_start = pad_end - padded
    dest = (pad_start[e_flat] + rank).astype(jnp.int32)
    n_blocks = -(-(a + N_EXPERTS * (MOE_ROWS - 1)) // MOE_ROWS)
    block_expert = jnp.minimum(
        jnp.searchsorted(pad_end, jnp.arange(n_blocks, dtype=jnp.int32) * MOE_ROWS, side='right'),
        N_EXPERTS - 1).astype(jnp.int32)
    return dest.reshape(k, n), block_expert, n_blocks


def _tile_rows_of(dest, tile):
    k, n = dest.shape
    return dest.reshape(k, n // tile, tile).transpose(1, 0, 2).reshape(n // tile, 1, k * tile)


def _dispatch_kernel(dest_ref, h_ref, xs_in_hbm, xs_hbm, sem):
    del xs_in_hbm
    rows = h_ref.shape[0]

    def start(r, c):
        for choice in range(TOP_K):
            pltpu.make_async_copy(h_ref.at[r], xs_hbm.at[dest_ref[0, 0, choice * rows + r]], sem).start()
        return c

    lax.fori_loop(0, rows, start, 0, unroll=8)
    for _ in range(TOP_K):
        pltpu.make_async_copy(h_ref, xs_hbm.at[pl.ds(0, rows)], sem).wait()


def _dispatch(h, dest, n_rows):
    n = h.shape[0]
    tile = DISPATCH_TILE if n % DISPATCH_TILE == 0 else ROW_TILE
    return pl.pallas_call(
        _dispatch_kernel,
        grid=(n // tile,),
        in_specs=[
            pl.BlockSpec((1, 1, TOP_K * tile), lambda t: (t, 0, 0), memory_space=pltpu.SMEM),
            pl.BlockSpec((tile, ROW_SUB, LANES), lambda t: (t, 0, 0)),
            pl.BlockSpec(memory_space=pl.ANY),
        ],
        out_specs=pl.BlockSpec(memory_space=pl.ANY),
        out_shape=jax.ShapeDtypeStruct((n_rows, ROW_SUB, LANES), jnp.uint32),
        scratch_shapes=[pltpu.SemaphoreType.DMA],
        input_output_aliases={2: 0},
        compiler_params=_cparams(("arbitrary",)),
        name="moe_dispatch",
    )(_tile_rows_of(dest, tile), h, jnp.zeros((n_rows, ROW_SUB, LANES), jnp.uint32))


def _experts_kernel(be_ref, x_ref, wg_ref, wu_ref, wd_ref, o_ref):
    x = _unpack_bf16_pairs(_load_rows(x_ref)).astype(BF16)
    hg = _dot(x, wg_ref[0])
    hu = _dot(x, wu_ref[0])
    _store_rows(o_ref, _pack_bf16_pairs(_dot((_silu(hg) * hu).astype(BF16), wd_ref[0])))


def _experts(xs, block_expert, wg, wu, wd):
    n_blocks = block_expert.shape[0]
    d, dff = wg.shape[1], wg.shape[2]
    rows_spec = pl.BlockSpec((MOE_ROWS, ROW_SUB, LANES), lambda i, be: (i, 0, 0))
    return pl.pallas_call(
        _experts_kernel,
        grid_spec=pltpu.PrefetchScalarGridSpec(
            num_scalar_prefetch=1,
            grid=(n_blocks,),
            in_specs=[
                rows_spec,
                pl.BlockSpec((1, d, dff), lambda i, be: (be[i], 0, 0)),
                pl.BlockSpec((1, d, dff), lambda i, be: (be[i], 0, 0)),
                pl.BlockSpec((1, dff, d), lambda i, be: (be[i], 0, 0)),
            ],
            out_specs=rows_spec,
        ),
        out_shape=jax.ShapeDtypeStruct(xs.shape, jnp.uint32),
        compiler_params=_cparams(("parallel",)),
        name="moe_experts",
    )(block_expert, xs, wg, wu, wd)


def _combine_kernel(dcur_ref, dnxt_ref, x_ref, gt_ref, mod_ref, lng_ref, y_hbm, o_ref, ybuf, sem, *,
                    tiles_per_batch, ctx_tiles, nb):
    t = pl.program_id(0)
    last = pl.num_programs(0) - 1
    rows = x_ref.shape[0]
    n = TOP_K * rows
    d = x_ref.shape[1]
    slot = lax.rem(t, 2)

    def start_all(idx_ref, s):
        def start(j, c):
            pltpu.make_async_copy(y_hbm.at[idx_ref[0, 0, j]], ybuf.at[s, j], sem.at[s]).start()
            return c

        lax.fori_loop(0, n, start, 0, unroll=8)

    @pl.when(t == 0)
    def _():
        start_all(dcur_ref, 0)

    @pl.when(t < last)
    def _():
        start_all(dnxt_ref, 1 - slot)

    pltpu.make_async_copy(y_hbm.at[pl.ds(0, n)], ybuf.at[slot], sem.at[slot]).wait()
    r = _mod_row(t, tiles_per_batch, ctx_tiles, nb)
    gt = gt_ref[...]
    yb = ybuf.at[slot]
    y = (_unpack_bf16_pairs(_load_rows(yb.at[pl.ds(0, rows)])) * gt[:, 0:1]
         + _unpack_bf16_pairs(_load_rows(yb.at[pl.ds(rows, rows)])) * gt[:, 1:2])
    g2 = mod_ref[0, pl.ds(r, 1), 5 * d:6 * d]
    z = ALPHA * x_ref[...] + g2 * y
    mu = jnp.mean(z, axis=-1, keepdims=True)
    zc = z - mu
    var = jnp.mean(zc * zc, axis=-1, keepdims=True)
    o_ref[...] = zc * lax.rsqrt(var + LN_EPS) * lng_ref[...]


def _combine(x1, dest, gates, mod, layer, lng, ys, *, tiles_per_batch, ctx_tiles, nb):
    n, d = x1.shape
    nt = n // ROW_TILE
    dest_t = _tile_rows_of(dest, ROW_TILE)
    kern = functools.partial(_combine_kernel, tiles_per_batch=tiles_per_batch, ctx_tiles=ctx_tiles, nb=nb)
    idx_spec = lambda f: pl.BlockSpec((1, 1, TOP_K * ROW_TILE), f, memory_space=pltpu.SMEM)
    return pl.pallas_call(
        kern,
        grid=(nt,),
        in_specs=[
            idx_spec(lambda t: (t, 0, 0)),
            idx_spec(lambda t: (jnp.minimum(t + 1, nt - 1), 0, 0)),
            pl.BlockSpec((ROW_TILE, d), lambda t: (t, 0)),
            pl.BlockSpec((ROW_TILE, TOP_K), lambda t: (t, 0)),
            pl.BlockSpec((1, SUBLANES, mod.shape[2]), lambda t: (layer, 0, 0)),
            pl.BlockSpec((1, d), lambda t: (0, 0)),
            pl.BlockSpec(memory_space=pl.ANY),
        ],
        out_specs=pl.BlockSpec((ROW_TILE, d), lambda t: (t, 0)),
        out_shape=jax.ShapeDtypeStruct((n, d), F32),
        scratch_shapes=[pltpu.VMEM((2, TOP_K * ROW_TILE, ROW_SUB, LANES), jnp.uint32),
                        pltpu.SemaphoreType.DMA((2,))],
        compiler_params=_cparams(("arbitrary",)),
        name="moe_combine%d" % layer,
    )(dest_t, dest_t, x1, gates.T, mod, lng, ys)


def _moe_layer(x1, h2, eidx, gates, mod, layer, lng, wg, wu, wd, *, tiles_per_batch, ctx_tiles, nb):
    dest, block_expert, n_blocks = _moe_plan(eidx)
    xs = _dispatch(h2, dest, n_blocks * MOE_ROWS)
    ys = _experts(xs, block_expert, wg, wu, wd)
    return _combine(x1, dest, gates, mod, layer, lng, ys,
                    tiles_per_batch=tiles_per_batch, ctx_tiles=ctx_tiles, nb=nb)


def _half_norms(x):
    lane = lax.broadcasted_iota(jnp.int32, x.shape, 1)
    sq = x * x
    lo = jnp.sum(jnp.where(lane < DIFF_DH, sq, 0.0), axis=-1, keepdims=True)
    hi = jnp.sum(jnp.where(lane >= DIFF_DH, sq, 0.0), axis=-1, keepdims=True)
    return jnp.sqrt(lo), jnp.sqrt(hi)


def _attn_kernel(lam_ref, q_ref, k_ref, v_ref, g_ref, o_ref, vext, s_buf0, s_buf1,
                 p_buf0, p_buf1, corr_buf0, corr_buf1, m_buf, acc, *, out_scale, ctx_len, online):
    t_len = k_ref.shape[0]
    nk = t_len // ATT_TK
    dv = v_ref.shape[1]
    tq = acc.shape[1]
    nq = o_ref.shape[0] // tq
    n_tiles = nq * nk

    vext[:, 0:dv] = v_ref[...]
    vext[:, dv:2 * dv] = jnp.ones((t_len, dv), BF16)
    lam = lam_ref[0]
    gain = g_ref[...] * out_scale

    s_bufs, p_bufs, corr_bufs = (s_buf0, s_buf1), (p_buf0, p_buf1), (corr_buf0, corr_buf1)

    def key_rows(kj):
        return pl.ds(pl.multiple_of(kj * ATT_TK, ATT_TK), ATT_TK)

    def query_rows(qi):
        return pl.ds(pl.multiple_of(ctx_len + qi * tq, ROW_TILE), tq)

    if not online:
        def key_norms(kj, carry):
            lo, hi = _half_norms(k_ref[key_rows(kj), :].astype(F32))
            return (jnp.maximum(carry[0], jnp.max(lo, axis=0, keepdims=True)),
                    jnp.maximum(carry[1], jnp.max(hi, axis=0, keepdims=True)))

        zero = jnp.zeros((1, 1), F32)
        kmax = lax.fori_loop(0, nk, key_norms, (zero, zero))

        def query_shifts(qi, carry):
            lo, hi = _half_norms(q_ref[query_rows(qi), :].astype(F32))
            m_buf[qi, :, 0:1] = lo * kmax[0]
            m_buf[qi, :, 1:2] = hi * kmax[1]
            return carry

        lax.fori_loop(0, nq, query_shifts, 0)

    def advance(tile):
        qi, kj = tile
        wrap = kj + 1 == nk
        return jnp.where(wrap, qi + 1, qi), jnp.where(wrap, 0, kj + 1)

    def scores(tile, slot):
        qi, kj = tile
        q = q_ref[query_rows(qi), :]
        lane = lax.broadcasted_iota(jnp.int32, q.shape, 1)
        zero = jnp.zeros(q.shape, q.dtype)
        k = k_ref[key_rows(kj), :]
        s_bufs[slot][0] = _dot_nt(jnp.where(lane < DIFF_DH, q, zero), k)
        s_bufs[slot][1] = _dot_nt(jnp.where(lane >= DIFF_DH, q, zero), k)

    def numerators(tile, slot):
        qi, kj = tile
        for w in range(2):
            s = s_bufs[slot][w]
            if online:
                m_old = jnp.where(kj == 0, -jnp.inf, m_buf[w])
                m_new = jnp.maximum(m_old, jnp.max(s, axis=-1, keepdims=True))
                corr_bufs[slot][w] = jnp.exp2(m_old - m_new)
                m_buf[w] = m_new
            else:
                m_new = m_buf[qi, :, w:w + 1]
            p_bufs[slot][w] = jnp.exp2(s - m_new).astype(BF16)

    def values(tile, slot):
        qi, kj = tile
        ve = vext[key_rows(kj), :]
        a = []
        for w in range(2):
            keep = corr_bufs[slot][w] if online else jnp.where(kj == 0, 0.0, 1.0)
            a.append(keep * acc[w] + _dot(p_bufs[slot][w], ve))
            acc[w] = a[w]
        o = a[0][:, 0:dv] / a[0][:, dv:2 * dv] - lam * (a[1][:, 0:dv] / a[1][:, dv:2 * dv])
        o = o * lax.rsqrt(jnp.mean(o * o, axis=-1, keepdims=True) + GN_EPS)
        o_ref[pl.ds(pl.multiple_of(qi * tq, tq), tq), :] = (o * gain).astype(BF16)

    def step(tiles, slot):
        a, b, c = tiles
        values(c, slot)
        scores(a, slot)
        numerators(b, 1 - slot)
        return advance(a), a, b

    acc[...] = jnp.zeros(acc.shape, F32)
    t0 = (jnp.int32(0), jnp.int32(0))
    t1 = advance(t0)
    scores(t0, 0)
    scores(t1, 1)
    numerators(t0, 0)

    def pair(_, tiles):
        return step(step(tiles, 0), 1)

    _, last, prev = lax.fori_loop(0, (n_tiles - 2) // 2, pair, (advance(t1), t1, t0))
    numerators(last, (n_tiles - 1) % 2)
    values(prev, n_tiles % 2)
    values(last, (n_tiles - 1) % 2)


def _diff_attention(qkv, lam, subln_g, lambda_init, *, nb, l_len, t_len, ctx_len, online):
    d = D_MODEL
    tq = 2 * ROW_TILE
    dv = 2 * DIFF_DH
    nq = l_len // tq
    assert (nq * (t_len // ATT_TK)) % 2 == 0
    kern = functools.partial(_attn_kernel, out_scale=1.0 - lambda_init, ctx_len=ctx_len, online=online)
    shift_buf = pltpu.VMEM((2, tq, 1), F32) if online else pltpu.VMEM((nq, tq, 2), F32)
    return pl.pallas_call(
        kern,
        grid_spec=pltpu.PrefetchScalarGridSpec(
            num_scalar_prefetch=1,
            grid=(nb, DIFF_HEADS),
            in_specs=[
                pl.BlockSpec((t_len, LANES), lambda b, h, lam: (b, h)),
                pl.BlockSpec((t_len, LANES), lambda b, h, lam: (b, DIFF_HEADS + h)),
                pl.BlockSpec((t_len, LANES), lambda b, h, lam: (b, 2 * DIFF_HEADS + h)),
                pl.BlockSpec((1, LANES), lambda b, h, lam: (0, 0)),
            ],
            out_specs=pl.BlockSpec((l_len, LANES), lambda b, h, lam: (b, h)),
            scratch_shapes=[
                pltpu.VMEM((t_len, 2 * dv), BF16),
                pltpu.VMEM((2, tq, ATT_TK), F32), pltpu.VMEM((2, tq, ATT_TK), F32),
                pltpu.VMEM((2, tq, ATT_TK), BF16), pltpu.VMEM((2, tq, ATT_TK), BF16),
                pltpu.VMEM((2, tq, 1), F32), pltpu.VMEM((2, tq, 1), F32),
                shift_buf,
                pltpu.VMEM((2, tq, 2 * dv), F32),
            ],
        ),
        out_shape=jax.ShapeDtypeStruct((nb * l_len, d), BF16),
        compiler_params=_cparams(("parallel", "parallel")),
        name="diff_attention_online" if online else "diff_attention",
    )(lam, qkv, qkv, qkv, subln_g.reshape(1, LANES).astype(F32))


def _ret_rope_tables(l_len, ctx_len):
    half = RET_DK // 2
    inv = ROPE_BASE ** (-jnp.arange(0, RET_DK, 2, dtype=F32) / RET_DK)
    ang = jnp.arange(l_len, dtype=F32)[:, None] * inv[None, :]
    ang = jnp.concatenate([jnp.zeros((ctx_len, half), F32), ang], axis=0)
    cos64 = jnp.concatenate([jnp.cos(ang), jnp.cos(ang)], axis=1)
    sin64 = jnp.concatenate([-jnp.sin(ang), jnp.sin(ang)], axis=1)
    kscale = RET_DK ** -0.5
    cos = jnp.concatenate([cos64, cos64 * kscale], axis=1)
    sin = jnp.concatenate([sin64, sin64 * kscale], axis=1)
    return cos[None], sin[None]


def _attn_rope_tables(l_len, ctx_len):
    quarter = DIFF_DH // 4
    inv = ROPE_BASE ** (-jnp.arange(0, DIFF_DH // 2, 2, dtype=F32) / (DIFF_DH // 2))
    pos = jnp.arange(l_len)
    ang_r = (pos // GRID_W).astype(F32)[:, None] * inv[None, :]
    ang_c = (pos % GRID_W).astype(F32)[:, None] * inv[None, :]
    pad = lambda a: jnp.concatenate([jnp.zeros((ctx_len, quarter), F32), a], axis=0)
    ang_r, ang_c = pad(ang_r), pad(ang_c)
    cos64 = jnp.concatenate([jnp.cos(ang_r)] * 2 + [jnp.cos(ang_c)] * 2, axis=1)
    sin64 = jnp.concatenate([-jnp.sin(ang_r), jnp.sin(ang_r), -jnp.sin(ang_c), jnp.sin(ang_c)], axis=1)
    cos = jnp.concatenate([cos64, cos64], axis=1)
    sin = jnp.concatenate([sin64, sin64], axis=1)
    qscale = DIFF_DH ** -0.5 * math.log2(math.e)
    return jnp.stack([cos * qscale, cos]), jnp.stack([sin * qscale, sin])


def kernel(x, c, ctx, c_ctx, ada_w, ada_b, ln_g, w_in_ab, ret_decay_logit, s5_lam_re, s5_lam_im,
           s5_log_dt, s5_b_re, s5_b_im, s5_c_re, s5_c_im, s5_d, s5_w_glu, w_out_ab, w_in_c,
           diff_lambda, diff_subln_g, w_out_c, router_w, router_bias, exp_w_gate, exp_w_up,
           exp_w_down):
    nb, l_len, d = x.shape
    ctx_len = ctx.shape[1]
    t_len = ctx_len + l_len
    nt = nb * t_len
    tpb = t_len // ROW_TILE
    ctx_tiles = ctx_len // ROW_TILE
    assert d == D_MODEL and nb < SUBLANES
    assert l_len % (2 * ROW_TILE) == 0 and ctx_len % ROW_TILE == 0 and t_len % ATT_TK == 0

    xt = jnp.concatenate([ctx, x], axis=1).reshape(nt, d)
    c_all = jnp.concatenate([c, c_ctx[None].astype(c.dtype)], axis=0)
    c_pad = jnp.zeros((SUBLANES, d), F32).at[:nb + 1].set(c_all)
    mod = _adaln(c_pad, ada_w, ada_b)

    wr_t = router_w.T
    rbias = router_bias.reshape(N_EXPERTS, 1).astype(F32)

    w0 = w_in_ab[0]
    q_w, k_w, v_w, g_w, u_w = jnp.split(w0, (RET_QK_W, 2 * RET_QK_W, 2 * RET_QK_W + RET_V_W,
                                             2 * RET_QK_W + 2 * RET_V_W), axis=1)
    qk_w = jnp.concatenate([q_w.reshape(d, RET_HEADS, RET_DK), k_w.reshape(d, RET_HEADS, RET_DK)],
                           axis=2).reshape(d, 2 * RET_QK_W)
    w0p = jnp.concatenate([qk_w, v_w, g_w, u_w], axis=1).astype(BF16)
    cos0, sin0 = _ret_rope_tables(l_len, ctx_len)
    rope_tab0 = [0] * RET_HEADS + [None] * ((w0p.shape[1] - S5_CH - 2 * RET_QK_W) // LANES)
    proj0, u5 = _inproj(xt, mod, 0, w0p, cos0, sin0, rope_tab0, RET_DK // 2,
                        nb=nb, t_len=t_len, ctx_len=ctx_len, s5_cols=S5_CH)

    log_gammas = jax.nn.log_sigmoid(ret_decay_logit[0].astype(F32))
    ret = _retention(proj0, log_gammas, nb=nb, t_len=t_len, ctx_len=ctx_len)

    s5_ops = _s5_operators(s5_lam_re[0], s5_lam_im[0], s5_log_dt[0], s5_b_re[0], s5_b_im[0],
                           s5_c_re[0], s5_c_im[0], s5_d[0])
    s5y = _s5(u5, *s5_ops, nb=nb, nctx=ctx_len // S5_CHUNK)

    x1, h2, eidx, gates = _merge0(ret, proj0, s5y, xt, mod, s5_w_glu[0].astype(BF16),
                                  w_out_ab[0].astype(BF16), ln_g[0, 0].reshape(1, d), wr_t, rbias,
                                  nb=nb, t_len=t_len, ctx_len=ctx_len)
    x2 = _moe_layer(x1, h2, eidx, gates, mod, 0, ln_g[0, 1].reshape(1, d),
                    exp_w_gate[0].astype(BF16), exp_w_up[0].astype(BF16), exp_w_down[0].astype(BF16),
                    tiles_per_batch=tpb, ctx_tiles=ctx_tiles, nb=nb)

    cos1, sin1 = _attn_rope_tables(l_len, ctx_len)
    n_heads_cols = D_MODEL // LANES
    rope_tab1 = [0] * n_heads_cols + [1] * n_heads_cols + [None] * n_heads_cols
    qkv, = _inproj(x2, mod, 1, w_in_c[0].astype(BF16), cos1, sin1, rope_tab1, DIFF_DH // 4,
                   nb=nb, t_len=t_len, ctx_len=ctx_len)
    lf = diff_lambda[0].astype(F32)
    lambda_init = 0.8 - 0.6 * math.exp(-0.3 * 1)
    lam = (jnp.exp(jnp.sum(lf[0] * lf[1])) - jnp.exp(jnp.sum(lf[2] * lf[3])) + lambda_init).reshape(1)
    halves = qkv[:, :2 * d].astype(F32).reshape(nb, t_len, 2, 2 * DIFF_HEADS, DIFF_DH)
    norms = jnp.sqrt(jnp.max(jnp.sum(halves * halves, axis=-1), axis=1))
    bounded = jnp.max(norms[:, 0] * norms[:, 1]) <= ATT_SAFE_SHIFT
    attend = lambda online: functools.partial(
        _diff_attention, lambda_init=lambda_init, nb=nb, l_len=l_len, t_len=t_len, ctx_len=ctx_len,
        online=online)
    att = lax.cond(bounded, attend(False), attend(True), qkv, lam, diff_subln_g[0])
    x3, h3, eidx1, gates1 = _merge1(att, x2, mod, 1, w_out_c[0].astype(BF16), ln_g[1, 0].reshape(1, d),
                                    wr_t, rbias, nb=nb, l_len=l_len, t_len=t_len, ctx_len=ctx_len)
    out = _moe_layer(x3, h3, eidx1, gates1, mod, 1, ln_g[1, 1].reshape(1, d),
                     exp_w_gate[1].astype(BF16), exp_w_up[1].astype(BF16), exp_w_down[1].astype(BF16),
                     tiles_per_batch=l_len // ROW_TILE, ctx_tiles=0, nb=nb)
    return out.reshape(nb, l_len, d)
```

```python
import functools
import math

import jax
import jax.numpy as jnp
import numpy as np
from jax import lax
from jax.experimental import pallas as pl
from jax.experimental.pallas import tpu as pltpu

F32 = jnp.float32
BF16 = jnp.bfloat16

D_MODEL = 1024
DEPTH = 2
GRID_W = 64
ALPHA = (2.0 * DEPTH) ** 0.25
LN_EPS = 1e-5
GN_EPS = 1e-6
ROPE_BASE = 10000.0
RET_DK = 64
RET_DV = 128
RET_HEADS = 6
RET_QK_W = RET_HEADS * RET_DK
RET_V_W = RET_HEADS * RET_DV
S5_CH = 256
S5_P = 16
S5_G = 16
S5_N = 64
DIFF_HEADS = 8
DIFF_DH = 64
N_EXPERTS = 16
EXPERTS_PER_GROUP = 4
TOP_K = 2

LANES = 128
SUBLANES = 8
MXU_DIM = 256
ROW_TILE = 256
RET_CHUNK = 256
RET_STATE_UNROLL = 3
RET_OUT_UNROLL = 11
S5_CHUNK = 8
S5_HALVES = 2
MOE_ROWS = 256
DISPATCH_TILE = 512
ROW_WORDS = D_MODEL // 2
ROW_SUB = ROW_WORDS // LANES
ATT_TK = 768
ATT_SAFE_SHIFT = 48.0
VMEM_LIMIT = 48 * 1024 * 1024


def _cparams(sem, flags=None):
    return pltpu.CompilerParams(dimension_semantics=sem, vmem_limit_bytes=VMEM_LIMIT, flags=flags)


def _dot(a, b):
    return jnp.dot(a, b, preferred_element_type=F32)


def _dot_nt(a, b):
    return lax.dot_general(a, b, (((1,), (1,)), ((), ())), preferred_element_type=F32)


def _dot_tn(a, b):
    return lax.dot_general(a, b, (((0,), (0,)), ((), ())), preferred_element_type=F32)


def _split_bf16(x):
    hi = x.astype(BF16)
    lo = (x - hi.astype(F32)).astype(BF16)
    return hi, lo


def _dot3(a, b):
    ah, al = _split_bf16(a)
    bh, bl = _split_bf16(b)
    return _dot(ah, bh) + _dot(ah, bl) + _dot(al, bh)


def _dot3_nt(a, b):
    ah, al = _split_bf16(a)
    bh, bl = _split_bf16(b)
    return _dot_nt(ah, bh) + _dot_nt(ah, bl) + _dot_nt(al, bh)


def _sigmoid(x):
    return 1.0 / (1.0 + jnp.exp(-x))


def _silu(x):
    return x * _sigmoid(x)


def _pack_bf16_pairs(v):
    half = v.shape[1] // 2
    bits = lax.bitcast_convert_type(v.astype(BF16).astype(F32), jnp.uint32)
    return (bits[:, :half] >> 16) | (bits[:, half:] & jnp.uint32(0xFFFF0000))


def _unpack_bf16_pairs(p):
    lo = lax.bitcast_convert_type(p << 16, F32)
    hi = lax.bitcast_convert_type(p & jnp.uint32(0xFFFF0000), F32)
    return jnp.concatenate([lo, hi], axis=1)


def _store_rows(ref, packed):
    for j in range(ROW_SUB):
        ref[:, j, :] = packed[:, j * LANES:(j + 1) * LANES]


def _load_rows(ref):
    return jnp.concatenate([ref[:, j, :] for j in range(ROW_SUB)], axis=1)


def _gelu_tanh(x):
    c = math.sqrt(2.0 / math.pi)
    return 0.5 * x * (1.0 + jnp.tanh(c * (x + 0.044715 * (x * x * x))))


def _adaln_kernel(c_ref, w_ref, b_ref, o_ref):
    c = c_ref[...]
    o_ref[0] = _dot3(_silu(c), w_ref[0]) + b_ref[0]


def _adaln(c_pad, ada_w, ada_b):
    depth, d, n = ada_w.shape
    tn = 1536
    return pl.pallas_call(
        _adaln_kernel,
        grid=(depth, n // tn),
        in_specs=[
            pl.BlockSpec((SUBLANES, d), lambda i, j: (0, 0)),
            pl.BlockSpec((1, d, tn), lambda i, j: (i, 0, j)),
            pl.BlockSpec((1, 1, tn), lambda i, j: (i, 0, j)),
        ],
        out_specs=pl.BlockSpec((1, SUBLANES, tn), lambda i, j: (i, 0, j)),
        out_shape=jax.ShapeDtypeStruct((depth, SUBLANES, n), F32),
        compiler_params=_cparams(("parallel", "parallel")),
        name="adaln",
    )(c_pad, ada_w, ada_b.reshape(depth, 1, n))


def _mod_row(t, tiles_per_batch, ctx_tiles, nb):
    b = lax.div(t, tiles_per_batch)
    w = lax.rem(t, tiles_per_batch)
    return jnp.where(w < ctx_tiles, nb, b)


def _rope_block(a, cos, sin, half):
    lane = lax.broadcasted_iota(jnp.int32, a.shape, 1)
    first = lax.rem(lane, 2 * half) < half
    rot = jnp.where(first, pltpu.roll(a, LANES - half, 1), pltpu.roll(a, half, 1))
    return a * cos + rot * sin


def _chunk_perm(rows, transpose):
    per = rows // S5_CHUNK
    r = lax.broadcasted_iota(jnp.int32, (rows, rows), 0)
    c = lax.broadcasted_iota(jnp.int32, (rows, rows), 1)
    if transpose:
        r, c = c, r
    return (c == S5_CHUNK * lax.rem(r, per) + lax.div(r, per)).astype(BF16)


def _inproj_kernel(x_ref, mod_ref, w_ref, cos_ref, sin_ref, o_ref, *rest, tiles_per_batch,
                   ctx_tiles, nb, rope_tab, rope_half):
    d = x_ref.shape[1]
    n = o_ref.shape[1]
    r = _mod_row(pl.program_id(0), tiles_per_batch, ctx_tiles, nb)
    sh = mod_ref[0, pl.ds(r, 1), 0:d]
    sc = mod_ref[0, pl.ds(r, 1), d:2 * d]
    xm = (x_ref[...] * (1.0 + sc) + sh).astype(BF16)
    for j in range(n // MXU_DIM):
        acc = _dot(xm, w_ref[:, j * MXU_DIM:(j + 1) * MXU_DIM])
        parts = []
        for s in range(MXU_DIM // LANES):
            blk = acc[:, s * LANES:(s + 1) * LANES]
            tab = rope_tab[j * (MXU_DIM // LANES) + s]
            if tab is not None:
                blk = _rope_block(blk, cos_ref[tab], sin_ref[tab], rope_half)
            parts.append(blk)
        o_ref[:, j * MXU_DIM:(j + 1) * MXU_DIM] = jnp.concatenate(parts, axis=1).astype(BF16)
    if rest:
        u_ref, = rest
        rows = x_ref.shape[0]
        per = rows // S5_CHUNK
        u = _dot(xm, w_ref[:, n:n + S5_CH]).astype(BF16)
        up = _dot(_chunk_perm(rows, False), u).astype(BF16)
        for s in range(S5_CHUNK):
            for hf in range(S5_CH // LANES):
                u_ref[hf, :, s * LANES:(s + 1) * LANES] = up[s * per:(s + 1) * per,
                                                             hf * LANES:(hf + 1) * LANES]


def _inproj(x, mod, layer, w, cos, sin, rope_tab, rope_half, *, nb, t_len, ctx_len, s5_cols=0):
    nt, d = x.shape
    n = w.shape[1] - s5_cols
    tpb = t_len // ROW_TILE
    kern = functools.partial(_inproj_kernel, tiles_per_batch=tpb, ctx_tiles=ctx_len // ROW_TILE,
                             nb=nb, rope_tab=tuple(rope_tab), rope_half=rope_half)
    ntab = cos.shape[0]
    out_specs = [pl.BlockSpec((ROW_TILE, n), lambda t: (t, 0))]
    out_shape = [jax.ShapeDtypeStruct((nt, n), BF16)]
    if s5_cols:
        halves, per = s5_cols // LANES, ROW_TILE // S5_CHUNK
        out_specs.append(pl.BlockSpec((halves, per, S5_CHUNK * LANES), lambda t: (0, t, 0)))
        out_shape.append(jax.ShapeDtypeStruct((halves, nt // S5_CHUNK, S5_CHUNK * LANES), BF16))
    return pl.pallas_call(
        kern,
        grid=(nt // ROW_TILE,),
        in_specs=[
            pl.BlockSpec((ROW_TILE, d), lambda t: (t, 0)),
            pl.BlockSpec((1, SUBLANES, mod.shape[2]), lambda t: (layer, 0, 0)),
            pl.BlockSpec(w.shape, lambda t: (0, 0)),
            pl.BlockSpec((ntab, ROW_TILE, LANES), lambda t: (0, lax.rem(t, tpb), 0)),
            pl.BlockSpec((ntab, ROW_TILE, LANES), lambda t: (0, lax.rem(t, tpb), 0)),
        ],
        out_specs=out_specs,
        out_shape=out_shape,
        compiler_params=_cparams(("parallel",)),
        name="inproj%d" % layer,
    )(x, mod, w, cos, sin)


def _retention_kernel(lg_ref, qk_ref, v_ref, o_ref, sf_ref, sb_ref, *, nctx):
    c_len = RET_CHUNK
    t_len = qk_ref.shape[0]
    nc = t_len // c_len
    h = pl.program_id(1)
    lgf = lg_ref[0, h]
    lgb = lg_ref[1, h]
    ii = lax.broadcasted_iota(jnp.int32, (c_len, 1), 0).astype(F32)
    jj = lax.broadcasted_iota(jnp.int32, (1, c_len), 1).astype(F32)
    diff = ii - jj
    decay = jnp.where(diff >= 0.0, jnp.exp(lgf * jnp.maximum(diff, 0.0)),
                      jnp.exp(lgb * jnp.maximum(-diff, 0.0)))
    kdf = jnp.exp(lgf * (c_len - 1.0 - ii))
    kdb = jnp.exp(lgb * ii)
    qdf = jnp.exp(lgf * (ii + 1.0))
    qdb = jnp.exp(lgb * (c_len - ii))
    zrow = jnp.zeros((1, RET_DV), F32)
    gf_chunk = jnp.exp(zrow + lgf * c_len)
    gb_chunk = jnp.exp(zrow + lgb * c_len)

    def load(c):
        rows = pl.ds(pl.multiple_of(c * c_len, c_len), c_len)
        qk = qk_ref[rows, :].astype(F32)
        return qk[:, :RET_DK], qk[:, RET_DK:], v_ref[rows, :]

    def states(j, carry):
        sf, sb = carry
        cb = jnp.where(j < nctx, nctx - 1 - j, nc - 1 - (j - nctx))
        sf_ref[j] = sf
        sb_ref[cb] = sb
        _, kf, vf = load(j)
        _, kb, vb = load(cb)
        return (gf_chunk * sf + _dot_tn((kf * kdf).astype(BF16), vf),
                gb_chunk * sb + _dot_tn((kb * kdb).astype(BF16), vb))

    zero_state = jnp.zeros((RET_DK, RET_DV), F32)
    lax.fori_loop(0, nc, states, (zero_state, zero_state), unroll=RET_STATE_UNROLL)

    def out_chunk(c, carry):
        q, k, v = load(c)
        scores = _dot_nt(q.astype(BF16), k.astype(BF16)) * decay
        o = _dot(scores.astype(BF16), v)
        o = o + _dot((q * qdf).astype(BF16), sf_ref[c].astype(BF16))
        o = o + _dot((q * qdb).astype(BF16), sb_ref[c].astype(BF16))
        mu = jnp.mean(o, axis=-1, keepdims=True)
        oc = o - mu
        var = jnp.mean(oc * oc, axis=-1, keepdims=True)
        rows = pl.ds(pl.multiple_of(c * c_len, c_len), c_len)
        o_ref[rows, :] = (oc * lax.rsqrt(var + GN_EPS)).astype(BF16)
        return carry

    lax.fori_loop(0, nc, out_chunk, 0, unroll=RET_OUT_UNROLL)


def _retention(proj, log_gammas, *, nb, t_len, ctx_len):
    nt = proj.shape[0]
    nc = t_len // RET_CHUNK
    kern = functools.partial(_retention_kernel, nctx=ctx_len // RET_CHUNK)
    vcol0 = RET_HEADS
    return pl.pallas_call(
        kern,
        grid_spec=pltpu.PrefetchScalarGridSpec(
            num_scalar_prefetch=1,
            grid=(nb, RET_HEADS),
            in_specs=[
                pl.BlockSpec((t_len, LANES), lambda b, h, lg: (b, h)),
                pl.BlockSpec((t_len, LANES), lambda b, h, lg: (b, vcol0 + h)),
            ],
            out_specs=pl.BlockSpec((t_len, LANES), lambda b, h, lg: (b, h)),
            scratch_shapes=[pltpu.VMEM((nc, RET_DK, RET_DV), F32),
                            pltpu.VMEM((nc, RET_DK, RET_DV), F32)],
        ),
        out_shape=jax.ShapeDtypeStruct((nt, RET_V_W), BF16),
        compiler_params=_cparams(("parallel", "parallel")),
        name="retention",
    )(log_gammas, proj, proj)


def _s5_operators(lam_re, lam_im, log_dt, b_re, b_im, c_re, c_im, d_skip):
    tc = S5_CHUNK
    hp = lax.Precision.HIGHEST
    ks = jnp.arange(tc + 1, dtype=F32)
    pw, bbar, cm = [], [], []
    for direction in range(2):
        dt = jnp.exp(log_dt[direction].astype(F32))[:, None]
        lam = lax.complex(lam_re[direction].astype(F32), lam_im[direction].astype(F32))
        z = lam * dt
        p = jnp.exp(z[None] * ks[:, None, None])
        lam_bar = p[1]
        bb = ((lam_bar - 1.0) / lam)[..., None] * lax.complex(
            b_re[direction].astype(F32), b_im[direction].astype(F32))
        pw.append(p)
        bbar.append(bb)
        cm.append(lax.complex(c_re[direction].astype(F32), c_im[direction].astype(F32)))

    def lag_kernel(p, bb, c):
        return jnp.einsum('gpn,kgn,gnq->kgpq', c, p[:tc], bb, precision=hp).real

    kf = lag_kernel(pw[0], bbar[0], cm[0])
    kb = lag_kernel(pw[1], bbar[1], cm[1])
    k0 = kf[0] + kb[0] + jnp.eye(S5_P, dtype=F32)[None] * d_skip.astype(F32)[:, :, None]
    kcat = jnp.concatenate([kb[1:][::-1], k0[None], kf[1:]], axis=0)
    s_idx = jnp.arange(tc)[:, None]
    t_idx = jnp.arange(tc)[None, :]
    m5 = kcat[t_idx - s_idx + tc - 1]
    hg = S5_G // S5_HALVES
    eye = jnp.eye(hg, dtype=F32)
    split = lambda z, axis: z.reshape(z.shape[:axis] + (S5_HALVES, hg) + z.shape[axis + 1:])
    wide = tc * hg * S5_P
    intra = jnp.einsum('sthgpq,gk->hsgqtkp', split(m5, 2), eye).reshape(S5_HALVES, wide, wide)

    ef = pw[0][:tc][::-1][:, :, :, None] * bbar[0][None]
    eb = pw[1][:tc][:, :, :, None] * bbar[1][None]
    to_in = lambda e: jnp.einsum('shgnq,gk->hsgqkn', split(e, 1), eye).reshape(
        S5_HALVES, wide, hg * S5_N)
    w_in = jnp.concatenate([to_in(ef.real), to_in(ef.imag), to_in(eb.real), to_in(eb.imag)], axis=2)
    of = cm[0][None] * pw[0][1:][:, :, None, :]
    ob = cm[1][None] * pw[1][1:][::-1][:, :, None, :]
    to_out = lambda o: jnp.einsum('thgpn,gk->hkntgp', split(o, 1), eye).reshape(
        S5_HALVES, hg * S5_N, wide)
    w_out = jnp.concatenate([to_out(of.real), -to_out(of.imag), to_out(ob.real), -to_out(ob.imag)],
                            axis=1)
    a = jnp.stack([pw[0][tc].real, pw[0][tc].imag, pw[1][tc].real, pw[1][tc].imag], axis=0)
    a = a.reshape(4, S5_HALVES, hg * S5_N).transpose(1, 0, 2)
    return intra.astype(BF16), w_in.astype(BF16), w_out.astype(BF16), a


def _s5_kernel(x_ref, wi_ref, win_ref, wout_ref, a_ref, y_ref, st_ref, *, nctx):
    x = x_ref[0]
    nc = x.shape[0]
    w = a_ref.shape[2]
    st_ref[...] = _dot(x, win_ref[0])
    afr, afi, abr, abi = (a_ref[0, i:i + 1, :] for i in range(4))

    def step(j, carry):
        fr, fi, br, bi = carry
        rf = pl.ds(j, 1)
        rb = pl.ds(jnp.where(j < nctx, nctx - 1 - j, nc - 1 - (j - nctx)), 1)
        efr, efi = st_ref[rf, 0:w], st_ref[rf, w:2 * w]
        ebr, ebi = st_ref[rb, 2 * w:3 * w], st_ref[rb, 3 * w:4 * w]
        st_ref[rf, 0:w] = fr
        st_ref[rf, w:2 * w] = fi
        st_ref[rb, 2 * w:3 * w] = br
        st_ref[rb, 3 * w:4 * w] = bi
        return (afr * fr - afi * fi + efr, afr * fi + afi * fr + efi,
                abr * br - abi * bi + ebr, abr * bi + abi * br + ebi)

    z = jnp.zeros((1, w), F32)
    lax.fori_loop(0, nc, step, (z, z, z, z), unroll=4)
    y_ref[0] = (_dot(x, wi_ref[0]) + _dot(st_ref[...].astype(BF16), wout_ref[0])).astype(BF16)


def _s5(xc, intra, w_in, w_out, a, *, nb, nctx):
    halves, rows, wide = xc.shape
    nc = rows // nb
    kern = functools.partial(_s5_kernel, nctx=nctx)
    per_half = lambda arr: pl.BlockSpec((1,) + arr.shape[1:], lambda hf, b: (hf, 0, 0))
    return pl.pallas_call(
        kern,
        grid=(halves, nb),
        in_specs=[
            pl.BlockSpec((1, nc, wide), lambda hf, b: (hf, b, 0)),
            per_half(intra), per_half(w_in), per_half(w_out), per_half(a),
        ],
        out_specs=pl.BlockSpec((1, nc, wide), lambda hf, b: (hf, b, 0)),
        out_shape=jax.ShapeDtypeStruct(xc.shape, BF16),
        scratch_shapes=[pltpu.VMEM((nc, w_in.shape[2]), F32)],
        compiler_params=_cparams(("parallel", "parallel")),
        name="s5",
    )(xc, intra, w_in, w_out, a)


def _route(logits_t, bias):
    scores = _sigmoid(logits_t)
    biased = scores + bias
    s_rows = [scores[e:e + 1, :] for e in range(N_EXPERTS)]
    b_rows = [biased[e:e + 1, :] for e in range(N_EXPERTS)]
    n_groups = N_EXPERTS // EXPERTS_PER_GROUP
    best = None
    sel = None
    for g in range(n_groups):
        a, b, c, d = b_rows[4 * g:4 * g + 4]
        hi1, lo1 = jnp.maximum(a, b), jnp.minimum(a, b)
        hi2, lo2 = jnp.maximum(c, d), jnp.minimum(c, d)
        top1 = jnp.maximum(hi1, hi2)
        top2 = jnp.maximum(jnp.minimum(hi1, hi2), jnp.maximum(lo1, lo2))
        gs = top1 + top2
        if g == 0:
            best, sel = gs, jnp.zeros(gs.shape, jnp.int32)
        else:
            better = gs > best
            sel = jnp.where(better, g, sel)
            best = jnp.where(better, gs, best)
    neg = jnp.full(best.shape, -jnp.inf, F32)
    masked = [jnp.where(sel == (e // EXPERTS_PER_GROUP), b_rows[e], neg) for e in range(N_EXPERTS)]
    v1, i1, g1 = masked[0], jnp.zeros(best.shape, jnp.int32), s_rows[0]
    for e in range(1, N_EXPERTS):
        better = masked[e] > v1
        v1 = jnp.where(better, masked[e], v1)
        i1 = jnp.where(better, e, i1)
        g1 = jnp.where(better, s_rows[e], g1)
    v2, i2, g2 = neg, jnp.zeros(best.shape, jnp.int32), jnp.zeros(best.shape, F32)
    for e in range(N_EXPERTS):
        cand = jnp.where(i1 == e, neg, masked[e])
        better = cand > v2
        v2 = jnp.where(better, cand, v2)
        i2 = jnp.where(better, e, i2)
        g2 = jnp.where(better, s_rows[e], g2)
    tot = g1 + g2
    return jnp.concatenate([i1, i2], axis=0), jnp.concatenate([g1 / tot, g2 / tot], axis=0)


def _tail(x, o, mod_ref, r, lng, wr, rb, x1_ref, h2_ref, ei_ref, gt_ref):
    d = x.shape[1]
    g1 = mod_ref[0, pl.ds(r, 1), 2 * d:3 * d]
    sh2 = mod_ref[0, pl.ds(r, 1), 3 * d:4 * d]
    sc2 = mod_ref[0, pl.ds(r, 1), 4 * d:5 * d]
    y = ALPHA * x + g1 * o
    mu = jnp.mean(y, axis=-1, keepdims=True)
    yc = y - mu
    var = jnp.mean(yc * yc, axis=-1, keepdims=True)
    x1 = yc * lax.rsqrt(var + LN_EPS) * lng
    h2 = x1 * (1.0 + sc2) + sh2
    x1_ref[...] = x1
    _store_rows(h2_ref, _pack_bf16_pairs(h2))
    ei, gt = _route(_dot3_nt(wr, h2), rb)
    ei_ref[...] = ei
    gt_ref[...] = gt


def _merge0_kernel(r_ref, g_ref, s_ref, x_ref, mod_ref, wglu_ref, wout_ref, lng_ref, wr_ref,
                   rb_ref, x1_ref, h2_ref, ei_ref, gt_ref, *, tiles_per_batch, ctx_tiles, nb):
    r = _mod_row(pl.program_id(0), tiles_per_batch, ctx_tiles, nb)
    ret = r_ref[...].astype(F32) * _silu(g_ref[...].astype(F32))
    rows = x_ref.shape[0]
    sp = jnp.concatenate(
        [jnp.concatenate([s_ref[hf, :, s * LANES:(s + 1) * LANES] for hf in range(S5_HALVES)], axis=1)
         for s in range(S5_CHUNK)], axis=0)
    s5 = _dot(_chunk_perm(rows, True), sp)
    z = _dot(_gelu_tanh(s5).astype(BF16), wglu_ref[...])
    zz = z[:, :S5_CH] * _sigmoid(z[:, S5_CH:])
    o = _dot(ret.astype(BF16), wout_ref[0:RET_V_W, :]) + _dot(zz.astype(BF16), wout_ref[RET_V_W:, :])
    _tail(x_ref[...], o, mod_ref, r, lng_ref[...], wr_ref[...], rb_ref[...],
          x1_ref, h2_ref, ei_ref, gt_ref)


def _merge1_kernel(a_ref, x_ref, mod_ref, wout_ref, lng_ref, wr_ref, rb_ref,
                   x1_ref, h2_ref, ei_ref, gt_ref, *, tiles_per_batch):
    r = lax.div(pl.program_id(0), tiles_per_batch)
    o = _dot(a_ref[...], wout_ref[...])
    _tail(x_ref[...], o, mod_ref, r, lng_ref[...], wr_ref[...], rb_ref[...],
          x1_ref, h2_ref, ei_ref, gt_ref)


def _tail_outs(n_rows, d):
    shapes = (jax.ShapeDtypeStruct((n_rows, d), F32),
              jax.ShapeDtypeStruct((n_rows, ROW_SUB, LANES), jnp.uint32),
              jax.ShapeDtypeStruct((TOP_K, n_rows), jnp.int32),
              jax.ShapeDtypeStruct((TOP_K, n_rows), F32))
    specs = (pl.BlockSpec((ROW_TILE, d), lambda t: (t, 0)),
             pl.BlockSpec((ROW_TILE, ROW_SUB, LANES), lambda t: (t, 0, 0)),
             pl.BlockSpec((TOP_K, ROW_TILE), lambda t: (0, t)),
             pl.BlockSpec((TOP_K, ROW_TILE), lambda t: (0, t)))
    return shapes, specs


def _merge0(ret, proj, s5y, x, mod, w_glu, w_out, lng, wr_t, rbias, *, nb, t_len, ctx_len):
    nt, d = x.shape
    tpb = t_len // ROW_TILE
    kern = functools.partial(_merge0_kernel, tiles_per_batch=tpb, ctx_tiles=ctx_len // ROW_TILE, nb=nb)
    shapes, specs = _tail_outs(nt, d)
    gcol = (2 * RET_QK_W + RET_V_W) // RET_V_W
    full = lambda a: pl.BlockSpec(a.shape, lambda t: (0,) * a.ndim)
    return pl.pallas_call(
        kern,
        grid=(nt // ROW_TILE,),
        in_specs=[
            pl.BlockSpec((ROW_TILE, RET_V_W), lambda t: (t, 0)),
            pl.BlockSpec((ROW_TILE, RET_V_W), lambda t: (t, gcol)),
            pl.BlockSpec((S5_HALVES, ROW_TILE // S5_CHUNK, S5_CHUNK * LANES), lambda t: (0, t, 0)),
            pl.BlockSpec((ROW_TILE, d), lambda t: (t, 0)),
            pl.BlockSpec((1, SUBLANES, mod.shape[2]), lambda t: (0, 0, 0)),
            full(w_glu), full(w_out), full(lng), full(wr_t), full(rbias),
        ],
        out_specs=specs,
        out_shape=shapes,
        compiler_params=_cparams(("parallel",)),
        name="merge0",
    )(ret, proj, s5y, x, mod, w_glu, w_out, lng, wr_t, rbias)


def _merge1(att, x, mod, layer, w_out, lng, wr_t, rbias, *, nb, l_len, t_len, ctx_len):
    n_lat, d = att.shape
    tpb = l_len // ROW_TILE
    tpb_t = t_len // ROW_TILE
    ctx_tiles = ctx_len // ROW_TILE
    kern = functools.partial(_merge1_kernel, tiles_per_batch=tpb)
    shapes, specs = _tail_outs(n_lat, d)
    full = lambda a: pl.BlockSpec(a.shape, lambda t: (0,) * a.ndim)
    xrow = lambda t: (lax.div(t, tpb) * tpb_t + ctx_tiles + lax.rem(t, tpb), 0)
    return pl.pallas_call(
        kern,
        grid=(n_lat // ROW_TILE,),
        in_specs=[
            pl.BlockSpec((ROW_TILE, d), lambda t: (t, 0)),
            pl.BlockSpec((ROW_TILE, d), xrow),
            pl.BlockSpec((1, SUBLANES, mod.shape[2]), lambda t: (layer, 0, 0)),
            full(w_out), full(lng), full(wr_t), full(rbias),
        ],
        out_specs=specs,
        out_shape=shapes,
        compiler_params=_cparams(("parallel",)),
        name="merge1",
    )(att, x, mod, w_out, lng, wr_t, rbias)


def _moe_plan(eidx):
    k, n = eidx.shape
    a = k * n
    e_flat = eidx.reshape(a)
    seg = TOP_K * ROW_TILE
    onehot = (e_flat[:, None] == jnp.arange(N_EXPERTS, dtype=jnp.int32)[None, :]).astype(F32)
    onehot = onehot.reshape(a // seg, seg, N_EXPERTS)
    within = jnp.einsum('ij,tjk->tik', jnp.tril(jnp.ones((seg, seg), F32)), onehot)
    seg_total = within[:, -1, :]
    seg_end = jnp.cumsum(seg_total, axis=0)
    counts = seg_end[-1].astype(jnp.int32)
    csum = within + (seg_end - seg_total)[:, None, :]
    padded = (counts + MOE_ROWS - 1) // MOE_ROWS * MOE_ROWS
    pad_end = jnp.cumsum(padded)
    pad_start = pad_end - padded
    dest = jnp.sum(onehot * (csum - 1.0 + pad_start.astype(F32)[None, None, :]), axis=-1)
    dest = dest.reshape(a).astype(jnp.int32)
    n_blocks = -(-(a + N_EXPERTS * (MOE_ROWS - 1)) // MOE_ROWS)
    block_expert = jnp.minimum(
        jnp.searchsorted(pad_end, jnp.arange(n_blocks, dtype=jnp.int32) * MOE_ROWS, side='right'),
        N_EXPERTS - 1).astype(jnp.int32)
    return dest.reshape(k, n), block_expert, n_blocks


def _tile_rows_of(dest, tile):
    k, n = dest.shape
    return dest.reshape(k, n // tile, tile).transpose(1, 0, 2).reshape(n // tile, 1, k * tile)


def _dispatch_kernel(dest_ref, h_ref, xs_in_hbm, xs_hbm, sem):
    del xs_in_hbm
    rows = h_ref.shape[0]

    def start(r, c):
        for choice in range(TOP_K):
            pltpu.make_async_copy(h_ref.at[r], xs_hbm.at[dest_ref[0, 0, choice * rows + r]], sem).start()
        return c

    lax.fori_loop(0, rows, start, 0, unroll=8)
    for _ in range(TOP_K):
        pltpu.make_async_copy(h_ref, xs_hbm.at[pl.ds(0, rows)], sem).wait()


def _dispatch(h, dest, n_rows):
    n = h.shape[0]
    tile = DISPATCH_TILE if n % DISPATCH_TILE == 0 else ROW_TILE
    return pl.pallas_call(
        _dispatch_kernel,
        grid=(n // tile,),
        in_specs=[
            pl.BlockSpec((1, 1, TOP_K * tile), lambda t: (t, 0, 0), memory_space=pltpu.SMEM),
            pl.BlockSpec((tile, ROW_SUB, LANES), lambda t: (t, 0, 0)),
            pl.BlockSpec(memory_space=pl.ANY),
        ],
        out_specs=pl.BlockSpec(memory_space=pl.ANY),
        out_shape=jax.ShapeDtypeStruct((n_rows, ROW_SUB, LANES), jnp.uint32),
        scratch_shapes=[pltpu.SemaphoreType.DMA],
        input_output_aliases={2: 0},
        compiler_params=_cparams(("arbitrary",)),
        name="moe_dispatch",
    )(_tile_rows_of(dest, tile), h, jnp.zeros((n_rows, ROW_SUB, LANES), jnp.uint32))


def _experts_kernel(be_ref, x_ref, wg_ref, wu_ref, wd_ref, o_ref):
    x = _unpack_bf16_pairs(_load_rows(x_ref)).astype(BF16)
    hg = _dot(x, wg_ref[0])
    hu = _dot(x, wu_ref[0])
    _store_rows(o_ref, _pack_bf16_pairs(_dot((_silu(hg) * hu).astype(BF16), wd_ref[0])))


def _experts(xs, block_expert, wg, wu, wd):
    n_blocks = block_expert.shape[0]
    d, dff = wg.shape[1], wg.shape[2]
    rows_spec = pl.BlockSpec((MOE_ROWS, ROW_SUB, LANES), lambda i, be: (i, 0, 0))
    return pl.pallas_call(
        _experts_kernel,
        grid_spec=pltpu.PrefetchScalarGridSpec(
            num_scalar_prefetch=1,
            grid=(n_blocks,),
            in_specs=[
                rows_spec,
                pl.BlockSpec((1, d, dff), lambda i, be: (be[i], 0, 0)),
                pl.BlockSpec((1, d, dff), lambda i, be: (be[i], 0, 0)),
                pl.BlockSpec((1, dff, d), lambda i, be: (be[i], 0, 0)),
            ],
            out_specs=rows_spec,
        ),
        out_shape=jax.ShapeDtypeStruct(xs.shape, jnp.uint32),
        compiler_params=_cparams(("parallel",)),
        name="moe_experts",
    )(block_expert, xs, wg, wu, wd)


def _combine_kernel(dcur_ref, dnxt_ref, x_ref, gt_ref, mod_ref, lng_ref, y_hbm, o_ref, ybuf, sem, *,
                    tiles_per_batch, ctx_tiles, nb):
    t = pl.program_id(0)
    last = pl.num_programs(0) - 1
    rows = x_ref.shape[0]
    n = TOP_K * rows
    d = x_ref.shape[1]
    slot = lax.rem(t, 2)

    def start_all(idx_ref, s):
        def start(j, c):
            pltpu.make_async_copy(y_hbm.at[idx_ref[0, 0, j]], ybuf.at[s, j], sem.at[s]).start()
            return c

        lax.fori_loop(0, n, start, 0, unroll=8)

    @pl.when(t == 0)
    def _():
        start_all(dcur_ref, 0)

    @pl.when(t < last)
    def _():
        start_all(dnxt_ref, 1 - slot)

    pltpu.make_async_copy(y_hbm.at[pl.ds(0, n)], ybuf.at[slot], sem.at[slot]).wait()
    r = _mod_row(t, tiles_per_batch, ctx_tiles, nb)
    gt = gt_ref[...]
    yb = ybuf.at[slot]
    y = (_unpack_bf16_pairs(_load_rows(yb.at[pl.ds(0, rows)])) * gt[:, 0:1]
         + _unpack_bf16_pairs(_load_rows(yb.at[pl.ds(rows, rows)])) * gt[:, 1:2])
    g2 = mod_ref[0, pl.ds(r, 1), 5 * d:6 * d]
    z = ALPHA * x_ref[...] + g2 * y
    mu = jnp.mean(z, axis=-1, keepdims=True)
    zc = z - mu
    var = jnp.mean(zc * zc, axis=-1, keepdims=True)
    o_ref[...] = zc * lax.rsqrt(var + LN_EPS) * lng_ref[...]


def _combine(x1, dest, gates, mod, layer, lng, ys, *, tiles_per_batch, ctx_tiles, nb):
    n, d = x1.shape
    nt = n // ROW_TILE
    dest_t = _tile_rows_of(dest, ROW_TILE)
    kern = functools.partial(_combine_kernel, tiles_per_batch=tiles_per_batch, ctx_tiles=ctx_tiles, nb=nb)
    idx_spec = lambda f: pl.BlockSpec((1, 1, TOP_K * ROW_TILE), f, memory_space=pltpu.SMEM)
    return pl.pallas_call(
        kern,
        grid=(nt,),
        in_specs=[
            idx_spec(lambda t: (t, 0, 0)),
            idx_spec(lambda t: (jnp.minimum(t + 1, nt - 1), 0, 0)),
            pl.BlockSpec((ROW_TILE, d), lambda t: (t, 0)),
            pl.BlockSpec((ROW_TILE, TOP_K), lambda t: (t, 0)),
            pl.BlockSpec((1, SUBLANES, mod.shape[2]), lambda t: (layer, 0, 0)),
            pl.BlockSpec((1, d), lambda t: (0, 0)),
            pl.BlockSpec(memory_space=pl.ANY),
        ],
        out_specs=pl.BlockSpec((ROW_TILE, d), lambda t: (t, 0)),
        out_shape=jax.ShapeDtypeStruct((n, d), F32),
        scratch_shapes=[pltpu.VMEM((2, TOP_K * ROW_TILE, ROW_SUB, LANES), jnp.uint32),
                        pltpu.SemaphoreType.DMA((2,))],
        compiler_params=_cparams(("arbitrary",)),
        name="moe_combine%d" % layer,
    )(dest_t, dest_t, x1, gates.T, mod, lng, ys)


def _moe_layer(x1, h2, eidx, gates, mod, layer, lng, wg, wu, wd, *, tiles_per_batch, ctx_tiles, nb):
    dest, block_expert, n_blocks = _moe_plan(eidx)
    xs = _dispatch(h2, dest, n_blocks * MOE_ROWS)
    ys = _experts(xs, block_expert, wg, wu, wd)
    return _combine(x1, dest, gates, mod, layer, lng, ys,
                    tiles_per_batch=tiles_per_batch, ctx_tiles=ctx_tiles, nb=nb)


def _half_norms(x):
    lane = lax.broadcasted_iota(jnp.int32, x.shape, 1)
    sq = x * x
    lo = jnp.sum(jnp.where(lane < DIFF_DH, sq, 0.0), axis=-1, keepdims=True)
    hi = jnp.sum(jnp.where(lane >= DIFF_DH, sq, 0.0), axis=-1, keepdims=True)
    return jnp.sqrt(lo), jnp.sqrt(hi)


def _attn_kernel(lam_ref, q_ref, k_ref, v_ref, g_ref, o_ref, vext, s_buf0, s_buf1,
                 p_buf0, p_buf1, corr_buf0, corr_buf1, m_buf, shift_buf, acc, *, out_scale, ctx_len):
    t_len = k_ref.shape[0]
    nk = t_len // ATT_TK
    dv = v_ref.shape[1]
    tq = acc.shape[1]
    nq = o_ref.shape[0] // tq
    n_tiles = nq * nk

    vext[:, 0:dv] = v_ref[...]
    vext[:, dv:2 * dv] = jnp.ones((t_len, dv), BF16)
    lam = lam_ref[0]
    gain = g_ref[...] * out_scale

    s_bufs, p_bufs, corr_bufs = (s_buf0, s_buf1), (p_buf0, p_buf1), (corr_buf0, corr_buf1)

    def key_rows(kj):
        return pl.ds(pl.multiple_of(kj * ATT_TK, ATT_TK), ATT_TK)

    def query_rows(qi):
        return pl.ds(pl.multiple_of(ctx_len + qi * tq, ROW_TILE), tq)

    def column_max(norms, carry):
        return tuple(jnp.maximum(c, jnp.max(n, axis=0, keepdims=True)) for n, c in zip(norms, carry))

    def key_norms(kj, carry):
        return column_max(_half_norms(k_ref[key_rows(kj), :].astype(F32)), carry)

    zero11 = jnp.zeros((1, 1), F32)
    kmax = lax.fori_loop(0, nk, key_norms, (zero11, zero11))

    def query_shifts(qi, carry):
        shifts = tuple(n * k for n, k in zip(_half_norms(q_ref[query_rows(qi), :].astype(F32)), kmax))
        shift_buf[qi, :, 0:1] = shifts[0]
        shift_buf[qi, :, 1:2] = shifts[1]
        return column_max(shifts, carry)

    worst = lax.fori_loop(0, nq, query_shifts, (zero11, zero11))
    bounded = jnp.max(jnp.maximum(worst[0], worst[1])) <= ATT_SAFE_SHIFT

    def advance(tile):
        qi, kj = tile
        wrap = kj + 1 == nk
        return jnp.where(wrap, qi + 1, qi), jnp.where(wrap, 0, kj + 1)

    def scores(tile, slot):
        qi, kj = tile
        q = q_ref[query_rows(qi), :]
        lane = lax.broadcasted_iota(jnp.int32, q.shape, 1)
        zero = jnp.zeros(q.shape, q.dtype)
        k = k_ref[key_rows(kj), :]
        s_bufs[slot][0] = _dot_nt(jnp.where(lane < DIFF_DH, q, zero), k)
        s_bufs[slot][1] = _dot_nt(jnp.where(lane >= DIFF_DH, q, zero), k)

    def numerators(tile, slot, online):
        qi, kj = tile
        for w in range(2):
            s = s_bufs[slot][w]
            if online:
                m_old = jnp.where(kj == 0, -jnp.inf, m_buf[w])
                m_new = jnp.maximum(m_old, jnp.max(s, axis=-1, keepdims=True))
                corr_bufs[slot][w] = jnp.exp2(m_old - m_new)
                m_buf[w] = m_new
            else:
                m_new = shift_buf[qi, :, w:w + 1]
            p_bufs[slot][w] = jnp.exp2(s - m_new).astype(BF16)

    def values(tile, slot, online):
        qi, kj = tile
        ve = vext[key_rows(kj), :]
        a = []
        for w in range(2):
            keep = corr_bufs[slot][w] if online else jnp.where(kj == 0, 0.0, 1.0)
            a.append(keep * acc[w] + _dot(p_bufs[slot][w], ve))
            acc[w] = a[w]
        o = a[0][:, 0:dv] / a[0][:, dv:2 * dv] - lam * (a[1][:, 0:dv] / a[1][:, dv:2 * dv])
        o = o * lax.rsqrt(jnp.mean(o * o, axis=-1, keepdims=True) + GN_EPS)
        o_ref[pl.ds(pl.multiple_of(qi * tq, tq), tq), :] = (o * gain).astype(BF16)

    def pipeline(online):
        def step(tiles, slot):
            a, b, c = tiles
            values(c, slot, online)
            scores(a, slot)
            numerators(b, 1 - slot, online)
            return advance(a), a, b

        acc[...] = jnp.zeros(acc.shape, F32)
        t0 = (jnp.int32(0), jnp.int32(0))
        t1 = advance(t0)
        scores(t0, 0)
        scores(t1, 1)
        numerators(t0, 0, online)

        def pair(_, tiles):
            return step(step(tiles, 0), 1)

        _, last, prev = lax.fori_loop(0, (n_tiles - 2) // 2, pair, (advance(t1), t1, t0))
        numerators(last, (n_tiles - 1) % 2, online)
        values(prev, n_tiles % 2, online)
        values(last, (n_tiles - 1) % 2, online)

    pl.when(bounded)(functools.partial(pipeline, False))
    pl.when(jnp.logical_not(bounded))(functools.partial(pipeline, True))


def _diff_attention(qkv, lam, subln_g, lambda_init, *, nb, l_len, t_len, ctx_len):
    d = D_MODEL
    tq = 2 * ROW_TILE
    dv = 2 * DIFF_DH
    nq = l_len // tq
    assert (nq * (t_len // ATT_TK)) % 2 == 0
    kern = functools.partial(_attn_kernel, out_scale=1.0 - lambda_init, ctx_len=ctx_len)
    return pl.pallas_call(
        kern,
        grid_spec=pltpu.PrefetchScalarGridSpec(
            num_scalar_prefetch=1,
            grid=(nb, DIFF_HEADS),
            in_specs=[
                pl.BlockSpec((t_len, LANES), lambda b, h, lam: (b, h)),
                pl.BlockSpec((t_len, LANES), lambda b, h, lam: (b, DIFF_HEADS + h)),
                pl.BlockSpec((t_len, LANES), lambda b, h, lam: (b, 2 * DIFF_HEADS + h)),
                pl.BlockSpec((1, LANES), lambda b, h, lam: (0, 0)),
            ],
            out_specs=pl.BlockSpec((l_len, LANES), lambda b, h, lam: (b, h)),
            scratch_shapes=[
                pltpu.VMEM((t_len, 2 * dv), BF16),
                pltpu.VMEM((2, tq, ATT_TK), F32), pltpu.VMEM((2, tq, ATT_TK), F32),
                pltpu.VMEM((2, tq, ATT_TK), BF16), pltpu.VMEM((2, tq, ATT_TK), BF16),
                pltpu.VMEM((2, tq, 1), F32), pltpu.VMEM((2, tq, 1), F32),
                pltpu.VMEM((2, tq, 1), F32), pltpu.VMEM((nq, tq, 2), F32),
                pltpu.VMEM((2, tq, 2 * dv), F32),
            ],
        ),
        out_shape=jax.ShapeDtypeStruct((nb * l_len, d), BF16),
        compiler_params=_cparams(("parallel", "parallel")),
        name="diff_attention",
    )(lam, qkv, qkv, qkv, subln_g.reshape(1, LANES).astype(F32))


def _ret_rope_tables(l_len, ctx_len):
    half = RET_DK // 2
    inv = ROPE_BASE ** (-jnp.arange(0, RET_DK, 2, dtype=F32) / RET_DK)
    ang = jnp.arange(l_len, dtype=F32)[:, None] * inv[None, :]
    ang = jnp.concatenate([jnp.zeros((ctx_len, half), F32), ang], axis=0)
    cos64 = jnp.concatenate([jnp.cos(ang), jnp.cos(ang)], axis=1)
    sin64 = jnp.concatenate([-jnp.sin(ang), jnp.sin(ang)], axis=1)
    kscale = RET_DK ** -0.5
    cos = jnp.concatenate([cos64, cos64 * kscale], axis=1)
    sin = jnp.concatenate([sin64, sin64 * kscale], axis=1)
    return cos[None], sin[None]


def _attn_rope_tables(l_len, ctx_len):
    quarter = DIFF_DH // 4
    inv = ROPE_BASE ** (-jnp.arange(0, DIFF_DH // 2, 2, dtype=F32) / (DIFF_DH // 2))
    pos = jnp.arange(l_len)
    ang_r = (pos // GRID_W).astype(F32)[:, None] * inv[None, :]
    ang_c = (pos % GRID_W).astype(F32)[:, None] * inv[None, :]
    pad = lambda a: jnp.concatenate([jnp.zeros((ctx_len, quarter), F32), a], axis=0)
    ang_r, ang_c = pad(ang_r), pad(ang_c)
    cos64 = jnp.concatenate([jnp.cos(ang_r)] * 2 + [jnp.cos(ang_c)] * 2, axis=1)
    sin64 = jnp.concatenate([-jnp.sin(ang_r), jnp.sin(ang_r), -jnp.sin(ang_c), jnp.sin(ang_c)], axis=1)
    cos = jnp.concatenate([cos64, cos64], axis=1)
    sin = jnp.concatenate([sin64, sin64], axis=1)
    qscale = DIFF_DH ** -0.5 * math.log2(math.e)
    return jnp.stack([cos * qscale, cos]), jnp.stack([sin * qscale, sin])


def kernel(x, c, ctx, c_ctx, ada_w, ada_b, ln_g, w_in_ab, ret_decay_logit, s5_lam_re, s5_lam_im,
           s5_log_dt, s5_b_re, s5_b_im, s5_c_re, s5_c_im, s5_d, s5_w_glu, w_out_ab, w_in_c,
           diff_lambda, diff_subln_g, w_out_c, router_w, router_bias, exp_w_gate, exp_w_up,
           exp_w_down):
    nb, l_len, d = x.shape
    ctx_len = ctx.shape[1]
    t_len = ctx_len + l_len
    nt = nb * t_len
    tpb = t_len // ROW_TILE
    ctx_tiles = ctx_len // ROW_TILE
    assert d == D_MODEL and nb < SUBLANES
    assert l_len % (2 * ROW_TILE) == 0 and ctx_len % ROW_TILE == 0 and t_len % ATT_TK == 0

    xt = jnp.concatenate([ctx, x], axis=1).reshape(nt, d)
    c_all = jnp.concatenate([c, c_ctx[None].astype(c.dtype)], axis=0)
    c_pad = jnp.zeros((SUBLANES, d), F32).at[:nb + 1].set(c_all)
    mod = _adaln(c_pad, ada_w, ada_b)

    wr_t = router_w.T
    rbias = router_bias.reshape(N_EXPERTS, 1).astype(F32)

    w0 = w_in_ab[0]
    q_w, k_w, v_w, g_w, u_w = jnp.split(w0, (RET_QK_W, 2 * RET_QK_W, 2 * RET_QK_W + RET_V_W,
                                             2 * RET_QK_W + 2 * RET_V_W), axis=1)
    qk_w = jnp.concatenate([q_w.reshape(d, RET_HEADS, RET_DK), k_w.reshape(d, RET_HEADS, RET_DK)],
                           axis=2).reshape(d, 2 * RET_QK_W)
    w0p = jnp.concatenate([qk_w, v_w, g_w, u_w], axis=1).astype(BF16)
    cos0, sin0 = _ret_rope_tables(l_len, ctx_len)
    rope_tab0 = [0] * RET_HEADS + [None] * ((w0p.shape[1] - S5_CH - 2 * RET_QK_W) // LANES)
    proj0, u5 = _inproj(xt, mod, 0, w0p, cos0, sin0, rope_tab0, RET_DK // 2,
                        nb=nb, t_len=t_len, ctx_len=ctx_len, s5_cols=S5_CH)

    log_gammas = jax.nn.log_sigmoid(ret_decay_logit[0].astype(F32))
    ret = _retention(proj0, log_gammas, nb=nb, t_len=t_len, ctx_len=ctx_len)

    s5_ops = _s5_operators(s5_lam_re[0], s5_lam_im[0], s5_log_dt[0], s5_b_re[0], s5_b_im[0],
                           s5_c_re[0], s5_c_im[0], s5_d[0])
    s5y = _s5(u5, *s5_ops, nb=nb, nctx=ctx_len // S5_CHUNK)

    x1, h2, eidx, gates = _merge0(ret, proj0, s5y, xt, mod, s5_w_glu[0].astype(BF16),
                                  w_out_ab[0].astype(BF16), ln_g[0, 0].reshape(1, d), wr_t, rbias,
                                  nb=nb, t_len=t_len, ctx_len=ctx_len)
    x2 = _moe_layer(x1, h2, eidx, gates, mod, 0, ln_g[0, 1].reshape(1, d),
                    exp_w_gate[0].astype(BF16), exp_w_up[0].astype(BF16), exp_w_down[0].astype(BF16),
                    tiles_per_batch=tpb, ctx_tiles=ctx_tiles, nb=nb)

    cos1, sin1 = _attn_rope_tables(l_len, ctx_len)
    n_heads_cols = D_MODEL // LANES
    rope_tab1 = [0] * n_heads_cols + [1] * n_heads_cols + [None] * n_heads_cols
    qkv, = _inproj(x2, mod, 1, w_in_c[0].astype(BF16), cos1, sin1, rope_tab1, DIFF_DH // 4,
                   nb=nb, t_len=t_len, ctx_len=ctx_len)
    lf = diff_lambda[0].astype(F32)
    lambda_init = 0.8 - 0.6 * math.exp(-0.3 * 1)
    lam = (jnp.exp(jnp.sum(lf[0] * lf[1])) - jnp.exp(jnp.sum(lf[2] * lf[3])) + lambda_init).reshape(1)
    att = _diff_attention(qkv, lam, diff_subln_g[0], lambda_init,
                          nb=nb, l_len=l_len, t_len=t_len, ctx_len=ctx_len)
    x3, h3, eidx1, gates1 = _merge1(att, x2, mod, 1, w_out_c[0].astype(BF16), ln_g[1, 0].reshape(1, d),
                                    wr_t, rbias, nb=nb, l_len=l_len, t_len=t_len, ctx_len=ctx_len)
    out = _moe_layer(x3, h3, eidx1, gates1, mod, 1, ln_g[1, 1].reshape(1, d),
                     exp_w_gate[1].astype(BF16), exp_w_up[1].astype(BF16), exp_w_down[1].astype(BF16),
                     tiles_per_batch=l_len // ROW_TILE, ctx_tiles=0, nb=nb)
    return out.reshape(nb, l_len, d)
```

```python
import functools
import math

import jax
import jax.numpy as jnp
import numpy as np
from jax import lax
from jax.experimental import pallas as pl
from jax.experimental.pallas import tpu as pltpu

F32 = jnp.float32
BF16 = jnp.bfloat16

D_MODEL = 1024
DEPTH = 2
GRID_W = 64
ALPHA = (2.0 * DEPTH) ** 0.25
LN_EPS = 1e-5
GN_EPS = 1e-6
ROPE_BASE = 10000.0
RET_DK = 64
RET_DV = 128
RET_HEADS = 6
RET_QK_W = RET_HEADS * RET_DK
RET_V_W = RET_HEADS * RET_DV
S5_CH = 256
S5_P = 16
S5_G = 16
S5_N = 64
DIFF_HEADS = 8
DIFF_DH = 64
N_EXPERTS = 16
EXPERTS_PER_GROUP = 4
TOP_K = 2

LANES = 128
SUBLANES = 8
MXU_DIM = 256
ROW_TILE = 256
RET_CHUNK = 256
RET_STATE_UNROLL = 3
RET_OUT_UNROLL = 11
S5_CHUNK = 8
S5_HALVES = 2
MOE_ROWS = 256
DISPATCH_TILE = 512
ROW_WORDS = D_MODEL // 2
ROW_SUB = ROW_WORDS // LANES
ATT_TK = 768
ATT_SAFE_SHIFT = 48.0
VMEM_LIMIT = 48 * 1024 * 1024


def _cparams(sem, flags=None):
    return pltpu.CompilerParams(dimension_semantics=sem, vmem_limit_bytes=VMEM_LIMIT, flags=flags)


def _dot(a, b):
    return jnp.dot(a, b, preferred_element_type=F32)


def _dot_nt(a, b):
    return lax.dot_general(a, b, (((1,), (1,)), ((), ())), preferred_element_type=F32)


def _dot_tn(a, b):
    return lax.dot_general(a, b, (((0,), (0,)), ((), ())), preferred_element_type=F32)


def _split_bf16(x):
    hi = x.astype(BF16)
    lo = (x - hi.astype(F32)).astype(BF16)
    return hi, lo


def _dot3(a, b):
    ah, al = _split_bf16(a)
    bh, bl = _split_bf16(b)
    return _dot(ah, bh) + _dot(ah, bl) + _dot(al, bh)


def _dot3_nt(a, b):
    ah, al = _split_bf16(a)
    bh, bl = _split_bf16(b)
    return _dot_nt(ah, bh) + _dot_nt(ah, bl) + _dot_nt(al, bh)


def _sigmoid(x):
    return 1.0 / (1.0 + jnp.exp(-x))


def _silu(x):
    return x * _sigmoid(x)


def _pack_bf16_pairs(v):
    half = v.shape[1] // 2
    bits = lax.bitcast_convert_type(v.astype(BF16).astype(F32), jnp.uint32)
    return (bits[:, :half] >> 16) | (bits[:, half:] & jnp.uint32(0xFFFF0000))


def _unpack_bf16_pairs(p):
    lo = lax.bitcast_convert_type(p << 16, F32)
    hi = lax.bitcast_convert_type(p & jnp.uint32(0xFFFF0000), F32)
    return jnp.concatenate([lo, hi], axis=1)


def _store_rows(ref, packed):
    for j in range(ROW_SUB):
        ref[:, j, :] = packed[:, j * LANES:(j + 1) * LANES]


def _load_rows(ref):
    return jnp.concatenate([ref[:, j, :] for j in range(ROW_SUB)], axis=1)


def _gelu_tanh(x):
    c = math.sqrt(2.0 / math.pi)
    return 0.5 * x * (1.0 + jnp.tanh(c * (x + 0.044715 * (x * x * x))))


def _adaln_kernel(c_ref, w_ref, b_ref, o_ref):
    c = c_ref[...]
    o_ref[0] = _dot3(_silu(c), w_ref[0]) + b_ref[0]


def _adaln(c_pad, ada_w, ada_b):
    depth, d, n = ada_w.shape
    tn = 1536
    return pl.pallas_call(
        _adaln_kernel,
        grid=(depth, n // tn),
        in_specs=[
            pl.BlockSpec((SUBLANES, d), lambda i, j: (0, 0)),
            pl.BlockSpec((1, d, tn), lambda i, j: (i, 0, j)),
            pl.BlockSpec((1, 1, tn), lambda i, j: (i, 0, j)),
        ],
        out_specs=pl.BlockSpec((1, SUBLANES, tn), lambda i, j: (i, 0, j)),
        out_shape=jax.ShapeDtypeStruct((depth, SUBLANES, n), F32),
        compiler_params=_cparams(("parallel", "parallel")),
        name="adaln",
    )(c_pad, ada_w, ada_b.reshape(depth, 1, n))


def _mod_row(t, tiles_per_batch, ctx_tiles, nb):
    b = lax.div(t, tiles_per_batch)
    w = lax.rem(t, tiles_per_batch)
    return jnp.where(w < ctx_tiles, nb, b)


def _rope_block(a, cos, sin, half):
    lane = lax.broadcasted_iota(jnp.int32, a.shape, 1)
    first = lax.rem(lane, 2 * half) < half
    rot = jnp.where(first, pltpu.roll(a, LANES - half, 1), pltpu.roll(a, half, 1))
    return a * cos + rot * sin


def _chunk_perm(rows, transpose):
    per = rows // S5_CHUNK
    r = lax.broadcasted_iota(jnp.int32, (rows, rows), 0)
    c = lax.broadcasted_iota(jnp.int32, (rows, rows), 1)
    if transpose:
        r, c = c, r
    return (c == S5_CHUNK * lax.rem(r, per) + lax.div(r, per)).astype(BF16)


def _inproj_kernel(x_ref, mod_ref, w_ref, cos_ref, sin_ref, o_ref, *rest, tiles_per_batch,
                   ctx_tiles, nb, rope_tab, rope_half):
    d = x_ref.shape[1]
    n = o_ref.shape[1]
    r = _mod_row(pl.program_id(0), tiles_per_batch, ctx_tiles, nb)
    sh = mod_ref[0, pl.ds(r, 1), 0:d]
    sc = mod_ref[0, pl.ds(r, 1), d:2 * d]
    xm = (x_ref[...] * (1.0 + sc) + sh).astype(BF16)
    for j in range(n // MXU_DIM):
        acc = _dot(xm, w_ref[:, j * MXU_DIM:(j + 1) * MXU_DIM])
        parts = []
        for s in range(MXU_DIM // LANES):
            blk = acc[:, s * LANES:(s + 1) * LANES]
            tab = rope_tab[j * (MXU_DIM // LANES) + s]
            if tab is not None:
                blk = _rope_block(blk, cos_ref[tab], sin_ref[tab], rope_half)
            parts.append(blk)
        o_ref[:, j * MXU_DIM:(j + 1) * MXU_DIM] = jnp.concatenate(parts, axis=1).astype(BF16)
    if rest:
        u_ref, = rest
        rows = x_ref.shape[0]
        per = rows // S5_CHUNK
        u = _dot(xm, w_ref[:, n:n + S5_CH]).astype(BF16)
        up = _dot(_chunk_perm(rows, False), u).astype(BF16)
        for s in range(S5_CHUNK):
            for hf in range(S5_CH // LANES):
                u_ref[hf, :, s * LANES:(s + 1) * LANES] = up[s * per:(s + 1) * per,
                                                             hf * LANES:(hf + 1) * LANES]


def _inproj(x, mod, layer, w, cos, sin, rope_tab, rope_half, *, nb, t_len, ctx_len, s5_cols=0):
    nt, d = x.shape
    n = w.shape[1] - s5_cols
    tpb = t_len // ROW_TILE
    kern = functools.partial(_inproj_kernel, tiles_per_batch=tpb, ctx_tiles=ctx_len // ROW_TILE,
                             nb=nb, rope_tab=tuple(rope_tab), rope_half=rope_half)
    ntab = cos.shape[0]
    out_specs = [pl.BlockSpec((ROW_TILE, n), lambda t: (t, 0))]
    out_shape = [jax.ShapeDtypeStruct((nt, n), BF16)]
    if s5_cols:
        halves, per = s5_cols // LANES, ROW_TILE // S5_CHUNK
        out_specs.append(pl.BlockSpec((halves, per, S5_CHUNK * LANES), lambda t: (0, t, 0)))
        out_shape.append(jax.ShapeDtypeStruct((halves, nt // S5_CHUNK, S5_CHUNK * LANES), BF16))
    return pl.pallas_call(
        kern,
        grid=(nt // ROW_TILE,),
        in_specs=[
            pl.BlockSpec((ROW_TILE, d), lambda t: (t, 0)),
            pl.BlockSpec((1, SUBLANES, mod.shape[2]), lambda t: (layer, 0, 0)),
            pl.BlockSpec(w.shape, lambda t: (0, 0)),
            pl.BlockSpec((ntab, ROW_TILE, LANES), lambda t: (0, lax.rem(t, tpb), 0)),
            pl.BlockSpec((ntab, ROW_TILE, LANES), lambda t: (0, lax.rem(t, tpb), 0)),
        ],
        out_specs=out_specs,
        out_shape=out_shape,
        compiler_params=_cparams(("parallel",)),
        name="inproj%d" % layer,
    )(x, mod, w, cos, sin)


def _retention_kernel(lg_ref, qk_ref, v_ref, o_ref, sf_ref, sb_ref, *, nctx):
    c_len = RET_CHUNK
    t_len = qk_ref.shape[0]
    nc = t_len // c_len
    h = pl.program_id(1)
    lgf = lg_ref[0, h]
    lgb = lg_ref[1, h]
    ii = lax.broadcasted_iota(jnp.int32, (c_len, 1), 0).astype(F32)
    jj = lax.broadcasted_iota(jnp.int32, (1, c_len), 1).astype(F32)
    diff = ii - jj
    decay = jnp.where(diff >= 0.0, jnp.exp(lgf * jnp.maximum(diff, 0.0)),
                      jnp.exp(lgb * jnp.maximum(-diff, 0.0)))
    kdf = jnp.exp(lgf * (c_len - 1.0 - ii))
    kdb = jnp.exp(lgb * ii)
    qdf = jnp.exp(lgf * (ii + 1.0))
    qdb = jnp.exp(lgb * (c_len - ii))
    zrow = jnp.zeros((1, RET_DV), F32)
    gf_chunk = jnp.exp(zrow + lgf * c_len)
    gb_chunk = jnp.exp(zrow + lgb * c_len)

    def load(c):
        rows = pl.ds(pl.multiple_of(c * c_len, c_len), c_len)
        qk = qk_ref[rows, :].astype(F32)
        return qk[:, :RET_DK], qk[:, RET_DK:], v_ref[rows, :]

    def states(j, carry):
        sf, sb = carry
        cb = jnp.where(j < nctx, nctx - 1 - j, nc - 1 - (j - nctx))
        sf_ref[j] = sf
        sb_ref[cb] = sb
        _, kf, vf = load(j)
        _, kb, vb = load(cb)
        return (gf_chunk * sf + _dot_tn((kf * kdf).astype(BF16), vf),
                gb_chunk * sb + _dot_tn((kb * kdb).astype(BF16), vb))

    zero_state = jnp.zeros((RET_DK, RET_DV), F32)
    lax.fori_loop(0, nc, states, (zero_state, zero_state), unroll=RET_STATE_UNROLL)

    def out_chunk(c, carry):
        q, k, v = load(c)
        scores = _dot_nt(q.astype(BF16), k.astype(BF16)) * decay
        o = _dot(scores.astype(BF16), v)
        o = o + _dot((q * qdf).astype(BF16), sf_ref[c].astype(BF16))
        o = o + _dot((q * qdb).astype(BF16), sb_ref[c].astype(BF16))
        mu = jnp.mean(o, axis=-1, keepdims=True)
        oc = o - mu
        var = jnp.mean(oc * oc, axis=-1, keepdims=True)
        rows = pl.ds(pl.multiple_of(c * c_len, c_len), c_len)
        o_ref[rows, :] = (oc * lax.rsqrt(var + GN_EPS)).astype(BF16)
        return carry

    lax.fori_loop(0, nc, out_chunk, 0, unroll=RET_OUT_UNROLL)


def _retention(proj, log_gammas, *, nb, t_len, ctx_len):
    nt = proj.shape[0]
    nc = t_len // RET_CHUNK
    kern = functools.partial(_retention_kernel, nctx=ctx_len // RET_CHUNK)
    vcol0 = RET_HEADS
    return pl.pallas_call(
        kern,
        grid_spec=pltpu.PrefetchScalarGridSpec(
            num_scalar_prefetch=1,
            grid=(nb, RET_HEADS),
            in_specs=[
                pl.BlockSpec((t_len, LANES), lambda b, h, lg: (b, h)),
                pl.BlockSpec((t_len, LANES), lambda b, h, lg: (b, vcol0 + h)),
            ],
            out_specs=pl.BlockSpec((t_len, LANES), lambda b, h, lg: (b, h)),
            scratch_shapes=[pltpu.VMEM((nc, RET_DK, RET_DV), F32),
                            pltpu.VMEM((nc, RET_DK, RET_DV), F32)],
        ),
        out_shape=jax.ShapeDtypeStruct((nt, RET_V_W), BF16),
        compiler_params=_cparams(("parallel", "parallel")),
        name="retention",
    )(log_gammas, proj, proj)


def _s5_operators(lam_re, lam_im, log_dt, b_re, b_im, c_re, c_im, d_skip):
    tc = S5_CHUNK
    hp = lax.Precision.HIGHEST
    ks = jnp.arange(tc + 1, dtype=F32)
    pw, bbar, cm = [], [], []
    for direction in range(2):
        dt = jnp.exp(log_dt[direction].astype(F32))[:, None]
        lam = lax.complex(lam_re[direction].astype(F32), lam_im[direction].astype(F32))
        z = lam * dt
        p = jnp.exp(z[None] * ks[:, None, None])
        lam_bar = p[1]
        bb = ((lam_bar - 1.0) / lam)[..., None] * lax.complex(
            b_re[direction].astype(F32), b_im[direction].astype(F32))
        pw.append(p)
        bbar.append(bb)
        cm.append(lax.complex(c_re[direction].astype(F32), c_im[direction].astype(F32)))

    def lag_kernel(p, bb, c):
        return jnp.einsum('gpn,kgn,gnq->kgpq', c, p[:tc], bb, precision=hp).real

    kf = lag_kernel(pw[0], bbar[0], cm[0])
    kb = lag_kernel(pw[1], bbar[1], cm[1])
    k0 = kf[0] + kb[0] + jnp.eye(S5_P, dtype=F32)[None] * d_skip.astype(F32)[:, :, None]
    kcat = jnp.concatenate([kb[1:][::-1], k0[None], kf[1:]], axis=0)
    s_idx = jnp.arange(tc)[:, None]
    t_idx = jnp.arange(tc)[None, :]
    m5 = kcat[t_idx - s_idx + tc - 1]
    hg = S5_G // S5_HALVES
    eye = jnp.eye(hg, dtype=F32)
    split = lambda z, axis: z.reshape(z.shape[:axis] + (S5_HALVES, hg) + z.shape[axis + 1:])
    wide = tc * hg * S5_P
    intra = jnp.einsum('sthgpq,gk->hsgqtkp', split(m5, 2), eye).reshape(S5_HALVES, wide, wide)

    ef = pw[0][:tc][::-1][:, :, :, None] * bbar[0][None]
    eb = pw[1][:tc][:, :, :, None] * bbar[1][None]
    to_in = lambda e: jnp.einsum('shgnq,gk->hsgqkn', split(e, 1), eye).reshape(
        S5_HALVES, wide, hg * S5_N)
    w_in = jnp.concatenate([to_in(ef.real), to_in(ef.imag), to_in(eb.real), to_in(eb.imag)], axis=2)
    of = cm[0][None] * pw[0][1:][:, :, None, :]
    ob = cm[1][None] * pw[1][1:][::-1][:, :, None, :]
    to_out = lambda o: jnp.einsum('thgpn,gk->hkntgp', split(o, 1), eye).reshape(
        S5_HALVES, hg * S5_N, wide)
    w_out = jnp.concatenate([to_out(of.real), -to_out(of.imag), to_out(ob.real), -to_out(ob.imag)],
                            axis=1)
    a = jnp.stack([pw[0][tc].real, pw[0][tc].imag, pw[1][tc].real, pw[1][tc].imag], axis=0)
    a = a.reshape(4, S5_HALVES, hg * S5_N).transpose(1, 0, 2)
    return intra.astype(BF16), w_in.astype(BF16), w_out.astype(BF16), a


def _s5_kernel(x_ref, wi_ref, win_ref, wout_ref, a_ref, y_ref, st_ref, *, nctx):
    x = x_ref[0]
    nc = x.shape[0]
    w = a_ref.shape[2]
    st_ref[...] = _dot(x, win_ref[0])
    afr, afi, abr, abi = (a_ref[0, i:i + 1, :] for i in range(4))

    def step(j, carry):
        fr, fi, br, bi = carry
        rf = pl.ds(j, 1)
        rb = pl.ds(jnp.where(j < nctx, nctx - 1 - j, nc - 1 - (j - nctx)), 1)
        efr, efi = st_ref[rf, 0:w], st_ref[rf, w:2 * w]
        ebr, ebi = st_ref[rb, 2 * w:3 * w], st_ref[rb, 3 * w:4 * w]
        st_ref[rf, 0:w] = fr
        st_ref[rf, w:2 * w] = fi
        st_ref[rb, 2 * w:3 * w] = br
        st_ref[rb, 3 * w:4 * w] = bi
        return (afr * fr - afi * fi + efr, afr * fi + afi * fr + efi,
                abr * br - abi * bi + ebr, abr * bi + abi * br + ebi)

    z = jnp.zeros((1, w), F32)
    lax.fori_loop(0, nc, step, (z, z, z, z), unroll=4)
    y_ref[0] = (_dot(x, wi_ref[0]) + _dot(st_ref[...].astype(BF16), wout_ref[0])).astype(BF16)


def _s5(xc, intra, w_in, w_out, a, *, nb, nctx):
    halves, rows, wide = xc.shape
    nc = rows // nb
    kern = functools.partial(_s5_kernel, nctx=nctx)
    per_half = lambda arr: pl.BlockSpec((1,) + arr.shape[1:], lambda hf, b: (hf, 0, 0))
    return pl.pallas_call(
        kern,
        grid=(halves, nb),
        in_specs=[
            pl.BlockSpec((1, nc, wide), lambda hf, b: (hf, b, 0)),
            per_half(intra), per_half(w_in), per_half(w_out), per_half(a),
        ],
        out_specs=pl.BlockSpec((1, nc, wide), lambda hf, b: (hf, b, 0)),
        out_shape=jax.ShapeDtypeStruct(xc.shape, BF16),
        scratch_shapes=[pltpu.VMEM((nc, w_in.shape[2]), F32)],
        compiler_params=_cparams(("parallel", "parallel")),
        name="s5",
    )(xc, intra, w_in, w_out, a)


def _route(logits_t, bias):
    scores = _sigmoid(logits_t)
    biased = scores + bias
    s_rows = [scores[e:e + 1, :] for e in range(N_EXPERTS)]
    b_rows = [biased[e:e + 1, :] for e in range(N_EXPERTS)]
    n_groups = N_EXPERTS // EXPERTS_PER_GROUP
    best = None
    sel = None
    for g in range(n_groups):
        a, b, c, d = b_rows[4 * g:4 * g + 4]
        hi1, lo1 = jnp.maximum(a, b), jnp.minimum(a, b)
        hi2, lo2 = jnp.maximum(c, d), jnp.minimum(c, d)
        top1 = jnp.maximum(hi1, hi2)
        top2 = jnp.maximum(jnp.minimum(hi1, hi2), jnp.maximum(lo1, lo2))
        gs = top1 + top2
        if g == 0:
            best, sel = gs, jnp.zeros(gs.shape, jnp.int32)
        else:
            better = gs > best
            sel = jnp.where(better, g, sel)
            best = jnp.where(better, gs, best)
    neg = jnp.full(best.shape, -jnp.inf, F32)
    masked = [jnp.where(sel == (e // EXPERTS_PER_GROUP), b_rows[e], neg) for e in range(N_EXPERTS)]
    v1, i1, g1 = masked[0], jnp.zeros(best.shape, jnp.int32), s_rows[0]
    for e in range(1, N_EXPERTS):
        better = masked[e] > v1
        v1 = jnp.where(better, masked[e], v1)
        i1 = jnp.where(better, e, i1)
        g1 = jnp.where(better, s_rows[e], g1)
    v2, i2, g2 = neg, jnp.zeros(best.shape, jnp.int32), jnp.zeros(best.shape, F32)
    for e in range(N_EXPERTS):
        cand = jnp.where(i1 == e, neg, masked[e])
        better = cand > v2
        v2 = jnp.where(better, cand, v2)
        i2 = jnp.where(better, e, i2)
        g2 = jnp.where(better, s_rows[e], g2)
    tot = g1 + g2
    return jnp.concatenate([i1, i2], axis=0), jnp.concatenate([g1 / tot, g2 / tot], axis=0)


def _tail(x, o, mod_ref, r, lng, wr, rb, x1_ref, h2_ref, ei_ref, gt_ref):
    d = x.shape[1]
    g1 = mod_ref[0, pl.ds(r, 1), 2 * d:3 * d]
    sh2 = mod_ref[0, pl.ds(r, 1), 3 * d:4 * d]
    sc2 = mod_ref[0, pl.ds(r, 1), 4 * d:5 * d]
    y = ALPHA * x + g1 * o
    mu = jnp.mean(y, axis=-1, keepdims=True)
    yc = y - mu
    var = jnp.mean(yc * yc, axis=-1, keepdims=True)
    x1 = yc * lax.rsqrt(var + LN_EPS) * lng
    h2 = x1 * (1.0 + sc2) + sh2
    x1_ref[...] = x1
    _store_rows(h2_ref, _pack_bf16_pairs(h2))
    ei, gt = _route(_dot3_nt(wr, h2), rb)
    ei_ref[...] = ei
    gt_ref[...] = gt


def _merge0_kernel(r_ref, g_ref, s_ref, x_ref, mod_ref, wglu_ref, wout_ref, lng_ref, wr_ref,
                   rb_ref, x1_ref, h2_ref, ei_ref, gt_ref, *, tiles_per_batch, ctx_tiles, nb):
    r = _mod_row(pl.program_id(0), tiles_per_batch, ctx_tiles, nb)
    ret = r_ref[...].astype(F32) * _silu(g_ref[...].astype(F32))
    rows = x_ref.shape[0]
    sp = jnp.concatenate(
        [jnp.concatenate([s_ref[hf, :, s * LANES:(s + 1) * LANES] for hf in range(S5_HALVES)], axis=1)
         for s in range(S5_CHUNK)], axis=0)
    s5 = _dot(_chunk_perm(rows, True), sp)
    z = _dot(_gelu_tanh(s5).astype(BF16), wglu_ref[...])
    zz = z[:, :S5_CH] * _sigmoid(z[:, S5_CH:])
    o = _dot(ret.astype(BF16), wout_ref[0:RET_V_W, :]) + _dot(zz.astype(BF16), wout_ref[RET_V_W:, :])
    _tail(x_ref[...], o, mod_ref, r, lng_ref[...], wr_ref[...], rb_ref[...],
          x1_ref, h2_ref, ei_ref, gt_ref)


def _merge1_kernel(a_ref, x_ref, mod_ref, wout_ref, lng_ref, wr_ref, rb_ref,
                   x1_ref, h2_ref, ei_ref, gt_ref, *, tiles_per_batch):
    r = lax.div(pl.program_id(0), tiles_per_batch)
    o = _dot(a_ref[...], wout_ref[...])
    _tail(x_ref[...], o, mod_ref, r, lng_ref[...], wr_ref[...], rb_ref[...],
          x1_ref, h2_ref, ei_ref, gt_ref)


def _tail_outs(n_rows, d):
    shapes = (jax.ShapeDtypeStruct((n_rows, d), F32),
              jax.ShapeDtypeStruct((n_rows, ROW_SUB, LANES), jnp.uint32),
              jax.ShapeDtypeStruct((TOP_K, n_rows), jnp.int32),
              jax.ShapeDtypeStruct((TOP_K, n_rows), F32))
    specs = (pl.BlockSpec((ROW_TILE, d), lambda t: (t, 0)),
             pl.BlockSpec((ROW_TILE, ROW_SUB, LANES), lambda t: (t, 0, 0)),
             pl.BlockSpec((TOP_K, ROW_TILE), lambda t: (0, t)),
             pl.BlockSpec((TOP_K, ROW_TILE), lambda t: (0, t)))
    return shapes, specs


def _merge0(ret, proj, s5y, x, mod, w_glu, w_out, lng, wr_t, rbias, *, nb, t_len, ctx_len):
    nt, d = x.shape
    tpb = t_len // ROW_TILE
    kern = functools.partial(_merge0_kernel, tiles_per_batch=tpb, ctx_tiles=ctx_len // ROW_TILE, nb=nb)
    shapes, specs = _tail_outs(nt, d)
    gcol = (2 * RET_QK_W + RET_V_W) // RET_V_W
    full = lambda a: pl.BlockSpec(a.shape, lambda t: (0,) * a.ndim)
    return pl.pallas_call(
        kern,
        grid=(nt // ROW_TILE,),
        in_specs=[
            pl.BlockSpec((ROW_TILE, RET_V_W), lambda t: (t, 0)),
            pl.BlockSpec((ROW_TILE, RET_V_W), lambda t: (t, gcol)),
            pl.BlockSpec((S5_HALVES, ROW_TILE // S5_CHUNK, S5_CHUNK * LANES), lambda t: (0, t, 0)),
            pl.BlockSpec((ROW_TILE, d), lambda t: (t, 0)),
            pl.BlockSpec((1, SUBLANES, mod.shape[2]), lambda t: (0, 0, 0)),
            full(w_glu), full(w_out), full(lng), full(wr_t), full(rbias),
        ],
        out_specs=specs,
        out_shape=shapes,
        compiler_params=_cparams(("parallel",)),
        name="merge0",
    )(ret, proj, s5y, x, mod, w_glu, w_out, lng, wr_t, rbias)


def _merge1(att, x, mod, layer, w_out, lng, wr_t, rbias, *, nb, l_len, t_len, ctx_len):
    n_lat, d = att.shape
    tpb = l_len // ROW_TILE
    tpb_t = t_len // ROW_TILE
    ctx_tiles = ctx_len // ROW_TILE
    kern = functools.partial(_merge1_kernel, tiles_per_batch=tpb)
    shapes, specs = _tail_outs(n_lat, d)
    full = lambda a: pl.BlockSpec(a.shape, lambda t: (0,) * a.ndim)
    xrow = lambda t: (lax.div(t, tpb) * tpb_t + ctx_tiles + lax.rem(t, tpb), 0)
    return pl.pallas_call(
        kern,
        grid=(n_lat // ROW_TILE,),
        in_specs=[
            pl.BlockSpec((ROW_TILE, d), lambda t: (t, 0)),
            pl.BlockSpec((ROW_TILE, d), xrow),
            pl.BlockSpec((1, SUBLANES, mod.shape[2]), lambda t: (layer, 0, 0)),
            full(w_out), full(lng), full(wr_t), full(rbias),
        ],
        out_specs=specs,
        out_shape=shapes,
        compiler_params=_cparams(("parallel",)),
        name="merge1",
    )(att, x, mod, w_out, lng, wr_t, rbias)


def _moe_plan(eidx):
    k, n = eidx.shape
    a = k * n
    e_flat = eidx.reshape(a)
    seg = TOP_K * ROW_TILE
    onehot = (e_flat[:, None] == jnp.arange(N_EXPERTS, dtype=jnp.int32)[None, :]).astype(F32)
    onehot = onehot.reshape(a // seg, seg, N_EXPERTS)
    tril = lambda m: jnp.tril(jnp.ones((m, m), F32))
    within = jnp.einsum('ij,tjk->tik', tril(seg), onehot)
    seg_total = within[:, -1, :]
    seg_end = jnp.sum(tril(a // seg)[:, :, None] * seg_total[None], axis=1)
    counts = seg_end[-1].astype(jnp.int32)
    csum = within + (seg_end - seg_total)[:, None, :]
    padded = (counts + MOE_ROWS - 1) // MOE_ROWS * MOE_ROWS
    pad_end = jnp.sum(jnp.tril(jnp.ones((N_EXPERTS, N_EXPERTS), jnp.int32)) * padded[None, :], axis=1)
    pad_start = pad_end - padded
    dest = jnp.sum(onehot * (csum - 1.0 + pad_start.astype(F32)[None, None, :]), axis=-1)
    dest = dest.reshape(a).astype(jnp.int32)
    n_blocks = -(-(a + N_EXPERTS * (MOE_ROWS - 1)) // MOE_ROWS)
    block_expert = jnp.minimum(
        jnp.searchsorted(pad_end, jnp.arange(n_blocks, dtype=jnp.int32) * MOE_ROWS, side='right'),
        N_EXPERTS - 1).astype(jnp.int32)
    return dest.reshape(k, n), block_expert, n_blocks


def _tile_rows_of(dest, tile):
    k, n = dest.shape
    return dest.reshape(k, n // tile, tile).transpose(1, 0, 2).reshape(n // tile, 1, k * tile)


def _dispatch_kernel(dest_ref, h_ref, xs_in_hbm, xs_hbm, sem):
    del xs_in_hbm
    rows = h_ref.shape[0]

    def start(r, c):
        for choice in range(TOP_K):
            pltpu.make_async_copy(h_ref.at[r], xs_hbm.at[dest_ref[0, 0, choice * rows + r]],
                                  sem).start(priority=choice)
        return c

    lax.fori_loop(0, rows, start, 0, unroll=8)
    for _ in range(TOP_K):
        pltpu.make_async_copy(h_ref, xs_hbm.at[pl.ds(0, rows)], sem).wait()


def _dispatch(h, dest, n_rows):
    n = h.shape[0]
    tile = DISPATCH_TILE if n % DISPATCH_TILE == 0 else ROW_TILE
    return pl.pallas_call(
        _dispatch_kernel,
        grid=(n // tile,),
        in_specs=[
            pl.BlockSpec((1, 1, TOP_K * tile), lambda t: (t, 0, 0), memory_space=pltpu.SMEM),
            pl.BlockSpec((tile, ROW_SUB, LANES), lambda t: (t, 0, 0)),
            pl.BlockSpec(memory_space=pl.ANY),
        ],
        out_specs=pl.BlockSpec(memory_space=pl.ANY),
        out_shape=jax.ShapeDtypeStruct((n_rows, ROW_SUB, LANES), jnp.uint32),
        scratch_shapes=[pltpu.SemaphoreType.DMA],
        input_output_aliases={2: 0},
        compiler_params=_cparams(("arbitrary",)),
        name="moe_dispatch",
    )(_tile_rows_of(dest, tile), h, jnp.zeros((n_rows, ROW_SUB, LANES), jnp.uint32))


def _experts_kernel(be_ref, x_ref, wg_ref, wu_ref, wd_ref, o_ref):
    x = _unpack_bf16_pairs(_load_rows(x_ref)).astype(BF16)
    hg = _dot(x, wg_ref[0])
    hu = _dot(x, wu_ref[0])
    _store_rows(o_ref, _pack_bf16_pairs(_dot((_silu(hg) * hu).astype(BF16), wd_ref[0])))


def _experts(xs, block_expert, layer, wg, wu, wd):
    n_blocks = block_expert.shape[0]
    d, dff = wg.shape[2], wg.shape[3]
    rows_spec = pl.BlockSpec((MOE_ROWS, ROW_SUB, LANES), lambda i, be: (i, 0, 0))
    return pl.pallas_call(
        _experts_kernel,
        grid_spec=pltpu.PrefetchScalarGridSpec(
            num_scalar_prefetch=1,
            grid=(n_blocks,),
            in_specs=[
                rows_spec,
                pl.BlockSpec((None, 1, d, dff), lambda i, be: (layer, be[i], 0, 0)),
                pl.BlockSpec((None, 1, d, dff), lambda i, be: (layer, be[i], 0, 0)),
                pl.BlockSpec((None, 1, dff, d), lambda i, be: (layer, be[i], 0, 0)),
            ],
            out_specs=rows_spec,
        ),
        out_shape=jax.ShapeDtypeStruct(xs.shape, jnp.uint32),
        compiler_params=_cparams(("parallel",)),
        name="moe_experts",
    )(block_expert, xs, wg, wu, wd)


def _combine_kernel(dcur_ref, dnxt_ref, x_ref, gt_ref, mod_ref, lng_ref, y_hbm, o_ref, ybuf, sem, *,
                    tiles_per_batch, ctx_tiles, nb):
    t = pl.program_id(0)
    last = pl.num_programs(0) - 1
    rows = x_ref.shape[0]
    n = TOP_K * rows
    d = x_ref.shape[1]
    slot = lax.rem(t, 2)

    def start_all(idx_ref, s):
        def start(r, c):
            for choice in range(TOP_K):
                j = choice * rows + r
                pltpu.make_async_copy(y_hbm.at[idx_ref[0, 0, j]], ybuf.at[s, j],
                                      sem.at[s]).start(priority=choice)
            return c

        lax.fori_loop(0, rows, start, 0, unroll=8)

    @pl.when(t == 0)
    def _():
        start_all(dcur_ref, 0)

    @pl.when(t < last)
    def _():
        start_all(dnxt_ref, 1 - slot)

    pltpu.make_async_copy(y_hbm.at[pl.ds(0, n)], ybuf.at[slot], sem.at[slot]).wait()
    r = _mod_row(t, tiles_per_batch, ctx_tiles, nb)
    gt = gt_ref[...]
    yb = ybuf.at[slot]
    y = (_unpack_bf16_pairs(_load_rows(yb.at[pl.ds(0, rows)])) * gt[:, 0:1]
         + _unpack_bf16_pairs(_load_rows(yb.at[pl.ds(rows, rows)])) * gt[:, 1:2])
    g2 = mod_ref[0, pl.ds(r, 1), 5 * d:6 * d]
    z = ALPHA * x_ref[...] + g2 * y
    mu = jnp.mean(z, axis=-1, keepdims=True)
    zc = z - mu
    var = jnp.mean(zc * zc, axis=-1, keepdims=True)
    o_ref[...] = zc * lax.rsqrt(var + LN_EPS) * lng_ref[...]


def _combine(x1, dest, gates, mod, layer, lng, ys, *, tiles_per_batch, ctx_tiles, nb):
    n, d = x1.shape
    nt = n // ROW_TILE
    dest_t = _tile_rows_of(dest, ROW_TILE)
    kern = functools.partial(_combine_kernel, tiles_per_batch=tiles_per_batch, ctx_tiles=ctx_tiles, nb=nb)
    idx_spec = lambda f: pl.BlockSpec((1, 1, TOP_K * ROW_TILE), f, memory_space=pltpu.SMEM)
    return pl.pallas_call(
        kern,
        grid=(nt,),
        in_specs=[
            idx_spec(lambda t: (t, 0, 0)),
            idx_spec(lambda t: (jnp.minimum(t + 1, nt - 1), 0, 0)),
            pl.BlockSpec((ROW_TILE, d), lambda t: (t, 0)),
            pl.BlockSpec((ROW_TILE, TOP_K), lambda t: (t, 0)),
            pl.BlockSpec((1, SUBLANES, mod.shape[2]), lambda t: (layer, 0, 0)),
            pl.BlockSpec((1, d), lambda t: (0, 0)),
            pl.BlockSpec(memory_space=pl.ANY),
        ],
        out_specs=pl.BlockSpec((ROW_TILE, d), lambda t: (t, 0)),
        out_shape=jax.ShapeDtypeStruct((n, d), F32),
        scratch_shapes=[pltpu.VMEM((2, TOP_K * ROW_TILE, ROW_SUB, LANES), jnp.uint32),
                        pltpu.SemaphoreType.DMA((2,))],
        compiler_params=_cparams(("arbitrary",)),
        name="moe_combine%d" % layer,
    )(dest_t, dest_t, x1, gates.T, mod, lng, ys)


def _moe_layer(x1, h2, eidx, gates, mod, layer, lng, wg, wu, wd, *, tiles_per_batch, ctx_tiles, nb):
    dest, block_expert, n_blocks = _moe_plan(eidx)
    xs = _dispatch(h2, dest, n_blocks * MOE_ROWS)
    ys = _experts(xs, block_expert, layer, wg, wu, wd)
    return _combine(x1, dest, gates, mod, layer, lng, ys,
                    tiles_per_batch=tiles_per_batch, ctx_tiles=ctx_tiles, nb=nb)


def _half_norms(x):
    lane = lax.broadcasted_iota(jnp.int32, x.shape, 1)
    sq = x * x
    lo = jnp.sum(jnp.where(lane < DIFF_DH, sq, 0.0), axis=-1, keepdims=True)
    hi = jnp.sum(jnp.where(lane >= DIFF_DH, sq, 0.0), axis=-1, keepdims=True)
    return jnp.sqrt(lo), jnp.sqrt(hi)


def _attn_kernel(lam_ref, q_ref, k_ref, v_ref, g_ref, o_ref, vext, s_buf0, s_buf1,
                 p_buf0, p_buf1, corr_buf0, corr_buf1, m_buf, shift_buf, acc, *, out_scale, ctx_len):
    t_len = k_ref.shape[0]
    nk = t_len // ATT_TK
    dv = v_ref.shape[1]
    tq = acc.shape[1]
    nq = o_ref.shape[0] // tq
    n_tiles = nq * nk

    vext[:, 0:dv] = v_ref[...]
    vext[:, dv:2 * dv] = jnp.ones((t_len, dv), BF16)
    lam = lam_ref[0]
    gain = g_ref[...] * out_scale

    s_bufs, p_bufs, corr_bufs = (s_buf0, s_buf1), (p_buf0, p_buf1), (corr_buf0, corr_buf1)

    def key_rows(kj):
        return pl.ds(pl.multiple_of(kj * ATT_TK, ATT_TK), ATT_TK)

    def query_rows(qi):
        return pl.ds(pl.multiple_of(ctx_len + qi * tq, ROW_TILE), tq)

    def column_max(norms, carry):
        return tuple(jnp.maximum(c, jnp.max(n, axis=0, keepdims=True)) for n, c in zip(norms, carry))

    def key_norms(kj, carry):
        return column_max(_half_norms(k_ref[key_rows(kj), :].astype(F32)), carry)

    zero11 = jnp.zeros((1, 1), F32)
    kmax = lax.fori_loop(0, nk, key_norms, (zero11, zero11))

    def query_shifts(qi, carry):
        shifts = tuple(n * k for n, k in zip(_half_norms(q_ref[query_rows(qi), :].astype(F32)), kmax))
        shift_buf[qi, :, 0:1] = shifts[0]
        shift_buf[qi, :, 1:2] = shifts[1]
        return column_max(shifts, carry)

    worst = lax.fori_loop(0, nq, query_shifts, (zero11, zero11))
    bounded = jnp.max(jnp.maximum(worst[0], worst[1])) <= ATT_SAFE_SHIFT

    def advance(tile):
        qi, kj = tile
        wrap = kj + 1 == nk
        return jnp.where(wrap, qi + 1, qi), jnp.where(wrap, 0, kj + 1)

    def scores(tile, slot):
        qi, kj = tile
        q = q_ref[query_rows(qi), :]
        lane = lax.broadcasted_iota(jnp.int32, q.shape, 1)
        zero = jnp.zeros(q.shape, q.dtype)
        k = k_ref[key_rows(kj), :]
        s_bufs[slot][0] = _dot_nt(jnp.where(lane < DIFF_DH, q, zero), k)
        s_bufs[slot][1] = _dot_nt(jnp.where(lane >= DIFF_DH, q, zero), k)

    def numerators(tile, slot, online):
        qi, kj = tile
        for w in range(2):
            s = s_bufs[slot][w]
            if online:
                m_old = jnp.where(kj == 0, -jnp.inf, m_buf[w])
                m_new = jnp.maximum(m_old, jnp.max(s, axis=-1, keepdims=True))
                corr_bufs[slot][w] = jnp.exp2(m_old - m_new)
                m_buf[w] = m_new
            else:
                m_new = shift_buf[qi, :, w:w + 1]
            p_bufs[slot][w] = jnp.exp2(s - m_new).astype(BF16)

    def values(tile, slot, online):
        qi, kj = tile
        ve = vext[key_rows(kj), :]
        a = []
        for w in range(2):
            keep = corr_bufs[slot][w] if online else jnp.where(kj == 0, 0.0, 1.0)
            a.append(keep * acc[w] + _dot(p_bufs[slot][w], ve))
            acc[w] = a[w]
        o = a[0][:, 0:dv] / a[0][:, dv:2 * dv] - lam * (a[1][:, 0:dv] / a[1][:, dv:2 * dv])
        o = o * lax.rsqrt(jnp.mean(o * o, axis=-1, keepdims=True) + GN_EPS)
        o_ref[pl.ds(pl.multiple_of(qi * tq, tq), tq), :] = (o * gain).astype(BF16)

    def pipeline(online):
        def step(tiles, slot):
            a, b, c = tiles
            values(c, slot, online)
            scores(a, slot)
            numerators(b, 1 - slot, online)
            return advance(a), a, b

        acc[...] = jnp.zeros(acc.shape, F32)
        t0 = (jnp.int32(0), jnp.int32(0))
        t1 = advance(t0)
        scores(t0, 0)
        scores(t1, 1)
        numerators(t0, 0, online)

        def pair(_, tiles):
            return step(step(tiles, 0), 1)

        _, last, prev = lax.fori_loop(0, (n_tiles - 2) // 2, pair, (advance(t1), t1, t0))
        numerators(last, (n_tiles - 1) % 2, online)
        values(prev, n_tiles % 2, online)
        values(last, (n_tiles - 1) % 2, online)

    pl.when(bounded)(functools.partial(pipeline, False))
    pl.when(jnp.logical_not(bounded))(functools.partial(pipeline, True))


def _diff_attention(qkv, lam, subln_g, lambda_init, *, nb, l_len, t_len, ctx_len):
    d = D_MODEL
    tq = 2 * ROW_TILE
    dv = 2 * DIFF_DH
    nq = l_len // tq
    assert (nq * (t_len // ATT_TK)) % 2 == 0
    kern = functools.partial(_attn_kernel, out_scale=1.0 - lambda_init, ctx_len=ctx_len)
    return pl.pallas_call(
        kern,
        grid_spec=pltpu.PrefetchScalarGridSpec(
            num_scalar_prefetch=1,
            grid=(nb, DIFF_HEADS),
            in_specs=[
                pl.BlockSpec((t_len, LANES), lambda b, h, lam: (b, h)),
                pl.BlockSpec((t_len, LANES), lambda b, h, lam: (b, DIFF_HEADS + h)),
                pl.BlockSpec((t_len, LANES), lambda b, h, lam: (b, 2 * DIFF_HEADS + h)),
                pl.BlockSpec((1, LANES), lambda b, h, lam: (0, 0)),
            ],
            out_specs=pl.BlockSpec((l_len, LANES), lambda b, h, lam: (b, h)),
            scratch_shapes=[
                pltpu.VMEM((t_len, 2 * dv), BF16),
                pltpu.VMEM((2, tq, ATT_TK), F32), pltpu.VMEM((2, tq, ATT_TK), F32),
                pltpu.VMEM((2, tq, ATT_TK), BF16), pltpu.VMEM((2, tq, ATT_TK), BF16),
                pltpu.VMEM((2, tq, 1), F32), pltpu.VMEM((2, tq, 1), F32),
                pltpu.VMEM((2, tq, 1), F32), pltpu.VMEM((nq, tq, 2), F32),
                pltpu.VMEM((2, tq, 2 * dv), F32),
            ],
        ),
        out_shape=jax.ShapeDtypeStruct((nb * l_len, d), BF16),
        compiler_params=_cparams(("parallel", "parallel")),
        name="diff_attention",
    )(lam, qkv, qkv, qkv, subln_g.reshape(1, LANES).astype(F32))


def _ret_rope_tables(l_len, ctx_len):
    half = RET_DK // 2
    inv = ROPE_BASE ** (-jnp.arange(0, RET_DK, 2, dtype=F32) / RET_DK)
    ang = jnp.arange(l_len, dtype=F32)[:, None] * inv[None, :]
    ang = jnp.concatenate([jnp.zeros((ctx_len, half), F32), ang], axis=0)
    cos64 = jnp.concatenate([jnp.cos(ang), jnp.cos(ang)], axis=1)
    sin64 = jnp.concatenate([-jnp.sin(ang), jnp.sin(ang)], axis=1)
    kscale = RET_DK ** -0.5
    cos = jnp.concatenate([cos64, cos64 * kscale], axis=1)
    sin = jnp.concatenate([sin64, sin64 * kscale], axis=1)
    return cos[None], sin[None]


def _attn_rope_tables(l_len, ctx_len):
    quarter = DIFF_DH // 4
    inv = ROPE_BASE ** (-jnp.arange(0, DIFF_DH // 2, 2, dtype=F32) / (DIFF_DH // 2))
    pos = jnp.arange(l_len)
    ang_r = (pos // GRID_W).astype(F32)[:, None] * inv[None, :]
    ang_c = (pos % GRID_W).astype(F32)[:, None] * inv[None, :]
    pad = lambda a: jnp.concatenate([jnp.zeros((ctx_len, quarter), F32), a], axis=0)
    ang_r, ang_c = pad(ang_r), pad(ang_c)
    cos64 = jnp.concatenate([jnp.cos(ang_r)] * 2 + [jnp.cos(ang_c)] * 2, axis=1)
    sin64 = jnp.concatenate([-jnp.sin(ang_r), jnp.sin(ang_r), -jnp.sin(ang_c), jnp.sin(ang_c)], axis=1)
    cos = jnp.concatenate([cos64, cos64], axis=1)
    sin = jnp.concatenate([sin64, sin64], axis=1)
    qscale = DIFF_DH ** -0.5 * math.log2(math.e)
    return jnp.stack([cos * qscale, cos]), jnp.stack([sin * qscale, sin])


def kernel(x, c, ctx, c_ctx, ada_w, ada_b, ln_g, w_in_ab, ret_decay_logit, s5_lam_re, s5_lam_im,
           s5_log_dt, s5_b_re, s5_b_im, s5_c_re, s5_c_im, s5_d, s5_w_glu, w_out_ab, w_in_c,
           diff_lambda, diff_subln_g, w_out_c, router_w, router_bias, exp_w_gate, exp_w_up,
           exp_w_down):
    nb, l_len, d = x.shape
    ctx_len = ctx.shape[1]
    t_len = ctx_len + l_len
    nt = nb * t_len
    tpb = t_len // ROW_TILE
    ctx_tiles = ctx_len // ROW_TILE
    assert d == D_MODEL and nb < SUBLANES
    assert l_len % (2 * ROW_TILE) == 0 and ctx_len % ROW_TILE == 0 and t_len % ATT_TK == 0

    xt = jnp.concatenate([ctx, x], axis=1).reshape(nt, d)
    c_all = jnp.concatenate([c, c_ctx[None].astype(c.dtype)], axis=0)
    c_pad = jnp.zeros((SUBLANES, d), F32).at[:nb + 1].set(c_all)
    mod = _adaln(c_pad, ada_w, ada_b)

    wr_t = router_w.T
    rbias = router_bias.reshape(N_EXPERTS, 1).astype(F32)

    w0 = w_in_ab[0]
    q_w, k_w, v_w, g_w, u_w = jnp.split(w0, (RET_QK_W, 2 * RET_QK_W, 2 * RET_QK_W + RET_V_W,
                                             2 * RET_QK_W + 2 * RET_V_W), axis=1)
    qk_w = jnp.concatenate([q_w.reshape(d, RET_HEADS, RET_DK), k_w.reshape(d, RET_HEADS, RET_DK)],
                           axis=2).reshape(d, 2 * RET_QK_W)
    w0p = jnp.concatenate([qk_w, v_w, g_w, u_w], axis=1).astype(BF16)
    cos0, sin0 = _ret_rope_tables(l_len, ctx_len)
    rope_tab0 = [0] * RET_HEADS + [None] * ((w0p.shape[1] - S5_CH - 2 * RET_QK_W) // LANES)
    proj0, u5 = _inproj(xt, mod, 0, w0p, cos0, sin0, rope_tab0, RET_DK // 2,
                        nb=nb, t_len=t_len, ctx_len=ctx_len, s5_cols=S5_CH)

    log_gammas = jax.nn.log_sigmoid(ret_decay_logit[0].astype(F32))
    ret = _retention(proj0, log_gammas, nb=nb, t_len=t_len, ctx_len=ctx_len)

    s5_ops = _s5_operators(s5_lam_re[0], s5_lam_im[0], s5_log_dt[0], s5_b_re[0], s5_b_im[0],
                           s5_c_re[0], s5_c_im[0], s5_d[0])
    s5y = _s5(u5, *s5_ops, nb=nb, nctx=ctx_len // S5_CHUNK)

    x1, h2, eidx, gates = _merge0(ret, proj0, s5y, xt, mod, s5_w_glu[0].astype(BF16),
                                  w_out_ab[0].astype(BF16), ln_g[0, 0].reshape(1, d), wr_t, rbias,
                                  nb=nb, t_len=t_len, ctx_len=ctx_len)
    experts = (exp_w_gate.astype(BF16), exp_w_up.astype(BF16), exp_w_down.astype(BF16))
    x2 = _moe_layer(x1, h2, eidx, gates, mod, 0, ln_g[0, 1].reshape(1, d), *experts,
                    tiles_per_batch=tpb, ctx_tiles=ctx_tiles, nb=nb)

    cos1, sin1 = _attn_rope_tables(l_len, ctx_len)
    n_heads_cols = D_MODEL // LANES
    rope_tab1 = [0] * n_heads_cols + [1] * n_heads_cols + [None] * n_heads_cols
    qkv, = _inproj(x2, mod, 1, w_in_c[0].astype(BF16), cos1, sin1, rope_tab1, DIFF_DH // 4,
                   nb=nb, t_len=t_len, ctx_len=ctx_len)
    lf = diff_lambda[0].astype(F32)
    lambda_init = 0.8 - 0.6 * math.exp(-0.3 * 1)
    lam = (jnp.exp(jnp.sum(lf[0] * lf[1])) - jnp.exp(jnp.sum(lf[2] * lf[3])) + lambda_init).reshape(1)
    att = _diff_attention(qkv, lam, diff_subln_g[0], lambda_init,
                          nb=nb, l_len=l_len, t_len=t_len, ctx_len=ctx_len)
    x3, h3, eidx1, gates1 = _merge1(att, x2, mod, 1, w_out_c[0].astype(BF16), ln_g[1, 0].reshape(1, d),
                                    wr_t, rbias, nb=nb, l_len=l_len, t_len=t_len, ctx_len=ctx_len)
    out = _moe_layer(x3, h3, eidx1, gates1, mod, 1, ln_g[1, 1].reshape(1, d), *experts,
                     tiles_per_batch=l_len // ROW_TILE, ctx_tiles=0, nb=nb)
    return out.reshape(nb, l_len, d)
```

```python
import functools
import math

import jax
import jax.numpy as jnp
import numpy as np
from jax import lax
from jax.experimental import pallas as pl
from jax.experimental.pallas import tpu as pltpu

F32 = jnp.float32
BF16 = jnp.bfloat16

D_MODEL = 1024
DEPTH = 2
GRID_W = 64
ALPHA = (2.0 * DEPTH) ** 0.25
LN_EPS = 1e-5
GN_EPS = 1e-6
ROPE_BASE = 10000.0
RET_DK = 64
RET_DV = 128
RET_HEADS = 6
RET_QK_W = RET_HEADS * RET_DK
RET_V_W = RET_HEADS * RET_DV
S5_CH = 256
S5_P = 16
S5_G = 16
S5_N = 64
DIFF_HEADS = 8
DIFF_DH = 64
N_EXPERTS = 16
EXPERTS_PER_GROUP = 4
TOP_K = 2

LANES = 128
SUBLANES = 8
MXU_DIM = 256
ROW_TILE = 256
RET_CHUNK = 256
RET_STATE_UNROLL = 3
RET_OUT_UNROLL = 11
S5_CHUNK = 8
S5_HALVES = 2
MOE_ROWS = 256
DISPATCH_TILE = 512
ROW_WORDS = D_MODEL // 2
ROW_SUB = ROW_WORDS // LANES
ATT_TK = 768
ATT_SAFE_SHIFT = 48.0
VMEM_LIMIT = 48 * 1024 * 1024


def _cparams(sem, flags=None):
    return pltpu.CompilerParams(dimension_semantics=sem, vmem_limit_bytes=VMEM_LIMIT, flags=flags)


def _dot(a, b):
    return jnp.dot(a, b, preferred_element_type=F32)


def _dot_nt(a, b):
    return lax.dot_general(a, b, (((1,), (1,)), ((), ())), preferred_element_type=F32)


def _dot_tn(a, b):
    return lax.dot_general(a, b, (((0,), (0,)), ((), ())), preferred_element_type=F32)


def _split_bf16(x):
    hi = x.astype(BF16)
    lo = (x - hi.astype(F32)).astype(BF16)
    return hi, lo


def _dot3(a, b):
    ah, al = _split_bf16(a)
    bh, bl = _split_bf16(b)
    return _dot(ah, bh) + _dot(ah, bl) + _dot(al, bh)


def _dot3_nt(a, b):
    ah, al = _split_bf16(a)
    bh, bl = _split_bf16(b)
    return _dot_nt(ah, bh) + _dot_nt(ah, bl) + _dot_nt(al, bh)


def _sigmoid(x):
    return 1.0 / (1.0 + jnp.exp(-x))


def _silu(x):
    return x * _sigmoid(x)


def _pack_bf16_pairs(v):
    half = v.shape[1] // 2
    bits = lax.bitcast_convert_type(v.astype(BF16).astype(F32), jnp.uint32)
    return (bits[:, :half] >> 16) | (bits[:, half:] & jnp.uint32(0xFFFF0000))


def _unpack_bf16_pairs(p):
    lo = lax.bitcast_convert_type(p << 16, F32)
    hi = lax.bitcast_convert_type(p & jnp.uint32(0xFFFF0000), F32)
    return jnp.concatenate([lo, hi], axis=1)


def _row_shape(n_rows):
    return (n_rows // SUBLANES, ROW_SUB, SUBLANES, LANES)


def _row(ref, r):
    return ref.at[lax.shift_right_logical(r, 3), :, r & (SUBLANES - 1), :]


def _store_rows(ref, packed):
    rows = packed.shape[0]
    for j in range(ROW_SUB):
        ref[:, j] = packed[:, j * LANES:(j + 1) * LANES].reshape(rows // SUBLANES, SUBLANES, LANES)


def _load_rows(ref):
    rows = ref.shape[0] * SUBLANES
    return jnp.concatenate([ref[:, j].reshape(rows, LANES) for j in range(ROW_SUB)], axis=1)


def _gelu_tanh(x):
    c = math.sqrt(2.0 / math.pi)
    return 0.5 * x * (1.0 + jnp.tanh(c * (x + 0.044715 * (x * x * x))))


def _adaln_kernel(c_ref, w_ref, b_ref, o_ref):
    c = c_ref[...]
    o_ref[0] = _dot3(_silu(c), w_ref[0]) + b_ref[0]


def _adaln(c_pad, ada_w, ada_b):
    depth, d, n = ada_w.shape
    tn = 1536
    return pl.pallas_call(
        _adaln_kernel,
        grid=(depth, n // tn),
        in_specs=[
            pl.BlockSpec((SUBLANES, d), lambda i, j: (0, 0)),
            pl.BlockSpec((1, d, tn), lambda i, j: (i, 0, j)),
            pl.BlockSpec((1, 1, tn), lambda i, j: (i, 0, j)),
        ],
        out_specs=pl.BlockSpec((1, SUBLANES, tn), lambda i, j: (i, 0, j)),
        out_shape=jax.ShapeDtypeStruct((depth, SUBLANES, n), F32),
        compiler_params=_cparams(("parallel", "parallel")),
        name="adaln",
    )(c_pad, ada_w, ada_b.reshape(depth, 1, n))


def _mod_row(t, tiles_per_batch, ctx_tiles, nb):
    b = lax.div(t, tiles_per_batch)
    w = lax.rem(t, tiles_per_batch)
    return jnp.where(w < ctx_tiles, nb, b)


def _rope_block(a, cos, sin, half):
    lane = lax.broadcasted_iota(jnp.int32, a.shape, 1)
    first = lax.rem(lane, 2 * half) < half
    rot = jnp.where(first, pltpu.roll(a, LANES - half, 1), pltpu.roll(a, half, 1))
    return a * cos + rot * sin


def _chunk_perm(rows, transpose):
    per = rows // S5_CHUNK
    r = lax.broadcasted_iota(jnp.int32, (rows, rows), 0)
    c = lax.broadcasted_iota(jnp.int32, (rows, rows), 1)
    if transpose:
        r, c = c, r
    return (c == S5_CHUNK * lax.rem(r, per) + lax.div(r, per)).astype(BF16)


def _inproj_kernel(x_ref, mod_ref, w_ref, cos_ref, sin_ref, o_ref, *rest, tiles_per_batch,
                   ctx_tiles, nb, rope_tab, rope_half):
    d = x_ref.shape[1]
    n = o_ref.shape[1]
    r = _mod_row(pl.program_id(0), tiles_per_batch, ctx_tiles, nb)
    sh = mod_ref[0, pl.ds(r, 1), 0:d]
    sc = mod_ref[0, pl.ds(r, 1), d:2 * d]
    xm = (x_ref[...] * (1.0 + sc) + sh).astype(BF16)
    for j in range(n // MXU_DIM):
        acc = _dot(xm, w_ref[:, j * MXU_DIM:(j + 1) * MXU_DIM])
        parts = []
        for s in range(MXU_DIM // LANES):
            blk = acc[:, s * LANES:(s + 1) * LANES]
            tab = rope_tab[j * (MXU_DIM // LANES) + s]
            if tab is not None:
                blk = _rope_block(blk, cos_ref[tab], sin_ref[tab], rope_half)
            parts.append(blk)
        o_ref[:, j * MXU_DIM:(j + 1) * MXU_DIM] = jnp.concatenate(parts, axis=1).astype(BF16)
    if rest:
        u_ref, = rest
        rows = x_ref.shape[0]
        per = rows // S5_CHUNK
        u = _dot(xm, w_ref[:, n:n + S5_CH]).astype(BF16)
        up = _dot(_chunk_perm(rows, False), u).astype(BF16)
        for s in range(S5_CHUNK):
            for hf in range(S5_CH // LANES):
                u_ref[hf, :, s * LANES:(s + 1) * LANES] = up[s * per:(s + 1) * per,
                                                             hf * LANES:(hf + 1) * LANES]


def _inproj(x, mod, layer, w, cos, sin, rope_tab, rope_half, *, nb, t_len, ctx_len, s5_cols=0):
    nt, d = x.shape
    n = w.shape[1] - s5_cols
    tpb = t_len // ROW_TILE
    kern = functools.partial(_inproj_kernel, tiles_per_batch=tpb, ctx_tiles=ctx_len // ROW_TILE,
                             nb=nb, rope_tab=tuple(rope_tab), rope_half=rope_half)
    ntab = cos.shape[0]
    out_specs = [pl.BlockSpec((ROW_TILE, n), lambda t: (t, 0))]
    out_shape = [jax.ShapeDtypeStruct((nt, n), BF16)]
    if s5_cols:
        halves, per = s5_cols // LANES, ROW_TILE // S5_CHUNK
        out_specs.append(pl.BlockSpec((halves, per, S5_CHUNK * LANES), lambda t: (0, t, 0)))
        out_shape.append(jax.ShapeDtypeStruct((halves, nt // S5_CHUNK, S5_CHUNK * LANES), BF16))
    return pl.pallas_call(
        kern,
        grid=(nt // ROW_TILE,),
        in_specs=[
            pl.BlockSpec((ROW_TILE, d), lambda t: (t, 0)),
            pl.BlockSpec((1, SUBLANES, mod.shape[2]), lambda t: (layer, 0, 0)),
            pl.BlockSpec(w.shape, lambda t: (0, 0)),
            pl.BlockSpec((ntab, ROW_TILE, LANES), lambda t: (0, lax.rem(t, tpb), 0)),
            pl.BlockSpec((ntab, ROW_TILE, LANES), lambda t: (0, lax.rem(t, tpb), 0)),
        ],
        out_specs=out_specs,
        out_shape=out_shape,
        compiler_params=_cparams(("parallel",)),
        name="inproj%d" % layer,
    )(x, mod, w, cos, sin)


def _retention_kernel(lg_ref, qk_ref, v_ref, o_ref, sf_ref, sb_ref, *, nctx):
    c_len = RET_CHUNK
    t_len = qk_ref.shape[0]
    nc = t_len // c_len
    h = pl.program_id(1)
    lgf = lg_ref[0, h]
    lgb = lg_ref[1, h]
    ii = lax.broadcasted_iota(jnp.int32, (c_len, 1), 0).astype(F32)
    jj = lax.broadcasted_iota(jnp.int32, (1, c_len), 1).astype(F32)
    diff = ii - jj
    decay = jnp.where(diff >= 0.0, jnp.exp(lgf * jnp.maximum(diff, 0.0)),
                      jnp.exp(lgb * jnp.maximum(-diff, 0.0)))
    kdf = jnp.exp(lgf * (c_len - 1.0 - ii))
    kdb = jnp.exp(lgb * ii)
    qdf = jnp.exp(lgf * (ii + 1.0))
    qdb = jnp.exp(lgb * (c_len - ii))
    zrow = jnp.zeros((1, RET_DV), F32)
    gf_chunk = jnp.exp(zrow + lgf * c_len)
    gb_chunk = jnp.exp(zrow + lgb * c_len)

    def load(c):
        rows = pl.ds(pl.multiple_of(c * c_len, c_len), c_len)
        qk = qk_ref[rows, :].astype(F32)
        return qk[:, :RET_DK], qk[:, RET_DK:], v_ref[rows, :]

    def states(j, carry):
        sf, sb = carry
        cb = jnp.where(j < nctx, nctx - 1 - j, nc - 1 - (j - nctx))
        sf_ref[j] = sf
        sb_ref[cb] = sb
        _, kf, vf = load(j)
        _, kb, vb = load(cb)
        return (gf_chunk * sf + _dot_tn((kf * kdf).astype(BF16), vf),
                gb_chunk * sb + _dot_tn((kb * kdb).astype(BF16), vb))

    zero_state = jnp.zeros((RET_DK, RET_DV), F32)
    lax.fori_loop(0, nc, states, (zero_state, zero_state), unroll=RET_STATE_UNROLL)

    def out_chunk(c, carry):
        q, k, v = load(c)
        scores = _dot_nt(q.astype(BF16), k.astype(BF16)) * decay
        o = _dot(scores.astype(BF16), v)
        o = o + _dot((q * qdf).astype(BF16), sf_ref[c].astype(BF16))
        o = o + _dot((q * qdb).astype(BF16), sb_ref[c].astype(BF16))
        mu = jnp.mean(o, axis=-1, keepdims=True)
        oc = o - mu
        var = jnp.mean(oc * oc, axis=-1, keepdims=True)
        rows = pl.ds(pl.multiple_of(c * c_len, c_len), c_len)
        o_ref[rows, :] = (oc * lax.rsqrt(var + GN_EPS)).astype(BF16)
        return carry

    lax.fori_loop(0, nc, out_chunk, 0, unroll=RET_OUT_UNROLL)


def _retention(proj, log_gammas, *, nb, t_len, ctx_len):
    nt = proj.shape[0]
    nc = t_len // RET_CHUNK
    kern = functools.partial(_retention_kernel, nctx=ctx_len // RET_CHUNK)
    vcol0 = RET_HEADS
    return pl.pallas_call(
        kern,
        grid_spec=pltpu.PrefetchScalarGridSpec(
            num_scalar_prefetch=1,
            grid=(nb, RET_HEADS),
            in_specs=[
                pl.BlockSpec((t_len, LANES), lambda b, h, lg: (b, h)),
                pl.BlockSpec((t_len, LANES), lambda b, h, lg: (b, vcol0 + h)),
            ],
            out_specs=pl.BlockSpec((t_len, LANES), lambda b, h, lg: (b, h)),
            scratch_shapes=[pltpu.VMEM((nc, RET_DK, RET_DV), F32),
                            pltpu.VMEM((nc, RET_DK, RET_DV), F32)],
        ),
        out_shape=jax.ShapeDtypeStruct((nt, RET_V_W), BF16),
        compiler_params=_cparams(("parallel", "parallel")),
        name="retention",
    )(log_gammas, proj, proj)


def _s5_operators(lam_re, lam_im, log_dt, b_re, b_im, c_re, c_im, d_skip):
    tc = S5_CHUNK
    hp = lax.Precision.HIGHEST
    ks = jnp.arange(tc + 1, dtype=F32)
    pw, bbar, cm = [], [], []
    for direction in range(2):
        dt = jnp.exp(log_dt[direction].astype(F32))[:, None]
        lam = lax.complex(lam_re[direction].astype(F32), lam_im[direction].astype(F32))
        z = lam * dt
        p = jnp.exp(z[None] * ks[:, None, None])
        lam_bar = p[1]
        bb = ((lam_bar - 1.0) / lam)[..., None] * lax.complex(
            b_re[direction].astype(F32), b_im[direction].astype(F32))
        pw.append(p)
        bbar.append(bb)
        cm.append(lax.complex(c_re[direction].astype(F32), c_im[direction].astype(F32)))

    def lag_kernel(p, bb, c):
        return jnp.einsum('gpn,kgn,gnq->kgpq', c, p[:tc], bb, precision=hp).real

    kf = lag_kernel(pw[0], bbar[0], cm[0])
    kb = lag_kernel(pw[1], bbar[1], cm[1])
    k0 = kf[0] + kb[0] + jnp.eye(S5_P, dtype=F32)[None] * d_skip.astype(F32)[:, :, None]
    kcat = jnp.concatenate([kb[1:][::-1], k0[None], kf[1:]], axis=0)
    s_idx = jnp.arange(tc)[:, None]
    t_idx = jnp.arange(tc)[None, :]
    m5 = kcat[t_idx - s_idx + tc - 1]
    hg = S5_G // S5_HALVES
    eye = jnp.eye(hg, dtype=F32)
    split = lambda z, axis: z.reshape(z.shape[:axis] + (S5_HALVES, hg) + z.shape[axis + 1:])
    wide = tc * hg * S5_P
    intra = jnp.einsum('sthgpq,gk->hsgqtkp', split(m5, 2), eye).reshape(S5_HALVES, wide, wide)

    ef = pw[0][:tc][::-1][:, :, :, None] * bbar[0][None]
    eb = pw[1][:tc][:, :, :, None] * bbar[1][None]
    to_in = lambda e: jnp.einsum('shgnq,gk->hsgqkn', split(e, 1), eye).reshape(
        S5_HALVES, wide, hg * S5_N)
    w_in = jnp.concatenate([to_in(ef.real), to_in(ef.imag), to_in(eb.real), to_in(eb.imag)], axis=2)
    of = cm[0][None] * pw[0][1:][:, :, None, :]
    ob = cm[1][None] * pw[1][1:][::-1][:, :, None, :]
    to_out = lambda o: jnp.einsum('thgpn,gk->hkntgp', split(o, 1), eye).reshape(
        S5_HALVES, hg * S5_N, wide)
    w_out = jnp.concatenate([to_out(of.real), -to_out(of.imag), to_out(ob.real), -to_out(ob.imag)],
                            axis=1)
    a = jnp.stack([pw[0][tc].real, pw[0][tc].imag, pw[1][tc].real, pw[1][tc].imag], axis=0)
    a = a.reshape(4, S5_HALVES, hg * S5_N).transpose(1, 0, 2)
    return intra.astype(BF16), w_in.astype(BF16), w_out.astype(BF16), a


def _s5_kernel(x_ref, wi_ref, win_ref, wout_ref, a_ref, y_ref, st_ref, *, nctx):
    x = x_ref[0]
    nc = x.shape[0]
    w = a_ref.shape[2]
    st_ref[...] = _dot(x, win_ref[0])
    afr, afi, abr, abi = (a_ref[0, i:i + 1, :] for i in range(4))

    def step(j, carry):
        fr, fi, br, bi = carry
        rf = pl.ds(j, 1)
        rb = pl.ds(jnp.where(j < nctx, nctx - 1 - j, nc - 1 - (j - nctx)), 1)
        efr, efi = st_ref[rf, 0:w], st_ref[rf, w:2 * w]
        ebr, ebi = st_ref[rb, 2 * w:3 * w], st_ref[rb, 3 * w:4 * w]
        st_ref[rf, 0:w] = fr
        st_ref[rf, w:2 * w] = fi
        st_ref[rb, 2 * w:3 * w] = br
        st_ref[rb, 3 * w:4 * w] = bi
        return (afr * fr - afi * fi + efr, afr * fi + afi * fr + efi,
                abr * br - abi * bi + ebr, abr * bi + abi * br + ebi)

    z = jnp.zeros((1, w), F32)
    lax.fori_loop(0, nc, step, (z, z, z, z), unroll=4)
    y_ref[0] = (_dot(x, wi_ref[0]) + _dot(st_ref[...].astype(BF16), wout_ref[0])).astype(BF16)


def _s5(xc, intra, w_in, w_out, a, *, nb, nctx):
    halves, rows, wide = xc.shape
    nc = rows // nb
    kern = functools.partial(_s5_kernel, nctx=nctx)
    per_half = lambda arr: pl.BlockSpec((1,) + arr.shape[1:], lambda hf, b: (hf, 0, 0))
    return pl.pallas_call(
        kern,
        grid=(halves, nb),
        in_specs=[
            pl.BlockSpec((1, nc, wide), lambda hf, b: (hf, b, 0)),
            per_half(intra), per_half(w_in), per_half(w_out), per_half(a),
        ],
        out_specs=pl.BlockSpec((1, nc, wide), lambda hf, b: (hf, b, 0)),
        out_shape=jax.ShapeDtypeStruct(xc.shape, BF16),
        scratch_shapes=[pltpu.VMEM((nc, w_in.shape[2]), F32)],
        compiler_params=_cparams(("parallel", "parallel")),
        name="s5",
    )(xc, intra, w_in, w_out, a)


def _route(logits_t, bias):
    scores = _sigmoid(logits_t)
    biased = scores + bias
    s_rows = [scores[e:e + 1, :] for e in range(N_EXPERTS)]
    b_rows = [biased[e:e + 1, :] for e in range(N_EXPERTS)]
    n_groups = N_EXPERTS // EXPERTS_PER_GROUP
    best = None
    sel = None
    for g in range(n_groups):
        a, b, c, d = b_rows[4 * g:4 * g + 4]
        hi1, lo1 = jnp.maximum(a, b), jnp.minimum(a, b)
        hi2, lo2 = jnp.maximum(c, d), jnp.minimum(c, d)
        top1 = jnp.maximum(hi1, hi2)
        top2 = jnp.maximum(jnp.minimum(hi1, hi2), jnp.maximum(lo1, lo2))
        gs = top1 + top2
        if g == 0:
            best, sel = gs, jnp.zeros(gs.shape, jnp.int32)
        else:
            better = gs > best
            sel = jnp.where(better, g, sel)
            best = jnp.where(better, gs, best)
    neg = jnp.full(best.shape, -jnp.inf, F32)
    masked = [jnp.where(sel == (e // EXPERTS_PER_GROUP), b_rows[e], neg) for e in range(N_EXPERTS)]
    v1, i1, g1 = masked[0], jnp.zeros(best.shape, jnp.int32), s_rows[0]
    for e in range(1, N_EXPERTS):
        better = masked[e] > v1
        v1 = jnp.where(better, masked[e], v1)
        i1 = jnp.where(better, e, i1)
        g1 = jnp.where(better, s_rows[e], g1)
    v2, i2, g2 = neg, jnp.zeros(best.shape, jnp.int32), jnp.zeros(best.shape, F32)
    for e in range(N_EXPERTS):
        cand = jnp.where(i1 == e, neg, masked[e])
        better = cand > v2
        v2 = jnp.where(better, cand, v2)
        i2 = jnp.where(better, e, i2)
        g2 = jnp.where(better, s_rows[e], g2)
    tot = g1 + g2
    return jnp.concatenate([i1, i2], axis=0), jnp.concatenate([g1 / tot, g2 / tot], axis=0)


def _tail(x, o, mod_ref, r, lng, wr, rb, x1_ref, h2_ref, ei_ref, gt_ref):
    d = x.shape[1]
    g1 = mod_ref[0, pl.ds(r, 1), 2 * d:3 * d]
    sh2 = mod_ref[0, pl.ds(r, 1), 3 * d:4 * d]
    sc2 = mod_ref[0, pl.ds(r, 1), 4 * d:5 * d]
    y = ALPHA * x + g1 * o
    mu = jnp.mean(y, axis=-1, keepdims=True)
    yc = y - mu
    var = jnp.mean(yc * yc, axis=-1, keepdims=True)
    x1 = yc * lax.rsqrt(var + LN_EPS) * lng
    h2 = x1 * (1.0 + sc2) + sh2
    x1_ref[...] = x1
    _store_rows(h2_ref, _pack_bf16_pairs(h2))
    ei, gt = _route(_dot3_nt(wr, h2), rb)
    ei_ref[...] = ei
    gt_ref[...] = gt


def _merge0_kernel(r_ref, g_ref, s_ref, x_ref, mod_ref, wglu_ref, wout_ref, lng_ref, wr_ref,
                   rb_ref, x1_ref, h2_ref, ei_ref, gt_ref, *, tiles_per_batch, ctx_tiles, nb):
    r = _mod_row(pl.program_id(0), tiles_per_batch, ctx_tiles, nb)
    ret = r_ref[...].astype(F32) * _silu(g_ref[...].astype(F32))
    rows = x_ref.shape[0]
    sp = jnp.concatenate(
        [jnp.concatenate([s_ref[hf, :, s * LANES:(s + 1) * LANES] for hf in range(S5_HALVES)], axis=1)
         for s in range(S5_CHUNK)], axis=0)
    s5 = _dot(_chunk_perm(rows, True), sp)
    z = _dot(_gelu_tanh(s5).astype(BF16), wglu_ref[...])
    zz = z[:, :S5_CH] * _sigmoid(z[:, S5_CH:])
    o = _dot(ret.astype(BF16), wout_ref[0:RET_V_W, :]) + _dot(zz.astype(BF16), wout_ref[RET_V_W:, :])
    _tail(x_ref[...], o, mod_ref, r, lng_ref[...], wr_ref[...], rb_ref[...],
          x1_ref, h2_ref, ei_ref, gt_ref)


def _merge1_kernel(a_ref, x_ref, mod_ref, wout_ref, lng_ref, wr_ref, rb_ref,
                   x1_ref, h2_ref, ei_ref, gt_ref, *, tiles_per_batch):
    r = lax.div(pl.program_id(0), tiles_per_batch)
    o = _dot(a_ref[...], wout_ref[...])
    _tail(x_ref[...], o, mod_ref, r, lng_ref[...], wr_ref[...], rb_ref[...],
          x1_ref, h2_ref, ei_ref, gt_ref)


def _tail_outs(n_rows, d):
    shapes = (jax.ShapeDtypeStruct((n_rows, d), F32),
              jax.ShapeDtypeStruct(_row_shape(n_rows), jnp.uint32),
              jax.ShapeDtypeStruct((TOP_K, n_rows), jnp.int32),
              jax.ShapeDtypeStruct((TOP_K, n_rows), F32))
    specs = (pl.BlockSpec((ROW_TILE, d), lambda t: (t, 0)),
             pl.BlockSpec(_row_shape(ROW_TILE), lambda t: (t, 0, 0, 0)),
             pl.BlockSpec((TOP_K, ROW_TILE), lambda t: (0, t)),
             pl.BlockSpec((TOP_K, ROW_TILE), lambda t: (0, t)))
    return shapes, specs


def _merge0(ret, proj, s5y, x, mod, w_glu, w_out, lng, wr_t, rbias, *, nb, t_len, ctx_len):
    nt, d = x.shape
    tpb = t_len // ROW_TILE
    kern = functools.partial(_merge0_kernel, tiles_per_batch=tpb, ctx_tiles=ctx_len // ROW_TILE, nb=nb)
    shapes, specs = _tail_outs(nt, d)
    gcol = (2 * RET_QK_W + RET_V_W) // RET_V_W
    full = lambda a: pl.BlockSpec(a.shape, lambda t: (0,) * a.ndim)
    return pl.pallas_call(
        kern,
        grid=(nt // ROW_TILE,),
        in_specs=[
            pl.BlockSpec((ROW_TILE, RET_V_W), lambda t: (t, 0)),
            pl.BlockSpec((ROW_TILE, RET_V_W), lambda t: (t, gcol)),
            pl.BlockSpec((S5_HALVES, ROW_TILE // S5_CHUNK, S5_CHUNK * LANES), lambda t: (0, t, 0)),
            pl.BlockSpec((ROW_TILE, d), lambda t: (t, 0)),
            pl.BlockSpec((1, SUBLANES, mod.shape[2]), lambda t: (0, 0, 0)),
            full(w_glu), full(w_out), full(lng), full(wr_t), full(rbias),
        ],
        out_specs=specs,
        out_shape=shapes,
        compiler_params=_cparams(("parallel",)),
        name="merge0",
    )(ret, proj, s5y, x, mod, w_glu, w_out, lng, wr_t, rbias)


def _merge1(att, x, mod, layer, w_out, lng, wr_t, rbias, *, nb, l_len, t_len, ctx_len):
    n_lat, d = att.shape
    tpb = l_len // ROW_TILE
    tpb_t = t_len // ROW_TILE
    ctx_tiles = ctx_len // ROW_TILE
    kern = functools.partial(_merge1_kernel, tiles_per_batch=tpb)
    shapes, specs = _tail_outs(n_lat, d)
    full = lambda a: pl.BlockSpec(a.shape, lambda t: (0,) * a.ndim)
    xrow = lambda t: (lax.div(t, tpb) * tpb_t + ctx_tiles + lax.rem(t, tpb), 0)
    return pl.pallas_call(
        kern,
        grid=(n_lat // ROW_TILE,),
        in_specs=[
            pl.BlockSpec((ROW_TILE, d), lambda t: (t, 0)),
            pl.BlockSpec((ROW_TILE, d), xrow),
            pl.BlockSpec((1, SUBLANES, mod.shape[2]), lambda t: (layer, 0, 0)),
            full(w_out), full(lng), full(wr_t), full(rbias),
        ],
        out_specs=specs,
        out_shape=shapes,
        compiler_params=_cparams(("parallel",)),
        name="merge1",
    )(att, x, mod, w_out, lng, wr_t, rbias)


def _moe_plan(eidx):
    k, n = eidx.shape
    a = k * n
    e_flat = eidx.reshape(a)
    seg = TOP_K * ROW_TILE
    onehot = (e_flat[:, None] == jnp.arange(N_EXPERTS, dtype=jnp.int32)[None, :]).astype(F32)
    onehot = onehot.reshape(a // seg, seg, N_EXPERTS)
    tril = lambda m: jnp.tril(jnp.ones((m, m), F32))
    within = jnp.einsum('ij,tjk->tik', tril(seg), onehot)
    seg_total = within[:, -1, :]
    seg_end = jnp.sum(tril(a // seg)[:, :, None] * seg_total[None], axis=1)
    counts = seg_end[-1].astype(jnp.int32)
    csum = within + (seg_end - seg_total)[:, None, :]
    padded = (counts + MOE_ROWS - 1) // MOE_ROWS * MOE_ROWS
    pad_end = jnp.sum(jnp.tril(jnp.ones((N_EXPERTS, N_EXPERTS), jnp.int32)) * padded[None, :], axis=1)
    pad_start = pad_end - padded
    dest = jnp.sum(onehot * (csum - 1.0 + pad_start.astype(F32)[None, None, :]), axis=-1)
    dest = dest.reshape(a).astype(jnp.int32)
    n_blocks = -(-(a + N_EXPERTS * (MOE_ROWS - 1)) // MOE_ROWS)
    first_row = jnp.arange(n_blocks, dtype=jnp.int32) * MOE_ROWS
    block_expert = jnp.minimum(jnp.sum((pad_end[None, :] <= first_row[:, None]).astype(jnp.int32), axis=1),
                               N_EXPERTS - 1)
    return dest.reshape(k, n), block_expert, n_blocks


def _tile_rows_of(dest, tile):
    k, n = dest.shape
    return dest.reshape(k, n // tile, tile).transpose(1, 0, 2).reshape(n // tile, 1, k * tile)


def _dispatch_kernel(dest_ref, h_ref, xs_in_hbm, xs_hbm, sem):
    del xs_in_hbm
    row_tiles = h_ref.shape[0]
    rows = row_tiles * SUBLANES

    def start(i, c):
        for u in range(SUBLANES):
            for choice in range(TOP_K):
                dst = dest_ref[0, 0, choice * rows + i * SUBLANES + u]
                pltpu.make_async_copy(h_ref.at[i, :, u, :], _row(xs_hbm, dst), sem).start(priority=choice)
        return c

    lax.fori_loop(0, row_tiles, start, 0)
    for _ in range(TOP_K):
        pltpu.make_async_copy(h_ref, xs_hbm.at[pl.ds(0, row_tiles)], sem).wait()


def _dispatch(h, dest, n_rows):
    n = h.shape[0] * SUBLANES
    tile = DISPATCH_TILE if n % DISPATCH_TILE == 0 else ROW_TILE
    return pl.pallas_call(
        _dispatch_kernel,
        grid=(n // tile,),
        in_specs=[
            pl.BlockSpec((1, 1, TOP_K * tile), lambda t: (t, 0, 0), memory_space=pltpu.SMEM),
            pl.BlockSpec(_row_shape(tile), lambda t: (t, 0, 0, 0)),
            pl.BlockSpec(memory_space=pl.ANY),
        ],
        out_specs=pl.BlockSpec(memory_space=pl.ANY),
        out_shape=jax.ShapeDtypeStruct(_row_shape(n_rows), jnp.uint32),
        scratch_shapes=[pltpu.SemaphoreType.DMA],
        input_output_aliases={2: 0},
        compiler_params=_cparams(("arbitrary",)),
        name="moe_dispatch",
    )(_tile_rows_of(dest, tile), h, jnp.zeros(_row_shape(n_rows), jnp.uint32))


def _experts_kernel(be_ref, x_ref, wg_ref, wu_ref, wd_ref, o_ref):
    x = _unpack_bf16_pairs(_load_rows(x_ref)).astype(BF16)
    hg = _dot(x, wg_ref[0])
    hu = _dot(x, wu_ref[0])
    _store_rows(o_ref, _pack_bf16_pairs(_dot((_silu(hg) * hu).astype(BF16), wd_ref[0])))


def _experts(xs, block_expert, layer, wg, wu, wd):
    n_blocks = block_expert.shape[0]
    d, dff = wg.shape[2], wg.shape[3]
    rows_spec = pl.BlockSpec(_row_shape(MOE_ROWS), lambda i, be: (i, 0, 0, 0))
    return pl.pallas_call(
        _experts_kernel,
        grid_spec=pltpu.PrefetchScalarGridSpec(
            num_scalar_prefetch=1,
            grid=(n_blocks,),
            in_specs=[
                rows_spec,
                pl.BlockSpec((None, 1, d, dff), lambda i, be: (layer, be[i], 0, 0)),
                pl.BlockSpec((None, 1, d, dff), lambda i, be: (layer, be[i], 0, 0)),
                pl.BlockSpec((None, 1, dff, d), lambda i, be: (layer, be[i], 0, 0)),
            ],
            out_specs=rows_spec,
        ),
        out_shape=jax.ShapeDtypeStruct(xs.shape, jnp.uint32),
        compiler_params=_cparams(("parallel",)),
        name="moe_experts",
    )(block_expert, xs, wg, wu, wd)


def _combine_kernel(dcur_ref, dnxt_ref, x_ref, gt_ref, mod_ref, lng_ref, y_hbm, o_ref, ybuf, sem, *,
                    tiles_per_batch, ctx_tiles, nb):
    t = pl.program_id(0)
    last = pl.num_programs(0) - 1
    rows = x_ref.shape[0]
    n = TOP_K * rows
    d = x_ref.shape[1]
    slot = lax.rem(t, 2)

    row_tiles = rows // SUBLANES

    def start_all(idx_ref, s):
        def start(i, c):
            for choice in range(TOP_K):
                for u in range(SUBLANES):
                    j = choice * rows + i * SUBLANES + u
                    pltpu.make_async_copy(_row(y_hbm, idx_ref[0, 0, j]),
                                          ybuf.at[s, choice * row_tiles + i, :, u, :],
                                          sem.at[s]).start(priority=choice)
            return c

        lax.fori_loop(0, row_tiles, start, 0)

    @pl.when(t == 0)
    def _():
        start_all(dcur_ref, 0)

    @pl.when(t < last)
    def _():
        start_all(dnxt_ref, 1 - slot)

    pltpu.make_async_copy(ybuf.at[slot], ybuf.at[slot], sem.at[slot]).wait()
    r = _mod_row(t, tiles_per_batch, ctx_tiles, nb)
    gt = gt_ref[...]

    def choice_rows(choice):
        tiles = pl.ds(choice * row_tiles, row_tiles)
        return jnp.concatenate([ybuf[slot, tiles, j].reshape(rows, LANES) for j in range(ROW_SUB)], axis=1)

    y = (_unpack_bf16_pairs(choice_rows(0)) * gt[:, 0:1]
         + _unpack_bf16_pairs(choice_rows(1)) * gt[:, 1:2])
    g2 = mod_ref[0, pl.ds(r, 1), 5 * d:6 * d]
    z = ALPHA * x_ref[...] + g2 * y
    mu = jnp.mean(z, axis=-1, keepdims=True)
    zc = z - mu
    var = jnp.mean(zc * zc, axis=-1, keepdims=True)
    o_ref[...] = zc * lax.rsqrt(var + LN_EPS) * lng_ref[...]


def _combine(x1, dest, gates, mod, layer, lng, ys, *, tiles_per_batch, ctx_tiles, nb):
    n, d = x1.shape
    nt = n // ROW_TILE
    dest_t = _tile_rows_of(dest, ROW_TILE)
    kern = functools.partial(_combine_kernel, tiles_per_batch=tiles_per_batch, ctx_tiles=ctx_tiles, nb=nb)
    idx_spec = lambda f: pl.BlockSpec((1, 1, TOP_K * ROW_TILE), f, memory_space=pltpu.SMEM)
    return pl.pallas_call(
        kern,
        grid=(nt,),
        in_specs=[
            idx_spec(lambda t: (t, 0, 0)),
            idx_spec(lambda t: (jnp.minimum(t + 1, nt - 1), 0, 0)),
            pl.BlockSpec((ROW_TILE, d), lambda t: (t, 0)),
            pl.BlockSpec((ROW_TILE, TOP_K), lambda t: (t, 0)),
            pl.BlockSpec((1, SUBLANES, mod.shape[2]), lambda t: (layer, 0, 0)),
            pl.BlockSpec((1, d), lambda t: (0, 0)),
            pl.BlockSpec(memory_space=pl.ANY),
        ],
        out_specs=pl.BlockSpec((ROW_TILE, d), lambda t: (t, 0)),
        out_shape=jax.ShapeDtypeStruct((n, d), F32),
        scratch_shapes=[pltpu.VMEM((2, TOP_K * ROW_TILE // SUBLANES, ROW_SUB, SUBLANES, LANES), jnp.uint32),
                        pltpu.SemaphoreType.DMA((2,))],
        compiler_params=_cparams(("arbitrary",)),
        name="moe_combine%d" % layer,
    )(dest_t, dest_t, x1, gates.T, mod, lng, ys)


def _moe_layer(x1, h2, eidx, gates, mod, layer, lng, wg, wu, wd, *, tiles_per_batch, ctx_tiles, nb):
    dest, block_expert, n_blocks = _moe_plan(eidx)
    xs = _dispatch(h2, dest, n_blocks * MOE_ROWS)
    ys = _experts(xs, block_expert, layer, wg, wu, wd)
    return _combine(x1, dest, gates, mod, layer, lng, ys,
                    tiles_per_batch=tiles_per_batch, ctx_tiles=ctx_tiles, nb=nb)


def _half_norms(x):
    lane = lax.broadcasted_iota(jnp.int32, x.shape, 1)
    sq = x * x
    lo = jnp.sum(jnp.where(lane < DIFF_DH, sq, 0.0), axis=-1, keepdims=True)
    hi = jnp.sum(jnp.where(lane >= DIFF_DH, sq, 0.0), axis=-1, keepdims=True)
    return jnp.sqrt(lo), jnp.sqrt(hi)


def _attn_kernel(lam_ref, q_ref, k_ref, v_ref, g_ref, o_ref, vext, s_buf0, s_buf1,
                 p_buf0, p_buf1, corr_buf0, corr_buf1, m_buf, shift_buf, acc, *, out_scale, ctx_len):
    t_len = k_ref.shape[0]
    nk = t_len // ATT_TK
    dv = v_ref.shape[1]
    tq = acc.shape[1]
    nq = o_ref.shape[0] // tq
    n_tiles = nq * nk

    vext[:, 0:dv] = v_ref[...]
    vext[:, dv:2 * dv] = jnp.ones((t_len, dv), BF16)
    lam = lam_ref[0]
    gain = g_ref[...] * out_scale

    s_bufs, p_bufs, corr_bufs = (s_buf0, s_buf1), (p_buf0, p_buf1), (corr_buf0, corr_buf1)

    def key_rows(kj):
        return pl.ds(pl.multiple_of(kj * ATT_TK, ATT_TK), ATT_TK)

    def query_rows(qi):
        return pl.ds(pl.multiple_of(ctx_len + qi * tq, ROW_TILE), tq)

    def column_max(norms, carry):
        return tuple(jnp.maximum(c, jnp.max(n, axis=0, keepdims=True)) for n, c in zip(norms, carry))

    def key_norms(kj, carry):
        return column_max(_half_norms(k_ref[key_rows(kj), :].astype(F32)), carry)

    zero11 = jnp.zeros((1, 1), F32)
    kmax = lax.fori_loop(0, nk, key_norms, (zero11, zero11))

    def query_shifts(qi, carry):
        shifts = tuple(n * k for n, k in zip(_half_norms(q_ref[query_rows(qi), :].astype(F32)), kmax))
        shift_buf[qi, :, 0:1] = shifts[0]
        shift_buf[qi, :, 1:2] = shifts[1]
        return column_max(shifts, carry)

    worst = lax.fori_loop(0, nq, query_shifts, (zero11, zero11))
    bounded = jnp.max(jnp.maximum(worst[0], worst[1])) <= ATT_SAFE_SHIFT

    def advance(tile):
        qi, kj = tile
        wrap = kj + 1 == nk
        return jnp.where(wrap, qi + 1, qi), jnp.where(wrap, 0, kj + 1)

    def scores(tile, slot):
        qi, kj = tile
        q = q_ref[query_rows(qi), :]
        lane = lax.broadcasted_iota(jnp.int32, q.shape, 1)
        zero = jnp.zeros(q.shape, q.dtype)
        k = k_ref[key_rows(kj), :]
        s_bufs[slot][0] = _dot_nt(jnp.where(lane < DIFF_DH, q, zero), k)
        s_bufs[slot][1] = _dot_nt(jnp.where(lane >= DIFF_DH, q, zero), k)

    def numerators(tile, slot, online):
        qi, kj = tile
        for w in range(2):
            s = s_bufs[slot][w]
            if online:
                m_old = jnp.where(kj == 0, -jnp.inf, m_buf[w])
                m_new = jnp.maximum(m_old, jnp.max(s, axis=-1, keepdims=True))
                corr_bufs[slot][w] = jnp.exp2(m_old - m_new)
                m_buf[w] = m_new
            else:
                m_new = shift_buf[qi, :, w:w + 1]
            p_bufs[slot][w] = jnp.exp2(s - m_new).astype(BF16)

    def values(tile, slot, online):
        qi, kj = tile
        ve = vext[key_rows(kj), :]
        a = []
        for w in range(2):
            keep = corr_bufs[slot][w] if online else jnp.where(kj == 0, 0.0, 1.0)
            a.append(keep * acc[w] + _dot(p_bufs[slot][w], ve))
            acc[w] = a[w]
        o = a[0][:, 0:dv] / a[0][:, dv:2 * dv] - lam * (a[1][:, 0:dv] / a[1][:, dv:2 * dv])
        o = o * lax.rsqrt(jnp.mean(o * o, axis=-1, keepdims=True) + GN_EPS)
        o_ref[pl.ds(pl.multiple_of(qi * tq, tq), tq), :] = (o * gain).astype(BF16)

    def pipeline(online):
        def step(tiles, slot):
            a, b, c = tiles
            values(c, slot, online)
            scores(a, slot)
            numerators(b, 1 - slot, online)
            return advance(a), a, b

        acc[...] = jnp.zeros(acc.shape, F32)
        t0 = (jnp.int32(0), jnp.int32(0))
        t1 = advance(t0)
        scores(t0, 0)
        scores(t1, 1)
        numerators(t0, 0, online)

        def pair(_, tiles):
            return step(step(tiles, 0), 1)

        _, last, prev = lax.fori_loop(0, (n_tiles - 2) // 2, pair, (advance(t1), t1, t0))
        numerators(last, (n_tiles - 1) % 2, online)
        values(prev, n_tiles % 2, online)
        values(last, (n_tiles - 1) % 2, online)

    pl.when(bounded)(functools.partial(pipeline, False))
    pl.when(jnp.logical_not(bounded))(functools.partial(pipeline, True))


def _diff_attention(qkv, lam, subln_g, lambda_init, *, nb, l_len, t_len, ctx_len):
    d = D_MODEL
    tq = 2 * ROW_TILE
    dv = 2 * DIFF_DH
    nq = l_len // tq
    assert (nq * (t_len // ATT_TK)) % 2 == 0
    kern = functools.partial(_attn_kernel, out_scale=1.0 - lambda_init, ctx_len=ctx_len)
    return pl.pallas_call(
        kern,
        grid_spec=pltpu.PrefetchScalarGridSpec(
            num_scalar_prefetch=1,
            grid=(nb, DIFF_HEADS),
            in_specs=[
                pl.BlockSpec((t_len, LANES), lambda b, h, lam: (b, h)),
                pl.BlockSpec((t_len, LANES), lambda b, h, lam: (b, DIFF_HEADS + h)),
                pl.BlockSpec((t_len, LANES), lambda b, h, lam: (b, 2 * DIFF_HEADS + h)),
                pl.BlockSpec((1, LANES), lambda b, h, lam: (0, 0)),
            ],
            out_specs=pl.BlockSpec((l_len, LANES), lambda b, h, lam: (b, h)),
            scratch_shapes=[
                pltpu.VMEM((t_len, 2 * dv), BF16),
                pltpu.VMEM((2, tq, ATT_TK), F32), pltpu.VMEM((2, tq, ATT_TK), F32),
                pltpu.VMEM((2, tq, ATT_TK), BF16), pltpu.VMEM((2, tq, ATT_TK), BF16),
                pltpu.VMEM((2, tq, 1), F32), pltpu.VMEM((2, tq, 1), F32),
                pltpu.VMEM((2, tq, 1), F32), pltpu.VMEM((nq, tq, 2), F32),
                pltpu.VMEM((2, tq, 2 * dv), F32),
            ],
        ),
        out_shape=jax.ShapeDtypeStruct((nb * l_len, d), BF16),
        compiler_params=_cparams(("parallel", "parallel")),
        name="diff_attention",
    )(lam, qkv, qkv, qkv, subln_g.reshape(1, LANES).astype(F32))


def _ret_rope_tables(l_len, ctx_len):
    half = RET_DK // 2
    inv = ROPE_BASE ** (-jnp.arange(0, RET_DK, 2, dtype=F32) / RET_DK)
    ang = jnp.arange(l_len, dtype=F32)[:, None] * inv[None, :]
    ang = jnp.concatenate([jnp.zeros((ctx_len, half), F32), ang], axis=0)
    cos64 = jnp.concatenate([jnp.cos(ang), jnp.cos(ang)], axis=1)
    sin64 = jnp.concatenate([-jnp.sin(ang), jnp.sin(ang)], axis=1)
    kscale = RET_DK ** -0.5
    cos = jnp.concatenate([cos64, cos64 * kscale], axis=1)
    sin = jnp.concatenate([sin64, sin64 * kscale], axis=1)
    return cos[None], sin[None]


def _attn_rope_tables(l_len, ctx_len):
    quarter = DIFF_DH // 4
    inv = ROPE_BASE ** (-jnp.arange(0, DIFF_DH // 2, 2, dtype=F32) / (DIFF_DH // 2))
    pos = jnp.arange(l_len)
    ang_r = (pos // GRID_W).astype(F32)[:, None] * inv[None, :]
    ang_c = (pos % GRID_W).astype(F32)[:, None] * inv[None, :]
    pad = lambda a: jnp.concatenate([jnp.zeros((ctx_len, quarter), F32), a], axis=0)
    ang_r, ang_c = pad(ang_r), pad(ang_c)
    cos64 = jnp.concatenate([jnp.cos(ang_r)] * 2 + [jnp.cos(ang_c)] * 2, axis=1)
    sin64 = jnp.concatenate([-jnp.sin(ang_r), jnp.sin(ang_r), -jnp.sin(ang_c), jnp.sin(ang_c)], axis=1)
    cos = jnp.concatenate([cos64, cos64], axis=1)
    sin = jnp.concatenate([sin64, sin64], axis=1)
    qscale = DIFF_DH ** -0.5 * math.log2(math.e)
    return jnp.stack([cos * qscale, cos]), jnp.stack([sin * qscale, sin])


def kernel(x, c, ctx, c_ctx, ada_w, ada_b, ln_g, w_in_ab, ret_decay_logit, s5_lam_re, s5_lam_im,
           s5_log_dt, s5_b_re, s5_b_im, s5_c_re, s5_c_im, s5_d, s5_w_glu, w_out_ab, w_in_c,
           diff_lambda, diff_subln_g, w_out_c, router_w, router_bias, exp_w_gate, exp_w_up,
           exp_w_down):
    nb, l_len, d = x.shape
    ctx_len = ctx.shape[1]
    t_len = ctx_len + l_len
    nt = nb * t_len
    tpb = t_len // ROW_TILE
    ctx_tiles = ctx_len // ROW_TILE
    assert d == D_MODEL and nb < SUBLANES
    assert l_len % (2 * ROW_TILE) == 0 and ctx_len % ROW_TILE == 0 and t_len % ATT_TK == 0

    xt = jnp.concatenate([ctx, x], axis=1).reshape(nt, d)
    c_all = jnp.concatenate([c, c_ctx[None].astype(c.dtype)], axis=0)
    c_pad = jnp.zeros((SUBLANES, d), F32).at[:nb + 1].set(c_all)
    mod = _adaln(c_pad, ada_w, ada_b)

    wr_t = router_w.T
    rbias = router_bias.reshape(N_EXPERTS, 1).astype(F32)

    w0 = w_in_ab[0]
    q_w, k_w, v_w, g_w, u_w = jnp.split(w0, (RET_QK_W, 2 * RET_QK_W, 2 * RET_QK_W + RET_V_W,
                                             2 * RET_QK_W + 2 * RET_V_W), axis=1)
    qk_w = jnp.concatenate([q_w.reshape(d, RET_HEADS, RET_DK), k_w.reshape(d, RET_HEADS, RET_DK)],
                           axis=2).reshape(d, 2 * RET_QK_W)
    w0p = jnp.concatenate([qk_w, v_w, g_w, u_w], axis=1).astype(BF16)
    cos0, sin0 = _ret_rope_tables(l_len, ctx_len)
    rope_tab0 = [0] * RET_HEADS + [None] * ((w0p.shape[1] - S5_CH - 2 * RET_QK_W) // LANES)
    proj0, u5 = _inproj(xt, mod, 0, w0p, cos0, sin0, rope_tab0, RET_DK // 2,
                        nb=nb, t_len=t_len, ctx_len=ctx_len, s5_cols=S5_CH)

    log_gammas = jax.nn.log_sigmoid(ret_decay_logit[0].astype(F32))
    ret = _retention(proj0, log_gammas, nb=nb, t_len=t_len, ctx_len=ctx_len)

    s5_ops = _s5_operators(s5_lam_re[0], s5_lam_im[0], s5_log_dt[0], s5_b_re[0], s5_b_im[0],
                           s5_c_re[0], s5_c_im[0], s5_d[0])
    s5y = _s5(u5, *s5_ops, nb=nb, nctx=ctx_len // S5_CHUNK)

    x1, h2, eidx, gates = _merge0(ret, proj0, s5y, xt, mod, s5_w_glu[0].astype(BF16),
                                  w_out_ab[0].astype(BF16), ln_g[0, 0].reshape(1, d), wr_t, rbias,
                                  nb=nb, t_len=t_len, ctx_len=ctx_len)
    experts = (exp_w_gate.astype(BF16), exp_w_up.astype(BF16), exp_w_down.astype(BF16))
    x2 = _moe_layer(x1, h2, eidx, gates, mod, 0, ln_g[0, 1].reshape(1, d), *experts,
                    tiles_per_batch=tpb, ctx_tiles=ctx_tiles, nb=nb)

    cos1, sin1 = _attn_rope_tables(l_len, ctx_len)
    n_heads_cols = D_MODEL // LANES
    rope_tab1 = [0] * n_heads_cols + [1] * n_heads_cols + [None] * n_heads_cols
    qkv, = _inproj(x2, mod, 1, w_in_c[0].astype(BF16), cos1, sin1, rope_tab1, DIFF_DH // 4,
                   nb=nb, t_len=t_len, ctx_len=ctx_len)
    lf = diff_lambda[0].astype(F32)
    lambda_init = 0.8 - 0.6 * math.exp(-0.3 * 1)
    lam = (jnp.exp(jnp.sum(lf[0] * lf[1])) - jnp.exp(jnp.sum(lf[2] * lf[3])) + lambda_init).reshape(1)
    att = _diff_attention(qkv, lam, diff_subln_g[0], lambda_init,
                          nb=nb, l_len=l_len, t_len=t_len, ctx_len=ctx_len)
    x3, h3, eidx1, gates1 = _merge1(att, x2, mod, 1, w_out_c[0].astype(BF16), ln_g[1, 0].reshape(1, d),
                                    wr_t, rbias, nb=nb, l_len=l_len, t_len=t_len, ctx_len=ctx_len)
    out = _moe_layer(x3, h3, eidx1, gates1, mod, 1, ln_g[1, 1].reshape(1, d), *experts,
                     tiles_per_batch=l_len // ROW_TILE, ctx_tiles=0, nb=nb)
    return out.reshape(nb, l_len, d)
```

```python
import functools
import math

import jax
import jax.numpy as jnp
import numpy as np
from jax import lax
from jax.experimental import pallas as pl
from jax.experimental.pallas import tpu as pltpu

F32 = jnp.float32
BF16 = jnp.bfloat16

D_MODEL = 1024
DEPTH = 2
GRID_W = 64
ALPHA = (2.0 * DEPTH) ** 0.25
LN_EPS = 1e-5
GN_EPS = 1e-6
ROPE_BASE = 10000.0
RET_DK = 64
RET_DV = 128
RET_HEADS = 6
RET_QK_W = RET_HEADS * RET_DK
RET_V_W = RET_HEADS * RET_DV
S5_CH = 256
S5_P = 16
S5_G = 16
S5_N = 64
DIFF_HEADS = 8
DIFF_DH = 64
N_EXPERTS = 16
EXPERTS_PER_GROUP = 4
TOP_K = 2

LANES = 128
SUBLANES = 8
MXU_DIM = 256
ROW_TILE = 256
RET_CHUNK = 256
RET_STATE_UNROLL = 3
RET_OUT_UNROLL = 11
S5_CHUNK = 8
S5_HALVES = 2
MOE_ROWS = 256
DISPATCH_TILE = 512
ROW_WORDS = D_MODEL // 2
ROW_SUB = ROW_WORDS // LANES
ATT_TK = 768
ATT_SAFE_SHIFT = 48.0
VMEM_LIMIT = 48 * 1024 * 1024


def _cparams(sem, flags=None):
    return pltpu.CompilerParams(dimension_semantics=sem, vmem_limit_bytes=VMEM_LIMIT, flags=flags)


def _dot(a, b):
    return jnp.dot(a, b, preferred_element_type=F32)


def _dot_nt(a, b):
    return lax.dot_general(a, b, (((1,), (1,)), ((), ())), preferred_element_type=F32)


def _dot_tn(a, b):
    return lax.dot_general(a, b, (((0,), (0,)), ((), ())), preferred_element_type=F32)


def _split_bf16(x):
    hi = x.astype(BF16)
    lo = (x - hi.astype(F32)).astype(BF16)
    return hi, lo


def _dot3(a, b):
    ah, al = _split_bf16(a)
    bh, bl = _split_bf16(b)
    return _dot(ah, bh) + _dot(ah, bl) + _dot(al, bh)


def _dot3_nt(a, b):
    ah, al = _split_bf16(a)
    bh, bl = _split_bf16(b)
    return _dot_nt(ah, bh) + _dot_nt(ah, bl) + _dot_nt(al, bh)


def _sigmoid(x):
    return 1.0 / (1.0 + jnp.exp(-x))


def _silu(x):
    return x * _sigmoid(x)


def _pack_bf16_pairs(v):
    half = v.shape[1] // 2
    bits = lax.bitcast_convert_type(v.astype(BF16).astype(F32), jnp.uint32)
    return (bits[:, :half] >> 16) | (bits[:, half:] & jnp.uint32(0xFFFF0000))


def _unpack_bf16_pairs(p):
    lo = lax.bitcast_convert_type(p << 16, F32)
    hi = lax.bitcast_convert_type(p & jnp.uint32(0xFFFF0000), F32)
    return jnp.concatenate([lo, hi], axis=1)


def _row_shape(n_rows):
    return (n_rows // SUBLANES, ROW_SUB, SUBLANES, LANES)


def _row(ref, r):
    return ref.at[lax.shift_right_logical(r, 3), :, r & (SUBLANES - 1), :]


def _store_rows(ref, packed):
    rows = packed.shape[0]
    for j in range(ROW_SUB):
        ref[:, j] = packed[:, j * LANES:(j + 1) * LANES].reshape(rows // SUBLANES, SUBLANES, LANES)


def _load_rows(ref):
    rows = ref.shape[0] * SUBLANES
    return jnp.concatenate([ref[:, j].reshape(rows, LANES) for j in range(ROW_SUB)], axis=1)


def _gelu_tanh(x):
    c = math.sqrt(2.0 / math.pi)
    return 0.5 * x * (1.0 + jnp.tanh(c * (x + 0.044715 * (x * x * x))))


def _adaln_kernel(c_ref, w_ref, b_ref, o_ref):
    c = c_ref[...]
    o_ref[0] = _dot3(_silu(c), w_ref[0]) + b_ref[0]


def _adaln(c_pad, ada_w, ada_b):
    depth, d, n = ada_w.shape
    tn = 1536
    return pl.pallas_call(
        _adaln_kernel,
        grid=(depth, n // tn),
        in_specs=[
            pl.BlockSpec((SUBLANES, d), lambda i, j: (0, 0)),
            pl.BlockSpec((1, d, tn), lambda i, j: (i, 0, j)),
            pl.BlockSpec((1, 1, tn), lambda i, j: (i, 0, j)),
        ],
        out_specs=pl.BlockSpec((1, SUBLANES, tn), lambda i, j: (i, 0, j)),
        out_shape=jax.ShapeDtypeStruct((depth, SUBLANES, n), F32),
        compiler_params=_cparams(("parallel", "parallel")),
        name="adaln",
    )(c_pad, ada_w, ada_b.reshape(depth, 1, n))


def _mod_row(t, tiles_per_batch, ctx_tiles, nb):
    b = lax.div(t, tiles_per_batch)
    w = lax.rem(t, tiles_per_batch)
    return jnp.where(w < ctx_tiles, nb, b)


def _rope_block(a, cos, sin, half):
    lane = lax.broadcasted_iota(jnp.int32, a.shape, 1)
    first = lax.rem(lane, 2 * half) < half
    rot = jnp.where(first, pltpu.roll(a, LANES - half, 1), pltpu.roll(a, half, 1))
    return a * cos + rot * sin


def _chunk_perm(rows, transpose):
    per = rows // S5_CHUNK
    r = lax.broadcasted_iota(jnp.int32, (rows, rows), 0)
    c = lax.broadcasted_iota(jnp.int32, (rows, rows), 1)
    if transpose:
        r, c = c, r
    return (c == S5_CHUNK * lax.rem(r, per) + lax.div(r, per)).astype(BF16)


def _token_specs(d, tiles_per_batch, ctx_tiles):
    lat_tiles = tiles_per_batch - ctx_tiles
    b = lambda t: lax.div(t, tiles_per_batch)
    w = lambda t: lax.rem(t, tiles_per_batch)
    return (pl.BlockSpec((ROW_TILE, d), lambda t: (b(t) * lat_tiles + jnp.maximum(w(t) - ctx_tiles, 0), 0)),
            pl.BlockSpec((ROW_TILE, d), lambda t: (b(t) * ctx_tiles + jnp.minimum(w(t), ctx_tiles - 1), 0)))


def _token_tile(x_refs, tiles_per_batch, ctx_tiles):
    if len(x_refs) == 1:
        return x_refs[0][...]
    is_ctx = lax.rem(pl.program_id(0), tiles_per_batch) < ctx_tiles
    return jnp.where(is_ctx, x_refs[1][...], x_refs[0][...])


def _inproj_kernel(*refs, n_x, tiles_per_batch, ctx_tiles, nb, rope_tab, rope_half):
    x_refs, (mod_ref, w_ref, cos_ref, sin_ref, o_ref), rest = refs[:n_x], refs[n_x:n_x + 5], refs[n_x + 5:]
    d = w_ref.shape[0]
    n = o_ref.shape[1]
    r = _mod_row(pl.program_id(0), tiles_per_batch, ctx_tiles, nb)
    sh = mod_ref[0, pl.ds(r, 1), 0:d]
    sc = mod_ref[0, pl.ds(r, 1), d:2 * d]
    xm = (_token_tile(x_refs, tiles_per_batch, ctx_tiles) * (1.0 + sc) + sh).astype(BF16)
    for j in range(n // MXU_DIM):
        acc = _dot(xm, w_ref[:, j * MXU_DIM:(j + 1) * MXU_DIM])
        parts = []
        for s in range(MXU_DIM // LANES):
            blk = acc[:, s * LANES:(s + 1) * LANES]
            tab = rope_tab[j * (MXU_DIM // LANES) + s]
            if tab is not None:
                blk = _rope_block(blk, cos_ref[tab], sin_ref[tab], rope_half)
            parts.append(blk)
        o_ref[:, j * MXU_DIM:(j + 1) * MXU_DIM] = jnp.concatenate(parts, axis=1).astype(BF16)
    if rest:
        u_ref, = rest
        rows = o_ref.shape[0]
        per = rows // S5_CHUNK
        u = _dot(xm, w_ref[:, n:n + S5_CH]).astype(BF16)
        up = _dot(_chunk_perm(rows, False), u).astype(BF16)
        for s in range(S5_CHUNK):
            for hf in range(S5_CH // LANES):
                u_ref[hf, :, s * LANES:(s + 1) * LANES] = up[s * per:(s + 1) * per,
                                                             hf * LANES:(hf + 1) * LANES]


def _inproj(xs, mod, layer, w, cos, sin, rope_tab, rope_half, *, nb, t_len, ctx_len, s5_cols=0):
    nt, d = nb * t_len, w.shape[0]
    n = w.shape[1] - s5_cols
    tpb = t_len // ROW_TILE
    ctx_tiles = ctx_len // ROW_TILE
    kern = functools.partial(_inproj_kernel, n_x=len(xs), tiles_per_batch=tpb, ctx_tiles=ctx_tiles,
                             nb=nb, rope_tab=tuple(rope_tab), rope_half=rope_half)
    x_specs = ([pl.BlockSpec((ROW_TILE, d), lambda t: (t, 0))] if len(xs) == 1
               else list(_token_specs(d, tpb, ctx_tiles)))
    ntab = cos.shape[0]
    out_specs = [pl.BlockSpec((ROW_TILE, n), lambda t: (t, 0))]
    out_shape = [jax.ShapeDtypeStruct((nt, n), BF16)]
    if s5_cols:
        halves, per = s5_cols // LANES, ROW_TILE // S5_CHUNK
        out_specs.append(pl.BlockSpec((halves, per, S5_CHUNK * LANES), lambda t: (0, t, 0)))
        out_shape.append(jax.ShapeDtypeStruct((halves, nt // S5_CHUNK, S5_CHUNK * LANES), BF16))
    return pl.pallas_call(
        kern,
        grid=(nt // ROW_TILE,),
        in_specs=x_specs + [
            pl.BlockSpec((1, SUBLANES, mod.shape[2]), lambda t: (layer, 0, 0)),
            pl.BlockSpec(w.shape, lambda t: (0, 0)),
            pl.BlockSpec((ntab, ROW_TILE, LANES), lambda t: (0, lax.rem(t, tpb), 0)),
            pl.BlockSpec((ntab, ROW_TILE, LANES), lambda t: (0, lax.rem(t, tpb), 0)),
        ],
        out_specs=out_specs,
        out_shape=out_shape,
        compiler_params=_cparams(("parallel",)),
        name="inproj%d" % layer,
    )(*xs, mod, w, cos, sin)


def _retention_kernel(lg_ref, qk_ref, v_ref, o_ref, sf_ref, sb_ref, *, nctx):
    c_len = RET_CHUNK
    t_len = qk_ref.shape[0]
    nc = t_len // c_len
    h = pl.program_id(1)
    lgf = lg_ref[0, h]
    lgb = lg_ref[1, h]
    ii = lax.broadcasted_iota(jnp.int32, (c_len, 1), 0).astype(F32)
    jj = lax.broadcasted_iota(jnp.int32, (1, c_len), 1).astype(F32)
    diff = ii - jj
    decay = jnp.where(diff >= 0.0, jnp.exp(lgf * jnp.maximum(diff, 0.0)),
                      jnp.exp(lgb * jnp.maximum(-diff, 0.0)))
    kdf = jnp.exp(lgf * (c_len - 1.0 - ii))
    kdb = jnp.exp(lgb * ii)
    qdf = jnp.exp(lgf * (ii + 1.0))
    qdb = jnp.exp(lgb * (c_len - ii))
    zrow = jnp.zeros((1, RET_DV), F32)
    gf_chunk = jnp.exp(zrow + lgf * c_len)
    gb_chunk = jnp.exp(zrow + lgb * c_len)

    def load(c):
        rows = pl.ds(pl.multiple_of(c * c_len, c_len), c_len)
        qk = qk_ref[rows, :].astype(F32)
        return qk[:, :RET_DK], qk[:, RET_DK:], v_ref[rows, :]

    def states(j, carry):
        sf, sb = carry
        cb = jnp.where(j < nctx, nctx - 1 - j, nc - 1 - (j - nctx))
        sf_ref[j] = sf
        sb_ref[cb] = sb
        _, kf, vf = load(j)
        _, kb, vb = load(cb)
        return (gf_chunk * sf + _dot_tn((kf * kdf).astype(BF16), vf),
                gb_chunk * sb + _dot_tn((kb * kdb).astype(BF16), vb))

    zero_state = jnp.zeros((RET_DK, RET_DV), F32)
    lax.fori_loop(0, nc, states, (zero_state, zero_state), unroll=RET_STATE_UNROLL)

    def out_chunk(c, carry):
        q, k, v = load(c)
        scores = _dot_nt(q.astype(BF16), k.astype(BF16)) * decay
        o = _dot(scores.astype(BF16), v)
        o = o + _dot((q * qdf).astype(BF16), sf_ref[c].astype(BF16))
        o = o + _dot((q * qdb).astype(BF16), sb_ref[c].astype(BF16))
        mu = jnp.mean(o, axis=-1, keepdims=True)
        oc = o - mu
        var = jnp.mean(oc * oc, axis=-1, keepdims=True)
        rows = pl.ds(pl.multiple_of(c * c_len, c_len), c_len)
        o_ref[rows, :] = (oc * lax.rsqrt(var + GN_EPS)).astype(BF16)
        return carry

    lax.fori_loop(0, nc, out_chunk, 0, unroll=RET_OUT_UNROLL)


def _retention(proj, log_gammas, *, nb, t_len, ctx_len):
    nt = proj.shape[0]
    nc = t_len // RET_CHUNK
    kern = functools.partial(_retention_kernel, nctx=ctx_len // RET_CHUNK)
    vcol0 = RET_HEADS
    return pl.pallas_call(
        kern,
        grid_spec=pltpu.PrefetchScalarGridSpec(
            num_scalar_prefetch=1,
            grid=(nb, RET_HEADS),
            in_specs=[
                pl.BlockSpec((t_len, LANES), lambda b, h, lg: (b, h)),
                pl.BlockSpec((t_len, LANES), lambda b, h, lg: (b, vcol0 + h)),
            ],
            out_specs=pl.BlockSpec((t_len, LANES), lambda b, h, lg: (b, h)),
            scratch_shapes=[pltpu.VMEM((nc, RET_DK, RET_DV), F32),
                            pltpu.VMEM((nc, RET_DK, RET_DV), F32)],
        ),
        out_shape=jax.ShapeDtypeStruct((nt, RET_V_W), BF16),
        compiler_params=_cparams(("parallel", "parallel")),
        name="retention",
    )(log_gammas, proj, proj)


def _s5_operators(lam_re, lam_im, log_dt, b_re, b_im, c_re, c_im, d_skip):
    tc = S5_CHUNK
    hp = lax.Precision.HIGHEST
    ks = jnp.arange(tc + 1, dtype=F32)
    pw, bbar, cm = [], [], []
    for direction in range(2):
        dt = jnp.exp(log_dt[direction].astype(F32))[:, None]
        lam = lax.complex(lam_re[direction].astype(F32), lam_im[direction].astype(F32))
        z = lam * dt
        p = jnp.exp(z[None] * ks[:, None, None])
        lam_bar = p[1]
        bb = ((lam_bar - 1.0) / lam)[..., None] * lax.complex(
            b_re[direction].astype(F32), b_im[direction].astype(F32))
        pw.append(p)
        bbar.append(bb)
        cm.append(lax.complex(c_re[direction].astype(F32), c_im[direction].astype(F32)))

    def lag_kernel(p, bb, c):
        return jnp.einsum('gpn,kgn,gnq->kgpq', c, p[:tc], bb, precision=hp).real

    kf = lag_kernel(pw[0], bbar[0], cm[0])
    kb = lag_kernel(pw[1], bbar[1], cm[1])
    k0 = kf[0] + kb[0] + jnp.eye(S5_P, dtype=F32)[None] * d_skip.astype(F32)[:, :, None]
    kcat = jnp.concatenate([kb[1:][::-1], k0[None], kf[1:]], axis=0)
    s_idx = jnp.arange(tc)[:, None]
    t_idx = jnp.arange(tc)[None, :]
    m5 = kcat[t_idx - s_idx + tc - 1]
    hg = S5_G // S5_HALVES
    eye = jnp.eye(hg, dtype=F32)
    split = lambda z, axis: z.reshape(z.shape[:axis] + (S5_HALVES, hg) + z.shape[axis + 1:])
    wide = tc * hg * S5_P
    intra = jnp.einsum('sthgpq,gk->hsgqtkp', split(m5, 2), eye).reshape(S5_HALVES, wide, wide)

    ef = pw[0][:tc][::-1][:, :, :, None] * bbar[0][None]
    eb = pw[1][:tc][:, :, :, None] * bbar[1][None]
    to_in = lambda e: jnp.einsum('shgnq,gk->hsgqkn', split(e, 1), eye).reshape(
        S5_HALVES, wide, hg * S5_N)
    w_in = jnp.concatenate([to_in(ef.real), to_in(ef.imag), to_in(eb.real), to_in(eb.imag)], axis=2)
    of = cm[0][None] * pw[0][1:][:, :, None, :]
    ob = cm[1][None] * pw[1][1:][::-1][:, :, None, :]
    to_out = lambda o: jnp.einsum('thgpn,gk->hkntgp', split(o, 1), eye).reshape(
        S5_HALVES, hg * S5_N, wide)
    w_out = jnp.concatenate([to_out(of.real), -to_out(of.imag), to_out(ob.real), -to_out(ob.imag)],
                            axis=1)
    a = jnp.stack([pw[0][tc].real, pw[0][tc].imag, pw[1][tc].real, pw[1][tc].imag], axis=0)
    a = a.reshape(4, S5_HALVES, hg * S5_N).transpose(1, 0, 2)
    return intra.astype(BF16), w_in.astype(BF16), w_out.astype(BF16), a


def _s5_kernel(x_ref, wi_ref, win_ref, wout_ref, a_ref, y_ref, st_ref, *, nctx):
    x = x_ref[0]
    nc = x.shape[0]
    w = a_ref.shape[2]
    st_ref[...] = _dot(x, win_ref[0])
    afr, afi, abr, abi = (a_ref[0, i:i + 1, :] for i in range(4))

    def step(j, carry):
        fr, fi, br, bi = carry
        rf = pl.ds(j, 1)
        rb = pl.ds(jnp.where(j < nctx, nctx - 1 - j, nc - 1 - (j - nctx)), 1)
        efr, efi = st_ref[rf, 0:w], st_ref[rf, w:2 * w]
        ebr, ebi = st_ref[rb, 2 * w:3 * w], st_ref[rb, 3 * w:4 * w]
        st_ref[rf, 0:w] = fr
        st_ref[rf, w:2 * w] = fi
        st_ref[rb, 2 * w:3 * w] = br
        st_ref[rb, 3 * w:4 * w] = bi
        return (afr * fr - afi * fi + efr, afr * fi + afi * fr + efi,
                abr * br - abi * bi + ebr, abr * bi + abi * br + ebi)

    z = jnp.zeros((1, w), F32)
    lax.fori_loop(0, nc, step, (z, z, z, z), unroll=4)
    y_ref[0] = (_dot(x, wi_ref[0]) + _dot(st_ref[...].astype(BF16), wout_ref[0])).astype(BF16)


def _s5(xc, intra, w_in, w_out, a, *, nb, nctx):
    halves, rows, wide = xc.shape
    nc = rows // nb
    kern = functools.partial(_s5_kernel, nctx=nctx)
    per_half = lambda arr: pl.BlockSpec((1,) + arr.shape[1:], lambda hf, b: (hf, 0, 0))
    return pl.pallas_call(
        kern,
        grid=(halves, nb),
        in_specs=[
            pl.BlockSpec((1, nc, wide), lambda hf, b: (hf, b, 0)),
            per_half(intra), per_half(w_in), per_half(w_out), per_half(a),
        ],
        out_specs=pl.BlockSpec((1, nc, wide), lambda hf, b: (hf, b, 0)),
        out_shape=jax.ShapeDtypeStruct(xc.shape, BF16),
        scratch_shapes=[pltpu.VMEM((nc, w_in.shape[2]), F32)],
        compiler_params=_cparams(("parallel", "parallel")),
        name="s5",
    )(xc, intra, w_in, w_out, a)


def _route(logits_t, bias):
    scores = _sigmoid(logits_t)
    biased = scores + bias
    s_rows = [scores[e:e + 1, :] for e in range(N_EXPERTS)]
    b_rows = [biased[e:e + 1, :] for e in range(N_EXPERTS)]
    n_groups = N_EXPERTS // EXPERTS_PER_GROUP
    best = None
    sel = None
    for g in range(n_groups):
        a, b, c, d = b_rows[4 * g:4 * g + 4]
        hi1, lo1 = jnp.maximum(a, b), jnp.minimum(a, b)
        hi2, lo2 = jnp.maximum(c, d), jnp.minimum(c, d)
        top1 = jnp.maximum(hi1, hi2)
        top2 = jnp.maximum(jnp.minimum(hi1, hi2), jnp.maximum(lo1, lo2))
        gs = top1 + top2
        if g == 0:
            best, sel = gs, jnp.zeros(gs.shape, jnp.int32)
        else:
            better = gs > best
            sel = jnp.where(better, g, sel)
            best = jnp.where(better, gs, best)
    neg = jnp.full(best.shape, -jnp.inf, F32)
    masked = [jnp.where(sel == (e // EXPERTS_PER_GROUP), b_rows[e], neg) for e in range(N_EXPERTS)]
    v1, i1, g1 = masked[0], jnp.zeros(best.shape, jnp.int32), s_rows[0]
    for e in range(1, N_EXPERTS):
        better = masked[e] > v1
        v1 = jnp.where(better, masked[e], v1)
        i1 = jnp.where(better, e, i1)
        g1 = jnp.where(better, s_rows[e], g1)
    v2, i2, g2 = neg, jnp.zeros(best.shape, jnp.int32), jnp.zeros(best.shape, F32)
    for e in range(N_EXPERTS):
        cand = jnp.where(i1 == e, neg, masked[e])
        better = cand > v2
        v2 = jnp.where(better, cand, v2)
        i2 = jnp.where(better, e, i2)
        g2 = jnp.where(better, s_rows[e], g2)
    tot = g1 + g2
    return jnp.concatenate([i1, i2], axis=0), jnp.concatenate([g1 / tot, g2 / tot], axis=0)


def _tail(x, o, mod_ref, r, lng, wr, rb, x1_ref, h2_ref, ei_ref, gt_ref):
    d = x.shape[1]
    g1 = mod_ref[0, pl.ds(r, 1), 2 * d:3 * d]
    sh2 = mod_ref[0, pl.ds(r, 1), 3 * d:4 * d]
    sc2 = mod_ref[0, pl.ds(r, 1), 4 * d:5 * d]
    y = ALPHA * x + g1 * o
    mu = jnp.mean(y, axis=-1, keepdims=True)
    yc = y - mu
    var = jnp.mean(yc * yc, axis=-1, keepdims=True)
    x1 = yc * lax.rsqrt(var + LN_EPS) * lng
    h2 = x1 * (1.0 + sc2) + sh2
    x1_ref[...] = x1
    _store_rows(h2_ref, _pack_bf16_pairs(h2))
    ei, gt = _route(_dot3_nt(wr, h2), rb)
    ei_ref[...] = ei
    gt_ref[...] = gt


def _merge0_kernel(r_ref, g_ref, s_ref, x_ref, xc_ref, mod_ref, wglu_ref, wout_ref, lng_ref, wr_ref,
                   rb_ref, x1_ref, h2_ref, ei_ref, gt_ref, *, tiles_per_batch, ctx_tiles, nb):
    r = _mod_row(pl.program_id(0), tiles_per_batch, ctx_tiles, nb)
    ret = r_ref[...].astype(F32) * _silu(g_ref[...].astype(F32))
    rows = x_ref.shape[0]
    x = _token_tile((x_ref, xc_ref), tiles_per_batch, ctx_tiles)
    sp = jnp.concatenate(
        [jnp.concatenate([s_ref[hf, :, s * LANES:(s + 1) * LANES] for hf in range(S5_HALVES)], axis=1)
         for s in range(S5_CHUNK)], axis=0)
    s5 = _dot(_chunk_perm(rows, True), sp)
    z = _dot(_gelu_tanh(s5).astype(BF16), wglu_ref[...])
    zz = z[:, :S5_CH] * _sigmoid(z[:, S5_CH:])
    o = _dot(ret.astype(BF16), wout_ref[0:RET_V_W, :]) + _dot(zz.astype(BF16), wout_ref[RET_V_W:, :])
    _tail(x, o, mod_ref, r, lng_ref[...], wr_ref[...], rb_ref[...],
          x1_ref, h2_ref, ei_ref, gt_ref)


def _merge1_kernel(a_ref, x_ref, mod_ref, wout_ref, lng_ref, wr_ref, rb_ref,
                   x1_ref, h2_ref, ei_ref, gt_ref, *, tiles_per_batch):
    r = lax.div(pl.program_id(0), tiles_per_batch)
    o = _dot(a_ref[...], wout_ref[...])
    _tail(x_ref[...], o, mod_ref, r, lng_ref[...], wr_ref[...], rb_ref[...],
          x1_ref, h2_ref, ei_ref, gt_ref)


def _tail_outs(n_rows, d):
    shapes = (jax.ShapeDtypeStruct((n_rows, d), F32),
              jax.ShapeDtypeStruct(_row_shape(n_rows), jnp.uint32),
              jax.ShapeDtypeStruct((TOP_K, n_rows), jnp.int32),
              jax.ShapeDtypeStruct((TOP_K, n_rows), F32))
    specs = (pl.BlockSpec((ROW_TILE, d), lambda t: (t, 0)),
             pl.BlockSpec(_row_shape(ROW_TILE), lambda t: (t, 0, 0, 0)),
             pl.BlockSpec((TOP_K, ROW_TILE), lambda t: (0, t)),
             pl.BlockSpec((TOP_K, ROW_TILE), lambda t: (0, t)))
    return shapes, specs


def _merge0(ret, proj, s5y, x_lat, x_ctx, mod, w_glu, w_out, lng, wr_t, rbias, *, nb, t_len, ctx_len):
    nt, d = nb * t_len, x_lat.shape[1]
    tpb = t_len // ROW_TILE
    ctx_tiles = ctx_len // ROW_TILE
    kern = functools.partial(_merge0_kernel, tiles_per_batch=tpb, ctx_tiles=ctx_tiles, nb=nb)
    shapes, specs = _tail_outs(nt, d)
    gcol = (2 * RET_QK_W + RET_V_W) // RET_V_W
    full = lambda a: pl.BlockSpec(a.shape, lambda t: (0,) * a.ndim)
    return pl.pallas_call(
        kern,
        grid=(nt // ROW_TILE,),
        in_specs=[
            pl.BlockSpec((ROW_TILE, RET_V_W), lambda t: (t, 0)),
            pl.BlockSpec((ROW_TILE, RET_V_W), lambda t: (t, gcol)),
            pl.BlockSpec((S5_HALVES, ROW_TILE // S5_CHUNK, S5_CHUNK * LANES), lambda t: (0, t, 0)),
            *_token_specs(d, tpb, ctx_tiles),
            pl.BlockSpec((1, SUBLANES, mod.shape[2]), lambda t: (0, 0, 0)),
            full(w_glu), full(w_out), full(lng), full(wr_t), full(rbias),
        ],
        out_specs=specs,
        out_shape=shapes,
        compiler_params=_cparams(("parallel",)),
        name="merge0",
    )(ret, proj, s5y, x_lat, x_ctx, mod, w_glu, w_out, lng, wr_t, rbias)


def _merge1(att, x, mod, layer, w_out, lng, wr_t, rbias, *, nb, l_len, t_len, ctx_len):
    n_lat, d = att.shape
    tpb = l_len // ROW_TILE
    tpb_t = t_len // ROW_TILE
    ctx_tiles = ctx_len // ROW_TILE
    kern = functools.partial(_merge1_kernel, tiles_per_batch=tpb)
    shapes, specs = _tail_outs(n_lat, d)
    full = lambda a: pl.BlockSpec(a.shape, lambda t: (0,) * a.ndim)
    xrow = lambda t: (lax.div(t, tpb) * tpb_t + ctx_tiles + lax.rem(t, tpb), 0)
    return pl.pallas_call(
        kern,
        grid=(n_lat // ROW_TILE,),
        in_specs=[
            pl.BlockSpec((ROW_TILE, d), lambda t: (t, 0)),
            pl.BlockSpec((ROW_TILE, d), xrow),
            pl.BlockSpec((1, SUBLANES, mod.shape[2]), lambda t: (layer, 0, 0)),
            full(w_out), full(lng), full(wr_t), full(rbias),
        ],
        out_specs=specs,
        out_shape=shapes,
        compiler_params=_cparams(("parallel",)),
        name="merge1",
    )(att, x, mod, w_out, lng, wr_t, rbias)


def _moe_plan(eidx):
    k, n = eidx.shape
    a = k * n
    e_flat = eidx.reshape(a)
    seg = TOP_K * ROW_TILE
    onehot = (e_flat[:, None] == jnp.arange(N_EXPERTS, dtype=jnp.int32)[None, :]).astype(F32)
    onehot = onehot.reshape(a // seg, seg, N_EXPERTS)
    tril = lambda m: jnp.tril(jnp.ones((m, m), F32))
    within = jnp.einsum('ij,tjk->tik', tril(seg), onehot)
    seg_total = within[:, -1, :]
    seg_end = jnp.sum(tril(a // seg)[:, :, None] * seg_total[None], axis=1)
    counts = seg_end[-1].astype(jnp.int32)
    csum = within + (seg_end - seg_total)[:, None, :]
    padded = (counts + MOE_ROWS - 1) // MOE_ROWS * MOE_ROWS
    pad_end = jnp.sum(jnp.tril(jnp.ones((N_EXPERTS, N_EXPERTS), jnp.int32)) * padded[None, :], axis=1)
    pad_start = pad_end - padded
    dest = jnp.sum(onehot * (csum - 1.0 + pad_start.astype(F32)[None, None, :]), axis=-1)
    dest = dest.reshape(a).astype(jnp.int32)
    n_blocks = -(-(a + N_EXPERTS * (MOE_ROWS - 1)) // MOE_ROWS)
    first_row = jnp.arange(n_blocks, dtype=jnp.int32) * MOE_ROWS
    block_expert = jnp.minimum(jnp.sum((pad_end[None, :] <= first_row[:, None]).astype(jnp.int32), axis=1),
                               N_EXPERTS - 1)
    return dest.reshape(k, n), block_expert, n_blocks


def _tile_rows_of(dest, tile):
    k, n = dest.shape
    return dest.reshape(k, n // tile, tile).transpose(1, 0, 2).reshape(n // tile, 1, k * tile)


def _dispatch_kernel(dest_ref, h_ref, xs_in_hbm, xs_hbm, sem):
    del xs_in_hbm
    row_tiles = h_ref.shape[0]
    rows = row_tiles * SUBLANES

    def start(i, c):
        for u in range(SUBLANES):
            for choice in range(TOP_K):
                dst = dest_ref[0, 0, choice * rows + i * SUBLANES + u]
                pltpu.make_async_copy(h_ref.at[i, :, u, :], _row(xs_hbm, dst), sem).start(priority=choice)
        return c

    lax.fori_loop(0, row_tiles, start, 0)
    for _ in range(TOP_K):
        pltpu.make_async_copy(h_ref, xs_hbm.at[pl.ds(0, row_tiles)], sem).wait()


def _dispatch(h, dest, n_rows):
    n = h.shape[0] * SUBLANES
    tile = DISPATCH_TILE if n % DISPATCH_TILE == 0 else ROW_TILE
    return pl.pallas_call(
        _dispatch_kernel,
        grid=(n // tile,),
        in_specs=[
            pl.BlockSpec((1, 1, TOP_K * tile), lambda t: (t, 0, 0), memory_space=pltpu.SMEM),
            pl.BlockSpec(_row_shape(tile), lambda t: (t, 0, 0, 0)),
            pl.BlockSpec(memory_space=pl.ANY),
        ],
        out_specs=pl.BlockSpec(memory_space=pl.ANY),
        out_shape=jax.ShapeDtypeStruct(_row_shape(n_rows), jnp.uint32),
        scratch_shapes=[pltpu.SemaphoreType.DMA],
        input_output_aliases={2: 0},
        compiler_params=_cparams(("arbitrary",)),
        name="moe_dispatch",
    )(_tile_rows_of(dest, tile), h, jnp.zeros(_row_shape(n_rows), jnp.uint32))


def _experts_kernel(be_ref, x_ref, wg_ref, wu_ref, wd_ref, o_ref):
    x = _unpack_bf16_pairs(_load_rows(x_ref)).astype(BF16)
    hg = _dot(x, wg_ref[0])
    hu = _dot(x, wu_ref[0])
    _store_rows(o_ref, _pack_bf16_pairs(_dot((_silu(hg) * hu).astype(BF16), wd_ref[0])))


def _experts(xs, block_expert, layer, wg, wu, wd):
    n_blocks = block_expert.shape[0]
    d, dff = wg.shape[2], wg.shape[3]
    rows_spec = pl.BlockSpec(_row_shape(MOE_ROWS), lambda i, be: (i, 0, 0, 0))
    return pl.pallas_call(
        _experts_kernel,
        grid_spec=pltpu.PrefetchScalarGridSpec(
            num_scalar_prefetch=1,
            grid=(n_blocks,),
            in_specs=[
                rows_spec,
                pl.BlockSpec((None, 1, d, dff), lambda i, be: (layer, be[i], 0, 0)),
                pl.BlockSpec((None, 1, d, dff), lambda i, be: (layer, be[i], 0, 0)),
                pl.BlockSpec((None, 1, dff, d), lambda i, be: (layer, be[i], 0, 0)),
            ],
            out_specs=rows_spec,
        ),
        out_shape=jax.ShapeDtypeStruct(xs.shape, jnp.uint32),
        compiler_params=_cparams(("parallel",)),
        name="moe_experts",
    )(block_expert, xs, wg, wu, wd)


def _combine_kernel(dcur_ref, dnxt_ref, x_ref, gt_ref, mod_ref, lng_ref, y_hbm, o_ref, ybuf, sem, *,
                    tiles_per_batch, ctx_tiles, nb):
    t = pl.program_id(0)
    last = pl.num_programs(0) - 1
    rows = x_ref.shape[0]
    n = TOP_K * rows
    d = x_ref.shape[1]
    slot = lax.rem(t, 2)

    row_tiles = rows // SUBLANES

    def start_all(idx_ref, s):
        def start(i, c):
            for choice in range(TOP_K):
                for u in range(SUBLANES):
                    j = choice * rows + i * SUBLANES + u
                    pltpu.make_async_copy(_row(y_hbm, idx_ref[0, 0, j]),
                                          ybuf.at[s, choice * row_tiles + i, :, u, :],
                                          sem.at[s]).start(priority=choice)
            return c

        lax.fori_loop(0, row_tiles, start, 0)

    @pl.when(t == 0)
    def _():
        start_all(dcur_ref, 0)

    @pl.when(t < last)
    def _():
        start_all(dnxt_ref, 1 - slot)

    pltpu.make_async_copy(ybuf.at[slot], ybuf.at[slot], sem.at[slot]).wait()
    r = _mod_row(t, tiles_per_batch, ctx_tiles, nb)
    gt = gt_ref[...]

    def choice_rows(choice):
        tiles = pl.ds(choice * row_tiles, row_tiles)
        return jnp.concatenate([ybuf[slot, tiles, j].reshape(rows, LANES) for j in range(ROW_SUB)], axis=1)

    y = (_unpack_bf16_pairs(choice_rows(0)) * gt[:, 0:1]
         + _unpack_bf16_pairs(choice_rows(1)) * gt[:, 1:2])
    g2 = mod_ref[0, pl.ds(r, 1), 5 * d:6 * d]
    z = ALPHA * x_ref[...] + g2 * y
    mu = jnp.mean(z, axis=-1, keepdims=True)
    zc = z - mu
    var = jnp.mean(zc * zc, axis=-1, keepdims=True)
    o_ref[...] = zc * lax.rsqrt(var + LN_EPS) * lng_ref[...]


def _combine(x1, dest, gates, mod, layer, lng, ys, *, tiles_per_batch, ctx_tiles, nb):
    n, d = x1.shape
    nt = n // ROW_TILE
    dest_t = _tile_rows_of(dest, ROW_TILE)
    kern = functools.partial(_combine_kernel, tiles_per_batch=tiles_per_batch, ctx_tiles=ctx_tiles, nb=nb)
    idx_spec = lambda f: pl.BlockSpec((1, 1, TOP_K * ROW_TILE), f, memory_space=pltpu.SMEM)
    return pl.pallas_call(
        kern,
        grid=(nt,),
        in_specs=[
            idx_spec(lambda t: (t, 0, 0)),
            idx_spec(lambda t: (jnp.minimum(t + 1, nt - 1), 0, 0)),
            pl.BlockSpec((ROW_TILE, d), lambda t: (t, 0)),
            pl.BlockSpec((ROW_TILE, TOP_K), lambda t: (t, 0)),
            pl.BlockSpec((1, SUBLANES, mod.shape[2]), lambda t: (layer, 0, 0)),
            pl.BlockSpec((1, d), lambda t: (0, 0)),
            pl.BlockSpec(memory_space=pl.ANY),
        ],
        out_specs=pl.BlockSpec((ROW_TILE, d), lambda t: (t, 0)),
        out_shape=jax.ShapeDtypeStruct((n, d), F32),
        scratch_shapes=[pltpu.VMEM((2, TOP_K * ROW_TILE // SUBLANES, ROW_SUB, SUBLANES, LANES), jnp.uint32),
                        pltpu.SemaphoreType.DMA((2,))],
        compiler_params=_cparams(("arbitrary",)),
        name="moe_combine%d" % layer,
    )(dest_t, dest_t, x1, gates.T, mod, lng, ys)


def _moe_layer(x1, h2, eidx, gates, mod, layer, lng, wg, wu, wd, *, tiles_per_batch, ctx_tiles, nb):
    dest, block_expert, n_blocks = _moe_plan(eidx)
    xs = _dispatch(h2, dest, n_blocks * MOE_ROWS)
    ys = _experts(xs, block_expert, layer, wg, wu, wd)
    return _combine(x1, dest, gates, mod, layer, lng, ys,
                    tiles_per_batch=tiles_per_batch, ctx_tiles=ctx_tiles, nb=nb)


def _half_norms(x):
    lane = lax.broadcasted_iota(jnp.int32, x.shape, 1)
    sq = x * x
    lo = jnp.sum(jnp.where(lane < DIFF_DH, sq, 0.0), axis=-1, keepdims=True)
    hi = jnp.sum(jnp.where(lane >= DIFF_DH, sq, 0.0), axis=-1, keepdims=True)
    return jnp.sqrt(lo), jnp.sqrt(hi)


def _attn_kernel(lam_ref, q_ref, k_ref, v_ref, g_ref, o_ref, vext, s_buf0, s_buf1,
                 p_buf0, p_buf1, corr_buf0, corr_buf1, m_buf, shift_buf, acc, *, out_scale, ctx_len):
    t_len = k_ref.shape[0]
    nk = t_len // ATT_TK
    dv = v_ref.shape[1]
    tq = acc.shape[1]
    nq = o_ref.shape[0] // tq
    n_tiles = nq * nk

    vext[:, 0:dv] = v_ref[...]
    vext[:, dv:2 * dv] = jnp.ones((t_len, dv), BF16)
    lam = lam_ref[0]
    gain = g_ref[...] * out_scale

    s_bufs, p_bufs, corr_bufs = (s_buf0, s_buf1), (p_buf0, p_buf1), (corr_buf0, corr_buf1)

    def key_rows(kj):
        return pl.ds(pl.multiple_of(kj * ATT_TK, ATT_TK), ATT_TK)

    def query_rows(qi):
        return pl.ds(pl.multiple_of(ctx_len + qi * tq, ROW_TILE), tq)

    def column_max(norms, carry):
        return tuple(jnp.maximum(c, jnp.max(n, axis=0, keepdims=True)) for n, c in zip(norms, carry))

    def key_norms(kj, carry):
        return column_max(_half_norms(k_ref[key_rows(kj), :].astype(F32)), carry)

    zero11 = jnp.zeros((1, 1), F32)
    kmax = lax.fori_loop(0, nk, key_norms, (zero11, zero11))

    def query_shifts(qi, carry):
        shifts = tuple(n * k for n, k in zip(_half_norms(q_ref[query_rows(qi), :].astype(F32)), kmax))
        shift_buf[qi, :, 0:1] = shifts[0]
        shift_buf[qi, :, 1:2] = shifts[1]
        return column_max(shifts, carry)

    worst = lax.fori_loop(0, nq, query_shifts, (zero11, zero11))
    bounded = jnp.max(jnp.maximum(worst[0], worst[1])) <= ATT_SAFE_SHIFT

    def advance(tile):
        qi, kj = tile
        wrap = kj + 1 == nk
        return jnp.where(wrap, qi + 1, qi), jnp.where(wrap, 0, kj + 1)

    def scores(tile, slot):
        qi, kj = tile
        q = q_ref[query_rows(qi), :]
        lane = lax.broadcasted_iota(jnp.int32, q.shape, 1)
        zero = jnp.zeros(q.shape, q.dtype)
        k = k_ref[key_rows(kj), :]
        s_bufs[slot][0] = _dot_nt(jnp.where(lane < DIFF_DH, q, zero), k)
        s_bufs[slot][1] = _dot_nt(jnp.where(lane >= DIFF_DH, q, zero), k)

    def numerators(tile, slot, online):
        qi, kj = tile
        for w in range(2):
            s = s_bufs[slot][w]
            if online:
                m_old = jnp.where(kj == 0, -jnp.inf, m_buf[w])
                m_new = jnp.maximum(m_old, jnp.max(s, axis=-1, keepdims=True))
                corr_bufs[slot][w] = jnp.exp2(m_old - m_new)
                m_buf[w] = m_new
            else:
                m_new = shift_buf[qi, :, w:w + 1]
            p_bufs[slot][w] = jnp.exp2(s - m_new).astype(BF16)

    def values(tile, slot, online):
        qi, kj = tile
        ve = vext[key_rows(kj), :]
        a = []
        for w in range(2):
            keep = corr_bufs[slot][w] if online else jnp.where(kj == 0, 0.0, 1.0)
            a.append(keep * acc[w] + _dot(p_bufs[slot][w], ve))
            acc[w] = a[w]
        o = a[0][:, 0:dv] / a[0][:, dv:2 * dv] - lam * (a[1][:, 0:dv] / a[1][:, dv:2 * dv])
        o = o * lax.rsqrt(jnp.mean(o * o, axis=-1, keepdims=True) + GN_EPS)
        o_ref[pl.ds(pl.multiple_of(qi * tq, tq), tq), :] = (o * gain).astype(BF16)

    def pipeline(online):
        def step(tiles, slot):
            a, b, c = tiles
            values(c, slot, online)
            scores(a, slot)
            numerators(b, 1 - slot, online)
            return advance(a), a, b

        acc[...] = jnp.zeros(acc.shape, F32)
        t0 = (jnp.int32(0), jnp.int32(0))
        t1 = advance(t0)
        scores(t0, 0)
        scores(t1, 1)
        numerators(t0, 0, online)

        def pair(_, tiles):
            return step(step(tiles, 0), 1)

        _, last, prev = lax.fori_loop(0, (n_tiles - 2) // 2, pair, (advance(t1), t1, t0))
        numerators(last, (n_tiles - 1) % 2, online)
        values(prev, n_tiles % 2, online)
        values(last, (n_tiles - 1) % 2, online)

    pl.when(bounded)(functools.partial(pipeline, False))
    pl.when(jnp.logical_not(bounded))(functools.partial(pipeline, True))


def _diff_attention(qkv, lam, subln_g, lambda_init, *, nb, l_len, t_len, ctx_len):
    d = D_MODEL
    tq = 2 * ROW_TILE
    dv = 2 * DIFF_DH
    nq = l_len // tq
    assert (nq * (t_len // ATT_TK)) % 2 == 0
    kern = functools.partial(_attn_kernel, out_scale=1.0 - lambda_init, ctx_len=ctx_len)
    return pl.pallas_call(
        kern,
        grid_spec=pltpu.PrefetchScalarGridSpec(
            num_scalar_prefetch=1,
            grid=(nb, DIFF_HEADS),
            in_specs=[
                pl.BlockSpec((t_len, LANES), lambda b, h, lam: (b, h)),
                pl.BlockSpec((t_len, LANES), lambda b, h, lam: (b, DIFF_HEADS + h)),
                pl.BlockSpec((t_len, LANES), lambda b, h, lam: (b, 2 * DIFF_HEADS + h)),
                pl.BlockSpec((1, LANES), lambda b, h, lam: (0, 0)),
            ],
            out_specs=pl.BlockSpec((l_len, LANES), lambda b, h, lam: (b, h)),
            scratch_shapes=[
                pltpu.VMEM((t_len, 2 * dv), BF16),
                pltpu.VMEM((2, tq, ATT_TK), F32), pltpu.VMEM((2, tq, ATT_TK), F32),
                pltpu.VMEM((2, tq, ATT_TK), BF16), pltpu.VMEM((2, tq, ATT_TK), BF16),
                pltpu.VMEM((2, tq, 1), F32), pltpu.VMEM((2, tq, 1), F32),
                pltpu.VMEM((2, tq, 1), F32), pltpu.VMEM((nq, tq, 2), F32),
                pltpu.VMEM((2, tq, 2 * dv), F32),
            ],
        ),
        out_shape=jax.ShapeDtypeStruct((nb * l_len, d), BF16),
        compiler_params=_cparams(("parallel", "parallel")),
        name="diff_attention",
    )(lam, qkv, qkv, qkv, subln_g.reshape(1, LANES).astype(F32))


def _ret_rope_tables(l_len, ctx_len):
    half = RET_DK // 2
    inv = ROPE_BASE ** (-jnp.arange(0, RET_DK, 2, dtype=F32) / RET_DK)
    ang = jnp.arange(l_len, dtype=F32)[:, None] * inv[None, :]
    ang = jnp.concatenate([jnp.zeros((ctx_len, half), F32), ang], axis=0)
    cos64 = jnp.concatenate([jnp.cos(ang), jnp.cos(ang)], axis=1)
    sin64 = jnp.concatenate([-jnp.sin(ang), jnp.sin(ang)], axis=1)
    kscale = RET_DK ** -0.5
    cos = jnp.concatenate([cos64, cos64 * kscale], axis=1)
    sin = jnp.concatenate([sin64, sin64 * kscale], axis=1)
    return cos[None], sin[None]


def _attn_rope_tables(l_len, ctx_len):
    quarter = DIFF_DH // 4
    inv = ROPE_BASE ** (-jnp.arange(0, DIFF_DH // 2, 2, dtype=F32) / (DIFF_DH // 2))
    pos = jnp.arange(l_len)
    ang_r = (pos // GRID_W).astype(F32)[:, None] * inv[None, :]
    ang_c = (pos % GRID_W).astype(F32)[:, None] * inv[None, :]
    pad = lambda a: jnp.concatenate([jnp.zeros((ctx_len, quarter), F32), a], axis=0)
    ang_r, ang_c = pad(ang_r), pad(ang_c)
    cos64 = jnp.concatenate([jnp.cos(ang_r)] * 2 + [jnp.cos(ang_c)] * 2, axis=1)
    sin64 = jnp.concatenate([-jnp.sin(ang_r), jnp.sin(ang_r), -jnp.sin(ang_c), jnp.sin(ang_c)], axis=1)
    cos = jnp.concatenate([cos64, cos64], axis=1)
    sin = jnp.concatenate([sin64, sin64], axis=1)
    qscale = DIFF_DH ** -0.5 * math.log2(math.e)
    return jnp.stack([cos * qscale, cos]), jnp.stack([sin * qscale, sin])


def kernel(x, c, ctx, c_ctx, ada_w, ada_b, ln_g, w_in_ab, ret_decay_logit, s5_lam_re, s5_lam_im,
           s5_log_dt, s5_b_re, s5_b_im, s5_c_re, s5_c_im, s5_d, s5_w_glu, w_out_ab, w_in_c,
           diff_lambda, diff_subln_g, w_out_c, router_w, router_bias, exp_w_gate, exp_w_up,
           exp_w_down):
    nb, l_len, d = x.shape
    ctx_len = ctx.shape[1]
    t_len = ctx_len + l_len
    nt = nb * t_len
    tpb = t_len // ROW_TILE
    ctx_tiles = ctx_len // ROW_TILE
    assert d == D_MODEL and nb < SUBLANES
    assert l_len % (2 * ROW_TILE) == 0 and ctx_len % ROW_TILE == 0 and t_len % ATT_TK == 0

    x_lat, x_ctx = x.reshape(nb * l_len, d), ctx.reshape(nb * ctx_len, d)
    c_all = jnp.concatenate([c, c_ctx[None].astype(c.dtype)], axis=0)
    c_pad = jnp.zeros((SUBLANES, d), F32).at[:nb + 1].set(c_all)
    mod = _adaln(c_pad, ada_w, ada_b)

    wr_t = router_w.T
    rbias = router_bias.reshape(N_EXPERTS, 1).astype(F32)

    w0 = w_in_ab[0]
    q_w, k_w, v_w, g_w, u_w = jnp.split(w0, (RET_QK_W, 2 * RET_QK_W, 2 * RET_QK_W + RET_V_W,
                                             2 * RET_QK_W + 2 * RET_V_W), axis=1)
    qk_w = jnp.concatenate([q_w.reshape(d, RET_HEADS, RET_DK), k_w.reshape(d, RET_HEADS, RET_DK)],
                           axis=2).reshape(d, 2 * RET_QK_W)
    w0p = jnp.concatenate([qk_w, v_w, g_w, u_w], axis=1).astype(BF16)
    cos0, sin0 = _ret_rope_tables(l_len, ctx_len)
    rope_tab0 = [0] * RET_HEADS + [None] * ((w0p.shape[1] - S5_CH - 2 * RET_QK_W) // LANES)
    proj0, u5 = _inproj((x_lat, x_ctx), mod, 0, w0p, cos0, sin0, rope_tab0, RET_DK // 2,
                        nb=nb, t_len=t_len, ctx_len=ctx_len, s5_cols=S5_CH)

    log_gammas = jax.nn.log_sigmoid(ret_decay_logit[0].astype(F32))
    ret = _retention(proj0, log_gammas, nb=nb, t_len=t_len, ctx_len=ctx_len)

    s5_ops = _s5_operators(s5_lam_re[0], s5_lam_im[0], s5_log_dt[0], s5_b_re[0], s5_b_im[0],
                           s5_c_re[0], s5_c_im[0], s5_d[0])
    s5y = _s5(u5, *s5_ops, nb=nb, nctx=ctx_len // S5_CHUNK)

    x1, h2, eidx, gates = _merge0(ret, proj0, s5y, x_lat, x_ctx, mod, s5_w_glu[0].astype(BF16),
                                  w_out_ab[0].astype(BF16), ln_g[0, 0].reshape(1, d), wr_t, rbias,
                                  nb=nb, t_len=t_len, ctx_len=ctx_len)
    experts = (exp_w_gate.astype(BF16), exp_w_up.astype(BF16), exp_w_down.astype(BF16))
    x2 = _moe_layer(x1, h2, eidx, gates, mod, 0, ln_g[0, 1].reshape(1, d), *experts,
                    tiles_per_batch=tpb, ctx_tiles=ctx_tiles, nb=nb)

    cos1, sin1 = _attn_rope_tables(l_len, ctx_len)
    n_heads_cols = D_MODEL // LANES
    rope_tab1 = [0] * n_heads_cols + [1] * n_heads_cols + [None] * n_heads_cols
    qkv, = _inproj((x2,), mod, 1, w_in_c[0].astype(BF16), cos1, sin1, rope_tab1, DIFF_DH // 4,
                   nb=nb, t_len=t_len, ctx_len=ctx_len)
    lf = diff_lambda[0].astype(F32)
    lambda_init = 0.8 - 0.6 * math.exp(-0.3 * 1)
    lam = (jnp.exp(jnp.sum(lf[0] * lf[1])) - jnp.exp(jnp.sum(lf[2] * lf[3])) + lambda_init).reshape(1)
    att = _diff_attention(qkv, lam, diff_subln_g[0], lambda_init,
                          nb=nb, l_len=l_len, t_len=t_len, ctx_len=ctx_len)
    x3, h3, eidx1, gates1 = _merge1(att, x2, mod, 1, w_out_c[0].astype(BF16), ln_g[1, 0].reshape(1, d),
                                    wr_t, rbias, nb=nb, l_len=l_len, t_len=t_len, ctx_len=ctx_len)
    out = _moe_layer(x3, h3, eidx1, gates1, mod, 1, ln_g[1, 1].reshape(1, d), *experts,
                     tiles_per_batch=l_len // ROW_TILE, ctx_tiles=0, nb=nb)
    return out.reshape(nb, l_len, d)
```

```python
import functools
import math

import jax
import jax.numpy as jnp
import numpy as np
from jax import lax
from jax.experimental import pallas as pl
from jax.experimental.pallas import tpu as pltpu

F32 = jnp.float32
BF16 = jnp.bfloat16

D_MODEL = 1024
DEPTH = 2
GRID_W = 64
ALPHA = (2.0 * DEPTH) ** 0.25
LN_EPS = 1e-5
GN_EPS = 1e-6
ROPE_BASE = 10000.0
RET_DK = 64
RET_DV = 128
RET_HEADS = 6
RET_QK_W = RET_HEADS * RET_DK
RET_V_W = RET_HEADS * RET_DV
S5_CH = 256
S5_P = 16
S5_G = 16
S5_N = 64
DIFF_HEADS = 8
DIFF_DH = 64
N_EXPERTS = 16
EXPERTS_PER_GROUP = 4
TOP_K = 2

LANES = 128
SUBLANES = 8
MXU_DIM = 256
ROW_TILE = 256
RET_CHUNK = 256
RET_STATE_UNROLL = 3
RET_OUT_UNROLL = 11
S5_CHUNK = 8
S5_HALVES = 2
MOE_ROWS = 256
DISPATCH_TILE = 512
ROW_WORDS = D_MODEL // 2
ROW_SUB = ROW_WORDS // LANES
ATT_TQ = 1024
ATT_TK = 768
ATT_SAFE_SHIFT = 48.0
VMEM_LIMIT = 48 * 1024 * 1024
ATT_VMEM_LIMIT = 56 * 1024 * 1024


def _cparams(sem, vmem_limit=VMEM_LIMIT):
    return pltpu.CompilerParams(dimension_semantics=sem, vmem_limit_bytes=vmem_limit)


def _dot(a, b):
    return jnp.dot(a, b, preferred_element_type=F32)


def _dot_nt(a, b):
    return lax.dot_general(a, b, (((1,), (1,)), ((), ())), preferred_element_type=F32)


def _dot_tn(a, b):
    return lax.dot_general(a, b, (((0,), (0,)), ((), ())), preferred_element_type=F32)


def _split_bf16(x):
    hi = x.astype(BF16)
    lo = (x - hi.astype(F32)).astype(BF16)
    return hi, lo


def _dot3(a, b):
    ah, al = _split_bf16(a)
    bh, bl = _split_bf16(b)
    return _dot(ah, bh) + _dot(ah, bl) + _dot(al, bh)


def _dot3_nt(a, b):
    ah, al = _split_bf16(a)
    bh, bl = _split_bf16(b)
    return _dot_nt(ah, bh) + _dot_nt(ah, bl) + _dot_nt(al, bh)


def _sigmoid(x):
    return 1.0 / (1.0 + jnp.exp(-x))


def _silu(x):
    return x * _sigmoid(x)


def _pack_bf16_pairs(v):
    half = v.shape[1] // 2
    bits = lax.bitcast_convert_type(v.astype(BF16).astype(F32), jnp.uint32)
    return (bits[:, :half] >> 16) | (bits[:, half:] & jnp.uint32(0xFFFF0000))


def _unpack_bf16_pairs(p):
    lo = lax.bitcast_convert_type(p << 16, F32)
    hi = lax.bitcast_convert_type(p & jnp.uint32(0xFFFF0000), F32)
    return jnp.concatenate([lo, hi], axis=1)


def _row_shape(n_rows):
    return (n_rows // SUBLANES, ROW_SUB, SUBLANES, LANES)


def _row(ref, r):
    return ref.at[lax.shift_right_logical(r, 3), :, r & (SUBLANES - 1), :]


def _store_rows(ref, packed):
    rows = packed.shape[0]
    for j in range(ROW_SUB):
        ref[:, j] = packed[:, j * LANES:(j + 1) * LANES].reshape(rows // SUBLANES, SUBLANES, LANES)


def _load_rows(ref):
    rows = ref.shape[0] * SUBLANES
    return jnp.concatenate([ref[:, j].reshape(rows, LANES) for j in range(ROW_SUB)], axis=1)


def _gelu_tanh(x):
    c = math.sqrt(2.0 / math.pi)
    return 0.5 * x * (1.0 + jnp.tanh(c * (x + 0.044715 * (x * x * x))))


def _adaln_kernel(c_ref, w_ref, b_ref, o_ref):
    c = c_ref[...]
    o_ref[0] = _dot3(_silu(c), w_ref[0]) + b_ref[0]


def _adaln(c_pad, ada_w, ada_b):
    depth, d, n = ada_w.shape
    tn = 1536
    return pl.pallas_call(
        _adaln_kernel,
        grid=(depth, n // tn),
        in_specs=[
            pl.BlockSpec((SUBLANES, d), lambda i, j: (0, 0)),
            pl.BlockSpec((1, d, tn), lambda i, j: (i, 0, j)),
            pl.BlockSpec((1, 1, tn), lambda i, j: (i, 0, j)),
        ],
        out_specs=pl.BlockSpec((1, SUBLANES, tn), lambda i, j: (i, 0, j)),
        out_shape=jax.ShapeDtypeStruct((depth, SUBLANES, n), F32),
        compiler_params=_cparams(("parallel", "parallel")),
        name="adaln",
    )(c_pad, ada_w, ada_b.reshape(depth, 1, n))


def _mod_row(t, tiles_per_batch, ctx_tiles, nb):
    b = lax.div(t, tiles_per_batch)
    w = lax.rem(t, tiles_per_batch)
    return jnp.where(w < ctx_tiles, nb, b)


def _rope_block(a, cos, sin, half):
    lane = lax.broadcasted_iota(jnp.int32, a.shape, 1)
    first = lax.rem(lane, 2 * half) < half
    rot = jnp.where(first, pltpu.roll(a, LANES - half, 1), pltpu.roll(a, half, 1))
    return a * cos + rot * sin


def _chunk_perm(rows, transpose):
    per = rows // S5_CHUNK
    r = lax.broadcasted_iota(jnp.int32, (rows, rows), 0)
    c = lax.broadcasted_iota(jnp.int32, (rows, rows), 1)
    if transpose:
        r, c = c, r
    return (c == S5_CHUNK * lax.rem(r, per) + lax.div(r, per)).astype(BF16)


def _token_specs(d, tiles_per_batch, ctx_tiles):
    lat_tiles = tiles_per_batch - ctx_tiles
    b = lambda t: lax.div(t, tiles_per_batch)
    w = lambda t: lax.rem(t, tiles_per_batch)
    return (pl.BlockSpec((ROW_TILE, d), lambda t: (b(t) * lat_tiles + jnp.maximum(w(t) - ctx_tiles, 0), 0)),
            pl.BlockSpec((ROW_TILE, d), lambda t: (b(t) * ctx_tiles + jnp.minimum(w(t), ctx_tiles - 1), 0)))


def _token_tile(x_refs, tiles_per_batch, ctx_tiles):
    if len(x_refs) == 1:
        return x_refs[0][...]
    is_ctx = lax.rem(pl.program_id(0), tiles_per_batch) < ctx_tiles
    return jnp.where(is_ctx, x_refs[1][...], x_refs[0][...])


def _inproj_kernel(*refs, n_x, tiles_per_batch, ctx_tiles, nb, rope_tab, rope_half):
    x_refs, (mod_ref, w_ref, cos_ref, sin_ref, o_ref), rest = refs[:n_x], refs[n_x:n_x + 5], refs[n_x + 5:]
    d = w_ref.shape[0]
    n = o_ref.shape[1]
    r = _mod_row(pl.program_id(0), tiles_per_batch, ctx_tiles, nb)
    sh = mod_ref[0, pl.ds(r, 1), 0:d]
    sc = mod_ref[0, pl.ds(r, 1), d:2 * d]
    xm = (_token_tile(x_refs, tiles_per_batch, ctx_tiles) * (1.0 + sc) + sh).astype(BF16)
    for j in range(n // MXU_DIM):
        acc = _dot(xm, w_ref[:, j * MXU_DIM:(j + 1) * MXU_DIM])
        parts = []
        for s in range(MXU_DIM // LANES):
            blk = acc[:, s * LANES:(s + 1) * LANES]
            tab = rope_tab[j * (MXU_DIM // LANES) + s]
            if tab is not None:
                blk = _rope_block(blk, cos_ref[tab], sin_ref[tab], rope_half)
            parts.append(blk)
        o_ref[:, j * MXU_DIM:(j + 1) * MXU_DIM] = jnp.concatenate(parts, axis=1).astype(BF16)
    if rest:
        u_ref, = rest
        rows = o_ref.shape[0]
        per = rows // S5_CHUNK
        u = _dot(xm, w_ref[:, n:n + S5_CH]).astype(BF16)
        up = _dot(_chunk_perm(rows, False), u).astype(BF16)
        for s in range(S5_CHUNK):
            for hf in range(S5_CH // LANES):
                u_ref[hf, :, s * LANES:(s + 1) * LANES] = up[s * per:(s + 1) * per,
                                                             hf * LANES:(hf + 1) * LANES]


def _inproj(xs, mod, layer, w, cos, sin, rope_tab, rope_half, *, nb, t_len, ctx_len, s5_cols=0):
    nt, d = nb * t_len, w.shape[0]
    n = w.shape[1] - s5_cols
    tpb = t_len // ROW_TILE
    ctx_tiles = ctx_len // ROW_TILE
    kern = functools.partial(_inproj_kernel, n_x=len(xs), tiles_per_batch=tpb, ctx_tiles=ctx_tiles,
                             nb=nb, rope_tab=tuple(rope_tab), rope_half=rope_half)
    x_specs = ([pl.BlockSpec((ROW_TILE, d), lambda t: (t, 0))] if len(xs) == 1
               else list(_token_specs(d, tpb, ctx_tiles)))
    ntab = cos.shape[0]
    out_specs = [pl.BlockSpec((ROW_TILE, n), lambda t: (t, 0))]
    out_shape = [jax.ShapeDtypeStruct((nt, n), BF16)]
    if s5_cols:
        halves, per = s5_cols // LANES, ROW_TILE // S5_CHUNK
        out_specs.append(pl.BlockSpec((halves, per, S5_CHUNK * LANES), lambda t: (0, t, 0)))
        out_shape.append(jax.ShapeDtypeStruct((halves, nt // S5_CHUNK, S5_CHUNK * LANES), BF16))
    return pl.pallas_call(
        kern,
        grid=(nt // ROW_TILE,),
        in_specs=x_specs + [
            pl.BlockSpec((1, SUBLANES, mod.shape[2]), lambda t: (layer, 0, 0)),
            pl.BlockSpec(w.shape, lambda t: (0, 0)),
            pl.BlockSpec((ntab, ROW_TILE, LANES), lambda t: (0, lax.rem(t, tpb), 0)),
            pl.BlockSpec((ntab, ROW_TILE, LANES), lambda t: (0, lax.rem(t, tpb), 0)),
        ],
        out_specs=out_specs,
        out_shape=out_shape,
        compiler_params=_cparams(("parallel",)),
        name="inproj%d" % layer,
    )(*xs, mod, w, cos, sin)


def _retention_kernel(lg_ref, qk_ref, v_ref, o_ref, sf_ref, sb_ref, *, nctx):
    c_len = RET_CHUNK
    t_len = qk_ref.shape[0]
    nc = t_len // c_len
    h = pl.program_id(1)
    lgf = lg_ref[0, h]
    lgb = lg_ref[1, h]
    ii = lax.broadcasted_iota(jnp.int32, (c_len, 1), 0).astype(F32)
    jj = lax.broadcasted_iota(jnp.int32, (1, c_len), 1).astype(F32)
    diff = ii - jj
    decay = jnp.where(diff >= 0.0, jnp.exp(lgf * jnp.maximum(diff, 0.0)),
                      jnp.exp(lgb * jnp.maximum(-diff, 0.0)))
    kdf = jnp.exp(lgf * (c_len - 1.0 - ii))
    kdb = jnp.exp(lgb * ii)
    qdf = jnp.exp(lgf * (ii + 1.0))
    qdb = jnp.exp(lgb * (c_len - ii))
    zrow = jnp.zeros((1, RET_DV), F32)
    gf_chunk = jnp.exp(zrow + lgf * c_len)
    gb_chunk = jnp.exp(zrow + lgb * c_len)

    def load(c):
        rows = pl.ds(pl.multiple_of(c * c_len, c_len), c_len)
        qk = qk_ref[rows, :].astype(F32)
        return qk[:, :RET_DK], qk[:, RET_DK:], v_ref[rows, :]

    def states(j, carry):
        sf, sb = carry
        cb = jnp.where(j < nctx, nctx - 1 - j, nc - 1 - (j - nctx))
        sf_ref[j] = sf
        sb_ref[cb] = sb
        _, kf, vf = load(j)
        _, kb, vb = load(cb)
        return (gf_chunk * sf + _dot_tn((kf * kdf).astype(BF16), vf),
                gb_chunk * sb + _dot_tn((kb * kdb).astype(BF16), vb))

    zero_state = jnp.zeros((RET_DK, RET_DV), F32)
    lax.fori_loop(0, nc, states, (zero_state, zero_state), unroll=RET_STATE_UNROLL)

    def out_chunk(c, carry):
        q, k, v = load(c)
        scores = _dot_nt(q.astype(BF16), k.astype(BF16)) * decay
        o = _dot(scores.astype(BF16), v)
        o = o + _dot((q * qdf).astype(BF16), sf_ref[c].astype(BF16))
        o = o + _dot((q * qdb).astype(BF16), sb_ref[c].astype(BF16))
        mu = jnp.mean(o, axis=-1, keepdims=True)
        oc = o - mu
        var = jnp.mean(oc * oc, axis=-1, keepdims=True)
        rows = pl.ds(pl.multiple_of(c * c_len, c_len), c_len)
        o_ref[rows, :] = (oc * lax.rsqrt(var + GN_EPS)).astype(BF16)
        return carry

    lax.fori_loop(0, nc, out_chunk, 0, unroll=RET_OUT_UNROLL)


def _retention(proj, log_gammas, *, nb, t_len, ctx_len):
    nt = proj.shape[0]
    nc = t_len // RET_CHUNK
    kern = functools.partial(_retention_kernel, nctx=ctx_len // RET_CHUNK)
    vcol0 = RET_HEADS
    return pl.pallas_call(
        kern,
        grid_spec=pltpu.PrefetchScalarGridSpec(
            num_scalar_prefetch=1,
            grid=(nb, RET_HEADS),
            in_specs=[
                pl.BlockSpec((t_len, LANES), lambda b, h, lg: (b, h)),
                pl.BlockSpec((t_len, LANES), lambda b, h, lg: (b, vcol0 + h)),
            ],
            out_specs=pl.BlockSpec((t_len, LANES), lambda b, h, lg: (b, h)),
            scratch_shapes=[pltpu.VMEM((nc, RET_DK, RET_DV), F32),
                            pltpu.VMEM((nc, RET_DK, RET_DV), F32)],
        ),
        out_shape=jax.ShapeDtypeStruct((nt, RET_V_W), BF16),
        compiler_params=_cparams(("parallel", "parallel")),
        name="retention",
    )(log_gammas, proj, proj)


def _s5_operators(lam_re, lam_im, log_dt, b_re, b_im, c_re, c_im, d_skip):
    tc = S5_CHUNK
    hp = lax.Precision.HIGHEST
    ks = jnp.arange(tc + 1, dtype=F32)
    pw, bbar, cm = [], [], []
    for direction in range(2):
        dt = jnp.exp(log_dt[direction].astype(F32))[:, None]
        lam = lax.complex(lam_re[direction].astype(F32), lam_im[direction].astype(F32))
        z = lam * dt
        p = jnp.exp(z[None] * ks[:, None, None])
        lam_bar = p[1]
        bb = ((lam_bar - 1.0) / lam)[..., None] * lax.complex(
            b_re[direction].astype(F32), b_im[direction].astype(F32))
        pw.append(p)
        bbar.append(bb)
        cm.append(lax.complex(c_re[direction].astype(F32), c_im[direction].astype(F32)))

    def lag_kernel(p, bb, c):
        return jnp.einsum('gpn,kgn,gnq->kgpq', c, p[:tc], bb, precision=hp).real

    kf = lag_kernel(pw[0], bbar[0], cm[0])
    kb = lag_kernel(pw[1], bbar[1], cm[1])
    k0 = kf[0] + kb[0] + jnp.eye(S5_P, dtype=F32)[None] * d_skip.astype(F32)[:, :, None]
    kcat = jnp.concatenate([kb[1:][::-1], k0[None], kf[1:]], axis=0)
    s_idx = jnp.arange(tc)[:, None]
    t_idx = jnp.arange(tc)[None, :]
    m5 = kcat[t_idx - s_idx + tc - 1]
    hg = S5_G // S5_HALVES
    eye = jnp.eye(hg, dtype=F32)
    split = lambda z, axis: z.reshape(z.shape[:axis] + (S5_HALVES, hg) + z.shape[axis + 1:])
    wide = tc * hg * S5_P
    intra = jnp.einsum('sthgpq,gk->hsgqtkp', split(m5, 2), eye).reshape(S5_HALVES, wide, wide)

    ef = pw[0][:tc][::-1][:, :, :, None] * bbar[0][None]
    eb = pw[1][:tc][:, :, :, None] * bbar[1][None]
    parts_in = jnp.stack([ef.real, ef.imag, eb.real, eb.imag], axis=0)
    w_in = jnp.einsum('cshgnq,gk->hsgqckn', split(parts_in, 2), eye).reshape(
        S5_HALVES, wide, 4 * hg * S5_N)
    of = cm[0][None] * pw[0][1:][:, :, None, :]
    ob = cm[1][None] * pw[1][1:][::-1][:, :, None, :]
    parts_out = jnp.stack([of.real, -of.imag, ob.real, -ob.imag], axis=0)
    w_out = jnp.einsum('cthgpn,gk->hckntgp', split(parts_out, 2), eye).reshape(
        S5_HALVES, 4 * hg * S5_N, wide)
    a = jnp.stack([pw[0][tc].real, pw[0][tc].imag, pw[1][tc].real, pw[1][tc].imag], axis=0)
    a = a.reshape(4, S5_HALVES, hg * S5_N).transpose(1, 0, 2)
    return intra.astype(BF16), w_in.astype(BF16), w_out.astype(BF16), a


def _s5_kernel(x_ref, wi_ref, win_ref, wout_ref, a_ref, y_ref, st_ref, *, nctx):
    x = x_ref[0]
    nc = x.shape[0]
    w = a_ref.shape[2]
    st_ref[...] = _dot(x, win_ref[0])
    afr, afi, abr, abi = (a_ref[0, i:i + 1, :] for i in range(4))

    def step(j, carry):
        fr, fi, br, bi = carry
        rf = pl.ds(j, 1)
        rb = pl.ds(jnp.where(j < nctx, nctx - 1 - j, nc - 1 - (j - nctx)), 1)
        efr, efi = st_ref[rf, 0:w], st_ref[rf, w:2 * w]
        ebr, ebi = st_ref[rb, 2 * w:3 * w], st_ref[rb, 3 * w:4 * w]
        st_ref[rf, 0:w] = fr
        st_ref[rf, w:2 * w] = fi
        st_ref[rb, 2 * w:3 * w] = br
        st_ref[rb, 3 * w:4 * w] = bi
        return (afr * fr - afi * fi + efr, afr * fi + afi * fr + efi,
                abr * br - abi * bi + ebr, abr * bi + abi * br + ebi)

    z = jnp.zeros((1, w), F32)
    lax.fori_loop(0, nc, step, (z, z, z, z), unroll=4)
    y_ref[0] = (_dot(x, wi_ref[0]) + _dot(st_ref[...].astype(BF16), wout_ref[0])).astype(BF16)


def _s5(xc, intra, w_in, w_out, a, *, nb, nctx):
    halves, rows, wide = xc.shape
    nc = rows // nb
    kern = functools.partial(_s5_kernel, nctx=nctx)
    per_half = lambda arr: pl.BlockSpec((1,) + arr.shape[1:], lambda hf, b: (hf, 0, 0))
    return pl.pallas_call(
        kern,
        grid=(halves, nb),
        in_specs=[
            pl.BlockSpec((1, nc, wide), lambda hf, b: (hf, b, 0)),
            per_half(intra), per_half(w_in), per_half(w_out), per_half(a),
        ],
        out_specs=pl.BlockSpec((1, nc, wide), lambda hf, b: (hf, b, 0)),
        out_shape=jax.ShapeDtypeStruct(xc.shape, BF16),
        scratch_shapes=[pltpu.VMEM((nc, w_in.shape[2]), F32)],
        compiler_params=_cparams(("parallel", "parallel")),
        name="s5",
    )(xc, intra, w_in, w_out, a)


def _route(logits_t, bias):
    scores = _sigmoid(logits_t)
    biased = scores + bias
    s_rows = [scores[e:e + 1, :] for e in range(N_EXPERTS)]
    b_rows = [biased[e:e + 1, :] for e in range(N_EXPERTS)]
    n_groups = N_EXPERTS // EXPERTS_PER_GROUP
    best = None
    sel = None
    for g in range(n_groups):
        a, b, c, d = b_rows[4 * g:4 * g + 4]
        hi1, lo1 = jnp.maximum(a, b), jnp.minimum(a, b)
        hi2, lo2 = jnp.maximum(c, d), jnp.minimum(c, d)
        top1 = jnp.maximum(hi1, hi2)
        top2 = jnp.maximum(jnp.minimum(hi1, hi2), jnp.maximum(lo1, lo2))
        gs = top1 + top2
        if g == 0:
            best, sel = gs, jnp.zeros(gs.shape, jnp.int32)
        else:
            better = gs > best
            sel = jnp.where(better, g, sel)
            best = jnp.where(better, gs, best)
    neg = jnp.full(best.shape, -jnp.inf, F32)
    masked = [jnp.where(sel == (e // EXPERTS_PER_GROUP), b_rows[e], neg) for e in range(N_EXPERTS)]
    v1, i1, g1 = masked[0], jnp.zeros(best.shape, jnp.int32), s_rows[0]
    for e in range(1, N_EXPERTS):
        better = masked[e] > v1
        v1 = jnp.where(better, masked[e], v1)
        i1 = jnp.where(better, e, i1)
        g1 = jnp.where(better, s_rows[e], g1)
    v2, i2, g2 = neg, jnp.zeros(best.shape, jnp.int32), jnp.zeros(best.shape, F32)
    for e in range(N_EXPERTS):
        cand = jnp.where(i1 == e, neg, masked[e])
        better = cand > v2
        v2 = jnp.where(better, cand, v2)
        i2 = jnp.where(better, e, i2)
        g2 = jnp.where(better, s_rows[e], g2)
    tot = g1 + g2
    return jnp.concatenate([i1, i2], axis=0), jnp.concatenate([g1 / tot, g2 / tot], axis=0)


def _tail(x, o, mod_ref, r, lng, wr, rb, x1_ref, h2_ref, ei_ref, gt_ref):
    d = x.shape[1]
    g1 = mod_ref[0, pl.ds(r, 1), 2 * d:3 * d]
    sh2 = mod_ref[0, pl.ds(r, 1), 3 * d:4 * d]
    sc2 = mod_ref[0, pl.ds(r, 1), 4 * d:5 * d]
    y = ALPHA * x + g1 * o
    mu = jnp.mean(y, axis=-1, keepdims=True)
    yc = y - mu
    var = jnp.mean(yc * yc, axis=-1, keepdims=True)
    x1 = yc * lax.rsqrt(var + LN_EPS) * lng
    h2 = x1 * (1.0 + sc2) + sh2
    x1_ref[...] = x1
    _store_rows(h2_ref, _pack_bf16_pairs(h2))
    ei, gt = _route(_dot3_nt(wr, h2), rb)
    ei_ref[...] = ei
    gt_ref[...] = gt


def _merge0_kernel(r_ref, g_ref, s_ref, x_ref, xc_ref, mod_ref, wglu_ref, wout_ref, lng_ref, wr_ref,
                   rb_ref, x1_ref, h2_ref, ei_ref, gt_ref, *, tiles_per_batch, ctx_tiles, nb):
    r = _mod_row(pl.program_id(0), tiles_per_batch, ctx_tiles, nb)
    ret = r_ref[...].astype(F32) * _silu(g_ref[...].astype(F32))
    rows = x_ref.shape[0]
    x = _token_tile((x_ref, xc_ref), tiles_per_batch, ctx_tiles)
    sp = jnp.concatenate(
        [jnp.concatenate([s_ref[hf, :, s * LANES:(s + 1) * LANES] for hf in range(S5_HALVES)], axis=1)
         for s in range(S5_CHUNK)], axis=0)
    s5 = _dot(_chunk_perm(rows, True), sp)
    z = _dot(_gelu_tanh(s5).astype(BF16), wglu_ref[...])
    zz = z[:, :S5_CH] * _sigmoid(z[:, S5_CH:])
    o = _dot(ret.astype(BF16), wout_ref[0:RET_V_W, :]) + _dot(zz.astype(BF16), wout_ref[RET_V_W:, :])
    _tail(x, o, mod_ref, r, lng_ref[...], wr_ref[...], rb_ref[...],
          x1_ref, h2_ref, ei_ref, gt_ref)


def _merge1_kernel(a_ref, x_ref, mod_ref, wout_ref, lng_ref, wr_ref, rb_ref,
                   x1_ref, h2_ref, ei_ref, gt_ref, *, tiles_per_batch):
    r = lax.div(pl.program_id(0), tiles_per_batch)
    o = _dot(a_ref[...], wout_ref[...])
    _tail(x_ref[...], o, mod_ref, r, lng_ref[...], wr_ref[...], rb_ref[...],
          x1_ref, h2_ref, ei_ref, gt_ref)


def _tail_outs(n_rows, d):
    shapes = (jax.ShapeDtypeStruct((n_rows, d), F32),
              jax.ShapeDtypeStruct(_row_shape(n_rows), jnp.uint32),
              jax.ShapeDtypeStruct((TOP_K, n_rows), jnp.int32),
              jax.ShapeDtypeStruct((TOP_K, n_rows), F32))
    specs = (pl.BlockSpec((ROW_TILE, d), lambda t: (t, 0)),
             pl.BlockSpec(_row_shape(ROW_TILE), lambda t: (t, 0, 0, 0)),
             pl.BlockSpec((TOP_K, ROW_TILE), lambda t: (0, t)),
             pl.BlockSpec((TOP_K, ROW_TILE), lambda t: (0, t)))
    return shapes, specs


def _merge0(ret, proj, s5y, x_lat, x_ctx, mod, w_glu, w_out, lng, wr_t, rbias, *, nb, t_len, ctx_len):
    nt, d = nb * t_len, x_lat.shape[1]
    tpb = t_len // ROW_TILE
    ctx_tiles = ctx_len // ROW_TILE
    kern = functools.partial(_merge0_kernel, tiles_per_batch=tpb, ctx_tiles=ctx_tiles, nb=nb)
    shapes, specs = _tail_outs(nt, d)
    gcol = (2 * RET_QK_W + RET_V_W) // RET_V_W
    full = lambda a: pl.BlockSpec(a.shape, lambda t: (0,) * a.ndim)
    return pl.pallas_call(
        kern,
        grid=(nt // ROW_TILE,),
        in_specs=[
            pl.BlockSpec((ROW_TILE, RET_V_W), lambda t: (t, 0)),
            pl.BlockSpec((ROW_TILE, RET_V_W), lambda t: (t, gcol)),
            pl.BlockSpec((S5_HALVES, ROW_TILE // S5_CHUNK, S5_CHUNK * LANES), lambda t: (0, t, 0)),
            *_token_specs(d, tpb, ctx_tiles),
            pl.BlockSpec((1, SUBLANES, mod.shape[2]), lambda t: (0, 0, 0)),
            full(w_glu), full(w_out), full(lng), full(wr_t), full(rbias),
        ],
        out_specs=specs,
        out_shape=shapes,
        compiler_params=_cparams(("parallel",)),
        name="merge0",
    )(ret, proj, s5y, x_lat, x_ctx, mod, w_glu, w_out, lng, wr_t, rbias)


def _merge1(att, x, mod, layer, w_out, lng, wr_t, rbias, *, nb, l_len, t_len, ctx_len):
    n_lat, d = att.shape
    tpb = l_len // ROW_TILE
    tpb_t = t_len // ROW_TILE
    ctx_tiles = ctx_len // ROW_TILE
    kern = functools.partial(_merge1_kernel, tiles_per_batch=tpb)
    shapes, specs = _tail_outs(n_lat, d)
    full = lambda a: pl.BlockSpec(a.shape, lambda t: (0,) * a.ndim)
    xrow = lambda t: (lax.div(t, tpb) * tpb_t + ctx_tiles + lax.rem(t, tpb), 0)
    return pl.pallas_call(
        kern,
        grid=(n_lat // ROW_TILE,),
        in_specs=[
            pl.BlockSpec((ROW_TILE, d), lambda t: (t, 0)),
            pl.BlockSpec((ROW_TILE, d), xrow),
            pl.BlockSpec((1, SUBLANES, mod.shape[2]), lambda t: (layer, 0, 0)),
            full(w_out), full(lng), full(wr_t), full(rbias),
        ],
        out_specs=specs,
        out_shape=shapes,
        compiler_params=_cparams(("parallel",)),
        name="merge1",
    )(att, x, mod, w_out, lng, wr_t, rbias)


def _moe_plan(eidx):
    k, n = eidx.shape
    a = k * n
    e_flat = eidx.reshape(a)
    seg = TOP_K * ROW_TILE
    onehot = (e_flat[:, None] == jnp.arange(N_EXPERTS, dtype=jnp.int32)[None, :]).astype(F32)
    onehot = onehot.reshape(a // seg, seg, N_EXPERTS)
    tril = lambda m: jnp.tril(jnp.ones((m, m), F32))
    within = jnp.einsum('ij,tjk->tik', tril(seg), onehot)
    seg_total = within[:, -1, :]
    seg_end = jnp.sum(tril(a // seg)[:, :, None] * seg_total[None], axis=1)
    counts = seg_end[-1].astype(jnp.int32)
    csum = within + (seg_end - seg_total)[:, None, :]
    padded = (counts + MOE_ROWS - 1) // MOE_ROWS * MOE_ROWS
    pad_end = jnp.sum(jnp.tril(jnp.ones((N_EXPERTS, N_EXPERTS), jnp.int32)) * padded[None, :], axis=1)
    pad_start = pad_end - padded
    dest = jnp.sum(onehot * (csum - 1.0 + pad_start.astype(F32)[None, None, :]), axis=-1)
    dest = dest.reshape(a).astype(jnp.int32)
    n_blocks = -(-(a + N_EXPERTS * (MOE_ROWS - 1)) // MOE_ROWS)
    first_row = jnp.arange(n_blocks, dtype=jnp.int32) * MOE_ROWS
    block_expert = jnp.minimum(jnp.sum((pad_end[None, :] <= first_row[:, None]).astype(jnp.int32), axis=1),
                               N_EXPERTS - 1)
    return dest.reshape(k, n), block_expert, n_blocks


def _tile_rows_of(dest, tile):
    k, n = dest.shape
    return dest.reshape(k, n // tile, tile).transpose(1, 0, 2).reshape(n // tile, 1, k * tile)


def _dispatch_kernel(dest_ref, h_ref, xs_in_hbm, xs_hbm, sem):
    del xs_in_hbm
    row_tiles = h_ref.shape[0]
    rows = row_tiles * SUBLANES

    def start(i, c):
        for u in range(SUBLANES):
            for choice in range(TOP_K):
                dst = dest_ref[0, 0, choice * rows + i * SUBLANES + u]
                pltpu.make_async_copy(h_ref.at[i, :, u, :], _row(xs_hbm, dst), sem).start(priority=choice)
        return c

    lax.fori_loop(0, row_tiles, start, 0)
    for _ in range(TOP_K):
        pltpu.make_async_copy(h_ref, xs_hbm.at[pl.ds(0, row_tiles)], sem).wait()


def _dispatch(h, dest, n_rows):
    n = h.shape[0] * SUBLANES
    tile = DISPATCH_TILE if n % DISPATCH_TILE == 0 else ROW_TILE
    return pl.pallas_call(
        _dispatch_kernel,
        grid=(n // tile,),
        in_specs=[
            pl.BlockSpec((1, 1, TOP_K * tile), lambda t: (t, 0, 0), memory_space=pltpu.SMEM),
            pl.BlockSpec(_row_shape(tile), lambda t: (t, 0, 0, 0)),
            pl.BlockSpec(memory_space=pl.ANY),
        ],
        out_specs=pl.BlockSpec(memory_space=pl.ANY),
        out_shape=jax.ShapeDtypeStruct(_row_shape(n_rows), jnp.uint32),
        scratch_shapes=[pltpu.SemaphoreType.DMA],
        input_output_aliases={2: 0},
        compiler_params=_cparams(("arbitrary",)),
        name="moe_dispatch",
    )(_tile_rows_of(dest, tile), h, jnp.zeros(_row_shape(n_rows), jnp.uint32))


def _experts_kernel(be_ref, x_ref, wg_ref, wu_ref, wd_ref, o_ref):
    x = _unpack_bf16_pairs(_load_rows(x_ref)).astype(BF16)
    hg = _dot(x, wg_ref[0])
    hu = _dot(x, wu_ref[0])
    _store_rows(o_ref, _pack_bf16_pairs(_dot((_silu(hg) * hu).astype(BF16), wd_ref[0])))


def _experts(xs, block_expert, layer, wg, wu, wd):
    n_blocks = block_expert.shape[0]
    d, dff = wg.shape[2], wg.shape[3]
    rows_spec = pl.BlockSpec(_row_shape(MOE_ROWS), lambda i, be: (i, 0, 0, 0))
    return pl.pallas_call(
        _experts_kernel,
        grid_spec=pltpu.PrefetchScalarGridSpec(
            num_scalar_prefetch=1,
            grid=(n_blocks,),
            in_specs=[
                rows_spec,
                pl.BlockSpec((None, 1, d, dff), lambda i, be: (layer, be[i], 0, 0)),
                pl.BlockSpec((None, 1, d, dff), lambda i, be: (layer, be[i], 0, 0)),
                pl.BlockSpec((None, 1, dff, d), lambda i, be: (layer, be[i], 0, 0)),
            ],
            out_specs=rows_spec,
        ),
        out_shape=jax.ShapeDtypeStruct(xs.shape, jnp.uint32),
        compiler_params=_cparams(("parallel",)),
        name="moe_experts",
    )(block_expert, xs, wg, wu, wd)


def _combine_kernel(dcur_ref, dnxt_ref, x_ref, gt_ref, mod_ref, lng_ref, y_hbm, o_ref, ybuf, sem, *,
                    tiles_per_batch, ctx_tiles, nb):
    t = pl.program_id(0)
    last = pl.num_programs(0) - 1
    rows = x_ref.shape[0]
    n = TOP_K * rows
    d = x_ref.shape[1]
    slot = lax.rem(t, 2)

    row_tiles = rows // SUBLANES

    def start_all(idx_ref, s):
        def start(i, c):
            for choice in range(TOP_K):
                for u in range(SUBLANES):
                    j = choice * rows + i * SUBLANES + u
                    pltpu.make_async_copy(_row(y_hbm, idx_ref[0, 0, j]),
                                          ybuf.at[s, choice * row_tiles + i, :, u, :],
                                          sem.at[s]).start(priority=choice)
            return c

        lax.fori_loop(0, row_tiles, start, 0)

    @pl.when(t == 0)
    def _():
        start_all(dcur_ref, 0)

    @pl.when(t < last)
    def _():
        start_all(dnxt_ref, 1 - slot)

    pltpu.make_async_copy(ybuf.at[slot], ybuf.at[slot], sem.at[slot]).wait()
    r = _mod_row(t, tiles_per_batch, ctx_tiles, nb)
    gt = gt_ref[...]

    def choice_rows(choice):
        tiles = pl.ds(choice * row_tiles, row_tiles)
        return jnp.concatenate([ybuf[slot, tiles, j].reshape(rows, LANES) for j in range(ROW_SUB)], axis=1)

    y = (_unpack_bf16_pairs(choice_rows(0)) * gt[:, 0:1]
         + _unpack_bf16_pairs(choice_rows(1)) * gt[:, 1:2])
    g2 = mod_ref[0, pl.ds(r, 1), 5 * d:6 * d]
    z = ALPHA * x_ref[...] + g2 * y
    mu = jnp.mean(z, axis=-1, keepdims=True)
    zc = z - mu
    var = jnp.mean(zc * zc, axis=-1, keepdims=True)
    o_ref[...] = zc * lax.rsqrt(var + LN_EPS) * lng_ref[...]


def _combine(x1, dest, gates, mod, layer, lng, ys, *, tiles_per_batch, ctx_tiles, nb):
    n, d = x1.shape
    nt = n // ROW_TILE
    dest_t = _tile_rows_of(dest, ROW_TILE)
    kern = functools.partial(_combine_kernel, tiles_per_batch=tiles_per_batch, ctx_tiles=ctx_tiles, nb=nb)
    idx_spec = lambda f: pl.BlockSpec((1, 1, TOP_K * ROW_TILE), f, memory_space=pltpu.SMEM)
    return pl.pallas_call(
        kern,
        grid=(nt,),
        in_specs=[
            idx_spec(lambda t: (t, 0, 0)),
            idx_spec(lambda t: (jnp.minimum(t + 1, nt - 1), 0, 0)),
            pl.BlockSpec((ROW_TILE, d), lambda t: (t, 0)),
            pl.BlockSpec((ROW_TILE, TOP_K), lambda t: (t, 0)),
            pl.BlockSpec((1, SUBLANES, mod.shape[2]), lambda t: (layer, 0, 0)),
            pl.BlockSpec((1, d), lambda t: (0, 0)),
            pl.BlockSpec(memory_space=pl.ANY),
        ],
        out_specs=pl.BlockSpec((ROW_TILE, d), lambda t: (t, 0)),
        out_shape=jax.ShapeDtypeStruct((n, d), F32),
        scratch_shapes=[pltpu.VMEM((2, TOP_K * ROW_TILE // SUBLANES, ROW_SUB, SUBLANES, LANES), jnp.uint32),
                        pltpu.SemaphoreType.DMA((2,))],
        compiler_params=_cparams(("arbitrary",)),
        name="moe_combine%d" % layer,
    )(dest_t, dest_t, x1, gates.T, mod, lng, ys)


def _moe_layer(x1, h2, eidx, gates, mod, layer, lng, wg, wu, wd, *, tiles_per_batch, ctx_tiles, nb):
    dest, block_expert, n_blocks = _moe_plan(eidx)
    xs = _dispatch(h2, dest, n_blocks * MOE_ROWS)
    ys = _experts(xs, block_expert, layer, wg, wu, wd)
    return _combine(x1, dest, gates, mod, layer, lng, ys,
                    tiles_per_batch=tiles_per_batch, ctx_tiles=ctx_tiles, nb=nb)


def _half_norms(x):
    lane = lax.broadcasted_iota(jnp.int32, x.shape, 1)
    sq = x * x
    lo = jnp.sum(jnp.where(lane < DIFF_DH, sq, 0.0), axis=-1, keepdims=True)
    hi = jnp.sum(jnp.where(lane >= DIFF_DH, sq, 0.0), axis=-1, keepdims=True)
    return jnp.sqrt(lo), jnp.sqrt(hi)


def _attn_kernel(lam_ref, q_ref, k_ref, v_ref, g_ref, o_ref, vext, s_buf0, s_buf1,
                 p_buf0, p_buf1, corr_buf0, corr_buf1, m_buf, shift_buf, acc, *, out_scale, ctx_len):
    t_len = k_ref.shape[0]
    nk = t_len // ATT_TK
    dv = v_ref.shape[1]
    tq = acc.shape[1]
    nq = o_ref.shape[0] // tq
    n_tiles = nq * nk

    vext[:, 0:dv] = v_ref[...]
    vext[:, dv:2 * dv] = jnp.ones((t_len, dv), BF16)
    lam = lam_ref[0]
    gain = g_ref[...] * out_scale

    s_bufs, p_bufs, corr_bufs = (s_buf0, s_buf1), (p_buf0, p_buf1), (corr_buf0, corr_buf1)

    def key_rows(kj):
        return pl.ds(pl.multiple_of(kj * ATT_TK, ATT_TK), ATT_TK)

    def query_rows(qi):
        return pl.ds(pl.multiple_of(ctx_len + qi * tq, ROW_TILE), tq)

    def column_max(norms, carry):
        return tuple(jnp.maximum(c, jnp.max(n, axis=0, keepdims=True)) for n, c in zip(norms, carry))

    def key_norms(kj, carry):
        return column_max(_half_norms(k_ref[key_rows(kj), :].astype(F32)), carry)

    zero11 = jnp.zeros((1, 1), F32)
    kmax = lax.fori_loop(0, nk, key_norms, (zero11, zero11))

    def query_shifts(qi, carry):
        shifts = tuple(n * k for n, k in zip(_half_norms(q_ref[query_rows(qi), :].astype(F32)), kmax))
        shift_buf[qi, :, 0:1] = shifts[0]
        shift_buf[qi, :, 1:2] = shifts[1]
        return column_max(shifts, carry)

    worst = lax.fori_loop(0, nq, query_shifts, (zero11, zero11))
    bounded = jnp.max(jnp.maximum(worst[0], worst[1])) <= ATT_SAFE_SHIFT

    def advance(tile):
        qi, kj = tile
        wrap = kj + 1 == nk
        return jnp.where(wrap, qi + 1, qi), jnp.where(wrap, 0, kj + 1)

    def scores(tile, slot):
        qi, kj = tile
        q = q_ref[query_rows(qi), :]
        lane = lax.broadcasted_iota(jnp.int32, q.shape, 1)
        zero = jnp.zeros(q.shape, q.dtype)
        k = k_ref[key_rows(kj), :]
        s_bufs[slot][0] = _dot_nt(jnp.where(lane < DIFF_DH, q, zero), k)
        s_bufs[slot][1] = _dot_nt(jnp.where(lane >= DIFF_DH, q, zero), k)

    def numerators(tile, slot, online):
        qi, kj = tile
        for w in range(2):
            s = s_bufs[slot][w]
            if online:
                m_old = jnp.where(kj == 0, -jnp.inf, m_buf[w])
                m_new = jnp.maximum(m_old, jnp.max(s, axis=-1, keepdims=True))
                corr_bufs[slot][w] = jnp.exp2(m_old - m_new)
                m_buf[w] = m_new
            else:
                m_new = shift_buf[qi, :, w:w + 1]
            p_bufs[slot][w] = jnp.exp2(s - m_new).astype(BF16)

    def values(tile, slot, online):
        qi, kj = tile
        ve = vext[key_rows(kj), :]
        a = []
        for w in range(2):
            keep = corr_bufs[slot][w] if online else jnp.where(kj == 0, 0.0, 1.0)
            a.append(keep * acc[w] + _dot(p_bufs[slot][w], ve))
            acc[w] = a[w]
        o = a[0][:, 0:dv] / a[0][:, dv:2 * dv] - lam * (a[1][:, 0:dv] / a[1][:, dv:2 * dv])
        o = o * lax.rsqrt(jnp.mean(o * o, axis=-1, keepdims=True) + GN_EPS)
        o_ref[pl.ds(pl.multiple_of(qi * tq, tq), tq), :] = (o * gain).astype(BF16)

    def pipeline(online):
        def step(tiles, slot):
            a, b, c = tiles
            values(c, slot, online)
            scores(a, slot)
            numerators(b, 1 - slot, online)
            return advance(a), a, b

        acc[...] = jnp.zeros(acc.shape, F32)
        t0 = (jnp.int32(0), jnp.int32(0))
        t1 = advance(t0)
        scores(t0, 0)
        scores(t1, 1)
        numerators(t0, 0, online)

        def pair(_, tiles):
            return step(step(tiles, 0), 1)

        _, last, prev = lax.fori_loop(0, (n_tiles - 2) // 2, pair, (advance(t1), t1, t0))
        numerators(last, (n_tiles - 1) % 2, online)
        values(prev, n_tiles % 2, online)
        values(last, (n_tiles - 1) % 2, online)

    pl.when(bounded)(functools.partial(pipeline, False))
    pl.when(jnp.logical_not(bounded))(functools.partial(pipeline, True))


def _diff_attention(qkv, lam, subln_g, lambda_init, *, nb, l_len, t_len, ctx_len):
    d = D_MODEL
    tq = ATT_TQ
    dv = 2 * DIFF_DH
    nq = l_len // tq
    assert (nq * (t_len // ATT_TK)) % 2 == 0
    kern = functools.partial(_attn_kernel, out_scale=1.0 - lambda_init, ctx_len=ctx_len)
    return pl.pallas_call(
        kern,
        grid_spec=pltpu.PrefetchScalarGridSpec(
            num_scalar_prefetch=1,
            grid=(nb, DIFF_HEADS),
            in_specs=[
                pl.BlockSpec((t_len, LANES), lambda b, h, lam: (b, h)),
                pl.BlockSpec((t_len, LANES), lambda b, h, lam: (b, DIFF_HEADS + h)),
                pl.BlockSpec((t_len, LANES), lambda b, h, lam: (b, 2 * DIFF_HEADS + h)),
                pl.BlockSpec((1, LANES), lambda b, h, lam: (0, 0)),
            ],
            out_specs=pl.BlockSpec((l_len, LANES), lambda b, h, lam: (b, h)),
            scratch_shapes=[
                pltpu.VMEM((t_len, 2 * dv), BF16),
                pltpu.VMEM((2, tq, ATT_TK), F32), pltpu.VMEM((2, tq, ATT_TK), F32),
                pltpu.VMEM((2, tq, ATT_TK), BF16), pltpu.VMEM((2, tq, ATT_TK), BF16),
                pltpu.VMEM((2, tq, 1), F32), pltpu.VMEM((2, tq, 1), F32),
                pltpu.VMEM((2, tq, 1), F32), pltpu.VMEM((nq, tq, 2), F32),
                pltpu.VMEM((2, tq, 2 * dv), F32),
            ],
        ),
        out_shape=jax.ShapeDtypeStruct((nb * l_len, d), BF16),
        compiler_params=_cparams(("parallel", "parallel"), ATT_VMEM_LIMIT),
        name="diff_attention",
    )(lam, qkv, qkv, qkv, subln_g.reshape(1, LANES).astype(F32))


def _ret_rope_tables(l_len, ctx_len):
    half = RET_DK // 2
    inv = ROPE_BASE ** (-jnp.arange(0, RET_DK, 2, dtype=F32) / RET_DK)
    ang = jnp.arange(l_len, dtype=F32)[:, None] * inv[None, :]
    ang = jnp.concatenate([jnp.zeros((ctx_len, half), F32), ang], axis=0)
    cos64 = jnp.concatenate([jnp.cos(ang), jnp.cos(ang)], axis=1)
    sin64 = jnp.concatenate([-jnp.sin(ang), jnp.sin(ang)], axis=1)
    kscale = RET_DK ** -0.5
    cos = jnp.concatenate([cos64, cos64 * kscale], axis=1)
    sin = jnp.concatenate([sin64, sin64 * kscale], axis=1)
    return cos[None], sin[None]


def _attn_rope_tables(l_len, ctx_len):
    quarter = DIFF_DH // 4
    inv = ROPE_BASE ** (-jnp.arange(0, DIFF_DH // 2, 2, dtype=F32) / (DIFF_DH // 2))
    pos = jnp.arange(l_len)
    ang_r = (pos // GRID_W).astype(F32)[:, None] * inv[None, :]
    ang_c = (pos % GRID_W).astype(F32)[:, None] * inv[None, :]
    pad = lambda a: jnp.concatenate([jnp.zeros((ctx_len, quarter), F32), a], axis=0)
    ang_r, ang_c = pad(ang_r), pad(ang_c)
    cos64 = jnp.concatenate([jnp.cos(ang_r)] * 2 + [jnp.cos(ang_c)] * 2, axis=1)
    sin64 = jnp.concatenate([-jnp.sin(ang_r), jnp.sin(ang_r), -jnp.sin(ang_c), jnp.sin(ang_c)], axis=1)
    cos = jnp.concatenate([cos64, cos64], axis=1)
    sin = jnp.concatenate([sin64, sin64], axis=1)
    qscale = DIFF_DH ** -0.5 * math.log2(math.e)
    return jnp.stack([cos * qscale, cos]), jnp.stack([sin * qscale, sin])


def kernel(x, c, ctx, c_ctx, ada_w, ada_b, ln_g, w_in_ab, ret_decay_logit, s5_lam_re, s5_lam_im,
           s5_log_dt, s5_b_re, s5_b_im, s5_c_re, s5_c_im, s5_d, s5_w_glu, w_out_ab, w_in_c,
           diff_lambda, diff_subln_g, w_out_c, router_w, router_bias, exp_w_gate, exp_w_up,
           exp_w_down):
    nb, l_len, d = x.shape
    ctx_len = ctx.shape[1]
    t_len = ctx_len + l_len
    nt = nb * t_len
    tpb = t_len // ROW_TILE
    ctx_tiles = ctx_len // ROW_TILE
    assert d == D_MODEL and nb < SUBLANES
    assert l_len % ATT_TQ == 0 and ctx_len % ROW_TILE == 0 and t_len % ATT_TK == 0

    x_lat, x_ctx = x.reshape(nb * l_len, d), ctx.reshape(nb * ctx_len, d)
    c_all = jnp.concatenate([c, c_ctx[None].astype(c.dtype)], axis=0)
    c_pad = jnp.zeros((SUBLANES, d), F32).at[:nb + 1].set(c_all)
    mod = _adaln(c_pad, ada_w, ada_b)

    wr_t = router_w.T
    rbias = router_bias.reshape(N_EXPERTS, 1).astype(F32)

    w0 = w_in_ab[0]
    q_w, k_w, v_w, g_w, u_w = jnp.split(w0, (RET_QK_W, 2 * RET_QK_W, 2 * RET_QK_W + RET_V_W,
                                             2 * RET_QK_W + 2 * RET_V_W), axis=1)
    qk_w = jnp.concatenate([q_w.reshape(d, RET_HEADS, RET_DK), k_w.reshape(d, RET_HEADS, RET_DK)],
                           axis=2).reshape(d, 2 * RET_QK_W)
    w0p = jnp.concatenate([qk_w, v_w, g_w, u_w], axis=1).astype(BF16)
    cos0, sin0 = _ret_rope_tables(l_len, ctx_len)
    rope_tab0 = [0] * RET_HEADS + [None] * ((w0p.shape[1] - S5_CH - 2 * RET_QK_W) // LANES)
    proj0, u5 = _inproj((x_lat, x_ctx), mod, 0, w0p, cos0, sin0, rope_tab0, RET_DK // 2,
                        nb=nb, t_len=t_len, ctx_len=ctx_len, s5_cols=S5_CH)

    log_gammas = jax.nn.log_sigmoid(ret_decay_logit[0].astype(F32))
    ret = _retention(proj0, log_gammas, nb=nb, t_len=t_len, ctx_len=ctx_len)

    s5_ops = _s5_operators(s5_lam_re[0], s5_lam_im[0], s5_log_dt[0], s5_b_re[0], s5_b_im[0],
                           s5_c_re[0], s5_c_im[0], s5_d[0])
    s5y = _s5(u5, *s5_ops, nb=nb, nctx=ctx_len // S5_CHUNK)

    x1, h2, eidx, gates = _merge0(ret, proj0, s5y, x_lat, x_ctx, mod, s5_w_glu[0].astype(BF16),
                                  w_out_ab[0].astype(BF16), ln_g[0, 0].reshape(1, d), wr_t, rbias,
                                  nb=nb, t_len=t_len, ctx_len=ctx_len)
    experts = (exp_w_gate.astype(BF16), exp_w_up.astype(BF16), exp_w_down.astype(BF16))
    x2 = _moe_layer(x1, h2, eidx, gates, mod, 0, ln_g[0, 1].reshape(1, d), *experts,
                    tiles_per_batch=tpb, ctx_tiles=ctx_tiles, nb=nb)

    cos1, sin1 = _attn_rope_tables(l_len, ctx_len)
    n_heads_cols = D_MODEL // LANES
    rope_tab1 = [0] * n_heads_cols + [1] * n_heads_cols + [None] * n_heads_cols
    qkv, = _inproj((x2,), mod, 1, w_in_c[0].astype(BF16), cos1, sin1, rope_tab1, DIFF_DH // 4,
                   nb=nb, t_len=t_len, ctx_len=ctx_len)
    lf = diff_lambda[0].astype(F32)
    lambda_init = 0.8 - 0.6 * math.exp(-0.3 * 1)
    lam = (jnp.exp(jnp.sum(lf[0] * lf[1])) - jnp.exp(jnp.sum(lf[2] * lf[3])) + lambda_init).reshape(1)
    att = _diff_attention(qkv, lam, diff_subln_g[0], lambda_init,
                          nb=nb, l_len=l_len, t_len=t_len, ctx_len=ctx_len)
    x3, h3, eidx1, gates1 = _merge1(att, x2, mod, 1, w_out_c[0].astype(BF16), ln_g[1, 0].reshape(1, d),
                                    wr_t, rbias, nb=nb, l_len=l_len, t_len=t_len, ctx_len=ctx_len)
    out = _moe_layer(x3, h3, eidx1, gates1, mod, 1, ln_g[1, 1].reshape(1, d), *experts,
                     tiles_per_batch=l_len // ROW_TILE, ctx_tiles=0, nb=nb)
    return out.reshape(nb, l_len, d)
```

```python
import functools
import math

import jax
import jax.numpy as jnp
import numpy as np
from jax import lax
from jax.experimental import pallas as pl
from jax.experimental.pallas import tpu as pltpu

F32 = jnp.float32
BF16 = jnp.bfloat16

D_MODEL = 1024
DEPTH = 2
GRID_W = 64
ALPHA = (2.0 * DEPTH) ** 0.25
LN_EPS = 1e-5
GN_EPS = 1e-6
ROPE_BASE = 10000.0
RET_DK = 64
RET_DV = 128
RET_HEADS = 6
RET_QK_W = RET_HEADS * RET_DK
RET_V_W = RET_HEADS * RET_DV
S5_CH = 256
S5_P = 16
S5_G = 16
S5_N = 64
DIFF_HEADS = 8
DIFF_DH = 64
N_EXPERTS = 16
EXPERTS_PER_GROUP = 4
TOP_K = 2

LANES = 128
SUBLANES = 8
MXU_DIM = 256
ROW_TILE = 256
RET_CHUNK = 256
RET_STATE_UNROLL = 3
RET_OUT_UNROLL = 11
S5_CHUNK = 8
S5_HALVES = 2
MOE_ROWS = 256
DISPATCH_TILE = 512
ROW_WORDS = D_MODEL // 2
ROW_SUB = ROW_WORDS // LANES
ATT_TQ = 1024
ATT_TK = 768
ATT_SAFE_SHIFT = 48.0
VMEM_LIMIT = 48 * 1024 * 1024
ATT_VMEM_LIMIT = 56 * 1024 * 1024


def _cparams(sem, vmem_limit=VMEM_LIMIT):
    return pltpu.CompilerParams(dimension_semantics=sem, vmem_limit_bytes=vmem_limit)


def _dot(a, b):
    return jnp.dot(a, b, preferred_element_type=F32)


def _dot_nt(a, b):
    return lax.dot_general(a, b, (((1,), (1,)), ((), ())), preferred_element_type=F32)


def _dot_tn(a, b):
    return lax.dot_general(a, b, (((0,), (0,)), ((), ())), preferred_element_type=F32)


def _split_bf16(x):
    hi = x.astype(BF16)
    lo = (x - hi.astype(F32)).astype(BF16)
    return hi, lo


def _dot3(a, b):
    ah, al = _split_bf16(a)
    bh, bl = _split_bf16(b)
    return _dot(ah, bh) + _dot(ah, bl) + _dot(al, bh)


def _dot3_nt(a, b):
    ah, al = _split_bf16(a)
    bh, bl = _split_bf16(b)
    return _dot_nt(ah, bh) + _dot_nt(ah, bl) + _dot_nt(al, bh)


def _sigmoid(x):
    return 1.0 / (1.0 + jnp.exp(-x))


def _silu(x):
    return x * _sigmoid(x)


def _pack_bf16_pairs(v):
    half = v.shape[1] // 2
    bits = lax.bitcast_convert_type(v.astype(BF16).astype(F32), jnp.uint32)
    return (bits[:, :half] >> 16) | (bits[:, half:] & jnp.uint32(0xFFFF0000))


def _unpack_bf16_pairs(p):
    lo = lax.bitcast_convert_type(p << 16, F32)
    hi = lax.bitcast_convert_type(p & jnp.uint32(0xFFFF0000), F32)
    return jnp.concatenate([lo, hi], axis=1)


def _row_shape(n_rows):
    return (n_rows // SUBLANES, ROW_SUB, SUBLANES, LANES)


def _row(ref, r):
    return ref.at[lax.shift_right_logical(r, 3), :, r & (SUBLANES - 1), :]


def _store_rows(ref, packed):
    rows = packed.shape[0]
    for j in range(ROW_SUB):
        ref[:, j] = packed[:, j * LANES:(j + 1) * LANES].reshape(rows // SUBLANES, SUBLANES, LANES)


def _load_rows(ref):
    rows = ref.shape[0] * SUBLANES
    return jnp.concatenate([ref[:, j].reshape(rows, LANES) for j in range(ROW_SUB)], axis=1)


def _gelu_tanh(x):
    c = math.sqrt(2.0 / math.pi)
    return 0.5 * x * (1.0 + jnp.tanh(c * (x + 0.044715 * (x * x * x))))


def _adaln_kernel(c_ref, w_ref, b_ref, o_ref):
    c = c_ref[...]
    o_ref[0] = _dot3(_silu(c), w_ref[0]) + b_ref[0]


def _adaln(c_pad, ada_w, ada_b):
    depth, d, n = ada_w.shape
    tn = 1536
    return pl.pallas_call(
        _adaln_kernel,
        grid=(depth, n // tn),
        in_specs=[
            pl.BlockSpec((SUBLANES, d), lambda i, j: (0, 0)),
            pl.BlockSpec((1, d, tn), lambda i, j: (i, 0, j)),
            pl.BlockSpec((1, 1, tn), lambda i, j: (i, 0, j)),
        ],
        out_specs=pl.BlockSpec((1, SUBLANES, tn), lambda i, j: (i, 0, j)),
        out_shape=jax.ShapeDtypeStruct((depth, SUBLANES, n), F32),
        compiler_params=_cparams(("parallel", "parallel")),
        name="adaln",
    )(c_pad, ada_w, ada_b.reshape(depth, 1, n))


def _mod_row(t, tiles_per_batch, ctx_tiles, nb):
    b = lax.div(t, tiles_per_batch)
    w = lax.rem(t, tiles_per_batch)
    return jnp.where(w < ctx_tiles, nb, b)


def _rope_block(a, cos, sin, half):
    lane = lax.broadcasted_iota(jnp.int32, a.shape, 1)
    first = lax.rem(lane, 2 * half) < half
    rot = jnp.where(first, pltpu.roll(a, LANES - half, 1), pltpu.roll(a, half, 1))
    return a * cos + rot * sin


def _chunk_perm(rows, transpose):
    per = rows // S5_CHUNK
    r = lax.broadcasted_iota(jnp.int32, (rows, rows), 0)
    c = lax.broadcasted_iota(jnp.int32, (rows, rows), 1)
    if transpose:
        r, c = c, r
    return (c == S5_CHUNK * lax.rem(r, per) + lax.div(r, per)).astype(BF16)


def _token_specs(d, tiles_per_batch, ctx_tiles):
    lat_tiles = tiles_per_batch - ctx_tiles
    b = lambda t: lax.div(t, tiles_per_batch)
    w = lambda t: lax.rem(t, tiles_per_batch)
    return (pl.BlockSpec((ROW_TILE, d), lambda t: (b(t) * lat_tiles + jnp.maximum(w(t) - ctx_tiles, 0), 0)),
            pl.BlockSpec((ROW_TILE, d), lambda t: (b(t) * ctx_tiles + jnp.minimum(w(t), ctx_tiles - 1), 0)))


def _token_tile(x_refs, tiles_per_batch, ctx_tiles):
    if len(x_refs) == 1:
        return x_refs[0][...]
    is_ctx = lax.rem(pl.program_id(0), tiles_per_batch) < ctx_tiles
    return jnp.where(is_ctx, x_refs[1][...], x_refs[0][...])


def _inproj_kernel(*refs, n_x, tiles_per_batch, ctx_tiles, nb, rope_tab, rope_half):
    x_refs, (mod_ref, w_ref, cos_ref, sin_ref, o_ref), rest = refs[:n_x], refs[n_x:n_x + 5], refs[n_x + 5:]
    d = w_ref.shape[0]
    n = o_ref.shape[1]
    r = _mod_row(pl.program_id(0), tiles_per_batch, ctx_tiles, nb)
    sh = mod_ref[0, pl.ds(r, 1), 0:d]
    sc = mod_ref[0, pl.ds(r, 1), d:2 * d]
    xm = (_token_tile(x_refs, tiles_per_batch, ctx_tiles) * (1.0 + sc) + sh).astype(BF16)
    for j in range(n // MXU_DIM):
        acc = _dot(xm, w_ref[:, j * MXU_DIM:(j + 1) * MXU_DIM])
        parts = []
        for s in range(MXU_DIM // LANES):
            blk = acc[:, s * LANES:(s + 1) * LANES]
            tab = rope_tab[j * (MXU_DIM // LANES) + s]
            if tab is not None:
                blk = _rope_block(blk, cos_ref[tab], sin_ref[tab], rope_half)
            parts.append(blk)
        o_ref[:, j * MXU_DIM:(j + 1) * MXU_DIM] = jnp.concatenate(parts, axis=1).astype(BF16)
    if rest:
        u_ref, = rest
        rows = o_ref.shape[0]
        per = rows // S5_CHUNK
        u = _dot(xm, w_ref[:, n:n + S5_CH]).astype(BF16)
        up = _dot(_chunk_perm(rows, False), u).astype(BF16)
        for s in range(S5_CHUNK):
            for hf in range(S5_CH // LANES):
                u_ref[hf, :, s * LANES:(s + 1) * LANES] = up[s * per:(s + 1) * per,
                                                             hf * LANES:(hf + 1) * LANES]


def _inproj(xs, mod, layer, w, cos, sin, rope_tab, rope_half, *, nb, t_len, ctx_len, s5_cols=0):
    nt, d = nb * t_len, w.shape[0]
    n = w.shape[1] - s5_cols
    tpb = t_len // ROW_TILE
    ctx_tiles = ctx_len // ROW_TILE
    kern = functools.partial(_inproj_kernel, n_x=len(xs), tiles_per_batch=tpb, ctx_tiles=ctx_tiles,
                             nb=nb, rope_tab=tuple(rope_tab), rope_half=rope_half)
    x_specs = ([pl.BlockSpec((ROW_TILE, d), lambda t: (t, 0))] if len(xs) == 1
               else list(_token_specs(d, tpb, ctx_tiles)))
    ntab = cos.shape[0]
    out_specs = [pl.BlockSpec((ROW_TILE, n), lambda t: (t, 0))]
    out_shape = [jax.ShapeDtypeStruct((nt, n), BF16)]
    if s5_cols:
        halves, per = s5_cols // LANES, ROW_TILE // S5_CHUNK
        out_specs.append(pl.BlockSpec((halves, per, S5_CHUNK * LANES), lambda t: (0, t, 0)))
        out_shape.append(jax.ShapeDtypeStruct((halves, nt // S5_CHUNK, S5_CHUNK * LANES), BF16))
    return pl.pallas_call(
        kern,
        grid=(nt // ROW_TILE,),
        in_specs=x_specs + [
            pl.BlockSpec((1, SUBLANES, mod.shape[2]), lambda t: (layer, 0, 0)),
            pl.BlockSpec(w.shape, lambda t: (0, 0)),
            pl.BlockSpec((ntab, ROW_TILE, LANES), lambda t: (0, lax.rem(t, tpb), 0)),
            pl.BlockSpec((ntab, ROW_TILE, LANES), lambda t: (0, lax.rem(t, tpb), 0)),
        ],
        out_specs=out_specs,
        out_shape=out_shape,
        compiler_params=_cparams(("parallel",)),
        name="inproj%d" % layer,
    )(*xs, mod, w, cos, sin)


def _retention_kernel(lg_ref, qk_ref, v_ref, o_ref, sf_ref, sb_ref, *, nctx):
    c_len = RET_CHUNK
    t_len = qk_ref.shape[0]
    nc = t_len // c_len
    h = pl.program_id(1)
    lgf = lg_ref[0, h]
    lgb = lg_ref[1, h]
    ii = lax.broadcasted_iota(jnp.int32, (c_len, 1), 0).astype(F32)
    jj = lax.broadcasted_iota(jnp.int32, (1, c_len), 1).astype(F32)
    diff = ii - jj
    decay = jnp.where(diff >= 0.0, jnp.exp(lgf * jnp.maximum(diff, 0.0)),
                      jnp.exp(lgb * jnp.maximum(-diff, 0.0)))
    kdf = jnp.exp(lgf * (c_len - 1.0 - ii))
    kdb = jnp.exp(lgb * ii)
    qdf = jnp.exp(lgf * (ii + 1.0))
    qdb = jnp.exp(lgb * (c_len - ii))
    zrow = jnp.zeros((1, RET_DV), F32)
    gf_chunk = jnp.exp(zrow + lgf * c_len)
    gb_chunk = jnp.exp(zrow + lgb * c_len)

    def load(c):
        rows = pl.ds(pl.multiple_of(c * c_len, c_len), c_len)
        qk = qk_ref[rows, :].astype(F32)
        return qk[:, :RET_DK], qk[:, RET_DK:], v_ref[rows, :]

    def states(j, carry):
        sf, sb = carry
        cb = jnp.where(j < nctx, nctx - 1 - j, nc - 1 - (j - nctx))
        sf_ref[j] = sf
        sb_ref[cb] = sb
        _, kf, vf = load(j)
        _, kb, vb = load(cb)
        return (gf_chunk * sf + _dot_tn((kf * kdf).astype(BF16), vf),
                gb_chunk * sb + _dot_tn((kb * kdb).astype(BF16), vb))

    zero_state = jnp.zeros((RET_DK, RET_DV), F32)
    lax.fori_loop(0, nc, states, (zero_state, zero_state), unroll=RET_STATE_UNROLL)

    def out_chunk(c, carry):
        q, k, v = load(c)
        scores = _dot_nt(q.astype(BF16), k.astype(BF16)) * decay
        o = _dot(scores.astype(BF16), v)
        o = o + _dot((q * qdf).astype(BF16), sf_ref[c].astype(BF16))
        o = o + _dot((q * qdb).astype(BF16), sb_ref[c].astype(BF16))
        mu = jnp.mean(o, axis=-1, keepdims=True)
        oc = o - mu
        var = jnp.mean(oc * oc, axis=-1, keepdims=True)
        rows = pl.ds(pl.multiple_of(c * c_len, c_len), c_len)
        o_ref[rows, :] = (oc * lax.rsqrt(var + GN_EPS)).astype(BF16)
        return carry

    lax.fori_loop(0, nc, out_chunk, 0, unroll=RET_OUT_UNROLL)


def _retention(proj, log_gammas, *, nb, t_len, ctx_len):
    nt = proj.shape[0]
    nc = t_len // RET_CHUNK
    kern = functools.partial(_retention_kernel, nctx=ctx_len // RET_CHUNK)
    vcol0 = RET_HEADS
    return pl.pallas_call(
        kern,
        grid_spec=pltpu.PrefetchScalarGridSpec(
            num_scalar_prefetch=1,
            grid=(nb, RET_HEADS),
            in_specs=[
                pl.BlockSpec((t_len, LANES), lambda b, h, lg: (b, h)),
                pl.BlockSpec((t_len, LANES), lambda b, h, lg: (b, vcol0 + h)),
            ],
            out_specs=pl.BlockSpec((t_len, LANES), lambda b, h, lg: (b, h)),
            scratch_shapes=[pltpu.VMEM((nc, RET_DK, RET_DV), F32),
                            pltpu.VMEM((nc, RET_DK, RET_DV), F32)],
        ),
        out_shape=jax.ShapeDtypeStruct((nt, RET_V_W), BF16),
        compiler_params=_cparams(("parallel", "parallel")),
        name="retention",
    )(log_gammas, proj, proj)


def _s5_operators(lam_re, lam_im, log_dt, b_re, b_im, c_re, c_im, d_skip):
    tc = S5_CHUNK
    hp = lax.Precision.HIGHEST
    ks = jnp.arange(tc + 1, dtype=F32)
    pw, bbar, cm = [], [], []
    for direction in range(2):
        dt = jnp.exp(log_dt[direction].astype(F32))[:, None]
        lam = lax.complex(lam_re[direction].astype(F32), lam_im[direction].astype(F32))
        z = lam * dt
        p = jnp.exp(z[None] * ks[:, None, None])
        lam_bar = p[1]
        bb = ((lam_bar - 1.0) / lam)[..., None] * lax.complex(
            b_re[direction].astype(F32), b_im[direction].astype(F32))
        pw.append(p)
        bbar.append(bb)
        cm.append(lax.complex(c_re[direction].astype(F32), c_im[direction].astype(F32)))

    def lag_kernel(p, bb, c):
        return jnp.einsum('gpn,kgn,gnq->kgpq', c, p[:tc], bb, precision=hp).real

    kf = lag_kernel(pw[0], bbar[0], cm[0])
    kb = lag_kernel(pw[1], bbar[1], cm[1])
    k0 = kf[0] + kb[0] + jnp.eye(S5_P, dtype=F32)[None] * d_skip.astype(F32)[:, :, None]
    kcat = jnp.concatenate([kb[1:][::-1], k0[None], kf[1:]], axis=0)
    s_idx = jnp.arange(tc)[:, None]
    t_idx = jnp.arange(tc)[None, :]
    m5 = kcat[t_idx - s_idx + tc - 1]
    hg = S5_G // S5_HALVES
    eye = jnp.eye(hg, dtype=F32)
    split = lambda z, axis: z.reshape(z.shape[:axis] + (S5_HALVES, hg) + z.shape[axis + 1:])
    wide = tc * hg * S5_P
    intra = jnp.einsum('sthgpq,gk->hsgqtkp', split(m5, 2), eye).reshape(S5_HALVES, wide, wide)

    ef = pw[0][:tc][::-1][:, :, :, None] * bbar[0][None]
    eb = pw[1][:tc][:, :, :, None] * bbar[1][None]
    parts_in = jnp.stack([ef.real, ef.imag, eb.real, eb.imag], axis=0)
    w_in = jnp.einsum('cshgnq,gk->hsgqckn', split(parts_in, 2), eye).reshape(
        S5_HALVES, wide, 4 * hg * S5_N)
    of = cm[0][None] * pw[0][1:][:, :, None, :]
    ob = cm[1][None] * pw[1][1:][::-1][:, :, None, :]
    parts_out = jnp.stack([of.real, -of.imag, ob.real, -ob.imag], axis=0)
    w_out = jnp.einsum('cthgpn,gk->hckntgp', split(parts_out, 2), eye).reshape(
        S5_HALVES, 4 * hg * S5_N, wide)
    a = jnp.stack([pw[0][tc].real, pw[0][tc].imag, pw[1][tc].real, pw[1][tc].imag], axis=0)
    a = a.reshape(4, S5_HALVES, hg * S5_N).transpose(1, 0, 2)
    return intra.astype(BF16), w_in.astype(BF16), w_out.astype(BF16), a


def _s5_kernel(x_ref, wi_ref, win_ref, wout_ref, a_ref, y_ref, st_ref, *, nctx):
    x = x_ref[0]
    nc = x.shape[0]
    w = a_ref.shape[2]
    st_ref[...] = _dot(x, win_ref[0])
    afr, afi, abr, abi = (a_ref[0, i:i + 1, :] for i in range(4))

    def step(j, carry):
        fr, fi, br, bi = carry
        rf = pl.ds(j, 1)
        rb = pl.ds(jnp.where(j < nctx, nctx - 1 - j, nc - 1 - (j - nctx)), 1)
        efr, efi = st_ref[rf, 0:w], st_ref[rf, w:2 * w]
        ebr, ebi = st_ref[rb, 2 * w:3 * w], st_ref[rb, 3 * w:4 * w]
        st_ref[rf, 0:w] = fr
        st_ref[rf, w:2 * w] = fi
        st_ref[rb, 2 * w:3 * w] = br
        st_ref[rb, 3 * w:4 * w] = bi
        return (afr * fr - afi * fi + efr, afr * fi + afi * fr + efi,
                abr * br - abi * bi + ebr, abr * bi + abi * br + ebi)

    z = jnp.zeros((1, w), F32)
    lax.fori_loop(0, nc, step, (z, z, z, z), unroll=4)
    y_ref[0] = (_dot(x, wi_ref[0]) + _dot(st_ref[...].astype(BF16), wout_ref[0])).astype(BF16)


def _s5(xc, intra, w_in, w_out, a, *, nb, nctx):
    halves, rows, wide = xc.shape
    nc = rows // nb
    kern = functools.partial(_s5_kernel, nctx=nctx)
    per_half = lambda arr: pl.BlockSpec((1,) + arr.shape[1:], lambda hf, b: (hf, 0, 0))
    return pl.pallas_call(
        kern,
        grid=(halves, nb),
        in_specs=[
            pl.BlockSpec((1, nc, wide), lambda hf, b: (hf, b, 0)),
            per_half(intra), per_half(w_in), per_half(w_out), per_half(a),
        ],
        out_specs=pl.BlockSpec((1, nc, wide), lambda hf, b: (hf, b, 0)),
        out_shape=jax.ShapeDtypeStruct(xc.shape, BF16),
        scratch_shapes=[pltpu.VMEM((nc, w_in.shape[2]), F32)],
        compiler_params=_cparams(("parallel", "parallel")),
        name="s5",
    )(xc, intra, w_in, w_out, a)


def _route(logits_t, bias):
    scores = _sigmoid(logits_t)
    biased = scores + bias
    s_rows = [scores[e:e + 1, :] for e in range(N_EXPERTS)]
    b_rows = [biased[e:e + 1, :] for e in range(N_EXPERTS)]
    n_groups = N_EXPERTS // EXPERTS_PER_GROUP
    best = None
    sel = None
    for g in range(n_groups):
        a, b, c, d = b_rows[4 * g:4 * g + 4]
        hi1, lo1 = jnp.maximum(a, b), jnp.minimum(a, b)
        hi2, lo2 = jnp.maximum(c, d), jnp.minimum(c, d)
        top1 = jnp.maximum(hi1, hi2)
        top2 = jnp.maximum(jnp.minimum(hi1, hi2), jnp.maximum(lo1, lo2))
        gs = top1 + top2
        if g == 0:
            best, sel = gs, jnp.zeros(gs.shape, jnp.int32)
        else:
            better = gs > best
            sel = jnp.where(better, g, sel)
            best = jnp.where(better, gs, best)
    neg = jnp.full(best.shape, -jnp.inf, F32)
    masked = [jnp.where(sel == (e // EXPERTS_PER_GROUP), b_rows[e], neg) for e in range(N_EXPERTS)]
    v1, i1, g1 = masked[0], jnp.zeros(best.shape, jnp.int32), s_rows[0]
    for e in range(1, N_EXPERTS):
        better = masked[e] > v1
        v1 = jnp.where(better, masked[e], v1)
        i1 = jnp.where(better, e, i1)
        g1 = jnp.where(better, s_rows[e], g1)
    v2, i2, g2 = neg, jnp.zeros(best.shape, jnp.int32), jnp.zeros(best.shape, F32)
    for e in range(N_EXPERTS):
        cand = jnp.where(i1 == e, neg, masked[e])
        better = cand > v2
        v2 = jnp.where(better, cand, v2)
        i2 = jnp.where(better, e, i2)
        g2 = jnp.where(better, s_rows[e], g2)
    tot = g1 + g2
    return jnp.concatenate([i1, i2], axis=0), jnp.concatenate([g1 / tot, g2 / tot], axis=0)


def _tail(x, o, mod_ref, r, lng, wr, rb, x1_ref, h2_ref, ei_ref, gt_ref):
    d = x.shape[1]
    g1 = mod_ref[0, pl.ds(r, 1), 2 * d:3 * d]
    sh2 = mod_ref[0, pl.ds(r, 1), 3 * d:4 * d]
    sc2 = mod_ref[0, pl.ds(r, 1), 4 * d:5 * d]
    y = ALPHA * x + g1 * o
    mu = jnp.mean(y, axis=-1, keepdims=True)
    yc = y - mu
    var = jnp.mean(yc * yc, axis=-1, keepdims=True)
    x1 = yc * lax.rsqrt(var + LN_EPS) * lng
    h2 = x1 * (1.0 + sc2) + sh2
    x1_ref[...] = x1
    _store_rows(h2_ref, _pack_bf16_pairs(h2))
    ei, gt = _route(_dot3_nt(wr, h2), rb)
    ei_ref[...] = ei
    gt_ref[...] = gt


def _merge0_kernel(r_ref, g_ref, s_ref, x_ref, xc_ref, mod_ref, wglu_ref, wout_ref, lng_ref, wr_ref,
                   rb_ref, x1_ref, h2_ref, ei_ref, gt_ref, *, tiles_per_batch, ctx_tiles, nb):
    r = _mod_row(pl.program_id(0), tiles_per_batch, ctx_tiles, nb)
    ret = r_ref[...].astype(F32) * _silu(g_ref[...].astype(F32))
    rows = x_ref.shape[0]
    x = _token_tile((x_ref, xc_ref), tiles_per_batch, ctx_tiles)
    sp = jnp.concatenate(
        [jnp.concatenate([s_ref[hf, :, s * LANES:(s + 1) * LANES] for hf in range(S5_HALVES)], axis=1)
         for s in range(S5_CHUNK)], axis=0)
    s5 = _dot(_chunk_perm(rows, True), sp)
    z = _dot(_gelu_tanh(s5).astype(BF16), wglu_ref[...])
    zz = z[:, :S5_CH] * _sigmoid(z[:, S5_CH:])
    o = _dot(ret.astype(BF16), wout_ref[0:RET_V_W, :]) + _dot(zz.astype(BF16), wout_ref[RET_V_W:, :])
    _tail(x, o, mod_ref, r, lng_ref[...], wr_ref[...], rb_ref[...],
          x1_ref, h2_ref, ei_ref, gt_ref)


def _merge1_kernel(a_ref, x_ref, mod_ref, wout_ref, lng_ref, wr_ref, rb_ref,
                   x1_ref, h2_ref, ei_ref, gt_ref, *, tiles_per_batch):
    r = lax.div(pl.program_id(0), tiles_per_batch)
    o = _dot(a_ref[...], wout_ref[...])
    _tail(x_ref[...], o, mod_ref, r, lng_ref[...], wr_ref[...], rb_ref[...],
          x1_ref, h2_ref, ei_ref, gt_ref)


def _tail_outs(n_rows, d):
    shapes = (jax.ShapeDtypeStruct((n_rows, d), F32),
              jax.ShapeDtypeStruct(_row_shape(n_rows), jnp.uint32),
              jax.ShapeDtypeStruct((TOP_K, n_rows), jnp.int32),
              jax.ShapeDtypeStruct((TOP_K, n_rows), F32))
    specs = (pl.BlockSpec((ROW_TILE, d), lambda t: (t, 0)),
             pl.BlockSpec(_row_shape(ROW_TILE), lambda t: (t, 0, 0, 0)),
             pl.BlockSpec((TOP_K, ROW_TILE), lambda t: (0, t)),
             pl.BlockSpec((TOP_K, ROW_TILE), lambda t: (0, t)))
    return shapes, specs


def _merge0(ret, proj, s5y, x_lat, x_ctx, mod, w_glu, w_out, lng, wr_t, rbias, *, nb, t_len, ctx_len):
    nt, d = nb * t_len, x_lat.shape[1]
    tpb = t_len // ROW_TILE
    ctx_tiles = ctx_len // ROW_TILE
    kern = functools.partial(_merge0_kernel, tiles_per_batch=tpb, ctx_tiles=ctx_tiles, nb=nb)
    shapes, specs = _tail_outs(nt, d)
    gcol = (2 * RET_QK_W + RET_V_W) // RET_V_W
    full = lambda a: pl.BlockSpec(a.shape, lambda t: (0,) * a.ndim)
    return pl.pallas_call(
        kern,
        grid=(nt // ROW_TILE,),
        in_specs=[
            pl.BlockSpec((ROW_TILE, RET_V_W), lambda t: (t, 0)),
            pl.BlockSpec((ROW_TILE, RET_V_W), lambda t: (t, gcol)),
            pl.BlockSpec((S5_HALVES, ROW_TILE // S5_CHUNK, S5_CHUNK * LANES), lambda t: (0, t, 0)),
            *_token_specs(d, tpb, ctx_tiles),
            pl.BlockSpec((1, SUBLANES, mod.shape[2]), lambda t: (0, 0, 0)),
            full(w_glu), full(w_out), full(lng), full(wr_t), full(rbias),
        ],
        out_specs=specs,
        out_shape=shapes,
        compiler_params=_cparams(("parallel",)),
        name="merge0",
    )(ret, proj, s5y, x_lat, x_ctx, mod, w_glu, w_out, lng, wr_t, rbias)


def _merge1(att, x, mod, layer, w_out, lng, wr_t, rbias, *, nb, l_len, t_len, ctx_len):
    n_lat, d = att.shape
    tpb = l_len // ROW_TILE
    tpb_t = t_len // ROW_TILE
    ctx_tiles = ctx_len // ROW_TILE
    kern = functools.partial(_merge1_kernel, tiles_per_batch=tpb)
    shapes, specs = _tail_outs(n_lat, d)
    full = lambda a: pl.BlockSpec(a.shape, lambda t: (0,) * a.ndim)
    xrow = lambda t: (lax.div(t, tpb) * tpb_t + ctx_tiles + lax.rem(t, tpb), 0)
    return pl.pallas_call(
        kern,
        grid=(n_lat // ROW_TILE,),
        in_specs=[
            pl.BlockSpec((ROW_TILE, d), lambda t: (t, 0)),
            pl.BlockSpec((ROW_TILE, d), xrow),
            pl.BlockSpec((1, SUBLANES, mod.shape[2]), lambda t: (layer, 0, 0)),
            full(w_out), full(lng), full(wr_t), full(rbias),
        ],
        out_specs=specs,
        out_shape=shapes,
        compiler_params=_cparams(("parallel",)),
        name="merge1",
    )(att, x, mod, w_out, lng, wr_t, rbias)


def _moe_plan(eidx):
    k, n = eidx.shape
    a = k * n
    e_flat = eidx.reshape(a)
    seg = TOP_K * ROW_TILE
    onehot = (e_flat[:, None] == jnp.arange(N_EXPERTS, dtype=jnp.int32)[None, :]).astype(F32)
    onehot = onehot.reshape(a // seg, seg, N_EXPERTS)
    tril = lambda m: jnp.tril(jnp.ones((m, m), F32))
    within = jnp.einsum('ij,tjk->tik', tril(seg), onehot)
    seg_total = within[:, -1, :]
    seg_end = jnp.sum(tril(a // seg)[:, :, None] * seg_total[None], axis=1)
    counts = seg_end[-1].astype(jnp.int32)
    csum = within + (seg_end - seg_total)[:, None, :]
    padded = (counts + MOE_ROWS - 1) // MOE_ROWS * MOE_ROWS
    pad_end = jnp.sum(jnp.tril(jnp.ones((N_EXPERTS, N_EXPERTS), jnp.int32)) * padded[None, :], axis=1)
    pad_start = pad_end - padded
    dest = jnp.sum(onehot * (csum - 1.0 + pad_start.astype(F32)[None, None, :]), axis=-1)
    dest = dest.reshape(a).astype(jnp.int32)
    n_blocks = -(-(a + N_EXPERTS * (MOE_ROWS - 1)) // MOE_ROWS)
    first_row = jnp.arange(n_blocks, dtype=jnp.int32) * MOE_ROWS
    block_expert = jnp.minimum(jnp.sum((pad_end[None, :] <= first_row[:, None]).astype(jnp.int32), axis=1),
                               N_EXPERTS - 1)
    return dest.reshape(k, n), block_expert, n_blocks


def _tile_rows_of(dest, tile):
    k, n = dest.shape
    return dest.reshape(k, n // tile, tile).transpose(1, 0, 2).reshape(n // tile, 1, k * tile)


def _dispatch_kernel(dest_ref, h_ref, xs_in_hbm, xs_hbm, sem):
    del xs_in_hbm
    row_tiles = h_ref.shape[0]
    rows = row_tiles * SUBLANES

    def start(i, c):
        for u in range(SUBLANES):
            for choice in range(TOP_K):
                dst = dest_ref[0, 0, choice * rows + i * SUBLANES + u]
                pltpu.make_async_copy(h_ref.at[i, :, u, :], _row(xs_hbm, dst), sem).start(priority=choice)
        return c

    lax.fori_loop(0, row_tiles, start, 0)
    for _ in range(TOP_K):
        pltpu.make_async_copy(h_ref, xs_hbm.at[pl.ds(0, row_tiles)], sem).wait()


def _dispatch(h, dest, n_rows):
    n = h.shape[0] * SUBLANES
    tile = DISPATCH_TILE if n % DISPATCH_TILE == 0 else ROW_TILE
    return pl.pallas_call(
        _dispatch_kernel,
        grid=(n // tile,),
        in_specs=[
            pl.BlockSpec((1, 1, TOP_K * tile), lambda t: (t, 0, 0), memory_space=pltpu.SMEM),
            pl.BlockSpec(_row_shape(tile), lambda t: (t, 0, 0, 0)),
            pl.BlockSpec(memory_space=pl.ANY),
        ],
        out_specs=pl.BlockSpec(memory_space=pl.ANY),
        out_shape=jax.ShapeDtypeStruct(_row_shape(n_rows), jnp.uint32),
        scratch_shapes=[pltpu.SemaphoreType.DMA],
        input_output_aliases={2: 0},
        compiler_params=_cparams(("arbitrary",)),
        name="moe_dispatch",
    )(_tile_rows_of(dest, tile), h, jnp.zeros(_row_shape(n_rows), jnp.uint32))


def _experts_kernel(be_ref, x_ref, wg_ref, wu_ref, wd_ref, o_ref):
    x = _unpack_bf16_pairs(_load_rows(x_ref)).astype(BF16)
    hg = _dot(x, wg_ref[0])
    hu = _dot(x, wu_ref[0])
    _store_rows(o_ref, _pack_bf16_pairs(_dot((_silu(hg) * hu).astype(BF16), wd_ref[0])))


def _experts(xs, block_expert, layer, wg, wu, wd):
    n_blocks = block_expert.shape[0]
    d, dff = wg.shape[2], wg.shape[3]
    rows_spec = pl.BlockSpec(_row_shape(MOE_ROWS), lambda i, be: (i, 0, 0, 0))
    return pl.pallas_call(
        _experts_kernel,
        grid_spec=pltpu.PrefetchScalarGridSpec(
            num_scalar_prefetch=1,
            grid=(n_blocks,),
            in_specs=[
                rows_spec,
                pl.BlockSpec((None, 1, d, dff), lambda i, be: (layer, be[i], 0, 0)),
                pl.BlockSpec((None, 1, d, dff), lambda i, be: (layer, be[i], 0, 0)),
                pl.BlockSpec((None, 1, dff, d), lambda i, be: (layer, be[i], 0, 0)),
            ],
            out_specs=rows_spec,
        ),
        out_shape=jax.ShapeDtypeStruct(xs.shape, jnp.uint32),
        compiler_params=_cparams(("parallel",)),
        name="moe_experts",
    )(block_expert, xs, wg, wu, wd)


def _combine_kernel(dcur_ref, dnxt_ref, x_ref, gt_ref, mod_ref, lng_ref, y_hbm, o_ref, ybuf, sem, *,
                    tiles_per_batch, ctx_tiles, nb):
    t = pl.program_id(0)
    last = pl.num_programs(0) - 1
    rows = x_ref.shape[0]
    n = TOP_K * rows
    d = x_ref.shape[1]
    slot = lax.rem(t, 2)

    row_tiles = rows // SUBLANES

    def start_all(idx_ref, s):
        def start(i, c):
            for choice in range(TOP_K):
                for u in range(SUBLANES):
                    j = choice * rows + i * SUBLANES + u
                    pltpu.make_async_copy(_row(y_hbm, idx_ref[0, 0, j]),
                                          ybuf.at[s, choice * row_tiles + i, :, u, :],
                                          sem.at[s]).start(priority=choice)
            return c

        lax.fori_loop(0, row_tiles, start, 0)

    @pl.when(t == 0)
    def _():
        start_all(dcur_ref, 0)

    @pl.when(t < last)
    def _():
        start_all(dnxt_ref, 1 - slot)

    pltpu.make_async_copy(ybuf.at[slot], ybuf.at[slot], sem.at[slot]).wait()
    r = _mod_row(t, tiles_per_batch, ctx_tiles, nb)
    gt = gt_ref[...]

    def choice_rows(choice):
        tiles = pl.ds(choice * row_tiles, row_tiles)
        return jnp.concatenate([ybuf[slot, tiles, j].reshape(rows, LANES) for j in range(ROW_SUB)], axis=1)

    y = (_unpack_bf16_pairs(choice_rows(0)) * gt[:, 0:1]
         + _unpack_bf16_pairs(choice_rows(1)) * gt[:, 1:2])
    g2 = mod_ref[0, pl.ds(r, 1), 5 * d:6 * d]
    z = ALPHA * x_ref[...] + g2 * y
    mu = jnp.mean(z, axis=-1, keepdims=True)
    zc = z - mu
    var = jnp.mean(zc * zc, axis=-1, keepdims=True)
    o_ref[...] = zc * lax.rsqrt(var + LN_EPS) * lng_ref[...]


def _combine(x1, dest, gates, mod, layer, lng, ys, *, tiles_per_batch, ctx_tiles, nb):
    n, d = x1.shape
    nt = n // ROW_TILE
    dest_t = _tile_rows_of(dest, ROW_TILE)
    kern = functools.partial(_combine_kernel, tiles_per_batch=tiles_per_batch, ctx_tiles=ctx_tiles, nb=nb)
    idx_spec = lambda f: pl.BlockSpec((1, 1, TOP_K * ROW_TILE), f, memory_space=pltpu.SMEM)
    return pl.pallas_call(
        kern,
        grid=(nt,),
        in_specs=[
            idx_spec(lambda t: (t, 0, 0)),
            idx_spec(lambda t: (jnp.minimum(t + 1, nt - 1), 0, 0)),
            pl.BlockSpec((ROW_TILE, d), lambda t: (t, 0)),
            pl.BlockSpec((ROW_TILE, TOP_K), lambda t: (t, 0)),
            pl.BlockSpec((1, SUBLANES, mod.shape[2]), lambda t: (layer, 0, 0)),
            pl.BlockSpec((1, d), lambda t: (0, 0)),
            pl.BlockSpec(memory_space=pl.ANY),
        ],
        out_specs=pl.BlockSpec((ROW_TILE, d), lambda t: (t, 0)),
        out_shape=jax.ShapeDtypeStruct((n, d), F32),
        scratch_shapes=[pltpu.VMEM((2, TOP_K * ROW_TILE // SUBLANES, ROW_SUB, SUBLANES, LANES), jnp.uint32),
                        pltpu.SemaphoreType.DMA((2,))],
        compiler_params=_cparams(("arbitrary",)),
        name="moe_combine%d" % layer,
    )(dest_t, dest_t, x1, gates.T, mod, lng, ys)


def _moe_layer(x1, h2, eidx, gates, mod, layer, lng, wg, wu, wd, *, tiles_per_batch, ctx_tiles, nb):
    dest, block_expert, n_blocks = _moe_plan(eidx)
    xs = _dispatch(h2, dest, n_blocks * MOE_ROWS)
    ys = _experts(xs, block_expert, layer, wg, wu, wd)
    return _combine(x1, dest, gates, mod, layer, lng, ys,
                    tiles_per_batch=tiles_per_batch, ctx_tiles=ctx_tiles, nb=nb)


def _half_norms(x):
    lane = lax.broadcasted_iota(jnp.int32, x.shape, 1)
    sq = x * x
    lo = jnp.sum(jnp.where(lane < DIFF_DH, sq, 0.0), axis=-1, keepdims=True)
    hi = jnp.sum(jnp.where(lane >= DIFF_DH, sq, 0.0), axis=-1, keepdims=True)
    return jnp.sqrt(lo), jnp.sqrt(hi)


def _attn_kernel(lam_ref, q_ref, k_ref, v_ref, g_ref, o_ref, vext, s_buf0, s_buf1,
                 p_buf0, p_buf1, corr_buf0, corr_buf1, m_buf, shift_buf, acc, *, out_scale, ctx_len):
    t_len = k_ref.shape[0]
    nk = t_len // ATT_TK
    dv = v_ref.shape[1]
    tq = acc.shape[1]
    nq = o_ref.shape[0] // tq
    n_tiles = nq * nk

    vext[:, 0:dv] = v_ref[...]
    vext[:, dv:2 * dv] = jnp.ones((t_len, dv), BF16)
    lam = lam_ref[0]
    gain = g_ref[...] * out_scale

    s_bufs, p_bufs, corr_bufs = (s_buf0, s_buf1), (p_buf0, p_buf1), (corr_buf0, corr_buf1)

    def key_rows(kj):
        return pl.ds(pl.multiple_of(kj * ATT_TK, ATT_TK), ATT_TK)

    def query_rows(qi):
        return pl.ds(pl.multiple_of(ctx_len + qi * tq, ROW_TILE), tq)

    def column_max(norms, carry):
        return tuple(jnp.maximum(c, jnp.max(n, axis=0, keepdims=True)) for n, c in zip(norms, carry))

    def key_norms(kj, carry):
        return column_max(_half_norms(k_ref[key_rows(kj), :].astype(F32)), carry)

    zero11 = jnp.zeros((1, 1), F32)
    kmax = lax.fori_loop(0, nk, key_norms, (zero11, zero11))

    def query_shifts(qi, carry):
        shifts = tuple(n * k for n, k in zip(_half_norms(q_ref[query_rows(qi), :].astype(F32)), kmax))
        shift_buf[qi, :, 0:1] = shifts[0]
        shift_buf[qi, :, 1:2] = shifts[1]
        return column_max(shifts, carry)

    worst = lax.fori_loop(0, nq, query_shifts, (zero11, zero11))
    bounded = jnp.max(jnp.maximum(worst[0], worst[1])) <= ATT_SAFE_SHIFT

    def advance(tile):
        qi, kj = tile
        wrap = kj + 1 == nk
        return jnp.where(wrap, qi + 1, qi), jnp.where(wrap, 0, kj + 1)

    def scores(tile, slot):
        qi, kj = tile
        q = q_ref[query_rows(qi), :]
        lane = lax.broadcasted_iota(jnp.int32, q.shape, 1)
        zero = jnp.zeros(q.shape, q.dtype)
        k = k_ref[key_rows(kj), :]
        s_bufs[slot][0] = _dot_nt(jnp.where(lane < DIFF_DH, q, zero), k)
        s_bufs[slot][1] = _dot_nt(jnp.where(lane >= DIFF_DH, q, zero), k)

    def numerators(tile, slot, online):
        qi, kj = tile
        for w in range(2):
            s = s_bufs[slot][w]
            if online:
                m_old = jnp.where(kj == 0, -jnp.inf, m_buf[w])
                m_new = jnp.maximum(m_old, jnp.max(s, axis=-1, keepdims=True))
                corr_bufs[slot][w] = jnp.exp2(m_old - m_new)
                m_buf[w] = m_new
            else:
                m_new = shift_buf[qi, :, w:w + 1]
            p_bufs[slot][w] = jnp.exp2(s - m_new).astype(BF16)

    def values(tile, slot, online):
        qi, kj = tile
        ve = vext[key_rows(kj), :]
        a = []
        for w in range(2):
            keep = corr_bufs[slot][w] if online else jnp.where(kj == 0, 0.0, 1.0)
            a.append(keep * acc[w] + _dot(p_bufs[slot][w], ve))
            acc[w] = a[w]
        o = a[0][:, 0:dv] / a[0][:, dv:2 * dv] - lam * (a[1][:, 0:dv] / a[1][:, dv:2 * dv])
        o = o * lax.rsqrt(jnp.mean(o * o, axis=-1, keepdims=True) + GN_EPS)
        o_ref[pl.ds(pl.multiple_of(qi * tq, tq), tq), :] = (o * gain).astype(BF16)

    def pipeline(online):
        def step(tiles, slot):
            a, b, c = tiles
            values(c, slot, online)
            scores(a, slot)
            numerators(b, 1 - slot, online)
            return advance(a), a, b

        acc[...] = jnp.zeros(acc.shape, F32)
        t0 = (jnp.int32(0), jnp.int32(0))
        t1 = advance(t0)
        scores(t0, 0)
        scores(t1, 1)
        numerators(t0, 0, online)

        def pair(_, tiles):
            return step(step(tiles, 0), 1)

        _, last, prev = lax.fori_loop(0, (n_tiles - 2) // 2, pair, (advance(t1), t1, t0))
        numerators(last, (n_tiles - 1) % 2, online)
        values(prev, n_tiles % 2, online)
        values(last, (n_tiles - 1) % 2, online)

    pl.when(bounded)(functools.partial(pipeline, False))
    pl.when(jnp.logical_not(bounded))(functools.partial(pipeline, True))


def _diff_attention(qkv, lam, subln_g, lambda_init, *, nb, l_len, t_len, ctx_len):
    d = D_MODEL
    tq = ATT_TQ
    dv = 2 * DIFF_DH
    nq = l_len // tq
    assert (nq * (t_len // ATT_TK)) % 2 == 0
    kern = functools.partial(_attn_kernel, out_scale=1.0 - lambda_init, ctx_len=ctx_len)
    return pl.pallas_call(
        kern,
        grid_spec=pltpu.PrefetchScalarGridSpec(
            num_scalar_prefetch=1,
            grid=(nb, DIFF_HEADS),
            in_specs=[
                pl.BlockSpec((t_len, LANES), lambda b, h, lam: (b, h)),
                pl.BlockSpec((t_len, LANES), lambda b, h, lam: (b, DIFF_HEADS + h)),
                pl.BlockSpec((t_len, LANES), lambda b, h, lam: (b, 2 * DIFF_HEADS + h)),
                pl.BlockSpec((1, LANES), lambda b, h, lam: (0, 0)),
            ],
            out_specs=pl.BlockSpec((l_len, LANES), lambda b, h, lam: (b, h)),
            scratch_shapes=[
                pltpu.VMEM((t_len, 2 * dv), BF16),
                pltpu.VMEM((2, tq, ATT_TK), F32), pltpu.VMEM((2, tq, ATT_TK), F32),
                pltpu.VMEM((2, tq, ATT_TK), BF16), pltpu.VMEM((2, tq, ATT_TK), BF16),
                pltpu.VMEM((2, tq, 1), F32), pltpu.VMEM((2, tq, 1), F32),
                pltpu.VMEM((2, tq, 1), F32), pltpu.VMEM((nq, tq, 2), F32),
                pltpu.VMEM((2, tq, 2 * dv), F32),
            ],
        ),
        out_shape=jax.ShapeDtypeStruct((nb * l_len, d), BF16),
        compiler_params=_cparams(("parallel", "parallel"), ATT_VMEM_LIMIT),
        name="diff_attention",
    )(lam, qkv, qkv, qkv, subln_g.reshape(1, LANES).astype(F32))


def _rope_angles(pos, dim):
    inv = (np.float32(ROPE_BASE) ** (-np.arange(0, dim, 2, dtype=np.float32) / np.float32(dim)))
    return pos.astype(np.float32)[:, None] * inv.astype(np.float32)[None, :]


def _ret_rope_tables(l_len, ctx_len):
    half = RET_DK // 2
    ang = _rope_angles(np.arange(l_len), RET_DK)
    ang = np.concatenate([np.zeros((ctx_len, half), np.float32), ang], axis=0)
    cos64 = np.concatenate([np.cos(ang), np.cos(ang)], axis=1)
    sin64 = np.concatenate([-np.sin(ang), np.sin(ang)], axis=1)
    kscale = np.float32(RET_DK ** -0.5)
    cos = np.concatenate([cos64, cos64 * kscale], axis=1)
    sin = np.concatenate([sin64, sin64 * kscale], axis=1)
    return jnp.asarray(cos[None], F32), jnp.asarray(sin[None], F32)


def _attn_rope_tables(l_len, ctx_len):
    quarter = DIFF_DH // 4
    pos = np.arange(l_len)
    pad = lambda a: np.concatenate([np.zeros((ctx_len, quarter), np.float32), a], axis=0)
    ang_r = pad(_rope_angles(pos // GRID_W, DIFF_DH // 2))
    ang_c = pad(_rope_angles(pos % GRID_W, DIFF_DH // 2))
    cos64 = np.concatenate([np.cos(ang_r)] * 2 + [np.cos(ang_c)] * 2, axis=1)
    sin64 = np.concatenate([-np.sin(ang_r), np.sin(ang_r), -np.sin(ang_c), np.sin(ang_c)], axis=1)
    cos = np.concatenate([cos64, cos64], axis=1)
    sin = np.concatenate([sin64, sin64], axis=1)
    qscale = np.float32(DIFF_DH ** -0.5 * math.log2(math.e))
    return (jnp.asarray(np.stack([cos * qscale, cos]), F32), jnp.asarray(np.stack([sin * qscale, sin]), F32))


def kernel(x, c, ctx, c_ctx, ada_w, ada_b, ln_g, w_in_ab, ret_decay_logit, s5_lam_re, s5_lam_im,
           s5_log_dt, s5_b_re, s5_b_im, s5_c_re, s5_c_im, s5_d, s5_w_glu, w_out_ab, w_in_c,
           diff_lambda, diff_subln_g, w_out_c, router_w, router_bias, exp_w_gate, exp_w_up,
           exp_w_down):
    nb, l_len, d = x.shape
    ctx_len = ctx.shape[1]
    t_len = ctx_len + l_len
    nt = nb * t_len
    tpb = t_len // ROW_TILE
    ctx_tiles = ctx_len // ROW_TILE
    assert d == D_MODEL and nb < SUBLANES
    assert l_len % ATT_TQ == 0 and ctx_len % ROW_TILE == 0 and t_len % ATT_TK == 0

    x_lat, x_ctx = x.reshape(nb * l_len, d), ctx.reshape(nb * ctx_len, d)
    c_all = jnp.concatenate([c, c_ctx[None].astype(c.dtype)], axis=0)
    c_pad = jnp.zeros((SUBLANES, d), F32).at[:nb + 1].set(c_all)
    mod = _adaln(c_pad, ada_w, ada_b)

    wr_t = router_w.T
    rbias = router_bias.reshape(N_EXPERTS, 1).astype(F32)

    w0 = w_in_ab[0]
    q_w, k_w, v_w, g_w, u_w = jnp.split(w0, (RET_QK_W, 2 * RET_QK_W, 2 * RET_QK_W + RET_V_W,
                                             2 * RET_QK_W + 2 * RET_V_W), axis=1)
    qk_w = jnp.concatenate([q_w.reshape(d, RET_HEADS, RET_DK), k_w.reshape(d, RET_HEADS, RET_DK)],
                           axis=2).reshape(d, 2 * RET_QK_W)
    w0p = jnp.concatenate([qk_w, v_w, g_w, u_w], axis=1).astype(BF16)
    cos0, sin0 = _ret_rope_tables(l_len, ctx_len)
    rope_tab0 = [0] * RET_HEADS + [None] * ((w0p.shape[1] - S5_CH - 2 * RET_QK_W) // LANES)
    proj0, u5 = _inproj((x_lat, x_ctx), mod, 0, w0p, cos0, sin0, rope_tab0, RET_DK // 2,
                        nb=nb, t_len=t_len, ctx_len=ctx_len, s5_cols=S5_CH)

    log_gammas = jax.nn.log_sigmoid(ret_decay_logit[0].astype(F32))
    ret = _retention(proj0, log_gammas, nb=nb, t_len=t_len, ctx_len=ctx_len)

    s5_ops = _s5_operators(s5_lam_re[0], s5_lam_im[0], s5_log_dt[0], s5_b_re[0], s5_b_im[0],
                           s5_c_re[0], s5_c_im[0], s5_d[0])
    s5y = _s5(u5, *s5_ops, nb=nb, nctx=ctx_len // S5_CHUNK)

    x1, h2, eidx, gates = _merge0(ret, proj0, s5y, x_lat, x_ctx, mod, s5_w_glu[0].astype(BF16),
                                  w_out_ab[0].astype(BF16), ln_g[0, 0].reshape(1, d), wr_t, rbias,
                                  nb=nb, t_len=t_len, ctx_len=ctx_len)
    experts = (exp_w_gate.astype(BF16), exp_w_up.astype(BF16), exp_w_down.astype(BF16))
    x2 = _moe_layer(x1, h2, eidx, gates, mod, 0, ln_g[0, 1].reshape(1, d), *experts,
                    tiles_per_batch=tpb, ctx_tiles=ctx_tiles, nb=nb)

    cos1, sin1 = _attn_rope_tables(l_len, ctx_len)
    n_heads_cols = D_MODEL // LANES
    rope_tab1 = [0] * n_heads_cols + [1] * n_heads_cols + [None] * n_heads_cols
    qkv, = _inproj((x2,), mod, 1, w_in_c[0].astype(BF16), cos1, sin1, rope_tab1, DIFF_DH // 4,
                   nb=nb, t_len=t_len, ctx_len=ctx_len)
    lf = diff_lambda[0].astype(F32)
    lambda_init = 0.8 - 0.6 * math.exp(-0.3 * 1)
    lam = (jnp.exp(jnp.sum(lf[0] * lf[1])) - jnp.exp(jnp.sum(lf[2] * lf[3])) + lambda_init).reshape(1)
    att = _diff_attention(qkv, lam, diff_subln_g[0], lambda_init,
                          nb=nb, l_len=l_len, t_len=t_len, ctx_len=ctx_len)
    x3, h3, eidx1, gates1 = _merge1(att, x2, mod, 1, w_out_c[0].astype(BF16), ln_g[1, 0].reshape(1, d),
                                    wr_t, rbias, nb=nb, l_len=l_len, t_len=t_len, ctx_len=ctx_len)
    out = _moe_layer(x3, h3, eidx1, gates1, mod, 1, ln_g[1, 1].reshape(1, d), *experts,
                     tiles_per_batch=l_len // ROW_TILE, ctx_tiles=0, nb=nb)
    return out.reshape(nb, l_len, d)
```

```python
import functools
import math

import jax
import jax.numpy as jnp
import numpy as np
from jax import lax
from jax.experimental import pallas as pl
from jax.experimental.pallas import tpu as pltpu

F32 = jnp.float32
BF16 = jnp.bfloat16

D_MODEL = 1024
DEPTH = 2
GRID_W = 64
ALPHA = (2.0 * DEPTH) ** 0.25
LN_EPS = 1e-5
GN_EPS = 1e-6
ROPE_BASE = 10000.0
RET_DK = 64
RET_DV = 128
RET_HEADS = 6
RET_QK_W = RET_HEADS * RET_DK
RET_V_W = RET_HEADS * RET_DV
S5_CH = 256
S5_P = 16
S5_G = 16
S5_N = 64
DIFF_HEADS = 8
DIFF_DH = 64
N_EXPERTS = 16
EXPERTS_PER_GROUP = 4
TOP_K = 2

LANES = 128
SUBLANES = 8
MXU_DIM = 256
ROW_TILE = 256
RET_CHUNK = 256
RET_STATE_UNROLL = 3
RET_OUT_UNROLL = 11
S5_CHUNK = 8
S5_HALVES = 2
MOE_ROWS = 256
DISPATCH_TILE = 1024
ROW_WORDS = D_MODEL // 2
ROW_SUB = ROW_WORDS // LANES
ATT_TQ = 1024
ATT_TK = 768
ATT_SAFE_SHIFT = 48.0
VMEM_LIMIT = 48 * 1024 * 1024
ATT_VMEM_LIMIT = 56 * 1024 * 1024


def _cparams(sem, vmem_limit=VMEM_LIMIT):
    return pltpu.CompilerParams(dimension_semantics=sem, vmem_limit_bytes=vmem_limit)


def _dot(a, b):
    return jnp.dot(a, b, preferred_element_type=F32)


def _dot_nt(a, b):
    return lax.dot_general(a, b, (((1,), (1,)), ((), ())), preferred_element_type=F32)


def _dot_tn(a, b):
    return lax.dot_general(a, b, (((0,), (0,)), ((), ())), preferred_element_type=F32)


def _split_bf16(x):
    hi = x.astype(BF16)
    lo = (x - hi.astype(F32)).astype(BF16)
    return hi, lo


def _dot3(a, b):
    ah, al = _split_bf16(a)
    bh, bl = _split_bf16(b)
    return _dot(ah, bh) + _dot(ah, bl) + _dot(al, bh)


def _dot3_nt(a, b):
    ah, al = _split_bf16(a)
    bh, bl = _split_bf16(b)
    return _dot_nt(ah, bh) + _dot_nt(ah, bl) + _dot_nt(al, bh)


def _sigmoid(x):
    return 1.0 / (1.0 + jnp.exp(-x))


def _silu(x):
    return x * _sigmoid(x)


def _pack_bf16_pairs(v):
    half = v.shape[1] // 2
    bits = lax.bitcast_convert_type(v.astype(BF16).astype(F32), jnp.uint32)
    return (bits[:, :half] >> 16) | (bits[:, half:] & jnp.uint32(0xFFFF0000))


def _unpack_bf16_pairs(p):
    lo = lax.bitcast_convert_type(p << 16, F32)
    hi = lax.bitcast_convert_type(p & jnp.uint32(0xFFFF0000), F32)
    return jnp.concatenate([lo, hi], axis=1)


def _row_shape(n_rows):
    return (n_rows // SUBLANES, ROW_SUB, SUBLANES, LANES)


def _row(ref, r):
    return ref.at[lax.shift_right_logical(r, 3), :, r & (SUBLANES - 1), :]


def _store_rows(ref, packed):
    rows = packed.shape[0]
    for j in range(ROW_SUB):
        ref[:, j] = packed[:, j * LANES:(j + 1) * LANES].reshape(rows // SUBLANES, SUBLANES, LANES)


def _load_rows(ref):
    rows = ref.shape[0] * SUBLANES
    return jnp.concatenate([ref[:, j].reshape(rows, LANES) for j in range(ROW_SUB)], axis=1)


def _gelu_tanh(x):
    c = math.sqrt(2.0 / math.pi)
    return 0.5 * x * (1.0 + jnp.tanh(c * (x + 0.044715 * (x * x * x))))


def _adaln_kernel(c_ref, w_ref, b_ref, o_ref):
    c = c_ref[...]
    o_ref[0] = _dot3(_silu(c), w_ref[0]) + b_ref[0]


def _adaln(c_pad, ada_w, ada_b):
    depth, d, n = ada_w.shape
    tn = 1536
    return pl.pallas_call(
        _adaln_kernel,
        grid=(depth, n // tn),
        in_specs=[
            pl.BlockSpec((SUBLANES, d), lambda i, j: (0, 0)),
            pl.BlockSpec((1, d, tn), lambda i, j: (i, 0, j)),
            pl.BlockSpec((1, 1, tn), lambda i, j: (i, 0, j)),
        ],
        out_specs=pl.BlockSpec((1, SUBLANES, tn), lambda i, j: (i, 0, j)),
        out_shape=jax.ShapeDtypeStruct((depth, SUBLANES, n), F32),
        compiler_params=_cparams(("parallel", "parallel")),
        name="adaln",
    )(c_pad, ada_w, ada_b.reshape(depth, 1, n))


def _mod_row(t, tiles_per_batch, ctx_tiles, nb):
    b = lax.div(t, tiles_per_batch)
    w = lax.rem(t, tiles_per_batch)
    return jnp.where(w < ctx_tiles, nb, b)


def _rope_block(a, cos, sin, half):
    lane = lax.broadcasted_iota(jnp.int32, a.shape, 1)
    first = lax.rem(lane, 2 * half) < half
    rot = jnp.where(first, pltpu.roll(a, LANES - half, 1), pltpu.roll(a, half, 1))
    return a * cos + rot * sin


def _chunk_perm(rows, transpose):
    per = rows // S5_CHUNK
    r = lax.broadcasted_iota(jnp.int32, (rows, rows), 0)
    c = lax.broadcasted_iota(jnp.int32, (rows, rows), 1)
    if transpose:
        r, c = c, r
    return (c == S5_CHUNK * lax.rem(r, per) + lax.div(r, per)).astype(BF16)


def _token_specs(d, tiles_per_batch, ctx_tiles):
    lat_tiles = tiles_per_batch - ctx_tiles
    b = lambda t: lax.div(t, tiles_per_batch)
    w = lambda t: lax.rem(t, tiles_per_batch)
    return (pl.BlockSpec((ROW_TILE, d), lambda t: (b(t) * lat_tiles + jnp.maximum(w(t) - ctx_tiles, 0), 0)),
            pl.BlockSpec((ROW_TILE, d), lambda t: (b(t) * ctx_tiles + jnp.minimum(w(t), ctx_tiles - 1), 0)))


def _token_tile(x_refs, tiles_per_batch, ctx_tiles):
    if len(x_refs) == 1:
        return x_refs[0][...]
    is_ctx = lax.rem(pl.program_id(0), tiles_per_batch) < ctx_tiles
    return jnp.where(is_ctx, x_refs[1][...], x_refs[0][...])


def _inproj_kernel(*refs, n_x, tiles_per_batch, ctx_tiles, nb, rope_tab, rope_half):
    x_refs, (mod_ref, w_ref, cos_ref, sin_ref, o_ref), rest = refs[:n_x], refs[n_x:n_x + 5], refs[n_x + 5:]
    d = w_ref.shape[0]
    n = o_ref.shape[1]
    r = _mod_row(pl.program_id(0), tiles_per_batch, ctx_tiles, nb)
    sh = mod_ref[0, pl.ds(r, 1), 0:d]
    sc = mod_ref[0, pl.ds(r, 1), d:2 * d]
    xm = (_token_tile(x_refs, tiles_per_batch, ctx_tiles) * (1.0 + sc) + sh).astype(BF16)
    for j in range(n // MXU_DIM):
        acc = _dot(xm, w_ref[:, j * MXU_DIM:(j + 1) * MXU_DIM])
        parts = []
        for s in range(MXU_DIM // LANES):
            blk = acc[:, s * LANES:(s + 1) * LANES]
            tab = rope_tab[j * (MXU_DIM // LANES) + s]
            if tab is not None:
                blk = _rope_block(blk, cos_ref[tab], sin_ref[tab], rope_half)
            parts.append(blk)
        o_ref[:, j * MXU_DIM:(j + 1) * MXU_DIM] = jnp.concatenate(parts, axis=1).astype(BF16)
    if rest:
        u_ref, = rest
        rows = o_ref.shape[0]
        per = rows // S5_CHUNK
        u = _dot(xm, w_ref[:, n:n + S5_CH]).astype(BF16)
        up = _dot(_chunk_perm(rows, False), u).astype(BF16)
        for s in range(S5_CHUNK):
            for hf in range(S5_CH // LANES):
                u_ref[hf, :, s * LANES:(s + 1) * LANES] = up[s * per:(s + 1) * per,
                                                             hf * LANES:(hf + 1) * LANES]


def _inproj(xs, mod, layer, w, cos, sin, rope_tab, rope_half, *, nb, t_len, ctx_len, s5_cols=0):
    nt, d = nb * t_len, w.shape[0]
    n = w.shape[1] - s5_cols
    tpb = t_len // ROW_TILE
    ctx_tiles = ctx_len // ROW_TILE
    kern = functools.partial(_inproj_kernel, n_x=len(xs), tiles_per_batch=tpb, ctx_tiles=ctx_tiles,
                             nb=nb, rope_tab=tuple(rope_tab), rope_half=rope_half)
    x_specs = ([pl.BlockSpec((ROW_TILE, d), lambda t: (t, 0))] if len(xs) == 1
               else list(_token_specs(d, tpb, ctx_tiles)))
    ntab = cos.shape[0]
    out_specs = [pl.BlockSpec((ROW_TILE, n), lambda t: (t, 0))]
    out_shape = [jax.ShapeDtypeStruct((nt, n), BF16)]
    if s5_cols:
        halves, per = s5_cols // LANES, ROW_TILE // S5_CHUNK
        out_specs.append(pl.BlockSpec((halves, per, S5_CHUNK * LANES), lambda t: (0, t, 0)))
        out_shape.append(jax.ShapeDtypeStruct((halves, nt // S5_CHUNK, S5_CHUNK * LANES), BF16))
    return pl.pallas_call(
        kern,
        grid=(nt // ROW_TILE,),
        in_specs=x_specs + [
            pl.BlockSpec((1, SUBLANES, mod.shape[2]), lambda t: (layer, 0, 0)),
            pl.BlockSpec(w.shape, lambda t: (0, 0)),
            pl.BlockSpec((ntab, ROW_TILE, LANES), lambda t: (0, lax.rem(t, tpb), 0)),
            pl.BlockSpec((ntab, ROW_TILE, LANES), lambda t: (0, lax.rem(t, tpb), 0)),
        ],
        out_specs=out_specs,
        out_shape=out_shape,
        compiler_params=_cparams(("parallel",)),
        name="inproj%d" % layer,
    )(*xs, mod, w, cos, sin)


def _retention_kernel(lg_ref, qk_ref, v_ref, o_ref, sf_ref, sb_ref, *, nctx):
    c_len = RET_CHUNK
    t_len = qk_ref.shape[0]
    nc = t_len // c_len
    h = pl.program_id(1)
    lgf = lg_ref[0, h]
    lgb = lg_ref[1, h]
    ii = lax.broadcasted_iota(jnp.int32, (c_len, 1), 0).astype(F32)
    jj = lax.broadcasted_iota(jnp.int32, (1, c_len), 1).astype(F32)
    diff = ii - jj
    decay = jnp.where(diff >= 0.0, jnp.exp(lgf * jnp.maximum(diff, 0.0)),
                      jnp.exp(lgb * jnp.maximum(-diff, 0.0)))
    kdf = jnp.exp(lgf * (c_len - 1.0 - ii))
    kdb = jnp.exp(lgb * ii)
    qdf = jnp.exp(lgf * (ii + 1.0))
    qdb = jnp.exp(lgb * (c_len - ii))
    zrow = jnp.zeros((1, RET_DV), F32)
    gf_chunk = jnp.exp(zrow + lgf * c_len)
    gb_chunk = jnp.exp(zrow + lgb * c_len)

    def load(c):
        rows = pl.ds(pl.multiple_of(c * c_len, c_len), c_len)
        qk = qk_ref[rows, :].astype(F32)
        return qk[:, :RET_DK], qk[:, RET_DK:], v_ref[rows, :]

    def states(j, carry):
        sf, sb = carry
        cb = jnp.where(j < nctx, nctx - 1 - j, nc - 1 - (j - nctx))
        sf_ref[j] = sf
        sb_ref[cb] = sb
        _, kf, vf = load(j)
        _, kb, vb = load(cb)
        return (gf_chunk * sf + _dot_tn((kf * kdf).astype(BF16), vf),
                gb_chunk * sb + _dot_tn((kb * kdb).astype(BF16), vb))

    zero_state = jnp.zeros((RET_DK, RET_DV), F32)
    lax.fori_loop(0, nc, states, (zero_state, zero_state), unroll=RET_STATE_UNROLL)

    def out_chunk(c, carry):
        q, k, v = load(c)
        scores = _dot_nt(q.astype(BF16), k.astype(BF16)) * decay
        o = _dot(scores.astype(BF16), v)
        o = o + _dot((q * qdf).astype(BF16), sf_ref[c].astype(BF16))
        o = o + _dot((q * qdb).astype(BF16), sb_ref[c].astype(BF16))
        mu = jnp.mean(o, axis=-1, keepdims=True)
        oc = o - mu
        var = jnp.mean(oc * oc, axis=-1, keepdims=True)
        rows = pl.ds(pl.multiple_of(c * c_len, c_len), c_len)
        o_ref[rows, :] = (oc * lax.rsqrt(var + GN_EPS)).astype(BF16)
        return carry

    lax.fori_loop(0, nc, out_chunk, 0, unroll=RET_OUT_UNROLL)


def _retention(proj, log_gammas, *, nb, t_len, ctx_len):
    nt = proj.shape[0]
    nc = t_len // RET_CHUNK
    kern = functools.partial(_retention_kernel, nctx=ctx_len // RET_CHUNK)
    vcol0 = RET_HEADS
    return pl.pallas_call(
        kern,
        grid_spec=pltpu.PrefetchScalarGridSpec(
            num_scalar_prefetch=1,
            grid=(nb, RET_HEADS),
            in_specs=[
                pl.BlockSpec((t_len, LANES), lambda b, h, lg: (b, h)),
                pl.BlockSpec((t_len, LANES), lambda b, h, lg: (b, vcol0 + h)),
            ],
            out_specs=pl.BlockSpec((t_len, LANES), lambda b, h, lg: (b, h)),
            scratch_shapes=[pltpu.VMEM((nc, RET_DK, RET_DV), F32),
                            pltpu.VMEM((nc, RET_DK, RET_DV), F32)],
        ),
        out_shape=jax.ShapeDtypeStruct((nt, RET_V_W), BF16),
        compiler_params=_cparams(("parallel", "parallel")),
        name="retention",
    )(log_gammas, proj, proj)


def _s5_operators(lam_re, lam_im, log_dt, b_re, b_im, c_re, c_im, d_skip):
    tc = S5_CHUNK
    hp = lax.Precision.HIGHEST
    ks = jnp.arange(tc + 1, dtype=F32)
    pw, bbar, cm = [], [], []
    for direction in range(2):
        dt = jnp.exp(log_dt[direction].astype(F32))[:, None]
        lam = lax.complex(lam_re[direction].astype(F32), lam_im[direction].astype(F32))
        z = lam * dt
        p = jnp.exp(z[None] * ks[:, None, None])
        lam_bar = p[1]
        bb = ((lam_bar - 1.0) / lam)[..., None] * lax.complex(
            b_re[direction].astype(F32), b_im[direction].astype(F32))
        pw.append(p)
        bbar.append(bb)
        cm.append(lax.complex(c_re[direction].astype(F32), c_im[direction].astype(F32)))

    def lag_kernel(p, bb, c):
        return jnp.einsum('gpn,kgn,gnq->kgpq', c, p[:tc], bb, precision=hp).real

    kf = lag_kernel(pw[0], bbar[0], cm[0])
    kb = lag_kernel(pw[1], bbar[1], cm[1])
    k0 = kf[0] + kb[0] + jnp.eye(S5_P, dtype=F32)[None] * d_skip.astype(F32)[:, :, None]
    kcat = jnp.concatenate([kb[1:][::-1], k0[None], kf[1:]], axis=0)
    s_idx = jnp.arange(tc)[:, None]
    t_idx = jnp.arange(tc)[None, :]
    m5 = kcat[t_idx - s_idx + tc - 1]
    hg = S5_G // S5_HALVES
    eye = jnp.eye(hg, dtype=F32)
    split = lambda z, axis: z.reshape(z.shape[:axis] + (S5_HALVES, hg) + z.shape[axis + 1:])
    wide = tc * hg * S5_P
    intra = jnp.einsum('sthgpq,gk->hsgqtkp', split(m5, 2), eye).reshape(S5_HALVES, wide, wide)

    ef = pw[0][:tc][::-1][:, :, :, None] * bbar[0][None]
    eb = pw[1][:tc][:, :, :, None] * bbar[1][None]
    parts_in = jnp.stack([ef.real, ef.imag, eb.real, eb.imag], axis=0)
    w_in = jnp.einsum('cshgnq,gk->hsgqckn', split(parts_in, 2), eye).reshape(
        S5_HALVES, wide, 4 * hg * S5_N)
    of = cm[0][None] * pw[0][1:][:, :, None, :]
    ob = cm[1][None] * pw[1][1:][::-1][:, :, None, :]
    parts_out = jnp.stack([of.real, -of.imag, ob.real, -ob.imag], axis=0)
    w_out = jnp.einsum('cthgpn,gk->hckntgp', split(parts_out, 2), eye).reshape(
        S5_HALVES, 4 * hg * S5_N, wide)
    a = jnp.stack([pw[0][tc].real, pw[0][tc].imag, pw[1][tc].real, pw[1][tc].imag], axis=0)
    a = a.reshape(4, S5_HALVES, hg * S5_N).transpose(1, 0, 2)
    return intra.astype(BF16), w_in.astype(BF16), w_out.astype(BF16), a


def _s5_kernel(x_ref, wi_ref, win_ref, wout_ref, a_ref, y_ref, st_ref, *, nctx):
    x = x_ref[0]
    nc = x.shape[0]
    w = a_ref.shape[2]
    st_ref[...] = _dot(x, win_ref[0])
    afr, afi, abr, abi = (a_ref[0, i:i + 1, :] for i in range(4))

    def step(j, carry):
        fr, fi, br, bi = carry
        rf = pl.ds(j, 1)
        rb = pl.ds(jnp.where(j < nctx, nctx - 1 - j, nc - 1 - (j - nctx)), 1)
        efr, efi = st_ref[rf, 0:w], st_ref[rf, w:2 * w]
        ebr, ebi = st_ref[rb, 2 * w:3 * w], st_ref[rb, 3 * w:4 * w]
        st_ref[rf, 0:w] = fr
        st_ref[rf, w:2 * w] = fi
        st_ref[rb, 2 * w:3 * w] = br
        st_ref[rb, 3 * w:4 * w] = bi
        return (afr * fr - afi * fi + efr, afr * fi + afi * fr + efi,
                abr * br - abi * bi + ebr, abr * bi + abi * br + ebi)

    z = jnp.zeros((1, w), F32)
    lax.fori_loop(0, nc, step, (z, z, z, z), unroll=4)
    y_ref[0] = (_dot(x, wi_ref[0]) + _dot(st_ref[...].astype(BF16), wout_ref[0])).astype(BF16)


def _s5(xc, intra, w_in, w_out, a, *, nb, nctx):
    halves, rows, wide = xc.shape
    nc = rows // nb
    kern = functools.partial(_s5_kernel, nctx=nctx)
    per_half = lambda arr: pl.BlockSpec((1,) + arr.shape[1:], lambda hf, b: (hf, 0, 0))
    return pl.pallas_call(
        kern,
        grid=(halves, nb),
        in_specs=[
            pl.BlockSpec((1, nc, wide), lambda hf, b: (hf, b, 0)),
            per_half(intra), per_half(w_in), per_half(w_out), per_half(a),
        ],
        out_specs=pl.BlockSpec((1, nc, wide), lambda hf, b: (hf, b, 0)),
        out_shape=jax.ShapeDtypeStruct(xc.shape, BF16),
        scratch_shapes=[pltpu.VMEM((nc, w_in.shape[2]), F32)],
        compiler_params=_cparams(("parallel", "parallel")),
        name="s5",
    )(xc, intra, w_in, w_out, a)


def _route(logits_t, bias):
    scores = _sigmoid(logits_t)
    biased = scores + bias
    s_rows = [scores[e:e + 1, :] for e in range(N_EXPERTS)]
    b_rows = [biased[e:e + 1, :] for e in range(N_EXPERTS)]
    n_groups = N_EXPERTS // EXPERTS_PER_GROUP
    best = None
    sel = None
    for g in range(n_groups):
        a, b, c, d = b_rows[4 * g:4 * g + 4]
        hi1, lo1 = jnp.maximum(a, b), jnp.minimum(a, b)
        hi2, lo2 = jnp.maximum(c, d), jnp.minimum(c, d)
        top1 = jnp.maximum(hi1, hi2)
        top2 = jnp.maximum(jnp.minimum(hi1, hi2), jnp.maximum(lo1, lo2))
        gs = top1 + top2
        if g == 0:
            best, sel = gs, jnp.zeros(gs.shape, jnp.int32)
        else:
            better = gs > best
            sel = jnp.where(better, g, sel)
            best = jnp.where(better, gs, best)
    neg = jnp.full(best.shape, -jnp.inf, F32)
    masked = [jnp.where(sel == (e // EXPERTS_PER_GROUP), b_rows[e], neg) for e in range(N_EXPERTS)]
    v1, i1, g1 = masked[0], jnp.zeros(best.shape, jnp.int32), s_rows[0]
    for e in range(1, N_EXPERTS):
        better = masked[e] > v1
        v1 = jnp.where(better, masked[e], v1)
        i1 = jnp.where(better, e, i1)
        g1 = jnp.where(better, s_rows[e], g1)
    v2, i2, g2 = neg, jnp.zeros(best.shape, jnp.int32), jnp.zeros(best.shape, F32)
    for e in range(N_EXPERTS):
        cand = jnp.where(i1 == e, neg, masked[e])
        better = cand > v2
        v2 = jnp.where(better, cand, v2)
        i2 = jnp.where(better, e, i2)
        g2 = jnp.where(better, s_rows[e], g2)
    tot = g1 + g2
    return jnp.concatenate([i1, i2], axis=0), jnp.concatenate([g1 / tot, g2 / tot], axis=0)


def _tail(x, o, mod_ref, r, lng, wr, rb, x1_ref, h2_ref, ei_ref, gt_ref):
    d = x.shape[1]
    g1 = mod_ref[0, pl.ds(r, 1), 2 * d:3 * d]
    sh2 = mod_ref[0, pl.ds(r, 1), 3 * d:4 * d]
    sc2 = mod_ref[0, pl.ds(r, 1), 4 * d:5 * d]
    y = ALPHA * x + g1 * o
    mu = jnp.mean(y, axis=-1, keepdims=True)
    yc = y - mu
    var = jnp.mean(yc * yc, axis=-1, keepdims=True)
    x1 = yc * lax.rsqrt(var + LN_EPS) * lng
    h2 = x1 * (1.0 + sc2) + sh2
    x1_ref[...] = x1
    _store_rows(h2_ref, _pack_bf16_pairs(h2))
    ei, gt = _route(_dot3_nt(wr, h2), rb)
    ei_ref[...] = ei
    gt_ref[...] = gt


def _merge0_kernel(r_ref, g_ref, s_ref, x_ref, xc_ref, mod_ref, wglu_ref, wout_ref, lng_ref, wr_ref,
                   rb_ref, x1_ref, h2_ref, ei_ref, gt_ref, *, tiles_per_batch, ctx_tiles, nb):
    r = _mod_row(pl.program_id(0), tiles_per_batch, ctx_tiles, nb)
    ret = r_ref[...].astype(F32) * _silu(g_ref[...].astype(F32))
    rows = x_ref.shape[0]
    x = _token_tile((x_ref, xc_ref), tiles_per_batch, ctx_tiles)
    sp = jnp.concatenate(
        [jnp.concatenate([s_ref[hf, :, s * LANES:(s + 1) * LANES] for hf in range(S5_HALVES)], axis=1)
         for s in range(S5_CHUNK)], axis=0)
    s5 = _dot(_chunk_perm(rows, True), sp)
    z = _dot(_gelu_tanh(s5).astype(BF16), wglu_ref[...])
    zz = z[:, :S5_CH] * _sigmoid(z[:, S5_CH:])
    o = _dot(ret.astype(BF16), wout_ref[0:RET_V_W, :]) + _dot(zz.astype(BF16), wout_ref[RET_V_W:, :])
    _tail(x, o, mod_ref, r, lng_ref[...], wr_ref[...], rb_ref[...],
          x1_ref, h2_ref, ei_ref, gt_ref)


def _merge1_kernel(a_ref, x_ref, mod_ref, wout_ref, lng_ref, wr_ref, rb_ref,
                   x1_ref, h2_ref, ei_ref, gt_ref, *, tiles_per_batch):
    r = lax.div(pl.program_id(0), tiles_per_batch)
    o = _dot(a_ref[...], wout_ref[...])
    _tail(x_ref[...], o, mod_ref, r, lng_ref[...], wr_ref[...], rb_ref[...],
          x1_ref, h2_ref, ei_ref, gt_ref)


def _tail_outs(n_rows, d):
    shapes = (jax.ShapeDtypeStruct((n_rows, d), F32),
              jax.ShapeDtypeStruct(_row_shape(n_rows), jnp.uint32),
              jax.ShapeDtypeStruct((TOP_K, n_rows), jnp.int32),
              jax.ShapeDtypeStruct((TOP_K, n_rows), F32))
    specs = (pl.BlockSpec((ROW_TILE, d), lambda t: (t, 0)),
             pl.BlockSpec(_row_shape(ROW_TILE), lambda t: (t, 0, 0, 0)),
             pl.BlockSpec((TOP_K, ROW_TILE), lambda t: (0, t)),
             pl.BlockSpec((TOP_K, ROW_TILE), lambda t: (0, t)))
    return shapes, specs


def _merge0(ret, proj, s5y, x_lat, x_ctx, mod, w_glu, w_out, lng, wr_t, rbias, *, nb, t_len, ctx_len):
    nt, d = nb * t_len, x_lat.shape[1]
    tpb = t_len // ROW_TILE
    ctx_tiles = ctx_len // ROW_TILE
    kern = functools.partial(_merge0_kernel, tiles_per_batch=tpb, ctx_tiles=ctx_tiles, nb=nb)
    shapes, specs = _tail_outs(nt, d)
    gcol = (2 * RET_QK_W + RET_V_W) // RET_V_W
    full = lambda a: pl.BlockSpec(a.shape, lambda t: (0,) * a.ndim)
    return pl.pallas_call(
        kern,
        grid=(nt // ROW_TILE,),
        in_specs=[
            pl.BlockSpec((ROW_TILE, RET_V_W), lambda t: (t, 0)),
            pl.BlockSpec((ROW_TILE, RET_V_W), lambda t: (t, gcol)),
            pl.BlockSpec((S5_HALVES, ROW_TILE // S5_CHUNK, S5_CHUNK * LANES), lambda t: (0, t, 0)),
            *_token_specs(d, tpb, ctx_tiles),
            pl.BlockSpec((1, SUBLANES, mod.shape[2]), lambda t: (0, 0, 0)),
            full(w_glu), full(w_out), full(lng), full(wr_t), full(rbias),
        ],
        out_specs=specs,
        out_shape=shapes,
        compiler_params=_cparams(("parallel",)),
        name="merge0",
    )(ret, proj, s5y, x_lat, x_ctx, mod, w_glu, w_out, lng, wr_t, rbias)


def _merge1(att, x, mod, layer, w_out, lng, wr_t, rbias, *, nb, l_len, t_len, ctx_len):
    n_lat, d = att.shape
    tpb = l_len // ROW_TILE
    tpb_t = t_len // ROW_TILE
    ctx_tiles = ctx_len // ROW_TILE
    kern = functools.partial(_merge1_kernel, tiles_per_batch=tpb)
    shapes, specs = _tail_outs(n_lat, d)
    full = lambda a: pl.BlockSpec(a.shape, lambda t: (0,) * a.ndim)
    xrow = lambda t: (lax.div(t, tpb) * tpb_t + ctx_tiles + lax.rem(t, tpb), 0)
    return pl.pallas_call(
        kern,
        grid=(n_lat // ROW_TILE,),
        in_specs=[
            pl.BlockSpec((ROW_TILE, d), lambda t: (t, 0)),
            pl.BlockSpec((ROW_TILE, d), xrow),
            pl.BlockSpec((1, SUBLANES, mod.shape[2]), lambda t: (layer, 0, 0)),
            full(w_out), full(lng), full(wr_t), full(rbias),
        ],
        out_specs=specs,
        out_shape=shapes,
        compiler_params=_cparams(("parallel",)),
        name="merge1",
    )(att, x, mod, w_out, lng, wr_t, rbias)


def _moe_plan(eidx):
    k, n = eidx.shape
    a = k * n
    e_flat = eidx.reshape(a)
    seg = TOP_K * ROW_TILE
    onehot = (e_flat[:, None] == jnp.arange(N_EXPERTS, dtype=jnp.int32)[None, :]).astype(F32)
    onehot = onehot.reshape(a // seg, seg, N_EXPERTS)
    tril = lambda m: jnp.tril(jnp.ones((m, m), F32))
    within = jnp.einsum('ij,tjk->tik', tril(seg), onehot)
    seg_total = within[:, -1, :]
    seg_end = jnp.sum(tril(a // seg)[:, :, None] * seg_total[None], axis=1)
    counts = seg_end[-1].astype(jnp.int32)
    csum = within + (seg_end - seg_total)[:, None, :]
    padded = (counts + MOE_ROWS - 1) // MOE_ROWS * MOE_ROWS
    pad_end = jnp.sum(jnp.tril(jnp.ones((N_EXPERTS, N_EXPERTS), jnp.int32)) * padded[None, :], axis=1)
    pad_start = pad_end - padded
    dest = jnp.sum(onehot * (csum - 1.0 + pad_start.astype(F32)[None, None, :]), axis=-1)
    dest = dest.reshape(a).astype(jnp.int32)
    n_blocks = -(-(a + N_EXPERTS * (MOE_ROWS - 1)) // MOE_ROWS)
    first_row = jnp.arange(n_blocks, dtype=jnp.int32) * MOE_ROWS
    block_expert = jnp.minimum(jnp.sum((pad_end[None, :] <= first_row[:, None]).astype(jnp.int32), axis=1),
                               N_EXPERTS - 1)
    return dest.reshape(k, n), block_expert, n_blocks


def _tile_rows_of(dest, tile):
    k, n = dest.shape
    return dest.reshape(k, n // tile, tile).transpose(1, 0, 2).reshape(n // tile, 1, k * tile)


def _dispatch_kernel(dest_ref, h_ref, xs_in_hbm, xs_hbm, sem):
    del xs_in_hbm
    row_tiles = h_ref.shape[0]
    rows = row_tiles * SUBLANES

    def start(i, c):
        for u in range(SUBLANES):
            for choice in range(TOP_K):
                dst = dest_ref[0, 0, choice * rows + i * SUBLANES + u]
                pltpu.make_async_copy(h_ref.at[i, :, u, :], _row(xs_hbm, dst), sem).start(priority=choice)
        return c

    lax.fori_loop(0, row_tiles, start, 0)
    for _ in range(TOP_K):
        pltpu.make_async_copy(h_ref, xs_hbm.at[pl.ds(0, row_tiles)], sem).wait()


def _dispatch(h, dest, n_rows):
    n = h.shape[0] * SUBLANES
    tile = DISPATCH_TILE if n % DISPATCH_TILE == 0 else ROW_TILE
    return pl.pallas_call(
        _dispatch_kernel,
        grid=(n // tile,),
        in_specs=[
            pl.BlockSpec((1, 1, TOP_K * tile), lambda t: (t, 0, 0), memory_space=pltpu.SMEM),
            pl.BlockSpec(_row_shape(tile), lambda t: (t, 0, 0, 0)),
            pl.BlockSpec(memory_space=pl.ANY),
        ],
        out_specs=pl.BlockSpec(memory_space=pl.ANY),
        out_shape=jax.ShapeDtypeStruct(_row_shape(n_rows), jnp.uint32),
        scratch_shapes=[pltpu.SemaphoreType.DMA],
        input_output_aliases={2: 0},
        compiler_params=_cparams(("arbitrary",)),
        name="moe_dispatch",
    )(_tile_rows_of(dest, tile), h, jnp.zeros(_row_shape(n_rows), jnp.uint32))


def _experts_kernel(be_ref, x_ref, wg_ref, wu_ref, wd_ref, o_ref):
    x = _unpack_bf16_pairs(_load_rows(x_ref)).astype(BF16)
    hg = _dot(x, wg_ref[0])
    hu = _dot(x, wu_ref[0])
    _store_rows(o_ref, _pack_bf16_pairs(_dot((_silu(hg) * hu).astype(BF16), wd_ref[0])))


def _experts(xs, block_expert, layer, wg, wu, wd):
    n_blocks = block_expert.shape[0]
    d, dff = wg.shape[2], wg.shape[3]
    rows_spec = pl.BlockSpec(_row_shape(MOE_ROWS), lambda i, be: (i, 0, 0, 0))
    return pl.pallas_call(
        _experts_kernel,
        grid_spec=pltpu.PrefetchScalarGridSpec(
            num_scalar_prefetch=1,
            grid=(n_blocks,),
            in_specs=[
                rows_spec,
                pl.BlockSpec((None, 1, d, dff), lambda i, be: (layer, be[i], 0, 0)),
                pl.BlockSpec((None, 1, d, dff), lambda i, be: (layer, be[i], 0, 0)),
                pl.BlockSpec((None, 1, dff, d), lambda i, be: (layer, be[i], 0, 0)),
            ],
            out_specs=rows_spec,
        ),
        out_shape=jax.ShapeDtypeStruct(xs.shape, jnp.uint32),
        compiler_params=_cparams(("parallel",)),
        name="moe_experts",
    )(block_expert, xs, wg, wu, wd)


def _combine_kernel(dcur_ref, dnxt_ref, x_ref, gt_ref, mod_ref, lng_ref, y_hbm, o_ref, ybuf, sem, *,
                    tiles_per_batch, ctx_tiles, nb):
    t = pl.program_id(0)
    last = pl.num_programs(0) - 1
    rows = x_ref.shape[0]
    n = TOP_K * rows
    d = x_ref.shape[1]
    slot = lax.rem(t, 2)

    row_tiles = rows // SUBLANES

    def start_all(idx_ref, s):
        def start(i, c):
            for choice in range(TOP_K):
                for u in range(SUBLANES):
                    j = choice * rows + i * SUBLANES + u
                    pltpu.make_async_copy(_row(y_hbm, idx_ref[0, 0, j]),
                                          ybuf.at[s, choice * row_tiles + i, :, u, :],
                                          sem.at[s]).start(priority=choice)
            return c

        lax.fori_loop(0, row_tiles, start, 0)

    @pl.when(t == 0)
    def _():
        start_all(dcur_ref, 0)

    @pl.when(t < last)
    def _():
        start_all(dnxt_ref, 1 - slot)

    pltpu.make_async_copy(ybuf.at[slot], ybuf.at[slot], sem.at[slot]).wait()
    r = _mod_row(t, tiles_per_batch, ctx_tiles, nb)
    gt = gt_ref[...]

    def choice_rows(choice):
        tiles = pl.ds(choice * row_tiles, row_tiles)
        return jnp.concatenate([ybuf[slot, tiles, j].reshape(rows, LANES) for j in range(ROW_SUB)], axis=1)

    y = (_unpack_bf16_pairs(choice_rows(0)) * gt[:, 0:1]
         + _unpack_bf16_pairs(choice_rows(1)) * gt[:, 1:2])
    g2 = mod_ref[0, pl.ds(r, 1), 5 * d:6 * d]
    z = ALPHA * x_ref[...] + g2 * y
    mu = jnp.mean(z, axis=-1, keepdims=True)
    zc = z - mu
    var = jnp.mean(zc * zc, axis=-1, keepdims=True)
    o_ref[...] = zc * lax.rsqrt(var + LN_EPS) * lng_ref[...]


def _combine(x1, dest, gates, mod, layer, lng, ys, *, tiles_per_batch, ctx_tiles, nb):
    n, d = x1.shape
    nt = n // ROW_TILE
    dest_t = _tile_rows_of(dest, ROW_TILE)
    kern = functools.partial(_combine_kernel, tiles_per_batch=tiles_per_batch, ctx_tiles=ctx_tiles, nb=nb)
    idx_spec = lambda f: pl.BlockSpec((1, 1, TOP_K * ROW_TILE), f, memory_space=pltpu.SMEM)
    return pl.pallas_call(
        kern,
        grid=(nt,),
        in_specs=[
            idx_spec(lambda t: (t, 0, 0)),
            idx_spec(lambda t: (jnp.minimum(t + 1, nt - 1), 0, 0)),
            pl.BlockSpec((ROW_TILE, d), lambda t: (t, 0)),
            pl.BlockSpec((ROW_TILE, TOP_K), lambda t: (t, 0)),
            pl.BlockSpec((1, SUBLANES, mod.shape[2]), lambda t: (layer, 0, 0)),
            pl.BlockSpec((1, d), lambda t: (0, 0)),
            pl.BlockSpec(memory_space=pl.ANY),
        ],
        out_specs=pl.BlockSpec((ROW_TILE, d), lambda t: (t, 0)),
        out_shape=jax.ShapeDtypeStruct((n, d), F32),
        scratch_shapes=[pltpu.VMEM((2, TOP_K * ROW_TILE // SUBLANES, ROW_SUB, SUBLANES, LANES), jnp.uint32),
                        pltpu.SemaphoreType.DMA((2,))],
        compiler_params=_cparams(("arbitrary",)),
        name="moe_combine%d" % layer,
    )(dest_t, dest_t, x1, gates.T, mod, lng, ys)


def _moe_layer(x1, h2, eidx, gates, mod, layer, lng, wg, wu, wd, *, tiles_per_batch, ctx_tiles, nb):
    dest, block_expert, n_blocks = _moe_plan(eidx)
    xs = _dispatch(h2, dest, n_blocks * MOE_ROWS)
    ys = _experts(xs, block_expert, layer, wg, wu, wd)
    return _combine(x1, dest, gates, mod, layer, lng, ys,
                    tiles_per_batch=tiles_per_batch, ctx_tiles=ctx_tiles, nb=nb)


def _half_norms(x):
    lane = lax.broadcasted_iota(jnp.int32, x.shape, 1)
    sq = x * x
    lo = jnp.sum(jnp.where(lane < DIFF_DH, sq, 0.0), axis=-1, keepdims=True)
    hi = jnp.sum(jnp.where(lane >= DIFF_DH, sq, 0.0), axis=-1, keepdims=True)
    return jnp.sqrt(lo), jnp.sqrt(hi)


def _attn_kernel(lam_ref, q_ref, k_ref, v_ref, g_ref, o_ref, vext, s_buf0, s_buf1,
                 p_buf0, p_buf1, corr_buf0, corr_buf1, m_buf, shift_buf, acc, *, out_scale, ctx_len):
    t_len = k_ref.shape[0]
    nk = t_len // ATT_TK
    dv = v_ref.shape[1]
    tq = acc.shape[1]
    nq = o_ref.shape[0] // tq
    n_tiles = nq * nk

    vext[:, 0:dv] = v_ref[...]
    vext[:, dv:2 * dv] = jnp.ones((t_len, dv), BF16)
    lam = lam_ref[0]
    gain = g_ref[...] * out_scale

    s_bufs, p_bufs, corr_bufs = (s_buf0, s_buf1), (p_buf0, p_buf1), (corr_buf0, corr_buf1)

    def key_rows(kj):
        return pl.ds(pl.multiple_of(kj * ATT_TK, ATT_TK), ATT_TK)

    def query_rows(qi):
        return pl.ds(pl.multiple_of(ctx_len + qi * tq, ROW_TILE), tq)

    def column_max(norms, carry):
        return tuple(jnp.maximum(c, jnp.max(n, axis=0, keepdims=True)) for n, c in zip(norms, carry))

    def key_norms(kj, carry):
        return column_max(_half_norms(k_ref[key_rows(kj), :].astype(F32)), carry)

    zero11 = jnp.zeros((1, 1), F32)
    kmax = lax.fori_loop(0, nk, key_norms, (zero11, zero11))

    def query_shifts(qi, carry):
        shifts = tuple(n * k for n, k in zip(_half_norms(q_ref[query_rows(qi), :].astype(F32)), kmax))
        shift_buf[qi, :, 0:1] = shifts[0]
        shift_buf[qi, :, 1:2] = shifts[1]
        return column_max(shifts, carry)

    worst = lax.fori_loop(0, nq, query_shifts, (zero11, zero11))
    bounded = jnp.max(jnp.maximum(worst[0], worst[1])) <= ATT_SAFE_SHIFT

    def advance(tile):
        qi, kj = tile
        wrap = kj + 1 == nk
        return jnp.where(wrap, qi + 1, qi), jnp.where(wrap, 0, kj + 1)

    def scores(tile, slot):
        qi, kj = tile
        q = q_ref[query_rows(qi), :]
        lane = lax.broadcasted_iota(jnp.int32, q.shape, 1)
        zero = jnp.zeros(q.shape, q.dtype)
        k = k_ref[key_rows(kj), :]
        s_bufs[slot][0] = _dot_nt(jnp.where(lane < DIFF_DH, q, zero), k)
        s_bufs[slot][1] = _dot_nt(jnp.where(lane >= DIFF_DH, q, zero), k)

    def numerators(tile, slot, online):
        qi, kj = tile
        for w in range(2):
            s = s_bufs[slot][w]
            if online:
                m_old = jnp.where(kj == 0, -jnp.inf, m_buf[w])
                m_new = jnp.maximum(m_old, jnp.max(s, axis=-1, keepdims=True))
                corr_bufs[slot][w] = jnp.exp2(m_old - m_new)
                m_buf[w] = m_new
            else:
                m_new = shift_buf[qi, :, w:w + 1]
            p_bufs[slot][w] = jnp.exp2(s - m_new).astype(BF16)

    def values(tile, slot, online):
        qi, kj = tile
        ve = vext[key_rows(kj), :]
        a = []
        for w in range(2):
            keep = corr_bufs[slot][w] if online else jnp.where(kj == 0, 0.0, 1.0)
            a.append(keep * acc[w] + _dot(p_bufs[slot][w], ve))
            acc[w] = a[w]
        o = a[0][:, 0:dv] / a[0][:, dv:2 * dv] - lam * (a[1][:, 0:dv] / a[1][:, dv:2 * dv])
        o = o * lax.rsqrt(jnp.mean(o * o, axis=-1, keepdims=True) + GN_EPS)
        o_ref[pl.ds(pl.multiple_of(qi * tq, tq), tq), :] = (o * gain).astype(BF16)

    def pipeline(online):
        def step(tiles, slot):
            a, b, c = tiles
            values(c, slot, online)
            scores(a, slot)
            numerators(b, 1 - slot, online)
            return advance(a), a, b

        acc[...] = jnp.zeros(acc.shape, F32)
        t0 = (jnp.int32(0), jnp.int32(0))
        t1 = advance(t0)
        scores(t0, 0)
        scores(t1, 1)
        numerators(t0, 0, online)

        def pair(_, tiles):
            return step(step(tiles, 0), 1)

        _, last, prev = lax.fori_loop(0, (n_tiles - 2) // 2, pair, (advance(t1), t1, t0))
        numerators(last, (n_tiles - 1) % 2, online)
        values(prev, n_tiles % 2, online)
        values(last, (n_tiles - 1) % 2, online)

    pl.when(bounded)(functools.partial(pipeline, False))
    pl.when(jnp.logical_not(bounded))(functools.partial(pipeline, True))


def _diff_attention(qkv, lam, subln_g, lambda_init, *, nb, l_len, t_len, ctx_len):
    d = D_MODEL
    tq = ATT_TQ
    dv = 2 * DIFF_DH
    nq = l_len // tq
    assert (nq * (t_len // ATT_TK)) % 2 == 0
    kern = functools.partial(_attn_kernel, out_scale=1.0 - lambda_init, ctx_len=ctx_len)
    return pl.pallas_call(
        kern,
        grid_spec=pltpu.PrefetchScalarGridSpec(
            num_scalar_prefetch=1,
            grid=(nb, DIFF_HEADS),
            in_specs=[
                pl.BlockSpec((t_len, LANES), lambda b, h, lam: (b, h)),
                pl.BlockSpec((t_len, LANES), lambda b, h, lam: (b, DIFF_HEADS + h)),
                pl.BlockSpec((t_len, LANES), lambda b, h, lam: (b, 2 * DIFF_HEADS + h)),
                pl.BlockSpec((1, LANES), lambda b, h, lam: (0, 0)),
            ],
            out_specs=pl.BlockSpec((l_len, LANES), lambda b, h, lam: (b, h)),
            scratch_shapes=[
                pltpu.VMEM((t_len, 2 * dv), BF16),
                pltpu.VMEM((2, tq, ATT_TK), F32), pltpu.VMEM((2, tq, ATT_TK), F32),
                pltpu.VMEM((2, tq, ATT_TK), BF16), pltpu.VMEM((2, tq, ATT_TK), BF16),
                pltpu.VMEM((2, tq, 1), F32), pltpu.VMEM((2, tq, 1), F32),
                pltpu.VMEM((2, tq, 1), F32), pltpu.VMEM((nq, tq, 2), F32),
                pltpu.VMEM((2, tq, 2 * dv), F32),
            ],
        ),
        out_shape=jax.ShapeDtypeStruct((nb * l_len, d), BF16),
        compiler_params=_cparams(("parallel", "parallel"), ATT_VMEM_LIMIT),
        name="diff_attention",
    )(lam, qkv, qkv, qkv, subln_g.reshape(1, LANES).astype(F32))


def _rope_angles(pos, dim):
    inv = (np.float32(ROPE_BASE) ** (-np.arange(0, dim, 2, dtype=np.float32) / np.float32(dim)))
    return pos.astype(np.float32)[:, None] * inv.astype(np.float32)[None, :]


def _ret_rope_tables(l_len, ctx_len):
    half = RET_DK // 2
    ang = _rope_angles(np.arange(l_len), RET_DK)
    ang = np.concatenate([np.zeros((ctx_len, half), np.float32), ang], axis=0)
    cos64 = np.concatenate([np.cos(ang), np.cos(ang)], axis=1)
    sin64 = np.concatenate([-np.sin(ang), np.sin(ang)], axis=1)
    kscale = np.float32(RET_DK ** -0.5)
    cos = np.concatenate([cos64, cos64 * kscale], axis=1)
    sin = np.concatenate([sin64, sin64 * kscale], axis=1)
    return jnp.asarray(cos[None], F32), jnp.asarray(sin[None], F32)


def _attn_rope_tables(l_len, ctx_len):
    quarter = DIFF_DH // 4
    pos = np.arange(l_len)
    pad = lambda a: np.concatenate([np.zeros((ctx_len, quarter), np.float32), a], axis=0)
    ang_r = pad(_rope_angles(pos // GRID_W, DIFF_DH // 2))
    ang_c = pad(_rope_angles(pos % GRID_W, DIFF_DH // 2))
    cos64 = np.concatenate([np.cos(ang_r)] * 2 + [np.cos(ang_c)] * 2, axis=1)
    sin64 = np.concatenate([-np.sin(ang_r), np.sin(ang_r), -np.sin(ang_c), np.sin(ang_c)], axis=1)
    cos = np.concatenate([cos64, cos64], axis=1)
    sin = np.concatenate([sin64, sin64], axis=1)
    qscale = np.float32(DIFF_DH ** -0.5 * math.log2(math.e))
    return (jnp.asarray(np.stack([cos * qscale, cos]), F32), jnp.asarray(np.stack([sin * qscale, sin]), F32))


def kernel(x, c, ctx, c_ctx, ada_w, ada_b, ln_g, w_in_ab, ret_decay_logit, s5_lam_re, s5_lam_im,
           s5_log_dt, s5_b_re, s5_b_im, s5_c_re, s5_c_im, s5_d, s5_w_glu, w_out_ab, w_in_c,
           diff_lambda, diff_subln_g, w_out_c, router_w, router_bias, exp_w_gate, exp_w_up,
           exp_w_down):
    nb, l_len, d = x.shape
    ctx_len = ctx.shape[1]
    t_len = ctx_len + l_len
    nt = nb * t_len
    tpb = t_len // ROW_TILE
    ctx_tiles = ctx_len // ROW_TILE
    assert d == D_MODEL and nb < SUBLANES
    assert l_len % ATT_TQ == 0 and ctx_len % ROW_TILE == 0 and t_len % ATT_TK == 0

    x_lat, x_ctx = x.reshape(nb * l_len, d), ctx.reshape(nb * ctx_len, d)
    c_all = jnp.concatenate([c, c_ctx[None].astype(c.dtype)], axis=0)
    c_pad = jnp.zeros((SUBLANES, d), F32).at[:nb + 1].set(c_all)
    mod = _adaln(c_pad, ada_w, ada_b)

    wr_t = router_w.T
    rbias = router_bias.reshape(N_EXPERTS, 1).astype(F32)

    w0 = w_in_ab[0]
    q_w, k_w, v_w, g_w, u_w = jnp.split(w0, (RET_QK_W, 2 * RET_QK_W, 2 * RET_QK_W + RET_V_W,
                                             2 * RET_QK_W + 2 * RET_V_W), axis=1)
    qk_w = jnp.concatenate([q_w.reshape(d, RET_HEADS, RET_DK), k_w.reshape(d, RET_HEADS, RET_DK)],
                           axis=2).reshape(d, 2 * RET_QK_W)
    w0p = jnp.concatenate([qk_w, v_w, g_w, u_w], axis=1).astype(BF16)
    cos0, sin0 = _ret_rope_tables(l_len, ctx_len)
    rope_tab0 = [0] * RET_HEADS + [None] * ((w0p.shape[1] - S5_CH - 2 * RET_QK_W) // LANES)
    proj0, u5 = _inproj((x_lat, x_ctx), mod, 0, w0p, cos0, sin0, rope_tab0, RET_DK // 2,
                        nb=nb, t_len=t_len, ctx_len=ctx_len, s5_cols=S5_CH)

    log_gammas = jax.nn.log_sigmoid(ret_decay_logit[0].astype(F32))
    ret = _retention(proj0, log_gammas, nb=nb, t_len=t_len, ctx_len=ctx_len)

    s5_ops = _s5_operators(s5_lam_re[0], s5_lam_im[0], s5_log_dt[0], s5_b_re[0], s5_b_im[0],
                           s5_c_re[0], s5_c_im[0], s5_d[0])
    s5y = _s5(u5, *s5_ops, nb=nb, nctx=ctx_len // S5_CHUNK)

    x1, h2, eidx, gates = _merge0(ret, proj0, s5y, x_lat, x_ctx, mod, s5_w_glu[0].astype(BF16),
                                  w_out_ab[0].astype(BF16), ln_g[0, 0].reshape(1, d), wr_t, rbias,
                                  nb=nb, t_len=t_len, ctx_len=ctx_len)
    experts = (exp_w_gate.astype(BF16), exp_w_up.astype(BF16), exp_w_down.astype(BF16))
    x2 = _moe_layer(x1, h2, eidx, gates, mod, 0, ln_g[0, 1].reshape(1, d), *experts,
                    tiles_per_batch=tpb, ctx_tiles=ctx_tiles, nb=nb)

    cos1, sin1 = _attn_rope_tables(l_len, ctx_len)
    n_heads_cols = D_MODEL // LANES
    rope_tab1 = [0] * n_heads_cols + [1] * n_heads_cols + [None] * n_heads_cols
    qkv, = _inproj((x2,), mod, 1, w_in_c[0].astype(BF16), cos1, sin1, rope_tab1, DIFF_DH // 4,
                   nb=nb, t_len=t_len, ctx_len=ctx_len)
    lf = diff_lambda[0].astype(F32)
    lambda_init = 0.8 - 0.6 * math.exp(-0.3 * 1)
    lam = (jnp.exp(jnp.sum(lf[0] * lf[1])) - jnp.exp(jnp.sum(lf[2] * lf[3])) + lambda_init).reshape(1)
    att = _diff_attention(qkv, lam, diff_subln_g[0], lambda_init,
                          nb=nb, l_len=l_len, t_len=t_len, ctx_len=ctx_len)
    x3, h3, eidx1, gates1 = _merge1(att, x2, mod, 1, w_out_c[0].astype(BF16), ln_g[1, 0].reshape(1, d),
                                    wr_t, rbias, nb=nb, l_len=l_len, t_len=t_len, ctx_len=ctx_len)
    out = _moe_layer(x3, h3, eidx1, gates1, mod, 1, ln_g[1, 1].reshape(1, d), *experts,
                     tiles_per_batch=l_len // ROW_TILE, ctx_tiles=0, nb=nb)
    return out.reshape(nb, l_len, d)
```

```python
import functools
import math

import jax
import jax.numpy as jnp
import numpy as np
from jax import lax
from jax.experimental import pallas as pl
from jax.experimental.pallas import tpu as pltpu

F32 = jnp.float32
BF16 = jnp.bfloat16

D_MODEL = 1024
DEPTH = 2
GRID_W = 64
ALPHA = (2.0 * DEPTH) ** 0.25
LN_EPS = 1e-5
GN_EPS = 1e-6
ROPE_BASE = 10000.0
RET_DK = 64
RET_DV = 128
RET_HEADS = 6
RET_QK_W = RET_HEADS * RET_DK
RET_V_W = RET_HEADS * RET_DV
S5_CH = 256
S5_P = 16
S5_G = 16
S5_N = 64
DIFF_HEADS = 8
DIFF_DH = 64
N_EXPERTS = 16
EXPERTS_PER_GROUP = 4
TOP_K = 2

LANES = 128
SUBLANES = 8
MXU_DIM = 256
ROW_TILE = 256
RET_CHUNK = 256
RET_STATE_UNROLL = 3
RET_OUT_UNROLL = 11
S5_CHUNK = 8
S5_HALVES = 2
MOE_ROWS = 256
DISPATCH_TILE = 1024
ROW_WORDS = D_MODEL // 2
ROW_SUB = ROW_WORDS // LANES
ATT_TQ = 1024
ATT_TK = 768
ATT_SAFE_SHIFT = 48.0
VMEM_LIMIT = 48 * 1024 * 1024
ATT_VMEM_LIMIT = 56 * 1024 * 1024


def _cparams(sem, vmem_limit=VMEM_LIMIT):
    return pltpu.CompilerParams(dimension_semantics=sem, vmem_limit_bytes=vmem_limit)


def _dot(a, b):
    return jnp.dot(a, b, preferred_element_type=F32)


def _dot_nt(a, b):
    return lax.dot_general(a, b, (((1,), (1,)), ((), ())), preferred_element_type=F32)


def _dot_tn(a, b):
    return lax.dot_general(a, b, (((0,), (0,)), ((), ())), preferred_element_type=F32)


def _split_bf16(x):
    hi = x.astype(BF16)
    lo = (x - hi.astype(F32)).astype(BF16)
    return hi, lo


def _dot3(a, b):
    ah, al = _split_bf16(a)
    bh, bl = _split_bf16(b)
    return _dot(ah, bh) + _dot(ah, bl) + _dot(al, bh)


def _dot3_nt(a, b):
    ah, al = _split_bf16(a)
    bh, bl = _split_bf16(b)
    return _dot_nt(ah, bh) + _dot_nt(ah, bl) + _dot_nt(al, bh)


def _sigmoid(x):
    return 1.0 / (1.0 + jnp.exp(-x))


def _silu(x):
    return x * _sigmoid(x)


def _pack_bf16_pairs(v):
    half = v.shape[1] // 2
    bits = lax.bitcast_convert_type(v.astype(BF16).astype(F32), jnp.uint32)
    return (bits[:, :half] >> 16) | (bits[:, half:] & jnp.uint32(0xFFFF0000))


def _unpack_bf16_pairs(p):
    lo = lax.bitcast_convert_type(p << 16, F32)
    hi = lax.bitcast_convert_type(p & jnp.uint32(0xFFFF0000), F32)
    return jnp.concatenate([lo, hi], axis=1)


def _row_shape(n_rows):
    return (n_rows // SUBLANES, ROW_SUB, SUBLANES, LANES)


def _row(ref, r):
    return ref.at[lax.shift_right_logical(r, 3), :, r & (SUBLANES - 1), :]


def _store_rows(ref, packed):
    rows = packed.shape[0]
    for j in range(ROW_SUB):
        ref[:, j] = packed[:, j * LANES:(j + 1) * LANES].reshape(rows // SUBLANES, SUBLANES, LANES)


def _load_rows(ref):
    rows = ref.shape[0] * SUBLANES
    return jnp.concatenate([ref[:, j].reshape(rows, LANES) for j in range(ROW_SUB)], axis=1)


def _gelu_tanh(x):
    c = math.sqrt(2.0 / math.pi)
    return 0.5 * x * (1.0 + jnp.tanh(c * (x + 0.044715 * (x * x * x))))


def _adaln_kernel(c_ref, w_ref, b_ref, o_ref):
    c = c_ref[...]
    o_ref[0] = _dot3(_silu(c), w_ref[0]) + b_ref[0]


def _adaln(c_pad, ada_w, ada_b):
    depth, d, n = ada_w.shape
    tn = 1536
    return pl.pallas_call(
        _adaln_kernel,
        grid=(depth, n // tn),
        in_specs=[
            pl.BlockSpec((SUBLANES, d), lambda i, j: (0, 0)),
            pl.BlockSpec((1, d, tn), lambda i, j: (i, 0, j)),
            pl.BlockSpec((1, 1, tn), lambda i, j: (i, 0, j)),
        ],
        out_specs=pl.BlockSpec((1, SUBLANES, tn), lambda i, j: (i, 0, j)),
        out_shape=jax.ShapeDtypeStruct((depth, SUBLANES, n), F32),
        compiler_params=_cparams(("parallel", "parallel")),
        name="adaln",
    )(c_pad, ada_w, ada_b.reshape(depth, 1, n))


def _mod_row(t, tiles_per_batch, ctx_tiles, nb):
    b = lax.div(t, tiles_per_batch)
    w = lax.rem(t, tiles_per_batch)
    return jnp.where(w < ctx_tiles, nb, b)


def _rope_block(a, cos, sin, half):
    lane = lax.broadcasted_iota(jnp.int32, a.shape, 1)
    first = lax.rem(lane, 2 * half) < half
    rot = jnp.where(first, pltpu.roll(a, LANES - half, 1), pltpu.roll(a, half, 1))
    return a * cos + rot * sin


def _chunk_perm(rows, transpose):
    per = rows // S5_CHUNK
    r = lax.broadcasted_iota(jnp.int32, (rows, rows), 0)
    c = lax.broadcasted_iota(jnp.int32, (rows, rows), 1)
    if transpose:
        r, c = c, r
    return (c == S5_CHUNK * lax.rem(r, per) + lax.div(r, per)).astype(BF16)


def _token_specs(d, tiles_per_batch, ctx_tiles):
    lat_tiles = tiles_per_batch - ctx_tiles
    b = lambda t: lax.div(t, tiles_per_batch)
    w = lambda t: lax.rem(t, tiles_per_batch)
    return (pl.BlockSpec((ROW_TILE, d), lambda t: (b(t) * lat_tiles + jnp.maximum(w(t) - ctx_tiles, 0), 0)),
            pl.BlockSpec((ROW_TILE, d), lambda t: (b(t) * ctx_tiles + jnp.minimum(w(t), ctx_tiles - 1), 0)))


def _token_tile(x_refs, tiles_per_batch, ctx_tiles):
    if len(x_refs) == 1:
        return x_refs[0][...]
    is_ctx = lax.rem(pl.program_id(0), tiles_per_batch) < ctx_tiles
    return jnp.where(is_ctx, x_refs[1][...], x_refs[0][...])


def _inproj_kernel(*refs, n_x, tiles_per_batch, ctx_tiles, nb, rope_tab, rope_half):
    x_refs, (mod_ref, w_ref, cos_ref, sin_ref, o_ref), rest = refs[:n_x], refs[n_x:n_x + 5], refs[n_x + 5:]
    d = w_ref.shape[0]
    n = o_ref.shape[1]
    r = _mod_row(pl.program_id(0), tiles_per_batch, ctx_tiles, nb)
    sh = mod_ref[0, pl.ds(r, 1), 0:d]
    sc = mod_ref[0, pl.ds(r, 1), d:2 * d]
    xm = (_token_tile(x_refs, tiles_per_batch, ctx_tiles) * (1.0 + sc) + sh).astype(BF16)
    for j in range(n // MXU_DIM):
        acc = _dot(xm, w_ref[:, j * MXU_DIM:(j + 1) * MXU_DIM])
        parts = []
        for s in range(MXU_DIM // LANES):
            blk = acc[:, s * LANES:(s + 1) * LANES]
            tab = rope_tab[j * (MXU_DIM // LANES) + s]
            if tab is not None:
                blk = _rope_block(blk, cos_ref[tab], sin_ref[tab], rope_half)
            parts.append(blk)
        o_ref[:, j * MXU_DIM:(j + 1) * MXU_DIM] = jnp.concatenate(parts, axis=1).astype(BF16)
    if rest:
        u_ref, = rest
        rows = o_ref.shape[0]
        per = rows // S5_CHUNK
        u = _dot(xm, w_ref[:, n:n + S5_CH]).astype(BF16)
        up = _dot(_chunk_perm(rows, False), u).astype(BF16)
        for s in range(S5_CHUNK):
            for hf in range(S5_CH // LANES):
                u_ref[hf, :, s * LANES:(s + 1) * LANES] = up[s * per:(s + 1) * per,
                                                             hf * LANES:(hf + 1) * LANES]


def _inproj(xs, mod, layer, w, cos, sin, rope_tab, rope_half, *, nb, t_len, ctx_len, s5_cols=0):
    nt, d = nb * t_len, w.shape[0]
    n = w.shape[1] - s5_cols
    tpb = t_len // ROW_TILE
    ctx_tiles = ctx_len // ROW_TILE
    kern = functools.partial(_inproj_kernel, n_x=len(xs), tiles_per_batch=tpb, ctx_tiles=ctx_tiles,
                             nb=nb, rope_tab=tuple(rope_tab), rope_half=rope_half)
    x_specs = ([pl.BlockSpec((ROW_TILE, d), lambda t: (t, 0))] if len(xs) == 1
               else list(_token_specs(d, tpb, ctx_tiles)))
    ntab = cos.shape[0]
    out_specs = [pl.BlockSpec((ROW_TILE, n), lambda t: (t, 0))]
    out_shape = [jax.ShapeDtypeStruct((nt, n), BF16)]
    if s5_cols:
        halves, per = s5_cols // LANES, ROW_TILE // S5_CHUNK
        out_specs.append(pl.BlockSpec((halves, per, S5_CHUNK * LANES), lambda t: (0, t, 0)))
        out_shape.append(jax.ShapeDtypeStruct((halves, nt // S5_CHUNK, S5_CHUNK * LANES), BF16))
    return pl.pallas_call(
        kern,
        grid=(nt // ROW_TILE,),
        in_specs=x_specs + [
            pl.BlockSpec((1, SUBLANES, mod.shape[2]), lambda t: (layer, 0, 0)),
            pl.BlockSpec(w.shape, lambda t: (0, 0)),
            pl.BlockSpec((ntab, ROW_TILE, LANES), lambda t: (0, lax.rem(t, tpb), 0)),
            pl.BlockSpec((ntab, ROW_TILE, LANES), lambda t: (0, lax.rem(t, tpb), 0)),
        ],
        out_specs=out_specs,
        out_shape=out_shape,
        compiler_params=_cparams(("parallel",)),
        name="inproj%d" % layer,
    )(*xs, mod, w, cos, sin)


def _retention_kernel(lg_ref, qk_ref, v_ref, o_ref, sf_ref, sb_ref, *, nctx):
    c_len = RET_CHUNK
    t_len = qk_ref.shape[0]
    nc = t_len // c_len
    h = pl.program_id(1)
    lgf = lg_ref[0, h]
    lgb = lg_ref[1, h]
    ii = lax.broadcasted_iota(jnp.int32, (c_len, 1), 0).astype(F32)
    jj = lax.broadcasted_iota(jnp.int32, (1, c_len), 1).astype(F32)
    diff = ii - jj
    decay = jnp.where(diff >= 0.0, jnp.exp(lgf * jnp.maximum(diff, 0.0)),
                      jnp.exp(lgb * jnp.maximum(-diff, 0.0)))
    kdf = jnp.exp(lgf * (c_len - 1.0 - ii))
    kdb = jnp.exp(lgb * ii)
    qdf = jnp.exp(lgf * (ii + 1.0))
    qdb = jnp.exp(lgb * (c_len - ii))
    zrow = jnp.zeros((1, RET_DV), F32)
    gf_chunk = jnp.exp(zrow + lgf * c_len)
    gb_chunk = jnp.exp(zrow + lgb * c_len)

    def load(c):
        rows = pl.ds(pl.multiple_of(c * c_len, c_len), c_len)
        qk = qk_ref[rows, :].astype(F32)
        return qk[:, :RET_DK], qk[:, RET_DK:], v_ref[rows, :]

    def states(j, carry):
        sf, sb = carry
        cb = jnp.where(j < nctx, nctx - 1 - j, nc - 1 - (j - nctx))
        sf_ref[j] = sf
        sb_ref[cb] = sb
        _, kf, vf = load(j)
        _, kb, vb = load(cb)
        return (gf_chunk * sf + _dot_tn((kf * kdf).astype(BF16), vf),
                gb_chunk * sb + _dot_tn((kb * kdb).astype(BF16), vb))

    zero_state = jnp.zeros((RET_DK, RET_DV), F32)
    lax.fori_loop(0, nc, states, (zero_state, zero_state), unroll=RET_STATE_UNROLL)

    def out_chunk(c, carry):
        q, k, v = load(c)
        scores = _dot_nt(q.astype(BF16), k.astype(BF16)) * decay
        o = _dot(scores.astype(BF16), v)
        o = o + _dot((q * qdf).astype(BF16), sf_ref[c].astype(BF16))
        o = o + _dot((q * qdb).astype(BF16), sb_ref[c].astype(BF16))
        mu = jnp.mean(o, axis=-1, keepdims=True)
        oc = o - mu
        var = jnp.mean(oc * oc, axis=-1, keepdims=True)
        rows = pl.ds(pl.multiple_of(c * c_len, c_len), c_len)
        o_ref[rows, :] = (oc * lax.rsqrt(var + GN_EPS)).astype(BF16)
        return carry

    lax.fori_loop(0, nc, out_chunk, 0, unroll=RET_OUT_UNROLL)


def _retention(proj, log_gammas, *, nb, t_len, ctx_len):
    nt = proj.shape[0]
    nc = t_len // RET_CHUNK
    kern = functools.partial(_retention_kernel, nctx=ctx_len // RET_CHUNK)
    vcol0 = RET_HEADS
    return pl.pallas_call(
        kern,
        grid_spec=pltpu.PrefetchScalarGridSpec(
            num_scalar_prefetch=1,
            grid=(nb, RET_HEADS),
            in_specs=[
                pl.BlockSpec((t_len, LANES), lambda b, h, lg: (b, h)),
                pl.BlockSpec((t_len, LANES), lambda b, h, lg: (b, vcol0 + h)),
            ],
            out_specs=pl.BlockSpec((t_len, LANES), lambda b, h, lg: (b, h)),
            scratch_shapes=[pltpu.VMEM((nc, RET_DK, RET_DV), F32),
                            pltpu.VMEM((nc, RET_DK, RET_DV), F32)],
        ),
        out_shape=jax.ShapeDtypeStruct((nt, RET_V_W), BF16),
        compiler_params=_cparams(("parallel", "parallel")),
        name="retention",
    )(log_gammas, proj, proj)


def _s5_operators(lam_re, lam_im, log_dt, b_re, b_im, c_re, c_im, d_skip):
    tc = S5_CHUNK
    hp = lax.Precision.HIGHEST
    ks = jnp.arange(tc + 1, dtype=F32)
    pw, bbar, cm = [], [], []
    for direction in range(2):
        dt = jnp.exp(log_dt[direction].astype(F32))[:, None]
        lam = lax.complex(lam_re[direction].astype(F32), lam_im[direction].astype(F32))
        z = lam * dt
        p = jnp.exp(z[None] * ks[:, None, None])
        lam_bar = p[1]
        bb = ((lam_bar - 1.0) / lam)[..., None] * lax.complex(
            b_re[direction].astype(F32), b_im[direction].astype(F32))
        pw.append(p)
        bbar.append(bb)
        cm.append(lax.complex(c_re[direction].astype(F32), c_im[direction].astype(F32)))

    def lag_kernel(p, bb, c):
        return jnp.einsum('gpn,kgn,gnq->kgpq', c, p[:tc], bb, precision=hp).real

    kf = lag_kernel(pw[0], bbar[0], cm[0])
    kb = lag_kernel(pw[1], bbar[1], cm[1])
    k0 = kf[0] + kb[0] + jnp.eye(S5_P, dtype=F32)[None] * d_skip.astype(F32)[:, :, None]
    kcat = jnp.concatenate([kb[1:][::-1], k0[None], kf[1:]], axis=0)
    s_idx = jnp.arange(tc)[:, None]
    t_idx = jnp.arange(tc)[None, :]
    m5 = kcat[t_idx - s_idx + tc - 1]
    hg = S5_G // S5_HALVES
    eye = jnp.eye(hg, dtype=F32)
    split = lambda z, axis: z.reshape(z.shape[:axis] + (S5_HALVES, hg) + z.shape[axis + 1:])
    wide = tc * hg * S5_P
    intra = jnp.einsum('sthgpq,gk->hsgqtkp', split(m5, 2), eye).reshape(S5_HALVES, wide, wide)

    ef = pw[0][:tc][::-1][:, :, :, None] * bbar[0][None]
    eb = pw[1][:tc][:, :, :, None] * bbar[1][None]
    parts_in = jnp.stack([ef.real, ef.imag, eb.real, eb.imag], axis=0)
    w_in = jnp.einsum('cshgnq,gk->hsgqckn', split(parts_in, 2), eye).reshape(
        S5_HALVES, wide, 4 * hg * S5_N)
    of = cm[0][None] * pw[0][1:][:, :, None, :]
    ob = cm[1][None] * pw[1][1:][::-1][:, :, None, :]
    parts_out = jnp.stack([of.real, -of.imag, ob.real, -ob.imag], axis=0)
    w_out = jnp.einsum('cthgpn,gk->hckntgp', split(parts_out, 2), eye).reshape(
        S5_HALVES, 4 * hg * S5_N, wide)
    a = jnp.stack([pw[0][tc].real, pw[0][tc].imag, pw[1][tc].real, pw[1][tc].imag], axis=0)
    a = a.reshape(4, S5_HALVES, hg * S5_N).transpose(1, 0, 2)
    return intra.astype(BF16), w_in.astype(BF16), w_out.astype(BF16), a


def _s5_kernel(x_ref, wi_ref, win_ref, wout_ref, a_ref, y_ref, st_ref, *, nctx):
    x = x_ref[0]
    nc = x.shape[0]
    w = a_ref.shape[2]
    st_ref[...] = _dot(x, win_ref[0])
    afr, afi, abr, abi = (a_ref[0, i:i + 1, :] for i in range(4))

    def step(j, carry):
        fr, fi, br, bi = carry
        rf = pl.ds(j, 1)
        rb = pl.ds(jnp.where(j < nctx, nctx - 1 - j, nc - 1 - (j - nctx)), 1)
        efr, efi = st_ref[rf, 0:w], st_ref[rf, w:2 * w]
        ebr, ebi = st_ref[rb, 2 * w:3 * w], st_ref[rb, 3 * w:4 * w]
        st_ref[rf, 0:w] = fr
        st_ref[rf, w:2 * w] = fi
        st_ref[rb, 2 * w:3 * w] = br
        st_ref[rb, 3 * w:4 * w] = bi
        return (afr * fr - afi * fi + efr, afr * fi + afi * fr + efi,
                abr * br - abi * bi + ebr, abr * bi + abi * br + ebi)

    z = jnp.zeros((1, w), F32)
    lax.fori_loop(0, nc, step, (z, z, z, z), unroll=4)
    y_ref[0] = (_dot(x, wi_ref[0]) + _dot(st_ref[...].astype(BF16), wout_ref[0])).astype(BF16)


def _s5(xc, intra, w_in, w_out, a, *, nb, nctx):
    halves, rows, wide = xc.shape
    nc = rows // nb
    kern = functools.partial(_s5_kernel, nctx=nctx)
    per_half = lambda arr: pl.BlockSpec((1,) + arr.shape[1:], lambda hf, b: (hf, 0, 0))
    return pl.pallas_call(
        kern,
        grid=(halves, nb),
        in_specs=[
            pl.BlockSpec((1, nc, wide), lambda hf, b: (hf, b, 0)),
            per_half(intra), per_half(w_in), per_half(w_out), per_half(a),
        ],
        out_specs=pl.BlockSpec((1, nc, wide), lambda hf, b: (hf, b, 0)),
        out_shape=jax.ShapeDtypeStruct(xc.shape, BF16),
        scratch_shapes=[pltpu.VMEM((nc, w_in.shape[2]), F32)],
        compiler_params=_cparams(("parallel", "parallel")),
        name="s5",
    )(xc, intra, w_in, w_out, a)


def _route(logits_t, bias):
    scores = _sigmoid(logits_t)
    biased = scores + bias
    s_rows = [scores[e:e + 1, :] for e in range(N_EXPERTS)]
    b_rows = [biased[e:e + 1, :] for e in range(N_EXPERTS)]
    n_groups = N_EXPERTS // EXPERTS_PER_GROUP
    best = None
    sel = None
    for g in range(n_groups):
        a, b, c, d = b_rows[4 * g:4 * g + 4]
        hi1, lo1 = jnp.maximum(a, b), jnp.minimum(a, b)
        hi2, lo2 = jnp.maximum(c, d), jnp.minimum(c, d)
        top1 = jnp.maximum(hi1, hi2)
        top2 = jnp.maximum(jnp.minimum(hi1, hi2), jnp.maximum(lo1, lo2))
        gs = top1 + top2
        if g == 0:
            best, sel = gs, jnp.zeros(gs.shape, jnp.int32)
        else:
            better = gs > best
            sel = jnp.where(better, g, sel)
            best = jnp.where(better, gs, best)
    neg = jnp.full(best.shape, -jnp.inf, F32)
    masked = [jnp.where(sel == (e // EXPERTS_PER_GROUP), b_rows[e], neg) for e in range(N_EXPERTS)]
    v1, i1, g1 = masked[0], jnp.zeros(best.shape, jnp.int32), s_rows[0]
    for e in range(1, N_EXPERTS):
        better = masked[e] > v1
        v1 = jnp.where(better, masked[e], v1)
        i1 = jnp.where(better, e, i1)
        g1 = jnp.where(better, s_rows[e], g1)
    v2, i2, g2 = neg, jnp.zeros(best.shape, jnp.int32), jnp.zeros(best.shape, F32)
    for e in range(N_EXPERTS):
        cand = jnp.where(i1 == e, neg, masked[e])
        better = cand > v2
        v2 = jnp.where(better, cand, v2)
        i2 = jnp.where(better, e, i2)
        g2 = jnp.where(better, s_rows[e], g2)
    tot = g1 + g2
    return jnp.concatenate([i1, i2], axis=0), jnp.concatenate([g1 / tot, g2 / tot], axis=0)


def _tail(x, o, mod_ref, r, lng, wr, rb, x1_ref, h2_ref, ei_ref, gt_ref):
    d = x.shape[1]
    g1 = mod_ref[0, pl.ds(r, 1), 2 * d:3 * d]
    sh2 = mod_ref[0, pl.ds(r, 1), 3 * d:4 * d]
    sc2 = mod_ref[0, pl.ds(r, 1), 4 * d:5 * d]
    y = ALPHA * x + g1 * o
    mu = jnp.mean(y, axis=-1, keepdims=True)
    yc = y - mu
    var = jnp.mean(yc * yc, axis=-1, keepdims=True)
    x1 = yc * lax.rsqrt(var + LN_EPS) * lng
    h2 = x1 * (1.0 + sc2) + sh2
    x1_ref[...] = x1
    _store_rows(h2_ref, _pack_bf16_pairs(h2))
    ei, gt = _route(_dot3_nt(wr, h2), rb)
    ei_ref[...] = ei
    gt_ref[...] = gt


def _merge0_kernel(r_ref, g_ref, s_ref, x_ref, xc_ref, mod_ref, wglu_ref, wout_ref, lng_ref, wr_ref,
                   rb_ref, x1_ref, h2_ref, ei_ref, gt_ref, *, tiles_per_batch, ctx_tiles, nb):
    r = _mod_row(pl.program_id(0), tiles_per_batch, ctx_tiles, nb)
    ret = r_ref[...].astype(F32) * _silu(g_ref[...].astype(F32))
    rows = x_ref.shape[0]
    x = _token_tile((x_ref, xc_ref), tiles_per_batch, ctx_tiles)
    sp = jnp.concatenate(
        [jnp.concatenate([s_ref[hf, :, s * LANES:(s + 1) * LANES] for hf in range(S5_HALVES)], axis=1)
         for s in range(S5_CHUNK)], axis=0)
    s5 = _dot(_chunk_perm(rows, True), sp)
    z = _dot(_gelu_tanh(s5).astype(BF16), wglu_ref[...])
    zz = z[:, :S5_CH] * _sigmoid(z[:, S5_CH:])
    o = _dot(ret.astype(BF16), wout_ref[0:RET_V_W, :]) + _dot(zz.astype(BF16), wout_ref[RET_V_W:, :])
    _tail(x, o, mod_ref, r, lng_ref[...], wr_ref[...], rb_ref[...],
          x1_ref, h2_ref, ei_ref, gt_ref)


def _merge1_kernel(a_ref, x_ref, mod_ref, wout_ref, lng_ref, wr_ref, rb_ref,
                   x1_ref, h2_ref, ei_ref, gt_ref, *, tiles_per_batch):
    r = lax.div(pl.program_id(0), tiles_per_batch)
    o = _dot(a_ref[...], wout_ref[...])
    _tail(x_ref[...], o, mod_ref, r, lng_ref[...], wr_ref[...], rb_ref[...],
          x1_ref, h2_ref, ei_ref, gt_ref)


def _tail_outs(n_rows, d):
    shapes = (jax.ShapeDtypeStruct((n_rows, d), F32),
              jax.ShapeDtypeStruct(_row_shape(n_rows), jnp.uint32),
              jax.ShapeDtypeStruct((TOP_K, n_rows), jnp.int32),
              jax.ShapeDtypeStruct((TOP_K, n_rows), F32))
    specs = (pl.BlockSpec((ROW_TILE, d), lambda t: (t, 0)),
             pl.BlockSpec(_row_shape(ROW_TILE), lambda t: (t, 0, 0, 0)),
             pl.BlockSpec((TOP_K, ROW_TILE), lambda t: (0, t)),
             pl.BlockSpec((TOP_K, ROW_TILE), lambda t: (0, t)))
    return shapes, specs


def _merge0(ret, proj, s5y, x_lat, x_ctx, mod, w_glu, w_out, lng, wr_t, rbias, *, nb, t_len, ctx_len):
    nt, d = nb * t_len, x_lat.shape[1]
    tpb = t_len // ROW_TILE
    ctx_tiles = ctx_len // ROW_TILE
    kern = functools.partial(_merge0_kernel, tiles_per_batch=tpb, ctx_tiles=ctx_tiles, nb=nb)
    shapes, specs = _tail_outs(nt, d)
    gcol = (2 * RET_QK_W + RET_V_W) // RET_V_W
    full = lambda a: pl.BlockSpec(a.shape, lambda t: (0,) * a.ndim)
    return pl.pallas_call(
        kern,
        grid=(nt // ROW_TILE,),
        in_specs=[
            pl.BlockSpec((ROW_TILE, RET_V_W), lambda t: (t, 0)),
            pl.BlockSpec((ROW_TILE, RET_V_W), lambda t: (t, gcol)),
            pl.BlockSpec((S5_HALVES, ROW_TILE // S5_CHUNK, S5_CHUNK * LANES), lambda t: (0, t, 0)),
            *_token_specs(d, tpb, ctx_tiles),
            pl.BlockSpec((1, SUBLANES, mod.shape[2]), lambda t: (0, 0, 0)),
            full(w_glu), full(w_out), full(lng), full(wr_t), full(rbias),
        ],
        out_specs=specs,
        out_shape=shapes,
        compiler_params=_cparams(("parallel",)),
        name="merge0",
    )(ret, proj, s5y, x_lat, x_ctx, mod, w_glu, w_out, lng, wr_t, rbias)


def _merge1(att, x, mod, layer, w_out, lng, wr_t, rbias, *, nb, l_len, t_len, ctx_len):
    n_lat, d = att.shape
    tpb = l_len // ROW_TILE
    tpb_t = t_len // ROW_TILE
    ctx_tiles = ctx_len // ROW_TILE
    kern = functools.partial(_merge1_kernel, tiles_per_batch=tpb)
    shapes, specs = _tail_outs(n_lat, d)
    full = lambda a: pl.BlockSpec(a.shape, lambda t: (0,) * a.ndim)
    xrow = lambda t: (lax.div(t, tpb) * tpb_t + ctx_tiles + lax.rem(t, tpb), 0)
    return pl.pallas_call(
        kern,
        grid=(n_lat // ROW_TILE,),
        in_specs=[
            pl.BlockSpec((ROW_TILE, d), lambda t: (t, 0)),
            pl.BlockSpec((ROW_TILE, d), xrow),
            pl.BlockSpec((1, SUBLANES, mod.shape[2]), lambda t: (layer, 0, 0)),
            full(w_out), full(lng), full(wr_t), full(rbias),
        ],
        out_specs=specs,
        out_shape=shapes,
        compiler_params=_cparams(("parallel",)),
        name="merge1",
    )(att, x, mod, w_out, lng, wr_t, rbias)


def _moe_plan(eidx):
    k, n = eidx.shape
    a = k * n
    e_flat = eidx.reshape(a)
    seg = TOP_K * ROW_TILE
    onehot = (e_flat[:, None] == jnp.arange(N_EXPERTS, dtype=jnp.int32)[None, :]).astype(F32)
    onehot = onehot.reshape(a // seg, seg, N_EXPERTS)
    tril = lambda m: jnp.tril(jnp.ones((m, m), F32))
    within = jnp.einsum('ij,tjk->tik', tril(seg), onehot)
    seg_total = within[:, -1, :]
    seg_end = jnp.sum(tril(a // seg)[:, :, None] * seg_total[None], axis=1)
    counts = seg_end[-1].astype(jnp.int32)
    csum = within + (seg_end - seg_total)[:, None, :]
    padded = (counts + MOE_ROWS - 1) // MOE_ROWS * MOE_ROWS
    pad_end = jnp.sum(jnp.tril(jnp.ones((N_EXPERTS, N_EXPERTS), jnp.int32)) * padded[None, :], axis=1)
    pad_start = pad_end - padded
    dest = jnp.sum(onehot * (csum - 1.0 + pad_start.astype(F32)[None, None, :]), axis=-1)
    dest = dest.reshape(a).astype(jnp.int32)
    n_blocks = -(-(a + N_EXPERTS * (MOE_ROWS - 1)) // MOE_ROWS)
    first_row = jnp.arange(n_blocks, dtype=jnp.int32) * MOE_ROWS
    block_expert = jnp.minimum(jnp.sum((pad_end[None, :] <= first_row[:, None]).astype(jnp.int32), axis=1),
                               N_EXPERTS - 1)
    return dest.reshape(k, n), block_expert, n_blocks


def _tile_rows_of(dest, tile):
    k, n = dest.shape
    return dest.reshape(k, n // tile, tile).transpose(1, 0, 2).reshape(n // tile, 1, k * tile)


def _dispatch_kernel(dest_ref, h_ref, xs_in_hbm, xs_hbm, sem):
    del xs_in_hbm
    row_tiles = h_ref.shape[0]
    rows = row_tiles * SUBLANES

    def start(i, c):
        for u in range(SUBLANES):
            for choice in range(TOP_K):
                dst = dest_ref[0, 0, choice * rows + i * SUBLANES + u]
                pltpu.make_async_copy(h_ref.at[i, :, u, :], _row(xs_hbm, dst), sem).start(priority=choice)
        return c

    lax.fori_loop(0, row_tiles, start, 0)
    for _ in range(TOP_K):
        pltpu.make_async_copy(h_ref, xs_hbm.at[pl.ds(0, row_tiles)], sem).wait()


def _dispatch(h, dest, n_rows):
    n = h.shape[0] * SUBLANES
    tile = DISPATCH_TILE if n % DISPATCH_TILE == 0 else ROW_TILE
    return pl.pallas_call(
        _dispatch_kernel,
        grid=(n // tile,),
        in_specs=[
            pl.BlockSpec((1, 1, TOP_K * tile), lambda t: (t, 0, 0), memory_space=pltpu.SMEM),
            pl.BlockSpec(_row_shape(tile), lambda t: (t, 0, 0, 0)),
            pl.BlockSpec(memory_space=pl.ANY),
        ],
        out_specs=pl.BlockSpec(memory_space=pl.ANY),
        out_shape=jax.ShapeDtypeStruct(_row_shape(n_rows), jnp.uint32),
        scratch_shapes=[pltpu.SemaphoreType.DMA],
        input_output_aliases={2: 0},
        compiler_params=_cparams(("arbitrary",)),
        name="moe_dispatch",
    )(_tile_rows_of(dest, tile), h, jnp.zeros(_row_shape(n_rows), jnp.uint32))


def _experts_kernel(be_ref, x_ref, wg_ref, wu_ref, wd_ref, o_ref):
    x = _unpack_bf16_pairs(_load_rows(x_ref)).astype(BF16)
    hg = _dot(x, wg_ref[0])
    hu = _dot(x, wu_ref[0])
    _store_rows(o_ref, _pack_bf16_pairs(_dot((_silu(hg) * hu).astype(BF16), wd_ref[0])))


def _experts(xs, block_expert, layer, wg, wu, wd):
    n_blocks = block_expert.shape[0]
    d, dff = wg.shape[2], wg.shape[3]
    rows_spec = pl.BlockSpec(_row_shape(MOE_ROWS), lambda i, be: (i, 0, 0, 0))
    return pl.pallas_call(
        _experts_kernel,
        grid_spec=pltpu.PrefetchScalarGridSpec(
            num_scalar_prefetch=1,
            grid=(n_blocks,),
            in_specs=[
                rows_spec,
                pl.BlockSpec((None, 1, d, dff), lambda i, be: (layer, be[i], 0, 0)),
                pl.BlockSpec((None, 1, d, dff), lambda i, be: (layer, be[i], 0, 0)),
                pl.BlockSpec((None, 1, dff, d), lambda i, be: (layer, be[i], 0, 0)),
            ],
            out_specs=rows_spec,
        ),
        out_shape=jax.ShapeDtypeStruct(xs.shape, jnp.uint32),
        compiler_params=_cparams(("parallel",)),
        name="moe_experts",
    )(block_expert, xs, wg, wu, wd)


def _combine_kernel(dcur_ref, dnxt_ref, x_ref, gt_ref, mod_ref, lng_ref, y_hbm, o_ref, ybuf, sem, *,
                    tiles_per_batch, ctx_tiles, nb):
    t = pl.program_id(0)
    last = pl.num_programs(0) - 1
    rows = x_ref.shape[0]
    n = TOP_K * rows
    d = x_ref.shape[1]
    slot = lax.rem(t, 2)

    row_tiles = rows // SUBLANES

    def start_tile(idx_ref, s, i):
        for choice in range(TOP_K):
            for u in range(SUBLANES):
                j = choice * rows + i * SUBLANES + u
                pltpu.make_async_copy(_row(y_hbm, idx_ref[0, 0, j]),
                                      ybuf.at[s, choice * row_tiles + i, :, u, :],
                                      sem.at[s]).start(priority=choice)

    def wait_all(s):
        pltpu.make_async_copy(ybuf.at[s], ybuf.at[s], sem.at[s]).wait()

    @pl.when(t == 0)
    def _():
        def start(i, c):
            start_tile(dcur_ref, 0, i)
            return c

        lax.fori_loop(0, row_tiles, start, 0)

    wait_all(slot)
    r = _mod_row(t, tiles_per_batch, ctx_tiles, nb)
    gt = gt_ref[...]

    def choice_rows(choice):
        tiles = pl.ds(choice * row_tiles, row_tiles)
        return jnp.concatenate([ybuf[slot, tiles, j].reshape(rows, LANES) for j in range(ROW_SUB)], axis=1)

    y0, y1 = choice_rows(0), choice_rows(1)
    for i in range(row_tiles):
        start_tile(dnxt_ref, 1 - slot, i)
    y = _unpack_bf16_pairs(y0) * gt[:, 0:1] + _unpack_bf16_pairs(y1) * gt[:, 1:2]
    g2 = mod_ref[0, pl.ds(r, 1), 5 * d:6 * d]
    z = ALPHA * x_ref[...] + g2 * y
    mu = jnp.mean(z, axis=-1, keepdims=True)
    zc = z - mu
    var = jnp.mean(zc * zc, axis=-1, keepdims=True)
    o_ref[...] = zc * lax.rsqrt(var + LN_EPS) * lng_ref[...]

    @pl.when(t == last)
    def _():
        wait_all(1 - slot)


def _combine(x1, dest, gates, mod, layer, lng, ys, *, tiles_per_batch, ctx_tiles, nb):
    n, d = x1.shape
    nt = n // ROW_TILE
    dest_t = _tile_rows_of(dest, ROW_TILE)
    kern = functools.partial(_combine_kernel, tiles_per_batch=tiles_per_batch, ctx_tiles=ctx_tiles, nb=nb)
    idx_spec = lambda f: pl.BlockSpec((1, 1, TOP_K * ROW_TILE), f, memory_space=pltpu.SMEM)
    return pl.pallas_call(
        kern,
        grid=(nt,),
        in_specs=[
            idx_spec(lambda t: (t, 0, 0)),
            idx_spec(lambda t: (jnp.minimum(t + 1, nt - 1), 0, 0)),
            pl.BlockSpec((ROW_TILE, d), lambda t: (t, 0)),
            pl.BlockSpec((ROW_TILE, TOP_K), lambda t: (t, 0)),
            pl.BlockSpec((1, SUBLANES, mod.shape[2]), lambda t: (layer, 0, 0)),
            pl.BlockSpec((1, d), lambda t: (0, 0)),
            pl.BlockSpec(memory_space=pl.ANY),
        ],
        out_specs=pl.BlockSpec((ROW_TILE, d), lambda t: (t, 0)),
        out_shape=jax.ShapeDtypeStruct((n, d), F32),
        scratch_shapes=[pltpu.VMEM((2, TOP_K * ROW_TILE // SUBLANES, ROW_SUB, SUBLANES, LANES), jnp.uint32),
                        pltpu.SemaphoreType.DMA((2,))],
        compiler_params=_cparams(("arbitrary",)),
        name="moe_combine%d" % layer,
    )(dest_t, dest_t, x1, gates.T, mod, lng, ys)


def _moe_layer(x1, h2, eidx, gates, mod, layer, lng, wg, wu, wd, *, tiles_per_batch, ctx_tiles, nb):
    dest, block_expert, n_blocks = _moe_plan(eidx)
    xs = _dispatch(h2, dest, n_blocks * MOE_ROWS)
    ys = _experts(xs, block_expert, layer, wg, wu, wd)
    return _combine(x1, dest, gates, mod, layer, lng, ys,
                    tiles_per_batch=tiles_per_batch, ctx_tiles=ctx_tiles, nb=nb)


def _half_norms(x):
    lane = lax.broadcasted_iota(jnp.int32, x.shape, 1)
    sq = x * x
    lo = jnp.sum(jnp.where(lane < DIFF_DH, sq, 0.0), axis=-1, keepdims=True)
    hi = jnp.sum(jnp.where(lane >= DIFF_DH, sq, 0.0), axis=-1, keepdims=True)
    return jnp.sqrt(lo), jnp.sqrt(hi)


def _attn_kernel(lam_ref, q_ref, k_ref, v_ref, g_ref, o_ref, vext, s_buf0, s_buf1,
                 p_buf0, p_buf1, corr_buf0, corr_buf1, m_buf, shift_buf, acc, *, out_scale, ctx_len):
    t_len = k_ref.shape[0]
    nk = t_len // ATT_TK
    dv = v_ref.shape[1]
    tq = acc.shape[1]
    nq = o_ref.shape[0] // tq
    n_tiles = nq * nk

    vext[:, 0:dv] = v_ref[...]
    vext[:, dv:2 * dv] = jnp.ones((t_len, dv), BF16)
    lam = lam_ref[0]
    gain = g_ref[...] * out_scale

    s_bufs, p_bufs, corr_bufs = (s_buf0, s_buf1), (p_buf0, p_buf1), (corr_buf0, corr_buf1)

    def key_rows(kj):
        return pl.ds(pl.multiple_of(kj * ATT_TK, ATT_TK), ATT_TK)

    def query_rows(qi):
        return pl.ds(pl.multiple_of(ctx_len + qi * tq, ROW_TILE), tq)

    def column_max(norms, carry):
        return tuple(jnp.maximum(c, jnp.max(n, axis=0, keepdims=True)) for n, c in zip(norms, carry))

    def key_norms(kj, carry):
        return column_max(_half_norms(k_ref[key_rows(kj), :].astype(F32)), carry)

    zero11 = jnp.zeros((1, 1), F32)
    kmax = lax.fori_loop(0, nk, key_norms, (zero11, zero11))

    def query_shifts(qi, carry):
        shifts = tuple(n * k for n, k in zip(_half_norms(q_ref[query_rows(qi), :].astype(F32)), kmax))
        shift_buf[qi, :, 0:1] = shifts[0]
        shift_buf[qi, :, 1:2] = shifts[1]
        return column_max(shifts, carry)

    worst = lax.fori_loop(0, nq, query_shifts, (zero11, zero11))
    bounded = jnp.max(jnp.maximum(worst[0], worst[1])) <= ATT_SAFE_SHIFT

    def advance(tile):
        qi, kj = tile
        wrap = kj + 1 == nk
        return jnp.where(wrap, qi + 1, qi), jnp.where(wrap, 0, kj + 1)

    def scores(tile, slot):
        qi, kj = tile
        q = q_ref[query_rows(qi), :]
        lane = lax.broadcasted_iota(jnp.int32, q.shape, 1)
        zero = jnp.zeros(q.shape, q.dtype)
        k = k_ref[key_rows(kj), :]
        s_bufs[slot][0] = _dot_nt(jnp.where(lane < DIFF_DH, q, zero), k)
        s_bufs[slot][1] = _dot_nt(jnp.where(lane >= DIFF_DH, q, zero), k)

    def numerators(tile, slot, online):
        qi, kj = tile
        for w in range(2):
            s = s_bufs[slot][w]
            if online:
                m_old = jnp.where(kj == 0, -jnp.inf, m_buf[w])
                m_new = jnp.maximum(m_old, jnp.max(s, axis=-1, keepdims=True))
                corr_bufs[slot][w] = jnp.exp2(m_old - m_new)
                m_buf[w] = m_new
            else:
                m_new = shift_buf[qi, :, w:w + 1]
            p_bufs[slot][w] = jnp.exp2(s - m_new).astype(BF16)

    def values(tile, slot, online):
        qi, kj = tile
        ve = vext[key_rows(kj), :]
        a = []
        for w in range(2):
            keep = corr_bufs[slot][w] if online else jnp.where(kj == 0, 0.0, 1.0)
            a.append(keep * acc[w] + _dot(p_bufs[slot][w], ve))
            acc[w] = a[w]
        o = a[0][:, 0:dv] / a[0][:, dv:2 * dv] - lam * (a[1][:, 0:dv] / a[1][:, dv:2 * dv])
        o = o * lax.rsqrt(jnp.mean(o * o, axis=-1, keepdims=True) + GN_EPS)
        o_ref[pl.ds(pl.multiple_of(qi * tq, tq), tq), :] = (o * gain).astype(BF16)

    def pipeline(online):
        def step(tiles, slot):
            a, b, c = tiles
            values(c, slot, online)
            scores(a, slot)
            numerators(b, 1 - slot, online)
            return advance(a), a, b

        acc[...] = jnp.zeros(acc.shape, F32)
        t0 = (jnp.int32(0), jnp.int32(0))
        t1 = advance(t0)
        scores(t0, 0)
        scores(t1, 1)
        numerators(t0, 0, online)

        def pair(_, tiles):
            return step(step(tiles, 0), 1)

        _, last, prev = lax.fori_loop(0, (n_tiles - 2) // 2, pair, (advance(t1), t1, t0))
        numerators(last, (n_tiles - 1) % 2, online)
        values(prev, n_tiles % 2, online)
        values(last, (n_tiles - 1) % 2, online)

    pl.when(bounded)(functools.partial(pipeline, False))
    pl.when(jnp.logical_not(bounded))(functools.partial(pipeline, True))


def _diff_attention(qkv, lam, subln_g, lambda_init, *, nb, l_len, t_len, ctx_len):
    d = D_MODEL
    tq = ATT_TQ
    dv = 2 * DIFF_DH
    nq = l_len // tq
    assert (nq * (t_len // ATT_TK)) % 2 == 0
    kern = functools.partial(_attn_kernel, out_scale=1.0 - lambda_init, ctx_len=ctx_len)
    return pl.pallas_call(
        kern,
        grid_spec=pltpu.PrefetchScalarGridSpec(
            num_scalar_prefetch=1,
            grid=(nb, DIFF_HEADS),
            in_specs=[
                pl.BlockSpec((t_len, LANES), lambda b, h, lam: (b, h)),
                pl.BlockSpec((t_len, LANES), lambda b, h, lam: (b, DIFF_HEADS + h)),
                pl.BlockSpec((t_len, LANES), lambda b, h, lam: (b, 2 * DIFF_HEADS + h)),
                pl.BlockSpec((1, LANES), lambda b, h, lam: (0, 0)),
            ],
            out_specs=pl.BlockSpec((l_len, LANES), lambda b, h, lam: (b, h)),
            scratch_shapes=[
                pltpu.VMEM((t_len, 2 * dv), BF16),
                pltpu.VMEM((2, tq, ATT_TK), F32), pltpu.VMEM((2, tq, ATT_TK), F32),
                pltpu.VMEM((2, tq, ATT_TK), BF16), pltpu.VMEM((2, tq, ATT_TK), BF16),
                pltpu.VMEM((2, tq, 1), F32), pltpu.VMEM((2, tq, 1), F32),
                pltpu.VMEM((2, tq, 1), F32), pltpu.VMEM((nq, tq, 2), F32),
                pltpu.VMEM((2, tq, 2 * dv), F32),
            ],
        ),
        out_shape=jax.ShapeDtypeStruct((nb * l_len, d), BF16),
        compiler_params=_cparams(("parallel", "parallel"), ATT_VMEM_LIMIT),
        name="diff_attention",
    )(lam, qkv, qkv, qkv, subln_g.reshape(1, LANES).astype(F32))


def _rope_angles(pos, dim):
    inv = (np.float32(ROPE_BASE) ** (-np.arange(0, dim, 2, dtype=np.float32) / np.float32(dim)))
    return pos.astype(np.float32)[:, None] * inv.astype(np.float32)[None, :]


def _ret_rope_tables(l_len, ctx_len):
    half = RET_DK // 2
    ang = _rope_angles(np.arange(l_len), RET_DK)
    ang = np.concatenate([np.zeros((ctx_len, half), np.float32), ang], axis=0)
    cos64 = np.concatenate([np.cos(ang), np.cos(ang)], axis=1)
    sin64 = np.concatenate([-np.sin(ang), np.sin(ang)], axis=1)
    kscale = np.float32(RET_DK ** -0.5)
    cos = np.concatenate([cos64, cos64 * kscale], axis=1)
    sin = np.concatenate([sin64, sin64 * kscale], axis=1)
    return jnp.asarray(cos[None], F32), jnp.asarray(sin[None], F32)


def _attn_rope_tables(l_len, ctx_len):
    quarter = DIFF_DH // 4
    pos = np.arange(l_len)
    pad = lambda a: np.concatenate([np.zeros((ctx_len, quarter), np.float32), a], axis=0)
    ang_r = pad(_rope_angles(pos // GRID_W, DIFF_DH // 2))
    ang_c = pad(_rope_angles(pos % GRID_W, DIFF_DH // 2))
    cos64 = np.concatenate([np.cos(ang_r)] * 2 + [np.cos(ang_c)] * 2, axis=1)
    sin64 = np.concatenate([-np.sin(ang_r), np.sin(ang_r), -np.sin(ang_c), np.sin(ang_c)], axis=1)
    cos = np.concatenate([cos64, cos64], axis=1)
    sin = np.concatenate([sin64, sin64], axis=1)
    qscale = np.float32(DIFF_DH ** -0.5 * math.log2(math.e))
    return (jnp.asarray(np.stack([cos * qscale, cos]), F32), jnp.asarray(np.stack([sin * qscale, sin]), F32))


def kernel(x, c, ctx, c_ctx, ada_w, ada_b, ln_g, w_in_ab, ret_decay_logit, s5_lam_re, s5_lam_im,
           s5_log_dt, s5_b_re, s5_b_im, s5_c_re, s5_c_im, s5_d, s5_w_glu, w_out_ab, w_in_c,
           diff_lambda, diff_subln_g, w_out_c, router_w, router_bias, exp_w_gate, exp_w_up,
           exp_w_down):
    nb, l_len, d = x.shape
    ctx_len = ctx.shape[1]
    t_len = ctx_len + l_len
    nt = nb * t_len
    tpb = t_len // ROW_TILE
    ctx_tiles = ctx_len // ROW_TILE
    assert d == D_MODEL and nb < SUBLANES
    assert l_len % ATT_TQ == 0 and ctx_len % ROW_TILE == 0 and t_len % ATT_TK == 0

    x_lat, x_ctx = x.reshape(nb * l_len, d), ctx.reshape(nb * ctx_len, d)
    c_all = jnp.concatenate([c, c_ctx[None].astype(c.dtype)], axis=0)
    c_pad = jnp.zeros((SUBLANES, d), F32).at[:nb + 1].set(c_all)
    mod = _adaln(c_pad, ada_w, ada_b)

    wr_t = router_w.T
    rbias = router_bias.reshape(N_EXPERTS, 1).astype(F32)

    w0 = w_in_ab[0]
    q_w, k_w, v_w, g_w, u_w = jnp.split(w0, (RET_QK_W, 2 * RET_QK_W, 2 * RET_QK_W + RET_V_W,
                                             2 * RET_QK_W + 2 * RET_V_W), axis=1)
    qk_w = jnp.concatenate([q_w.reshape(d, RET_HEADS, RET_DK), k_w.reshape(d, RET_HEADS, RET_DK)],
                           axis=2).reshape(d, 2 * RET_QK_W)
    w0p = jnp.concatenate([qk_w, v_w, g_w, u_w], axis=1).astype(BF16)
    cos0, sin0 = _ret_rope_tables(l_len, ctx_len)
    rope_tab0 = [0] * RET_HEADS + [None] * ((w0p.shape[1] - S5_CH - 2 * RET_QK_W) // LANES)
    proj0, u5 = _inproj((x_lat, x_ctx), mod, 0, w0p, cos0, sin0, rope_tab0, RET_DK // 2,
                        nb=nb, t_len=t_len, ctx_len=ctx_len, s5_cols=S5_CH)

    log_gammas = jax.nn.log_sigmoid(ret_decay_logit[0].astype(F32))
    ret = _retention(proj0, log_gammas, nb=nb, t_len=t_len, ctx_len=ctx_len)

    s5_ops = _s5_operators(s5_lam_re[0], s5_lam_im[0], s5_log_dt[0], s5_b_re[0], s5_b_im[0],
                           s5_c_re[0], s5_c_im[0], s5_d[0])
    s5y = _s5(u5, *s5_ops, nb=nb, nctx=ctx_len // S5_CHUNK)

    x1, h2, eidx, gates = _merge0(ret, proj0, s5y, x_lat, x_ctx, mod, s5_w_glu[0].astype(BF16),
                                  w_out_ab[0].astype(BF16), ln_g[0, 0].reshape(1, d), wr_t, rbias,
                                  nb=nb, t_len=t_len, ctx_len=ctx_len)
    experts = (exp_w_gate.astype(BF16), exp_w_up.astype(BF16), exp_w_down.astype(BF16))
    x2 = _moe_layer(x1, h2, eidx, gates, mod, 0, ln_g[0, 1].reshape(1, d), *experts,
                    tiles_per_batch=tpb, ctx_tiles=ctx_tiles, nb=nb)

    cos1, sin1 = _attn_rope_tables(l_len, ctx_len)
    n_heads_cols = D_MODEL // LANES
    rope_tab1 = [0] * n_heads_cols + [1] * n_heads_cols + [None] * n_heads_cols
    qkv, = _inproj((x2,), mod, 1, w_in_c[0].astype(BF16), cos1, sin1, rope_tab1, DIFF_DH // 4,
                   nb=nb, t_len=t_len, ctx_len=ctx_len)
    lf = diff_lambda[0].astype(F32)
    lambda_init = 0.8 - 0.6 * math.exp(-0.3 * 1)
    lam = (jnp.exp(jnp.sum(lf[0] * lf[1])) - jnp.exp(jnp.sum(lf[2] * lf[3])) + lambda_init).reshape(1)
    att = _diff_attention(qkv, lam, diff_subln_g[0], lambda_init,
                          nb=nb, l_len=l_len, t_len=t_len, ctx_len=ctx_len)
    x3, h3, eidx1, gates1 = _merge1(att, x2, mod, 1, w_out_c[0].astype(BF16), ln_g[1, 0].reshape(1, d),
                                    wr_t, rbias, nb=nb, l_len=l_len, t_len=t_len, ctx_len=ctx_len)
    out = _moe_layer(x3, h3, eidx1, gates1, mod, 1, ln_g[1, 1].reshape(1, d), *experts,
                     tiles_per_batch=l_len // ROW_TILE, ctx_tiles=0, nb=nb)
    return out.reshape(nb, l_len, d)
```

```python
import functools
import math

import jax
import jax.numpy as jnp
import numpy as np
from jax import lax
from jax.experimental import pallas as pl
from jax.experimental.pallas import tpu as pltpu

F32 = jnp.float32
BF16 = jnp.bfloat16

D_MODEL = 1024
DEPTH = 2
GRID_W = 64
ALPHA = (2.0 * DEPTH) ** 0.25
LN_EPS = 1e-5
GN_EPS = 1e-6
ROPE_BASE = 10000.0
RET_DK = 64
RET_DV = 128
RET_HEADS = 6
RET_QK_W = RET_HEADS * RET_DK
RET_V_W = RET_HEADS * RET_DV
S5_CH = 256
S5_P = 16
S5_G = 16
S5_N = 64
DIFF_HEADS = 8
DIFF_DH = 64
N_EXPERTS = 16
EXPERTS_PER_GROUP = 4
TOP_K = 2

LANES = 128
SUBLANES = 8
MXU_DIM = 256
ROW_TILE = 256
RET_CHUNK = 256
RET_STATE_UNROLL = 3
RET_OUT_UNROLL = 11
S5_CHUNK = 8
S5_HALVES = 2
MOE_ROWS = 256
DISPATCH_TILE = 1024
ROW_WORDS = D_MODEL // 2
ROW_SUB = ROW_WORDS // LANES
ATT_TQ = 1024
ATT_TK = 768
ATT_SAFE_SHIFT = 48.0
VMEM_LIMIT = 48 * 1024 * 1024
ATT_VMEM_LIMIT = 56 * 1024 * 1024


def _cparams(sem, vmem_limit=VMEM_LIMIT):
    return pltpu.CompilerParams(dimension_semantics=sem, vmem_limit_bytes=vmem_limit)


def _dot(a, b):
    return jnp.dot(a, b, preferred_element_type=F32)


def _dot_nt(a, b):
    return lax.dot_general(a, b, (((1,), (1,)), ((), ())), preferred_element_type=F32)


def _dot_tn(a, b):
    return lax.dot_general(a, b, (((0,), (0,)), ((), ())), preferred_element_type=F32)


def _split_bf16(x):
    hi = x.astype(BF16)
    lo = (x - hi.astype(F32)).astype(BF16)
    return hi, lo


def _dot3(a, b):
    ah, al = _split_bf16(a)
    bh, bl = _split_bf16(b)
    return _dot(ah, bh) + _dot(ah, bl) + _dot(al, bh)


def _dot3_nt(a, b):
    ah, al = _split_bf16(a)
    bh, bl = _split_bf16(b)
    return _dot_nt(ah, bh) + _dot_nt(ah, bl) + _dot_nt(al, bh)


def _sigmoid(x):
    return 1.0 / (1.0 + jnp.exp(-x))


def _silu(x):
    return x * _sigmoid(x)


def _pack_bf16_pairs(v):
    half = v.shape[1] // 2
    bits = lax.bitcast_convert_type(v.astype(BF16).astype(F32), jnp.uint32)
    return (bits[:, :half] >> 16) | (bits[:, half:] & jnp.uint32(0xFFFF0000))


def _unpack_bf16_pairs(p):
    lo = lax.bitcast_convert_type(p << 16, F32)
    hi = lax.bitcast_convert_type(p & jnp.uint32(0xFFFF0000), F32)
    return jnp.concatenate([lo, hi], axis=1)


def _row_shape(n_rows):
    return (n_rows // SUBLANES, ROW_SUB, SUBLANES, LANES)


def _row(ref, r):
    return ref.at[lax.shift_right_logical(r, 3), :, r & (SUBLANES - 1), :]


def _store_rows(ref, packed):
    rows = packed.shape[0]
    for j in range(ROW_SUB):
        ref[:, j] = packed[:, j * LANES:(j + 1) * LANES].reshape(rows // SUBLANES, SUBLANES, LANES)


def _load_rows(ref):
    rows = ref.shape[0] * SUBLANES
    return jnp.concatenate([ref[:, j].reshape(rows, LANES) for j in range(ROW_SUB)], axis=1)


def _gelu_tanh(x):
    c = math.sqrt(2.0 / math.pi)
    return 0.5 * x * (1.0 + jnp.tanh(c * (x + 0.044715 * (x * x * x))))


def _adaln_kernel(c_ref, w_ref, b_ref, o_ref):
    c = c_ref[...]
    o_ref[0] = _dot3(_silu(c), w_ref[0]) + b_ref[0]


def _adaln(c_pad, ada_w, ada_b):
    depth, d, n = ada_w.shape
    tn = 1536
    return pl.pallas_call(
        _adaln_kernel,
        grid=(depth, n // tn),
        in_specs=[
            pl.BlockSpec((SUBLANES, d), lambda i, j: (0, 0)),
            pl.BlockSpec((1, d, tn), lambda i, j: (i, 0, j)),
            pl.BlockSpec((1, 1, tn), lambda i, j: (i, 0, j)),
        ],
        out_specs=pl.BlockSpec((1, SUBLANES, tn), lambda i, j: (i, 0, j)),
        out_shape=jax.ShapeDtypeStruct((depth, SUBLANES, n), F32),
        compiler_params=_cparams(("parallel", "parallel")),
        name="adaln",
    )(c_pad, ada_w, ada_b.reshape(depth, 1, n))


def _mod_row(t, tiles_per_batch, ctx_tiles, nb):
    b = lax.div(t, tiles_per_batch)
    w = lax.rem(t, tiles_per_batch)
    return jnp.where(w < ctx_tiles, nb, b)


def _rope_block(a, cos, sin, half):
    lane = lax.broadcasted_iota(jnp.int32, a.shape, 1)
    first = lax.rem(lane, 2 * half) < half
    rot = jnp.where(first, pltpu.roll(a, LANES - half, 1), pltpu.roll(a, half, 1))
    return a * cos + rot * sin


def _chunk_perm(rows, transpose):
    per = rows // S5_CHUNK
    r = lax.broadcasted_iota(jnp.int32, (rows, rows), 0)
    c = lax.broadcasted_iota(jnp.int32, (rows, rows), 1)
    if transpose:
        r, c = c, r
    return (c == S5_CHUNK * lax.rem(r, per) + lax.div(r, per)).astype(BF16)


def _token_specs(d, tiles_per_batch, ctx_tiles):
    lat_tiles = tiles_per_batch - ctx_tiles
    b = lambda t: lax.div(t, tiles_per_batch)
    w = lambda t: lax.rem(t, tiles_per_batch)
    return (pl.BlockSpec((ROW_TILE, d), lambda t: (b(t) * lat_tiles + jnp.maximum(w(t) - ctx_tiles, 0), 0)),
            pl.BlockSpec((ROW_TILE, d), lambda t: (b(t) * ctx_tiles + jnp.minimum(w(t), ctx_tiles - 1), 0)))


def _token_tile(x_refs, tiles_per_batch, ctx_tiles):
    if len(x_refs) == 1:
        return x_refs[0][...]
    is_ctx = lax.rem(pl.program_id(0), tiles_per_batch) < ctx_tiles
    return jnp.where(is_ctx, x_refs[1][...], x_refs[0][...])


def _inproj_kernel(*refs, n_x, tiles_per_batch, ctx_tiles, nb, rope_tab, rope_half):
    x_refs, (mod_ref, w_ref, cos_ref, sin_ref, o_ref), rest = refs[:n_x], refs[n_x:n_x + 5], refs[n_x + 5:]
    d = w_ref.shape[0]
    n = o_ref.shape[1]
    r = _mod_row(pl.program_id(0), tiles_per_batch, ctx_tiles, nb)
    sh = mod_ref[0, pl.ds(r, 1), 0:d]
    sc = mod_ref[0, pl.ds(r, 1), d:2 * d]
    xm = (_token_tile(x_refs, tiles_per_batch, ctx_tiles) * (1.0 + sc) + sh).astype(BF16)
    for j in range(n // MXU_DIM):
        acc = _dot(xm, w_ref[:, j * MXU_DIM:(j + 1) * MXU_DIM])
        parts = []
        for s in range(MXU_DIM // LANES):
            blk = acc[:, s * LANES:(s + 1) * LANES]
            tab = rope_tab[j * (MXU_DIM // LANES) + s]
            if tab is not None:
                blk = _rope_block(blk, cos_ref[tab], sin_ref[tab], rope_half)
            parts.append(blk)
        o_ref[:, j * MXU_DIM:(j + 1) * MXU_DIM] = jnp.concatenate(parts, axis=1).astype(BF16)
    if rest:
        u_ref, = rest
        rows = o_ref.shape[0]
        per = rows // S5_CHUNK
        u = _dot(xm, w_ref[:, n:n + S5_CH]).astype(BF16)
        up = _dot(_chunk_perm(rows, False), u).astype(BF16)
        for s in range(S5_CHUNK):
            for hf in range(S5_CH // LANES):
                u_ref[hf, :, s * LANES:(s + 1) * LANES] = up[s * per:(s + 1) * per,
                                                             hf * LANES:(hf + 1) * LANES]


def _inproj(xs, mod, layer, w, cos, sin, rope_tab, rope_half, *, nb, t_len, ctx_len, s5_cols=0):
    nt, d = nb * t_len, w.shape[0]
    n = w.shape[1] - s5_cols
    tpb = t_len // ROW_TILE
    ctx_tiles = ctx_len // ROW_TILE
    kern = functools.partial(_inproj_kernel, n_x=len(xs), tiles_per_batch=tpb, ctx_tiles=ctx_tiles,
                             nb=nb, rope_tab=tuple(rope_tab), rope_half=rope_half)
    x_specs = ([pl.BlockSpec((ROW_TILE, d), lambda t: (t, 0))] if len(xs) == 1
               else list(_token_specs(d, tpb, ctx_tiles)))
    ntab = cos.shape[0]
    out_specs = [pl.BlockSpec((ROW_TILE, n), lambda t: (t, 0))]
    out_shape = [jax.ShapeDtypeStruct((nt, n), BF16)]
    if s5_cols:
        halves, per = s5_cols // LANES, ROW_TILE // S5_CHUNK
        out_specs.append(pl.BlockSpec((halves, per, S5_CHUNK * LANES), lambda t: (0, t, 0)))
        out_shape.append(jax.ShapeDtypeStruct((halves, nt // S5_CHUNK, S5_CHUNK * LANES), BF16))
    return pl.pallas_call(
        kern,
        grid=(nt // ROW_TILE,),
        in_specs=x_specs + [
            pl.BlockSpec((1, SUBLANES, mod.shape[2]), lambda t: (layer, 0, 0)),
            pl.BlockSpec(w.shape, lambda t: (0, 0)),
            pl.BlockSpec((ntab, ROW_TILE, LANES), lambda t: (0, lax.rem(t, tpb), 0)),
            pl.BlockSpec((ntab, ROW_TILE, LANES), lambda t: (0, lax.rem(t, tpb), 0)),
        ],
        out_specs=out_specs,
        out_shape=out_shape,
        compiler_params=_cparams(("parallel",)),
        name="inproj%d" % layer,
    )(*xs, mod, w, cos, sin)


def _retention_kernel(lg_ref, qk_ref, v_ref, o_ref, sf_ref, sb_ref, *, nctx):
    c_len = RET_CHUNK
    t_len = qk_ref.shape[0]
    nc = t_len // c_len
    h = pl.program_id(1)
    lgf = lg_ref[0, h]
    lgb = lg_ref[1, h]
    ii = lax.broadcasted_iota(jnp.int32, (c_len, 1), 0).astype(F32)
    jj = lax.broadcasted_iota(jnp.int32, (1, c_len), 1).astype(F32)
    diff = ii - jj
    decay = jnp.where(diff >= 0.0, jnp.exp(lgf * jnp.maximum(diff, 0.0)),
                      jnp.exp(lgb * jnp.maximum(-diff, 0.0)))
    kdf = jnp.exp(lgf * (c_len - 1.0 - ii))
    kdb = jnp.exp(lgb * ii)
    qdf = jnp.exp(lgf * (ii + 1.0))
    qdb = jnp.exp(lgb * (c_len - ii))
    zrow = jnp.zeros((1, RET_DV), F32)
    gf_chunk = jnp.exp(zrow + lgf * c_len)
    gb_chunk = jnp.exp(zrow + lgb * c_len)

    def load(c):
        rows = pl.ds(pl.multiple_of(c * c_len, c_len), c_len)
        qk = qk_ref[rows, :].astype(F32)
        return qk[:, :RET_DK], qk[:, RET_DK:], v_ref[rows, :]

    def states(j, carry):
        sf, sb = carry
        cb = jnp.where(j < nctx, nctx - 1 - j, nc - 1 - (j - nctx))
        sf_ref[j] = sf
        sb_ref[cb] = sb
        _, kf, vf = load(j)
        _, kb, vb = load(cb)
        return (gf_chunk * sf + _dot_tn((kf * kdf).astype(BF16), vf),
                gb_chunk * sb + _dot_tn((kb * kdb).astype(BF16), vb))

    zero_state = jnp.zeros((RET_DK, RET_DV), F32)
    lax.fori_loop(0, nc, states, (zero_state, zero_state), unroll=RET_STATE_UNROLL)

    def out_chunk(c, carry):
        q, k, v = load(c)
        scores = _dot_nt(q.astype(BF16), k.astype(BF16)) * decay
        o = _dot(scores.astype(BF16), v)
        o = o + _dot((q * qdf).astype(BF16), sf_ref[c].astype(BF16))
        o = o + _dot((q * qdb).astype(BF16), sb_ref[c].astype(BF16))
        mu = jnp.mean(o, axis=-1, keepdims=True)
        oc = o - mu
        var = jnp.mean(oc * oc, axis=-1, keepdims=True)
        rows = pl.ds(pl.multiple_of(c * c_len, c_len), c_len)
        o_ref[rows, :] = (oc * lax.rsqrt(var + GN_EPS)).astype(BF16)
        return carry

    lax.fori_loop(0, nc, out_chunk, 0, unroll=RET_OUT_UNROLL)


def _retention(proj, log_gammas, *, nb, t_len, ctx_len):
    nt = proj.shape[0]
    nc = t_len // RET_CHUNK
    kern = functools.partial(_retention_kernel, nctx=ctx_len // RET_CHUNK)
    vcol0 = RET_HEADS
    return pl.pallas_call(
        kern,
        grid_spec=pltpu.PrefetchScalarGridSpec(
            num_scalar_prefetch=1,
            grid=(nb, RET_HEADS),
            in_specs=[
                pl.BlockSpec((t_len, LANES), lambda b, h, lg: (b, h)),
                pl.BlockSpec((t_len, LANES), lambda b, h, lg: (b, vcol0 + h)),
            ],
            out_specs=pl.BlockSpec((t_len, LANES), lambda b, h, lg: (b, h)),
            scratch_shapes=[pltpu.VMEM((nc, RET_DK, RET_DV), F32),
                            pltpu.VMEM((nc, RET_DK, RET_DV), F32)],
        ),
        out_shape=jax.ShapeDtypeStruct((nt, RET_V_W), BF16),
        compiler_params=_cparams(("parallel", "parallel")),
        name="retention",
    )(log_gammas, proj, proj)


def _s5_operators(lam_re, lam_im, log_dt, b_re, b_im, c_re, c_im, d_skip):
    tc = S5_CHUNK
    hp = lax.Precision.HIGHEST
    ks = jnp.arange(tc + 1, dtype=F32)
    pw, bbar, cm = [], [], []
    for direction in range(2):
        dt = jnp.exp(log_dt[direction].astype(F32))[:, None]
        lam = lax.complex(lam_re[direction].astype(F32), lam_im[direction].astype(F32))
        z = lam * dt
        p = jnp.exp(z[None] * ks[:, None, None])
        lam_bar = p[1]
        bb = ((lam_bar - 1.0) / lam)[..., None] * lax.complex(
            b_re[direction].astype(F32), b_im[direction].astype(F32))
        pw.append(p)
        bbar.append(bb)
        cm.append(lax.complex(c_re[direction].astype(F32), c_im[direction].astype(F32)))

    def lag_kernel(p, bb, c):
        return jnp.einsum('gpn,kgn,gnq->kgpq', c, p[:tc], bb, precision=hp).real

    kf = lag_kernel(pw[0], bbar[0], cm[0])
    kb = lag_kernel(pw[1], bbar[1], cm[1])
    k0 = kf[0] + kb[0] + jnp.eye(S5_P, dtype=F32)[None] * d_skip.astype(F32)[:, :, None]
    kcat = jnp.concatenate([kb[1:][::-1], k0[None], kf[1:]], axis=0)
    s_idx = jnp.arange(tc)[:, None]
    t_idx = jnp.arange(tc)[None, :]
    m5 = kcat[t_idx - s_idx + tc - 1]
    hg = S5_G // S5_HALVES
    eye = jnp.eye(hg, dtype=F32)
    split = lambda z, axis: z.reshape(z.shape[:axis] + (S5_HALVES, hg) + z.shape[axis + 1:])
    wide = tc * hg * S5_P
    intra = jnp.einsum('sthgpq,gk->hsgqtkp', split(m5, 2), eye).reshape(S5_HALVES, wide, wide)

    ef = pw[0][:tc][::-1][:, :, :, None] * bbar[0][None]
    eb = pw[1][:tc][:, :, :, None] * bbar[1][None]
    parts_in = jnp.stack([ef.real, ef.imag, eb.real, eb.imag], axis=0)
    w_in = jnp.einsum('cshgnq,gk->hsgqckn', split(parts_in, 2), eye).reshape(
        S5_HALVES, wide, 4 * hg * S5_N)
    of = cm[0][None] * pw[0][1:][:, :, None, :]
    ob = cm[1][None] * pw[1][1:][::-1][:, :, None, :]
    parts_out = jnp.stack([of.real, -of.imag, ob.real, -ob.imag], axis=0)
    w_out = jnp.einsum('cthgpn,gk->hckntgp', split(parts_out, 2), eye).reshape(
        S5_HALVES, 4 * hg * S5_N, wide)
    a = jnp.stack([pw[0][tc].real, pw[0][tc].imag, pw[1][tc].real, pw[1][tc].imag], axis=0)
    a = a.reshape(4, S5_HALVES, hg * S5_N).transpose(1, 0, 2)
    return intra.astype(BF16), w_in.astype(BF16), w_out.astype(BF16), a


def _s5_kernel(x_ref, wi_ref, win_ref, wout_ref, a_ref, y_ref, st_ref, *, nctx):
    x = x_ref[0]
    nc = x.shape[0]
    w = a_ref.shape[2]
    st_ref[...] = _dot(x, win_ref[0])
    afr, afi, abr, abi = (a_ref[0, i:i + 1, :] for i in range(4))

    def step(j, carry):
        fr, fi, br, bi = carry
        rf = pl.ds(j, 1)
        rb = pl.ds(jnp.where(j < nctx, nctx - 1 - j, nc - 1 - (j - nctx)), 1)
        efr, efi = st_ref[rf, 0:w], st_ref[rf, w:2 * w]
        ebr, ebi = st_ref[rb, 2 * w:3 * w], st_ref[rb, 3 * w:4 * w]
        st_ref[rf, 0:w] = fr
        st_ref[rf, w:2 * w] = fi
        st_ref[rb, 2 * w:3 * w] = br
        st_ref[rb, 3 * w:4 * w] = bi
        return (afr * fr - afi * fi + efr, afr * fi + afi * fr + efi,
                abr * br - abi * bi + ebr, abr * bi + abi * br + ebi)

    z = jnp.zeros((1, w), F32)
    lax.fori_loop(0, nc, step, (z, z, z, z), unroll=4)
    y_ref[0] = (_dot(x, wi_ref[0]) + _dot(st_ref[...].astype(BF16), wout_ref[0])).astype(BF16)


def _s5(xc, intra, w_in, w_out, a, *, nb, nctx):
    halves, rows, wide = xc.shape
    nc = rows // nb
    kern = functools.partial(_s5_kernel, nctx=nctx)
    per_half = lambda arr: pl.BlockSpec((1,) + arr.shape[1:], lambda hf, b: (hf, 0, 0))
    return pl.pallas_call(
        kern,
        grid=(halves, nb),
        in_specs=[
            pl.BlockSpec((1, nc, wide), lambda hf, b: (hf, b, 0)),
            per_half(intra), per_half(w_in), per_half(w_out), per_half(a),
        ],
        out_specs=pl.BlockSpec((1, nc, wide), lambda hf, b: (hf, b, 0)),
        out_shape=jax.ShapeDtypeStruct(xc.shape, BF16),
        scratch_shapes=[pltpu.VMEM((nc, w_in.shape[2]), F32)],
        compiler_params=_cparams(("parallel", "parallel")),
        name="s5",
    )(xc, intra, w_in, w_out, a)


def _route(logits_t, bias):
    scores = _sigmoid(logits_t)
    biased = scores + bias
    s_rows = [scores[e:e + 1, :] for e in range(N_EXPERTS)]
    b_rows = [biased[e:e + 1, :] for e in range(N_EXPERTS)]
    n_groups = N_EXPERTS // EXPERTS_PER_GROUP
    best = None
    sel = None
    for g in range(n_groups):
        a, b, c, d = b_rows[4 * g:4 * g + 4]
        hi1, lo1 = jnp.maximum(a, b), jnp.minimum(a, b)
        hi2, lo2 = jnp.maximum(c, d), jnp.minimum(c, d)
        top1 = jnp.maximum(hi1, hi2)
        top2 = jnp.maximum(jnp.minimum(hi1, hi2), jnp.maximum(lo1, lo2))
        gs = top1 + top2
        if g == 0:
            best, sel = gs, jnp.zeros(gs.shape, jnp.int32)
        else:
            better = gs > best
            sel = jnp.where(better, g, sel)
            best = jnp.where(better, gs, best)
    neg = jnp.full(best.shape, -jnp.inf, F32)
    masked = [jnp.where(sel == (e // EXPERTS_PER_GROUP), b_rows[e], neg) for e in range(N_EXPERTS)]
    v1, i1, g1 = masked[0], jnp.zeros(best.shape, jnp.int32), s_rows[0]
    for e in range(1, N_EXPERTS):
        better = masked[e] > v1
        v1 = jnp.where(better, masked[e], v1)
        i1 = jnp.where(better, e, i1)
        g1 = jnp.where(better, s_rows[e], g1)
    v2, i2, g2 = neg, jnp.zeros(best.shape, jnp.int32), jnp.zeros(best.shape, F32)
    for e in range(N_EXPERTS):
        cand = jnp.where(i1 == e, neg, masked[e])
        better = cand > v2
        v2 = jnp.where(better, cand, v2)
        i2 = jnp.where(better, e, i2)
        g2 = jnp.where(better, s_rows[e], g2)
    tot = g1 + g2
    return jnp.concatenate([i1, i2], axis=0), jnp.concatenate([g1 / tot, g2 / tot], axis=0)


def _tail(x, o, mod_ref, r, lng, wr, rb, x1_ref, h2_ref, ei_ref, gt_ref):
    d = x.shape[1]
    g1 = mod_ref[0, pl.ds(r, 1), 2 * d:3 * d]
    sh2 = mod_ref[0, pl.ds(r, 1), 3 * d:4 * d]
    sc2 = mod_ref[0, pl.ds(r, 1), 4 * d:5 * d]
    y = ALPHA * x + g1 * o
    mu = jnp.mean(y, axis=-1, keepdims=True)
    yc = y - mu
    var = jnp.mean(yc * yc, axis=-1, keepdims=True)
    x1 = yc * lax.rsqrt(var + LN_EPS) * lng
    h2 = x1 * (1.0 + sc2) + sh2
    x1_ref[...] = x1
    _store_rows(h2_ref, _pack_bf16_pairs(h2))
    ei, gt = _route(_dot3_nt(wr, h2), rb)
    ei_ref[...] = ei
    gt_ref[...] = gt


def _merge0_kernel(r_ref, g_ref, s_ref, x_ref, xc_ref, mod_ref, wglu_ref, wout_ref, lng_ref, wr_ref,
                   rb_ref, x1_ref, h2_ref, ei_ref, gt_ref, *, tiles_per_batch, ctx_tiles, nb):
    r = _mod_row(pl.program_id(0), tiles_per_batch, ctx_tiles, nb)
    ret = r_ref[...].astype(F32) * _silu(g_ref[...].astype(F32))
    rows = x_ref.shape[0]
    x = _token_tile((x_ref, xc_ref), tiles_per_batch, ctx_tiles)
    sp = jnp.concatenate(
        [jnp.concatenate([s_ref[hf, :, s * LANES:(s + 1) * LANES] for hf in range(S5_HALVES)], axis=1)
         for s in range(S5_CHUNK)], axis=0)
    s5 = _dot(_chunk_perm(rows, True), sp)
    z = _dot(_gelu_tanh(s5).astype(BF16), wglu_ref[...])
    zz = z[:, :S5_CH] * _sigmoid(z[:, S5_CH:])
    o = _dot(ret.astype(BF16), wout_ref[0:RET_V_W, :]) + _dot(zz.astype(BF16), wout_ref[RET_V_W:, :])
    _tail(x, o, mod_ref, r, lng_ref[...], wr_ref[...], rb_ref[...],
          x1_ref, h2_ref, ei_ref, gt_ref)


def _merge1_kernel(a_ref, x_ref, mod_ref, wout_ref, lng_ref, wr_ref, rb_ref,
                   x1_ref, h2_ref, ei_ref, gt_ref, *, tiles_per_batch):
    r = lax.div(pl.program_id(0), tiles_per_batch)
    o = _dot(a_ref[...], wout_ref[...])
    _tail(x_ref[...], o, mod_ref, r, lng_ref[...], wr_ref[...], rb_ref[...],
          x1_ref, h2_ref, ei_ref, gt_ref)


def _tail_outs(n_rows, d):
    shapes = (jax.ShapeDtypeStruct((n_rows, d), F32),
              jax.ShapeDtypeStruct(_row_shape(n_rows), jnp.uint32),
              jax.ShapeDtypeStruct((TOP_K, n_rows), jnp.int32),
              jax.ShapeDtypeStruct((TOP_K, n_rows), F32))
    specs = (pl.BlockSpec((ROW_TILE, d), lambda t: (t, 0)),
             pl.BlockSpec(_row_shape(ROW_TILE), lambda t: (t, 0, 0, 0)),
             pl.BlockSpec((TOP_K, ROW_TILE), lambda t: (0, t)),
             pl.BlockSpec((TOP_K, ROW_TILE), lambda t: (0, t)))
    return shapes, specs


def _merge0(ret, proj, s5y, x_lat, x_ctx, mod, w_glu, w_out, lng, wr_t, rbias, *, nb, t_len, ctx_len):
    nt, d = nb * t_len, x_lat.shape[1]
    tpb = t_len // ROW_TILE
    ctx_tiles = ctx_len // ROW_TILE
    kern = functools.partial(_merge0_kernel, tiles_per_batch=tpb, ctx_tiles=ctx_tiles, nb=nb)
    shapes, specs = _tail_outs(nt, d)
    gcol = (2 * RET_QK_W + RET_V_W) // RET_V_W
    full = lambda a: pl.BlockSpec(a.shape, lambda t: (0,) * a.ndim)
    return pl.pallas_call(
        kern,
        grid=(nt // ROW_TILE,),
        in_specs=[
            pl.BlockSpec((ROW_TILE, RET_V_W), lambda t: (t, 0)),
            pl.BlockSpec((ROW_TILE, RET_V_W), lambda t: (t, gcol)),
            pl.BlockSpec((S5_HALVES, ROW_TILE // S5_CHUNK, S5_CHUNK * LANES), lambda t: (0, t, 0)),
            *_token_specs(d, tpb, ctx_tiles),
            pl.BlockSpec((1, SUBLANES, mod.shape[2]), lambda t: (0, 0, 0)),
            full(w_glu), full(w_out), full(lng), full(wr_t), full(rbias),
        ],
        out_specs=specs,
        out_shape=shapes,
        compiler_params=_cparams(("parallel",)),
        name="merge0",
    )(ret, proj, s5y, x_lat, x_ctx, mod, w_glu, w_out, lng, wr_t, rbias)


def _merge1(att, x, mod, layer, w_out, lng, wr_t, rbias, *, nb, l_len, t_len, ctx_len):
    n_lat, d = att.shape
    tpb = l_len // ROW_TILE
    tpb_t = t_len // ROW_TILE
    ctx_tiles = ctx_len // ROW_TILE
    kern = functools.partial(_merge1_kernel, tiles_per_batch=tpb)
    shapes, specs = _tail_outs(n_lat, d)
    full = lambda a: pl.BlockSpec(a.shape, lambda t: (0,) * a.ndim)
    xrow = lambda t: (lax.div(t, tpb) * tpb_t + ctx_tiles + lax.rem(t, tpb), 0)
    return pl.pallas_call(
        kern,
        grid=(n_lat // ROW_TILE,),
        in_specs=[
            pl.BlockSpec((ROW_TILE, d), lambda t: (t, 0)),
            pl.BlockSpec((ROW_TILE, d), xrow),
            pl.BlockSpec((1, SUBLANES, mod.shape[2]), lambda t: (layer, 0, 0)),
            full(w_out), full(lng), full(wr_t), full(rbias),
        ],
        out_specs=specs,
        out_shape=shapes,
        compiler_params=_cparams(("parallel",)),
        name="merge1",
    )(att, x, mod, w_out, lng, wr_t, rbias)


def _moe_plan(eidx):
    k, n = eidx.shape
    a = k * n
    e_flat = eidx.reshape(a)
    seg = TOP_K * ROW_TILE
    onehot = (e_flat[:, None] == jnp.arange(N_EXPERTS, dtype=jnp.int32)[None, :]).astype(F32)
    onehot = onehot.reshape(a // seg, seg, N_EXPERTS)
    tril = lambda m: jnp.tril(jnp.ones((m, m), F32))
    within = jnp.einsum('ij,tjk->tik', tril(seg), onehot)
    seg_total = within[:, -1, :]
    seg_end = jnp.sum(tril(a // seg)[:, :, None] * seg_total[None], axis=1)
    counts = seg_end[-1].astype(jnp.int32)
    csum = within + (seg_end - seg_total)[:, None, :]
    padded = (counts + MOE_ROWS - 1) // MOE_ROWS * MOE_ROWS
    pad_end = jnp.sum(jnp.tril(jnp.ones((N_EXPERTS, N_EXPERTS), jnp.int32)) * padded[None, :], axis=1)
    pad_start = pad_end - padded
    dest = jnp.sum(onehot * (csum - 1.0 + pad_start.astype(F32)[None, None, :]), axis=-1)
    dest = dest.reshape(a).astype(jnp.int32)
    n_blocks = -(-(a + N_EXPERTS * (MOE_ROWS - 1)) // MOE_ROWS)
    first_row = jnp.arange(n_blocks, dtype=jnp.int32) * MOE_ROWS
    block_expert = jnp.minimum(jnp.sum((pad_end[None, :] <= first_row[:, None]).astype(jnp.int32), axis=1),
                               N_EXPERTS - 1)
    return dest.reshape(k, n), block_expert, n_blocks


def _tile_rows_of(dest, tile):
    k, n = dest.shape
    return dest.reshape(k, n // tile, tile).transpose(1, 0, 2).reshape(n // tile, 1, k * tile)


def _dispatch_kernel(dest_ref, h_ref, xs_in_hbm, xs_hbm, sem):
    del xs_in_hbm
    row_tiles = h_ref.shape[0]
    rows = row_tiles * SUBLANES

    def start(i, c):
        for u in range(SUBLANES):
            for choice in range(TOP_K):
                dst = dest_ref[0, 0, choice * rows + i * SUBLANES + u]
                pltpu.make_async_copy(h_ref.at[i, :, u, :], _row(xs_hbm, dst), sem).start(priority=choice)
        return c

    lax.fori_loop(0, row_tiles, start, 0)
    for _ in range(TOP_K):
        pltpu.make_async_copy(h_ref, xs_hbm.at[pl.ds(0, row_tiles)], sem).wait()


def _dispatch(h, dest, n_rows):
    n = h.shape[0] * SUBLANES
    tile = DISPATCH_TILE if n % DISPATCH_TILE == 0 else ROW_TILE
    return pl.pallas_call(
        _dispatch_kernel,
        grid=(n // tile,),
        in_specs=[
            pl.BlockSpec((1, 1, TOP_K * tile), lambda t: (t, 0, 0), memory_space=pltpu.SMEM),
            pl.BlockSpec(_row_shape(tile), lambda t: (t, 0, 0, 0)),
            pl.BlockSpec(memory_space=pl.ANY),
        ],
        out_specs=pl.BlockSpec(memory_space=pl.ANY),
        out_shape=jax.ShapeDtypeStruct(_row_shape(n_rows), jnp.uint32),
        scratch_shapes=[pltpu.SemaphoreType.DMA],
        input_output_aliases={2: 0},
        compiler_params=_cparams(("arbitrary",)),
        name="moe_dispatch",
    )(_tile_rows_of(dest, tile), h, jnp.zeros(_row_shape(n_rows), jnp.uint32))


def _experts_kernel(be_ref, x_ref, wg_ref, wu_ref, wd_ref, o_ref):
    x = _unpack_bf16_pairs(_load_rows(x_ref)).astype(BF16)
    hg = _dot(x, wg_ref[0])
    hu = _dot(x, wu_ref[0])
    _store_rows(o_ref, _pack_bf16_pairs(_dot((_silu(hg) * hu).astype(BF16), wd_ref[0])))


def _experts(xs, block_expert, layer, wg, wu, wd):
    n_blocks = block_expert.shape[0]
    d, dff = wg.shape[2], wg.shape[3]
    rows_spec = pl.BlockSpec(_row_shape(MOE_ROWS), lambda i, be: (i, 0, 0, 0))
    return pl.pallas_call(
        _experts_kernel,
        grid_spec=pltpu.PrefetchScalarGridSpec(
            num_scalar_prefetch=1,
            grid=(n_blocks,),
            in_specs=[
                rows_spec,
                pl.BlockSpec((None, 1, d, dff), lambda i, be: (layer, be[i], 0, 0)),
                pl.BlockSpec((None, 1, d, dff), lambda i, be: (layer, be[i], 0, 0)),
                pl.BlockSpec((None, 1, dff, d), lambda i, be: (layer, be[i], 0, 0)),
            ],
            out_specs=rows_spec,
        ),
        out_shape=jax.ShapeDtypeStruct(xs.shape, jnp.uint32),
        compiler_params=_cparams(("parallel",)),
        name="moe_experts",
    )(block_expert, xs, wg, wu, wd)


def _combine_kernel(dcur_ref, dnxt_ref, x_ref, gt_ref, mod_ref, lng_ref, y_hbm, o_ref, ybuf, sem, *,
                    tiles_per_batch, ctx_tiles, nb):
    t = pl.program_id(0)
    last = pl.num_programs(0) - 1
    rows = x_ref.shape[0]
    n = TOP_K * rows
    d = x_ref.shape[1]
    slot = lax.rem(t, 2)

    row_tiles = rows // SUBLANES

    def start_all(idx_ref, s):
        def start(i, c):
            for choice in range(TOP_K):
                for u in range(SUBLANES):
                    j = choice * rows + i * SUBLANES + u
                    pltpu.make_async_copy(_row(y_hbm, idx_ref[0, 0, j]),
                                          ybuf.at[s, choice * row_tiles + i, :, u, :],
                                          sem.at[s]).start(priority=choice)
            return c

        lax.fori_loop(0, row_tiles, start, 0)

    @pl.when(t == 0)
    def _():
        start_all(dcur_ref, 0)

    @pl.when(t < last)
    def _():
        start_all(dnxt_ref, 1 - slot)

    pltpu.make_async_copy(ybuf.at[slot], ybuf.at[slot], sem.at[slot]).wait()
    r = _mod_row(t, tiles_per_batch, ctx_tiles, nb)
    gt = gt_ref[...]

    def choice_rows(choice):
        tiles = pl.ds(choice * row_tiles, row_tiles)
        return jnp.concatenate([ybuf[slot, tiles, j].reshape(rows, LANES) for j in range(ROW_SUB)], axis=1)

    y = (_unpack_bf16_pairs(choice_rows(0)) * gt[:, 0:1]
         + _unpack_bf16_pairs(choice_rows(1)) * gt[:, 1:2])
    g2 = mod_ref[0, pl.ds(r, 1), 5 * d:6 * d]
    z = ALPHA * x_ref[...] + g2 * y
    mu = jnp.mean(z, axis=-1, keepdims=True)
    zc = z - mu
    var = jnp.mean(zc * zc, axis=-1, keepdims=True)
    o_ref[...] = zc * lax.rsqrt(var + LN_EPS) * lng_ref[...]


def _combine(x1, dest, gates, mod, layer, lng, ys, *, tiles_per_batch, ctx_tiles, nb):
    n, d = x1.shape
    nt = n // ROW_TILE
    dest_t = _tile_rows_of(dest, ROW_TILE)
    kern = functools.partial(_combine_kernel, tiles_per_batch=tiles_per_batch, ctx_tiles=ctx_tiles, nb=nb)
    idx_spec = lambda f: pl.BlockSpec((1, 1, TOP_K * ROW_TILE), f, memory_space=pltpu.SMEM)
    return pl.pallas_call(
        kern,
        grid=(nt,),
        in_specs=[
            idx_spec(lambda t: (t, 0, 0)),
            idx_spec(lambda t: (jnp.minimum(t + 1, nt - 1), 0, 0)),
            pl.BlockSpec((ROW_TILE, d), lambda t: (t, 0)),
            pl.BlockSpec((ROW_TILE, TOP_K), lambda t: (t, 0)),
            pl.BlockSpec((1, SUBLANES, mod.shape[2]), lambda t: (layer, 0, 0)),
            pl.BlockSpec((1, d), lambda t: (0, 0)),
            pl.BlockSpec(memory_space=pl.ANY),
        ],
        out_specs=pl.BlockSpec((ROW_TILE, d), lambda t: (t, 0)),
        out_shape=jax.ShapeDtypeStruct((n, d), F32),
        scratch_shapes=[pltpu.VMEM((2, TOP_K * ROW_TILE // SUBLANES, ROW_SUB, SUBLANES, LANES), jnp.uint32),
                        pltpu.SemaphoreType.DMA((2,))],
        compiler_params=_cparams(("arbitrary",)),
        name="moe_combine%d" % layer,
    )(dest_t, dest_t, x1, gates.T, mod, lng, ys)


def _moe_layer(x1, h2, eidx, gates, mod, layer, lng, wg, wu, wd, *, tiles_per_batch, ctx_tiles, nb):
    dest, block_expert, n_blocks = _moe_plan(eidx)
    xs = _dispatch(h2, dest, n_blocks * MOE_ROWS)
    ys = _experts(xs, block_expert, layer, wg, wu, wd)
    return _combine(x1, dest, gates, mod, layer, lng, ys,
                    tiles_per_batch=tiles_per_batch, ctx_tiles=ctx_tiles, nb=nb)


def _half_norms(x):
    lane = lax.broadcasted_iota(jnp.int32, x.shape, 1)
    sq = x * x
    lo = jnp.sum(jnp.where(lane < DIFF_DH, sq, 0.0), axis=-1, keepdims=True)
    hi = jnp.sum(jnp.where(lane >= DIFF_DH, sq, 0.0), axis=-1, keepdims=True)
    return jnp.sqrt(lo), jnp.sqrt(hi)


def _attn_kernel(lam_ref, q_ref, k_ref, v_ref, g_ref, o_ref, vext, s_buf0, s_buf1,
                 p_buf0, p_buf1, corr_buf0, corr_buf1, m_buf, shift_buf, acc, *, out_scale, ctx_len):
    t_len = k_ref.shape[0]
    nk = t_len // ATT_TK
    dv = v_ref.shape[1]
    tq = acc.shape[1]
    nq = o_ref.shape[0] // tq
    n_tiles = nq * nk

    vext[:, 0:dv] = v_ref[...]
    vext[:, dv:2 * dv] = jnp.ones((t_len, dv), BF16)
    lam = lam_ref[0]
    gain = g_ref[...] * out_scale

    s_bufs, p_bufs, corr_bufs = (s_buf0, s_buf1), (p_buf0, p_buf1), (corr_buf0, corr_buf1)

    def key_rows(kj):
        return pl.ds(pl.multiple_of(kj * ATT_TK, ATT_TK), ATT_TK)

    def query_rows(qi):
        return pl.ds(pl.multiple_of(ctx_len + qi * tq, ROW_TILE), tq)

    def column_max(norms, carry):
        return tuple(jnp.maximum(c, jnp.max(n, axis=0, keepdims=True)) for n, c in zip(norms, carry))

    def key_norms(kj, carry):
        return column_max(_half_norms(k_ref[key_rows(kj), :].astype(F32)), carry)

    zero11 = jnp.zeros((1, 1), F32)
    kmax = lax.fori_loop(0, nk, key_norms, (zero11, zero11))

    def query_shifts(qi, carry):
        shifts = tuple(n * k for n, k in zip(_half_norms(q_ref[query_rows(qi), :].astype(F32)), kmax))
        shift_buf[qi, :, 0:1] = shifts[0]
        shift_buf[qi, :, 1:2] = shifts[1]
        return column_max(shifts, carry)

    worst = lax.fori_loop(0, nq, query_shifts, (zero11, zero11))
    bounded = jnp.max(jnp.maximum(worst[0], worst[1])) <= ATT_SAFE_SHIFT

    def advance(tile):
        qi, kj = tile
        wrap = kj + 1 == nk
        return jnp.where(wrap, qi + 1, qi), jnp.where(wrap, 0, kj + 1)

    def scores(tile, slot):
        qi, kj = tile
        q = q_ref[query_rows(qi), :]
        lane = lax.broadcasted_iota(jnp.int32, q.shape, 1)
        zero = jnp.zeros(q.shape, q.dtype)
        k = k_ref[key_rows(kj), :]
        s_bufs[slot][0] = _dot_nt(jnp.where(lane < DIFF_DH, q, zero), k)
        s_bufs[slot][1] = _dot_nt(jnp.where(lane >= DIFF_DH, q, zero), k)

    def numerators(tile, slot, online):
        qi, kj = tile
        for w in range(2):
            s = s_bufs[slot][w]
            if online:
                m_old = jnp.where(kj == 0, -jnp.inf, m_buf[w])
                m_new = jnp.maximum(m_old, jnp.max(s, axis=-1, keepdims=True))
                corr_bufs[slot][w] = jnp.exp2(m_old - m_new)
                m_buf[w] = m_new
            else:
                m_new = shift_buf[qi, :, w:w + 1]
            p_bufs[slot][w] = jnp.exp2(s - m_new).astype(BF16)

    def values(tile, slot, online):
        qi, kj = tile
        ve = vext[key_rows(kj), :]
        a = []
        for w in range(2):
            keep = corr_bufs[slot][w] if online else jnp.where(kj == 0, 0.0, 1.0)
            a.append(keep * acc[w] + _dot(p_bufs[slot][w], ve))
            acc[w] = a[w]
        o = a[0][:, 0:dv] / a[0][:, dv:2 * dv] - lam * (a[1][:, 0:dv] / a[1][:, dv:2 * dv])
        o = o * lax.rsqrt(jnp.mean(o * o, axis=-1, keepdims=True) + GN_EPS)
        o_ref[pl.ds(pl.multiple_of(qi * tq, tq), tq), :] = (o * gain).astype(BF16)

    def pipeline(online):
        def step(tiles, slot):
            a, b, c = tiles
            values(c, slot, online)
            scores(a, slot)
            numerators(b, 1 - slot, online)
            return advance(a), a, b

        acc[...] = jnp.zeros(acc.shape, F32)
        t0 = (jnp.int32(0), jnp.int32(0))
        t1 = advance(t0)
        scores(t0, 0)
        scores(t1, 1)
        numerators(t0, 0, online)

        def pair(_, tiles):
            return step(step(tiles, 0), 1)

        _, last, prev = lax.fori_loop(0, (n_tiles - 2) // 2, pair, (advance(t1), t1, t0))
        numerators(last, (n_tiles - 1) % 2, online)
        values(prev, n_tiles % 2, online)
        values(last, (n_tiles - 1) % 2, online)

    pl.when(bounded)(functools.partial(pipeline, False))
    pl.when(jnp.logical_not(bounded))(functools.partial(pipeline, True))


def _diff_attention(qkv, lam, subln_g, lambda_init, *, nb, l_len, t_len, ctx_len):
    d = D_MODEL
    tq = ATT_TQ
    dv = 2 * DIFF_DH
    nq = l_len // tq
    assert (nq * (t_len // ATT_TK)) % 2 == 0
    kern = functools.partial(_attn_kernel, out_scale=1.0 - lambda_init, ctx_len=ctx_len)
    return pl.pallas_call(
        kern,
        grid_spec=pltpu.PrefetchScalarGridSpec(
            num_scalar_prefetch=1,
            grid=(nb, DIFF_HEADS),
            in_specs=[
                pl.BlockSpec((t_len, LANES), lambda b, h, lam: (b, h)),
                pl.BlockSpec((t_len, LANES), lambda b, h, lam: (b, DIFF_HEADS + h)),
                pl.BlockSpec((t_len, LANES), lambda b, h, lam: (b, 2 * DIFF_HEADS + h)),
                pl.BlockSpec((1, LANES), lambda b, h, lam: (0, 0)),
            ],
            out_specs=pl.BlockSpec((l_len, LANES), lambda b, h, lam: (b, h)),
            scratch_shapes=[
                pltpu.VMEM((t_len, 2 * dv), BF16),
                pltpu.VMEM((2, tq, ATT_TK), F32), pltpu.VMEM((2, tq, ATT_TK), F32),
                pltpu.VMEM((2, tq, ATT_TK), BF16), pltpu.VMEM((2, tq, ATT_TK), BF16),
                pltpu.VMEM((2, tq, 1), F32), pltpu.VMEM((2, tq, 1), F32),
                pltpu.VMEM((2, tq, 1), F32), pltpu.VMEM((nq, tq, 2), F32),
                pltpu.VMEM((2, tq, 2 * dv), F32),
            ],
        ),
        out_shape=jax.ShapeDtypeStruct((nb * l_len, d), BF16),
        compiler_params=_cparams(("parallel", "parallel"), ATT_VMEM_LIMIT),
        name="diff_attention",
    )(lam, qkv, qkv, qkv, subln_g.reshape(1, LANES).astype(F32))


def _rope_angles(pos, dim):
    inv = (np.float32(ROPE_BASE) ** (-np.arange(0, dim, 2, dtype=np.float32) / np.float32(dim)))
    return pos.astype(np.float32)[:, None] * inv.astype(np.float32)[None, :]


def _ret_rope_tables(l_len, ctx_len):
    half = RET_DK // 2
    ang = _rope_angles(np.arange(l_len), RET_DK)
    ang = np.concatenate([np.zeros((ctx_len, half), np.float32), ang], axis=0)
    cos64 = np.concatenate([np.cos(ang), np.cos(ang)], axis=1)
    sin64 = np.concatenate([-np.sin(ang), np.sin(ang)], axis=1)
    kscale = np.float32(RET_DK ** -0.5)
    cos = np.concatenate([cos64, cos64 * kscale], axis=1)
    sin = np.concatenate([sin64, sin64 * kscale], axis=1)
    return jnp.asarray(cos[None], F32), jnp.asarray(sin[None], F32)


def _attn_rope_tables(l_len, ctx_len):
    quarter = DIFF_DH // 4
    pos = np.arange(l_len)
    pad = lambda a: np.concatenate([np.zeros((ctx_len, quarter), np.float32), a], axis=0)
    ang_r = pad(_rope_angles(pos // GRID_W, DIFF_DH // 2))
    ang_c = pad(_rope_angles(pos % GRID_W, DIFF_DH // 2))
    cos64 = np.concatenate([np.cos(ang_r)] * 2 + [np.cos(ang_c)] * 2, axis=1)
    sin64 = np.concatenate([-np.sin(ang_r), np.sin(ang_r), -np.sin(ang_c), np.sin(ang_c)], axis=1)
    cos = np.concatenate([cos64, cos64], axis=1)
    sin = np.concatenate([sin64, sin64], axis=1)
    qscale = np.float32(DIFF_DH ** -0.5 * math.log2(math.e))
    return (jnp.asarray(np.stack([cos * qscale, cos]), F32), jnp.asarray(np.stack([sin * qscale, sin]), F32))


def kernel(x, c, ctx, c_ctx, ada_w, ada_b, ln_g, w_in_ab, ret_decay_logit, s5_lam_re, s5_lam_im,
           s5_log_dt, s5_b_re, s5_b_im, s5_c_re, s5_c_im, s5_d, s5_w_glu, w_out_ab, w_in_c,
           diff_lambda, diff_subln_g, w_out_c, router_w, router_bias, exp_w_gate, exp_w_up,
           exp_w_down):
    nb, l_len, d = x.shape
    ctx_len = ctx.shape[1]
    t_len = ctx_len + l_len
    tpb = t_len // ROW_TILE
    ctx_tiles = ctx_len // ROW_TILE
    assert d == D_MODEL and nb < SUBLANES
    assert l_len % ATT_TQ == 0 and ctx_len % ROW_TILE == 0 and t_len % ATT_TK == 0

    x_lat, x_ctx = x.reshape(nb * l_len, d), ctx.reshape(nb * ctx_len, d)
    c_all = jnp.concatenate([c, c_ctx[None].astype(c.dtype)], axis=0)
    c_pad = jnp.zeros((SUBLANES, d), F32).at[:nb + 1].set(c_all)
    mod = _adaln(c_pad, ada_w, ada_b)

    wr_t = router_w.T
    rbias = router_bias.reshape(N_EXPERTS, 1).astype(F32)

    w0 = w_in_ab[0]
    q_w, k_w, v_w, g_w, u_w = jnp.split(w0, (RET_QK_W, 2 * RET_QK_W, 2 * RET_QK_W + RET_V_W,
                                             2 * RET_QK_W + 2 * RET_V_W), axis=1)
    qk_w = jnp.concatenate([q_w.reshape(d, RET_HEADS, RET_DK), k_w.reshape(d, RET_HEADS, RET_DK)],
                           axis=2).reshape(d, 2 * RET_QK_W)
    w0p = jnp.concatenate([qk_w, v_w, g_w, u_w], axis=1).astype(BF16)
    cos0, sin0 = _ret_rope_tables(l_len, ctx_len)
    rope_tab0 = [0] * RET_HEADS + [None] * ((w0p.shape[1] - S5_CH - 2 * RET_QK_W) // LANES)
    proj0, u5 = _inproj((x_lat, x_ctx), mod, 0, w0p, cos0, sin0, rope_tab0, RET_DK // 2,
                        nb=nb, t_len=t_len, ctx_len=ctx_len, s5_cols=S5_CH)

    log_gammas = jax.nn.log_sigmoid(ret_decay_logit[0].astype(F32))
    ret = _retention(proj0, log_gammas, nb=nb, t_len=t_len, ctx_len=ctx_len)

    s5_ops = _s5_operators(s5_lam_re[0], s5_lam_im[0], s5_log_dt[0], s5_b_re[0], s5_b_im[0],
                           s5_c_re[0], s5_c_im[0], s5_d[0])
    s5y = _s5(u5, *s5_ops, nb=nb, nctx=ctx_len // S5_CHUNK)

    x1, h2, eidx, gates = _merge0(ret, proj0, s5y, x_lat, x_ctx, mod, s5_w_glu[0].astype(BF16),
                                  w_out_ab[0].astype(BF16), ln_g[0, 0].reshape(1, d), wr_t, rbias,
                                  nb=nb, t_len=t_len, ctx_len=ctx_len)
    experts = (exp_w_gate.astype(BF16), exp_w_up.astype(BF16), exp_w_down.astype(BF16))
    x2 = _moe_layer(x1, h2, eidx, gates, mod, 0, ln_g[0, 1].reshape(1, d), *experts,
                    tiles_per_batch=tpb, ctx_tiles=ctx_tiles, nb=nb)

    cos1, sin1 = _attn_rope_tables(l_len, ctx_len)
    n_heads_cols = D_MODEL // LANES
    rope_tab1 = [0] * n_heads_cols + [1] * n_heads_cols + [None] * n_heads_cols
    qkv, = _inproj((x2,), mod, 1, w_in_c[0].astype(BF16), cos1, sin1, rope_tab1, DIFF_DH // 4,
                   nb=nb, t_len=t_len, ctx_len=ctx_len)
    lf = diff_lambda[0].astype(F32)
    lambda_init = 0.8 - 0.6 * math.exp(-0.3 * 1)
    lam = (jnp.exp(jnp.sum(lf[0] * lf[1])) - jnp.exp(jnp.sum(lf[2] * lf[3])) + lambda_init).reshape(1)
    att = _diff_attention(qkv, lam, diff_subln_g[0], lambda_init,
                          nb=nb, l_len=l_len, t_len=t_len, ctx_len=ctx_len)
    x3, h3, eidx1, gates1 = _merge1(att, x2, mod, 1, w_out_c[0].astype(BF16), ln_g[1, 0].reshape(1, d),
                                    wr_t, rbias, nb=nb, l_len=l_len, t_len=t_len, ctx_len=ctx_len)
    out = _moe_layer(x3, h3, eidx1, gates1, mod, 1, ln_g[1, 1].reshape(1, d), *experts,
                     tiles_per_batch=l_len // ROW_TILE, ctx_tiles=0, nb=nb)
    return out.reshape(nb, l_len, d)
```

```python
import functools
import math

import jax
import jax.numpy as jnp
import numpy as np
from jax import lax
from jax.experimental import pallas as pl
from jax.experimental.pallas import tpu as pltpu

F32 = jnp.float32
BF16 = jnp.bfloat16

D_MODEL = 1024
DEPTH = 2
GRID_W = 64
ALPHA = (2.0 * DEPTH) ** 0.25
LN_EPS = 1e-5
GN_EPS = 1e-6
ROPE_BASE = 10000.0
RET_DK = 64
RET_DV = 128
RET_HEADS = 6
RET_QK_W = RET_HEADS * RET_DK
RET_V_W = RET_HEADS * RET_DV
S5_CH = 256
S5_P = 16
S5_G = 16
S5_N = 64
DIFF_HEADS = 8
DIFF_DH = 64
N_EXPERTS = 16
EXPERTS_PER_GROUP = 4
TOP_K = 2

LANES = 128
SUBLANES = 8
MXU_DIM = 256
ROW_TILE = 256
RET_CHUNK = 256
RET_STATE_UNROLL = 3
RET_OUT_UNROLL = 11
S5_CHUNK = 8
S5_HALVES = 2
MOE_ROWS = 256
DISPATCH_TILE = 1024
ROW_WORDS = D_MODEL // 2
ROW_SUB = ROW_WORDS // LANES
ATT_TQ = 1024
ATT_TK = 768
ATT_SAFE_SHIFT = 48.0
VMEM_LIMIT = 48 * 1024 * 1024
ATT_VMEM_LIMIT = 56 * 1024 * 1024


def _cparams(sem, vmem_limit=VMEM_LIMIT):
    return pltpu.CompilerParams(dimension_semantics=sem, vmem_limit_bytes=vmem_limit)


def _dot(a, b):
    return jnp.dot(a, b, preferred_element_type=F32)


def _dot_nt(a, b):
    return lax.dot_general(a, b, (((1,), (1,)), ((), ())), preferred_element_type=F32)


def _dot_tn(a, b):
    return lax.dot_general(a, b, (((0,), (0,)), ((), ())), preferred_element_type=F32)


def _split_bf16(x):
    hi = x.astype(BF16)
    lo = (x - hi.astype(F32)).astype(BF16)
    return hi, lo


def _dot3(a, b):
    ah, al = _split_bf16(a)
    bh, bl = _split_bf16(b)
    return _dot(ah, bh) + _dot(ah, bl) + _dot(al, bh)


def _dot3_nt(a, b):
    ah, al = _split_bf16(a)
    bh, bl = _split_bf16(b)
    return _dot_nt(ah, bh) + _dot_nt(ah, bl) + _dot_nt(al, bh)


def _sigmoid(x):
    return 1.0 / (1.0 + jnp.exp(-x))


def _silu(x):
    return x * _sigmoid(x)


def _pack_bf16_pairs(v):
    half = v.shape[1] // 2
    bits = lax.bitcast_convert_type(v.astype(BF16).astype(F32), jnp.uint32)
    return (bits[:, :half] >> 16) | (bits[:, half:] & jnp.uint32(0xFFFF0000))


def _unpack_bf16_pairs(p):
    lo = lax.bitcast_convert_type(p << 16, F32)
    hi = lax.bitcast_convert_type(p & jnp.uint32(0xFFFF0000), F32)
    return jnp.concatenate([lo, hi], axis=1)


def _row_shape(n_rows):
    return (n_rows // SUBLANES, ROW_SUB, SUBLANES, LANES)


def _row(ref, r):
    return ref.at[lax.shift_right_logical(r, 3), :, r & (SUBLANES - 1), :]


def _store_rows(ref, packed):
    rows = packed.shape[0]
    for j in range(ROW_SUB):
        ref[:, j] = packed[:, j * LANES:(j + 1) * LANES].reshape(rows // SUBLANES, SUBLANES, LANES)


def _load_rows(ref):
    rows = ref.shape[0] * SUBLANES
    return jnp.concatenate([ref[:, j].reshape(rows, LANES) for j in range(ROW_SUB)], axis=1)


def _gelu_tanh(x):
    c = math.sqrt(2.0 / math.pi)
    return 0.5 * x * (1.0 + jnp.tanh(c * (x + 0.044715 * (x * x * x))))


def _adaln_kernel(c_ref, w_ref, b_ref, o_ref):
    c = c_ref[...]
    o_ref[0] = _dot3(_silu(c), w_ref[0]) + b_ref[0]


def _adaln(c_pad, ada_w, ada_b):
    depth, d, n = ada_w.shape
    tn = 1536
    return pl.pallas_call(
        _adaln_kernel,
        grid=(depth, n // tn),
        in_specs=[
            pl.BlockSpec((SUBLANES, d), lambda i, j: (0, 0)),
            pl.BlockSpec((1, d, tn), lambda i, j: (i, 0, j)),
            pl.BlockSpec((1, 1, tn), lambda i, j: (i, 0, j)),
        ],
        out_specs=pl.BlockSpec((1, SUBLANES, tn), lambda i, j: (i, 0, j)),
        out_shape=jax.ShapeDtypeStruct((depth, SUBLANES, n), F32),
        compiler_params=_cparams(("parallel", "parallel")),
        name="adaln",
    )(c_pad, ada_w, ada_b.reshape(depth, 1, n))


def _mod_row(t, tiles_per_batch, ctx_tiles, nb):
    b = lax.div(t, tiles_per_batch)
    w = lax.rem(t, tiles_per_batch)
    return jnp.where(w < ctx_tiles, nb, b)


def _rope_block(a, cos, sin, half):
    lane = lax.broadcasted_iota(jnp.int32, a.shape, 1)
    first = lax.rem(lane, 2 * half) < half
    rot = jnp.where(first, pltpu.roll(a, LANES - half, 1), pltpu.roll(a, half, 1))
    return a * cos + rot * sin


def _chunk_perm(rows, transpose):
    per = rows // S5_CHUNK
    r = lax.broadcasted_iota(jnp.int32, (rows, rows), 0)
    c = lax.broadcasted_iota(jnp.int32, (rows, rows), 1)
    if transpose:
        r, c = c, r
    return (c == S5_CHUNK * lax.rem(r, per) + lax.div(r, per)).astype(BF16)


def _token_specs(d, tiles_per_batch, ctx_tiles):
    lat_tiles = tiles_per_batch - ctx_tiles
    b = lambda t: lax.div(t, tiles_per_batch)
    w = lambda t: lax.rem(t, tiles_per_batch)
    return (pl.BlockSpec((ROW_TILE, d), lambda t: (b(t) * lat_tiles + jnp.maximum(w(t) - ctx_tiles, 0), 0)),
            pl.BlockSpec((ROW_TILE, d), lambda t: (b(t) * ctx_tiles + jnp.minimum(w(t), ctx_tiles - 1), 0)))


def _token_tile(x_refs, tiles_per_batch, ctx_tiles):
    if len(x_refs) == 1:
        return x_refs[0][...]
    is_ctx = lax.rem(pl.program_id(0), tiles_per_batch) < ctx_tiles
    return jnp.where(is_ctx, x_refs[1][...], x_refs[0][...])


def _inproj_kernel(*refs, n_x, tiles_per_batch, ctx_tiles, nb, rope_tab, rope_half):
    x_refs, (mod_ref, w_ref, cos_ref, sin_ref, o_ref), rest = refs[:n_x], refs[n_x:n_x + 5], refs[n_x + 5:]
    d = w_ref.shape[0]
    n = o_ref.shape[1]
    r = _mod_row(pl.program_id(0), tiles_per_batch, ctx_tiles, nb)
    sh = mod_ref[0, pl.ds(r, 1), 0:d]
    sc = mod_ref[0, pl.ds(r, 1), d:2 * d]
    xm = (_token_tile(x_refs, tiles_per_batch, ctx_tiles) * (1.0 + sc) + sh).astype(BF16)
    for j in range(n // MXU_DIM):
        acc = _dot(xm, w_ref[:, j * MXU_DIM:(j + 1) * MXU_DIM])
        parts = []
        for s in range(MXU_DIM // LANES):
            blk = acc[:, s * LANES:(s + 1) * LANES]
            tab = rope_tab[j * (MXU_DIM // LANES) + s]
            if tab is not None:
                blk = _rope_block(blk, cos_ref[tab], sin_ref[tab], rope_half)
            parts.append(blk)
        o_ref[:, j * MXU_DIM:(j + 1) * MXU_DIM] = jnp.concatenate(parts, axis=1).astype(BF16)
    if rest:
        u_ref, = rest
        rows = o_ref.shape[0]
        per = rows // S5_CHUNK
        u = _dot(xm, w_ref[:, n:n + S5_CH]).astype(BF16)
        up = _dot(_chunk_perm(rows, False), u).astype(BF16)
        for s in range(S5_CHUNK):
            for hf in range(S5_CH // LANES):
                u_ref[hf, :, s * LANES:(s + 1) * LANES] = up[s * per:(s + 1) * per,
                                                             hf * LANES:(hf + 1) * LANES]


def _inproj(xs, mod, layer, w, cos, sin, rope_tab, rope_half, *, nb, t_len, ctx_len, s5_cols=0):
    nt, d = nb * t_len, w.shape[0]
    n = w.shape[1] - s5_cols
    tpb = t_len // ROW_TILE
    ctx_tiles = ctx_len // ROW_TILE
    kern = functools.partial(_inproj_kernel, n_x=len(xs), tiles_per_batch=tpb, ctx_tiles=ctx_tiles,
                             nb=nb, rope_tab=tuple(rope_tab), rope_half=rope_half)
    x_specs = ([pl.BlockSpec((ROW_TILE, d), lambda t: (t, 0))] if len(xs) == 1
               else list(_token_specs(d, tpb, ctx_tiles)))
    ntab = cos.shape[0]
    out_specs = [pl.BlockSpec((ROW_TILE, n), lambda t: (t, 0))]
    out_shape = [jax.ShapeDtypeStruct((nt, n), BF16)]
    if s5_cols:
        halves, per = s5_cols // LANES, ROW_TILE // S5_CHUNK
        out_specs.append(pl.BlockSpec((halves, per, S5_CHUNK * LANES), lambda t: (0, t, 0)))
        out_shape.append(jax.ShapeDtypeStruct((halves, nt // S5_CHUNK, S5_CHUNK * LANES), BF16))
    return pl.pallas_call(
        kern,
        grid=(nt // ROW_TILE,),
        in_specs=x_specs + [
            pl.BlockSpec((1, SUBLANES, mod.shape[2]), lambda t: (layer, 0, 0)),
            pl.BlockSpec(w.shape, lambda t: (0, 0)),
            pl.BlockSpec((ntab, ROW_TILE, LANES), lambda t: (0, lax.rem(t, tpb), 0)),
            pl.BlockSpec((ntab, ROW_TILE, LANES), lambda t: (0, lax.rem(t, tpb), 0)),
        ],
        out_specs=out_specs,
        out_shape=out_shape,
        compiler_params=_cparams(("parallel",)),
        name="inproj%d" % layer,
    )(*xs, mod, w, cos, sin)


def _retention_kernel(lg_ref, qk_ref, v_ref, o_ref, sf_ref, sb_ref, *, nctx):
    c_len = RET_CHUNK
    t_len = qk_ref.shape[0]
    nc = t_len // c_len
    h = pl.program_id(1)
    lgf = lg_ref[0, h]
    lgb = lg_ref[1, h]
    ii = lax.broadcasted_iota(jnp.int32, (c_len, 1), 0).astype(F32)
    jj = lax.broadcasted_iota(jnp.int32, (1, c_len), 1).astype(F32)
    diff = ii - jj
    decay = jnp.where(diff >= 0.0, jnp.exp(lgf * jnp.maximum(diff, 0.0)),
                      jnp.exp(lgb * jnp.maximum(-diff, 0.0)))
    kdf = jnp.exp(lgf * (c_len - 1.0 - ii))
    kdb = jnp.exp(lgb * ii)
    qdf = jnp.exp(lgf * (ii + 1.0))
    qdb = jnp.exp(lgb * (c_len - ii))
    zrow = jnp.zeros((1, RET_DV), F32)
    gf_chunk = jnp.exp(zrow + lgf * c_len)
    gb_chunk = jnp.exp(zrow + lgb * c_len)

    def load(c):
        rows = pl.ds(pl.multiple_of(c * c_len, c_len), c_len)
        qk = qk_ref[rows, :].astype(F32)
        return qk[:, :RET_DK], qk[:, RET_DK:], v_ref[rows, :]

    def states(j, carry):
        sf, sb = carry
        cb = jnp.where(j < nctx, nctx - 1 - j, nc - 1 - (j - nctx))
        sf_ref[j] = sf
        sb_ref[cb] = sb
        _, kf, vf = load(j)
        _, kb, vb = load(cb)
        return (gf_chunk * sf + _dot_tn((kf * kdf).astype(BF16), vf),
                gb_chunk * sb + _dot_tn((kb * kdb).astype(BF16), vb))

    zero_state = jnp.zeros((RET_DK, RET_DV), F32)
    lax.fori_loop(0, nc, states, (zero_state, zero_state), unroll=RET_STATE_UNROLL)

    def out_chunk(c, carry):
        q, k, v = load(c)
        scores = _dot_nt(q.astype(BF16), k.astype(BF16)) * decay
        o = _dot(scores.astype(BF16), v)
        o = o + _dot((q * qdf).astype(BF16), sf_ref[c].astype(BF16))
        o = o + _dot((q * qdb).astype(BF16), sb_ref[c].astype(BF16))
        mu = jnp.mean(o, axis=-1, keepdims=True)
        oc = o - mu
        var = jnp.mean(oc * oc, axis=-1, keepdims=True)
        rows = pl.ds(pl.multiple_of(c * c_len, c_len), c_len)
        o_ref[rows, :] = (oc * lax.rsqrt(var + GN_EPS)).astype(BF16)
        return carry

    lax.fori_loop(0, nc, out_chunk, 0, unroll=RET_OUT_UNROLL)


def _retention(proj, log_gammas, *, nb, t_len, ctx_len):
    nt = proj.shape[0]
    nc = t_len // RET_CHUNK
    kern = functools.partial(_retention_kernel, nctx=ctx_len // RET_CHUNK)
    vcol0 = RET_HEADS
    return pl.pallas_call(
        kern,
        grid_spec=pltpu.PrefetchScalarGridSpec(
            num_scalar_prefetch=1,
            grid=(nb, RET_HEADS),
            in_specs=[
                pl.BlockSpec((t_len, LANES), lambda b, h, lg: (b, h)),
                pl.BlockSpec((t_len, LANES), lambda b, h, lg: (b, vcol0 + h)),
            ],
            out_specs=pl.BlockSpec((t_len, LANES), lambda b, h, lg: (b, h)),
            scratch_shapes=[pltpu.VMEM((nc, RET_DK, RET_DV), F32),
                            pltpu.VMEM((nc, RET_DK, RET_DV), F32)],
        ),
        out_shape=jax.ShapeDtypeStruct((nt, RET_V_W), BF16),
        compiler_params=_cparams(("parallel", "parallel")),
        name="retention",
    )(log_gammas, proj, proj)


def _s5_operators(lam_re, lam_im, log_dt, b_re, b_im, c_re, c_im, d_skip):
    tc = S5_CHUNK
    hp = lax.Precision.HIGHEST
    ks = jnp.arange(tc + 1, dtype=F32)
    pw, bbar, cm = [], [], []
    for direction in range(2):
        dt = jnp.exp(log_dt[direction].astype(F32))[:, None]
        lam = lax.complex(lam_re[direction].astype(F32), lam_im[direction].astype(F32))
        z = lam * dt
        p = jnp.exp(z[None] * ks[:, None, None])
        lam_bar = p[1]
        bb = ((lam_bar - 1.0) / lam)[..., None] * lax.complex(
            b_re[direction].astype(F32), b_im[direction].astype(F32))
        pw.append(p)
        bbar.append(bb)
        cm.append(lax.complex(c_re[direction].astype(F32), c_im[direction].astype(F32)))

    def lag_kernel(p, bb, c):
        return jnp.einsum('gpn,kgn,gnq->kgpq', c, p[:tc], bb, precision=hp).real

    kf = lag_kernel(pw[0], bbar[0], cm[0])
    kb = lag_kernel(pw[1], bbar[1], cm[1])
    k0 = kf[0] + kb[0] + jnp.eye(S5_P, dtype=F32)[None] * d_skip.astype(F32)[:, :, None]
    kcat = jnp.concatenate([kb[1:][::-1], k0[None], kf[1:]], axis=0)
    s_idx = jnp.arange(tc)[:, None]
    t_idx = jnp.arange(tc)[None, :]
    m5 = kcat[t_idx - s_idx + tc - 1]
    hg = S5_G // S5_HALVES
    eye = jnp.eye(hg, dtype=F32)
    split = lambda z, axis: z.reshape(z.shape[:axis] + (S5_HALVES, hg) + z.shape[axis + 1:])
    wide = tc * hg * S5_P
    intra = jnp.einsum('sthgpq,gk->hsgqtkp', split(m5, 2), eye).reshape(S5_HALVES, wide, wide)

    ef = pw[0][:tc][::-1][:, :, :, None] * bbar[0][None]
    eb = pw[1][:tc][:, :, :, None] * bbar[1][None]
    parts_in = jnp.stack([ef.real, ef.imag, eb.real, eb.imag], axis=0)
    w_in = jnp.einsum('cshgnq,gk->hsgqckn', split(parts_in, 2), eye).reshape(
        S5_HALVES, wide, 4 * hg * S5_N)
    of = cm[0][None] * pw[0][1:][:, :, None, :]
    ob = cm[1][None] * pw[1][1:][::-1][:, :, None, :]
    parts_out = jnp.stack([of.real, -of.imag, ob.real, -ob.imag], axis=0)
    w_out = jnp.einsum('cthgpn,gk->hckntgp', split(parts_out, 2), eye).reshape(
        S5_HALVES, 4 * hg * S5_N, wide)
    a = jnp.stack([pw[0][tc].real, pw[0][tc].imag, pw[1][tc].real, pw[1][tc].imag], axis=0)
    a = a.reshape(4, S5_HALVES, hg * S5_N).transpose(1, 0, 2)
    return intra.astype(BF16), w_in.astype(BF16), w_out.astype(BF16), a


def _s5_kernel(x_ref, wi_ref, win_ref, wout_ref, a_ref, y_ref, st_ref, *, nctx):
    x = x_ref[0]
    nc = x.shape[0]
    w = a_ref.shape[2]
    st_ref[...] = _dot(x, win_ref[0])
    afr, afi, abr, abi = (a_ref[0, i:i + 1, :] for i in range(4))

    def step(j, carry):
        fr, fi, br, bi = carry
        rf = pl.ds(j, 1)
        rb = pl.ds(jnp.where(j < nctx, nctx - 1 - j, nc - 1 - (j - nctx)), 1)
        efr, efi = st_ref[rf, 0:w], st_ref[rf, w:2 * w]
        ebr, ebi = st_ref[rb, 2 * w:3 * w], st_ref[rb, 3 * w:4 * w]
        st_ref[rf, 0:w] = fr
        st_ref[rf, w:2 * w] = fi
        st_ref[rb, 2 * w:3 * w] = br
        st_ref[rb, 3 * w:4 * w] = bi
        return (afr * fr - afi * fi + efr, afr * fi + afi * fr + efi,
                abr * br - abi * bi + ebr, abr * bi + abi * br + ebi)

    z = jnp.zeros((1, w), F32)
    lax.fori_loop(0, nc, step, (z, z, z, z), unroll=4)
    y_ref[0] = (_dot(x, wi_ref[0]) + _dot(st_ref[...].astype(BF16), wout_ref[0])).astype(BF16)


def _s5(xc, intra, w_in, w_out, a, *, nb, nctx):
    halves, rows, wide = xc.shape
    nc = rows // nb
    kern = functools.partial(_s5_kernel, nctx=nctx)
    per_half = lambda arr: pl.BlockSpec((1,) + arr.shape[1:], lambda hf, b: (hf, 0, 0))
    return pl.pallas_call(
        kern,
        grid=(halves, nb),
        in_specs=[
            pl.BlockSpec((1, nc, wide), lambda hf, b: (hf, b, 0)),
            per_half(intra), per_half(w_in), per_half(w_out), per_half(a),
        ],
        out_specs=pl.BlockSpec((1, nc, wide), lambda hf, b: (hf, b, 0)),
        out_shape=jax.ShapeDtypeStruct(xc.shape, BF16),
        scratch_shapes=[pltpu.VMEM((nc, w_in.shape[2]), F32)],
        compiler_params=_cparams(("parallel", "parallel")),
        name="s5",
    )(xc, intra, w_in, w_out, a)


def _route(logits_t, bias):
    scores = _sigmoid(logits_t)
    biased = scores + bias
    s_rows = [scores[e:e + 1, :] for e in range(N_EXPERTS)]
    b_rows = [biased[e:e + 1, :] for e in range(N_EXPERTS)]
    n_groups = N_EXPERTS // EXPERTS_PER_GROUP
    best = None
    sel = None
    for g in range(n_groups):
        a, b, c, d = b_rows[4 * g:4 * g + 4]
        hi1, lo1 = jnp.maximum(a, b), jnp.minimum(a, b)
        hi2, lo2 = jnp.maximum(c, d), jnp.minimum(c, d)
        top1 = jnp.maximum(hi1, hi2)
        top2 = jnp.maximum(jnp.minimum(hi1, hi2), jnp.maximum(lo1, lo2))
        gs = top1 + top2
        if g == 0:
            best, sel = gs, jnp.zeros(gs.shape, jnp.int32)
        else:
            better = gs > best
            sel = jnp.where(better, g, sel)
            best = jnp.where(better, gs, best)
    neg = jnp.full(best.shape, -jnp.inf, F32)
    masked = [jnp.where(sel == (e // EXPERTS_PER_GROUP), b_rows[e], neg) for e in range(N_EXPERTS)]
    v1, i1, g1 = masked[0], jnp.zeros(best.shape, jnp.int32), s_rows[0]
    for e in range(1, N_EXPERTS):
        better = masked[e] > v1
        v1 = jnp.where(better, masked[e], v1)
        i1 = jnp.where(better, e, i1)
        g1 = jnp.where(better, s_rows[e], g1)
    v2, i2, g2 = neg, jnp.zeros(best.shape, jnp.int32), jnp.zeros(best.shape, F32)
    for e in range(N_EXPERTS):
        cand = jnp.where(i1 == e, neg, masked[e])
        better = cand > v2
        v2 = jnp.where(better, cand, v2)
        i2 = jnp.where(better, e, i2)
        g2 = jnp.where(better, s_rows[e], g2)
    tot = g1 + g2
    return jnp.concatenate([i1, i2], axis=0), jnp.concatenate([g1 / tot, g2 / tot], axis=0)


def _tail(x, o, mod_ref, r, lng, wr, rb, x1_ref, h2_ref, ei_ref, gt_ref):
    d = x.shape[1]
    g1 = mod_ref[0, pl.ds(r, 1), 2 * d:3 * d]
    sh2 = mod_ref[0, pl.ds(r, 1), 3 * d:4 * d]
    sc2 = mod_ref[0, pl.ds(r, 1), 4 * d:5 * d]
    y = ALPHA * x + g1 * o
    mu = jnp.mean(y, axis=-1, keepdims=True)
    yc = y - mu
    var = jnp.mean(yc * yc, axis=-1, keepdims=True)
    x1 = yc * lax.rsqrt(var + LN_EPS) * lng
    h2 = x1 * (1.0 + sc2) + sh2
    x1_ref[...] = x1
    _store_rows(h2_ref, _pack_bf16_pairs(h2))
    ei, gt = _route(_dot3_nt(wr, h2), rb)
    ei_ref[...] = ei
    gt_ref[...] = gt


def _merge0_kernel(r_ref, g_ref, s_ref, x_ref, xc_ref, mod_ref, wglu_ref, wout_ref, lng_ref, wr_ref,
                   rb_ref, x1_ref, h2_ref, ei_ref, gt_ref, *, tiles_per_batch, ctx_tiles, nb):
    r = _mod_row(pl.program_id(0), tiles_per_batch, ctx_tiles, nb)
    ret = r_ref[...].astype(F32) * _silu(g_ref[...].astype(F32))
    rows = x_ref.shape[0]
    x = _token_tile((x_ref, xc_ref), tiles_per_batch, ctx_tiles)
    sp = jnp.concatenate(
        [jnp.concatenate([s_ref[hf, :, s * LANES:(s + 1) * LANES] for hf in range(S5_HALVES)], axis=1)
         for s in range(S5_CHUNK)], axis=0)
    s5 = _dot(_chunk_perm(rows, True), sp)
    z = _dot(_gelu_tanh(s5).astype(BF16), wglu_ref[...])
    zz = z[:, :S5_CH] * _sigmoid(z[:, S5_CH:])
    o = _dot(ret.astype(BF16), wout_ref[0:RET_V_W, :]) + _dot(zz.astype(BF16), wout_ref[RET_V_W:, :])
    _tail(x, o, mod_ref, r, lng_ref[...], wr_ref[...], rb_ref[...],
          x1_ref, h2_ref, ei_ref, gt_ref)


def _merge1_kernel(a_ref, x_ref, mod_ref, wout_ref, lng_ref, wr_ref, rb_ref,
                   x1_ref, h2_ref, ei_ref, gt_ref, *, tiles_per_batch):
    r = lax.div(pl.program_id(0), tiles_per_batch)
    o = _dot(a_ref[...], wout_ref[...])
    _tail(x_ref[...], o, mod_ref, r, lng_ref[...], wr_ref[...], rb_ref[...],
          x1_ref, h2_ref, ei_ref, gt_ref)


def _tail_outs(n_rows, d):
    shapes = (jax.ShapeDtypeStruct((n_rows, d), F32),
              jax.ShapeDtypeStruct(_row_shape(n_rows), jnp.uint32),
              jax.ShapeDtypeStruct((TOP_K, n_rows), jnp.int32),
              jax.ShapeDtypeStruct((TOP_K, n_rows), F32))
    specs = (pl.BlockSpec((ROW_TILE, d), lambda t: (t, 0)),
             pl.BlockSpec(_row_shape(ROW_TILE), lambda t: (t, 0, 0, 0)),
             pl.BlockSpec((TOP_K, ROW_TILE), lambda t: (0, t)),
             pl.BlockSpec((TOP_K, ROW_TILE), lambda t: (0, t)))
    return shapes, specs


def _merge0(ret, proj, s5y, x_lat, x_ctx, mod, w_glu, w_out, lng, wr_t, rbias, *, nb, t_len, ctx_len):
    nt, d = nb * t_len, x_lat.shape[1]
    tpb = t_len // ROW_TILE
    ctx_tiles = ctx_len // ROW_TILE
    kern = functools.partial(_merge0_kernel, tiles_per_batch=tpb, ctx_tiles=ctx_tiles, nb=nb)
    shapes, specs = _tail_outs(nt, d)
    gcol = (2 * RET_QK_W + RET_V_W) // RET_V_W
    full = lambda a: pl.BlockSpec(a.shape, lambda t: (0,) * a.ndim)
    return pl.pallas_call(
        kern,
        grid=(nt // ROW_TILE,),
        in_specs=[
            pl.BlockSpec((ROW_TILE, RET_V_W), lambda t: (t, 0)),
            pl.BlockSpec((ROW_TILE, RET_V_W), lambda t: (t, gcol)),
            pl.BlockSpec((S5_HALVES, ROW_TILE // S5_CHUNK, S5_CHUNK * LANES), lambda t: (0, t, 0)),
            *_token_specs(d, tpb, ctx_tiles),
            pl.BlockSpec((1, SUBLANES, mod.shape[2]), lambda t: (0, 0, 0)),
            full(w_glu), full(w_out), full(lng), full(wr_t), full(rbias),
        ],
        out_specs=specs,
        out_shape=shapes,
        compiler_params=_cparams(("parallel",)),
        name="merge0",
    )(ret, proj, s5y, x_lat, x_ctx, mod, w_glu, w_out, lng, wr_t, rbias)


def _merge1(att, x, mod, layer, w_out, lng, wr_t, rbias, *, nb, l_len, t_len, ctx_len):
    n_lat, d = att.shape
    tpb = l_len // ROW_TILE
    tpb_t = t_len // ROW_TILE
    ctx_tiles = ctx_len // ROW_TILE
    kern = functools.partial(_merge1_kernel, tiles_per_batch=tpb)
    shapes, specs = _tail_outs(n_lat, d)
    full = lambda a: pl.BlockSpec(a.shape, lambda t: (0,) * a.ndim)
    xrow = lambda t: (lax.div(t, tpb) * tpb_t + ctx_tiles + lax.rem(t, tpb), 0)
    return pl.pallas_call(
        kern,
        grid=(n_lat // ROW_TILE,),
        in_specs=[
            pl.BlockSpec((ROW_TILE, d), lambda t: (t, 0)),
            pl.BlockSpec((ROW_TILE, d), xrow),
            pl.BlockSpec((1, SUBLANES, mod.shape[2]), lambda t: (layer, 0, 0)),
            full(w_out), full(lng), full(wr_t), full(rbias),
        ],
        out_specs=specs,
        out_shape=shapes,
        compiler_params=_cparams(("parallel",)),
        name="merge1",
    )(att, x, mod, w_out, lng, wr_t, rbias)


def _moe_plan(eidx):
    k, n = eidx.shape
    a = k * n
    e_flat = eidx.reshape(a)
    seg = TOP_K * ROW_TILE
    onehot = (e_flat[:, None] == jnp.arange(N_EXPERTS, dtype=jnp.int32)[None, :]).astype(F32)
    onehot = onehot.reshape(a // seg, seg, N_EXPERTS)
    tril = lambda m: jnp.tril(jnp.ones((m, m), F32))
    within = jnp.einsum('ij,tjk->tik', tril(seg), onehot)
    seg_total = within[:, -1, :]
    seg_end = jnp.sum(tril(a // seg)[:, :, None] * seg_total[None], axis=1)
    counts = seg_end[-1].astype(jnp.int32)
    csum = within + (seg_end - seg_total)[:, None, :]
    padded = (counts + MOE_ROWS - 1) // MOE_ROWS * MOE_ROWS
    pad_end = jnp.sum(jnp.tril(jnp.ones((N_EXPERTS, N_EXPERTS), jnp.int32)) * padded[None, :], axis=1)
    pad_start = pad_end - padded
    dest = jnp.sum(onehot * (csum - 1.0 + pad_start.astype(F32)[None, None, :]), axis=-1)
    dest = dest.reshape(a).astype(jnp.int32)
    n_blocks = -(-(a + N_EXPERTS * (MOE_ROWS - 1)) // MOE_ROWS)
    first_row = jnp.arange(n_blocks, dtype=jnp.int32) * MOE_ROWS
    block_expert = jnp.minimum(jnp.sum((pad_end[None, :] <= first_row[:, None]).astype(jnp.int32), axis=1),
                               N_EXPERTS - 1)
    return dest.reshape(k, n), block_expert, n_blocks


def _tile_rows_of(dest, tile):
    k, n = dest.shape
    return dest.reshape(k, n // tile, tile).transpose(1, 0, 2).reshape(n // tile, 1, k * tile)


def _dispatch_kernel(dest_ref, h_ref, xs_in_hbm, xs_hbm, sem):
    del xs_in_hbm
    row_tiles = h_ref.shape[0]
    rows = row_tiles * SUBLANES

    def start(i, c):
        for u in range(SUBLANES):
            for choice in range(TOP_K):
                dst = dest_ref[0, 0, choice * rows + i * SUBLANES + u]
                pltpu.make_async_copy(h_ref.at[i, :, u, :], _row(xs_hbm, dst), sem).start(priority=choice)
        return c

    lax.fori_loop(0, row_tiles, start, 0)
    for _ in range(TOP_K):
        pltpu.make_async_copy(h_ref, xs_hbm.at[pl.ds(0, row_tiles)], sem).wait()


def _dispatch(h, dest, n_rows):
    n = h.shape[0] * SUBLANES
    tile = DISPATCH_TILE if n % DISPATCH_TILE == 0 else ROW_TILE
    return pl.pallas_call(
        _dispatch_kernel,
        grid=(n // tile,),
        in_specs=[
            pl.BlockSpec((1, 1, TOP_K * tile), lambda t: (t, 0, 0), memory_space=pltpu.SMEM),
            pl.BlockSpec(_row_shape(tile), lambda t: (t, 0, 0, 0)),
            pl.BlockSpec(memory_space=pl.ANY),
        ],
        out_specs=pl.BlockSpec(memory_space=pl.ANY),
        out_shape=jax.ShapeDtypeStruct(_row_shape(n_rows), jnp.uint32),
        scratch_shapes=[pltpu.SemaphoreType.DMA],
        input_output_aliases={2: 0},
        compiler_params=_cparams(("arbitrary",)),
        name="moe_dispatch",
    )(_tile_rows_of(dest, tile), h, jnp.zeros(_row_shape(n_rows), jnp.uint32))


def _experts_kernel(be_ref, x_ref, wg_ref, wu_ref, wd_ref, o_ref, wbf):
    i = pl.program_id(0)

    @pl.when(jnp.logical_or(i == 0, be_ref[i] != be_ref[jnp.maximum(i - 1, 0)]))
    def _():
        for k, w_ref in enumerate((wg_ref, wu_ref, wd_ref)):
            wbf[k] = w_ref[0].astype(BF16)

    x = _unpack_bf16_pairs(_load_rows(x_ref)).astype(BF16)
    hg = _dot(x, wbf[0])
    hu = _dot(x, wbf[1])
    _store_rows(o_ref, _pack_bf16_pairs(_dot((_silu(hg) * hu).astype(BF16), wbf[2])))


def _experts(xs, block_expert, layer, wg, wu, wd):
    n_blocks = block_expert.shape[0]
    d, dff = wg.shape[2], wg.shape[3]
    rows_spec = pl.BlockSpec(_row_shape(MOE_ROWS), lambda i, be: (i, 0, 0, 0))
    return pl.pallas_call(
        _experts_kernel,
        grid_spec=pltpu.PrefetchScalarGridSpec(
            num_scalar_prefetch=1,
            grid=(n_blocks,),
            in_specs=[
                rows_spec,
                pl.BlockSpec((None, 1, d, dff), lambda i, be: (layer, be[i], 0, 0)),
                pl.BlockSpec((None, 1, d, dff), lambda i, be: (layer, be[i], 0, 0)),
                pl.BlockSpec((None, 1, dff, d), lambda i, be: (layer, be[i], 0, 0)),
            ],
            out_specs=rows_spec,
            scratch_shapes=[pltpu.VMEM((3, d, dff), BF16)],
        ),
        out_shape=jax.ShapeDtypeStruct(xs.shape, jnp.uint32),
        compiler_params=_cparams(("arbitrary",)),
        name="moe_experts",
    )(block_expert, xs, wg, wu, wd)


def _combine_kernel(dcur_ref, dnxt_ref, x_ref, gt_ref, mod_ref, lng_ref, y_hbm, o_ref, ybuf, sem, *,
                    tiles_per_batch, ctx_tiles, nb):
    t = pl.program_id(0)
    last = pl.num_programs(0) - 1
    rows = x_ref.shape[0]
    n = TOP_K * rows
    d = x_ref.shape[1]
    slot = lax.rem(t, 2)

    row_tiles = rows // SUBLANES

    def start_all(idx_ref, s):
        def start(i, c):
            for choice in range(TOP_K):
                for u in range(SUBLANES):
                    j = choice * rows + i * SUBLANES + u
                    pltpu.make_async_copy(_row(y_hbm, idx_ref[0, 0, j]),
                                          ybuf.at[s, choice * row_tiles + i, :, u, :],
                                          sem.at[s]).start(priority=choice)
            return c

        lax.fori_loop(0, row_tiles, start, 0)

    @pl.when(t == 0)
    def _():
        start_all(dcur_ref, 0)

    @pl.when(t < last)
    def _():
        start_all(dnxt_ref, 1 - slot)

    pltpu.make_async_copy(ybuf.at[slot], ybuf.at[slot], sem.at[slot]).wait()
    r = _mod_row(t, tiles_per_batch, ctx_tiles, nb)
    gt = gt_ref[...]

    def choice_rows(choice):
        tiles = pl.ds(choice * row_tiles, row_tiles)
        return jnp.concatenate([ybuf[slot, tiles, j].reshape(rows, LANES) for j in range(ROW_SUB)], axis=1)

    y = (_unpack_bf16_pairs(choice_rows(0)) * gt[:, 0:1]
         + _unpack_bf16_pairs(choice_rows(1)) * gt[:, 1:2])
    g2 = mod_ref[0, pl.ds(r, 1), 5 * d:6 * d]
    z = ALPHA * x_ref[...] + g2 * y
    mu = jnp.mean(z, axis=-1, keepdims=True)
    zc = z - mu
    var = jnp.mean(zc * zc, axis=-1, keepdims=True)
    o_ref[...] = zc * lax.rsqrt(var + LN_EPS) * lng_ref[...]


def _combine(x1, dest, gates, mod, layer, lng, ys, *, tiles_per_batch, ctx_tiles, nb):
    n, d = x1.shape
    nt = n // ROW_TILE
    dest_t = _tile_rows_of(dest, ROW_TILE)
    kern = functools.partial(_combine_kernel, tiles_per_batch=tiles_per_batch, ctx_tiles=ctx_tiles, nb=nb)
    idx_spec = lambda f: pl.BlockSpec((1, 1, TOP_K * ROW_TILE), f, memory_space=pltpu.SMEM)
    return pl.pallas_call(
        kern,
        grid=(nt,),
        in_specs=[
            idx_spec(lambda t: (t, 0, 0)),
            idx_spec(lambda t: (jnp.minimum(t + 1, nt - 1), 0, 0)),
            pl.BlockSpec((ROW_TILE, d), lambda t: (t, 0)),
            pl.BlockSpec((ROW_TILE, TOP_K), lambda t: (t, 0)),
            pl.BlockSpec((1, SUBLANES, mod.shape[2]), lambda t: (layer, 0, 0)),
            pl.BlockSpec((1, d), lambda t: (0, 0)),
            pl.BlockSpec(memory_space=pl.ANY),
        ],
        out_specs=pl.BlockSpec((ROW_TILE, d), lambda t: (t, 0)),
        out_shape=jax.ShapeDtypeStruct((n, d), F32),
        scratch_shapes=[pltpu.VMEM((2, TOP_K * ROW_TILE // SUBLANES, ROW_SUB, SUBLANES, LANES), jnp.uint32),
                        pltpu.SemaphoreType.DMA((2,))],
        compiler_params=_cparams(("arbitrary",)),
        name="moe_combine%d" % layer,
    )(dest_t, dest_t, x1, gates.T, mod, lng, ys)


def _moe_layer(x1, h2, eidx, gates, mod, layer, lng, wg, wu, wd, *, tiles_per_batch, ctx_tiles, nb):
    dest, block_expert, n_blocks = _moe_plan(eidx)
    xs = _dispatch(h2, dest, n_blocks * MOE_ROWS)
    ys = _experts(xs, block_expert, layer, wg, wu, wd)
    return _combine(x1, dest, gates, mod, layer, lng, ys,
                    tiles_per_batch=tiles_per_batch, ctx_tiles=ctx_tiles, nb=nb)


def _half_norms(x):
    lane = lax.broadcasted_iota(jnp.int32, x.shape, 1)
    sq = x * x
    lo = jnp.sum(jnp.where(lane < DIFF_DH, sq, 0.0), axis=-1, keepdims=True)
    hi = jnp.sum(jnp.where(lane >= DIFF_DH, sq, 0.0), axis=-1, keepdims=True)
    return jnp.sqrt(lo), jnp.sqrt(hi)


def _attn_kernel(lam_ref, q_ref, k_ref, v_ref, g_ref, o_ref, vext, s_buf0, s_buf1,
                 p_buf0, p_buf1, corr_buf0, corr_buf1, m_buf, shift_buf, acc, *, out_scale, ctx_len):
    t_len = k_ref.shape[0]
    nk = t_len // ATT_TK
    dv = v_ref.shape[1]
    tq = acc.shape[1]
    nq = o_ref.shape[0] // tq
    n_tiles = nq * nk

    vext[:, 0:dv] = v_ref[...]
    vext[:, dv:2 * dv] = jnp.ones((t_len, dv), BF16)
    lam = lam_ref[0]
    gain = g_ref[...] * out_scale

    s_bufs, p_bufs, corr_bufs = (s_buf0, s_buf1), (p_buf0, p_buf1), (corr_buf0, corr_buf1)

    def key_rows(kj):
        return pl.ds(pl.multiple_of(kj * ATT_TK, ATT_TK), ATT_TK)

    def query_rows(qi):
        return pl.ds(pl.multiple_of(ctx_len + qi * tq, ROW_TILE), tq)

    def column_max(norms, carry):
        return tuple(jnp.maximum(c, jnp.max(n, axis=0, keepdims=True)) for n, c in zip(norms, carry))

    def key_norms(kj, carry):
        return column_max(_half_norms(k_ref[key_rows(kj), :].astype(F32)), carry)

    zero11 = jnp.zeros((1, 1), F32)
    kmax = lax.fori_loop(0, nk, key_norms, (zero11, zero11))

    def query_shifts(qi, carry):
        shifts = tuple(n * k for n, k in zip(_half_norms(q_ref[query_rows(qi), :].astype(F32)), kmax))
        shift_buf[qi, :, 0:1] = shifts[0]
        shift_buf[qi, :, 1:2] = shifts[1]
        return column_max(shifts, carry)

    worst = lax.fori_loop(0, nq, query_shifts, (zero11, zero11))
    bounded = jnp.max(jnp.maximum(worst[0], worst[1])) <= ATT_SAFE_SHIFT

    def advance(tile):
        qi, kj = tile
        wrap = kj + 1 == nk
        return jnp.where(wrap, qi + 1, qi), jnp.where(wrap, 0, kj + 1)

    def scores(tile, slot):
        qi, kj = tile
        q = q_ref[query_rows(qi), :]
        lane = lax.broadcasted_iota(jnp.int32, q.shape, 1)
        zero = jnp.zeros(q.shape, q.dtype)
        k = k_ref[key_rows(kj), :]
        s_bufs[slot][0] = _dot_nt(jnp.where(lane < DIFF_DH, q, zero), k)
        s_bufs[slot][1] = _dot_nt(jnp.where(lane >= DIFF_DH, q, zero), k)

    def numerators(tile, slot, online):
        qi, kj = tile
        for w in range(2):
            s = s_bufs[slot][w]
            if online:
                m_old = jnp.where(kj == 0, -jnp.inf, m_buf[w])
                m_new = jnp.maximum(m_old, jnp.max(s, axis=-1, keepdims=True))
                corr_bufs[slot][w] = jnp.exp2(m_old - m_new)
                m_buf[w] = m_new
            else:
                m_new = shift_buf[qi, :, w:w + 1]
            p_bufs[slot][w] = jnp.exp2(s - m_new).astype(BF16)

    def values(tile, slot, online):
        qi, kj = tile
        ve = vext[key_rows(kj), :]
        a = []
        for w in range(2):
            keep = corr_bufs[slot][w] if online else jnp.where(kj == 0, 0.0, 1.0)
            a.append(keep * acc[w] + _dot(p_bufs[slot][w], ve))
            acc[w] = a[w]
        o = a[0][:, 0:dv] / a[0][:, dv:2 * dv] - lam * (a[1][:, 0:dv] / a[1][:, dv:2 * dv])
        o = o * lax.rsqrt(jnp.mean(o * o, axis=-1, keepdims=True) + GN_EPS)
        o_ref[pl.ds(pl.multiple_of(qi * tq, tq), tq), :] = (o * gain).astype(BF16)

    def pipeline(online):
        def step(tiles, slot):
            a, b, c = tiles
            values(c, slot, online)
            scores(a, slot)
            numerators(b, 1 - slot, online)
            return advance(a), a, b

        acc[...] = jnp.zeros(acc.shape, F32)
        t0 = (jnp.int32(0), jnp.int32(0))
        t1 = advance(t0)
        scores(t0, 0)
        scores(t1, 1)
        numerators(t0, 0, online)

        def pair(_, tiles):
            return step(step(tiles, 0), 1)

        _, last, prev = lax.fori_loop(0, (n_tiles - 2) // 2, pair, (advance(t1), t1, t0))
        numerators(last, (n_tiles - 1) % 2, online)
        values(prev, n_tiles % 2, online)
        values(last, (n_tiles - 1) % 2, online)

    pl.when(bounded)(functools.partial(pipeline, False))
    pl.when(jnp.logical_not(bounded))(functools.partial(pipeline, True))


def _diff_attention(qkv, lam, subln_g, lambda_init, *, nb, l_len, t_len, ctx_len):
    d = D_MODEL
    tq = ATT_TQ
    dv = 2 * DIFF_DH
    nq = l_len // tq
    assert (nq * (t_len // ATT_TK)) % 2 == 0
    kern = functools.partial(_attn_kernel, out_scale=1.0 - lambda_init, ctx_len=ctx_len)
    return pl.pallas_call(
        kern,
        grid_spec=pltpu.PrefetchScalarGridSpec(
            num_scalar_prefetch=1,
            grid=(nb, DIFF_HEADS),
            in_specs=[
                pl.BlockSpec((t_len, LANES), lambda b, h, lam: (b, h)),
                pl.BlockSpec((t_len, LANES), lambda b, h, lam: (b, DIFF_HEADS + h)),
                pl.BlockSpec((t_len, LANES), lambda b, h, lam: (b, 2 * DIFF_HEADS + h)),
                pl.BlockSpec((1, LANES), lambda b, h, lam: (0, 0)),
            ],
            out_specs=pl.BlockSpec((l_len, LANES), lambda b, h, lam: (b, h)),
            scratch_shapes=[
                pltpu.VMEM((t_len, 2 * dv), BF16),
                pltpu.VMEM((2, tq, ATT_TK), F32), pltpu.VMEM((2, tq, ATT_TK), F32),
                pltpu.VMEM((2, tq, ATT_TK), BF16), pltpu.VMEM((2, tq, ATT_TK), BF16),
                pltpu.VMEM((2, tq, 1), F32), pltpu.VMEM((2, tq, 1), F32),
                pltpu.VMEM((2, tq, 1), F32), pltpu.VMEM((nq, tq, 2), F32),
                pltpu.VMEM((2, tq, 2 * dv), F32),
            ],
        ),
        out_shape=jax.ShapeDtypeStruct((nb * l_len, d), BF16),
        compiler_params=_cparams(("parallel", "parallel"), ATT_VMEM_LIMIT),
        name="diff_attention",
    )(lam, qkv, qkv, qkv, subln_g.reshape(1, LANES).astype(F32))


def _rope_angles(pos, dim):
    inv = (np.float32(ROPE_BASE) ** (-np.arange(0, dim, 2, dtype=np.float32) / np.float32(dim)))
    return pos.astype(np.float32)[:, None] * inv.astype(np.float32)[None, :]


def _ret_rope_tables(l_len, ctx_len):
    half = RET_DK // 2
    ang = _rope_angles(np.arange(l_len), RET_DK)
    ang = np.concatenate([np.zeros((ctx_len, half), np.float32), ang], axis=0)
    cos64 = np.concatenate([np.cos(ang), np.cos(ang)], axis=1)
    sin64 = np.concatenate([-np.sin(ang), np.sin(ang)], axis=1)
    kscale = np.float32(RET_DK ** -0.5)
    cos = np.concatenate([cos64, cos64 * kscale], axis=1)
    sin = np.concatenate([sin64, sin64 * kscale], axis=1)
    return jnp.asarray(cos[None], F32), jnp.asarray(sin[None], F32)


def _attn_rope_tables(l_len, ctx_len):
    quarter = DIFF_DH // 4
    pos = np.arange(l_len)
    pad = lambda a: np.concatenate([np.zeros((ctx_len, quarter), np.float32), a], axis=0)
    ang_r = pad(_rope_angles(pos // GRID_W, DIFF_DH // 2))
    ang_c = pad(_rope_angles(pos % GRID_W, DIFF_DH // 2))
    cos64 = np.concatenate([np.cos(ang_r)] * 2 + [np.cos(ang_c)] * 2, axis=1)
    sin64 = np.concatenate([-np.sin(ang_r), np.sin(ang_r), -np.sin(ang_c), np.sin(ang_c)], axis=1)
    cos = np.concatenate([cos64, cos64], axis=1)
    sin = np.concatenate([sin64, sin64], axis=1)
    qscale = np.float32(DIFF_DH ** -0.5 * math.log2(math.e))
    return (jnp.asarray(np.stack([cos * qscale, cos]), F32), jnp.asarray(np.stack([sin * qscale, sin]), F32))


def kernel(x, c, ctx, c_ctx, ada_w, ada_b, ln_g, w_in_ab, ret_decay_logit, s5_lam_re, s5_lam_im,
           s5_log_dt, s5_b_re, s5_b_im, s5_c_re, s5_c_im, s5_d, s5_w_glu, w_out_ab, w_in_c,
           diff_lambda, diff_subln_g, w_out_c, router_w, router_bias, exp_w_gate, exp_w_up,
           exp_w_down):
    nb, l_len, d = x.shape
    ctx_len = ctx.shape[1]
    t_len = ctx_len + l_len
    tpb = t_len // ROW_TILE
    ctx_tiles = ctx_len // ROW_TILE
    assert d == D_MODEL and nb < SUBLANES
    assert l_len % ATT_TQ == 0 and ctx_len % ROW_TILE == 0 and t_len % ATT_TK == 0

    x_lat, x_ctx = x.reshape(nb * l_len, d), ctx.reshape(nb * ctx_len, d)
    c_all = jnp.concatenate([c, c_ctx[None].astype(c.dtype)], axis=0)
    c_pad = jnp.zeros((SUBLANES, d), F32).at[:nb + 1].set(c_all)
    mod = _adaln(c_pad, ada_w, ada_b)

    wr_t = router_w.T
    rbias = router_bias.reshape(N_EXPERTS, 1).astype(F32)

    w0 = w_in_ab[0]
    q_w, k_w, v_w, g_w, u_w = jnp.split(w0, (RET_QK_W, 2 * RET_QK_W, 2 * RET_QK_W + RET_V_W,
                                             2 * RET_QK_W + 2 * RET_V_W), axis=1)
    qk_w = jnp.concatenate([q_w.reshape(d, RET_HEADS, RET_DK), k_w.reshape(d, RET_HEADS, RET_DK)],
                           axis=2).reshape(d, 2 * RET_QK_W)
    w0p = jnp.concatenate([qk_w, v_w, g_w, u_w], axis=1).astype(BF16)
    cos0, sin0 = _ret_rope_tables(l_len, ctx_len)
    rope_tab0 = [0] * RET_HEADS + [None] * ((w0p.shape[1] - S5_CH - 2 * RET_QK_W) // LANES)
    proj0, u5 = _inproj((x_lat, x_ctx), mod, 0, w0p, cos0, sin0, rope_tab0, RET_DK // 2,
                        nb=nb, t_len=t_len, ctx_len=ctx_len, s5_cols=S5_CH)

    log_gammas = jax.nn.log_sigmoid(ret_decay_logit[0].astype(F32))
    ret = _retention(proj0, log_gammas, nb=nb, t_len=t_len, ctx_len=ctx_len)

    s5_ops = _s5_operators(s5_lam_re[0], s5_lam_im[0], s5_log_dt[0], s5_b_re[0], s5_b_im[0],
                           s5_c_re[0], s5_c_im[0], s5_d[0])
    s5y = _s5(u5, *s5_ops, nb=nb, nctx=ctx_len // S5_CHUNK)

    x1, h2, eidx, gates = _merge0(ret, proj0, s5y, x_lat, x_ctx, mod, s5_w_glu[0].astype(BF16),
                                  w_out_ab[0].astype(BF16), ln_g[0, 0].reshape(1, d), wr_t, rbias,
                                  nb=nb, t_len=t_len, ctx_len=ctx_len)
    experts = (exp_w_gate, exp_w_up, exp_w_down)
    x2 = _moe_layer(x1, h2, eidx, gates, mod, 0, ln_g[0, 1].reshape(1, d), *experts,
                    tiles_per_batch=tpb, ctx_tiles=ctx_tiles, nb=nb)

    cos1, sin1 = _attn_rope_tables(l_len, ctx_len)
    n_heads_cols = D_MODEL // LANES
    rope_tab1 = [0] * n_heads_cols + [1] * n_heads_cols + [None] * n_heads_cols
    qkv, = _inproj((x2,), mod, 1, w_in_c[0].astype(BF16), cos1, sin1, rope_tab1, DIFF_DH // 4,
                   nb=nb, t_len=t_len, ctx_len=ctx_len)
    lf = diff_lambda[0].astype(F32)
    lambda_init = 0.8 - 0.6 * math.exp(-0.3 * 1)
    lam = (jnp.exp(jnp.sum(lf[0] * lf[1])) - jnp.exp(jnp.sum(lf[2] * lf[3])) + lambda_init).reshape(1)
    att = _diff_attention(qkv, lam, diff_subln_g[0], lambda_init,
                          nb=nb, l_len=l_len, t_len=t_len, ctx_len=ctx_len)
    x3, h3, eidx1, gates1 = _merge1(att, x2, mod, 1, w_out_c[0].astype(BF16), ln_g[1, 0].reshape(1, d),
                                    wr_t, rbias, nb=nb, l_len=l_len, t_len=t_len, ctx_len=ctx_len)
    out = _moe_layer(x3, h3, eidx1, gates1, mod, 1, ln_g[1, 1].reshape(1, d), *experts,
                     tiles_per_batch=l_len // ROW_TILE, ctx_tiles=0, nb=nb)
    return out.reshape(nb, l_len, d)
```

```python
import functools
import math

import jax
import jax.numpy as jnp
import numpy as np
from jax import lax
from jax.experimental import pallas as pl
from jax.experimental.pallas import tpu as pltpu

F32 = jnp.float32
BF16 = jnp.bfloat16

D_MODEL = 1024
DEPTH = 2
GRID_W = 64
ALPHA = (2.0 * DEPTH) ** 0.25
LN_EPS = 1e-5
GN_EPS = 1e-6
ROPE_BASE = 10000.0
RET_DK = 64
RET_DV = 128
RET_HEADS = 6
RET_QK_W = RET_HEADS * RET_DK
RET_V_W = RET_HEADS * RET_DV
S5_CH = 256
S5_P = 16
S5_G = 16
S5_N = 64
DIFF_HEADS = 8
DIFF_DH = 64
N_EXPERTS = 16
EXPERTS_PER_GROUP = 4
TOP_K = 2

LANES = 128
SUBLANES = 8
MXU_DIM = 256
ROW_TILE = 256
RET_CHUNK = 256
RET_STATE_UNROLL = 3
RET_OUT_UNROLL = 11
S5_CHUNK = 8
S5_HALVES = 2
MOE_ROWS = 256
DISPATCH_TILE = 1024
ROW_WORDS = D_MODEL // 2
ROW_SUB = ROW_WORDS // LANES
ATT_TQ = 1024
ATT_TK = 768
ATT_SAFE_SHIFT = 48.0
VMEM_LIMIT = 48 * 1024 * 1024
ATT_VMEM_LIMIT = 56 * 1024 * 1024


def _cparams(sem, vmem_limit=VMEM_LIMIT):
    return pltpu.CompilerParams(dimension_semantics=sem, vmem_limit_bytes=vmem_limit)


def _dot(a, b):
    return jnp.dot(a, b, preferred_element_type=F32)


def _dot_nt(a, b):
    return lax.dot_general(a, b, (((1,), (1,)), ((), ())), preferred_element_type=F32)


def _dot_tn(a, b):
    return lax.dot_general(a, b, (((0,), (0,)), ((), ())), preferred_element_type=F32)


def _split_bf16(x):
    hi = x.astype(BF16)
    lo = (x - hi.astype(F32)).astype(BF16)
    return hi, lo


def _dot3(a, b):
    ah, al = _split_bf16(a)
    bh, bl = _split_bf16(b)
    return _dot(ah, bh) + _dot(ah, bl) + _dot(al, bh)


def _dot3_nt(a, b):
    ah, al = _split_bf16(a)
    bh, bl = _split_bf16(b)
    return _dot_nt(ah, bh) + _dot_nt(ah, bl) + _dot_nt(al, bh)


def _sigmoid(x):
    return 1.0 / (1.0 + jnp.exp(-x))


def _silu(x):
    return x * _sigmoid(x)


def _pack_bf16_pairs(v):
    half = v.shape[1] // 2
    bits = lax.bitcast_convert_type(v.astype(BF16).astype(F32), jnp.uint32)
    return (bits[:, :half] >> 16) | (bits[:, half:] & jnp.uint32(0xFFFF0000))


def _unpack_bf16_pairs(p):
    lo = lax.bitcast_convert_type(p << 16, F32)
    hi = lax.bitcast_convert_type(p & jnp.uint32(0xFFFF0000), F32)
    return jnp.concatenate([lo, hi], axis=1)


def _row_shape(n_rows):
    return (n_rows // SUBLANES, ROW_SUB, SUBLANES, LANES)


def _row(ref, r):
    return ref.at[lax.shift_right_logical(r, 3), :, r & (SUBLANES - 1), :]


def _store_rows(ref, packed):
    rows = packed.shape[0]
    for j in range(ROW_SUB):
        ref[:, j] = packed[:, j * LANES:(j + 1) * LANES].reshape(rows // SUBLANES, SUBLANES, LANES)


def _load_rows(ref):
    rows = ref.shape[0] * SUBLANES
    return jnp.concatenate([ref[:, j].reshape(rows, LANES) for j in range(ROW_SUB)], axis=1)


def _gelu_tanh(x):
    c = math.sqrt(2.0 / math.pi)
    return 0.5 * x * (1.0 + jnp.tanh(c * (x + 0.044715 * (x * x * x))))


def _adaln_kernel(c_ref, w_ref, b_ref, o_ref):
    c = c_ref[...]
    o_ref[0] = _dot3(_silu(c), w_ref[0]) + b_ref[0]


def _adaln(c_pad, ada_w, ada_b):
    depth, d, n = ada_w.shape
    tn = 1536
    return pl.pallas_call(
        _adaln_kernel,
        grid=(depth, n // tn),
        in_specs=[
            pl.BlockSpec((SUBLANES, d), lambda i, j: (0, 0)),
            pl.BlockSpec((1, d, tn), lambda i, j: (i, 0, j)),
            pl.BlockSpec((1, 1, tn), lambda i, j: (i, 0, j)),
        ],
        out_specs=pl.BlockSpec((1, SUBLANES, tn), lambda i, j: (i, 0, j)),
        out_shape=jax.ShapeDtypeStruct((depth, SUBLANES, n), F32),
        compiler_params=_cparams(("parallel", "parallel")),
        name="adaln",
    )(c_pad, ada_w, ada_b.reshape(depth, 1, n))


def _mod_row(t, tiles_per_batch, ctx_tiles, nb):
    b = lax.div(t, tiles_per_batch)
    w = lax.rem(t, tiles_per_batch)
    return jnp.where(w < ctx_tiles, nb, b)


def _rope_block(a, cos, sin, half):
    lane = lax.broadcasted_iota(jnp.int32, a.shape, 1)
    first = lax.rem(lane, 2 * half) < half
    rot = jnp.where(first, pltpu.roll(a, LANES - half, 1), pltpu.roll(a, half, 1))
    return a * cos + rot * sin


def _chunk_perm(rows, transpose):
    per = rows // S5_CHUNK
    r = lax.broadcasted_iota(jnp.int32, (rows, rows), 0)
    c = lax.broadcasted_iota(jnp.int32, (rows, rows), 1)
    if transpose:
        r, c = c, r
    return (c == S5_CHUNK * lax.rem(r, per) + lax.div(r, per)).astype(BF16)


def _token_specs(d, tiles_per_batch, ctx_tiles):
    lat_tiles = tiles_per_batch - ctx_tiles
    b = lambda t: lax.div(t, tiles_per_batch)
    w = lambda t: lax.rem(t, tiles_per_batch)
    return (pl.BlockSpec((ROW_TILE, d), lambda t: (b(t) * lat_tiles + jnp.maximum(w(t) - ctx_tiles, 0), 0)),
            pl.BlockSpec((ROW_TILE, d), lambda t: (b(t) * ctx_tiles + jnp.minimum(w(t), ctx_tiles - 1), 0)))


def _token_tile(x_refs, tiles_per_batch, ctx_tiles):
    if len(x_refs) == 1:
        return x_refs[0][...]
    is_ctx = lax.rem(pl.program_id(0), tiles_per_batch) < ctx_tiles
    return jnp.where(is_ctx, x_refs[1][...], x_refs[0][...])


def _inproj_kernel(*refs, n_x, tiles_per_batch, ctx_tiles, nb, rope_tab, rope_half):
    x_refs, (mod_ref, w_ref, cos_ref, sin_ref, o_ref), rest = refs[:n_x], refs[n_x:n_x + 5], refs[n_x + 5:]
    d = w_ref.shape[0]
    n = o_ref.shape[1]
    r = _mod_row(pl.program_id(0), tiles_per_batch, ctx_tiles, nb)
    sh = mod_ref[0, pl.ds(r, 1), 0:d]
    sc = mod_ref[0, pl.ds(r, 1), d:2 * d]
    xm = (_token_tile(x_refs, tiles_per_batch, ctx_tiles) * (1.0 + sc) + sh).astype(BF16)
    for j in range(n // MXU_DIM):
        acc = _dot(xm, w_ref[:, j * MXU_DIM:(j + 1) * MXU_DIM])
        parts = []
        for s in range(MXU_DIM // LANES):
            blk = acc[:, s * LANES:(s + 1) * LANES]
            tab = rope_tab[j * (MXU_DIM // LANES) + s]
            if tab is not None:
                blk = _rope_block(blk, cos_ref[tab], sin_ref[tab], rope_half)
            parts.append(blk)
        o_ref[:, j * MXU_DIM:(j + 1) * MXU_DIM] = jnp.concatenate(parts, axis=1).astype(BF16)
    if rest:
        u_ref, = rest
        rows = o_ref.shape[0]
        per = rows // S5_CHUNK
        u = _dot(xm, w_ref[:, n:n + S5_CH]).astype(BF16)
        up = _dot(_chunk_perm(rows, False), u).astype(BF16)
        for s in range(S5_CHUNK):
            for hf in range(S5_CH // LANES):
                u_ref[hf, :, s * LANES:(s + 1) * LANES] = up[s * per:(s + 1) * per,
                                                             hf * LANES:(hf + 1) * LANES]


def _inproj(xs, mod, layer, w, cos, sin, rope_tab, rope_half, *, nb, t_len, ctx_len, s5_cols=0):
    nt, d = nb * t_len, w.shape[0]
    n = w.shape[1] - s5_cols
    tpb = t_len // ROW_TILE
    ctx_tiles = ctx_len // ROW_TILE
    kern = functools.partial(_inproj_kernel, n_x=len(xs), tiles_per_batch=tpb, ctx_tiles=ctx_tiles,
                             nb=nb, rope_tab=tuple(rope_tab), rope_half=rope_half)
    x_specs = ([pl.BlockSpec((ROW_TILE, d), lambda t: (t, 0))] if len(xs) == 1
               else list(_token_specs(d, tpb, ctx_tiles)))
    ntab = cos.shape[0]
    out_specs = [pl.BlockSpec((ROW_TILE, n), lambda t: (t, 0))]
    out_shape = [jax.ShapeDtypeStruct((nt, n), BF16)]
    if s5_cols:
        halves, per = s5_cols // LANES, ROW_TILE // S5_CHUNK
        out_specs.append(pl.BlockSpec((halves, per, S5_CHUNK * LANES), lambda t: (0, t, 0)))
        out_shape.append(jax.ShapeDtypeStruct((halves, nt // S5_CHUNK, S5_CHUNK * LANES), BF16))
    return pl.pallas_call(
        kern,
        grid=(nt // ROW_TILE,),
        in_specs=x_specs + [
            pl.BlockSpec((1, SUBLANES, mod.shape[2]), lambda t: (layer, 0, 0)),
            pl.BlockSpec(w.shape, lambda t: (0, 0)),
            pl.BlockSpec((ntab, ROW_TILE, LANES), lambda t: (0, lax.rem(t, tpb), 0)),
            pl.BlockSpec((ntab, ROW_TILE, LANES), lambda t: (0, lax.rem(t, tpb), 0)),
        ],
        out_specs=out_specs,
        out_shape=out_shape,
        compiler_params=_cparams(("parallel",)),
        name="inproj%d" % layer,
    )(*xs, mod, w, cos, sin)


def _retention_kernel(lg_ref, qk_ref, v_ref, o_ref, sf_ref, sb_ref, *, nctx):
    c_len = RET_CHUNK
    t_len = qk_ref.shape[0]
    nc = t_len // c_len
    h = pl.program_id(1)
    lgf = lg_ref[0, h]
    lgb = lg_ref[1, h]
    ii = lax.broadcasted_iota(jnp.int32, (c_len, 1), 0).astype(F32)
    jj = lax.broadcasted_iota(jnp.int32, (1, c_len), 1).astype(F32)
    diff = ii - jj
    decay = jnp.where(diff >= 0.0, jnp.exp(lgf * jnp.maximum(diff, 0.0)),
                      jnp.exp(lgb * jnp.maximum(-diff, 0.0)))
    kdf = jnp.exp(lgf * (c_len - 1.0 - ii))
    kdb = jnp.exp(lgb * ii)
    qdf = jnp.exp(lgf * (ii + 1.0))
    qdb = jnp.exp(lgb * (c_len - ii))
    zrow = jnp.zeros((1, RET_DV), F32)
    gf_chunk = jnp.exp(zrow + lgf * c_len)
    gb_chunk = jnp.exp(zrow + lgb * c_len)

    def load(c):
        rows = pl.ds(pl.multiple_of(c * c_len, c_len), c_len)
        qk = qk_ref[rows, :].astype(F32)
        return qk[:, :RET_DK], qk[:, RET_DK:], v_ref[rows, :]

    def states(j, carry):
        sf, sb = carry
        cb = jnp.where(j < nctx, nctx - 1 - j, nc - 1 - (j - nctx))
        sf_ref[j] = sf
        sb_ref[cb] = sb
        _, kf, vf = load(j)
        _, kb, vb = load(cb)
        return (gf_chunk * sf + _dot_tn((kf * kdf).astype(BF16), vf),
                gb_chunk * sb + _dot_tn((kb * kdb).astype(BF16), vb))

    zero_state = jnp.zeros((RET_DK, RET_DV), F32)
    lax.fori_loop(0, nc, states, (zero_state, zero_state), unroll=RET_STATE_UNROLL)

    def out_chunk(c, carry):
        q, k, v = load(c)
        scores = _dot_nt(q.astype(BF16), k.astype(BF16)) * decay
        o = _dot(scores.astype(BF16), v)
        o = o + _dot((q * qdf).astype(BF16), sf_ref[c].astype(BF16))
        o = o + _dot((q * qdb).astype(BF16), sb_ref[c].astype(BF16))
        mu = jnp.mean(o, axis=-1, keepdims=True)
        oc = o - mu
        var = jnp.mean(oc * oc, axis=-1, keepdims=True)
        rows = pl.ds(pl.multiple_of(c * c_len, c_len), c_len)
        o_ref[rows, :] = (oc * lax.rsqrt(var + GN_EPS)).astype(BF16)
        return carry

    lax.fori_loop(0, nc, out_chunk, 0, unroll=RET_OUT_UNROLL)


def _retention(proj, log_gammas, *, nb, t_len, ctx_len):
    nt = proj.shape[0]
    nc = t_len // RET_CHUNK
    kern = functools.partial(_retention_kernel, nctx=ctx_len // RET_CHUNK)
    vcol0 = RET_HEADS
    return pl.pallas_call(
        kern,
        grid_spec=pltpu.PrefetchScalarGridSpec(
            num_scalar_prefetch=1,
            grid=(nb, RET_HEADS),
            in_specs=[
                pl.BlockSpec((t_len, LANES), lambda b, h, lg: (b, h)),
                pl.BlockSpec((t_len, LANES), lambda b, h, lg: (b, vcol0 + h)),
            ],
            out_specs=pl.BlockSpec((t_len, LANES), lambda b, h, lg: (b, h)),
            scratch_shapes=[pltpu.VMEM((nc, RET_DK, RET_DV), F32),
                            pltpu.VMEM((nc, RET_DK, RET_DV), F32)],
        ),
        out_shape=jax.ShapeDtypeStruct((nt, RET_V_W), BF16),
        compiler_params=_cparams(("parallel", "parallel")),
        name="retention",
    )(log_gammas, proj, proj)


def _s5_operators(lam_re, lam_im, log_dt, b_re, b_im, c_re, c_im, d_skip):
    tc = S5_CHUNK
    hp = lax.Precision.HIGHEST
    ks = jnp.arange(tc + 1, dtype=F32)
    pw, bbar, cm = [], [], []
    for direction in range(2):
        dt = jnp.exp(log_dt[direction].astype(F32))[:, None]
        lam = lax.complex(lam_re[direction].astype(F32), lam_im[direction].astype(F32))
        z = lam * dt
        p = jnp.exp(z[None] * ks[:, None, None])
        lam_bar = p[1]
        bb = ((lam_bar - 1.0) / lam)[..., None] * lax.complex(
            b_re[direction].astype(F32), b_im[direction].astype(F32))
        pw.append(p)
        bbar.append(bb)
        cm.append(lax.complex(c_re[direction].astype(F32), c_im[direction].astype(F32)))

    def lag_kernel(p, bb, c):
        return jnp.einsum('gpn,kgn,gnq->kgpq', c, p[:tc], bb, precision=hp).real

    kf = lag_kernel(pw[0], bbar[0], cm[0])
    kb = lag_kernel(pw[1], bbar[1], cm[1])
    k0 = kf[0] + kb[0] + jnp.eye(S5_P, dtype=F32)[None] * d_skip.astype(F32)[:, :, None]
    kcat = jnp.concatenate([kb[1:][::-1], k0[None], kf[1:]], axis=0)
    s_idx = jnp.arange(tc)[:, None]
    t_idx = jnp.arange(tc)[None, :]
    m5 = kcat[t_idx - s_idx + tc - 1]
    hg = S5_G // S5_HALVES
    eye = jnp.eye(hg, dtype=F32)
    split = lambda z, axis: z.reshape(z.shape[:axis] + (S5_HALVES, hg) + z.shape[axis + 1:])
    wide = tc * hg * S5_P
    intra = jnp.einsum('sthgpq,gk->hsgqtkp', split(m5, 2), eye).reshape(S5_HALVES, wide, wide)

    ef = pw[0][:tc][::-1][:, :, :, None] * bbar[0][None]
    eb = pw[1][:tc][:, :, :, None] * bbar[1][None]
    parts_in = jnp.stack([ef.real, ef.imag, eb.real, eb.imag], axis=0)
    w_in = jnp.einsum('cshgnq,gk->hsgqckn', split(parts_in, 2), eye).reshape(
        S5_HALVES, wide, 4 * hg * S5_N)
    of = cm[0][None] * pw[0][1:][:, :, None, :]
    ob = cm[1][None] * pw[1][1:][::-1][:, :, None, :]
    parts_out = jnp.stack([of.real, -of.imag, ob.real, -ob.imag], axis=0)
    w_out = jnp.einsum('cthgpn,gk->hckntgp', split(parts_out, 2), eye).reshape(
        S5_HALVES, 4 * hg * S5_N, wide)
    a = jnp.stack([pw[0][tc].real, pw[0][tc].imag, pw[1][tc].real, pw[1][tc].imag], axis=0)
    a = a.reshape(4, S5_HALVES, hg * S5_N).transpose(1, 0, 2)
    return intra.astype(BF16), w_in.astype(BF16), w_out.astype(BF16), a


def _s5_kernel(x_ref, wi_ref, win_ref, wout_ref, a_ref, y_ref, st_ref, *, nctx):
    x = x_ref[0]
    nc = x.shape[0]
    w = a_ref.shape[2]
    st_ref[...] = _dot(x, win_ref[0])
    afr, afi, abr, abi = (a_ref[0, i:i + 1, :] for i in range(4))

    def step(j, carry):
        fr, fi, br, bi = carry
        rf = pl.ds(j, 1)
        rb = pl.ds(jnp.where(j < nctx, nctx - 1 - j, nc - 1 - (j - nctx)), 1)
        efr, efi = st_ref[rf, 0:w], st_ref[rf, w:2 * w]
        ebr, ebi = st_ref[rb, 2 * w:3 * w], st_ref[rb, 3 * w:4 * w]
        st_ref[rf, 0:w] = fr
        st_ref[rf, w:2 * w] = fi
        st_ref[rb, 2 * w:3 * w] = br
        st_ref[rb, 3 * w:4 * w] = bi
        return (afr * fr - afi * fi + efr, afr * fi + afi * fr + efi,
                abr * br - abi * bi + ebr, abr * bi + abi * br + ebi)

    z = jnp.zeros((1, w), F32)
    lax.fori_loop(0, nc, step, (z, z, z, z), unroll=4)
    y_ref[0] = (_dot(x, wi_ref[0]) + _dot(st_ref[...].astype(BF16), wout_ref[0])).astype(BF16)


def _s5(xc, intra, w_in, w_out, a, *, nb, nctx):
    halves, rows, wide = xc.shape
    nc = rows // nb
    kern = functools.partial(_s5_kernel, nctx=nctx)
    per_half = lambda arr: pl.BlockSpec((1,) + arr.shape[1:], lambda hf, b: (hf, 0, 0))
    return pl.pallas_call(
        kern,
        grid=(halves, nb),
        in_specs=[
            pl.BlockSpec((1, nc, wide), lambda hf, b: (hf, b, 0)),
            per_half(intra), per_half(w_in), per_half(w_out), per_half(a),
        ],
        out_specs=pl.BlockSpec((1, nc, wide), lambda hf, b: (hf, b, 0)),
        out_shape=jax.ShapeDtypeStruct(xc.shape, BF16),
        scratch_shapes=[pltpu.VMEM((nc, w_in.shape[2]), F32)],
        compiler_params=_cparams(("parallel", "parallel")),
        name="s5",
    )(xc, intra, w_in, w_out, a)


def _route(logits_t, bias):
    scores = _sigmoid(logits_t)
    biased = scores + bias
    s_rows = [scores[e:e + 1, :] for e in range(N_EXPERTS)]
    b_rows = [biased[e:e + 1, :] for e in range(N_EXPERTS)]
    n_groups = N_EXPERTS // EXPERTS_PER_GROUP
    best = None
    sel = None
    for g in range(n_groups):
        a, b, c, d = b_rows[4 * g:4 * g + 4]
        hi1, lo1 = jnp.maximum(a, b), jnp.minimum(a, b)
        hi2, lo2 = jnp.maximum(c, d), jnp.minimum(c, d)
        top1 = jnp.maximum(hi1, hi2)
        top2 = jnp.maximum(jnp.minimum(hi1, hi2), jnp.maximum(lo1, lo2))
        gs = top1 + top2
        if g == 0:
            best, sel = gs, jnp.zeros(gs.shape, jnp.int32)
        else:
            better = gs > best
            sel = jnp.where(better, g, sel)
            best = jnp.where(better, gs, best)
    neg = jnp.full(best.shape, -jnp.inf, F32)
    masked = [jnp.where(sel == (e // EXPERTS_PER_GROUP), b_rows[e], neg) for e in range(N_EXPERTS)]
    v1, i1, g1 = masked[0], jnp.zeros(best.shape, jnp.int32), s_rows[0]
    for e in range(1, N_EXPERTS):
        better = masked[e] > v1
        v1 = jnp.where(better, masked[e], v1)
        i1 = jnp.where(better, e, i1)
        g1 = jnp.where(better, s_rows[e], g1)
    v2, i2, g2 = neg, jnp.zeros(best.shape, jnp.int32), jnp.zeros(best.shape, F32)
    for e in range(N_EXPERTS):
        cand = jnp.where(i1 == e, neg, masked[e])
        better = cand > v2
        v2 = jnp.where(better, cand, v2)
        i2 = jnp.where(better, e, i2)
        g2 = jnp.where(better, s_rows[e], g2)
    tot = g1 + g2
    return jnp.concatenate([i1, i2], axis=0), jnp.concatenate([g1 / tot, g2 / tot], axis=0)


def _tail(x, o, mod_ref, r, lng, wr, rb, x1_ref, h2_ref, ei_ref, gt_ref):
    d = x.shape[1]
    g1 = mod_ref[0, pl.ds(r, 1), 2 * d:3 * d]
    sh2 = mod_ref[0, pl.ds(r, 1), 3 * d:4 * d]
    sc2 = mod_ref[0, pl.ds(r, 1), 4 * d:5 * d]
    y = ALPHA * x + g1 * o
    mu = jnp.mean(y, axis=-1, keepdims=True)
    yc = y - mu
    var = jnp.mean(yc * yc, axis=-1, keepdims=True)
    x1 = yc * lax.rsqrt(var + LN_EPS) * lng
    h2 = x1 * (1.0 + sc2) + sh2
    x1_ref[...] = x1
    _store_rows(h2_ref, _pack_bf16_pairs(h2))
    ei, gt = _route(_dot3_nt(wr, h2), rb)
    ei_ref[...] = ei
    gt_ref[...] = gt


def _merge0_kernel(r_ref, g_ref, s_ref, x_ref, xc_ref, mod_ref, wglu_ref, wout_ref, lng_ref, wr_ref,
                   rb_ref, x1_ref, h2_ref, ei_ref, gt_ref, *, tiles_per_batch, ctx_tiles, nb):
    r = _mod_row(pl.program_id(0), tiles_per_batch, ctx_tiles, nb)
    ret = r_ref[...].astype(F32) * _silu(g_ref[...].astype(F32))
    rows = x_ref.shape[0]
    x = _token_tile((x_ref, xc_ref), tiles_per_batch, ctx_tiles)
    sp = jnp.concatenate(
        [jnp.concatenate([s_ref[hf, :, s * LANES:(s + 1) * LANES] for hf in range(S5_HALVES)], axis=1)
         for s in range(S5_CHUNK)], axis=0)
    s5 = _dot(_chunk_perm(rows, True), sp)
    z = _dot(_gelu_tanh(s5).astype(BF16), wglu_ref[...])
    zz = z[:, :S5_CH] * _sigmoid(z[:, S5_CH:])
    o = _dot(ret.astype(BF16), wout_ref[0:RET_V_W, :]) + _dot(zz.astype(BF16), wout_ref[RET_V_W:, :])
    _tail(x, o, mod_ref, r, lng_ref[...], wr_ref[...], rb_ref[...],
          x1_ref, h2_ref, ei_ref, gt_ref)


def _merge1_kernel(a_ref, x_ref, mod_ref, wout_ref, lng_ref, wr_ref, rb_ref,
                   x1_ref, h2_ref, ei_ref, gt_ref, *, tiles_per_batch):
    r = lax.div(pl.program_id(0), tiles_per_batch)
    o = _dot(a_ref[...], wout_ref[...])
    _tail(x_ref[...], o, mod_ref, r, lng_ref[...], wr_ref[...], rb_ref[...],
          x1_ref, h2_ref, ei_ref, gt_ref)


def _tail_outs(n_rows, d):
    shapes = (jax.ShapeDtypeStruct((n_rows, d), F32),
              jax.ShapeDtypeStruct(_row_shape(n_rows), jnp.uint32),
              jax.ShapeDtypeStruct((TOP_K, n_rows), jnp.int32),
              jax.ShapeDtypeStruct((TOP_K, n_rows), F32))
    specs = (pl.BlockSpec((ROW_TILE, d), lambda t: (t, 0)),
             pl.BlockSpec(_row_shape(ROW_TILE), lambda t: (t, 0, 0, 0)),
             pl.BlockSpec((TOP_K, ROW_TILE), lambda t: (0, t)),
             pl.BlockSpec((TOP_K, ROW_TILE), lambda t: (0, t)))
    return shapes, specs


def _merge0(ret, proj, s5y, x_lat, x_ctx, mod, w_glu, w_out, lng, wr_t, rbias, *, nb, t_len, ctx_len):
    nt, d = nb * t_len, x_lat.shape[1]
    tpb = t_len // ROW_TILE
    ctx_tiles = ctx_len // ROW_TILE
    kern = functools.partial(_merge0_kernel, tiles_per_batch=tpb, ctx_tiles=ctx_tiles, nb=nb)
    shapes, specs = _tail_outs(nt, d)
    gcol = (2 * RET_QK_W + RET_V_W) // RET_V_W
    full = lambda a: pl.BlockSpec(a.shape, lambda t: (0,) * a.ndim)
    return pl.pallas_call(
        kern,
        grid=(nt // ROW_TILE,),
        in_specs=[
            pl.BlockSpec((ROW_TILE, RET_V_W), lambda t: (t, 0)),
            pl.BlockSpec((ROW_TILE, RET_V_W), lambda t: (t, gcol)),
            pl.BlockSpec((S5_HALVES, ROW_TILE // S5_CHUNK, S5_CHUNK * LANES), lambda t: (0, t, 0)),
            *_token_specs(d, tpb, ctx_tiles),
            pl.BlockSpec((1, SUBLANES, mod.shape[2]), lambda t: (0, 0, 0)),
            full(w_glu), full(w_out), full(lng), full(wr_t), full(rbias),
        ],
        out_specs=specs,
        out_shape=shapes,
        compiler_params=_cparams(("parallel",)),
        name="merge0",
    )(ret, proj, s5y, x_lat, x_ctx, mod, w_glu, w_out, lng, wr_t, rbias)


def _merge1(att, x, mod, layer, w_out, lng, wr_t, rbias, *, nb, l_len, t_len, ctx_len):
    n_lat, d = att.shape
    tpb = l_len // ROW_TILE
    tpb_t = t_len // ROW_TILE
    ctx_tiles = ctx_len // ROW_TILE
    kern = functools.partial(_merge1_kernel, tiles_per_batch=tpb)
    shapes, specs = _tail_outs(n_lat, d)
    full = lambda a: pl.BlockSpec(a.shape, lambda t: (0,) * a.ndim)
    xrow = lambda t: (lax.div(t, tpb) * tpb_t + ctx_tiles + lax.rem(t, tpb), 0)
    return pl.pallas_call(
        kern,
        grid=(n_lat // ROW_TILE,),
        in_specs=[
            pl.BlockSpec((ROW_TILE, d), lambda t: (t, 0)),
            pl.BlockSpec((ROW_TILE, d), xrow),
            pl.BlockSpec((1, SUBLANES, mod.shape[2]), lambda t: (layer, 0, 0)),
            full(w_out), full(lng), full(wr_t), full(rbias),
        ],
        out_specs=specs,
        out_shape=shapes,
        compiler_params=_cparams(("parallel",)),
        name="merge1",
    )(att, x, mod, w_out, lng, wr_t, rbias)


def _moe_plan(eidx):
    k, n = eidx.shape
    a = k * n
    e_flat = eidx.reshape(a)
    seg = TOP_K * ROW_TILE
    onehot = (e_flat[:, None] == jnp.arange(N_EXPERTS, dtype=jnp.int32)[None, :]).astype(F32)
    onehot = onehot.reshape(a // seg, seg, N_EXPERTS)
    tril = lambda m: jnp.tril(jnp.ones((m, m), F32))
    within = jnp.einsum('ij,tjk->tik', tril(seg), onehot)
    seg_total = within[:, -1, :]
    seg_end = jnp.sum(tril(a // seg)[:, :, None] * seg_total[None], axis=1)
    counts = seg_end[-1].astype(jnp.int32)
    csum = within + (seg_end - seg_total)[:, None, :]
    padded = (counts + MOE_ROWS - 1) // MOE_ROWS * MOE_ROWS
    pad_end = jnp.sum(jnp.tril(jnp.ones((N_EXPERTS, N_EXPERTS), jnp.int32)) * padded[None, :], axis=1)
    pad_start = pad_end - padded
    dest = jnp.sum(onehot * (csum - 1.0 + pad_start.astype(F32)[None, None, :]), axis=-1)
    dest = dest.reshape(a).astype(jnp.int32)
    n_blocks = -(-(a + N_EXPERTS * (MOE_ROWS - 1)) // MOE_ROWS)
    first_row = jnp.arange(n_blocks, dtype=jnp.int32) * MOE_ROWS
    block_expert = jnp.minimum(jnp.sum((pad_end[None, :] <= first_row[:, None]).astype(jnp.int32), axis=1),
                               N_EXPERTS - 1)
    return dest.reshape(k, n), block_expert, n_blocks


def _tile_rows_of(dest, tile):
    k, n = dest.shape
    return dest.reshape(k, n // tile, tile).transpose(1, 0, 2).reshape(n // tile, 1, k * tile)


def _dispatch_kernel(dest_ref, h_ref, xs_in_hbm, xs_hbm, sem):
    del xs_in_hbm
    row_tiles = h_ref.shape[0]
    rows = row_tiles * SUBLANES

    def start(i, c):
        for u in range(SUBLANES):
            for choice in range(TOP_K):
                dst = dest_ref[0, 0, choice * rows + i * SUBLANES + u]
                pltpu.make_async_copy(h_ref.at[i, :, u, :], _row(xs_hbm, dst), sem).start(priority=choice)
        return c

    lax.fori_loop(0, row_tiles, start, 0)
    for _ in range(TOP_K):
        pltpu.make_async_copy(h_ref, xs_hbm.at[pl.ds(0, row_tiles)], sem).wait()


def _dispatch(h, dest, n_rows):
    n = h.shape[0] * SUBLANES
    tile = DISPATCH_TILE if n % DISPATCH_TILE == 0 else ROW_TILE
    return pl.pallas_call(
        _dispatch_kernel,
        grid=(n // tile,),
        in_specs=[
            pl.BlockSpec((1, 1, TOP_K * tile), lambda t: (t, 0, 0), memory_space=pltpu.SMEM),
            pl.BlockSpec(_row_shape(tile), lambda t: (t, 0, 0, 0)),
            pl.BlockSpec(memory_space=pl.ANY),
        ],
        out_specs=pl.BlockSpec(memory_space=pl.ANY),
        out_shape=jax.ShapeDtypeStruct(_row_shape(n_rows), jnp.uint32),
        scratch_shapes=[pltpu.SemaphoreType.DMA],
        input_output_aliases={2: 0},
        compiler_params=_cparams(("arbitrary",)),
        name="moe_dispatch",
    )(_tile_rows_of(dest, tile), h, jnp.zeros(_row_shape(n_rows), jnp.uint32))


def _experts_kernel(be_ref, x_ref, wg_ref, wu_ref, wd_ref, o_ref, wbf):
    i = pl.program_id(0)

    @pl.when(jnp.logical_or(i == 0, be_ref[i] != be_ref[jnp.maximum(i - 1, 0)]))
    def _():
        for k, w_ref in enumerate((wg_ref, wu_ref, wd_ref)):
            wbf[k] = w_ref[0].astype(BF16)

    x = _unpack_bf16_pairs(_load_rows(x_ref)).astype(BF16)
    hg = _dot(x, wbf[0])
    hu = _dot(x, wbf[1])
    _store_rows(o_ref, _pack_bf16_pairs(_dot((_silu(hg) * hu).astype(BF16), wbf[2])))


def _experts(xs, block_expert, layer, wg, wu, wd):
    n_blocks = block_expert.shape[0]
    d, dff = wg.shape[2], wg.shape[3]
    rows_spec = pl.BlockSpec(_row_shape(MOE_ROWS), lambda i, be: (i, 0, 0, 0))
    return pl.pallas_call(
        _experts_kernel,
        grid_spec=pltpu.PrefetchScalarGridSpec(
            num_scalar_prefetch=1,
            grid=(n_blocks,),
            in_specs=[
                rows_spec,
                pl.BlockSpec((None, 1, d, dff), lambda i, be: (layer, be[i], 0, 0)),
                pl.BlockSpec((None, 1, d, dff), lambda i, be: (layer, be[i], 0, 0)),
                pl.BlockSpec((None, 1, dff, d), lambda i, be: (layer, be[i], 0, 0)),
            ],
            out_specs=rows_spec,
            scratch_shapes=[pltpu.VMEM((3, d, dff), BF16)],
        ),
        out_shape=jax.ShapeDtypeStruct(xs.shape, jnp.uint32),
        compiler_params=_cparams(("arbitrary",)),
        name="moe_experts",
    )(block_expert, xs, wg, wu, wd)


def _combine_kernel(dcur_ref, dnxt_ref, dnn_ref, x_ref, gt_ref, mod_ref, lng_ref, y_hbm, o_ref, ybuf,
                    sem, *, tiles_per_batch, ctx_tiles, nb):
    t = pl.program_id(0)
    last = pl.num_programs(0) - 1
    rows = x_ref.shape[0]
    d = x_ref.shape[1]
    slot = lax.rem(t, 3)

    row_tiles = rows // SUBLANES

    def start_tile(idx_ref, s, i):
        for choice in range(TOP_K):
            for u in range(SUBLANES):
                j = choice * rows + i * SUBLANES + u
                pltpu.make_async_copy(_row(y_hbm, idx_ref[0, 0, j]),
                                      ybuf.at[s, choice * row_tiles + i, :, u, :],
                                      sem.at[s]).start(priority=choice)

    def wait_all(s):
        pltpu.make_async_copy(ybuf.at[s], ybuf.at[s], sem.at[s]).wait()

    @pl.when(t == 0)
    def _():
        def start(i, c):
            start_tile(dcur_ref, 0, i)
            start_tile(dnxt_ref, 1, i)
            return c

        lax.fori_loop(0, row_tiles, start, 0)

    wait_all(slot)
    r = _mod_row(t, tiles_per_batch, ctx_tiles, nb)
    gt = gt_ref[...]

    def choice_rows(choice):
        tiles = pl.ds(choice * row_tiles, row_tiles)
        return jnp.concatenate([ybuf[slot, tiles, j].reshape(rows, LANES) for j in range(ROW_SUB)], axis=1)

    y0, y1 = choice_rows(0), choice_rows(1)
    ahead = lax.rem(t + 2, 3)
    for i in range(row_tiles):
        start_tile(dnn_ref, ahead, i)
    y = _unpack_bf16_pairs(y0) * gt[:, 0:1] + _unpack_bf16_pairs(y1) * gt[:, 1:2]
    g2 = mod_ref[0, pl.ds(r, 1), 5 * d:6 * d]
    z = ALPHA * x_ref[...] + g2 * y
    mu = jnp.mean(z, axis=-1, keepdims=True)
    zc = z - mu
    var = jnp.mean(zc * zc, axis=-1, keepdims=True)
    o_ref[...] = zc * lax.rsqrt(var + LN_EPS) * lng_ref[...]

    @pl.when(t == last)
    def _():
        wait_all(lax.rem(t + 1, 3))
        wait_all(ahead)


def _combine(x1, dest, gates, mod, layer, lng, ys, *, tiles_per_batch, ctx_tiles, nb):
    n, d = x1.shape
    nt = n // ROW_TILE
    dest_t = _tile_rows_of(dest, ROW_TILE)
    kern = functools.partial(_combine_kernel, tiles_per_batch=tiles_per_batch, ctx_tiles=ctx_tiles, nb=nb)
    idx_spec = lambda f: pl.BlockSpec((1, 1, TOP_K * ROW_TILE), f, memory_space=pltpu.SMEM)
    return pl.pallas_call(
        kern,
        grid=(nt,),
        in_specs=[
            idx_spec(lambda t: (t, 0, 0)),
            idx_spec(lambda t: (jnp.minimum(t + 1, nt - 1), 0, 0)),
            idx_spec(lambda t: (jnp.minimum(t + 2, nt - 1), 0, 0)),
            pl.BlockSpec((ROW_TILE, d), lambda t: (t, 0)),
            pl.BlockSpec((ROW_TILE, TOP_K), lambda t: (t, 0)),
            pl.BlockSpec((1, SUBLANES, mod.shape[2]), lambda t: (layer, 0, 0)),
            pl.BlockSpec((1, d), lambda t: (0, 0)),
            pl.BlockSpec(memory_space=pl.ANY),
        ],
        out_specs=pl.BlockSpec((ROW_TILE, d), lambda t: (t, 0)),
        out_shape=jax.ShapeDtypeStruct((n, d), F32),
        scratch_shapes=[pltpu.VMEM((3, TOP_K * ROW_TILE // SUBLANES, ROW_SUB, SUBLANES, LANES), jnp.uint32),
                        pltpu.SemaphoreType.DMA((3,))],
        compiler_params=_cparams(("arbitrary",)),
        name="moe_combine%d" % layer,
    )(dest_t, dest_t, dest_t, x1, gates.T, mod, lng, ys)


def _moe_layer(x1, h2, eidx, gates, mod, layer, lng, wg, wu, wd, *, tiles_per_batch, ctx_tiles, nb):
    dest, block_expert, n_blocks = _moe_plan(eidx)
    xs = _dispatch(h2, dest, n_blocks * MOE_ROWS)
    ys = _experts(xs, block_expert, layer, wg, wu, wd)
    return _combine(x1, dest, gates, mod, layer, lng, ys,
                    tiles_per_batch=tiles_per_batch, ctx_tiles=ctx_tiles, nb=nb)


def _half_norms(x):
    lane = lax.broadcasted_iota(jnp.int32, x.shape, 1)
    sq = x * x
    lo = jnp.sum(jnp.where(lane < DIFF_DH, sq, 0.0), axis=-1, keepdims=True)
    hi = jnp.sum(jnp.where(lane >= DIFF_DH, sq, 0.0), axis=-1, keepdims=True)
    return jnp.sqrt(lo), jnp.sqrt(hi)


def _attn_kernel(lam_ref, q_ref, k_ref, v_ref, g_ref, o_ref, vext, s_buf0, s_buf1,
                 p_buf0, p_buf1, corr_buf0, corr_buf1, m_buf, shift_buf, acc, *, out_scale, ctx_len):
    t_len = k_ref.shape[0]
    nk = t_len // ATT_TK
    dv = v_ref.shape[1]
    tq = acc.shape[1]
    nq = o_ref.shape[0] // tq
    n_tiles = nq * nk

    vext[:, 0:dv] = v_ref[...]
    vext[:, dv:2 * dv] = jnp.ones((t_len, dv), BF16)
    lam = lam_ref[0]
    gain = g_ref[...] * out_scale

    s_bufs, p_bufs, corr_bufs = (s_buf0, s_buf1), (p_buf0, p_buf1), (corr_buf0, corr_buf1)

    def key_rows(kj):
        return pl.ds(pl.multiple_of(kj * ATT_TK, ATT_TK), ATT_TK)

    def query_rows(qi):
        return pl.ds(pl.multiple_of(ctx_len + qi * tq, ROW_TILE), tq)

    def column_max(norms, carry):
        return tuple(jnp.maximum(c, jnp.max(n, axis=0, keepdims=True)) for n, c in zip(norms, carry))

    def key_norms(kj, carry):
        return column_max(_half_norms(k_ref[key_rows(kj), :].astype(F32)), carry)

    zero11 = jnp.zeros((1, 1), F32)
    kmax = lax.fori_loop(0, nk, key_norms, (zero11, zero11))

    def query_shifts(qi, carry):
        shifts = tuple(n * k for n, k in zip(_half_norms(q_ref[query_rows(qi), :].astype(F32)), kmax))
        shift_buf[qi, :, 0:1] = shifts[0]
        shift_buf[qi, :, 1:2] = shifts[1]
        return column_max(shifts, carry)

    worst = lax.fori_loop(0, nq, query_shifts, (zero11, zero11))
    bounded = jnp.max(jnp.maximum(worst[0], worst[1])) <= ATT_SAFE_SHIFT

    def advance(tile):
        qi, kj = tile
        wrap = kj + 1 == nk
        return jnp.where(wrap, qi + 1, qi), jnp.where(wrap, 0, kj + 1)

    def scores(tile, slot):
        qi, kj = tile
        q = q_ref[query_rows(qi), :]
        lane = lax.broadcasted_iota(jnp.int32, q.shape, 1)
        zero = jnp.zeros(q.shape, q.dtype)
        k = k_ref[key_rows(kj), :]
        s_bufs[slot][0] = _dot_nt(jnp.where(lane < DIFF_DH, q, zero), k)
        s_bufs[slot][1] = _dot_nt(jnp.where(lane >= DIFF_DH, q, zero), k)

    def numerators(tile, slot, online):
        qi, kj = tile
        for w in range(2):
            s = s_bufs[slot][w]
            if online:
                m_old = jnp.where(kj == 0, -jnp.inf, m_buf[w])
                m_new = jnp.maximum(m_old, jnp.max(s, axis=-1, keepdims=True))
                corr_bufs[slot][w] = jnp.exp2(m_old - m_new)
                m_buf[w] = m_new
            else:
                m_new = shift_buf[qi, :, w:w + 1]
            p_bufs[slot][w] = jnp.exp2(s - m_new).astype(BF16)

    def values(tile, slot, online):
        qi, kj = tile
        ve = vext[key_rows(kj), :]
        a = []
        for w in range(2):
            keep = corr_bufs[slot][w] if online else jnp.where(kj == 0, 0.0, 1.0)
            a.append(keep * acc[w] + _dot(p_bufs[slot][w], ve))
            acc[w] = a[w]
        o = a[0][:, 0:dv] / a[0][:, dv:2 * dv] - lam * (a[1][:, 0:dv] / a[1][:, dv:2 * dv])
        o = o * lax.rsqrt(jnp.mean(o * o, axis=-1, keepdims=True) + GN_EPS)
        o_ref[pl.ds(pl.multiple_of(qi * tq, tq), tq), :] = (o * gain).astype(BF16)

    def pipeline(online):
        def step(tiles, slot):
            a, b, c = tiles
            values(c, slot, online)
            scores(a, slot)
            numerators(b, 1 - slot, online)
            return advance(a), a, b

        acc[...] = jnp.zeros(acc.shape, F32)
        t0 = (jnp.int32(0), jnp.int32(0))
        t1 = advance(t0)
        scores(t0, 0)
        scores(t1, 1)
        numerators(t0, 0, online)

        def pair(_, tiles):
            return step(step(tiles, 0), 1)

        _, last, prev = lax.fori_loop(0, (n_tiles - 2) // 2, pair, (advance(t1), t1, t0))
        numerators(last, (n_tiles - 1) % 2, online)
        values(prev, n_tiles % 2, online)
        values(last, (n_tiles - 1) % 2, online)

    pl.when(bounded)(functools.partial(pipeline, False))
    pl.when(jnp.logical_not(bounded))(functools.partial(pipeline, True))


def _diff_attention(qkv, lam, subln_g, lambda_init, *, nb, l_len, t_len, ctx_len):
    d = D_MODEL
    tq = ATT_TQ
    dv = 2 * DIFF_DH
    nq = l_len // tq
    assert (nq * (t_len // ATT_TK)) % 2 == 0
    kern = functools.partial(_attn_kernel, out_scale=1.0 - lambda_init, ctx_len=ctx_len)
    return pl.pallas_call(
        kern,
        grid_spec=pltpu.PrefetchScalarGridSpec(
            num_scalar_prefetch=1,
            grid=(nb, DIFF_HEADS),
            in_specs=[
                pl.BlockSpec((t_len, LANES), lambda b, h, lam: (b, h)),
                pl.BlockSpec((t_len, LANES), lambda b, h, lam: (b, DIFF_HEADS + h)),
                pl.BlockSpec((t_len, LANES), lambda b, h, lam: (b, 2 * DIFF_HEADS + h)),
                pl.BlockSpec((1, LANES), lambda b, h, lam: (0, 0)),
            ],
            out_specs=pl.BlockSpec((l_len, LANES), lambda b, h, lam: (b, h)),
            scratch_shapes=[
                pltpu.VMEM((t_len, 2 * dv), BF16),
                pltpu.VMEM((2, tq, ATT_TK), F32), pltpu.VMEM((2, tq, ATT_TK), F32),
                pltpu.VMEM((2, tq, ATT_TK), BF16), pltpu.VMEM((2, tq, ATT_TK), BF16),
                pltpu.VMEM((2, tq, 1), F32), pltpu.VMEM((2, tq, 1), F32),
                pltpu.VMEM((2, tq, 1), F32), pltpu.VMEM((nq, tq, 2), F32),
                pltpu.VMEM((2, tq, 2 * dv), F32),
            ],
        ),
        out_shape=jax.ShapeDtypeStruct((nb * l_len, d), BF16),
        compiler_params=_cparams(("parallel", "parallel"), ATT_VMEM_LIMIT),
        name="diff_attention",
    )(lam, qkv, qkv, qkv, subln_g.reshape(1, LANES).astype(F32))


def _rope_angles(pos, dim):
    inv = (np.float32(ROPE_BASE) ** (-np.arange(0, dim, 2, dtype=np.float32) / np.float32(dim)))
    return pos.astype(np.float32)[:, None] * inv.astype(np.float32)[None, :]


def _ret_rope_tables(l_len, ctx_len):
    half = RET_DK // 2
    ang = _rope_angles(np.arange(l_len), RET_DK)
    ang = np.concatenate([np.zeros((ctx_len, half), np.float32), ang], axis=0)
    cos64 = np.concatenate([np.cos(ang), np.cos(ang)], axis=1)
    sin64 = np.concatenate([-np.sin(ang), np.sin(ang)], axis=1)
    kscale = np.float32(RET_DK ** -0.5)
    cos = np.concatenate([cos64, cos64 * kscale], axis=1)
    sin = np.concatenate([sin64, sin64 * kscale], axis=1)
    return jnp.asarray(cos[None], F32), jnp.asarray(sin[None], F32)


def _attn_rope_tables(l_len, ctx_len):
    quarter = DIFF_DH // 4
    pos = np.arange(l_len)
    pad = lambda a: np.concatenate([np.zeros((ctx_len, quarter), np.float32), a], axis=0)
    ang_r = pad(_rope_angles(pos // GRID_W, DIFF_DH // 2))
    ang_c = pad(_rope_angles(pos % GRID_W, DIFF_DH // 2))
    cos64 = np.concatenate([np.cos(ang_r)] * 2 + [np.cos(ang_c)] * 2, axis=1)
    sin64 = np.concatenate([-np.sin(ang_r), np.sin(ang_r), -np.sin(ang_c), np.sin(ang_c)], axis=1)
    cos = np.concatenate([cos64, cos64], axis=1)
    sin = np.concatenate([sin64, sin64], axis=1)
    qscale = np.float32(DIFF_DH ** -0.5 * math.log2(math.e))
    return (jnp.asarray(np.stack([cos * qscale, cos]), F32), jnp.asarray(np.stack([sin * qscale, sin]), F32))


def kernel(x, c, ctx, c_ctx, ada_w, ada_b, ln_g, w_in_ab, ret_decay_logit, s5_lam_re, s5_lam_im,
           s5_log_dt, s5_b_re, s5_b_im, s5_c_re, s5_c_im, s5_d, s5_w_glu, w_out_ab, w_in_c,
           diff_lambda, diff_subln_g, w_out_c, router_w, router_bias, exp_w_gate, exp_w_up,
           exp_w_down):
    nb, l_len, d = x.shape
    ctx_len = ctx.shape[1]
    t_len = ctx_len + l_len
    tpb = t_len // ROW_TILE
    ctx_tiles = ctx_len // ROW_TILE
    assert d == D_MODEL and nb < SUBLANES
    assert l_len % ATT_TQ == 0 and ctx_len % ROW_TILE == 0 and t_len % ATT_TK == 0

    x_lat, x_ctx = x.reshape(nb * l_len, d), ctx.reshape(nb * ctx_len, d)
    c_all = jnp.concatenate([c, c_ctx[None].astype(c.dtype)], axis=0)
    c_pad = jnp.zeros((SUBLANES, d), F32).at[:nb + 1].set(c_all)
    mod = _adaln(c_pad, ada_w, ada_b)

    wr_t = router_w.T
    rbias = router_bias.reshape(N_EXPERTS, 1).astype(F32)

    w0 = w_in_ab[0]
    q_w, k_w, v_w, g_w, u_w = jnp.split(w0, (RET_QK_W, 2 * RET_QK_W, 2 * RET_QK_W + RET_V_W,
                                             2 * RET_QK_W + 2 * RET_V_W), axis=1)
    qk_w = jnp.concatenate([q_w.reshape(d, RET_HEADS, RET_DK), k_w.reshape(d, RET_HEADS, RET_DK)],
                           axis=2).reshape(d, 2 * RET_QK_W)
    w0p = jnp.concatenate([qk_w, v_w, g_w, u_w], axis=1).astype(BF16)
    cos0, sin0 = _ret_rope_tables(l_len, ctx_len)
    rope_tab0 = [0] * RET_HEADS + [None] * ((w0p.shape[1] - S5_CH - 2 * RET_QK_W) // LANES)
    proj0, u5 = _inproj((x_lat, x_ctx), mod, 0, w0p, cos0, sin0, rope_tab0, RET_DK // 2,
                        nb=nb, t_len=t_len, ctx_len=ctx_len, s5_cols=S5_CH)

    log_gammas = jax.nn.log_sigmoid(ret_decay_logit[0].astype(F32))
    ret = _retention(proj0, log_gammas, nb=nb, t_len=t_len, ctx_len=ctx_len)

    s5_ops = _s5_operators(s5_lam_re[0], s5_lam_im[0], s5_log_dt[0], s5_b_re[0], s5_b_im[0],
                           s5_c_re[0], s5_c_im[0], s5_d[0])
    s5y = _s5(u5, *s5_ops, nb=nb, nctx=ctx_len // S5_CHUNK)

    x1, h2, eidx, gates = _merge0(ret, proj0, s5y, x_lat, x_ctx, mod, s5_w_glu[0].astype(BF16),
                                  w_out_ab[0].astype(BF16), ln_g[0, 0].reshape(1, d), wr_t, rbias,
                                  nb=nb, t_len=t_len, ctx_len=ctx_len)
    experts = (exp_w_gate, exp_w_up, exp_w_down)
    x2 = _moe_layer(x1, h2, eidx, gates, mod, 0, ln_g[0, 1].reshape(1, d), *experts,
                    tiles_per_batch=tpb, ctx_tiles=ctx_tiles, nb=nb)

    cos1, sin1 = _attn_rope_tables(l_len, ctx_len)
    n_heads_cols = D_MODEL // LANES
    rope_tab1 = [0] * n_heads_cols + [1] * n_heads_cols + [None] * n_heads_cols
    qkv, = _inproj((x2,), mod, 1, w_in_c[0].astype(BF16), cos1, sin1, rope_tab1, DIFF_DH // 4,
                   nb=nb, t_len=t_len, ctx_len=ctx_len)
    lf = diff_lambda[0].astype(F32)
    lambda_init = 0.8 - 0.6 * math.exp(-0.3 * 1)
    lam = (jnp.exp(jnp.sum(lf[0] * lf[1])) - jnp.exp(jnp.sum(lf[2] * lf[3])) + lambda_init).reshape(1)
    att = _diff_attention(qkv, lam, diff_subln_g[0], lambda_init,
                          nb=nb, l_len=l_len, t_len=t_len, ctx_len=ctx_len)
    x3, h3, eidx1, gates1 = _merge1(att, x2, mod, 1, w_out_c[0].astype(BF16), ln_g[1, 0].reshape(1, d),
                                    wr_t, rbias, nb=nb, l_len=l_len, t_len=t_len, ctx_len=ctx_len)
    out = _moe_layer(x3, h3, eidx1, gates1, mod, 1, ln_g[1, 1].reshape(1, d), *experts,
                     tiles_per_batch=l_len // ROW_TILE, ctx_tiles=0, nb=nb)
    return out.reshape(nb, l_len, d)
```
